```python
import math
import jax, jax.numpy as jnp
from jax import lax
import numpy as np

D_MODEL = 1024
BATCH = 8
SEQ = 8192
DEPTH = 1

HEAD_DIM = D_MODEL // 16
N_ATTN_HEADS = 12
N_GMLP_GROUPS = 4
GMLP_DIM = HEAD_DIM
ATTN_WIDTH = N_ATTN_HEADS * HEAD_DIM
GMLP_WIDTH = N_GMLP_GROUPS * GMLP_DIM
MIX_WIDTH = ATTN_WIDTH + GMLP_WIDTH
IN_WIDTH = 3 * ATTN_WIDTH + 2 * GMLP_WIDTH
CHUNK = 128
BLOCK = 128
DILATED_PATTERNS = ((128, 1), (512, 4), (2048, 16))
D_FF = 4 * D_MODEL
EPS = 1e-6

kernel_name = "hymba_gmlp_longnet_alibi_block"


def alibi_slopes(n):
    def pow2_slopes(m):
        start = 2.0 ** (-8.0 / m)
        return [start ** (i + 1) for i in range(m)]
    if math.log2(n).is_integer():
        s = pow2_slopes(n)
    else:
        c = 2 ** int(math.floor(math.log2(n)))
        s = pow2_slopes(c) + pow2_slopes(2 * c)[0::2][: n - c]
    return np.asarray(s, dtype=np.float32)


def rms_norm(x, g):
    xf = x.astype(jnp.float32)
    y = xf * lax.rsqrt(jnp.mean(xf * xf, axis=-1, keepdims=True) + EPS)
    return (y * g.astype(jnp.float32)).astype(x.dtype)


def layer_norm(x, g, b):
    xf = x.astype(jnp.float32)
    mu = jnp.mean(xf, axis=-1, keepdims=True)
    var = jnp.mean(jnp.square(xf - mu), axis=-1, keepdims=True)
    y = (xf - mu) * lax.rsqrt(var + EPS)
    return (y * g.astype(jnp.float32) + b.astype(jnp.float32)).astype(x.dtype)


def chunked_spatial_gating(u, z, ln_g, ln_b, w_s, b_s):
    B, S, G, C = u.shape
    u = jax.nn.gelu(u)
    z = layer_norm(jax.nn.gelu(z), ln_g, ln_b)
    zc = z.reshape(B, S // CHUNK, CHUNK, G, C)
    causal = jnp.tril(jnp.ones((CHUNK, CHUNK), dtype=w_s.dtype))
    ws = w_s * causal[None]
    mixed = jnp.einsum('gts,bnsgc->bntgc', ws, zc) + b_s.T[None, None, :, :, None]
    return u * mixed.reshape(B, S, G, C)


def dilated_window_attention(q, k, v, slopes, window, dilation):
    B, S, H, Dh = q.shape
    span = BLOCK * dilation
    S_pad = -(-S // span) * span
    pad = S_pad - S
    L = S_pad // dilation
    nb = L // BLOCK

    def to_sub(t):
        t = jnp.pad(t.astype(jnp.float32), ((0, 0), (0, pad), (0, 0), (0, 0)))
        t = t.reshape(B, L, dilation, H, Dh).transpose(0, 2, 3, 1, 4)
        return t.reshape(B, dilation, H, nb, BLOCK, Dh)

    qs, ks, vs = to_sub(q), to_sub(k), to_sub(v)
    blk_pad = ((0, 0), (0, 0), (0, 0), (1, 0), (0, 0), (0, 0))
    kb = jnp.concatenate([jnp.pad(ks, blk_pad)[:, :, :, :-1], ks], axis=4)
    vb = jnp.concatenate([jnp.pad(vs, blk_pad)[:, :, :, :-1], vs], axis=4)

    scores = jnp.einsum('brhnqd,brhnkd->brhnqk', qs, kb)
    qi = jnp.arange(BLOCK)[:, None]
    kj = jnp.arange(2 * BLOCK)[None, :]
    steps = qi + BLOCK - kj
    band = (steps >= 0) & (steps <= window // dilation)
    blk = jnp.arange(nb)[:, None, None]
    valid = band[None] & ~((blk == 0) & (kj[None] < BLOCK))
    alibi = -slopes[:, None, None] * (steps * dilation).astype(jnp.float32)[None]
    scores = scores + alibi[None, None, :, None]
    scores = jnp.where(valid[None, None, None], scores, -jnp.inf)

    m = jnp.max(scores, axis=-1, keepdims=True)
    p = jnp.exp(scores - m)
    l = jnp.sum(p, axis=-1, keepdims=True)
    o = jnp.einsum('brhnqk,brhnkd->brhnqd', p, vb) / l
    lse = (m + jnp.log(l))[..., 0]

    o = o.reshape(B, dilation, H, L, Dh).transpose(0, 3, 1, 2, 4).reshape(B, S_pad, H, Dh)[:, :S]
    lse = lse.reshape(B, dilation, H, L).transpose(0, 3, 1, 2).reshape(B, S_pad, H)[:, :S]
    return o, lse


def mixture_of_dilations(q, k, v, slopes):
    outs, lses = [], []
    for window, dilation in DILATED_PATTERNS:
        o, lse = dilated_window_attention(q, k, v, slopes, window, dilation)
        outs.append(o)
        lses.append(lse)
    w = jax.nn.softmax(jnp.stack(lses, axis=0), axis=0)
    return jnp.sum(w[..., None] * jnp.stack(outs, axis=0), axis=0)


def _fwd_setup_inputs(seed: int = 0) -> dict:
    key = jax.random.key(seed)
    ks = jax.random.split(key, 16)
    f32 = jnp.float32
    nrm = lambda k, shape, scale: jax.random.normal(k, shape, f32) * scale
    G, C = N_GMLP_GROUPS, GMLP_DIM
    return {
        "x": nrm(ks[0], (BATCH, SEQ, D_MODEL), 1.0),
        "norm1_g": 1.0 + nrm(ks[1], (DEPTH, D_MODEL), 0.02),
        "w_in": nrm(ks[2], (DEPTH, D_MODEL, IN_WIDTH), D_MODEL ** -0.5),
        "sgu_ln_g": 1.0 + nrm(ks[3], (DEPTH, G, C), 0.02),
        "sgu_ln_b": nrm(ks[4], (DEPTH, G, C), 0.02),
        "sgu_w": nrm(ks[5], (DEPTH, G, CHUNK, CHUNK), CHUNK ** -0.5),
        "sgu_b": 1.0 + nrm(ks[6], (DEPTH, G, CHUNK), 0.02),
        "attn_out_g": 1.0 + nrm(ks[7], (DEPTH, ATTN_WIDTH), 0.02),
        "gmlp_out_g": 1.0 + nrm(ks[8], (DEPTH, GMLP_WIDTH), 0.02),
        "w_out": nrm(ks[9], (DEPTH, MIX_WIDTH, D_MODEL), MIX_WIDTH ** -0.5),
        "norm2_g": 1.0 + nrm(ks[10], (DEPTH, D_MODEL), 0.02),
        "w_ff1": nrm(ks[11], (DEPTH, D_MODEL, D_FF), D_MODEL ** -0.5),
        "w_ff2": nrm(ks[12], (DEPTH, D_FF, D_MODEL), D_FF ** -0.5),
        "final_norm_g": 1.0 + nrm(ks[13], (D_MODEL,), 0.02),
    }


def _fwd_reference(x, norm1_g, w_in, sgu_ln_g, sgu_ln_b, sgu_w, sgu_b, attn_out_g, gmlp_out_g,
              w_out, norm2_g, w_ff1, w_ff2, final_norm_g):
    B, S, _ = x.shape
    slopes = jnp.asarray(alibi_slopes(N_ATTN_HEADS), dtype=jnp.float32)
    scale = HEAD_DIM ** -0.5
    A, Gw = ATTN_WIDTH, GMLP_WIDTH
    h = x
    for l in range(DEPTH):
        hn = rms_norm(h, norm1_g[l])
        proj = hn @ w_in[l]
        q = (proj[..., :A] * scale).reshape(B, S, N_ATTN_HEADS, HEAD_DIM)
        k = proj[..., A:2 * A].reshape(B, S, N_ATTN_HEADS, HEAD_DIM)
        v = proj[..., 2 * A:3 * A].reshape(B, S, N_ATTN_HEADS, HEAD_DIM)
        u = proj[..., 3 * A:3 * A + Gw].reshape(B, S, N_GMLP_GROUPS, GMLP_DIM)
        z = proj[..., 3 * A + Gw:].reshape(B, S, N_GMLP_GROUPS, GMLP_DIM)

        attn = mixture_of_dilations(q, k, v, slopes).astype(h.dtype).reshape(B, S, A)
        gmlp = chunked_spatial_gating(u, z, sgu_ln_g[l], sgu_ln_b[l], sgu_w[l], sgu_b[l]).reshape(B, S, Gw)

        mixed = jnp.concatenate([rms_norm(attn, attn_out_g[l]), rms_norm(gmlp, gmlp_out_g[l])], axis=-1)
        h = h + mixed @ w_out[l]

        hn = rms_norm(h, norm2_g[l])
        h = h + jnp.square(jax.nn.relu(hn @ w_ff1[l])) @ w_ff2[l]
    return rms_norm(h, final_norm_g)


import jax as _jax
import jax.numpy as _jnp

TWIN_FORMAT = 'train_step'
FWD_PARAMS = ['x', 'norm1_g', 'w_in', 'sgu_ln_g', 'sgu_ln_b', 'sgu_w', 'sgu_b', 'attn_out_g', 'gmlp_out_g', 'w_out', 'norm2_g', 'w_ff1', 'w_ff2', 'final_norm_g']
TWIN_WEIGHTS = ['norm1_g', 'w_in', 'sgu_ln_g', 'sgu_ln_b', 'sgu_w', 'sgu_b', 'attn_out_g', 'gmlp_out_g', 'w_out', 'norm2_g', 'w_ff1', 'w_ff2', 'final_norm_g']
TWIN_DIFF_INPUT = 'x'
TWIN_INPUTS = ['x', 'norm1_g', 'w_in', 'sgu_ln_g', 'sgu_ln_b', 'sgu_w', 'sgu_b', 'attn_out_g', 'gmlp_out_g', 'w_out', 'norm2_g', 'w_ff1', 'w_ff2', 'final_norm_g', 'loss_target', 'm_norm1_g', 'm_w_in', 'm_sgu_ln_g', 'm_sgu_ln_b', 'm_sgu_w', 'm_sgu_b', 'm_attn_out_g', 'm_gmlp_out_g', 'm_w_out', 'm_norm2_g', 'm_w_ff1', 'm_w_ff2', 'm_final_norm_g', 'v_norm1_g', 'v_w_in', 'v_sgu_ln_g', 'v_sgu_ln_b', 'v_sgu_w', 'v_sgu_b', 'v_attn_out_g', 'v_gmlp_out_g', 'v_w_out', 'v_norm2_g', 'v_w_ff1', 'v_w_ff2', 'v_final_norm_g']
TWIN_OUTPUTS = ['loss', 'grad_x', 'grad_norm1_g', 'grad_w_in', 'grad_sgu_ln_g', 'grad_sgu_ln_b', 'grad_sgu_w', 'grad_sgu_b', 'grad_attn_out_g', 'grad_gmlp_out_g', 'grad_w_out', 'grad_norm2_g', 'grad_w_ff1', 'grad_w_ff2', 'grad_final_norm_g', 'delta_norm1_g', 'delta_w_in', 'delta_sgu_ln_g', 'delta_sgu_ln_b', 'delta_sgu_w', 'delta_sgu_b', 'delta_attn_out_g', 'delta_gmlp_out_g', 'delta_w_out', 'delta_norm2_g', 'delta_w_ff1', 'delta_w_ff2', 'delta_final_norm_g', 'new_m_norm1_g', 'new_m_w_in', 'new_m_sgu_ln_g', 'new_m_sgu_ln_b', 'new_m_sgu_w', 'new_m_sgu_b', 'new_m_attn_out_g', 'new_m_gmlp_out_g', 'new_m_w_out', 'new_m_norm2_g', 'new_m_w_ff1', 'new_m_w_ff2', 'new_m_final_norm_g', 'new_v_norm1_g', 'new_v_w_in', 'new_v_sgu_ln_g', 'new_v_sgu_ln_b', 'new_v_sgu_w', 'new_v_sgu_b', 'new_v_attn_out_g', 'new_v_gmlp_out_g', 'new_v_w_out', 'new_v_norm2_g', 'new_v_w_ff1', 'new_v_w_ff2', 'new_v_final_norm_g']
TWIN_LEAF_KINDS = {'loss': 'loss', 'grad_x': 'grad_x', 'grad_norm1_g': 'grad_w', 'grad_w_in': 'grad_w', 'grad_sgu_ln_g': 'grad_w', 'grad_sgu_ln_b': 'grad_w', 'grad_sgu_w': 'grad_w', 'grad_sgu_b': 'grad_w', 'grad_attn_out_g': 'grad_w', 'grad_gmlp_out_g': 'grad_w', 'grad_w_out': 'grad_w', 'grad_norm2_g': 'grad_w', 'grad_w_ff1': 'grad_w', 'grad_w_ff2': 'grad_w', 'grad_final_norm_g': 'grad_w', 'delta_norm1_g': 'delta_w', 'delta_w_in': 'delta_w', 'delta_sgu_ln_g': 'delta_w', 'delta_sgu_ln_b': 'delta_w', 'delta_sgu_w': 'delta_w', 'delta_sgu_b': 'delta_w', 'delta_attn_out_g': 'delta_w', 'delta_gmlp_out_g': 'delta_w', 'delta_w_out': 'delta_w', 'delta_norm2_g': 'delta_w', 'delta_w_ff1': 'delta_w', 'delta_w_ff2': 'delta_w', 'delta_final_norm_g': 'delta_w', 'new_m_norm1_g': 'new_m', 'new_m_w_in': 'new_m', 'new_m_sgu_ln_g': 'new_m', 'new_m_sgu_ln_b': 'new_m', 'new_m_sgu_w': 'new_m', 'new_m_sgu_b': 'new_m', 'new_m_attn_out_g': 'new_m', 'new_m_gmlp_out_g': 'new_m', 'new_m_w_out': 'new_m', 'new_m_norm2_g': 'new_m', 'new_m_w_ff1': 'new_m', 'new_m_w_ff2': 'new_m', 'new_m_final_norm_g': 'new_m', 'new_v_norm1_g': 'new_v', 'new_v_w_in': 'new_v', 'new_v_sgu_ln_g': 'new_v', 'new_v_sgu_ln_b': 'new_v', 'new_v_sgu_w': 'new_v', 'new_v_sgu_b': 'new_v', 'new_v_attn_out_g': 'new_v', 'new_v_gmlp_out_g': 'new_v', 'new_v_w_out': 'new_v', 'new_v_norm2_g': 'new_v', 'new_v_w_ff1': 'new_v', 'new_v_w_ff2': 'new_v', 'new_v_final_norm_g': 'new_v'}


def _forward(args):
    return _fwd_reference(*[args[k] for k in FWD_PARAMS])


def _output_shape():
    def fwd():
        inp = _fwd_setup_inputs(0)
        return _fwd_reference(*[inp[k] for k in FWD_PARAMS])
    out = _jax.eval_shape(fwd)
    return out.shape, out.dtype

N_MICROBATCH = 1
ADAM_LR = 0.001
ADAM_B1 = 0.9
ADAM_B2 = 0.999
ADAM_EPS = 1e-08
ADAM_WD = 0.01
ADAM_STEP = 10
PER_EXAMPLE_BATCH_AXIS = {'x': 0, 'loss_target': 0}
SHARED_INPUTS = []
_WEIGHT_DTYPES = {'norm1_g': _jnp.float32, 'w_in': _jnp.float32, 'sgu_ln_g': _jnp.float32, 'sgu_ln_b': _jnp.float32, 'sgu_w': _jnp.float32, 'sgu_b': _jnp.float32, 'attn_out_g': _jnp.float32, 'gmlp_out_g': _jnp.float32, 'w_out': _jnp.float32, 'norm2_g': _jnp.float32, 'w_ff1': _jnp.float32, 'w_ff2': _jnp.float32, 'final_norm_g': _jnp.float32}
MOMENT_SCALE = {'norm1_g': 2.389549e-01, 'w_in': 1.499615e-01, 'sgu_ln_g': 1.094274e-01, 'sgu_ln_b': 1.150662e-01, 'sgu_w': 7.398225e-02, 'sgu_b': 1.007879e-01, 'attn_out_g': 1.992511e-01, 'gmlp_out_g': 1.848619e-01, 'w_out': 1.967079e-01, 'norm2_g': 1.940266e-01, 'w_ff1': 9.622692e-02, 'w_ff2': 2.068376e-01, 'final_norm_g': 6.449329e+01}


def _to_microbatches(a, axis):
    t = _jnp.moveaxis(a, axis, 0)
    t = t.reshape((N_MICROBATCH, t.shape[0] // N_MICROBATCH) + t.shape[1:])
    return _jnp.moveaxis(t, 1, axis + 1)


def setup_inputs(seed: int = 0) -> dict:
    inp = _fwd_setup_inputs(seed)
    key = _jax.random.fold_in(_jax.random.key(seed), 7919)
    shape, _ = _output_shape()
    out = dict(inp)
    out["loss_target"] = _jax.random.normal(_jax.random.fold_in(key, 0), shape, _jnp.float32)
    for i, name in enumerate(TWIN_WEIGHTS):
        w = inp[name].astype(_jnp.float32)
        if MOMENT_SCALE is None:
            s = _jnp.sqrt(_jnp.mean(_jnp.square(w)) + 1e-30)
        else:
            s = MOMENT_SCALE[name]
        km, kv = _jax.random.split(_jax.random.fold_in(key, i + 1))
        out[name] = w
        out["m_" + name] = s * _jax.random.normal(km, w.shape, _jnp.float32)
        out["v_" + name] = (s * s) * _jax.random.uniform(kv, w.shape, _jnp.float32, 0.5, 1.5)
    if N_MICROBATCH > 1:
        for name, axis in PER_EXAMPLE_BATCH_AXIS.items():
            out[name] = _to_microbatches(out[name], axis)
    return {'x': out['x'], 'norm1_g': out['norm1_g'], 'w_in': out['w_in'], 'sgu_ln_g': out['sgu_ln_g'], 'sgu_ln_b': out['sgu_ln_b'], 'sgu_w': out['sgu_w'], 'sgu_b': out['sgu_b'], 'attn_out_g': out['attn_out_g'], 'gmlp_out_g': out['gmlp_out_g'], 'w_out': out['w_out'], 'norm2_g': out['norm2_g'], 'w_ff1': out['w_ff1'], 'w_ff2': out['w_ff2'], 'final_norm_g': out['final_norm_g'], 'loss_target': out['loss_target'], 'm_norm1_g': out['m_norm1_g'], 'm_w_in': out['m_w_in'], 'm_sgu_ln_g': out['m_sgu_ln_g'], 'm_sgu_ln_b': out['m_sgu_ln_b'], 'm_sgu_w': out['m_sgu_w'], 'm_sgu_b': out['m_sgu_b'], 'm_attn_out_g': out['m_attn_out_g'], 'm_gmlp_out_g': out['m_gmlp_out_g'], 'm_w_out': out['m_w_out'], 'm_norm2_g': out['m_norm2_g'], 'm_w_ff1': out['m_w_ff1'], 'm_w_ff2': out['m_w_ff2'], 'm_final_norm_g': out['m_final_norm_g'], 'v_norm1_g': out['v_norm1_g'], 'v_w_in': out['v_w_in'], 'v_sgu_ln_g': out['v_sgu_ln_g'], 'v_sgu_ln_b': out['v_sgu_ln_b'], 'v_sgu_w': out['v_sgu_w'], 'v_sgu_b': out['v_sgu_b'], 'v_attn_out_g': out['v_attn_out_g'], 'v_gmlp_out_g': out['v_gmlp_out_g'], 'v_w_out': out['v_w_out'], 'v_norm2_g': out['v_norm2_g'], 'v_w_ff1': out['v_w_ff1'], 'v_w_ff2': out['v_w_ff2'], 'v_final_norm_g': out['v_final_norm_g']}


def _loss(weights, diff, rest, loss_target):
    with _jax.named_scope("forward"):
        args = {**rest, TWIN_DIFF_INPUT: diff, **{k: w.astype(_WEIGHT_DTYPES[k]) for k, w in weights.items()}}
        y = _forward(args)
    with _jax.named_scope("loss_head"):
        err = _jnp.square(y.astype(_jnp.float32) - loss_target)
        return 0.5 * _jnp.sum(_jnp.mean(err, axis=-1)) if err.ndim else 0.5 * err


def _adamw(w, g, m, v):
    m = ADAM_B1 * m + (1.0 - ADAM_B1) * g
    v = ADAM_B2 * v + (1.0 - ADAM_B2) * _jnp.square(g)
    m_hat = m / (1.0 - ADAM_B1 ** ADAM_STEP)
    v_hat = v / (1.0 - ADAM_B2 ** ADAM_STEP)
    delta = -ADAM_LR * (m_hat / (_jnp.sqrt(v_hat) + ADAM_EPS) + ADAM_WD * w)
    return delta, m, v


def reference(x, norm1_g, w_in, sgu_ln_g, sgu_ln_b, sgu_w, sgu_b, attn_out_g, gmlp_out_g, w_out, norm2_g, w_ff1, w_ff2, final_norm_g, loss_target, m_norm1_g, m_w_in, m_sgu_ln_g, m_sgu_ln_b, m_sgu_w, m_sgu_b, m_attn_out_g, m_gmlp_out_g, m_w_out, m_norm2_g, m_w_ff1, m_w_ff2, m_final_norm_g, v_norm1_g, v_w_in, v_sgu_ln_g, v_sgu_ln_b, v_sgu_w, v_sgu_b, v_attn_out_g, v_gmlp_out_g, v_w_out, v_norm2_g, v_w_ff1, v_w_ff2, v_final_norm_g):
    given = dict(x=x, norm1_g=norm1_g, w_in=w_in, sgu_ln_g=sgu_ln_g, sgu_ln_b=sgu_ln_b, sgu_w=sgu_w, sgu_b=sgu_b, attn_out_g=attn_out_g, gmlp_out_g=gmlp_out_g, w_out=w_out, norm2_g=norm2_g, w_ff1=w_ff1, w_ff2=w_ff2, final_norm_g=final_norm_g, loss_target=loss_target, m_norm1_g=m_norm1_g, m_w_in=m_w_in, m_sgu_ln_g=m_sgu_ln_g, m_sgu_ln_b=m_sgu_ln_b, m_sgu_w=m_sgu_w, m_sgu_b=m_sgu_b, m_attn_out_g=m_attn_out_g, m_gmlp_out_g=m_gmlp_out_g, m_w_out=m_w_out, m_norm2_g=m_norm2_g, m_w_ff1=m_w_ff1, m_w_ff2=m_w_ff2, m_final_norm_g=m_final_norm_g, v_norm1_g=v_norm1_g, v_w_in=v_w_in, v_sgu_ln_g=v_sgu_ln_g, v_sgu_ln_b=v_sgu_ln_b, v_sgu_w=v_sgu_w, v_sgu_b=v_sgu_b, v_attn_out_g=v_attn_out_g, v_gmlp_out_g=v_gmlp_out_g, v_w_out=v_w_out, v_norm2_g=v_norm2_g, v_w_ff1=v_w_ff1, v_w_ff2=v_w_ff2, v_final_norm_g=v_final_norm_g)
    weights = {n: given[n] for n in TWIN_WEIGHTS}
    shared = {n: given[n] for n in SHARED_INPUTS}
    per_example = {n: given[n] for n in ['x']}
    grad_fn = _jax.value_and_grad(_loss, argnums=(0, 1))

    def one_microbatch(ex, loss_target):
        ex = dict(ex)
        diff = ex.pop(TWIN_DIFF_INPUT)
        return grad_fn(weights, diff, {**shared, **ex}, loss_target)

    if N_MICROBATCH == 1:
        loss, (grad_w, grad_x) = one_microbatch(per_example, given["loss_target"])
    else:
        def body(carry, xs):
            loss_sum, grad_sum = carry
            l_k, (gw_k, gx_k) = one_microbatch(xs[0], xs[1])
            with _jax.named_scope("update"):
                return (loss_sum + l_k, _jax.tree.map(_jnp.add, grad_sum, gw_k)), gx_k

        init = (_jnp.zeros((), _jnp.float32), _jax.tree.map(_jnp.zeros_like, weights))
        (loss, grad_w), grad_x = _jax.lax.scan(body, init, (per_example, given["loss_target"]))
    with _jax.named_scope("update"):
        delta_w, new_m, new_v = {}, {}, {}
        for n in TWIN_WEIGHTS:
            delta_w[n], new_m[n], new_v[n] = _adamw(weights[n], grad_w[n], given["m_" + n], given["v_" + n])
    return (loss, grad_x, *[grad_w[n] for n in TWIN_WEIGHTS], *[delta_w[n] for n in TWIN_WEIGHTS],
            *[new_m[n] for n in TWIN_WEIGHTS], *[new_v[n] for n in TWIN_WEIGHTS])
```

```python
import functools
import math

import numpy as np
import jax
import jax.numpy as jnp
from jax import lax
from jax.experimental import pallas as pl
from jax.experimental.pallas import tpu as pltpu

F32 = jnp.float32
BF16 = jnp.bfloat16

D = 1024
NH = 12
DH = 64
A = NH * DH
NG = 4
GW = NG * DH
INW = 3 * A + 2 * GW
DFF = 4 * D
CHUNK = 128
PATTERNS = ((128, 1), (512, 4), (2048, 16))
EPS = 1e-6
SCALE = DH ** -0.5
NEG = -1e30

LR, B1, B2, AEPS, WD, STEP = 0.001, 0.9, 0.999, 1e-08, 0.01, 10

TM = 512
ATT_ROWS = 1024
FF_CH = 1024
LANES = 128
NCHIP = 4
SHARD_ROWS = INW // NCHIP + D // NCHIP + DFF // NCHIP + DFF // NCHIP
VMEM_LIMIT = 56 * 1024 * 1024
MESH = pl.DeviceIdType.MESH


def _cparams(*sem, **kw):
    return pltpu.CompilerParams(dimension_semantics=sem if sem else None,
                                vmem_limit_bytes=VMEM_LIMIT, **kw)


def _dot(a, b):
    return jnp.dot(a, b, preferred_element_type=F32)


def _dot_nt(a, b):
    return lax.dot_general(a, b, (((1,), (1,)), ((), ())), preferred_element_type=F32)


def _dot_tn(a, b):
    return lax.dot_general(a, b, (((0,), (0,)), ((), ())), preferred_element_type=F32)


def _dot_hi(a, b):
    return jnp.dot(a, b, preferred_element_type=F32, precision=lax.Precision.HIGHEST)


def _alibi_slopes(n):
    def pow2(m):
        start = 2.0 ** (-8.0 / m)
        return [start ** (i + 1) for i in range(m)]
    if math.log2(n).is_integer():
        s = pow2(n)
    else:
        c = 2 ** int(math.floor(math.log2(n)))
        s = pow2(c) + pow2(2 * c)[0::2][: n - c]
    return np.asarray(s, dtype=np.float32)


def _rms_fwd(v, g):
    r = lax.rsqrt(jnp.mean(v * v, axis=-1, keepdims=True) + EPS)
    vn = v * r
    return vn * g, vn, r


def _rms_bwd(dy, vn, r, g):
    w = dy * g
    dv = r * (w - vn * jnp.mean(w * vn, axis=-1, keepdims=True))
    return dv, jnp.sum(dy * vn, axis=0, keepdims=True)


_K0 = math.sqrt(2.0 / math.pi)
_K1 = 0.044715


def _gelu(v):
    return 0.5 * v * (1.0 + jnp.tanh(_K0 * (v + _K1 * (v * v * v))))


def _gelu_grad(v):
    t = jnp.tanh(_K0 * (v + _K1 * (v * v * v)))
    return 0.5 * (1.0 + t) + 0.5 * v * (1.0 - t * t) * (_K0 * (1.0 + 3.0 * _K1 * v * v))


def _row_spec(rows, cols):
    return pl.BlockSpec((rows, cols), lambda i: (i, 0))


def _const_spec(shape):
    nd = len(shape)
    return pl.BlockSpec(shape, lambda i: (0,) * nd, pipeline_mode=pl.Buffered(1))


def _inproj_fwd(x, g1, win_t):
    s = x.shape[0]

    def body(x_ref, g_ref, w_ref, hn_ref, q_ref, k_ref, v_ref, u_ref, z_ref):
        hn, _, _ = _rms_fwd(x_ref[...], g_ref[...])
        hn = hn.astype(BF16)
        hn_ref[...] = hn
        q_ref[...] = (_dot_nt(hn, w_ref[0:A, :]) * SCALE).astype(BF16)
        k_ref[...] = _dot_nt(hn, w_ref[A:2 * A, :]).astype(BF16)
        v_ref[...] = _dot_nt(hn, w_ref[2 * A:3 * A, :]).astype(BF16)
        u_ref[...] = _dot_nt(hn, w_ref[3 * A:3 * A + GW, :])
        z_ref[...] = _dot_nt(hn, w_ref[3 * A + GW:INW, :])

    sd = jax.ShapeDtypeStruct
    return pl.pallas_call(
        body, name="inproj_fwd", grid=(s // TM,),
        in_specs=[_row_spec(TM, D), _const_spec((1, D)), _const_spec((INW, D))],
        out_specs=[_row_spec(TM, D), _row_spec(TM, A), _row_spec(TM, A), _row_spec(TM, A),
                   _row_spec(TM, GW), _row_spec(TM, GW)],
        out_shape=[sd((s, D), BF16), sd((s, A), BF16), sd((s, A), BF16), sd((s, A), BF16),
                   sd((s, GW), F32), sd((s, GW), F32)],
        compiler_params=_cparams("arbitrary"),
    )(x, g1, win_t)


def _band_bias(slope, dil):
    qi = lax.broadcasted_iota(jnp.int32, (CHUNK, CHUNK), 0)
    kj = lax.broadcasted_iota(jnp.int32, (CHUNK, CHUNK), 1)
    cur = jnp.where(kj <= qi, -slope * ((qi - kj) * dil).astype(F32), NEG)
    prev = jnp.where(kj >= qi, -slope * ((qi + CHUNK - kj) * dil).astype(F32), NEG)
    return cur, prev


def _att_geometry(s, dil):
    length = s // dil
    rows = min(length, ATT_ROWS)
    return length, rows, length // rows, rows // CHUNK


def _attn_fwd(q, k, v, slopes, dil):
    s = q.shape[0]
    length, rows, nch, nsub = _att_geometry(s, dil)

    def body(sl_ref, q_ref, k_ref, v_ref, kh_ref, vh_ref, o_ref, lse_ref, kbuf, vbuf):
        hp = pl.program_id(1)
        ch = pl.program_id(2)
        kbuf[0:CHUNK, :] = kh_ref[...]
        kbuf[CHUNK:, :] = k_ref[...]
        vbuf[0:CHUNK, :] = vh_ref[...]
        vbuf[CHUNK:, :] = v_ref[...]
        lane = lax.broadcasted_iota(jnp.int32, (CHUNK, LANES), 1)
        head_mask = (lane < DH, lane >= DH)
        bias = [_band_bias(sl_ref[2 * hp + h], dil) for h in range(2)]

        def sub(i, carry):
            row = pl.multiple_of(i * CHUNK, CHUNK)
            qn = q_ref[pl.ds(row, CHUNK), :]
            kp = kbuf[pl.ds(row, CHUNK), :]
            kc = kbuf[pl.ds(row + CHUNK, CHUNK), :]
            vp = vbuf[pl.ds(row, CHUNK), :]
            vc = vbuf[pl.ds(row + CHUNK, CHUNK), :]
            pen = jnp.where((ch == 0) & (i == 0), NEG, 0.0)
            outs, lses = [], []
            for h in range(2):
                qh = jnp.where(head_mask[h], qn, jnp.zeros_like(qn))
                sc = _dot_nt(qh, kc) + bias[h][0]
                sp = _dot_nt(qh, kp) + (bias[h][1] + pen)
                m = jnp.maximum(jnp.max(sc, axis=-1, keepdims=True), jnp.max(sp, axis=-1, keepdims=True))
                pc = jnp.exp(sc - m)
                pp = jnp.exp(sp - m)
                l = jnp.sum(pc, axis=-1, keepdims=True) + jnp.sum(pp, axis=-1, keepdims=True)
                acc = _dot(pc.astype(BF16), vc) + _dot(pp.astype(BF16), vp)
                outs.append(acc / l)
                lses.append(jnp.broadcast_to(m + jnp.log(l), (CHUNK, LANES)))
            o_ref[pl.ds(row, CHUNK), :] = jnp.where(head_mask[0], outs[0], outs[1])
            lse_ref[pl.ds(row, CHUNK), :] = jnp.where(head_mask[0], lses[0], lses[1])
            return carry

        lax.fori_loop(0, nsub, sub, 0)

    main = pl.BlockSpec((rows, LANES), lambda r, hp, c: (c, r * 6 + hp))
    halo = pl.BlockSpec((CHUNK, LANES), lambda r, hp, c: (jnp.maximum(c * nsub - 1, 0), r * 6 + hp))
    view = lambda t: t.reshape(length, dil * A)
    sd = jax.ShapeDtypeStruct
    o, lse = pl.pallas_call(
        body, name=f"attn_fwd_d{dil}", grid=(dil, NH // 2, nch),
        in_specs=[pl.BlockSpec(memory_space=pltpu.SMEM), main, main, main, halo, halo],
        out_specs=[main, main],
        out_shape=[sd((length, dil * A), F32), sd((length, dil * A), F32)],
        scratch_shapes=[pltpu.VMEM((rows + CHUNK, LANES), BF16), pltpu.VMEM((rows + CHUNK, LANES), BF16)],
        compiler_params=_cparams("arbitrary", "arbitrary", "arbitrary"),
    )(slopes, view(q), view(k), view(v), view(k), view(v))
    return o.reshape(s, A), lse.reshape(s, A)


def _attn_bwd_dq(q, k, v, do, lse, delta, slopes, dil, acc):
    s = q.shape[0]
    length, rows, nch, nsub = _att_geometry(s, dil)
    has_acc = acc is not None

    def body(sl_ref, q_ref, k_ref, v_ref, do_ref, lse_ref, dl_ref, kh_ref, vh_ref, *rest):
        if has_acc:
            acc_ref, dq_ref, kbuf, vbuf = rest
        else:
            dq_ref, kbuf, vbuf = rest
        hp = pl.program_id(1)
        ch = pl.program_id(2)
        kbuf[0:CHUNK, :] = kh_ref[...]
        kbuf[CHUNK:, :] = k_ref[...]
        vbuf[0:CHUNK, :] = vh_ref[...]
        vbuf[CHUNK:, :] = v_ref[...]
        lane = lax.broadcasted_iota(jnp.int32, (CHUNK, LANES), 1)
        head_mask = (lane < DH, lane >= DH)
        bias = [_band_bias(sl_ref[2 * hp + h], dil) for h in range(2)]

        def sub(i, carry):
            row = pl.multiple_of(i * CHUNK, CHUNK)
            rs = pl.ds(row, CHUNK)
            qn = q_ref[rs, :]
            don = do_ref[rs, :]
            lse_t = lse_ref[rs, :]
            dl_t = dl_ref[rs, :]
            kp = kbuf[rs, :]
            kc = kbuf[pl.ds(row + CHUNK, CHUNK), :]
            vp = vbuf[rs, :]
            vc = vbuf[pl.ds(row + CHUNK, CHUNK), :]
            pen = jnp.where((ch == 0) & (i == 0), NEG, 0.0)
            zero = jnp.zeros_like(qn)
            total = None
            for h in range(2):
                col = slice(h * DH, h * DH + 1)
                lse_h = lse_t[:, col]
                dl_h = dl_t[:, col]
                qh = jnp.where(head_mask[h], qn, zero)
                doh = jnp.where(head_mask[h], don, zero)
                pc = jnp.exp(_dot_nt(qh, kc) + bias[h][0] - lse_h)
                pp = jnp.exp(_dot_nt(qh, kp) + (bias[h][1] + pen) - lse_h)
                dsc = pc * (_dot_nt(doh, vc) - dl_h)
                dsp = pp * (_dot_nt(doh, vp) - dl_h)
                kch = jnp.where(head_mask[h], kc, zero)
                kph = jnp.where(head_mask[h], kp, zero)
                part = _dot(dsc.astype(BF16), kch) + _dot(dsp.astype(BF16), kph)
                total = part if total is None else total + part
            if has_acc:
                total = total + acc_ref[rs, :]
            dq_ref[rs, :] = total
            return carry

        lax.fori_loop(0, nsub, sub, 0)

    main = pl.BlockSpec((rows, LANES), lambda r, hp, c: (c, r * 6 + hp))
    halo = pl.BlockSpec((CHUNK, LANES), lambda r, hp, c: (jnp.maximum(c * nsub - 1, 0), r * 6 + hp))
    view = lambda t: t.reshape(length, dil * A)
    ins = [slopes, view(q), view(k), view(v), view(do), view(lse), view(delta), view(k), view(v)]
    specs = [pl.BlockSpec(memory_space=pltpu.SMEM), main, main, main, main, main, main, halo, halo]
    if has_acc:
        ins.append(view(acc))
        specs.append(main)
    dq = pl.pallas_call(
        body, name=f"attn_dq_d{dil}", grid=(dil, NH // 2, nch),
        in_specs=specs, out_specs=main,
        out_shape=jax.ShapeDtypeStruct((length, dil * A), F32),
        scratch_shapes=[pltpu.VMEM((rows + CHUNK, LANES), BF16), pltpu.VMEM((rows + CHUNK, LANES), BF16)],
        compiler_params=_cparams("arbitrary", "arbitrary", "arbitrary"),
    )(*ins)
    return dq.reshape(s, A)


def _attn_bwd_dkv(q, k, v, do, lse, delta, slopes, dil, acc_k, acc_v):
    s = q.shape[0]
    length, rows, nch, nsub = _att_geometry(s, dil)
    nblk = length // CHUNK
    has_acc = acc_k is not None

    def body(sl_ref, k_ref, v_ref, q_ref, do_ref, lse_ref, dl_ref, qh_ref, doh_ref, lseh_ref, dlh_ref, *rest):
        if has_acc:
            ak_ref, av_ref, dk_ref, dv_ref, qbuf, dobuf, lsebuf, dlbuf = rest
        else:
            dk_ref, dv_ref, qbuf, dobuf, lsebuf, dlbuf = rest
        hp = pl.program_id(1)
        ch = pl.program_id(2)
        for buf, main_ref, halo_ref in ((qbuf, q_ref, qh_ref), (dobuf, do_ref, doh_ref),
                                        (lsebuf, lse_ref, lseh_ref), (dlbuf, dl_ref, dlh_ref)):
            buf[0:rows, :] = main_ref[...]
            buf[rows:, :] = halo_ref[...]
        lane = lax.broadcasted_iota(jnp.int32, (CHUNK, LANES), 1)
        head_mask = (lane < DH, lane >= DH)
        bias = [_band_bias(sl_ref[2 * hp + h], dil) for h in range(2)]

        def sub(i, carry):
            row = pl.multiple_of(i * CHUNK, CHUNK)
            rs = pl.ds(row, CHUNK)
            rn = pl.ds(row + CHUNK, CHUNK)
            kc = k_ref[rs, :]
            vc = v_ref[rs, :]
            pen = jnp.where((ch == nch - 1) & (i == nsub - 1), NEG, 0.0)
            zero = jnp.zeros_like(kc)
            dk = None
            dv = None
            for h in range(2):
                col = slice(h * DH, h * DH + 1)
                for which, sl in ((0, rs), (1, rn)):
                    qn = qbuf[sl, :]
                    don = dobuf[sl, :]
                    lse_h = lsebuf[sl, :][:, col]
                    dl_h = dlbuf[sl, :][:, col]
                    qh = jnp.where(head_mask[h], qn, zero)
                    doh = jnp.where(head_mask[h], don, zero)
                    b = bias[h][0] if which == 0 else bias[h][1] + pen
                    p = jnp.exp(_dot_nt(qh, kc) + b - lse_h)
                    ds = p * (_dot_nt(doh, vc) - dl_h)
                    pv = _dot_tn(p.astype(BF16), doh)
                    pk = _dot_tn(ds.astype(BF16), qh)
                    dv = pv if dv is None else dv + pv
                    dk = pk if dk is None else dk + pk
            if has_acc:
                dk = dk + ak_ref[rs, :]
                dv = dv + av_ref[rs, :]
            dk_ref[rs, :] = dk
            dv_ref[rs, :] = dv
            return carry

        lax.fori_loop(0, nsub, sub, 0)

    main = pl.BlockSpec((rows, LANES), lambda r, hp, c: (c, r * 6 + hp))
    halo = pl.BlockSpec((CHUNK, LANES), lambda r, hp, c: (jnp.minimum((c + 1) * nsub, nblk - 1), r * 6 + hp))
    view = lambda t: t.reshape(length, dil * A)
    ins = [slopes, view(k), view(v), view(q), view(do), view(lse), view(delta),
           view(q), view(do), view(lse), view(delta)]
    specs = [pl.BlockSpec(memory_space=pltpu.SMEM), main, main, main, main, main, main, halo, halo, halo, halo]
    if has_acc:
        ins += [view(acc_k), view(acc_v)]
        specs += [main, main]
    sd = jax.ShapeDtypeStruct
    dk, dv = pl.pallas_call(
        body, name=f"attn_dkv_d{dil}", grid=(dil, NH // 2, nch),
        in_specs=specs, out_specs=[main, main],
        out_shape=[sd((length, dil * A), F32), sd((length, dil * A), F32)],
        scratch_shapes=[pltpu.VMEM((rows + CHUNK, LANES), BF16), pltpu.VMEM((rows + CHUNK, LANES), BF16),
                        pltpu.VMEM((rows + CHUNK, LANES), F32), pltpu.VMEM((rows + CHUNK, LANES), F32)],
        compiler_params=_cparams("arbitrary", "arbitrary", "arbitrary"),
    )(*ins)
    return dk.reshape(s, A), dv.reshape(s, A)


def _group_masks(width):
    lane = lax.broadcasted_iota(jnp.int32, (1, width), 1)
    return [(lane >= g * DH) & (lane < (g + 1) * DH) for g in range(width // DH)]


def _group_mean_matrix():
    i = lax.broadcasted_iota(jnp.int32, (GW, GW), 0) // DH
    j = lax.broadcasted_iota(jnp.int32, (GW, GW), 1) // DH
    return jnp.where(i == j, 1.0 / DH, 0.0).astype(F32)


def _tri_mask(lower):
    t = lax.broadcasted_iota(jnp.int32, (CHUNK, CHUNK), 0)
    u = lax.broadcasted_iota(jnp.int32, (CHUNK, CHUNK), 1)
    return (u <= t) if lower else (u >= t)


def _sgu_forward(u, z, lng, lnb, w_ref, bias_t, pmat, rows):
    ug = _gelu(u)
    zg = _gelu(z)
    mu = _dot_hi(zg, pmat)
    zc = zg - mu
    var = _dot_hi(zc * zc, pmat)
    rstd = lax.rsqrt(var + EPS)
    zhat = zc * rstd
    zn = (zhat * lng + lnb).astype(BF16)
    gm = _group_masks(GW)
    tri = _tri_mask(True)
    ws = [jnp.where(tri, w_ref[g], 0.0).astype(BF16) for g in range(NG)]
    pieces = []
    for c in range(rows // CHUNK):
        znc = zn[c * CHUNK:(c + 1) * CHUNK, :]
        mix = None
        for g in range(NG):
            part = jnp.where(gm[g], _dot(ws[g], znc), 0.0)
            mix = part if mix is None else mix + part
        pieces.append(mix + bias_t)
    mixed = jnp.concatenate(pieces, axis=0) if len(pieces) > 1 else pieces[0]
    return ug * mixed, ug, zhat, rstd, zn, mixed


def _mix_fwd(os_, ls_, u, z, x, lng, lnb, sgu_w, bias_t, ga, gg, wout):
    s = x.shape[0]

    def body(o1, o2, o3, l1, l2, l3, u_ref, z_ref, x_ref, lng_ref, lnb_ref, w_ref, bt_ref, ga_ref, gg_ref,
             wo_ref, attn_ref, lse_ref, mixed_ref, h1_ref):
        la, lb, lc = l1[...], l2[...], l3[...]
        mx = jnp.maximum(jnp.maximum(la, lb), lc)
        ea, eb, ec = jnp.exp(la - mx), jnp.exp(lb - mx), jnp.exp(lc - mx)
        den = ea + eb + ec
        attn = (ea * o1[...] + eb * o2[...] + ec * o3[...]) / den
        attn_ref[...] = attn
        lse_ref[...] = mx + jnp.log(den)
        an, _, _ = _rms_fwd(attn, ga_ref[...])
        gmv, _, _, _, _, _ = _sgu_forward(u_ref[...], z_ref[...], lng_ref[...], lnb_ref[...], w_ref,
                                          bt_ref[...], _group_mean_matrix(), TM)
        gn, _, _ = _rms_fwd(gmv, gg_ref[...])
        mixed = jnp.concatenate([an, gn], axis=-1).astype(BF16)
        mixed_ref[...] = mixed
        h1_ref[...] = x_ref[...] + _dot(mixed, wo_ref[...])

    sd = jax.ShapeDtypeStruct
    return pl.pallas_call(
        body, name="mix_fwd", grid=(s // TM,),
        in_specs=[_row_spec(TM, A)] * 6 + [_row_spec(TM, GW), _row_spec(TM, GW), _row_spec(TM, D),
                  _const_spec((1, GW)), _const_spec((1, GW)), _const_spec((NG, CHUNK, CHUNK)),
                  _const_spec((CHUNK, GW)), _const_spec((1, A)), _const_spec((1, GW)), _const_spec((D, D))],
        out_specs=[_row_spec(TM, A), _row_spec(TM, A), _row_spec(TM, D), _row_spec(TM, D)],
        out_shape=[sd((s, A), F32), sd((s, A), F32), sd((s, D), BF16), sd((s, D), F32)],
        compiler_params=_cparams("arbitrary"),
    )(*os_, *ls_, u, z, x, lng, lnb, sgu_w, bias_t, ga, gg, wout)


def _mlp_fwd(h1, g2, wff1_t, wff2, gf, target):
    s = h1.shape[0]

    def body(h1_ref, g2_ref, w1_ref, w2_ref, gf_ref, t_ref, hn_ref, rf_ref, a_ref, dh2_ref, loss_ref, dgf_ref):
        i = pl.program_id(0)
        h1v = h1_ref[...]
        hn, _, _ = _rms_fwd(h1v, g2_ref[...])
        hn = hn.astype(BF16)
        hn_ref[...] = hn
        acc = h1v
        for j in range(DFF // FF_CH):
            cols = slice(j * FF_CH, (j + 1) * FF_CH)
            rf = jnp.maximum(_dot_nt(hn, w1_ref[cols, :]), 0.0)
            act = (rf * rf).astype(BF16)
            rf_ref[:, cols] = rf.astype(BF16)
            a_ref[:, cols] = act
            acc = acc + _dot(act, w2_ref[cols, :])
        y, h2n, r3 = _rms_fwd(acc, gf_ref[...])
        err = y - t_ref[...]
        part = 0.5 * jnp.sum(jnp.mean(err * err, axis=-1, keepdims=True), axis=0, keepdims=True)
        dy = err * (1.0 / D)
        dh2, dgf = _rms_bwd(dy, h2n, r3, gf_ref[...])
        dh2_ref[...] = dh2

        @pl.when(i == 0)
        def _():
            loss_ref[...] = jnp.zeros_like(loss_ref)
            dgf_ref[...] = jnp.zeros_like(dgf_ref)

        loss_ref[...] += jnp.broadcast_to(part, loss_ref.shape)
        dgf_ref[...] += dgf

    sd = jax.ShapeDtypeStruct
    return pl.pallas_call(
        body, name="mlp_fwd", grid=(s // TM,),
        in_specs=[_row_spec(TM, D), _const_spec((1, D)), _const_spec((DFF, D)), _const_spec((DFF, D)),
                  _const_spec((1, D)), _row_spec(TM, D)],
        out_specs=[_row_spec(TM, D), _row_spec(TM, DFF), _row_spec(TM, DFF), _row_spec(TM, D),
                   _const_spec((1, LANES)), _const_spec((1, D))],
        out_shape=[sd((s, D), BF16), sd((s, DFF), BF16), sd((s, DFF), BF16), sd((s, D), F32),
                   sd((1, LANES), F32), sd((1, D), F32)],
        compiler_params=_cparams("arbitrary"),
    )(h1, g2, wff1_t, wff2, gf, target)


def _mlp_bwd(dh2, rf, h1, g2, wff1_t, wff2):
    s = h1.shape[0]

    def body(dh2_ref, rf_ref, h1_ref, g2_ref, w1_ref, w2_ref, df_ref, dh1_ref, dg2_ref):
        i = pl.program_id(0)
        dh2v = dh2_ref[...]
        dh2b = dh2v.astype(BF16)
        dhn = jnp.zeros((TM, D), F32)
        for j in range(DFF // FF_CH):
            cols = slice(j * FF_CH, (j + 1) * FF_CH)
            da = _dot_nt(dh2b, w2_ref[cols, :])
            df = (da * (2.0 * rf_ref[:, cols].astype(F32))).astype(BF16)
            df_ref[:, cols] = df
            dhn = dhn + _dot(df, w1_ref[cols, :])
        _, h1n, r2 = _rms_fwd(h1_ref[...], g2_ref[...])
        dres, dg2 = _rms_bwd(dhn, h1n, r2, g2_ref[...])
        dh1_ref[...] = dh2v + dres

        @pl.when(i == 0)
        def _():
            dg2_ref[...] = jnp.zeros_like(dg2_ref)

        dg2_ref[...] += dg2

    sd = jax.ShapeDtypeStruct
    return pl.pallas_call(
        body, name="mlp_bwd", grid=(s // TM,),
        in_specs=[_row_spec(TM, D), _row_spec(TM, DFF), _row_spec(TM, D), _const_spec((1, D)),
                  _const_spec((DFF, D)), _const_spec((DFF, D))],
        out_specs=[_row_spec(TM, DFF), _row_spec(TM, D), _const_spec((1, D))],
        out_shape=[sd((s, DFF), BF16), sd((s, D), F32), sd((1, D), F32)],
        compiler_params=_cparams("arbitrary"),
    )(dh2, rf, h1, g2, wff1_t, wff2)


def _mix_bwd(dh1, attn, u, z, lng, lnb, sgu_w, sgu_wt, bias_t, ga, gg, wout):
    s = dh1.shape[0]
    nsteps = s // TM

    def body(dh1_ref, attn_ref, u_ref, z_ref, lng_ref, lnb_ref, w_ref, wt_ref, bt_ref, ga_ref, gg_ref, wo_ref,
             do_ref, dl_ref, du_ref, dz_ref, dga_ref, dgg_ref, dlng_ref, dlnb_ref, dws_ref, db_ref, dbt_acc):
        i = pl.program_id(0)

        @pl.when(i == 0)
        def _():
            for r in (dga_ref, dgg_ref, dlng_ref, dlnb_ref, dws_ref, db_ref, dbt_acc):
                r[...] = jnp.zeros_like(r)

        dmixed = _dot_nt(dh1_ref[...].astype(BF16), wo_ref[...])
        attn = attn_ref[...]
        _, an, ra = _rms_fwd(attn, ga_ref[...])
        dattn, dga = _rms_bwd(dmixed[:, :A], an, ra, ga_ref[...])
        dga_ref[...] += dga
        do_ref[...] = dattn.astype(BF16)
        prod = dattn * attn
        delta = jnp.zeros_like(prod)
        for hm in _group_masks(A):
            delta = delta + jnp.where(hm, jnp.sum(jnp.where(hm, prod, 0.0), axis=-1, keepdims=True), 0.0)
        dl_ref[...] = delta
        pmat = _group_mean_matrix()
        lng = lng_ref[...]
        uv, zv = u_ref[...], z_ref[...]
        gmv, ug, zhat, rstd, zn, mixed = _sgu_forward(uv, zv, lng, lnb_ref[...], w_ref, bt_ref[...], pmat, TM)
        _, gmn, rg = _rms_fwd(gmv, gg_ref[...])
        dgm, dgg = _rms_bwd(dmixed[:, A:], gmn, rg, gg_ref[...])
        dgg_ref[...] += dgg
        du_ref[...] = dgm * mixed * _gelu_grad(uv)
        dmx = dgm * ug
        dmxb = dmx.astype(BF16)
        gm = _group_masks(GW)
        tri_t = _tri_mask(False)
        wst = [jnp.where(tri_t, wt_ref[g], 0.0).astype(BF16) for g in range(NG)]
        zero = jnp.zeros((CHUNK, GW), BF16)
        dzn_pieces = []
        for c in range(TM // CHUNK):
            rs = slice(c * CHUNK, (c + 1) * CHUNK)
            dmc = dmxb[rs, :]
            znc = zn[rs, :]
            dbt_acc[...] += dmx[rs, :]
            dzn = None
            for g in range(NG):
                dws_ref[g] += _dot_nt(jnp.where(gm[g], dmc, zero), znc)
                part = jnp.where(gm[g], _dot(wst[g], dmc), 0.0)
                dzn = part if dzn is None else dzn + part
            dzn_pieces.append(dzn)
        dzn = jnp.concatenate(dzn_pieces, axis=0)
        dlng_ref[...] += jnp.sum(dzn * zhat, axis=0, keepdims=True)
        dlnb_ref[...] += jnp.sum(dzn, axis=0, keepdims=True)
        dzh = dzn * lng
        dzg = rstd * (dzh - _dot_hi(dzh, pmat) - zhat * _dot_hi(dzh * zhat, pmat))
        dz_ref[...] = dzg * _gelu_grad(zv)

        @pl.when(i == nsteps - 1)
        def _():
            tri = _tri_mask(True)
            for g in range(NG):
                dws_ref[g] = jnp.where(tri, dws_ref[g], 0.0)
            acc = dbt_acc[...]
            lane = lax.broadcasted_iota(jnp.int32, (CHUNK, LANES), 1)
            out = jnp.zeros((CHUNK, LANES), F32)
            for g in range(NG):
                sg = jnp.sum(jnp.where(gm[g], acc, 0.0), axis=-1, keepdims=True)
                out = jnp.where(lane == g, sg, out)
            db_ref[...] = out

    sd = jax.ShapeDtypeStruct
    return pl.pallas_call(
        body, name="mix_bwd", grid=(nsteps,),
        in_specs=[_row_spec(TM, D), _row_spec(TM, A), _row_spec(TM, GW), _row_spec(TM, GW),
                  _const_spec((1, GW)), _const_spec((1, GW)), _const_spec((NG, CHUNK, CHUNK)),
                  _const_spec((NG, CHUNK, CHUNK)), _const_spec((CHUNK, GW)), _const_spec((1, A)),
                  _const_spec((1, GW)), _const_spec((D, D))],
        out_specs=[_row_spec(TM, A), _row_spec(TM, A), _row_spec(TM, GW), _row_spec(TM, GW),
                   _const_spec((1, A)), _const_spec((1, GW)), _const_spec((1, GW)), _const_spec((1, GW)),
                   _const_spec((NG, CHUNK, CHUNK)), _const_spec((CHUNK, LANES))],
        out_shape=[sd((s, A), BF16), sd((s, A), F32), sd((s, GW), F32), sd((s, GW), F32),
                   sd((1, A), F32), sd((1, GW), F32), sd((1, GW), F32), sd((1, GW), F32),
                   sd((NG, CHUNK, CHUNK), F32), sd((CHUNK, LANES), F32)],
        scratch_shapes=[pltpu.VMEM((CHUNK, GW), F32)],
        compiler_params=_cparams("arbitrary"),
    )(dh1, attn, u, z, lng, lnb, sgu_w, sgu_wt, bias_t, ga, gg, wout)


def _inproj_bwd(dq, dk, dv, du, dz, dh1, x, g1, win_t):
    s = x.shape[0]

    def body(dq_ref, dk_ref, dv_ref, du_ref, dz_ref, dh1_ref, x_ref, g_ref, w_ref, dp_ref, dx_ref, dg_ref):
        i = pl.program_id(0)
        dp = jnp.concatenate([dq_ref[...] * SCALE, dk_ref[...], dv_ref[...], du_ref[...], dz_ref[...]],
                             axis=-1).astype(BF16)
        dp_ref[...] = dp
        dhn = _dot(dp, w_ref[...])
        _, xn, r1 = _rms_fwd(x_ref[...], g_ref[...])
        dres, dg = _rms_bwd(dhn, xn, r1, g_ref[...])
        dx_ref[...] = dh1_ref[...] + dres

        @pl.when(i == 0)
        def _():
            dg_ref[...] = jnp.zeros_like(dg_ref)

        dg_ref[...] += dg

    sd = jax.ShapeDtypeStruct
    return pl.pallas_call(
        body, name="inproj_bwd", grid=(s // TM,),
        in_specs=[_row_spec(TM, A)] * 3 + [_row_spec(TM, GW)] * 2 + [_row_spec(TM, D), _row_spec(TM, D),
                  _const_spec((1, D)), _const_spec((INW, D))],
        out_specs=[_row_spec(TM, INW), _row_spec(TM, D), _const_spec((1, D))],
        out_shape=[sd((s, INW), BF16), sd((s, D), F32), sd((1, D), F32)],
        compiler_params=_cparams("arbitrary"),
    )(dq, dk, dv, du, dz, dh1, x, g1, win_t)


def _wgrad(a, b, name, bm, bn, bk=TM):
    s, m = a.shape
    n = b.shape[1]
    bm, bn = min(bm, m), min(bn, n)

    def body(a_ref, b_ref, o_ref):
        @pl.when(pl.program_id(2) == 0)
        def _():
            o_ref[...] = jnp.zeros_like(o_ref)

        o_ref[...] += _dot_tn(a_ref[...].astype(BF16), b_ref[...].astype(BF16))

    return pl.pallas_call(
        body, name=name, grid=(m // bm, n // bn, s // bk),
        in_specs=[pl.BlockSpec((bk, bm), lambda i, j, k: (k, i)), pl.BlockSpec((bk, bn), lambda i, j, k: (k, j))],
        out_specs=pl.BlockSpec((bm, bn), lambda i, j, k: (i, j)),
        out_shape=jax.ShapeDtypeStruct((m, n), F32),
        compiler_params=_cparams("arbitrary", "arbitrary", "arbitrary"),
    )(a, b)


def _adamw_math(w, g, m, v):
    m = B1 * m + (1.0 - B1) * g
    v = B2 * v + (1.0 - B2) * (g * g)
    m_hat = m / (1.0 - B1 ** STEP)
    v_hat = v / (1.0 - B2 ** STEP)
    delta = -LR * (m_hat / (jnp.sqrt(v_hat) + AEPS) + WD * w)
    return delta, m, v


def _adamw(w, g, m, v, name):
    rows, cols = w.shape
    br = min(rows, 256)
    while rows % br:
        br -= 8

    def body(w_ref, g_ref, m_ref, v_ref, d_ref, mo_ref, vo_ref):
        d, mn, vn = _adamw_math(w_ref[...], g_ref[...], m_ref[...], v_ref[...])
        d_ref[...] = d
        mo_ref[...] = mn
        vo_ref[...] = vn

    spec = _row_spec(br, cols)
    sd = jax.ShapeDtypeStruct((rows, cols), F32)
    return pl.pallas_call(
        body, name=name, grid=(rows // br,), in_specs=[spec] * 4, out_specs=[spec] * 3,
        out_shape=[sd, sd, sd], compiler_params=_cparams("arbitrary"),
    )(w, g, m, v)


def _local_step(x, target, small, win_t, wout, wff1_t, wff2):
    slopes = jnp.asarray(_alibi_slopes(NH))
    hn1, q, k, v, u, z = _inproj_fwd(x, small["norm1_g"], win_t)
    outs, lses = [], []
    for _, dil in PATTERNS:
        o, l = _attn_fwd(q, k, v, slopes, dil)
        outs.append(o)
        lses.append(l)
    attn, lse, mixed, h1 = _mix_fwd(outs, lses, u, z, x, small["ln_g"], small["ln_b"], small["sgu_w"],
                                    small["bias_t"], small["attn_out_g"], small["gmlp_out_g"], wout)
    hn2, rf, act, dh2, loss, dgf = _mlp_fwd(h1, small["norm2_g"], wff1_t, wff2, small["final_norm_g"], target)
    df, dh1, dg2 = _mlp_bwd(dh2, rf, h1, small["norm2_g"], wff1_t, wff2)
    (do, delta, du, dz, dga, dgg, dlng, dlnb, dws, db) = _mix_bwd(
        dh1, attn, u, z, small["ln_g"], small["ln_b"], small["sgu_w"], small["sgu_wt"], small["bias_t"],
        small["attn_out_g"], small["gmlp_out_g"], wout)
    dq = dk = dv = None
    for _, dil in PATTERNS:
        dq = _attn_bwd_dq(q, k, v, do, lse, delta, slopes, dil, dq)
        dk, dv = _attn_bwd_dkv(q, k, v, do, lse, delta, slopes, dil, dk, dv)
    dproj, dx, dg1 = _inproj_bwd(dq, dk, dv, du, dz, dh1, x, small["norm1_g"], win_t)
    gwin_t = _wgrad(dproj, hn1, "wgrad_in", INW // 2, D)
    gwout = _wgrad(mixed, dh1, "wgrad_out", D, D)
    gwff1_t = _wgrad(df, hn2, "wgrad_ff1", 1024, D)
    gwff2 = _wgrad(act, dh2, "wgrad_ff2", 1024, D)
    small_grads = dict(norm1_g=dg1, ln_g=dlng, ln_b=dlnb, sgu_w=dws, sgu_b=db[:, :NG].T,
                       attn_out_g=dga, gmlp_out_g=dgg, norm2_g=dg2, final_norm_g=dgf)
    return loss[0, 0], dx, small_grads, (gwin_t, gwout, gwff1_t, gwff2)


ANY = pl.BlockSpec(memory_space=pl.ANY)
HALF_ROWS = SHARD_ROWS // 2
ADD_ROWS = 376


def _position():
    return lax.axis_index("x"), lax.axis_index("y"), lax.axis_index("c")


def _other_chips(x, y):
    return [(1 - x, y), (x, 1 - y), (1 - x, 1 - y)]


def _remote(src, dst, send_sem, recv_sem, device):
    return pltpu.make_async_remote_copy(src_ref=src, dst_ref=dst, send_sem=send_sem, recv_sem=recv_sem,
                                        device_id=device, device_id_type=MESH)


def _gather_weights(wp):
    def body(w_ref, out_ref, send_sems, recv_sems, local_sem):
        x, y, c = _position()
        me = 2 * x + y
        mine = pltpu.make_async_copy(w_ref, out_ref.at[me], local_sem)
        mine.start()
        sends = []
        for k, (px, py) in enumerate(_other_chips(x, y)):
            cp = _remote(w_ref, out_ref.at[me], send_sems.at[k], recv_sems.at[k], (px, py, c))
            cp.start()
            sends.append(cp)
        for k, (px, py) in enumerate(_other_chips(x, y)):
            _remote(w_ref, out_ref.at[2 * px + py], send_sems.at[k], recv_sems.at[k], (px, py, c)).wait_recv()
        for cp in sends:
            cp.wait_send()
        mine.wait()

    return pl.pallas_call(
        body, name="gather_weights", in_specs=[ANY], out_specs=ANY,
        out_shape=jax.ShapeDtypeStruct((NCHIP,) + wp.shape, wp.dtype),
        scratch_shapes=[pltpu.SemaphoreType.DMA((3,)), pltpu.SemaphoreType.DMA((3,)), pltpu.SemaphoreType.DMA],
        compiler_params=pltpu.CompilerParams(has_side_effects=True),
    )(wp)


def _exchange_halves(gall):
    def body(g_ref, own_ref, got_ref, send_sem, recv_sem, local_sem):
        x, y, c = _position()
        keep = g_ref.at[:, pl.ds(c * HALF_ROWS, HALF_ROWS), :]
        give = g_ref.at[:, pl.ds((1 - c) * HALF_ROWS, HALF_ROWS), :]
        local = pltpu.make_async_copy(keep, own_ref, local_sem)
        local.start()
        cp = _remote(give, got_ref, send_sem, recv_sem, (x, y, 1 - c))
        cp.start()
        cp.wait()
        local.wait()

    sd = jax.ShapeDtypeStruct((NCHIP, HALF_ROWS, D), F32)
    return pl.pallas_call(
        body, name="exchange_halves", in_specs=[ANY], out_specs=[ANY, ANY], out_shape=[sd, sd],
        scratch_shapes=[pltpu.SemaphoreType.DMA, pltpu.SemaphoreType.DMA, pltpu.SemaphoreType.DMA],
        compiler_params=pltpu.CompilerParams(has_side_effects=True),
    )(gall)


def _scatter_to_owners(part):
    def body(p_ref, own_ref, got_ref, send_sems, recv_sems, local_sem):
        x, y, c = _position()
        local = pltpu.make_async_copy(p_ref.at[2 * x + y], own_ref, local_sem)
        local.start()
        sends = []
        for k, (px, py) in enumerate(_other_chips(x, y)):
            cp = _remote(p_ref.at[2 * px + py], got_ref.at[k], send_sems.at[k], recv_sems.at[k], (px, py, c))
            cp.start()
            sends.append(cp)
        for cp in sends:
            cp.wait()
        local.wait()

    return pl.pallas_call(
        body, name="scatter_to_owners", in_specs=[ANY], out_specs=[ANY, ANY],
        out_shape=[jax.ShapeDtypeStruct((HALF_ROWS, D), F32), jax.ShapeDtypeStruct((3, HALF_ROWS, D), F32)],
        scratch_shapes=[pltpu.SemaphoreType.DMA((3,)), pltpu.SemaphoreType.DMA((3,)), pltpu.SemaphoreType.DMA],
        compiler_params=pltpu.CompilerParams(has_side_effects=True),
    )(part)


def _share_with_sibling(half):
    def body(h_ref, out_ref, send_sem, recv_sem, local_sem):
        x, y, c = _position()
        local = pltpu.make_async_copy(h_ref, out_ref.at[c], local_sem)
        local.start()
        cp = _remote(h_ref, out_ref.at[c], send_sem, recv_sem, (x, y, 1 - c))
        cp.start()
        cp.wait_send()
        _remote(h_ref, out_ref.at[1 - c], send_sem, recv_sem, (x, y, 1 - c)).wait_recv()
        local.wait()

    return pl.pallas_call(
        body, name="share_with_sibling", in_specs=[ANY], out_specs=ANY,
        out_shape=jax.ShapeDtypeStruct((2, HALF_ROWS, D), F32),
        scratch_shapes=[pltpu.SemaphoreType.DMA, pltpu.SemaphoreType.DMA, pltpu.SemaphoreType.DMA],
        compiler_params=pltpu.CompilerParams(has_side_effects=True),
    )(half)


def _add_slabs(terms, out_rows, name):
    n = len(terms)

    def body(*refs):
        acc = refs[0][...]
        for r in refs[1:n]:
            acc = acc + r[...]
        refs[n][...] = acc

    specs = [pl.BlockSpec((ADD_ROWS, D), functools.partial(lambda i, off: (i + off, 0), off=first // ADD_ROWS))
             for _, first in terms]
    return pl.pallas_call(
        body, name=name, grid=(out_rows // ADD_ROWS,), in_specs=specs, out_specs=_row_spec(ADD_ROWS, D),
        out_shape=jax.ShapeDtypeStruct((out_rows, D), F32), compiler_params=_cparams("arbitrary"),
    )(*[a for a, _ in terms])


def _reduce_grads(gall):
    own, got = _exchange_halves(gall)
    rows = NCHIP * HALF_ROWS
    chip = _add_slabs([(own.reshape(rows, D), 0), (got.reshape(rows, D), 0)], rows, "add_sibling")
    mine, others = _scatter_to_owners(chip.reshape(NCHIP, HALF_ROWS, D))
    flat = others.reshape(3 * HALF_ROWS, D)
    half = _add_slabs([(mine, 0), (flat, 0), (flat, HALF_ROWS), (flat, 2 * HALF_ROWS)], HALF_ROWS, "add_chips")
    return _share_with_sibling(half).reshape(SHARD_ROWS, D)


SMALL_SIZES = (("norm1_g", D), ("sgu_ln_g", GW), ("sgu_ln_b", GW), ("sgu_w", NG * CHUNK * CHUNK),
               ("sgu_b", NG * CHUNK), ("attn_out_g", A), ("gmlp_out_g", GW), ("norm2_g", D),
               ("final_norm_g", D))
SMALL_ROWS = sum(n for _, n in SMALL_SIZES) // LANES
NDEV = 8


def _pack_small(tree):
    return jnp.concatenate([tree[n].reshape(-1) for n, _ in SMALL_SIZES]).reshape(SMALL_ROWS, LANES)


def _unpack_small(pack, shapes):
    flat = pack.reshape(-1)
    out, off = {}, 0
    for n, size in SMALL_SIZES:
        out[n] = flat[off:off + size].reshape(shapes[n])
        off += size
    return out


def _small_allreduce_adamw(gpack, wpack, mpack, vpack):
    def body(g_ref, w_ref, m_ref, v_ref, go_ref, d_ref, mo_ref, vo_ref, slots, send_sems, recv_sems):
        x, y, c = _position()
        me = 4 * x + 2 * y + c
        slots[me] = g_ref[...]
        sends = []
        for k in range(1, NDEV):
            kx, ky, kc = (k >> 2) & 1, (k >> 1) & 1, k & 1
            peer = (1 - x if kx else x, 1 - y if ky else y, 1 - c if kc else c)
            cp = _remote(g_ref, slots.at[me], send_sems.at[k - 1], recv_sems.at[k - 1], peer)
            cp.start()
            sends.append((cp, peer))
        for k in range(1, NDEV):
            _, (px, py, pc) = sends[k - 1]
            _remote(g_ref, slots.at[4 * px + 2 * py + pc], send_sems.at[k - 1], recv_sems.at[k - 1],
                    (px, py, pc)).wait_recv()
        for cp, _ in sends:
            cp.wait_send()
        total = slots[0]
        for k in range(1, NDEV):
            total = total + slots[k]
        go_ref[...] = total
        d, mn, vn = _adamw_math(w_ref[...], total, m_ref[...], v_ref[...])
        d_ref[...] = d
        mo_ref[...] = mn
        vo_ref[...] = vn

    sd = jax.ShapeDtypeStruct((SMALL_ROWS, LANES), F32)
    vm = pl.BlockSpec(memory_space=pltpu.VMEM)
    return pl.pallas_call(
        body, name="small_allreduce_adamw", in_specs=[vm] * 4, out_specs=[vm] * 4, out_shape=[sd] * 4,
        scratch_shapes=[pltpu.VMEM((NDEV, SMALL_ROWS, LANES), F32), pltpu.SemaphoreType.DMA((NDEV - 1,)),
                        pltpu.SemaphoreType.DMA((NDEV - 1,))],
        compiler_params=pltpu.CompilerParams(has_side_effects=True),
    )(gpack, wpack, mpack, vpack)


def kernel(x, norm1_g, w_in, sgu_ln_g, sgu_ln_b, sgu_w, sgu_b, attn_out_g, gmlp_out_g, w_out, norm2_g, w_ff1, w_ff2, final_norm_g, loss_target, m_norm1_g, m_w_in, m_sgu_ln_g, m_sgu_ln_b, m_sgu_w, m_sgu_b, m_attn_out_g, m_gmlp_out_g, m_w_out, m_norm2_g, m_w_ff1, m_w_ff2, m_final_norm_g, v_norm1_g, v_w_in, v_sgu_ln_g, v_sgu_ln_b, v_sgu_w, v_sgu_b, v_attn_out_g, v_gmlp_out_g, v_w_out, v_norm2_g, v_w_ff1, v_w_ff2, v_final_norm_g):
    names = [n for n, _ in SMALL_SIZES]
    w_small = dict(norm1_g=norm1_g, sgu_ln_g=sgu_ln_g, sgu_ln_b=sgu_ln_b, sgu_w=sgu_w, sgu_b=sgu_b,
                   attn_out_g=attn_out_g, gmlp_out_g=gmlp_out_g, norm2_g=norm2_g, final_norm_g=final_norm_g)
    m_small = dict(norm1_g=m_norm1_g, sgu_ln_g=m_sgu_ln_g, sgu_ln_b=m_sgu_ln_b, sgu_w=m_sgu_w, sgu_b=m_sgu_b,
                   attn_out_g=m_attn_out_g, gmlp_out_g=m_gmlp_out_g, norm2_g=m_norm2_g,
                   final_norm_g=m_final_norm_g)
    v_small = dict(norm1_g=v_norm1_g, sgu_ln_g=v_sgu_ln_g, sgu_ln_b=v_sgu_ln_b, sgu_w=v_sgu_w, sgu_b=v_sgu_b,
                   attn_out_g=v_attn_out_g, gmlp_out_g=v_gmlp_out_g, norm2_g=v_norm2_g,
                   final_norm_g=v_final_norm_g)
    shapes = {n: w_small[n].shape for n in names}

    r_in, r_out, r_ff = INW // NCHIP, D // NCHIP, DFF // NCHIP
    packed = jnp.concatenate([w_in[0].T, w_out[0], w_ff1[0].T, w_ff2[0]], axis=0).astype(BF16)
    wall = _gather_weights(packed)
    o1, o2, o3 = r_in, r_in + r_out, r_in + r_out + r_ff
    win_t = wall[:, :o1].reshape(INW, D)
    wout = wall[:, o1:o2].reshape(D, D)
    wff1_t = wall[:, o2:o3].reshape(DFF, D)
    wff2 = wall[:, o3:].reshape(DFF, D)

    small = dict(
        norm1_g=norm1_g, ln_g=sgu_ln_g.reshape(1, GW), ln_b=sgu_ln_b.reshape(1, GW), sgu_w=sgu_w[0],
        sgu_wt=jnp.swapaxes(sgu_w[0], 1, 2), bias_t=jnp.repeat(sgu_b[0].T, DH, axis=1),
        attn_out_g=attn_out_g, gmlp_out_g=gmlp_out_g, norm2_g=norm2_g, final_norm_g=final_norm_g.reshape(1, D))
    loss_part, dx, sg, (gwin_t, gwout, gwff1_t, gwff2) = _local_step(
        x[0], loss_target[0], small, win_t, wout, wff1_t, wff2)
    loss = lax.psum(loss_part, ("x", "y", "c"))

    gall = jnp.concatenate([gwin_t.reshape(NCHIP, r_in, D), gwout.reshape(NCHIP, r_out, D),
                            gwff1_t.reshape(NCHIP, r_ff, D), gwff2.reshape(NCHIP, r_ff, D)], axis=1)
    gsum = _reduce_grads(gall)
    g_big = dict(w_in=gsum[:o1].T, w_out=gsum[o1:o2], w_ff1=gsum[o2:o3].T, w_ff2=gsum[o3:])
    w_big = dict(w_in=(w_in, m_w_in, v_w_in), w_out=(w_out, m_w_out, v_w_out),
                 w_ff1=(w_ff1, m_w_ff1, v_w_ff1), w_ff2=(w_ff2, m_w_ff2, v_w_ff2))
    grads, deltas, new_m, new_v = {}, {}, {}, {}
    for n, (w, m, v) in w_big.items():
        d, mn, vn = _adamw(w[0], g_big[n], m[0], v[0], "adamw_" + n)
        grads[n], deltas[n], new_m[n], new_v[n] = g_big[n][None], d[None], mn[None], vn[None]

    g_small = dict(norm1_g=sg["norm1_g"], sgu_ln_g=sg["ln_g"], sgu_ln_b=sg["ln_b"], sgu_w=sg["sgu_w"],
                   sgu_b=sg["sgu_b"], attn_out_g=sg["attn_out_g"], gmlp_out_g=sg["gmlp_out_g"],
                   norm2_g=sg["norm2_g"], final_norm_g=sg["final_norm_g"])
    packs = _small_allreduce_adamw(_pack_small(g_small), _pack_small(w_small), _pack_small(m_small),
                                   _pack_small(v_small))
    for tree, pack in zip((grads, deltas, new_m, new_v), packs):
        tree.update(_unpack_small(pack, shapes))

    order = ["norm1_g", "w_in", "sgu_ln_g", "sgu_ln_b", "sgu_w", "sgu_b", "attn_out_g", "gmlp_out_g", "w_out",
             "norm2_g", "w_ff1", "w_ff2", "final_norm_g"]
    return (loss, dx[None], *[grads[n] for n in order], *[deltas[n] for n in order],
            *[new_m[n] for n in order], *[new_v[n] for n in order])
```

```python
import functools
import math

import numpy as np
import jax
import jax.numpy as jnp
from jax import lax
from jax.experimental import pallas as pl
from jax.experimental.pallas import tpu as pltpu

F32 = jnp.float32
BF16 = jnp.bfloat16

D = 1024
NH = 12
DH = 64
A = NH * DH
NG = 4
GW = NG * DH
INW = 3 * A + 2 * GW
DFF = 4 * D
CHUNK = 128
PATTERNS = ((128, 1), (512, 4), (2048, 16))
EPS = 1e-6
SCALE = DH ** -0.5
NEG = -1e30

LR, B1, B2, AEPS, WD, STEP = 0.001, 0.9, 0.999, 1e-08, 0.01, 10

TM = 512
TMX = 256
ATT_ROWS = 1024
FF_CH = 1024
LANES = 128
NCHIP = 4
SHARD_ROWS = INW // NCHIP + D // NCHIP + DFF // NCHIP + DFF // NCHIP
VMEM_LIMIT = 56 * 1024 * 1024
MESH = pl.DeviceIdType.MESH


def _cparams(*sem, **kw):
    return pltpu.CompilerParams(dimension_semantics=sem if sem else None,
                                vmem_limit_bytes=VMEM_LIMIT, **kw)


def _dot(a, b):
    return jnp.dot(a, b, preferred_element_type=F32)


def _dot_nt(a, b):
    return lax.dot_general(a, b, (((1,), (1,)), ((), ())), preferred_element_type=F32)


def _dot_tn(a, b):
    return lax.dot_general(a, b, (((0,), (0,)), ((), ())), preferred_element_type=F32)


def _dot_hi(a, b):
    return jnp.dot(a, b, preferred_element_type=F32, precision=lax.Precision.HIGHEST)


def _alibi_slopes(n):
    def pow2(m):
        start = 2.0 ** (-8.0 / m)
        return [start ** (i + 1) for i in range(m)]
    if math.log2(n).is_integer():
        s = pow2(n)
    else:
        c = 2 ** int(math.floor(math.log2(n)))
        s = pow2(c) + pow2(2 * c)[0::2][: n - c]
    return np.asarray(s, dtype=np.float32)


def _rms_fwd(v, g):
    r = lax.rsqrt(jnp.mean(v * v, axis=-1, keepdims=True) + EPS)
    vn = v * r
    return vn * g, vn, r


def _rms_bwd(dy, vn, r, g):
    w = dy * g
    dv = r * (w - vn * jnp.mean(w * vn, axis=-1, keepdims=True))
    return dv, jnp.sum(dy * vn, axis=0, keepdims=True)


_K0 = math.sqrt(2.0 / math.pi)
_K1 = 0.044715


def _gelu(v):
    return 0.5 * v * (1.0 + jnp.tanh(_K0 * (v + _K1 * (v * v * v))))


def _gelu_grad(v):
    t = jnp.tanh(_K0 * (v + _K1 * (v * v * v)))
    return 0.5 * (1.0 + t) + 0.5 * v * (1.0 - t * t) * (_K0 * (1.0 + 3.0 * _K1 * v * v))


def _row_spec(rows, cols):
    return pl.BlockSpec((rows, cols), lambda i: (i, 0))


def _const_spec(shape):
    nd = len(shape)
    return pl.BlockSpec(shape, lambda i: (0,) * nd, pipeline_mode=pl.Buffered(1))


DILS = tuple(d for _, d in PATTERNS)


def _fill_cols(scr, value):
    for cb in range(value.shape[1] // LANES):
        scr[cb] = value[:, cb * LANES:(cb + 1) * LANES]


def _split_residues(scr, out_ref, dil):
    nb, rows, _ = scr.shape
    for r in range(dil):
        for cb in range(nb):
            piece = scr.at[cb][pl.ds(r, rows // dil, stride=dil), :]
            out_ref[r, :, cb * LANES:(cb + 1) * LANES] = piece.astype(out_ref.dtype)


def _merge_residues(in_ref, scr, dil):
    nb, rows, _ = scr.shape
    for r in range(dil):
        for cb in range(nb):
            scr.at[cb][pl.ds(r, rows // dil, stride=dil), :] = in_ref[r, :, cb * LANES:(cb + 1) * LANES]
    return jnp.concatenate([scr[cb] for cb in range(nb)], axis=-1)


def _col_scratch(rows, width):
    return pltpu.VMEM((width // LANES, rows, LANES), F32)


def _res_spec(dil, rows, width):
    return pl.BlockSpec((dil, rows // dil, width), lambda i: (0, i, 0))


def _res_shape(s, dil, width, dtype):
    return jax.ShapeDtypeStruct((dil, s // dil, width), dtype)


def _inproj_fwd(x, g1, win_t):
    s = x.shape[0]
    nd = len(DILS)

    def body(x_ref, g_ref, w_ref, hn_ref, *rest):
        qkv_refs = rest[:3 * nd]
        u_ref, z_ref, scr = rest[3 * nd:]
        hn, _, _ = _rms_fwd(x_ref[...], g_ref[...])
        hn = hn.astype(BF16)
        hn_ref[...] = hn
        for t in range(3):
            seg = _dot_nt(hn, w_ref[t * A:(t + 1) * A, :])
            seg = seg * SCALE if t == 0 else seg
            _fill_cols(scr, seg)
            for di, dil in enumerate(DILS):
                if dil == 1:
                    qkv_refs[t * nd + di][0] = seg.astype(BF16)
                else:
                    _split_residues(scr, qkv_refs[t * nd + di], dil)
        u_ref[...] = _dot_nt(hn, w_ref[3 * A:3 * A + GW, :])
        z_ref[...] = _dot_nt(hn, w_ref[3 * A + GW:INW, :])

    res = pl.pallas_call(
        body, name="inproj_fwd", grid=(s // TM,),
        in_specs=[_row_spec(TM, D), _const_spec((1, D)), _const_spec((INW, D))],
        out_specs=[_row_spec(TM, D)] + [_res_spec(d, TM, A) for _ in range(3) for d in DILS]
                  + [_row_spec(TM, GW), _row_spec(TM, GW)],
        out_shape=[jax.ShapeDtypeStruct((s, D), BF16)] + [_res_shape(s, d, A, BF16) for _ in range(3) for d in DILS]
                  + [jax.ShapeDtypeStruct((s, GW), F32)] * 2,
        scratch_shapes=[_col_scratch(TM, A)],
        compiler_params=_cparams("arbitrary"),
    )(x, g1, win_t)
    hn1 = res[0]
    q, k, v = (res[1 + t * nd:1 + (t + 1) * nd] for t in range(3))
    return hn1, q, k, v, res[-2], res[-1]


def _att_geometry(s, dil):
    length = s // dil
    rows = min(length, ATT_ROWS)
    return length, rows, length // rows, rows // CHUNK


def _stack_heads(t):
    lane = lax.broadcasted_iota(jnp.int32, t.shape, 1)
    zero = jnp.zeros_like(t)
    return jnp.concatenate([jnp.where(lane < DH, t, zero), jnp.where(lane >= DH, t, zero)], axis=0)


def _stack_cols(t):
    return jnp.concatenate([t[:, 0:1], t[:, DH:DH + 1]], axis=0)


def _unstack_heads(t2):
    n = t2.shape[0] // 2
    lane = lax.broadcasted_iota(jnp.int32, (n, LANES), 1)
    return jnp.where(lane < DH, t2[:n], t2[n:])


def _query_window_bias(s0, s1, dil, first):
    row = lax.broadcasted_iota(jnp.int32, (2 * CHUNK, 2 * CHUNK), 0)
    col = lax.broadcasted_iota(jnp.int32, (2 * CHUNK, 2 * CHUNK), 1)
    steps = (row & (CHUNK - 1)) + CHUNK - col
    valid = (steps >= 0) & (steps <= CHUNK)
    if first:
        valid = valid & (col >= CHUNK)
    slope = jnp.where(row < CHUNK, s0, s1)
    return jnp.where(valid, -slope * (steps * dil).astype(F32), NEG)


def _key_block_bias(s0, s1, dil, last):
    key = lax.broadcasted_iota(jnp.int32, (CHUNK, 4 * CHUNK), 0)
    col = lax.broadcasted_iota(jnp.int32, (CHUNK, 4 * CHUNK), 1)
    wq = col & (2 * CHUNK - 1)
    steps = wq - key
    valid = (steps >= 0) & (steps <= CHUNK)
    if last:
        valid = valid & (wq < CHUNK)
    slope = jnp.where(col < 2 * CHUNK, s0, s1)
    return jnp.where(valid, -slope * (steps * dil).astype(F32), NEG)


def _head_rows(t):
    row = lax.broadcasted_iota(jnp.int32, (8, LANES), 0)
    lane = lax.broadcasted_iota(jnp.int32, (8, LANES), 1)
    pick = jnp.where(((row == 0) & (lane == 0)) | ((row == 1) & (lane == DH)), 1.0, 0.0).astype(BF16)
    hi = t.astype(BF16)
    rest = t - hi.astype(F32)
    mid = rest.astype(BF16)
    low = (rest - mid.astype(F32)).astype(BF16)
    return _dot_nt(pick, hi) + _dot_nt(pick, mid) + _dot_nt(pick, low)


def _att_specs(dil, rows, nsub, nblk):
    main = pl.BlockSpec((None, rows, LANES), lambda r, hp, c: (r, c, hp))
    prev = pl.BlockSpec((None, CHUNK, LANES), lambda r, hp, c: (r, jnp.maximum(c * nsub - 1, 0), hp))
    nxt = pl.BlockSpec((None, CHUNK, LANES), lambda r, hp, c: (r, jnp.minimum((c + 1) * nsub, nblk - 1), hp))
    return main, prev, nxt


def _row_start(i):
    return i * CHUNK if isinstance(i, int) else pl.multiple_of(i * CHUNK, CHUNK)


def _attn_fwd(q, k, v, slopes, dil):
    length = q.shape[1]
    _, rows, nch, nsub = _att_geometry(length * dil, dil)
    main, prev, _ = _att_specs(dil, rows, nsub, length // CHUNK)

    def body(sl_ref, q_ref, k_ref, v_ref, kh_ref, vh_ref, o_ref, lse_ref, kbuf, vbuf, bias_buf):
        hp = pl.program_id(1)
        ch = pl.program_id(2)
        kbuf[0:CHUNK, :] = kh_ref[...]
        kbuf[CHUNK:, :] = k_ref[...]
        vbuf[0:CHUNK, :] = vh_ref[...]
        vbuf[CHUNK:, :] = v_ref[...]
        s0, s1 = sl_ref[2 * hp], sl_ref[2 * hp + 1]

        def block(i, bias):
            row = _row_start(i)
            rs = pl.ds(row, CHUNK)
            q2 = _stack_heads(q_ref[rs, :])
            kw = kbuf[pl.ds(row, 2 * CHUNK), :]
            vw = vbuf[pl.ds(row, 2 * CHUNK), :]
            sc = _dot_nt(q2, kw) + bias
            m = jnp.max(sc, axis=-1, keepdims=True)
            p = jnp.exp(sc - m)
            l = jnp.sum(p, axis=-1, keepdims=True)
            o2 = _dot(p.astype(BF16), vw) * (1.0 / l)
            o_ref[rs, :] = _unstack_heads(o2)
            lse_ref[rs, :] = _unstack_heads(jnp.broadcast_to(m + jnp.log(l), (2 * CHUNK, LANES)))

        bias_buf[...] = _query_window_bias(s0, s1, dil, False)

        @pl.when(ch == 0)
        def _():
            block(0, _query_window_bias(s0, s1, dil, True))

        @pl.when(ch != 0)
        def _():
            block(0, bias_buf[...])

        for i in range(1, nsub):
            block(i, bias_buf[...])

    sd = jax.ShapeDtypeStruct((dil, length, A), F32)
    return pl.pallas_call(
        body, name=f"attn_fwd_d{dil}", grid=(dil, NH // 2, nch),
        in_specs=[pl.BlockSpec(memory_space=pltpu.SMEM), main, main, main, prev, prev],
        out_specs=[main, main], out_shape=[sd, sd],
        scratch_shapes=[pltpu.VMEM((rows + CHUNK, LANES), BF16), pltpu.VMEM((rows + CHUNK, LANES), BF16),
                        pltpu.VMEM((2 * CHUNK, 2 * CHUNK), F32)],
        compiler_params=_cparams("arbitrary", "arbitrary", "arbitrary"),
    )(slopes, q, k, v, k, v)


def _attn_bwd_dq(q, k, v, do, lse, delta, slopes, dil):
    length = q.shape[1]
    _, rows, nch, nsub = _att_geometry(length * dil, dil)
    main, prev, _ = _att_specs(dil, rows, nsub, length // CHUNK)

    def body(sl_ref, q_ref, k_ref, v_ref, do_ref, lse_ref, dl_ref, kh_ref, vh_ref, dq_ref, kbuf, vbuf, bias_buf):
        hp = pl.program_id(1)
        ch = pl.program_id(2)
        kbuf[0:CHUNK, :] = kh_ref[...]
        kbuf[CHUNK:, :] = k_ref[...]
        vbuf[0:CHUNK, :] = vh_ref[...]
        vbuf[CHUNK:, :] = v_ref[...]
        s0, s1 = sl_ref[2 * hp], sl_ref[2 * hp + 1]

        def block(i, bias):
            row = _row_start(i)
            rs = pl.ds(row, CHUNK)
            q2 = _stack_heads(q_ref[rs, :])
            do2 = _stack_heads(do_ref[rs, :])
            lse2 = _stack_cols(lse_ref[rs, :])
            dl2 = _stack_cols(dl_ref[rs, :])
            kw = kbuf[pl.ds(row, 2 * CHUNK), :]
            vw = vbuf[pl.ds(row, 2 * CHUNK), :]
            p = jnp.exp(_dot_nt(q2, kw) + bias - lse2)
            ds = p * (_dot_nt(do2, vw) - dl2)
            dq_ref[rs, :] = _unstack_heads(_dot(ds.astype(BF16), kw))

        bias_buf[...] = _query_window_bias(s0, s1, dil, False)

        @pl.when(ch == 0)
        def _():
            block(0, _query_window_bias(s0, s1, dil, True))

        @pl.when(ch != 0)
        def _():
            block(0, bias_buf[...])

        for i in range(1, nsub):
            block(i, bias_buf[...])

    return pl.pallas_call(
        body, name=f"attn_dq_d{dil}", grid=(dil, NH // 2, nch),
        in_specs=[pl.BlockSpec(memory_space=pltpu.SMEM), main, main, main, main, main, main, prev, prev],
        out_specs=main, out_shape=jax.ShapeDtypeStruct((dil, length, A), F32),
        scratch_shapes=[pltpu.VMEM((rows + CHUNK, LANES), BF16), pltpu.VMEM((rows + CHUNK, LANES), BF16),
                        pltpu.VMEM((2 * CHUNK, 2 * CHUNK), F32)],
        compiler_params=_cparams("arbitrary", "arbitrary", "arbitrary"),
    )(slopes, q, k, v, do, lse, delta, k, v)


def _attn_bwd_dkv(q, k, v, do, lse, delta, slopes, dil):
    length = q.shape[1]
    _, rows, nch, nsub = _att_geometry(length * dil, dil)
    main, _, nxt = _att_specs(dil, rows, nsub, length // CHUNK)

    def body(sl_ref, k_ref, v_ref, q_ref, do_ref, lse_ref, dl_ref, qh_ref, doh_ref, lseh_ref, dlh_ref,
             dk_ref, dv_ref, qbuf, dobuf, lse_rows, dl_rows, bias_buf):
        hp = pl.program_id(1)
        ch = pl.program_id(2)
        for buf, main_ref, halo_ref in ((qbuf, q_ref, qh_ref), (dobuf, do_ref, doh_ref)):
            buf[0:rows, :] = main_ref[...]
            buf[rows:, :] = halo_ref[...]
        for buf, main_ref, halo_ref in ((lse_rows, lse_ref, lseh_ref), (dl_rows, dl_ref, dlh_ref)):
            buf[:, 0:rows] = _head_rows(main_ref[...])
            buf[:, rows:] = _head_rows(halo_ref[...])
        s0, s1 = sl_ref[2 * hp], sl_ref[2 * hp + 1]

        def block(i, bias):
            row = _row_start(i)
            rs = pl.ds(row, CHUNK)
            win = pl.ds(row, 2 * CHUNK)
            kc = k_ref[rs, :]
            vc = v_ref[rs, :]
            q2 = _stack_heads(qbuf[win, :])
            do2 = _stack_heads(dobuf[win, :])
            cols = slice(i * CHUNK, (i + 2) * CHUNK)
            lse2 = jnp.concatenate([lse_rows[0:1, cols], lse_rows[1:2, cols]], axis=1)
            dl2 = jnp.concatenate([dl_rows[0:1, cols], dl_rows[1:2, cols]], axis=1)
            pt = jnp.exp(_dot_nt(kc, q2) + bias - lse2)
            dst = pt * (_dot_nt(vc, do2) - dl2)
            dv_ref[rs, :] = _dot(pt.astype(BF16), do2)
            dk_ref[rs, :] = _dot(dst.astype(BF16), q2)

        bias_buf[...] = _key_block_bias(s0, s1, dil, False)

        for i in range(nsub - 1):
            block(i, bias_buf[...])

        @pl.when(ch != nch - 1)
        def _():
            block(nsub - 1, bias_buf[...])

        @pl.when(ch == nch - 1)
        def _():
            block(nsub - 1, _key_block_bias(s0, s1, dil, True))

    sd = jax.ShapeDtypeStruct((dil, length, A), F32)
    return pl.pallas_call(
        body, name=f"attn_dkv_d{dil}", grid=(dil, NH // 2, nch),
        in_specs=[pl.BlockSpec(memory_space=pltpu.SMEM), main, main, main, main, main, main, nxt, nxt, nxt, nxt],
        out_specs=[main, main], out_shape=[sd, sd],
        scratch_shapes=[pltpu.VMEM((rows + CHUNK, LANES), BF16), pltpu.VMEM((rows + CHUNK, LANES), BF16),
                        pltpu.VMEM((8, rows + CHUNK), F32), pltpu.VMEM((8, rows + CHUNK), F32),
                        pltpu.VMEM((CHUNK, 4 * CHUNK), F32)],
        compiler_params=_cparams("arbitrary", "arbitrary", "arbitrary"),
    )(slopes, k, v, q, do, lse, delta, q, do, lse, delta)


def _group_masks(width):
    lane = lax.broadcasted_iota(jnp.int32, (1, width), 1)
    return [(lane >= g * DH) & (lane < (g + 1) * DH) for g in range(width // DH)]


def _group_mean_matrix():
    i = lax.broadcasted_iota(jnp.int32, (GW, GW), 0) // DH
    j = lax.broadcasted_iota(jnp.int32, (GW, GW), 1) // DH
    return jnp.where(i == j, 1.0 / DH, 0.0).astype(F32)


def _tri_mask(lower):
    t = lax.broadcasted_iota(jnp.int32, (CHUNK, CHUNK), 0)
    u = lax.broadcasted_iota(jnp.int32, (CHUNK, CHUNK), 1)
    return (u <= t) if lower else (u >= t)


def _sgu_forward(u, z, lng, lnb, w_ref, bias_t, pmat, rows):
    ug = _gelu(u)
    zg = _gelu(z)
    mu = _dot_hi(zg, pmat)
    zc = zg - mu
    var = _dot_hi(zc * zc, pmat)
    rstd = lax.rsqrt(var + EPS)
    zhat = zc * rstd
    zn = (zhat * lng + lnb).astype(BF16)
    gm = _group_masks(GW)
    tri = _tri_mask(True)
    ws = [jnp.where(tri, w_ref[g], 0.0).astype(BF16) for g in range(NG)]
    pieces = []
    for c in range(rows // CHUNK):
        znc = zn[c * CHUNK:(c + 1) * CHUNK, :]
        mix = None
        for g in range(NG):
            part = jnp.where(gm[g], _dot(ws[g], znc), 0.0)
            mix = part if mix is None else mix + part
        pieces.append(mix + bias_t)
    mixed = jnp.concatenate(pieces, axis=0) if len(pieces) > 1 else pieces[0]
    return ug * mixed, ug, zhat, rstd, zn, mixed


def _mix_fwd(os_, ls_, u, z, x, lng, lnb, sgu_w, bias_t, ga, gg, wout):
    s = x.shape[0]
    nd = len(DILS)
    nscr = sum(1 for d in DILS if d > 1)

    def body(*refs):
        o_refs, l_refs = refs[:nd], refs[nd:2 * nd]
        u_ref, z_ref, x_ref, lng_ref, lnb_ref, w_ref, bt_ref, ga_ref, gg_ref, wo_ref = refs[2 * nd:2 * nd + 10]
        attn_ref = refs[2 * nd + 10]
        lse_refs = refs[2 * nd + 11:3 * nd + 11]
        mixed_ref, h1_ref = refs[3 * nd + 11:3 * nd + 13]
        scr = refs[3 * nd + 13:]
        scr_o, scr_l, scr_lse = scr[:nscr], scr[nscr:2 * nscr], scr[2 * nscr]
        ov, lv, j = [], [], 0
        for di, dil in enumerate(DILS):
            if dil == 1:
                ov.append(o_refs[di][0])
                lv.append(l_refs[di][0])
            else:
                ov.append(_merge_residues(o_refs[di], scr_o[j], dil))
                lv.append(_merge_residues(l_refs[di], scr_l[j], dil))
                j += 1
        mx = functools.reduce(jnp.maximum, lv)
        es = [jnp.exp(l - mx) for l in lv]
        den = functools.reduce(lambda a, b: a + b, es)
        attn = functools.reduce(lambda a, b: a + b, [e * o for e, o in zip(es, ov)]) / den
        attn_ref[...] = attn
        lse = mx + jnp.log(den)
        _fill_cols(scr_lse, lse)
        for di, dil in enumerate(DILS):
            if dil == 1:
                lse_refs[di][0] = lse
            else:
                _split_residues(scr_lse, lse_refs[di], dil)
        an, _, _ = _rms_fwd(attn, ga_ref[...])
        gmv, _, _, _, _, _ = _sgu_forward(u_ref[...], z_ref[...], lng_ref[...], lnb_ref[...], w_ref,
                                          bt_ref[...], _group_mean_matrix(), TMX)
        gn, _, _ = _rms_fwd(gmv, gg_ref[...])
        mixed = jnp.concatenate([an, gn], axis=-1).astype(BF16)
        mixed_ref[...] = mixed
        h1_ref[...] = x_ref[...] + _dot(mixed, wo_ref[...])

    sd = jax.ShapeDtypeStruct
    res = pl.pallas_call(
        body, name="mix_fwd", grid=(s // TMX,),
        in_specs=[_res_spec(d, TMX, A) for d in DILS] * 2 + [_row_spec(TMX, GW), _row_spec(TMX, GW),
                  _row_spec(TMX, D), _const_spec((1, GW)), _const_spec((1, GW)), _const_spec((NG, CHUNK, CHUNK)),
                  _const_spec((CHUNK, GW)), _const_spec((1, A)), _const_spec((1, GW)), _const_spec((D, D))],
        out_specs=[_row_spec(TMX, A)] + [_res_spec(d, TMX, A) for d in DILS] + [_row_spec(TMX, D), _row_spec(TMX, D)],
        out_shape=[sd((s, A), F32)] + [_res_shape(s, d, A, F32) for d in DILS] + [sd((s, D), BF16), sd((s, D), F32)],
        scratch_shapes=[_col_scratch(TMX, A)] * (2 * nscr + 1),
        compiler_params=_cparams("arbitrary"),
    )(*os_, *ls_, u, z, x, lng, lnb, sgu_w, bias_t, ga, gg, wout)
    return res[0], res[1:1 + nd], res[1 + nd], res[2 + nd]


def _mlp_fwd(h1, g2, wff1_t, wff2, gf, target):
    s = h1.shape[0]

    def body(h1_ref, g2_ref, w1_ref, w2_ref, gf_ref, t_ref, hn_ref, rf_ref, a_ref, dh2_ref, loss_ref, dgf_ref):
        i = pl.program_id(0)
        h1v = h1_ref[...]
        hn, _, _ = _rms_fwd(h1v, g2_ref[...])
        hn = hn.astype(BF16)
        hn_ref[...] = hn
        acc = h1v
        for j in range(DFF // FF_CH):
            cols = slice(j * FF_CH, (j + 1) * FF_CH)
            rf = jnp.maximum(_dot_nt(hn, w1_ref[cols, :]), 0.0)
            act = (rf * rf).astype(BF16)
            rf_ref[:, cols] = rf.astype(BF16)
            a_ref[:, cols] = act
            acc = acc + _dot(act, w2_ref[cols, :])
        y, h2n, r3 = _rms_fwd(acc, gf_ref[...])
        err = y - t_ref[...]
        part = 0.5 * jnp.sum(jnp.mean(err * err, axis=-1, keepdims=True), axis=0, keepdims=True)
        dy = err * (1.0 / D)
        dh2, dgf = _rms_bwd(dy, h2n, r3, gf_ref[...])
        dh2_ref[...] = dh2

        @pl.when(i == 0)
        def _():
            loss_ref[...] = jnp.zeros_like(loss_ref)
            dgf_ref[...] = jnp.zeros_like(dgf_ref)

        loss_ref[...] += jnp.broadcast_to(part, loss_ref.shape)
        dgf_ref[...] += dgf

    sd = jax.ShapeDtypeStruct
    return pl.pallas_call(
        body, name="mlp_fwd", grid=(s // TM,),
        in_specs=[_row_spec(TM, D), _const_spec((1, D)), _const_spec((DFF, D)), _const_spec((DFF, D)),
                  _const_spec((1, D)), _row_spec(TM, D)],
        out_specs=[_row_spec(TM, D), _row_spec(TM, DFF), _row_spec(TM, DFF), _row_spec(TM, D),
                   _const_spec((1, LANES)), _const_spec((1, D))],
        out_shape=[sd((s, D), BF16), sd((s, DFF), BF16), sd((s, DFF), BF16), sd((s, D), F32),
                   sd((1, LANES), F32), sd((1, D), F32)],
        compiler_params=_cparams("arbitrary"),
    )(h1, g2, wff1_t, wff2, gf, target)


def _mlp_bwd(dh2, rf, h1, g2, wff1_t, wff2):
    s = h1.shape[0]

    def body(dh2_ref, rf_ref, h1_ref, g2_ref, w1_ref, w2_ref, df_ref, dh1_ref, dg2_ref):
        i = pl.program_id(0)
        dh2v = dh2_ref[...]
        dh2b = dh2v.astype(BF16)
        dhn = jnp.zeros((TM, D), F32)
        for j in range(DFF // FF_CH):
            cols = slice(j * FF_CH, (j + 1) * FF_CH)
            da = _dot_nt(dh2b, w2_ref[cols, :])
            df = (da * (2.0 * rf_ref[:, cols].astype(F32))).astype(BF16)
            df_ref[:, cols] = df
            dhn = dhn + _dot(df, w1_ref[cols, :])
        _, h1n, r2 = _rms_fwd(h1_ref[...], g2_ref[...])
        dres, dg2 = _rms_bwd(dhn, h1n, r2, g2_ref[...])
        dh1_ref[...] = dh2v + dres

        @pl.when(i == 0)
        def _():
            dg2_ref[...] = jnp.zeros_like(dg2_ref)

        dg2_ref[...] += dg2

    sd = jax.ShapeDtypeStruct
    return pl.pallas_call(
        body, name="mlp_bwd", grid=(s // TM,),
        in_specs=[_row_spec(TM, D), _row_spec(TM, DFF), _row_spec(TM, D), _const_spec((1, D)),
                  _const_spec((DFF, D)), _const_spec((DFF, D))],
        out_specs=[_row_spec(TM, DFF), _row_spec(TM, D), _const_spec((1, D))],
        out_shape=[sd((s, DFF), BF16), sd((s, D), F32), sd((1, D), F32)],
        compiler_params=_cparams("arbitrary"),
    )(dh2, rf, h1, g2, wff1_t, wff2)


def _mix_bwd(dh1, attn, u, z, lng, lnb, sgu_w, sgu_wt, bias_t, ga, gg, wout):
    s = dh1.shape[0]
    nsteps = s // TMX
    nd = len(DILS)

    def body(*refs):
        dh1_ref, attn_ref, u_ref, z_ref, lng_ref, lnb_ref, w_ref, wt_ref, bt_ref, ga_ref, gg_ref, wo_ref = refs[:12]
        do_refs, dl_refs = refs[12:12 + nd], refs[12 + nd:12 + 2 * nd]
        (du_ref, dz_ref, dga_ref, dgg_ref, dlng_ref, dlnb_ref, dws_ref, db_ref,
         dbt_acc, scr_do, scr_dl) = refs[12 + 2 * nd:]
        i = pl.program_id(0)

        @pl.when(i == 0)
        def _():
            for r in (dga_ref, dgg_ref, dlng_ref, dlnb_ref, dws_ref, db_ref, dbt_acc):
                r[...] = jnp.zeros_like(r)

        dmixed = _dot_nt(dh1_ref[...].astype(BF16), wo_ref[...])
        attn = attn_ref[...]
        _, an, ra = _rms_fwd(attn, ga_ref[...])
        dattn, dga = _rms_bwd(dmixed[:, :A], an, ra, ga_ref[...])
        dga_ref[...] += dga
        _fill_cols(scr_do, dattn)
        prod = dattn * attn
        delta = jnp.zeros_like(prod)
        for hm in _group_masks(A):
            delta = delta + jnp.where(hm, jnp.sum(jnp.where(hm, prod, 0.0), axis=-1, keepdims=True), 0.0)
        _fill_cols(scr_dl, delta)
        for di, dil in enumerate(DILS):
            if dil == 1:
                do_refs[di][0] = dattn.astype(BF16)
                dl_refs[di][0] = delta
            else:
                _split_residues(scr_do, do_refs[di], dil)
                _split_residues(scr_dl, dl_refs[di], dil)
        pmat = _group_mean_matrix()
        lng = lng_ref[...]
        uv, zv = u_ref[...], z_ref[...]
        gmv, ug, zhat, rstd, zn, mixed = _sgu_forward(uv, zv, lng, lnb_ref[...], w_ref, bt_ref[...], pmat, TMX)
        _, gmn, rg = _rms_fwd(gmv, gg_ref[...])
        dgm, dgg = _rms_bwd(dmixed[:, A:], gmn, rg, gg_ref[...])
        dgg_ref[...] += dgg
        du_ref[...] = dgm * mixed * _gelu_grad(uv)
        dmx = dgm * ug
        dmxb = dmx.astype(BF16)
        gm = _group_masks(GW)
        tri_t = _tri_mask(False)
        wst = [jnp.where(tri_t, wt_ref[g], 0.0).astype(BF16) for g in range(NG)]
        zero = jnp.zeros((CHUNK, GW), BF16)
        dzn_pieces = []
        for c in range(TMX // CHUNK):
            rs = slice(c * CHUNK, (c + 1) * CHUNK)
            dmc = dmxb[rs, :]
            znc = zn[rs, :]
            dbt_acc[...] += dmx[rs, :]
            dzn = None
            for g in range(NG):
                dws_ref[g] += _dot_nt(jnp.where(gm[g], dmc, zero), znc)
                part = jnp.where(gm[g], _dot(wst[g], dmc), 0.0)
                dzn = part if dzn is None else dzn + part
            dzn_pieces.append(dzn)
        dzn = jnp.concatenate(dzn_pieces, axis=0)
        dlng_ref[...] += jnp.sum(dzn * zhat, axis=0, keepdims=True)
        dlnb_ref[...] += jnp.sum(dzn, axis=0, keepdims=True)
        dzh = dzn * lng
        dzg = rstd * (dzh - _dot_hi(dzh, pmat) - zhat * _dot_hi(dzh * zhat, pmat))
        dz_ref[...] = dzg * _gelu_grad(zv)

        @pl.when(i == nsteps - 1)
        def _():
            tri = _tri_mask(True)
            for g in range(NG):
                dws_ref[g] = jnp.where(tri, dws_ref[g], 0.0)
            acc = dbt_acc[...]
            lane = lax.broadcasted_iota(jnp.int32, (CHUNK, LANES), 1)
            out = jnp.zeros((CHUNK, LANES), F32)
            for g in range(NG):
                sg = jnp.sum(jnp.where(gm[g], acc, 0.0), axis=-1, keepdims=True)
                out = jnp.where(lane == g, sg, out)
            db_ref[...] = out

    sd = jax.ShapeDtypeStruct
    res = pl.pallas_call(
        body, name="mix_bwd", grid=(nsteps,),
        in_specs=[_row_spec(TMX, D), _row_spec(TMX, A), _row_spec(TMX, GW), _row_spec(TMX, GW),
                  _const_spec((1, GW)), _const_spec((1, GW)), _const_spec((NG, CHUNK, CHUNK)),
                  _const_spec((NG, CHUNK, CHUNK)), _const_spec((CHUNK, GW)), _const_spec((1, A)),
                  _const_spec((1, GW)), _const_spec((D, D))],
        out_specs=[_res_spec(d, TMX, A) for d in DILS] * 2 + [_row_spec(TMX, GW), _row_spec(TMX, GW),
                   _const_spec((1, A)), _const_spec((1, GW)), _const_spec((1, GW)), _const_spec((1, GW)),
                   _const_spec((NG, CHUNK, CHUNK)), _const_spec((CHUNK, LANES))],
        out_shape=[_res_shape(s, d, A, BF16) for d in DILS] + [_res_shape(s, d, A, F32) for d in DILS]
                  + [sd((s, GW), F32), sd((s, GW), F32),
                   sd((1, A), F32), sd((1, GW), F32), sd((1, GW), F32), sd((1, GW), F32),
                   sd((NG, CHUNK, CHUNK), F32), sd((CHUNK, LANES), F32)],
        scratch_shapes=[pltpu.VMEM((CHUNK, GW), F32), _col_scratch(TMX, A), _col_scratch(TMX, A)],
        compiler_params=_cparams("arbitrary"),
    )(dh1, attn, u, z, lng, lnb, sgu_w, sgu_wt, bias_t, ga, gg, wout)
    return (res[:nd], res[nd:2 * nd]) + tuple(res[2 * nd:])


def _inproj_bwd(dqs, dks, dvs, du, dz, dh1, x, g1, win_t):
    s = x.shape[0]
    nd = len(DILS)
    nscr = sum(1 for d in DILS if d > 1)

    def body(*refs):
        parts = [refs[t * nd:(t + 1) * nd] for t in range(3)]
        du_ref, dz_ref, dh1_ref, x_ref, g_ref, w_ref, dp_ref, dx_ref, dg_ref = refs[3 * nd:3 * nd + 9]
        scr = refs[3 * nd + 9:]
        i = pl.program_id(0)
        sums = []
        for t in range(3):
            total, j = None, 0
            for di, dil in enumerate(DILS):
                if dil == 1:
                    term = parts[t][di][0]
                else:
                    term = _merge_residues(parts[t][di], scr[t * nscr + j], dil)
                    j += 1
                total = term if total is None else total + term
            sums.append(total)
        dp = jnp.concatenate([sums[0] * SCALE, sums[1], sums[2], du_ref[...], dz_ref[...]], axis=-1).astype(BF16)
        dp_ref[...] = dp
        dhn = _dot(dp, w_ref[...])
        _, xn, r1 = _rms_fwd(x_ref[...], g_ref[...])
        dres, dg = _rms_bwd(dhn, xn, r1, g_ref[...])
        dx_ref[...] = dh1_ref[...] + dres

        @pl.when(i == 0)
        def _():
            dg_ref[...] = jnp.zeros_like(dg_ref)

        dg_ref[...] += dg

    sd = jax.ShapeDtypeStruct
    return pl.pallas_call(
        body, name="inproj_bwd", grid=(s // TMX,),
        in_specs=[_res_spec(d, TMX, A) for d in DILS] * 3 + [_row_spec(TMX, GW)] * 2
                 + [_row_spec(TMX, D), _row_spec(TMX, D), _const_spec((1, D)), _const_spec((INW, D))],
        out_specs=[_row_spec(TMX, INW), _row_spec(TMX, D), _const_spec((1, D))],
        out_shape=[sd((s, INW), BF16), sd((s, D), F32), sd((1, D), F32)],
        scratch_shapes=[_col_scratch(TMX, A)] * (3 * nscr),
        compiler_params=_cparams("arbitrary"),
    )(*dqs, *dks, *dvs, du, dz, dh1, x, g1, win_t)


def _wgrad(a, b, name, bm, bn, bk=TM):
    s, m = a.shape
    n = b.shape[1]
    bm, bn = min(bm, m), min(bn, n)

    def body(a_ref, b_ref, o_ref):
        @pl.when(pl.program_id(2) == 0)
        def _():
            o_ref[...] = jnp.zeros_like(o_ref)

        o_ref[...] += _dot_tn(a_ref[...].astype(BF16), b_ref[...].astype(BF16))

    return pl.pallas_call(
        body, name=name, grid=(m // bm, n // bn, s // bk),
        in_specs=[pl.BlockSpec((bk, bm), lambda i, j, k: (k, i)), pl.BlockSpec((bk, bn), lambda i, j, k: (k, j))],
        out_specs=pl.BlockSpec((bm, bn), lambda i, j, k: (i, j)),
        out_shape=jax.ShapeDtypeStruct((m, n), F32),
        compiler_params=_cparams("arbitrary", "arbitrary", "arbitrary"),
    )(a, b)


def _adamw_math(w, g, m, v):
    m = B1 * m + (1.0 - B1) * g
    v = B2 * v + (1.0 - B2) * (g * g)
    m_hat = m / (1.0 - B1 ** STEP)
    v_hat = v / (1.0 - B2 ** STEP)
    delta = -LR * (m_hat / (jnp.sqrt(v_hat) + AEPS) + WD * w)
    return delta, m, v


def _adamw(w, g, m, v, name):
    rows, cols = w.shape
    br = min(rows, 256)
    while rows % br:
        br -= 8

    def body(w_ref, g_ref, m_ref, v_ref, d_ref, mo_ref, vo_ref):
        d, mn, vn = _adamw_math(w_ref[...], g_ref[...], m_ref[...], v_ref[...])
        d_ref[...] = d
        mo_ref[...] = mn
        vo_ref[...] = vn

    spec = _row_spec(br, cols)
    sd = jax.ShapeDtypeStruct((rows, cols), F32)
    return pl.pallas_call(
        body, name=name, grid=(rows // br,), in_specs=[spec] * 4, out_specs=[spec] * 3,
        out_shape=[sd, sd, sd], compiler_params=_cparams("arbitrary"),
    )(w, g, m, v)


def _local_step(x, target, small, win_t, wout, wff1_t, wff2):
    slopes = jnp.asarray(_alibi_slopes(NH))
    hn1, q, k, v, u, z = _inproj_fwd(x, small["norm1_g"], win_t)
    outs, lses = [], []
    for i, dil in enumerate(DILS):
        o, l = _attn_fwd(q[i], k[i], v[i], slopes, dil)
        outs.append(o)
        lses.append(l)
    attn, lse, mixed, h1 = _mix_fwd(outs, lses, u, z, x, small["ln_g"], small["ln_b"], small["sgu_w"],
                                    small["bias_t"], small["attn_out_g"], small["gmlp_out_g"], wout)
    hn2, rf, act, dh2, loss, dgf = _mlp_fwd(h1, small["norm2_g"], wff1_t, wff2, small["final_norm_g"], target)
    df, dh1, dg2 = _mlp_bwd(dh2, rf, h1, small["norm2_g"], wff1_t, wff2)
    (do, delta, du, dz, dga, dgg, dlng, dlnb, dws, db) = _mix_bwd(
        dh1, attn, u, z, small["ln_g"], small["ln_b"], small["sgu_w"], small["sgu_wt"], small["bias_t"],
        small["attn_out_g"], small["gmlp_out_g"], wout)
    dqs, dks, dvs = [], [], []
    for i, dil in enumerate(DILS):
        dqs.append(_attn_bwd_dq(q[i], k[i], v[i], do[i], lse[i], delta[i], slopes, dil))
        dk, dv = _attn_bwd_dkv(q[i], k[i], v[i], do[i], lse[i], delta[i], slopes, dil)
        dks.append(dk)
        dvs.append(dv)
    dproj, dx, dg1 = _inproj_bwd(dqs, dks, dvs, du, dz, dh1, x, small["norm1_g"], win_t)
    gwin_t = _wgrad(dproj, hn1, "wgrad_in", INW // 2, D)
    gwout = _wgrad(mixed, dh1, "wgrad_out", D, D)
    gwff1_t = _wgrad(df, hn2, "wgrad_ff1", 1024, D)
    gwff2 = _wgrad(act, dh2, "wgrad_ff2", 1024, D)
    small_grads = dict(norm1_g=dg1, ln_g=dlng, ln_b=dlnb, sgu_w=dws, sgu_b=db[:, :NG].T,
                       attn_out_g=dga, gmlp_out_g=dgg, norm2_g=dg2, final_norm_g=dgf)
    return loss[0, 0], dx, small_grads, (gwin_t, gwout, gwff1_t, gwff2)


ANY = pl.BlockSpec(memory_space=pl.ANY)
HALF_ROWS = SHARD_ROWS // 2
ADD_ROWS = 376
D2D_SPLIT = 4


def _position():
    return lax.axis_index("x"), lax.axis_index("y"), lax.axis_index("c")


def _other_chips(x, y):
    return [(1 - x, y), (x, 1 - y), (1 - x, 1 - y)]


def _remote(src, dst, send_sem, recv_sem, device):
    return pltpu.make_async_remote_copy(src_ref=src, dst_ref=dst, send_sem=send_sem, recv_sem=recv_sem,
                                        device_id=device, device_id_type=MESH)


def _gather_weights(wp):
    def body(w_ref, out_ref, send_sems, recv_sems, local_sem):
        x, y, c = _position()
        me = 2 * x + y
        mine = pltpu.make_async_copy(w_ref, out_ref.at[me], local_sem)
        mine.start()
        sends = []
        for k, (px, py) in enumerate(_other_chips(x, y)):
            cp = _remote(w_ref, out_ref.at[me], send_sems.at[k], recv_sems.at[k], (px, py, c))
            cp.start()
            sends.append(cp)
        for k, (px, py) in enumerate(_other_chips(x, y)):
            _remote(w_ref, out_ref.at[2 * px + py], send_sems.at[k], recv_sems.at[k], (px, py, c)).wait_recv()
        for cp in sends:
            cp.wait_send()
        mine.wait()

    return pl.pallas_call(
        body, name="gather_weights", in_specs=[ANY], out_specs=ANY,
        out_shape=jax.ShapeDtypeStruct((NCHIP,) + wp.shape, wp.dtype),
        scratch_shapes=[pltpu.SemaphoreType.DMA((3,)), pltpu.SemaphoreType.DMA((3,)), pltpu.SemaphoreType.DMA],
        compiler_params=pltpu.CompilerParams(has_side_effects=True),
    )(wp)


def _exchange_halves(gall):
    part = HALF_ROWS // D2D_SPLIT

    def body(g_ref, own_ref, got_ref, send_sems, recv_sems, local_sem):
        x, y, c = _position()
        keep = g_ref.at[:, pl.ds(c * HALF_ROWS, HALF_ROWS), :]
        local = pltpu.make_async_copy(keep, own_ref, local_sem)
        local.start()
        copies = []
        for j in range(NCHIP):
            for t in range(D2D_SPLIT):
                src = g_ref.at[j, pl.ds((1 - c) * HALF_ROWS + t * part, part), :]
                dst = got_ref.at[j, pl.ds(t * part, part), :]
                n = j * D2D_SPLIT + t
                cp = _remote(src, dst, send_sems.at[n], recv_sems.at[n], (x, y, 1 - c))
                cp.start()
                copies.append(cp)
        for cp in copies:
            cp.wait()
        local.wait()

    sd = jax.ShapeDtypeStruct((NCHIP, HALF_ROWS, D), F32)
    n = NCHIP * D2D_SPLIT
    return pl.pallas_call(
        body, name="exchange_halves", in_specs=[ANY], out_specs=[ANY, ANY], out_shape=[sd, sd],
        scratch_shapes=[pltpu.SemaphoreType.DMA((n,)), pltpu.SemaphoreType.DMA((n,)), pltpu.SemaphoreType.DMA],
        compiler_params=pltpu.CompilerParams(has_side_effects=True),
    )(gall)


def _scatter_to_owners(part):
    def body(p_ref, own_ref, got_ref, send_sems, recv_sems, local_sem):
        x, y, c = _position()
        local = pltpu.make_async_copy(p_ref.at[2 * x + y], own_ref, local_sem)
        local.start()
        sends = []
        for k, (px, py) in enumerate(_other_chips(x, y)):
            cp = _remote(p_ref.at[2 * px + py], got_ref.at[k], send_sems.at[k], recv_sems.at[k], (px, py, c))
            cp.start()
            sends.append(cp)
        for cp in sends:
            cp.wait()
        local.wait()

    return pl.pallas_call(
        body, name="scatter_to_owners", in_specs=[ANY], out_specs=[ANY, ANY],
        out_shape=[jax.ShapeDtypeStruct((HALF_ROWS, D), F32), jax.ShapeDtypeStruct((3, HALF_ROWS, D), F32)],
        scratch_shapes=[pltpu.SemaphoreType.DMA((3,)), pltpu.SemaphoreType.DMA((3,)), pltpu.SemaphoreType.DMA],
        compiler_params=pltpu.CompilerParams(has_side_effects=True),
    )(part)


def _share_with_sibling(half):
    n = D2D_SPLIT
    part = HALF_ROWS // n

    def body(h_ref, out_ref, send_sems, recv_sems, local_sem):
        x, y, c = _position()
        local = pltpu.make_async_copy(h_ref, out_ref.at[c], local_sem)
        local.start()
        copies = []
        for t in range(n):
            rows = pl.ds(t * part, part)
            cp = _remote(h_ref.at[rows, :], out_ref.at[c, rows, :], send_sems.at[t], recv_sems.at[t], (x, y, 1 - c))
            cp.start()
            copies.append(cp)
        for t, cp in enumerate(copies):
            cp.wait_send()
            rows = pl.ds(t * part, part)
            _remote(h_ref.at[rows, :], out_ref.at[1 - c, rows, :], send_sems.at[t], recv_sems.at[t],
                    (x, y, 1 - c)).wait_recv()
        local.wait()

    return pl.pallas_call(
        body, name="share_with_sibling", in_specs=[ANY], out_specs=ANY,
        out_shape=jax.ShapeDtypeStruct((2, HALF_ROWS, D), F32),
        scratch_shapes=[pltpu.SemaphoreType.DMA((n,)), pltpu.SemaphoreType.DMA((n,)), pltpu.SemaphoreType.DMA],
        compiler_params=pltpu.CompilerParams(has_side_effects=True),
    )(half)


def _add_slabs(terms, out_rows, name):
    n = len(terms)

    def body(*refs):
        acc = refs[0][...]
        for r in refs[1:n]:
            acc = acc + r[...]
        refs[n][...] = acc

    specs = [pl.BlockSpec((ADD_ROWS, D), functools.partial(lambda i, off: (i + off, 0), off=first // ADD_ROWS))
             for _, first in terms]
    return pl.pallas_call(
        body, name=name, grid=(out_rows // ADD_ROWS,), in_specs=specs, out_specs=_row_spec(ADD_ROWS, D),
        out_shape=jax.ShapeDtypeStruct((out_rows, D), F32), compiler_params=_cparams("arbitrary"),
    )(*[a for a, _ in terms])


def _reduce_grads(gall):
    own, got = _exchange_halves(gall)
    rows = NCHIP * HALF_ROWS
    chip = _add_slabs([(own.reshape(rows, D), 0), (got.reshape(rows, D), 0)], rows, "add_sibling")
    mine, others = _scatter_to_owners(chip.reshape(NCHIP, HALF_ROWS, D))
    flat = others.reshape(3 * HALF_ROWS, D)
    half = _add_slabs([(mine, 0), (flat, 0), (flat, HALF_ROWS), (flat, 2 * HALF_ROWS)], HALF_ROWS, "add_chips")
    return _share_with_sibling(half).reshape(SHARD_ROWS, D)


SMALL_SIZES = (("norm1_g", D), ("sgu_ln_g", GW), ("sgu_ln_b", GW), ("sgu_w", NG * CHUNK * CHUNK),
               ("sgu_b", NG * CHUNK), ("attn_out_g", A), ("gmlp_out_g", GW), ("norm2_g", D),
               ("final_norm_g", D))
SMALL_ROWS = sum(n for _, n in SMALL_SIZES) // LANES
NDEV = 8


def _pack_small(tree):
    return jnp.concatenate([tree[n].reshape(-1) for n, _ in SMALL_SIZES]).reshape(SMALL_ROWS, LANES)


def _unpack_small(pack, shapes):
    flat = pack.reshape(-1)
    out, off = {}, 0
    for n, size in SMALL_SIZES:
        out[n] = flat[off:off + size].reshape(shapes[n])
        off += size
    return out


def _small_allreduce_adamw(gpack, wpack, mpack, vpack):
    def body(g_ref, w_ref, m_ref, v_ref, go_ref, d_ref, mo_ref, vo_ref, slots, send_sems, recv_sems):
        x, y, c = _position()
        me = 4 * x + 2 * y + c
        slots[me] = g_ref[...]
        sends = []
        for k in range(1, NDEV):
            kx, ky, kc = (k >> 2) & 1, (k >> 1) & 1, k & 1
            peer = (1 - x if kx else x, 1 - y if ky else y, 1 - c if kc else c)
            cp = _remote(g_ref, slots.at[me], send_sems.at[k - 1], recv_sems.at[k - 1], peer)
            cp.start()
            sends.append((cp, peer))
        for k in range(1, NDEV):
            _, (px, py, pc) = sends[k - 1]
            _remote(g_ref, slots.at[4 * px + 2 * py + pc], send_sems.at[k - 1], recv_sems.at[k - 1],
                    (px, py, pc)).wait_recv()
        for cp, _ in sends:
            cp.wait_send()
        total = slots[0]
        for k in range(1, NDEV):
            total = total + slots[k]
        go_ref[...] = total
        d, mn, vn = _adamw_math(w_ref[...], total, m_ref[...], v_ref[...])
        d_ref[...] = d
        mo_ref[...] = mn
        vo_ref[...] = vn

    sd = jax.ShapeDtypeStruct((SMALL_ROWS, LANES), F32)
    vm = pl.BlockSpec(memory_space=pltpu.VMEM)
    return pl.pallas_call(
        body, name="small_allreduce_adamw", in_specs=[vm] * 4, out_specs=[vm] * 4, out_shape=[sd] * 4,
        scratch_shapes=[pltpu.VMEM((NDEV, SMALL_ROWS, LANES), F32), pltpu.SemaphoreType.DMA((NDEV - 1,)),
                        pltpu.SemaphoreType.DMA((NDEV - 1,))],
        compiler_params=pltpu.CompilerParams(has_side_effects=True),
    )(gpack, wpack, mpack, vpack)


def kernel(x, norm1_g, w_in, sgu_ln_g, sgu_ln_b, sgu_w, sgu_b, attn_out_g, gmlp_out_g, w_out, norm2_g, w_ff1, w_ff2, final_norm_g, loss_target, m_norm1_g, m_w_in, m_sgu_ln_g, m_sgu_ln_b, m_sgu_w, m_sgu_b, m_attn_out_g, m_gmlp_out_g, m_w_out, m_norm2_g, m_w_ff1, m_w_ff2, m_final_norm_g, v_norm1_g, v_w_in, v_sgu_ln_g, v_sgu_ln_b, v_sgu_w, v_sgu_b, v_attn_out_g, v_gmlp_out_g, v_w_out, v_norm2_g, v_w_ff1, v_w_ff2, v_final_norm_g):
    names = [n for n, _ in SMALL_SIZES]
    w_small = dict(norm1_g=norm1_g, sgu_ln_g=sgu_ln_g, sgu_ln_b=sgu_ln_b, sgu_w=sgu_w, sgu_b=sgu_b,
                   attn_out_g=attn_out_g, gmlp_out_g=gmlp_out_g, norm2_g=norm2_g, final_norm_g=final_norm_g)
    m_small = dict(norm1_g=m_norm1_g, sgu_ln_g=m_sgu_ln_g, sgu_ln_b=m_sgu_ln_b, sgu_w=m_sgu_w, sgu_b=m_sgu_b,
                   attn_out_g=m_attn_out_g, gmlp_out_g=m_gmlp_out_g, norm2_g=m_norm2_g,
                   final_norm_g=m_final_norm_g)
    v_small = dict(norm1_g=v_norm1_g, sgu_ln_g=v_sgu_ln_g, sgu_ln_b=v_sgu_ln_b, sgu_w=v_sgu_w, sgu_b=v_sgu_b,
                   attn_out_g=v_attn_out_g, gmlp_out_g=v_gmlp_out_g, norm2_g=v_norm2_g,
                   final_norm_g=v_final_norm_g)
    shapes = {n: w_small[n].shape for n in names}

    r_in, r_out, r_ff = INW // NCHIP, D // NCHIP, DFF // NCHIP
    packed = jnp.concatenate([w_in[0].T, w_out[0], w_ff1[0].T, w_ff2[0]], axis=0).astype(BF16)
    wall = _gather_weights(packed)
    o1, o2, o3 = r_in, r_in + r_out, r_in + r_out + r_ff
    win_t = wall[:, :o1].reshape(INW, D)
    wout = wall[:, o1:o2].reshape(D, D)
    wff1_t = wall[:, o2:o3].reshape(DFF, D)
    wff2 = wall[:, o3:].reshape(DFF, D)

    small = dict(
        norm1_g=norm1_g, ln_g=sgu_ln_g.reshape(1, GW), ln_b=sgu_ln_b.reshape(1, GW), sgu_w=sgu_w[0],
        sgu_wt=jnp.swapaxes(sgu_w[0], 1, 2), bias_t=jnp.repeat(sgu_b[0].T, DH, axis=1),
        attn_out_g=attn_out_g, gmlp_out_g=gmlp_out_g, norm2_g=norm2_g, final_norm_g=final_norm_g.reshape(1, D))
    loss_part, dx, sg, (gwin_t, gwout, gwff1_t, gwff2) = _local_step(
        x[0], loss_target[0], small, win_t, wout, wff1_t, wff2)
    loss = lax.psum(loss_part, ("x", "y", "c"))

    gall = jnp.concatenate([gwin_t.reshape(NCHIP, r_in, D), gwout.reshape(NCHIP, r_out, D),
                            gwff1_t.reshape(NCHIP, r_ff, D), gwff2.reshape(NCHIP, r_ff, D)], axis=1)
    gsum = _reduce_grads(gall)
    g_big = dict(w_in=gsum[:o1].T, w_out=gsum[o1:o2], w_ff1=gsum[o2:o3].T, w_ff2=gsum[o3:])
    w_big = dict(w_in=(w_in, m_w_in, v_w_in), w_out=(w_out, m_w_out, v_w_out),
                 w_ff1=(w_ff1, m_w_ff1, v_w_ff1), w_ff2=(w_ff2, m_w_ff2, v_w_ff2))
    grads, deltas, new_m, new_v = {}, {}, {}, {}
    for n, (w, m, v) in w_big.items():
        d, mn, vn = _adamw(w[0], g_big[n], m[0], v[0], "adamw_" + n)
        grads[n], deltas[n], new_m[n], new_v[n] = g_big[n][None], d[None], mn[None], vn[None]

    g_small = dict(norm1_g=sg["norm1_g"], sgu_ln_g=sg["ln_g"], sgu_ln_b=sg["ln_b"], sgu_w=sg["sgu_w"],
                   sgu_b=sg["sgu_b"], attn_out_g=sg["attn_out_g"], gmlp_out_g=sg["gmlp_out_g"],
                   norm2_g=sg["norm2_g"], final_norm_g=sg["final_norm_g"])
    packs = _small_allreduce_adamw(_pack_small(g_small), _pack_small(w_small), _pack_small(m_small),
                                   _pack_small(v_small))
    for tree, pack in zip((grads, deltas, new_m, new_v), packs):
        tree.update(_unpack_small(pack, shapes))

    order = ["norm1_g", "w_in", "sgu_ln_g", "sgu_ln_b", "sgu_w", "sgu_b", "attn_out_g", "gmlp_out_g", "w_out",
             "norm2_g", "w_ff1", "w_ff2", "final_norm_g"]
    return (loss, dx[None], *[grads[n] for n in order], *[deltas[n] for n in order],
            *[new_m[n] for n in order], *[new_v[n] for n in order])
```

```python
import functools
import math

import numpy as np
import jax
import jax.numpy as jnp
from jax import lax
from jax.experimental import pallas as pl
from jax.experimental.pallas import tpu as pltpu

F32 = jnp.float32
BF16 = jnp.bfloat16

D = 1024
NH = 12
DH = 64
A = NH * DH
NG = 4
GW = NG * DH
INW = 3 * A + 2 * GW
DFF = 4 * D
CHUNK = 128
PATTERNS = ((128, 1), (512, 4), (2048, 16))
EPS = 1e-6
SCALE = DH ** -0.5
NEG = -1e30

LR, B1, B2, AEPS, WD, STEP = 0.001, 0.9, 0.999, 1e-08, 0.01, 10

TM = 512
TMX = 256
ATT_ROWS = 1024
FF_CH = 1024
LANES = 128
NCHIP = 4
SHARD_ROWS = INW // NCHIP + D // NCHIP + DFF // NCHIP + DFF // NCHIP
VMEM_LIMIT = 56 * 1024 * 1024
MESH = pl.DeviceIdType.MESH


def _cparams(*sem, **kw):
    return pltpu.CompilerParams(dimension_semantics=sem if sem else None,
                                vmem_limit_bytes=VMEM_LIMIT, **kw)


def _dot(a, b):
    return jnp.dot(a, b, preferred_element_type=F32)


def _dot_nt(a, b):
    return lax.dot_general(a, b, (((1,), (1,)), ((), ())), preferred_element_type=F32)


def _dot_tn(a, b):
    return lax.dot_general(a, b, (((0,), (0,)), ((), ())), preferred_element_type=F32)


def _dot_hi(a, b):
    return jnp.dot(a, b, preferred_element_type=F32, precision=lax.Precision.HIGHEST)


def _alibi_slopes(n):
    def pow2(m):
        start = 2.0 ** (-8.0 / m)
        return [start ** (i + 1) for i in range(m)]
    if math.log2(n).is_integer():
        s = pow2(n)
    else:
        c = 2 ** int(math.floor(math.log2(n)))
        s = pow2(c) + pow2(2 * c)[0::2][: n - c]
    return np.asarray(s, dtype=np.float32)


def _rms_fwd(v, g):
    r = lax.rsqrt(jnp.mean(v * v, axis=-1, keepdims=True) + EPS)
    vn = v * r
    return vn * g, vn, r


def _rms_bwd(dy, vn, r, g):
    w = dy * g
    dv = r * (w - vn * jnp.mean(w * vn, axis=-1, keepdims=True))
    return dv, jnp.sum(dy * vn, axis=0, keepdims=True)


_K0 = math.sqrt(2.0 / math.pi)
_K1 = 0.044715


def _gelu(v):
    return 0.5 * v * (1.0 + jnp.tanh(_K0 * (v + _K1 * (v * v * v))))


def _gelu_grad(v):
    t = jnp.tanh(_K0 * (v + _K1 * (v * v * v)))
    return 0.5 * (1.0 + t) + 0.5 * v * (1.0 - t * t) * (_K0 * (1.0 + 3.0 * _K1 * v * v))


def _row_spec(rows, cols):
    return pl.BlockSpec((rows, cols), lambda i: (i, 0))


def _const_spec(shape):
    nd = len(shape)
    return pl.BlockSpec(shape, lambda i: (0,) * nd, pipeline_mode=pl.Buffered(1))


DILS = tuple(d for _, d in PATTERNS)


def _fill_cols(scr, value):
    for cb in range(value.shape[1] // LANES):
        scr[cb] = value[:, cb * LANES:(cb + 1) * LANES]


def _split_residues(scr, out_ref, dil):
    nb, rows, _ = scr.shape
    for r in range(dil):
        for cb in range(nb):
            piece = scr.at[cb][pl.ds(r, rows // dil, stride=dil), :]
            out_ref[r, :, cb * LANES:(cb + 1) * LANES] = piece.astype(out_ref.dtype)


def _merge_residues(in_ref, scr, dil):
    nb, rows, _ = scr.shape
    for r in range(dil):
        for cb in range(nb):
            scr.at[cb][pl.ds(r, rows // dil, stride=dil), :] = in_ref[r, :, cb * LANES:(cb + 1) * LANES]
    return jnp.concatenate([scr[cb] for cb in range(nb)], axis=-1)


def _col_scratch(rows, width):
    return pltpu.VMEM((width // LANES, rows, LANES), F32)


def _res_spec(dil, rows, width):
    return pl.BlockSpec((dil, rows // dil, width), lambda i: (0, i, 0))


def _res_shape(s, dil, width, dtype):
    return jax.ShapeDtypeStruct((dil, s // dil, width), dtype)


def _inproj_fwd(x, g1, win_t):
    s = x.shape[0]
    nd = len(DILS)

    def body(x_ref, g_ref, w_ref, hn_ref, *rest):
        qkv_refs = rest[:3 * nd]
        u_ref, z_ref, scr = rest[3 * nd:]
        hn, _, _ = _rms_fwd(x_ref[...], g_ref[...])
        hn = hn.astype(BF16)
        hn_ref[...] = hn
        for t in range(3):
            seg = _dot_nt(hn, w_ref[t * A:(t + 1) * A, :])
            seg = seg * SCALE if t == 0 else seg
            _fill_cols(scr, seg)
            for di, dil in enumerate(DILS):
                if dil == 1:
                    qkv_refs[t * nd + di][0] = seg.astype(BF16)
                else:
                    _split_residues(scr, qkv_refs[t * nd + di], dil)
        u_ref[...] = _dot_nt(hn, w_ref[3 * A:3 * A + GW, :])
        z_ref[...] = _dot_nt(hn, w_ref[3 * A + GW:INW, :])

    res = pl.pallas_call(
        body, name="inproj_fwd", grid=(s // TM,),
        in_specs=[_row_spec(TM, D), _const_spec((1, D)), _const_spec((INW, D))],
        out_specs=[_row_spec(TM, D)] + [_res_spec(d, TM, A) for _ in range(3) for d in DILS]
                  + [_row_spec(TM, GW), _row_spec(TM, GW)],
        out_shape=[jax.ShapeDtypeStruct((s, D), BF16)] + [_res_shape(s, d, A, BF16) for _ in range(3) for d in DILS]
                  + [jax.ShapeDtypeStruct((s, GW), F32)] * 2,
        scratch_shapes=[_col_scratch(TM, A)],
        compiler_params=_cparams("arbitrary"),
    )(x, g1, win_t)
    hn1 = res[0]
    q, k, v = (res[1 + t * nd:1 + (t + 1) * nd] for t in range(3))
    return hn1, q, k, v, res[-2], res[-1]


def _att_geometry(s, dil):
    length = s // dil
    rows = min(length, ATT_ROWS)
    return length, rows, length // rows, rows // CHUNK


def _stack_heads(t):
    lane = lax.broadcasted_iota(jnp.int32, t.shape, 1)
    zero = jnp.zeros_like(t)
    return jnp.concatenate([jnp.where(lane < DH, t, zero), jnp.where(lane >= DH, t, zero)], axis=0)


def _stack_cols(t):
    return jnp.concatenate([t[:, 0:1], t[:, DH:DH + 1]], axis=0)


def _unstack_heads(t2):
    n = t2.shape[0] // 2
    lane = lax.broadcasted_iota(jnp.int32, (n, LANES), 1)
    return jnp.where(lane < DH, t2[:n], t2[n:])


def _query_window_bias(s0, s1, dil, first):
    row = lax.broadcasted_iota(jnp.int32, (2 * CHUNK, 2 * CHUNK), 0)
    col = lax.broadcasted_iota(jnp.int32, (2 * CHUNK, 2 * CHUNK), 1)
    steps = (row & (CHUNK - 1)) + CHUNK - col
    valid = (steps >= 0) & (steps <= CHUNK)
    if first:
        valid = valid & (col >= CHUNK)
    slope = jnp.where(row < CHUNK, s0, s1)
    return jnp.where(valid, -slope * (steps * dil).astype(F32), NEG)


def _key_block_bias(s0, s1, dil, last):
    key = lax.broadcasted_iota(jnp.int32, (CHUNK, 4 * CHUNK), 0)
    col = lax.broadcasted_iota(jnp.int32, (CHUNK, 4 * CHUNK), 1)
    wq = col & (2 * CHUNK - 1)
    steps = wq - key
    valid = (steps >= 0) & (steps <= CHUNK)
    if last:
        valid = valid & (wq < CHUNK)
    slope = jnp.where(col < 2 * CHUNK, s0, s1)
    return jnp.where(valid, -slope * (steps * dil).astype(F32), NEG)


def _head_rows(t):
    row = lax.broadcasted_iota(jnp.int32, (8, LANES), 0)
    lane = lax.broadcasted_iota(jnp.int32, (8, LANES), 1)
    pick = jnp.where(((row == 0) & (lane == 0)) | ((row == 1) & (lane == DH)), 1.0, 0.0).astype(BF16)
    hi = t.astype(BF16)
    rest = t - hi.astype(F32)
    mid = rest.astype(BF16)
    low = (rest - mid.astype(F32)).astype(BF16)
    return _dot_nt(pick, hi) + _dot_nt(pick, mid) + _dot_nt(pick, low)


def _att_specs(dil, rows, nsub, nblk):
    main = pl.BlockSpec((None, rows, LANES), lambda r, hp, c: (r, c, hp))
    prev = pl.BlockSpec((None, CHUNK, LANES), lambda r, hp, c: (r, jnp.maximum(c * nsub - 1, 0), hp))
    nxt = pl.BlockSpec((None, CHUNK, LANES), lambda r, hp, c: (r, jnp.minimum((c + 1) * nsub, nblk - 1), hp))
    return main, prev, nxt


def _row_start(i):
    return i * CHUNK if isinstance(i, int) else pl.multiple_of(i * CHUNK, CHUNK)


def _attn_fwd(q, k, v, slopes, dil):
    length = q.shape[1]
    _, rows, nch, nsub = _att_geometry(length * dil, dil)
    main, prev, _ = _att_specs(dil, rows, nsub, length // CHUNK)

    def body(sl_ref, q_ref, k_ref, v_ref, kh_ref, vh_ref, o_ref, lse_ref, kbuf, vbuf, bias_buf):
        hp = pl.program_id(1)
        ch = pl.program_id(2)
        kbuf[0:CHUNK, :] = kh_ref[...]
        kbuf[CHUNK:, :] = k_ref[...]
        vbuf[0:CHUNK, :] = vh_ref[...]
        vbuf[CHUNK:, :] = v_ref[...]
        s0, s1 = sl_ref[2 * hp], sl_ref[2 * hp + 1]

        def block(i, bias):
            row = _row_start(i)
            rs = pl.ds(row, CHUNK)
            q2 = _stack_heads(q_ref[rs, :])
            kw = kbuf[pl.ds(row, 2 * CHUNK), :]
            vw = vbuf[pl.ds(row, 2 * CHUNK), :]
            sc = _dot_nt(q2, kw) + bias
            m = jnp.max(sc, axis=-1, keepdims=True)
            p = jnp.exp(sc - m)
            l = jnp.sum(p, axis=-1, keepdims=True)
            o2 = _dot(p.astype(BF16), vw) * (1.0 / l)
            o_ref[rs, :] = _unstack_heads(o2)
            lse_ref[rs, :] = _unstack_heads(jnp.broadcast_to(m + jnp.log(l), (2 * CHUNK, LANES)))

        bias_buf[...] = _query_window_bias(s0, s1, dil, False)

        @pl.when(ch == 0)
        def _():
            block(0, _query_window_bias(s0, s1, dil, True))

        @pl.when(ch != 0)
        def _():
            block(0, bias_buf[...])

        for i in range(1, nsub):
            block(i, bias_buf[...])

    sd = jax.ShapeDtypeStruct((dil, length, A), F32)
    return pl.pallas_call(
        body, name=f"attn_fwd_d{dil}", grid=(dil, NH // 2, nch),
        in_specs=[pl.BlockSpec(memory_space=pltpu.SMEM), main, main, main, prev, prev],
        out_specs=[main, main], out_shape=[sd, sd],
        scratch_shapes=[pltpu.VMEM((rows + CHUNK, LANES), BF16), pltpu.VMEM((rows + CHUNK, LANES), BF16),
                        pltpu.VMEM((2 * CHUNK, 2 * CHUNK), F32)],
        compiler_params=_cparams("arbitrary", "arbitrary", "arbitrary"),
    )(slopes, q, k, v, k, v)


def _attn_bwd_dq(q, k, v, do, lse, delta, slopes, dil):
    length = q.shape[1]
    _, rows, nch, nsub = _att_geometry(length * dil, dil)
    main, prev, _ = _att_specs(dil, rows, nsub, length // CHUNK)

    def body(sl_ref, q_ref, k_ref, v_ref, do_ref, lse_ref, dl_ref, kh_ref, vh_ref, dq_ref, kbuf, vbuf, bias_buf):
        hp = pl.program_id(1)
        ch = pl.program_id(2)
        kbuf[0:CHUNK, :] = kh_ref[...]
        kbuf[CHUNK:, :] = k_ref[...]
        vbuf[0:CHUNK, :] = vh_ref[...]
        vbuf[CHUNK:, :] = v_ref[...]
        s0, s1 = sl_ref[2 * hp], sl_ref[2 * hp + 1]

        def block(i, bias):
            row = _row_start(i)
            rs = pl.ds(row, CHUNK)
            q2 = _stack_heads(q_ref[rs, :])
            do2 = _stack_heads(do_ref[rs, :])
            lse2 = _stack_cols(lse_ref[rs, :])
            dl2 = _stack_cols(dl_ref[rs, :])
            kw = kbuf[pl.ds(row, 2 * CHUNK), :]
            vw = vbuf[pl.ds(row, 2 * CHUNK), :]
            p = jnp.exp(_dot_nt(q2, kw) + bias - lse2)
            ds = p * (_dot_nt(do2, vw) - dl2)
            dq_ref[rs, :] = _unstack_heads(_dot(ds.astype(BF16), kw))

        bias_buf[...] = _query_window_bias(s0, s1, dil, False)

        @pl.when(ch == 0)
        def _():
            block(0, _query_window_bias(s0, s1, dil, True))

        @pl.when(ch != 0)
        def _():
            block(0, bias_buf[...])

        for i in range(1, nsub):
            block(i, bias_buf[...])

    return pl.pallas_call(
        body, name=f"attn_dq_d{dil}", grid=(dil, NH // 2, nch),
        in_specs=[pl.BlockSpec(memory_space=pltpu.SMEM), main, main, main, main, main, main, prev, prev],
        out_specs=main, out_shape=jax.ShapeDtypeStruct((dil, length, A), F32),
        scratch_shapes=[pltpu.VMEM((rows + CHUNK, LANES), BF16), pltpu.VMEM((rows + CHUNK, LANES), BF16),
                        pltpu.VMEM((2 * CHUNK, 2 * CHUNK), F32)],
        compiler_params=_cparams("arbitrary", "arbitrary", "arbitrary"),
    )(slopes, q, k, v, do, lse, delta, k, v)


def _attn_bwd_dkv(q, k, v, do, lse, delta, slopes, dil):
    length = q.shape[1]
    _, rows, nch, nsub = _att_geometry(length * dil, dil)
    main, _, nxt = _att_specs(dil, rows, nsub, length // CHUNK)

    def body(sl_ref, k_ref, v_ref, q_ref, do_ref, lse_ref, dl_ref, qh_ref, doh_ref, lseh_ref, dlh_ref,
             dk_ref, dv_ref, qbuf, dobuf, lse_rows, dl_rows, bias_buf):
        hp = pl.program_id(1)
        ch = pl.program_id(2)
        for buf, main_ref, halo_ref in ((qbuf, q_ref, qh_ref), (dobuf, do_ref, doh_ref)):
            buf[0:rows, :] = main_ref[...]
            buf[rows:, :] = halo_ref[...]
        for buf, main_ref, halo_ref in ((lse_rows, lse_ref, lseh_ref), (dl_rows, dl_ref, dlh_ref)):
            buf[:, 0:rows] = _head_rows(main_ref[...])
            buf[:, rows:] = _head_rows(halo_ref[...])
        s0, s1 = sl_ref[2 * hp], sl_ref[2 * hp + 1]

        def block(i, bias):
            row = _row_start(i)
            rs = pl.ds(row, CHUNK)
            win = pl.ds(row, 2 * CHUNK)
            kc = k_ref[rs, :]
            vc = v_ref[rs, :]
            q2 = _stack_heads(qbuf[win, :])
            do2 = _stack_heads(dobuf[win, :])
            cols = slice(i * CHUNK, (i + 2) * CHUNK)
            lse2 = jnp.concatenate([lse_rows[0:1, cols], lse_rows[1:2, cols]], axis=1)
            dl2 = jnp.concatenate([dl_rows[0:1, cols], dl_rows[1:2, cols]], axis=1)
            pt = jnp.exp(_dot_nt(kc, q2) + bias - lse2)
            dst = pt * (_dot_nt(vc, do2) - dl2)
            dv_ref[rs, :] = _dot(pt.astype(BF16), do2)
            dk_ref[rs, :] = _dot(dst.astype(BF16), q2)

        bias_buf[...] = _key_block_bias(s0, s1, dil, False)

        for i in range(nsub - 1):
            block(i, bias_buf[...])

        @pl.when(ch != nch - 1)
        def _():
            block(nsub - 1, bias_buf[...])

        @pl.when(ch == nch - 1)
        def _():
            block(nsub - 1, _key_block_bias(s0, s1, dil, True))

    sd = jax.ShapeDtypeStruct((dil, length, A), F32)
    return pl.pallas_call(
        body, name=f"attn_dkv_d{dil}", grid=(dil, NH // 2, nch),
        in_specs=[pl.BlockSpec(memory_space=pltpu.SMEM), main, main, main, main, main, main, nxt, nxt, nxt, nxt],
        out_specs=[main, main], out_shape=[sd, sd],
        scratch_shapes=[pltpu.VMEM((rows + CHUNK, LANES), BF16), pltpu.VMEM((rows + CHUNK, LANES), BF16),
                        pltpu.VMEM((8, rows + CHUNK), F32), pltpu.VMEM((8, rows + CHUNK), F32),
                        pltpu.VMEM((CHUNK, 4 * CHUNK), F32)],
        compiler_params=_cparams("arbitrary", "arbitrary", "arbitrary"),
    )(slopes, k, v, q, do, lse, delta, q, do, lse, delta)


def _group_masks(width):
    lane = lax.broadcasted_iota(jnp.int32, (1, width), 1)
    return [(lane >= g * DH) & (lane < (g + 1) * DH) for g in range(width // DH)]


def _group_mean_matrix():
    i = lax.broadcasted_iota(jnp.int32, (GW, GW), 0) // DH
    j = lax.broadcasted_iota(jnp.int32, (GW, GW), 1) // DH
    return jnp.where(i == j, 1.0 / DH, 0.0).astype(F32)


def _tri_mask(lower):
    t = lax.broadcasted_iota(jnp.int32, (CHUNK, CHUNK), 0)
    u = lax.broadcasted_iota(jnp.int32, (CHUNK, CHUNK), 1)
    return (u <= t) if lower else (u >= t)


def _sgu_forward(u, z, lng, lnb, w_ref, bias_t, pmat, rows):
    ug = _gelu(u)
    zg = _gelu(z)
    mu = _dot_hi(zg, pmat)
    zc = zg - mu
    var = _dot_hi(zc * zc, pmat)
    rstd = lax.rsqrt(var + EPS)
    zhat = zc * rstd
    zn = (zhat * lng + lnb).astype(BF16)
    gm = _group_masks(GW)
    tri = _tri_mask(True)
    ws = [jnp.where(tri, w_ref[g], 0.0).astype(BF16) for g in range(NG)]
    pieces = []
    for c in range(rows // CHUNK):
        znc = zn[c * CHUNK:(c + 1) * CHUNK, :]
        mix = None
        for g in range(NG):
            part = jnp.where(gm[g], _dot(ws[g], znc), 0.0)
            mix = part if mix is None else mix + part
        pieces.append(mix + bias_t)
    mixed = jnp.concatenate(pieces, axis=0) if len(pieces) > 1 else pieces[0]
    return ug * mixed, ug, zhat, rstd, zn, mixed


def _mix_fwd(os_, ls_, u, z, x, lng, lnb, sgu_w, bias_t, ga, gg, wout):
    s = x.shape[0]
    nd = len(DILS)
    nscr = sum(1 for d in DILS if d > 1)

    def body(*refs):
        o_refs, l_refs = refs[:nd], refs[nd:2 * nd]
        u_ref, z_ref, x_ref, lng_ref, lnb_ref, w_ref, bt_ref, ga_ref, gg_ref, wo_ref = refs[2 * nd:2 * nd + 10]
        attn_ref = refs[2 * nd + 10]
        lse_refs = refs[2 * nd + 11:3 * nd + 11]
        mixed_ref, h1_ref = refs[3 * nd + 11:3 * nd + 13]
        scr = refs[3 * nd + 13:]
        scr_o, scr_l, scr_lse = scr[:nscr], scr[nscr:2 * nscr], scr[2 * nscr]
        ov, lv, j = [], [], 0
        for di, dil in enumerate(DILS):
            if dil == 1:
                ov.append(o_refs[di][0])
                lv.append(l_refs[di][0])
            else:
                ov.append(_merge_residues(o_refs[di], scr_o[j], dil))
                lv.append(_merge_residues(l_refs[di], scr_l[j], dil))
                j += 1
        mx = functools.reduce(jnp.maximum, lv)
        es = [jnp.exp(l - mx) for l in lv]
        den = functools.reduce(lambda a, b: a + b, es)
        attn = functools.reduce(lambda a, b: a + b, [e * o for e, o in zip(es, ov)]) / den
        attn_ref[...] = attn
        lse = mx + jnp.log(den)
        _fill_cols(scr_lse, lse)
        for di, dil in enumerate(DILS):
            if dil == 1:
                lse_refs[di][0] = lse
            else:
                _split_residues(scr_lse, lse_refs[di], dil)
        an, _, _ = _rms_fwd(attn, ga_ref[...])
        gmv, _, _, _, _, _ = _sgu_forward(u_ref[...], z_ref[...], lng_ref[...], lnb_ref[...], w_ref,
                                          bt_ref[...], _group_mean_matrix(), TMX)
        gn, _, _ = _rms_fwd(gmv, gg_ref[...])
        mixed = jnp.concatenate([an, gn], axis=-1).astype(BF16)
        mixed_ref[...] = mixed
        h1_ref[...] = x_ref[...] + _dot(mixed, wo_ref[...])

    sd = jax.ShapeDtypeStruct
    res = pl.pallas_call(
        body, name="mix_fwd", grid=(s // TMX,),
        in_specs=[_res_spec(d, TMX, A) for d in DILS] * 2 + [_row_spec(TMX, GW), _row_spec(TMX, GW),
                  _row_spec(TMX, D), _const_spec((1, GW)), _const_spec((1, GW)), _const_spec((NG, CHUNK, CHUNK)),
                  _const_spec((CHUNK, GW)), _const_spec((1, A)), _const_spec((1, GW)), _const_spec((D, D))],
        out_specs=[_row_spec(TMX, A)] + [_res_spec(d, TMX, A) for d in DILS] + [_row_spec(TMX, D), _row_spec(TMX, D)],
        out_shape=[sd((s, A), F32)] + [_res_shape(s, d, A, F32) for d in DILS] + [sd((s, D), BF16), sd((s, D), F32)],
        scratch_shapes=[_col_scratch(TMX, A)] * (2 * nscr + 1),
        compiler_params=_cparams("arbitrary"),
    )(*os_, *ls_, u, z, x, lng, lnb, sgu_w, bias_t, ga, gg, wout)
    return res[0], res[1:1 + nd], res[1 + nd], res[2 + nd]


def _mlp_fwd(h1, g2, wff1_t, wff2, gf, target):
    s = h1.shape[0]

    def body(h1_ref, g2_ref, w1_ref, w2_ref, gf_ref, t_ref, hn_ref, rf_ref, a_ref, dh2_ref, loss_ref, dgf_ref):
        i = pl.program_id(0)
        h1v = h1_ref[...]
        hn, _, _ = _rms_fwd(h1v, g2_ref[...])
        hn = hn.astype(BF16)
        hn_ref[...] = hn
        acc = h1v
        for j in range(DFF // FF_CH):
            cols = slice(j * FF_CH, (j + 1) * FF_CH)
            rf = jnp.maximum(_dot_nt(hn, w1_ref[cols, :]), 0.0)
            act = (rf * rf).astype(BF16)
            rf_ref[:, cols] = rf.astype(BF16)
            a_ref[:, cols] = act
            acc = acc + _dot(act, w2_ref[cols, :])
        y, h2n, r3 = _rms_fwd(acc, gf_ref[...])
        err = y - t_ref[...]
        part = 0.5 * jnp.sum(jnp.mean(err * err, axis=-1, keepdims=True), axis=0, keepdims=True)
        dy = err * (1.0 / D)
        dh2, dgf = _rms_bwd(dy, h2n, r3, gf_ref[...])
        dh2_ref[...] = dh2

        @pl.when(i == 0)
        def _():
            loss_ref[...] = jnp.zeros_like(loss_ref)
            dgf_ref[...] = jnp.zeros_like(dgf_ref)

        loss_ref[...] += jnp.broadcast_to(part, loss_ref.shape)
        dgf_ref[...] += dgf

    sd = jax.ShapeDtypeStruct
    return pl.pallas_call(
        body, name="mlp_fwd", grid=(s // TM,),
        in_specs=[_row_spec(TM, D), _const_spec((1, D)), _const_spec((DFF, D)), _const_spec((DFF, D)),
                  _const_spec((1, D)), _row_spec(TM, D)],
        out_specs=[_row_spec(TM, D), _row_spec(TM, DFF), _row_spec(TM, DFF), _row_spec(TM, D),
                   _const_spec((1, LANES)), _const_spec((1, D))],
        out_shape=[sd((s, D), BF16), sd((s, DFF), BF16), sd((s, DFF), BF16), sd((s, D), F32),
                   sd((1, LANES), F32), sd((1, D), F32)],
        compiler_params=_cparams("arbitrary"),
    )(h1, g2, wff1_t, wff2, gf, target)


def _mlp_bwd(dh2, rf, h1, g2, wff1_t, wff2):
    s = h1.shape[0]

    def body(dh2_ref, rf_ref, h1_ref, g2_ref, w1_ref, w2_ref, df_ref, dh1_ref, dg2_ref):
        i = pl.program_id(0)
        dh2v = dh2_ref[...]
        dh2b = dh2v.astype(BF16)
        dhn = jnp.zeros((TM, D), F32)
        for j in range(DFF // FF_CH):
            cols = slice(j * FF_CH, (j + 1) * FF_CH)
            da = _dot_nt(dh2b, w2_ref[cols, :])
            df = (da * (2.0 * rf_ref[:, cols].astype(F32))).astype(BF16)
            df_ref[:, cols] = df
            dhn = dhn + _dot(df, w1_ref[cols, :])
        _, h1n, r2 = _rms_fwd(h1_ref[...], g2_ref[...])
        dres, dg2 = _rms_bwd(dhn, h1n, r2, g2_ref[...])
        dh1_ref[...] = dh2v + dres

        @pl.when(i == 0)
        def _():
            dg2_ref[...] = jnp.zeros_like(dg2_ref)

        dg2_ref[...] += dg2

    sd = jax.ShapeDtypeStruct
    return pl.pallas_call(
        body, name="mlp_bwd", grid=(s // TM,),
        in_specs=[_row_spec(TM, D), _row_spec(TM, DFF), _row_spec(TM, D), _const_spec((1, D)),
                  _const_spec((DFF, D)), _const_spec((DFF, D))],
        out_specs=[_row_spec(TM, DFF), _row_spec(TM, D), _const_spec((1, D))],
        out_shape=[sd((s, DFF), BF16), sd((s, D), F32), sd((1, D), F32)],
        compiler_params=_cparams("arbitrary"),
    )(dh2, rf, h1, g2, wff1_t, wff2)


def _mix_bwd(dh1, attn, u, z, lng, lnb, sgu_w, sgu_wt, bias_t, ga, gg, wout):
    s = dh1.shape[0]
    nsteps = s // TMX
    nd = len(DILS)

    def body(*refs):
        dh1_ref, attn_ref, u_ref, z_ref, lng_ref, lnb_ref, w_ref, wt_ref, bt_ref, ga_ref, gg_ref, wo_ref = refs[:12]
        do_refs, dl_refs = refs[12:12 + nd], refs[12 + nd:12 + 2 * nd]
        (du_ref, dz_ref, dga_ref, dgg_ref, dlng_ref, dlnb_ref, dws_ref, db_ref,
         dbt_acc, scr_do, scr_dl) = refs[12 + 2 * nd:]
        i = pl.program_id(0)

        @pl.when(i == 0)
        def _():
            for r in (dga_ref, dgg_ref, dlng_ref, dlnb_ref, dws_ref, db_ref, dbt_acc):
                r[...] = jnp.zeros_like(r)

        dmixed = _dot_nt(dh1_ref[...].astype(BF16), wo_ref[...])
        attn = attn_ref[...]
        _, an, ra = _rms_fwd(attn, ga_ref[...])
        dattn, dga = _rms_bwd(dmixed[:, :A], an, ra, ga_ref[...])
        dga_ref[...] += dga
        _fill_cols(scr_do, dattn)
        prod = dattn * attn
        delta = jnp.zeros_like(prod)
        for hm in _group_masks(A):
            delta = delta + jnp.where(hm, jnp.sum(jnp.where(hm, prod, 0.0), axis=-1, keepdims=True), 0.0)
        _fill_cols(scr_dl, delta)
        for di, dil in enumerate(DILS):
            if dil == 1:
                do_refs[di][0] = dattn.astype(BF16)
                dl_refs[di][0] = delta
            else:
                _split_residues(scr_do, do_refs[di], dil)
                _split_residues(scr_dl, dl_refs[di], dil)
        pmat = _group_mean_matrix()
        lng = lng_ref[...]
        uv, zv = u_ref[...], z_ref[...]
        gmv, ug, zhat, rstd, zn, mixed = _sgu_forward(uv, zv, lng, lnb_ref[...], w_ref, bt_ref[...], pmat, TMX)
        _, gmn, rg = _rms_fwd(gmv, gg_ref[...])
        dgm, dgg = _rms_bwd(dmixed[:, A:], gmn, rg, gg_ref[...])
        dgg_ref[...] += dgg
        du_ref[...] = dgm * mixed * _gelu_grad(uv)
        dmx = dgm * ug
        dmxb = dmx.astype(BF16)
        gm = _group_masks(GW)
        tri_t = _tri_mask(False)
        wst = [jnp.where(tri_t, wt_ref[g], 0.0).astype(BF16) for g in range(NG)]
        zero = jnp.zeros((CHUNK, GW), BF16)
        dzn_pieces = []
        for c in range(TMX // CHUNK):
            rs = slice(c * CHUNK, (c + 1) * CHUNK)
            dmc = dmxb[rs, :]
            znc = zn[rs, :]
            dbt_acc[...] += dmx[rs, :]
            dzn = None
            for g in range(NG):
                dws_ref[g] += _dot_nt(jnp.where(gm[g], dmc, zero), znc)
                part = jnp.where(gm[g], _dot(wst[g], dmc), 0.0)
                dzn = part if dzn is None else dzn + part
            dzn_pieces.append(dzn)
        dzn = jnp.concatenate(dzn_pieces, axis=0)
        dlng_ref[...] += jnp.sum(dzn * zhat, axis=0, keepdims=True)
        dlnb_ref[...] += jnp.sum(dzn, axis=0, keepdims=True)
        dzh = dzn * lng
        dzg = rstd * (dzh - _dot_hi(dzh, pmat) - zhat * _dot_hi(dzh * zhat, pmat))
        dz_ref[...] = dzg * _gelu_grad(zv)

        @pl.when(i == nsteps - 1)
        def _():
            tri = _tri_mask(True)
            for g in range(NG):
                dws_ref[g] = jnp.where(tri, dws_ref[g], 0.0)
            acc = dbt_acc[...]
            lane = lax.broadcasted_iota(jnp.int32, (CHUNK, LANES), 1)
            out = jnp.zeros((CHUNK, LANES), F32)
            for g in range(NG):
                sg = jnp.sum(jnp.where(gm[g], acc, 0.0), axis=-1, keepdims=True)
                out = jnp.where(lane == g, sg, out)
            db_ref[...] = out

    sd = jax.ShapeDtypeStruct
    res = pl.pallas_call(
        body, name="mix_bwd", grid=(nsteps,),
        in_specs=[_row_spec(TMX, D), _row_spec(TMX, A), _row_spec(TMX, GW), _row_spec(TMX, GW),
                  _const_spec((1, GW)), _const_spec((1, GW)), _const_spec((NG, CHUNK, CHUNK)),
                  _const_spec((NG, CHUNK, CHUNK)), _const_spec((CHUNK, GW)), _const_spec((1, A)),
                  _const_spec((1, GW)), _const_spec((D, D))],
        out_specs=[_res_spec(d, TMX, A) for d in DILS] * 2 + [_row_spec(TMX, GW), _row_spec(TMX, GW),
                   _const_spec((1, A)), _const_spec((1, GW)), _const_spec((1, GW)), _const_spec((1, GW)),
                   _const_spec((NG, CHUNK, CHUNK)), _const_spec((CHUNK, LANES))],
        out_shape=[_res_shape(s, d, A, BF16) for d in DILS] + [_res_shape(s, d, A, F32) for d in DILS]
                  + [sd((s, GW), F32), sd((s, GW), F32),
                   sd((1, A), F32), sd((1, GW), F32), sd((1, GW), F32), sd((1, GW), F32),
                   sd((NG, CHUNK, CHUNK), F32), sd((CHUNK, LANES), F32)],
        scratch_shapes=[pltpu.VMEM((CHUNK, GW), F32), _col_scratch(TMX, A), _col_scratch(TMX, A)],
        compiler_params=_cparams("arbitrary"),
    )(dh1, attn, u, z, lng, lnb, sgu_w, sgu_wt, bias_t, ga, gg, wout)
    return (res[:nd], res[nd:2 * nd]) + tuple(res[2 * nd:])


def _inproj_bwd(dqs, dks, dvs, du, dz, dh1, x, g1, win_t):
    s = x.shape[0]
    nd = len(DILS)
    nscr = sum(1 for d in DILS if d > 1)

    def body(*refs):
        parts = [refs[t * nd:(t + 1) * nd] for t in range(3)]
        du_ref, dz_ref, dh1_ref, x_ref, g_ref, w_ref, dp_ref, dx_ref, dg_ref = refs[3 * nd:3 * nd + 9]
        scr = refs[3 * nd + 9:]
        i = pl.program_id(0)
        sums = []
        for t in range(3):
            total, j = None, 0
            for di, dil in enumerate(DILS):
                if dil == 1:
                    term = parts[t][di][0]
                else:
                    term = _merge_residues(parts[t][di], scr[t * nscr + j], dil)
                    j += 1
                total = term if total is None else total + term
            sums.append(total)
        dp = jnp.concatenate([sums[0] * SCALE, sums[1], sums[2], du_ref[...], dz_ref[...]], axis=-1).astype(BF16)
        dp_ref[...] = dp
        dhn = _dot(dp, w_ref[...])
        _, xn, r1 = _rms_fwd(x_ref[...], g_ref[...])
        dres, dg = _rms_bwd(dhn, xn, r1, g_ref[...])
        dx_ref[...] = dh1_ref[...] + dres

        @pl.when(i == 0)
        def _():
            dg_ref[...] = jnp.zeros_like(dg_ref)

        dg_ref[...] += dg

    sd = jax.ShapeDtypeStruct
    return pl.pallas_call(
        body, name="inproj_bwd", grid=(s // TMX,),
        in_specs=[_res_spec(d, TMX, A) for d in DILS] * 3 + [_row_spec(TMX, GW)] * 2
                 + [_row_spec(TMX, D), _row_spec(TMX, D), _const_spec((1, D)), _const_spec((INW, D))],
        out_specs=[_row_spec(TMX, INW), _row_spec(TMX, D), _const_spec((1, D))],
        out_shape=[sd((s, INW), BF16), sd((s, D), F32), sd((1, D), F32)],
        scratch_shapes=[_col_scratch(TMX, A)] * (3 * nscr),
        compiler_params=_cparams("arbitrary"),
    )(*dqs, *dks, *dvs, du, dz, dh1, x, g1, win_t)


def _wgrad(a, b, name, bm, bn, bk=TM):
    s, m = a.shape
    n = b.shape[1]
    bm, bn = min(bm, m), min(bn, n)

    def body(a_ref, b_ref, o_ref):
        @pl.when(pl.program_id(2) == 0)
        def _():
            o_ref[...] = jnp.zeros_like(o_ref)

        o_ref[...] += _dot_tn(a_ref[...].astype(BF16), b_ref[...].astype(BF16))

    return pl.pallas_call(
        body, name=name, grid=(m // bm, n // bn, s // bk),
        in_specs=[pl.BlockSpec((bk, bm), lambda i, j, k: (k, i)), pl.BlockSpec((bk, bn), lambda i, j, k: (k, j))],
        out_specs=pl.BlockSpec((bm, bn), lambda i, j, k: (i, j)),
        out_shape=jax.ShapeDtypeStruct((m, n), F32),
        compiler_params=_cparams("arbitrary", "arbitrary", "arbitrary"),
    )(a, b)


def _adamw_math(w, g, m, v):
    m = B1 * m + (1.0 - B1) * g
    v = B2 * v + (1.0 - B2) * (g * g)
    m_hat = m / (1.0 - B1 ** STEP)
    v_hat = v / (1.0 - B2 ** STEP)
    delta = -LR * (m_hat / (jnp.sqrt(v_hat) + AEPS) + WD * w)
    return delta, m, v


def _adamw(w, g, m, v, name):
    rows, cols = w.shape
    br = min(rows, 256)
    while rows % br:
        br -= 8

    def body(w_ref, g_ref, m_ref, v_ref, d_ref, mo_ref, vo_ref):
        d, mn, vn = _adamw_math(w_ref[...], g_ref[...], m_ref[...], v_ref[...])
        d_ref[...] = d
        mo_ref[...] = mn
        vo_ref[...] = vn

    spec = _row_spec(br, cols)
    sd = jax.ShapeDtypeStruct((rows, cols), F32)
    return pl.pallas_call(
        body, name=name, grid=(rows // br,), in_specs=[spec] * 4, out_specs=[spec] * 3,
        out_shape=[sd, sd, sd], compiler_params=_cparams("arbitrary"),
    )(w, g, m, v)


def _local_step(x, target, small, win_t, wout, wff1_t, wff2):
    slopes = jnp.asarray(_alibi_slopes(NH))
    hn1, q, k, v, u, z = _inproj_fwd(x, small["norm1_g"], win_t)
    outs, lses = [], []
    for i, dil in enumerate(DILS):
        o, l = _attn_fwd(q[i], k[i], v[i], slopes, dil)
        outs.append(o)
        lses.append(l)
    attn, lse, mixed, h1 = _mix_fwd(outs, lses, u, z, x, small["ln_g"], small["ln_b"], small["sgu_w"],
                                    small["bias_t"], small["attn_out_g"], small["gmlp_out_g"], wout)
    hn2, rf, act, dh2, loss, dgf = _mlp_fwd(h1, small["norm2_g"], wff1_t, wff2, small["final_norm_g"], target)
    df, dh1, dg2 = _mlp_bwd(dh2, rf, h1, small["norm2_g"], wff1_t, wff2)
    (do, delta, du, dz, dga, dgg, dlng, dlnb, dws, db) = _mix_bwd(
        dh1, attn, u, z, small["ln_g"], small["ln_b"], small["sgu_w"], small["sgu_wt"], small["bias_t"],
        small["attn_out_g"], small["gmlp_out_g"], wout)
    dqs, dks, dvs = [], [], []
    for i, dil in enumerate(DILS):
        dqs.append(_attn_bwd_dq(q[i], k[i], v[i], do[i], lse[i], delta[i], slopes, dil))
        dk, dv = _attn_bwd_dkv(q[i], k[i], v[i], do[i], lse[i], delta[i], slopes, dil)
        dks.append(dk)
        dvs.append(dv)
    dproj, dx, dg1 = _inproj_bwd(dqs, dks, dvs, du, dz, dh1, x, small["norm1_g"], win_t)
    gwin_t = _wgrad(dproj, hn1, "wgrad_in", INW // 2, D)
    gwout = _wgrad(mixed, dh1, "wgrad_out", D, D)
    gwff1_t = _wgrad(df, hn2, "wgrad_ff1", 1024, D)
    gwff2 = _wgrad(act, dh2, "wgrad_ff2", 1024, D)
    small_grads = dict(norm1_g=dg1, ln_g=dlng, ln_b=dlnb, sgu_w=dws, sgu_b=db[:, :NG].T,
                       attn_out_g=dga, gmlp_out_g=dgg, norm2_g=dg2, final_norm_g=dgf)
    return loss[0, 0], dx, small_grads, (gwin_t, gwout, gwff1_t, gwff2)


ANY = pl.BlockSpec(memory_space=pl.ANY)
HALF_ROWS = SHARD_ROWS // 2
ADD_ROWS = 376
D2D_SPLIT = 4


def _position():
    return lax.axis_index("x"), lax.axis_index("y"), lax.axis_index("c")


def _other_chips(x, y):
    return [(1 - x, y), (x, 1 - y), (1 - x, 1 - y)]


def _remote(src, dst, send_sem, recv_sem, device):
    return pltpu.make_async_remote_copy(src_ref=src, dst_ref=dst, send_sem=send_sem, recv_sem=recv_sem,
                                        device_id=device, device_id_type=MESH)


def _gather_weights(wp):
    def body(w_ref, out_ref, send_sems, recv_sems, local_sem):
        x, y, c = _position()
        me = 2 * x + y
        mine = pltpu.make_async_copy(w_ref, out_ref.at[me], local_sem)
        mine.start()
        sends = []
        for k, (px, py) in enumerate(_other_chips(x, y)):
            cp = _remote(w_ref, out_ref.at[me], send_sems.at[k], recv_sems.at[k], (px, py, c))
            cp.start()
            sends.append(cp)
        for k, (px, py) in enumerate(_other_chips(x, y)):
            _remote(w_ref, out_ref.at[2 * px + py], send_sems.at[k], recv_sems.at[k], (px, py, c)).wait_recv()
        for cp in sends:
            cp.wait_send()
        mine.wait()

    return pl.pallas_call(
        body, name="gather_weights", in_specs=[ANY], out_specs=ANY,
        out_shape=jax.ShapeDtypeStruct((NCHIP,) + wp.shape, wp.dtype),
        scratch_shapes=[pltpu.SemaphoreType.DMA((3,)), pltpu.SemaphoreType.DMA((3,)), pltpu.SemaphoreType.DMA],
        compiler_params=pltpu.CompilerParams(has_side_effects=True),
    )(wp)


def _exchange_halves(gall):
    def body(g_ref, own_ref, got_ref, out_buf, in_buf, load_sems, send_sems, recv_sems, store_sems, local_sem):
        x, y, c = _position()
        keep = g_ref.at[:, pl.ds(c * HALF_ROWS, HALF_ROWS), :]
        local = pltpu.make_async_copy(keep, own_ref, local_sem)
        local.start()
        loads = [pltpu.make_async_copy(g_ref.at[j, pl.ds((1 - c) * HALF_ROWS, HALF_ROWS), :], out_buf.at[j],
                                       load_sems.at[j]) for j in range(NCHIP)]
        for ld in loads:
            ld.start()
        sends = []
        for j in range(NCHIP):
            loads[j].wait()
            cp = _remote(out_buf.at[j], in_buf.at[j], send_sems.at[j], recv_sems.at[j], (x, y, 1 - c))
            cp.start()
            sends.append(cp)
        stores = []
        for j in range(NCHIP):
            sends[j].wait_recv()
            st = pltpu.make_async_copy(in_buf.at[j], got_ref.at[j], store_sems.at[j])
            st.start()
            stores.append(st)
        for j in range(NCHIP):
            sends[j].wait_send()
            stores[j].wait()
        local.wait()

    sd = jax.ShapeDtypeStruct((NCHIP, HALF_ROWS, D), F32)
    dma = pltpu.SemaphoreType.DMA((NCHIP,))
    return pl.pallas_call(
        body, name="exchange_halves", in_specs=[ANY], out_specs=[ANY, ANY], out_shape=[sd, sd],
        scratch_shapes=[pltpu.VMEM((NCHIP, HALF_ROWS, D), F32), pltpu.VMEM((NCHIP, HALF_ROWS, D), F32),
                        dma, dma, dma, dma, pltpu.SemaphoreType.DMA],
        compiler_params=pltpu.CompilerParams(has_side_effects=True, vmem_limit_bytes=VMEM_LIMIT),
    )(gall)


def _scatter_to_owners(part):
    def body(p_ref, own_ref, got_ref, send_sems, recv_sems, local_sem):
        x, y, c = _position()
        local = pltpu.make_async_copy(p_ref.at[2 * x + y], own_ref, local_sem)
        local.start()
        sends = []
        for k, (px, py) in enumerate(_other_chips(x, y)):
            cp = _remote(p_ref.at[2 * px + py], got_ref.at[k], send_sems.at[k], recv_sems.at[k], (px, py, c))
            cp.start()
            sends.append(cp)
        for cp in sends:
            cp.wait()
        local.wait()

    return pl.pallas_call(
        body, name="scatter_to_owners", in_specs=[ANY], out_specs=[ANY, ANY],
        out_shape=[jax.ShapeDtypeStruct((HALF_ROWS, D), F32), jax.ShapeDtypeStruct((3, HALF_ROWS, D), F32)],
        scratch_shapes=[pltpu.SemaphoreType.DMA((3,)), pltpu.SemaphoreType.DMA((3,)), pltpu.SemaphoreType.DMA],
        compiler_params=pltpu.CompilerParams(has_side_effects=True),
    )(part)


def _share_with_sibling(half):
    n = D2D_SPLIT
    part = HALF_ROWS // n

    def body(h_ref, out_ref, buf, load_sems, send_sems, recv_sems, local_sem):
        x, y, c = _position()
        local = pltpu.make_async_copy(h_ref, out_ref.at[c], local_sem)
        local.start()
        loads = []
        for t in range(n):
            rows = pl.ds(t * part, part)
            ld = pltpu.make_async_copy(h_ref.at[rows, :], buf.at[rows, :], load_sems.at[t])
            ld.start()
            loads.append(ld)
        copies = []
        for t in range(n):
            rows = pl.ds(t * part, part)
            loads[t].wait()
            cp = _remote(buf.at[rows, :], out_ref.at[c, rows, :], send_sems.at[t], recv_sems.at[t], (x, y, 1 - c))
            cp.start()
            copies.append(cp)
        for t, cp in enumerate(copies):
            cp.wait_send()
            rows = pl.ds(t * part, part)
            _remote(buf.at[rows, :], out_ref.at[1 - c, rows, :], send_sems.at[t], recv_sems.at[t],
                    (x, y, 1 - c)).wait_recv()
        local.wait()

    dma = pltpu.SemaphoreType.DMA((n,))
    return pl.pallas_call(
        body, name="share_with_sibling", in_specs=[ANY], out_specs=ANY,
        out_shape=jax.ShapeDtypeStruct((2, HALF_ROWS, D), F32),
        scratch_shapes=[pltpu.VMEM((HALF_ROWS, D), F32), dma, dma, dma, pltpu.SemaphoreType.DMA],
        compiler_params=pltpu.CompilerParams(has_side_effects=True, vmem_limit_bytes=VMEM_LIMIT),
    )(half)


def _add_slabs(terms, out_rows, name):
    n = len(terms)

    def body(*refs):
        acc = refs[0][...]
        for r in refs[1:n]:
            acc = acc + r[...]
        refs[n][...] = acc

    specs = [pl.BlockSpec((ADD_ROWS, D), functools.partial(lambda i, off: (i + off, 0), off=first // ADD_ROWS))
             for _, first in terms]
    return pl.pallas_call(
        body, name=name, grid=(out_rows // ADD_ROWS,), in_specs=specs, out_specs=_row_spec(ADD_ROWS, D),
        out_shape=jax.ShapeDtypeStruct((out_rows, D), F32), compiler_params=_cparams("arbitrary"),
    )(*[a for a, _ in terms])


def _reduce_grads(gall):
    own, got = _exchange_halves(gall)
    rows = NCHIP * HALF_ROWS
    chip = _add_slabs([(own.reshape(rows, D), 0), (got.reshape(rows, D), 0)], rows, "add_sibling")
    mine, others = _scatter_to_owners(chip.reshape(NCHIP, HALF_ROWS, D))
    flat = others.reshape(3 * HALF_ROWS, D)
    half = _add_slabs([(mine, 0), (flat, 0), (flat, HALF_ROWS), (flat, 2 * HALF_ROWS)], HALF_ROWS, "add_chips")
    return _share_with_sibling(half).reshape(SHARD_ROWS, D)


SMALL_SIZES = (("norm1_g", D), ("sgu_ln_g", GW), ("sgu_ln_b", GW), ("sgu_w", NG * CHUNK * CHUNK),
               ("sgu_b", NG * CHUNK), ("attn_out_g", A), ("gmlp_out_g", GW), ("norm2_g", D),
               ("final_norm_g", D))
SMALL_ROWS = sum(n for _, n in SMALL_SIZES) // LANES
NDEV = 8


def _pack_small(tree):
    return jnp.concatenate([tree[n].reshape(-1) for n, _ in SMALL_SIZES]).reshape(SMALL_ROWS, LANES)


def _unpack_small(pack, shapes):
    flat = pack.reshape(-1)
    out, off = {}, 0
    for n, size in SMALL_SIZES:
        out[n] = flat[off:off + size].reshape(shapes[n])
        off += size
    return out


def _small_allreduce_adamw(gpack, wpack, mpack, vpack):
    def body(g_ref, w_ref, m_ref, v_ref, go_ref, d_ref, mo_ref, vo_ref, slots, send_sems, recv_sems):
        x, y, c = _position()
        me = 4 * x + 2 * y + c
        slots[me] = g_ref[...]
        sends = []
        for k in range(1, NDEV):
            kx, ky, kc = (k >> 2) & 1, (k >> 1) & 1, k & 1
            peer = (1 - x if kx else x, 1 - y if ky else y, 1 - c if kc else c)
            cp = _remote(g_ref, slots.at[me], send_sems.at[k - 1], recv_sems.at[k - 1], peer)
            cp.start()
            sends.append((cp, peer))
        for k in range(1, NDEV):
            _, (px, py, pc) = sends[k - 1]
            _remote(g_ref, slots.at[4 * px + 2 * py + pc], send_sems.at[k - 1], recv_sems.at[k - 1],
                    (px, py, pc)).wait_recv()
        for cp, _ in sends:
            cp.wait_send()
        total = slots[0]
        for k in range(1, NDEV):
            total = total + slots[k]
        go_ref[...] = total
        d, mn, vn = _adamw_math(w_ref[...], total, m_ref[...], v_ref[...])
        d_ref[...] = d
        mo_ref[...] = mn
        vo_ref[...] = vn

    sd = jax.ShapeDtypeStruct((SMALL_ROWS, LANES), F32)
    vm = pl.BlockSpec(memory_space=pltpu.VMEM)
    return pl.pallas_call(
        body, name="small_allreduce_adamw", in_specs=[vm] * 4, out_specs=[vm] * 4, out_shape=[sd] * 4,
        scratch_shapes=[pltpu.VMEM((NDEV, SMALL_ROWS, LANES), F32), pltpu.SemaphoreType.DMA((NDEV - 1,)),
                        pltpu.SemaphoreType.DMA((NDEV - 1,))],
        compiler_params=pltpu.CompilerParams(has_side_effects=True),
    )(gpack, wpack, mpack, vpack)


def kernel(x, norm1_g, w_in, sgu_ln_g, sgu_ln_b, sgu_w, sgu_b, attn_out_g, gmlp_out_g, w_out, norm2_g, w_ff1, w_ff2, final_norm_g, loss_target, m_norm1_g, m_w_in, m_sgu_ln_g, m_sgu_ln_b, m_sgu_w, m_sgu_b, m_attn_out_g, m_gmlp_out_g, m_w_out, m_norm2_g, m_w_ff1, m_w_ff2, m_final_norm_g, v_norm1_g, v_w_in, v_sgu_ln_g, v_sgu_ln_b, v_sgu_w, v_sgu_b, v_attn_out_g, v_gmlp_out_g, v_w_out, v_norm2_g, v_w_ff1, v_w_ff2, v_final_norm_g):
    names = [n for n, _ in SMALL_SIZES]
    w_small = dict(norm1_g=norm1_g, sgu_ln_g=sgu_ln_g, sgu_ln_b=sgu_ln_b, sgu_w=sgu_w, sgu_b=sgu_b,
                   attn_out_g=attn_out_g, gmlp_out_g=gmlp_out_g, norm2_g=norm2_g, final_norm_g=final_norm_g)
    m_small = dict(norm1_g=m_norm1_g, sgu_ln_g=m_sgu_ln_g, sgu_ln_b=m_sgu_ln_b, sgu_w=m_sgu_w, sgu_b=m_sgu_b,
                   attn_out_g=m_attn_out_g, gmlp_out_g=m_gmlp_out_g, norm2_g=m_norm2_g,
                   final_norm_g=m_final_norm_g)
    v_small = dict(norm1_g=v_norm1_g, sgu_ln_g=v_sgu_ln_g, sgu_ln_b=v_sgu_ln_b, sgu_w=v_sgu_w, sgu_b=v_sgu_b,
                   attn_out_g=v_attn_out_g, gmlp_out_g=v_gmlp_out_g, norm2_g=v_norm2_g,
                   final_norm_g=v_final_norm_g)
    shapes = {n: w_small[n].shape for n in names}

    r_in, r_out, r_ff = INW // NCHIP, D // NCHIP, DFF // NCHIP
    packed = jnp.concatenate([w_in[0].T, w_out[0], w_ff1[0].T, w_ff2[0]], axis=0).astype(BF16)
    wall = _gather_weights(packed)
    o1, o2, o3 = r_in, r_in + r_out, r_in + r_out + r_ff
    win_t = wall[:, :o1].reshape(INW, D)
    wout = wall[:, o1:o2].reshape(D, D)
    wff1_t = wall[:, o2:o3].reshape(DFF, D)
    wff2 = wall[:, o3:].reshape(DFF, D)

    small = dict(
        norm1_g=norm1_g, ln_g=sgu_ln_g.reshape(1, GW), ln_b=sgu_ln_b.reshape(1, GW), sgu_w=sgu_w[0],
        sgu_wt=jnp.swapaxes(sgu_w[0], 1, 2), bias_t=jnp.repeat(sgu_b[0].T, DH, axis=1),
        attn_out_g=attn_out_g, gmlp_out_g=gmlp_out_g, norm2_g=norm2_g, final_norm_g=final_norm_g.reshape(1, D))
    loss_part, dx, sg, (gwin_t, gwout, gwff1_t, gwff2) = _local_step(
        x[0], loss_target[0], small, win_t, wout, wff1_t, wff2)
    loss = lax.psum(loss_part, ("x", "y", "c"))

    gall = jnp.concatenate([gwin_t.reshape(NCHIP, r_in, D), gwout.reshape(NCHIP, r_out, D),
                            gwff1_t.reshape(NCHIP, r_ff, D), gwff2.reshape(NCHIP, r_ff, D)], axis=1)
    gsum = _reduce_grads(gall)
    g_big = dict(w_in=gsum[:o1].T, w_out=gsum[o1:o2], w_ff1=gsum[o2:o3].T, w_ff2=gsum[o3:])
    w_big = dict(w_in=(w_in, m_w_in, v_w_in), w_out=(w_out, m_w_out, v_w_out),
                 w_ff1=(w_ff1, m_w_ff1, v_w_ff1), w_ff2=(w_ff2, m_w_ff2, v_w_ff2))
    grads, deltas, new_m, new_v = {}, {}, {}, {}
    for n, (w, m, v) in w_big.items():
        d, mn, vn = _adamw(w[0], g_big[n], m[0], v[0], "adamw_" + n)
        grads[n], deltas[n], new_m[n], new_v[n] = g_big[n][None], d[None], mn[None], vn[None]

    g_small = dict(norm1_g=sg["norm1_g"], sgu_ln_g=sg["ln_g"], sgu_ln_b=sg["ln_b"], sgu_w=sg["sgu_w"],
                   sgu_b=sg["sgu_b"], attn_out_g=sg["attn_out_g"], gmlp_out_g=sg["gmlp_out_g"],
                   norm2_g=sg["norm2_g"], final_norm_g=sg["final_norm_g"])
    packs = _small_allreduce_adamw(_pack_small(g_small), _pack_small(w_small), _pack_small(m_small),
                                   _pack_small(v_small))
    for tree, pack in zip((grads, deltas, new_m, new_v), packs):
        tree.update(_unpack_small(pack, shapes))

    order = ["norm1_g", "w_in", "sgu_ln_g", "sgu_ln_b", "sgu_w", "sgu_b", "attn_out_g", "gmlp_out_g", "w_out",
             "norm2_g", "w_ff1", "w_ff2", "final_norm_g"]
    return (loss, dx[None], *[grads[n] for n in order], *[deltas[n] for n in order],
            *[new_m[n] for n in order], *[new_v[n] for n in order])
```

```python
import functools
import math

import numpy as np
import jax
import jax.numpy as jnp
from jax import lax
from jax.experimental import pallas as pl
from jax.experimental.pallas import tpu as pltpu

F32 = jnp.float32
BF16 = jnp.bfloat16

D = 1024
NH = 12
DH = 64
A = NH * DH
NG = 4
GW = NG * DH
INW = 3 * A + 2 * GW
DFF = 4 * D
CHUNK = 128
PATTERNS = ((128, 1), (512, 4), (2048, 16))
EPS = 1e-6
SCALE = DH ** -0.5
NEG = -1e30

LR, B1, B2, AEPS, WD, STEP = 0.001, 0.9, 0.999, 1e-08, 0.01, 10

TM = 512
TMX = 256
ATT_ROWS = 1024
FF_CH = 1024
LANES = 128
NCHIP = 4
SHARD_ROWS = INW // NCHIP + D // NCHIP + DFF // NCHIP + DFF // NCHIP
VMEM_LIMIT = 56 * 1024 * 1024
MESH = pl.DeviceIdType.MESH


def _cparams(*sem, **kw):
    return pltpu.CompilerParams(dimension_semantics=sem if sem else None,
                                vmem_limit_bytes=VMEM_LIMIT, **kw)


def _dot(a, b):
    return jnp.dot(a, b, preferred_element_type=F32)


def _dot_nt(a, b):
    return lax.dot_general(a, b, (((1,), (1,)), ((), ())), preferred_element_type=F32)


def _dot_tn(a, b):
    return lax.dot_general(a, b, (((0,), (0,)), ((), ())), preferred_element_type=F32)


def _dot_hi(a, b):
    return jnp.dot(a, b, preferred_element_type=F32, precision=lax.Precision.HIGHEST)


def _alibi_slopes(n):
    def pow2(m):
        start = 2.0 ** (-8.0 / m)
        return [start ** (i + 1) for i in range(m)]
    if math.log2(n).is_integer():
        s = pow2(n)
    else:
        c = 2 ** int(math.floor(math.log2(n)))
        s = pow2(c) + pow2(2 * c)[0::2][: n - c]
    return np.asarray(s, dtype=np.float32)


def _rms_fwd(v, g):
    r = lax.rsqrt(jnp.mean(v * v, axis=-1, keepdims=True) + EPS)
    vn = v * r
    return vn * g, vn, r


def _rms_bwd(dy, vn, r, g):
    w = dy * g
    dv = r * (w - vn * jnp.mean(w * vn, axis=-1, keepdims=True))
    return dv, jnp.sum(dy * vn, axis=0, keepdims=True)


_K0 = math.sqrt(2.0 / math.pi)
_K1 = 0.044715


def _gelu(v):
    return 0.5 * v * (1.0 + jnp.tanh(_K0 * (v + _K1 * (v * v * v))))


def _gelu_grad(v):
    t = jnp.tanh(_K0 * (v + _K1 * (v * v * v)))
    return 0.5 * (1.0 + t) + 0.5 * v * (1.0 - t * t) * (_K0 * (1.0 + 3.0 * _K1 * v * v))


def _row_spec(rows, cols):
    return pl.BlockSpec((rows, cols), lambda i: (i, 0))


def _const_spec(shape):
    nd = len(shape)
    return pl.BlockSpec(shape, lambda i: (0,) * nd, pipeline_mode=pl.Buffered(1))


DILS = tuple(d for _, d in PATTERNS)


def _fill_cols(scr, value):
    for cb in range(value.shape[1] // LANES):
        scr[cb] = value[:, cb * LANES:(cb + 1) * LANES]


def _split_residues(scr, out_ref, dil):
    nb, rows, _ = scr.shape
    for r in range(dil):
        for cb in range(nb):
            piece = scr.at[cb][pl.ds(r, rows // dil, stride=dil), :]
            out_ref[r, :, cb * LANES:(cb + 1) * LANES] = piece.astype(out_ref.dtype)


def _merge_residues(in_ref, scr, dil):
    nb, rows, _ = scr.shape
    for r in range(dil):
        for cb in range(nb):
            scr.at[cb][pl.ds(r, rows // dil, stride=dil), :] = in_ref[r, :, cb * LANES:(cb + 1) * LANES]
    return jnp.concatenate([scr[cb] for cb in range(nb)], axis=-1)


def _col_scratch(rows, width):
    return pltpu.VMEM((width // LANES, rows, LANES), F32)


def _res_spec(dil, rows, width):
    return pl.BlockSpec((dil, rows // dil, width), lambda i: (0, i, 0))


def _res_shape(s, dil, width, dtype):
    return jax.ShapeDtypeStruct((dil, s // dil, width), dtype)


def _inproj_fwd(x, g1, win_t):
    s = x.shape[0]
    nd = len(DILS)

    def body(x_ref, g_ref, w_ref, hn_ref, *rest):
        qkv_refs = rest[:3 * nd]
        u_ref, z_ref, scr = rest[3 * nd:]
        hn, _, _ = _rms_fwd(x_ref[...], g_ref[...])
        hn = hn.astype(BF16)
        hn_ref[...] = hn
        for t in range(3):
            seg = _dot_nt(hn, w_ref[t * A:(t + 1) * A, :])
            seg = seg * SCALE if t == 0 else seg
            _fill_cols(scr, seg)
            for di, dil in enumerate(DILS):
                if dil == 1:
                    qkv_refs[t * nd + di][0] = seg.astype(BF16)
                else:
                    _split_residues(scr, qkv_refs[t * nd + di], dil)
        u_ref[...] = _dot_nt(hn, w_ref[3 * A:3 * A + GW, :])
        z_ref[...] = _dot_nt(hn, w_ref[3 * A + GW:INW, :])

    res = pl.pallas_call(
        body, name="inproj_fwd", grid=(s // TM,),
        in_specs=[_row_spec(TM, D), _const_spec((1, D)), _const_spec((INW, D))],
        out_specs=[_row_spec(TM, D)] + [_res_spec(d, TM, A) for _ in range(3) for d in DILS]
                  + [_row_spec(TM, GW), _row_spec(TM, GW)],
        out_shape=[jax.ShapeDtypeStruct((s, D), BF16)] + [_res_shape(s, d, A, BF16) for _ in range(3) for d in DILS]
                  + [jax.ShapeDtypeStruct((s, GW), F32)] * 2,
        scratch_shapes=[_col_scratch(TM, A)],
        compiler_params=_cparams("arbitrary"),
    )(x, g1, win_t)
    hn1 = res[0]
    q, k, v = (res[1 + t * nd:1 + (t + 1) * nd] for t in range(3))
    return hn1, q, k, v, res[-2], res[-1]


def _att_geometry(s, dil):
    length = s // dil
    rows = min(length, ATT_ROWS)
    return length, rows, length // rows, rows // CHUNK


def _stack_heads(t):
    lane = lax.broadcasted_iota(jnp.int32, t.shape, 1)
    zero = jnp.zeros_like(t)
    return jnp.concatenate([jnp.where(lane < DH, t, zero), jnp.where(lane >= DH, t, zero)], axis=0)


def _stack_cols(t):
    return jnp.concatenate([t[:, 0:1], t[:, DH:DH + 1]], axis=0)


def _unstack_heads(t2):
    n = t2.shape[0] // 2
    lane = lax.broadcasted_iota(jnp.int32, (n, LANES), 1)
    return jnp.where(lane < DH, t2[:n], t2[n:])


def _query_window_bias(s0, s1, dil, first):
    row = lax.broadcasted_iota(jnp.int32, (2 * CHUNK, 2 * CHUNK), 0)
    col = lax.broadcasted_iota(jnp.int32, (2 * CHUNK, 2 * CHUNK), 1)
    steps = (row & (CHUNK - 1)) + CHUNK - col
    valid = (steps >= 0) & (steps <= CHUNK)
    if first:
        valid = valid & (col >= CHUNK)
    slope = jnp.where(row < CHUNK, s0, s1)
    return jnp.where(valid, -slope * (steps * dil).astype(F32), NEG)


def _key_block_bias(s0, s1, dil, last):
    key = lax.broadcasted_iota(jnp.int32, (CHUNK, 4 * CHUNK), 0)
    col = lax.broadcasted_iota(jnp.int32, (CHUNK, 4 * CHUNK), 1)
    wq = col & (2 * CHUNK - 1)
    steps = wq - key
    valid = (steps >= 0) & (steps <= CHUNK)
    if last:
        valid = valid & (wq < CHUNK)
    slope = jnp.where(col < 2 * CHUNK, s0, s1)
    return jnp.where(valid, -slope * (steps * dil).astype(F32), NEG)


def _head_rows(t):
    row = lax.broadcasted_iota(jnp.int32, (8, LANES), 0)
    lane = lax.broadcasted_iota(jnp.int32, (8, LANES), 1)
    pick = jnp.where(((row == 0) & (lane == 0)) | ((row == 1) & (lane == DH)), 1.0, 0.0).astype(BF16)
    hi = t.astype(BF16)
    rest = t - hi.astype(F32)
    mid = rest.astype(BF16)
    low = (rest - mid.astype(F32)).astype(BF16)
    return _dot_nt(pick, hi) + _dot_nt(pick, mid) + _dot_nt(pick, low)


def _att_specs(dil, rows, nsub, nblk):
    main = pl.BlockSpec((None, rows, LANES), lambda r, hp, c: (r, c, hp))
    prev = pl.BlockSpec((None, CHUNK, LANES), lambda r, hp, c: (r, jnp.maximum(c * nsub - 1, 0), hp))
    nxt = pl.BlockSpec((None, CHUNK, LANES), lambda r, hp, c: (r, jnp.minimum((c + 1) * nsub, nblk - 1), hp))
    return main, prev, nxt


def _row_start(i):
    return i * CHUNK if isinstance(i, int) else pl.multiple_of(i * CHUNK, CHUNK)


def _attn_fwd(q, k, v, slopes, dil):
    length = q.shape[1]
    _, rows, nch, nsub = _att_geometry(length * dil, dil)
    main, prev, _ = _att_specs(dil, rows, nsub, length // CHUNK)

    def body(sl_ref, q_ref, k_ref, v_ref, kh_ref, vh_ref, o_ref, lse_ref, kbuf, vbuf, bias_buf):
        hp = pl.program_id(1)
        ch = pl.program_id(2)
        kbuf[0:CHUNK, :] = kh_ref[...]
        kbuf[CHUNK:, :] = k_ref[...]
        vbuf[0:CHUNK, :] = vh_ref[...]
        vbuf[CHUNK:, :] = v_ref[...]
        s0, s1 = sl_ref[2 * hp], sl_ref[2 * hp + 1]

        def block(i, bias):
            row = _row_start(i)
            rs = pl.ds(row, CHUNK)
            q2 = _stack_heads(q_ref[rs, :])
            kw = kbuf[pl.ds(row, 2 * CHUNK), :]
            vw = vbuf[pl.ds(row, 2 * CHUNK), :]
            sc = _dot_nt(q2, kw) + bias
            m = jnp.max(sc, axis=-1, keepdims=True)
            p = jnp.exp(sc - m)
            l = jnp.sum(p, axis=-1, keepdims=True)
            o2 = _dot(p.astype(BF16), vw) * (1.0 / l)
            o_ref[rs, :] = _unstack_heads(o2)
            lse_ref[rs, :] = _unstack_heads(jnp.broadcast_to(m + jnp.log(l), (2 * CHUNK, LANES)))

        bias_buf[...] = _query_window_bias(s0, s1, dil, False)

        @pl.when(ch == 0)
        def _():
            block(0, _query_window_bias(s0, s1, dil, True))

        @pl.when(ch != 0)
        def _():
            block(0, bias_buf[...])

        for i in range(1, nsub):
            block(i, bias_buf[...])

    sd = jax.ShapeDtypeStruct((dil, length, A), F32)
    return pl.pallas_call(
        body, name=f"attn_fwd_d{dil}", grid=(dil, NH // 2, nch),
        in_specs=[pl.BlockSpec(memory_space=pltpu.SMEM), main, main, main, prev, prev],
        out_specs=[main, main], out_shape=[sd, sd],
        scratch_shapes=[pltpu.VMEM((rows + CHUNK, LANES), BF16), pltpu.VMEM((rows + CHUNK, LANES), BF16),
                        pltpu.VMEM((2 * CHUNK, 2 * CHUNK), F32)],
        compiler_params=_cparams("arbitrary", "arbitrary", "arbitrary"),
    )(slopes, q, k, v, k, v)


def _attn_bwd_dq(q, k, v, do, lse, delta, slopes, dil):
    length = q.shape[1]
    _, rows, nch, nsub = _att_geometry(length * dil, dil)
    main, prev, _ = _att_specs(dil, rows, nsub, length // CHUNK)

    def body(sl_ref, q_ref, k_ref, v_ref, do_ref, lse_ref, dl_ref, kh_ref, vh_ref, dq_ref, kbuf, vbuf, bias_buf):
        hp = pl.program_id(1)
        ch = pl.program_id(2)
        kbuf[0:CHUNK, :] = kh_ref[...]
        kbuf[CHUNK:, :] = k_ref[...]
        vbuf[0:CHUNK, :] = vh_ref[...]
        vbuf[CHUNK:, :] = v_ref[...]
        s0, s1 = sl_ref[2 * hp], sl_ref[2 * hp + 1]

        def block(i, bias):
            row = _row_start(i)
            rs = pl.ds(row, CHUNK)
            q2 = _stack_heads(q_ref[rs, :])
            do2 = _stack_heads(do_ref[rs, :])
            lse2 = _stack_cols(lse_ref[rs, :])
            dl2 = _stack_cols(dl_ref[rs, :])
            kw = kbuf[pl.ds(row, 2 * CHUNK), :]
            vw = vbuf[pl.ds(row, 2 * CHUNK), :]
            p = jnp.exp(_dot_nt(q2, kw) + bias - lse2)
            ds = p * (_dot_nt(do2, vw) - dl2)
            dq_ref[rs, :] = _unstack_heads(_dot(ds.astype(BF16), kw))

        bias_buf[...] = _query_window_bias(s0, s1, dil, False)

        @pl.when(ch == 0)
        def _():
            block(0, _query_window_bias(s0, s1, dil, True))

        @pl.when(ch != 0)
        def _():
            block(0, bias_buf[...])

        for i in range(1, nsub):
            block(i, bias_buf[...])

    return pl.pallas_call(
        body, name=f"attn_dq_d{dil}", grid=(dil, NH // 2, nch),
        in_specs=[pl.BlockSpec(memory_space=pltpu.SMEM), main, main, main, main, main, main, prev, prev],
        out_specs=main, out_shape=jax.ShapeDtypeStruct((dil, length, A), F32),
        scratch_shapes=[pltpu.VMEM((rows + CHUNK, LANES), BF16), pltpu.VMEM((rows + CHUNK, LANES), BF16),
                        pltpu.VMEM((2 * CHUNK, 2 * CHUNK), F32)],
        compiler_params=_cparams("arbitrary", "arbitrary", "arbitrary"),
    )(slopes, q, k, v, do, lse, delta, k, v)


def _attn_bwd_dkv(q, k, v, do, lse, delta, slopes, dil):
    length = q.shape[1]
    _, rows, nch, nsub = _att_geometry(length * dil, dil)
    main, _, nxt = _att_specs(dil, rows, nsub, length // CHUNK)

    def body(sl_ref, k_ref, v_ref, q_ref, do_ref, lse_ref, dl_ref, qh_ref, doh_ref, lseh_ref, dlh_ref,
             dk_ref, dv_ref, qbuf, dobuf, lse_rows, dl_rows, bias_buf):
        hp = pl.program_id(1)
        ch = pl.program_id(2)
        for buf, main_ref, halo_ref in ((qbuf, q_ref, qh_ref), (dobuf, do_ref, doh_ref)):
            buf[0:rows, :] = main_ref[...]
            buf[rows:, :] = halo_ref[...]
        for buf, main_ref, halo_ref in ((lse_rows, lse_ref, lseh_ref), (dl_rows, dl_ref, dlh_ref)):
            buf[:, 0:rows] = _head_rows(main_ref[...])
            buf[:, rows:] = _head_rows(halo_ref[...])
        s0, s1 = sl_ref[2 * hp], sl_ref[2 * hp + 1]

        def block(i, bias):
            row = _row_start(i)
            rs = pl.ds(row, CHUNK)
            win = pl.ds(row, 2 * CHUNK)
            kc = k_ref[rs, :]
            vc = v_ref[rs, :]
            q2 = _stack_heads(qbuf[win, :])
            do2 = _stack_heads(dobuf[win, :])
            cols = slice(i * CHUNK, (i + 2) * CHUNK)
            lse2 = jnp.concatenate([lse_rows[0:1, cols], lse_rows[1:2, cols]], axis=1)
            dl2 = jnp.concatenate([dl_rows[0:1, cols], dl_rows[1:2, cols]], axis=1)
            pt = jnp.exp(_dot_nt(kc, q2) + bias - lse2)
            dst = pt * (_dot_nt(vc, do2) - dl2)
            dv_ref[rs, :] = _dot(pt.astype(BF16), do2)
            dk_ref[rs, :] = _dot(dst.astype(BF16), q2)

        bias_buf[...] = _key_block_bias(s0, s1, dil, False)

        for i in range(nsub - 1):
            block(i, bias_buf[...])

        @pl.when(ch != nch - 1)
        def _():
            block(nsub - 1, bias_buf[...])

        @pl.when(ch == nch - 1)
        def _():
            block(nsub - 1, _key_block_bias(s0, s1, dil, True))

    sd = jax.ShapeDtypeStruct((dil, length, A), F32)
    return pl.pallas_call(
        body, name=f"attn_dkv_d{dil}", grid=(dil, NH // 2, nch),
        in_specs=[pl.BlockSpec(memory_space=pltpu.SMEM), main, main, main, main, main, main, nxt, nxt, nxt, nxt],
        out_specs=[main, main], out_shape=[sd, sd],
        scratch_shapes=[pltpu.VMEM((rows + CHUNK, LANES), BF16), pltpu.VMEM((rows + CHUNK, LANES), BF16),
                        pltpu.VMEM((8, rows + CHUNK), F32), pltpu.VMEM((8, rows + CHUNK), F32),
                        pltpu.VMEM((CHUNK, 4 * CHUNK), F32)],
        compiler_params=_cparams("arbitrary", "arbitrary", "arbitrary"),
    )(slopes, k, v, q, do, lse, delta, q, do, lse, delta)


def _group_masks(width):
    lane = lax.broadcasted_iota(jnp.int32, (1, width), 1)
    return [(lane >= g * DH) & (lane < (g + 1) * DH) for g in range(width // DH)]


def _group_mean_matrix():
    i = lax.broadcasted_iota(jnp.int32, (GW, GW), 0) // DH
    j = lax.broadcasted_iota(jnp.int32, (GW, GW), 1) // DH
    return jnp.where(i == j, 1.0 / DH, 0.0).astype(F32)


def _tri_mask(lower):
    t = lax.broadcasted_iota(jnp.int32, (CHUNK, CHUNK), 0)
    u = lax.broadcasted_iota(jnp.int32, (CHUNK, CHUNK), 1)
    return (u <= t) if lower else (u >= t)


def _sgu_forward(u, z, lng, lnb, w_ref, bias_t, pmat, rows):
    ug = _gelu(u)
    zg = _gelu(z)
    mu = _dot_hi(zg, pmat)
    zc = zg - mu
    var = _dot_hi(zc * zc, pmat)
    rstd = lax.rsqrt(var + EPS)
    zhat = zc * rstd
    zn = (zhat * lng + lnb).astype(BF16)
    gm = _group_masks(GW)
    tri = _tri_mask(True)
    ws = [jnp.where(tri, w_ref[g], 0.0).astype(BF16) for g in range(NG)]
    pieces = []
    for c in range(rows // CHUNK):
        znc = zn[c * CHUNK:(c + 1) * CHUNK, :]
        mix = None
        for g in range(NG):
            part = jnp.where(gm[g], _dot(ws[g], znc), 0.0)
            mix = part if mix is None else mix + part
        pieces.append(mix + bias_t)
    mixed = jnp.concatenate(pieces, axis=0) if len(pieces) > 1 else pieces[0]
    return ug * mixed, ug, zhat, rstd, zn, mixed


def _mix_fwd(os_, ls_, u, z, x, lng, lnb, sgu_w, bias_t, ga, gg, wout):
    s = x.shape[0]
    nd = len(DILS)
    nscr = sum(1 for d in DILS if d > 1)

    def body(*refs):
        o_refs, l_refs = refs[:nd], refs[nd:2 * nd]
        u_ref, z_ref, x_ref, lng_ref, lnb_ref, w_ref, bt_ref, ga_ref, gg_ref, wo_ref = refs[2 * nd:2 * nd + 10]
        attn_ref = refs[2 * nd + 10]
        lse_refs = refs[2 * nd + 11:3 * nd + 11]
        mixed_ref, h1_ref = refs[3 * nd + 11:3 * nd + 13]
        scr = refs[3 * nd + 13:]
        scr_o, scr_l, scr_lse = scr[:nscr], scr[nscr:2 * nscr], scr[2 * nscr]
        ov, lv, j = [], [], 0
        for di, dil in enumerate(DILS):
            if dil == 1:
                ov.append(o_refs[di][0])
                lv.append(l_refs[di][0])
            else:
                ov.append(_merge_residues(o_refs[di], scr_o[j], dil))
                lv.append(_merge_residues(l_refs[di], scr_l[j], dil))
                j += 1
        mx = functools.reduce(jnp.maximum, lv)
        es = [jnp.exp(l - mx) for l in lv]
        den = functools.reduce(lambda a, b: a + b, es)
        attn = functools.reduce(lambda a, b: a + b, [e * o for e, o in zip(es, ov)]) / den
        attn_ref[...] = attn
        lse = mx + jnp.log(den)
        _fill_cols(scr_lse, lse)
        for di, dil in enumerate(DILS):
            if dil == 1:
                lse_refs[di][0] = lse
            else:
                _split_residues(scr_lse, lse_refs[di], dil)
        an, _, _ = _rms_fwd(attn, ga_ref[...])
        gmv, _, _, _, _, _ = _sgu_forward(u_ref[...], z_ref[...], lng_ref[...], lnb_ref[...], w_ref,
                                          bt_ref[...], _group_mean_matrix(), TMX)
        gn, _, _ = _rms_fwd(gmv, gg_ref[...])
        mixed = jnp.concatenate([an, gn], axis=-1).astype(BF16)
        mixed_ref[...] = mixed
        h1_ref[...] = x_ref[...] + _dot(mixed, wo_ref[...])

    sd = jax.ShapeDtypeStruct
    res = pl.pallas_call(
        body, name="mix_fwd", grid=(s // TMX,),
        in_specs=[_res_spec(d, TMX, A) for d in DILS] * 2 + [_row_spec(TMX, GW), _row_spec(TMX, GW),
                  _row_spec(TMX, D), _const_spec((1, GW)), _const_spec((1, GW)), _const_spec((NG, CHUNK, CHUNK)),
                  _const_spec((CHUNK, GW)), _const_spec((1, A)), _const_spec((1, GW)), _const_spec((D, D))],
        out_specs=[_row_spec(TMX, A)] + [_res_spec(d, TMX, A) for d in DILS] + [_row_spec(TMX, D), _row_spec(TMX, D)],
        out_shape=[sd((s, A), F32)] + [_res_shape(s, d, A, F32) for d in DILS] + [sd((s, D), BF16), sd((s, D), F32)],
        scratch_shapes=[_col_scratch(TMX, A)] * (2 * nscr + 1),
        compiler_params=_cparams("arbitrary"),
    )(*os_, *ls_, u, z, x, lng, lnb, sgu_w, bias_t, ga, gg, wout)
    return res[0], res[1:1 + nd], res[1 + nd], res[2 + nd]


def _mlp_fwd(h1, g2, wff1_t, wff2, gf, target):
    s = h1.shape[0]

    def body(h1_ref, g2_ref, w1_ref, w2_ref, gf_ref, t_ref, hn_ref, rf_ref, a_ref, dh2_ref, loss_ref, dgf_ref):
        i = pl.program_id(0)
        h1v = h1_ref[...]
        hn, _, _ = _rms_fwd(h1v, g2_ref[...])
        hn = hn.astype(BF16)
        hn_ref[...] = hn
        acc = h1v
        for j in range(DFF // FF_CH):
            cols = slice(j * FF_CH, (j + 1) * FF_CH)
            rf = jnp.maximum(_dot_nt(hn, w1_ref[cols, :]), 0.0)
            act = (rf * rf).astype(BF16)
            rf_ref[:, cols] = rf.astype(BF16)
            a_ref[:, cols] = act
            acc = acc + _dot(act, w2_ref[cols, :])
        y, h2n, r3 = _rms_fwd(acc, gf_ref[...])
        err = y - t_ref[...]
        part = 0.5 * jnp.sum(jnp.mean(err * err, axis=-1, keepdims=True), axis=0, keepdims=True)
        dy = err * (1.0 / D)
        dh2, dgf = _rms_bwd(dy, h2n, r3, gf_ref[...])
        dh2_ref[...] = dh2

        @pl.when(i == 0)
        def _():
            loss_ref[...] = jnp.zeros_like(loss_ref)
            dgf_ref[...] = jnp.zeros_like(dgf_ref)

        loss_ref[...] += jnp.broadcast_to(part, loss_ref.shape)
        dgf_ref[...] += dgf

    sd = jax.ShapeDtypeStruct
    return pl.pallas_call(
        body, name="mlp_fwd", grid=(s // TM,),
        in_specs=[_row_spec(TM, D), _const_spec((1, D)), _const_spec((DFF, D)), _const_spec((DFF, D)),
                  _const_spec((1, D)), _row_spec(TM, D)],
        out_specs=[_row_spec(TM, D), _row_spec(TM, DFF), _row_spec(TM, DFF), _row_spec(TM, D),
                   _const_spec((1, LANES)), _const_spec((1, D))],
        out_shape=[sd((s, D), BF16), sd((s, DFF), BF16), sd((s, DFF), BF16), sd((s, D), F32),
                   sd((1, LANES), F32), sd((1, D), F32)],
        compiler_params=_cparams("arbitrary"),
    )(h1, g2, wff1_t, wff2, gf, target)


def _mlp_bwd(dh2, rf, h1, g2, wff1_t, wff2):
    s = h1.shape[0]

    def body(dh2_ref, rf_ref, h1_ref, g2_ref, w1_ref, w2_ref, df_ref, dh1_ref, dg2_ref):
        i = pl.program_id(0)
        dh2v = dh2_ref[...]
        dh2b = dh2v.astype(BF16)
        dhn = jnp.zeros((TM, D), F32)
        for j in range(DFF // FF_CH):
            cols = slice(j * FF_CH, (j + 1) * FF_CH)
            da = _dot_nt(dh2b, w2_ref[cols, :])
            df = (da * (2.0 * rf_ref[:, cols].astype(F32))).astype(BF16)
            df_ref[:, cols] = df
            dhn = dhn + _dot(df, w1_ref[cols, :])
        _, h1n, r2 = _rms_fwd(h1_ref[...], g2_ref[...])
        dres, dg2 = _rms_bwd(dhn, h1n, r2, g2_ref[...])
        dh1_ref[...] = dh2v + dres

        @pl.when(i == 0)
        def _():
            dg2_ref[...] = jnp.zeros_like(dg2_ref)

        dg2_ref[...] += dg2

    sd = jax.ShapeDtypeStruct
    return pl.pallas_call(
        body, name="mlp_bwd", grid=(s // TM,),
        in_specs=[_row_spec(TM, D), _row_spec(TM, DFF), _row_spec(TM, D), _const_spec((1, D)),
                  _const_spec((DFF, D)), _const_spec((DFF, D))],
        out_specs=[_row_spec(TM, DFF), _row_spec(TM, D), _const_spec((1, D))],
        out_shape=[sd((s, DFF), BF16), sd((s, D), F32), sd((1, D), F32)],
        compiler_params=_cparams("arbitrary"),
    )(dh2, rf, h1, g2, wff1_t, wff2)


def _mix_bwd(dh1, attn, u, z, lng, lnb, sgu_w, sgu_wt, bias_t, ga, gg, wout):
    s = dh1.shape[0]
    nsteps = s // TMX
    nd = len(DILS)

    def body(*refs):
        dh1_ref, attn_ref, u_ref, z_ref, lng_ref, lnb_ref, w_ref, wt_ref, bt_ref, ga_ref, gg_ref, wo_ref = refs[:12]
        do_refs, dl_refs = refs[12:12 + nd], refs[12 + nd:12 + 2 * nd]
        (du_ref, dz_ref, dga_ref, dgg_ref, dlng_ref, dlnb_ref, dws_ref, db_ref,
         dbt_acc, scr_do, scr_dl) = refs[12 + 2 * nd:]
        i = pl.program_id(0)

        @pl.when(i == 0)
        def _():
            for r in (dga_ref, dgg_ref, dlng_ref, dlnb_ref, dws_ref, db_ref, dbt_acc):
                r[...] = jnp.zeros_like(r)

        dmixed = _dot_nt(dh1_ref[...].astype(BF16), wo_ref[...])
        attn = attn_ref[...]
        _, an, ra = _rms_fwd(attn, ga_ref[...])
        dattn, dga = _rms_bwd(dmixed[:, :A], an, ra, ga_ref[...])
        dga_ref[...] += dga
        _fill_cols(scr_do, dattn)
        prod = dattn * attn
        delta = jnp.zeros_like(prod)
        for hm in _group_masks(A):
            delta = delta + jnp.where(hm, jnp.sum(jnp.where(hm, prod, 0.0), axis=-1, keepdims=True), 0.0)
        _fill_cols(scr_dl, delta)
        for di, dil in enumerate(DILS):
            if dil == 1:
                do_refs[di][0] = dattn.astype(BF16)
                dl_refs[di][0] = delta
            else:
                _split_residues(scr_do, do_refs[di], dil)
                _split_residues(scr_dl, dl_refs[di], dil)
        pmat = _group_mean_matrix()
        lng = lng_ref[...]
        uv, zv = u_ref[...], z_ref[...]
        gmv, ug, zhat, rstd, zn, mixed = _sgu_forward(uv, zv, lng, lnb_ref[...], w_ref, bt_ref[...], pmat, TMX)
        _, gmn, rg = _rms_fwd(gmv, gg_ref[...])
        dgm, dgg = _rms_bwd(dmixed[:, A:], gmn, rg, gg_ref[...])
        dgg_ref[...] += dgg
        du_ref[...] = dgm * mixed * _gelu_grad(uv)
        dmx = dgm * ug
        dmxb = dmx.astype(BF16)
        gm = _group_masks(GW)
        tri_t = _tri_mask(False)
        wst = [jnp.where(tri_t, wt_ref[g], 0.0).astype(BF16) for g in range(NG)]
        zero = jnp.zeros((CHUNK, GW), BF16)
        dzn_pieces = []
        for c in range(TMX // CHUNK):
            rs = slice(c * CHUNK, (c + 1) * CHUNK)
            dmc = dmxb[rs, :]
            znc = zn[rs, :]
            dbt_acc[...] += dmx[rs, :]
            dzn = None
            for g in range(NG):
                dws_ref[g] += _dot_nt(jnp.where(gm[g], dmc, zero), znc)
                part = jnp.where(gm[g], _dot(wst[g], dmc), 0.0)
                dzn = part if dzn is None else dzn + part
            dzn_pieces.append(dzn)
        dzn = jnp.concatenate(dzn_pieces, axis=0)
        dlng_ref[...] += jnp.sum(dzn * zhat, axis=0, keepdims=True)
        dlnb_ref[...] += jnp.sum(dzn, axis=0, keepdims=True)
        dzh = dzn * lng
        dzg = rstd * (dzh - _dot_hi(dzh, pmat) - zhat * _dot_hi(dzh * zhat, pmat))
        dz_ref[...] = dzg * _gelu_grad(zv)

        @pl.when(i == nsteps - 1)
        def _():
            tri = _tri_mask(True)
            for g in range(NG):
                dws_ref[g] = jnp.where(tri, dws_ref[g], 0.0)
            acc = dbt_acc[...]
            lane = lax.broadcasted_iota(jnp.int32, (CHUNK, LANES), 1)
            out = jnp.zeros((CHUNK, LANES), F32)
            for g in range(NG):
                sg = jnp.sum(jnp.where(gm[g], acc, 0.0), axis=-1, keepdims=True)
                out = jnp.where(lane == g, sg, out)
            db_ref[...] = out

    sd = jax.ShapeDtypeStruct
    res = pl.pallas_call(
        body, name="mix_bwd", grid=(nsteps,),
        in_specs=[_row_spec(TMX, D), _row_spec(TMX, A), _row_spec(TMX, GW), _row_spec(TMX, GW),
                  _const_spec((1, GW)), _const_spec((1, GW)), _const_spec((NG, CHUNK, CHUNK)),
                  _const_spec((NG, CHUNK, CHUNK)), _const_spec((CHUNK, GW)), _const_spec((1, A)),
                  _const_spec((1, GW)), _const_spec((D, D))],
        out_specs=[_res_spec(d, TMX, A) for d in DILS] * 2 + [_row_spec(TMX, GW), _row_spec(TMX, GW),
                   _const_spec((1, A)), _const_spec((1, GW)), _const_spec((1, GW)), _const_spec((1, GW)),
                   _const_spec((NG, CHUNK, CHUNK)), _const_spec((CHUNK, LANES))],
        out_shape=[_res_shape(s, d, A, BF16) for d in DILS] + [_res_shape(s, d, A, F32) for d in DILS]
                  + [sd((s, GW), F32), sd((s, GW), F32),
                   sd((1, A), F32), sd((1, GW), F32), sd((1, GW), F32), sd((1, GW), F32),
                   sd((NG, CHUNK, CHUNK), F32), sd((CHUNK, LANES), F32)],
        scratch_shapes=[pltpu.VMEM((CHUNK, GW), F32), _col_scratch(TMX, A), _col_scratch(TMX, A)],
        compiler_params=_cparams("arbitrary"),
    )(dh1, attn, u, z, lng, lnb, sgu_w, sgu_wt, bias_t, ga, gg, wout)
    return (res[:nd], res[nd:2 * nd]) + tuple(res[2 * nd:])


def _inproj_bwd(dqs, dks, dvs, du, dz, dh1, x, g1, win_t):
    s = x.shape[0]
    nd = len(DILS)
    nscr = sum(1 for d in DILS if d > 1)

    def body(*refs):
        parts = [refs[t * nd:(t + 1) * nd] for t in range(3)]
        du_ref, dz_ref, dh1_ref, x_ref, g_ref, w_ref, dp_ref, dx_ref, dg_ref = refs[3 * nd:3 * nd + 9]
        scr = refs[3 * nd + 9:]
        i = pl.program_id(0)
        sums = []
        for t in range(3):
            total, j = None, 0
            for di, dil in enumerate(DILS):
                if dil == 1:
                    term = parts[t][di][0]
                else:
                    term = _merge_residues(parts[t][di], scr[t * nscr + j], dil)
                    j += 1
                total = term if total is None else total + term
            sums.append(total)
        dp = jnp.concatenate([sums[0] * SCALE, sums[1], sums[2], du_ref[...], dz_ref[...]], axis=-1).astype(BF16)
        dp_ref[...] = dp
        dhn = _dot(dp, w_ref[...])
        _, xn, r1 = _rms_fwd(x_ref[...], g_ref[...])
        dres, dg = _rms_bwd(dhn, xn, r1, g_ref[...])
        dx_ref[...] = dh1_ref[...] + dres

        @pl.when(i == 0)
        def _():
            dg_ref[...] = jnp.zeros_like(dg_ref)

        dg_ref[...] += dg

    sd = jax.ShapeDtypeStruct
    return pl.pallas_call(
        body, name="inproj_bwd", grid=(s // TMX,),
        in_specs=[_res_spec(d, TMX, A) for d in DILS] * 3 + [_row_spec(TMX, GW)] * 2
                 + [_row_spec(TMX, D), _row_spec(TMX, D), _const_spec((1, D)), _const_spec((INW, D))],
        out_specs=[_row_spec(TMX, INW), _row_spec(TMX, D), _const_spec((1, D))],
        out_shape=[sd((s, INW), BF16), sd((s, D), F32), sd((1, D), F32)],
        scratch_shapes=[_col_scratch(TMX, A)] * (3 * nscr),
        compiler_params=_cparams("arbitrary"),
    )(*dqs, *dks, *dvs, du, dz, dh1, x, g1, win_t)


def _wgrad(a, b, name, bm, bn, bk=TM):
    s, m = a.shape
    n = b.shape[1]
    bm, bn = min(bm, m), min(bn, n)

    def body(a_ref, b_ref, o_ref):
        @pl.when(pl.program_id(2) == 0)
        def _():
            o_ref[...] = jnp.zeros_like(o_ref)

        o_ref[...] += _dot_tn(a_ref[...].astype(BF16), b_ref[...].astype(BF16))

    return pl.pallas_call(
        body, name=name, grid=(m // bm, n // bn, s // bk),
        in_specs=[pl.BlockSpec((bk, bm), lambda i, j, k: (k, i)), pl.BlockSpec((bk, bn), lambda i, j, k: (k, j))],
        out_specs=pl.BlockSpec((bm, bn), lambda i, j, k: (i, j)),
        out_shape=jax.ShapeDtypeStruct((m, n), F32),
        compiler_params=_cparams("arbitrary", "arbitrary", "arbitrary"),
    )(a, b)


def _adamw_math(w, g, m, v):
    m = B1 * m + (1.0 - B1) * g
    v = B2 * v + (1.0 - B2) * (g * g)
    m_hat = m / (1.0 - B1 ** STEP)
    v_hat = v / (1.0 - B2 ** STEP)
    delta = -LR * (m_hat / (jnp.sqrt(v_hat) + AEPS) + WD * w)
    return delta, m, v


def _adamw(w, g, m, v, name):
    rows, cols = w.shape
    br = min(rows, 256)
    while rows % br:
        br -= 8

    def body(w_ref, g_ref, m_ref, v_ref, d_ref, mo_ref, vo_ref):
        d, mn, vn = _adamw_math(w_ref[...], g_ref[...], m_ref[...], v_ref[...])
        d_ref[...] = d
        mo_ref[...] = mn
        vo_ref[...] = vn

    spec = _row_spec(br, cols)
    sd = jax.ShapeDtypeStruct((rows, cols), F32)
    return pl.pallas_call(
        body, name=name, grid=(rows // br,), in_specs=[spec] * 4, out_specs=[spec] * 3,
        out_shape=[sd, sd, sd], compiler_params=_cparams("arbitrary"),
    )(w, g, m, v)


def _local_step(x, target, small, win_t, rest_weights):
    slopes = jnp.asarray(_alibi_slopes(NH))
    hn1, q, k, v, u, z = _inproj_fwd(x, small["norm1_g"], win_t)
    outs, lses = [], []
    for i, dil in enumerate(DILS):
        o, l = _attn_fwd(q[i], k[i], v[i], slopes, dil)
        outs.append(o)
        lses.append(l)
    wout, wff1_t, wff2 = rest_weights(lses[-1])
    attn, lse, mixed, h1 = _mix_fwd(outs, lses, u, z, x, small["ln_g"], small["ln_b"], small["sgu_w"],
                                    small["bias_t"], small["attn_out_g"], small["gmlp_out_g"], wout)
    hn2, rf, act, dh2, loss, dgf = _mlp_fwd(h1, small["norm2_g"], wff1_t, wff2, small["final_norm_g"], target)
    df, dh1, dg2 = _mlp_bwd(dh2, rf, h1, small["norm2_g"], wff1_t, wff2)
    (do, delta, du, dz, dga, dgg, dlng, dlnb, dws, db) = _mix_bwd(
        dh1, attn, u, z, small["ln_g"], small["ln_b"], small["sgu_w"], small["sgu_wt"], small["bias_t"],
        small["attn_out_g"], small["gmlp_out_g"], wout)
    dqs, dks, dvs = [], [], []
    for i, dil in enumerate(DILS):
        dqs.append(_attn_bwd_dq(q[i], k[i], v[i], do[i], lse[i], delta[i], slopes, dil))
        dk, dv = _attn_bwd_dkv(q[i], k[i], v[i], do[i], lse[i], delta[i], slopes, dil)
        dks.append(dk)
        dvs.append(dv)
    dproj, dx, dg1 = _inproj_bwd(dqs, dks, dvs, du, dz, dh1, x, small["norm1_g"], win_t)
    gwin_t = _wgrad(dproj, hn1, "wgrad_in", INW // 2, D)
    gwout = _wgrad(mixed, dh1, "wgrad_out", D, D)
    gwff1_t = _wgrad(df, hn2, "wgrad_ff1", 1024, D)
    gwff2 = _wgrad(act, dh2, "wgrad_ff2", 1024, D)
    small_grads = dict(norm1_g=dg1, ln_g=dlng, ln_b=dlnb, sgu_w=dws, sgu_b=db[:, :NG].T,
                       attn_out_g=dga, gmlp_out_g=dgg, norm2_g=dg2, final_norm_g=dgf)
    return loss[0, 0], dx, small_grads, (gwin_t, gwout, gwff1_t, gwff2)


ANY = pl.BlockSpec(memory_space=pl.ANY)
HALF_ROWS = SHARD_ROWS // 2
ADD_ROWS = 376
D2D_SPLIT = 4


def _position():
    return lax.axis_index("x"), lax.axis_index("y"), lax.axis_index("c")


def _other_chips(x, y):
    return [(1 - x, y), (x, 1 - y), (1 - x, 1 - y)]


def _remote(src, dst, send_sem, recv_sem, device):
    return pltpu.make_async_remote_copy(src_ref=src, dst_ref=dst, send_sem=send_sem, recv_sem=recv_sem,
                                        device_id=device, device_id_type=MESH)


HBM = pl.BlockSpec(memory_space=pltpu.HBM)
SEM = pl.BlockSpec(memory_space=pltpu.SEMAPHORE)
DATAFLOW = pltpu.SideEffectType.DATAFLOW_SIDE_EFFECTING


def _in_hbm(a):
    return pltpu.with_memory_space_constraint(a, pltpu.HBM)


def _gather_start(shard, name):
    def body(w_ref, land_ref, send_sems, recv_sems, w_thru, land_thru, token):
        x, y, c = _position()
        for k, (px, py) in enumerate(_other_chips(x, y)):
            _remote(w_ref, land_ref.at[2 * x + y], send_sems.at[k], recv_sems.at[k], (px, py, c)).start()
        token[...] = jnp.zeros_like(token)

    land = jnp.broadcast_to(shard[None], (NCHIP,) + shard.shape)
    return pl.pallas_call(
        body, name=name,
        out_shape=(pltpu.SemaphoreType.DMA((3,)), pltpu.SemaphoreType.DMA((3,)),
                   pltpu.HBM(shard.shape, shard.dtype), pltpu.HBM(land.shape, land.dtype),
                   jax.ShapeDtypeStruct((8, LANES), F32)),
        in_specs=(HBM, HBM), out_specs=(SEM, SEM, HBM, HBM, pl.BlockSpec(memory_space=pltpu.VMEM)),
        input_output_aliases={0: 2, 1: 3},
        compiler_params=pltpu.CompilerParams(has_side_effects=DATAFLOW),
    )(_in_hbm(shard), _in_hbm(land))


def _gather_wait(send_sems, recv_sems, w_thru, land_thru, after, name):
    def body(w_ref, land_ref, send_sems, recv_sems, after_ref, w_dead, got_ref):
        x, y, c = _position()
        for k, (px, py) in enumerate(_other_chips(x, y)):
            cp = _remote(w_ref, land_ref.at[2 * px + py], send_sems.at[k], recv_sems.at[k], (px, py, c))
            cp.wait_send()
            cp.wait_recv()

    return pl.pallas_call(
        body, name=name,
        out_shape=(pltpu.HBM(w_thru.shape, w_thru.dtype), pltpu.HBM(land_thru.shape, land_thru.dtype)),
        in_specs=(HBM, HBM, SEM, SEM, ANY), out_specs=(HBM, HBM),
        input_output_aliases={0: 0, 1: 1},
        compiler_params=pltpu.CompilerParams(has_side_effects=DATAFLOW),
    )(w_thru, land_thru, send_sems, recv_sems, after)[1]


def _exchange_halves(gall):
    def body(g_ref, own_ref, got_ref, out_buf, in_buf, load_sems, send_sems, recv_sems, store_sems, local_sem):
        x, y, c = _position()
        keep = g_ref.at[:, pl.ds(c * HALF_ROWS, HALF_ROWS), :]
        local = pltpu.make_async_copy(keep, own_ref, local_sem)
        local.start()
        loads = [pltpu.make_async_copy(g_ref.at[j, pl.ds((1 - c) * HALF_ROWS, HALF_ROWS), :], out_buf.at[j],
                                       load_sems.at[j]) for j in range(NCHIP)]
        for ld in loads:
            ld.start()
        sends = []
        for j in range(NCHIP):
            loads[j].wait()
            cp = _remote(out_buf.at[j], in_buf.at[j], send_sems.at[j], recv_sems.at[j], (x, y, 1 - c))
            cp.start()
            sends.append(cp)
        stores = []
        for j in range(NCHIP):
            sends[j].wait_recv()
            st = pltpu.make_async_copy(in_buf.at[j], got_ref.at[j], store_sems.at[j])
            st.start()
            stores.append(st)
        for j in range(NCHIP):
            sends[j].wait_send()
            stores[j].wait()
        local.wait()

    sd = jax.ShapeDtypeStruct((NCHIP, HALF_ROWS, D), F32)
    dma = pltpu.SemaphoreType.DMA((NCHIP,))
    return pl.pallas_call(
        body, name="exchange_halves", in_specs=[ANY], out_specs=[ANY, ANY], out_shape=[sd, sd],
        scratch_shapes=[pltpu.VMEM((NCHIP, HALF_ROWS, D), F32), pltpu.VMEM((NCHIP, HALF_ROWS, D), F32),
                        dma, dma, dma, dma, pltpu.SemaphoreType.DMA],
        compiler_params=pltpu.CompilerParams(has_side_effects=True, vmem_limit_bytes=VMEM_LIMIT),
    )(gall)


def _scatter_to_owners(part):
    def body(p_ref, own_ref, got_ref, send_sems, recv_sems, local_sem):
        x, y, c = _position()
        local = pltpu.make_async_copy(p_ref.at[2 * x + y], own_ref, local_sem)
        local.start()
        sends = []
        for k, (px, py) in enumerate(_other_chips(x, y)):
            cp = _remote(p_ref.at[2 * px + py], got_ref.at[k], send_sems.at[k], recv_sems.at[k], (px, py, c))
            cp.start()
            sends.append(cp)
        for cp in sends:
            cp.wait()
        local.wait()

    return pl.pallas_call(
        body, name="scatter_to_owners", in_specs=[ANY], out_specs=[ANY, ANY],
        out_shape=[jax.ShapeDtypeStruct((HALF_ROWS, D), F32), jax.ShapeDtypeStruct((3, HALF_ROWS, D), F32)],
        scratch_shapes=[pltpu.SemaphoreType.DMA((3,)), pltpu.SemaphoreType.DMA((3,)), pltpu.SemaphoreType.DMA],
        compiler_params=pltpu.CompilerParams(has_side_effects=True),
    )(part)


def _share_with_sibling(half):
    n = D2D_SPLIT
    part = HALF_ROWS // n

    def body(h_ref, out_ref, buf, load_sems, send_sems, recv_sems, local_sem):
        x, y, c = _position()
        local = pltpu.make_async_copy(h_ref, out_ref.at[c], local_sem)
        local.start()
        loads = []
        for t in range(n):
            rows = pl.ds(t * part, part)
            ld = pltpu.make_async_copy(h_ref.at[rows, :], buf.at[rows, :], load_sems.at[t])
            ld.start()
            loads.append(ld)
        copies = []
        for t in range(n):
            rows = pl.ds(t * part, part)
            loads[t].wait()
            cp = _remote(buf.at[rows, :], out_ref.at[c, rows, :], send_sems.at[t], recv_sems.at[t], (x, y, 1 - c))
            cp.start()
            copies.append(cp)
        for t, cp in enumerate(copies):
            cp.wait_send()
            rows = pl.ds(t * part, part)
            _remote(buf.at[rows, :], out_ref.at[1 - c, rows, :], send_sems.at[t], recv_sems.at[t],
                    (x, y, 1 - c)).wait_recv()
        local.wait()

    dma = pltpu.SemaphoreType.DMA((n,))
    return pl.pallas_call(
        body, name="share_with_sibling", in_specs=[ANY], out_specs=ANY,
        out_shape=jax.ShapeDtypeStruct((2, HALF_ROWS, D), F32),
        scratch_shapes=[pltpu.VMEM((HALF_ROWS, D), F32), dma, dma, dma, pltpu.SemaphoreType.DMA],
        compiler_params=pltpu.CompilerParams(has_side_effects=True, vmem_limit_bytes=VMEM_LIMIT),
    )(half)


def _add_slabs(terms, out_rows, name):
    n = len(terms)

    def body(*refs):
        acc = refs[0][...]
        for r in refs[1:n]:
            acc = acc + r[...]
        refs[n][...] = acc

    specs = [pl.BlockSpec((ADD_ROWS, D), functools.partial(lambda i, off: (i + off, 0), off=first // ADD_ROWS))
             for _, first in terms]
    return pl.pallas_call(
        body, name=name, grid=(out_rows // ADD_ROWS,), in_specs=specs, out_specs=_row_spec(ADD_ROWS, D),
        out_shape=jax.ShapeDtypeStruct((out_rows, D), F32), compiler_params=_cparams("arbitrary"),
    )(*[a for a, _ in terms])


def _reduce_grads(gall):
    own, got = _exchange_halves(gall)
    rows = NCHIP * HALF_ROWS
    chip = _add_slabs([(own.reshape(rows, D), 0), (got.reshape(rows, D), 0)], rows, "add_sibling")
    mine, others = _scatter_to_owners(chip.reshape(NCHIP, HALF_ROWS, D))
    flat = others.reshape(3 * HALF_ROWS, D)
    half = _add_slabs([(mine, 0), (flat, 0), (flat, HALF_ROWS), (flat, 2 * HALF_ROWS)], HALF_ROWS, "add_chips")
    return _share_with_sibling(half).reshape(SHARD_ROWS, D)


SMALL_SIZES = (("norm1_g", D), ("sgu_ln_g", GW), ("sgu_ln_b", GW), ("sgu_w", NG * CHUNK * CHUNK),
               ("sgu_b", NG * CHUNK), ("attn_out_g", A), ("gmlp_out_g", GW), ("norm2_g", D),
               ("final_norm_g", D))
SMALL_ROWS = sum(n for _, n in SMALL_SIZES) // LANES
NDEV = 8


def _pack_small(tree):
    return jnp.concatenate([tree[n].reshape(-1) for n, _ in SMALL_SIZES]).reshape(SMALL_ROWS, LANES)


def _unpack_small(pack, shapes):
    flat = pack.reshape(-1)
    out, off = {}, 0
    for n, size in SMALL_SIZES:
        out[n] = flat[off:off + size].reshape(shapes[n])
        off += size
    return out


def _small_allreduce_adamw(gpack, wpack, mpack, vpack):
    def body(g_ref, w_ref, m_ref, v_ref, go_ref, d_ref, mo_ref, vo_ref, slots, send_sems, recv_sems):
        x, y, c = _position()
        me = 4 * x + 2 * y + c
        slots[me] = g_ref[...]
        sends = []
        for k in range(1, NDEV):
            kx, ky, kc = (k >> 2) & 1, (k >> 1) & 1, k & 1
            peer = (1 - x if kx else x, 1 - y if ky else y, 1 - c if kc else c)
            cp = _remote(g_ref, slots.at[me], send_sems.at[k - 1], recv_sems.at[k - 1], peer)
            cp.start()
            sends.append((cp, peer))
        for k in range(1, NDEV):
            _, (px, py, pc) = sends[k - 1]
            _remote(g_ref, slots.at[4 * px + 2 * py + pc], send_sems.at[k - 1], recv_sems.at[k - 1],
                    (px, py, pc)).wait_recv()
        for cp, _ in sends:
            cp.wait_send()
        total = slots[0]
        for k in range(1, NDEV):
            total = total + slots[k]
        go_ref[...] = total
        d, mn, vn = _adamw_math(w_ref[...], total, m_ref[...], v_ref[...])
        d_ref[...] = d
        mo_ref[...] = mn
        vo_ref[...] = vn

    sd = jax.ShapeDtypeStruct((SMALL_ROWS, LANES), F32)
    vm = pl.BlockSpec(memory_space=pltpu.VMEM)
    return pl.pallas_call(
        body, name="small_allreduce_adamw", in_specs=[vm] * 4, out_specs=[vm] * 4, out_shape=[sd] * 4,
        scratch_shapes=[pltpu.VMEM((NDEV, SMALL_ROWS, LANES), F32), pltpu.SemaphoreType.DMA((NDEV - 1,)),
                        pltpu.SemaphoreType.DMA((NDEV - 1,))],
        compiler_params=pltpu.CompilerParams(has_side_effects=True),
    )(gpack, wpack, mpack, vpack)


def kernel(x, norm1_g, w_in, sgu_ln_g, sgu_ln_b, sgu_w, sgu_b, attn_out_g, gmlp_out_g, w_out, norm2_g, w_ff1, w_ff2, final_norm_g, loss_target, m_norm1_g, m_w_in, m_sgu_ln_g, m_sgu_ln_b, m_sgu_w, m_sgu_b, m_attn_out_g, m_gmlp_out_g, m_w_out, m_norm2_g, m_w_ff1, m_w_ff2, m_final_norm_g, v_norm1_g, v_w_in, v_sgu_ln_g, v_sgu_ln_b, v_sgu_w, v_sgu_b, v_attn_out_g, v_gmlp_out_g, v_w_out, v_norm2_g, v_w_ff1, v_w_ff2, v_final_norm_g):
    names = [n for n, _ in SMALL_SIZES]
    w_small = dict(norm1_g=norm1_g, sgu_ln_g=sgu_ln_g, sgu_ln_b=sgu_ln_b, sgu_w=sgu_w, sgu_b=sgu_b,
                   attn_out_g=attn_out_g, gmlp_out_g=gmlp_out_g, norm2_g=norm2_g, final_norm_g=final_norm_g)
    m_small = dict(norm1_g=m_norm1_g, sgu_ln_g=m_sgu_ln_g, sgu_ln_b=m_sgu_ln_b, sgu_w=m_sgu_w, sgu_b=m_sgu_b,
                   attn_out_g=m_attn_out_g, gmlp_out_g=m_gmlp_out_g, norm2_g=m_norm2_g,
                   final_norm_g=m_final_norm_g)
    v_small = dict(norm1_g=v_norm1_g, sgu_ln_g=v_sgu_ln_g, sgu_ln_b=v_sgu_ln_b, sgu_w=v_sgu_w, sgu_b=v_sgu_b,
                   attn_out_g=v_attn_out_g, gmlp_out_g=v_gmlp_out_g, norm2_g=v_norm2_g,
                   final_norm_g=v_final_norm_g)
    shapes = {n: w_small[n].shape for n in names}

    r_in, r_out, r_ff = INW // NCHIP, D // NCHIP, DFF // NCHIP
    o1, o2, o3 = r_in, r_in + r_out, r_in + r_out + r_ff
    start_in = _gather_start(w_in[0].T.astype(BF16), "gather_in_start")
    start_rest = _gather_start(jnp.concatenate([w_out[0], w_ff1[0].T, w_ff2[0]], axis=0).astype(BF16),
                               "gather_rest_start")
    win_t = _gather_wait(*start_in[:4], after=start_rest[4], name="gather_in_wait").reshape(INW, D)

    def rest_weights(after):
        rest = _gather_wait(*start_rest[:4], after=after, name="gather_rest_wait")
        return (rest[:, :r_out].reshape(D, D), rest[:, r_out:r_out + r_ff].reshape(DFF, D),
                rest[:, r_out + r_ff:].reshape(DFF, D))

    small = dict(
        norm1_g=norm1_g, ln_g=sgu_ln_g.reshape(1, GW), ln_b=sgu_ln_b.reshape(1, GW), sgu_w=sgu_w[0],
        sgu_wt=jnp.swapaxes(sgu_w[0], 1, 2), bias_t=jnp.repeat(sgu_b[0].T, DH, axis=1),
        attn_out_g=attn_out_g, gmlp_out_g=gmlp_out_g, norm2_g=norm2_g, final_norm_g=final_norm_g.reshape(1, D))
    loss_part, dx, sg, (gwin_t, gwout, gwff1_t, gwff2) = _local_step(
        x[0], loss_target[0], small, win_t, rest_weights)
    loss = lax.psum(loss_part, ("x", "y", "c"))

    gall = jnp.concatenate([gwin_t.reshape(NCHIP, r_in, D), gwout.reshape(NCHIP, r_out, D),
                            gwff1_t.reshape(NCHIP, r_ff, D), gwff2.reshape(NCHIP, r_ff, D)], axis=1)
    gsum = _reduce_grads(gall)
    g_big = dict(w_in=gsum[:o1].T, w_out=gsum[o1:o2], w_ff1=gsum[o2:o3].T, w_ff2=gsum[o3:])
    w_big = dict(w_in=(w_in, m_w_in, v_w_in), w_out=(w_out, m_w_out, v_w_out),
                 w_ff1=(w_ff1, m_w_ff1, v_w_ff1), w_ff2=(w_ff2, m_w_ff2, v_w_ff2))
    grads, deltas, new_m, new_v = {}, {}, {}, {}
    for n, (w, m, v) in w_big.items():
        d, mn, vn = _adamw(w[0], g_big[n], m[0], v[0], "adamw_" + n)
        grads[n], deltas[n], new_m[n], new_v[n] = g_big[n][None], d[None], mn[None], vn[None]

    g_small = dict(norm1_g=sg["norm1_g"], sgu_ln_g=sg["ln_g"], sgu_ln_b=sg["ln_b"], sgu_w=sg["sgu_w"],
                   sgu_b=sg["sgu_b"], attn_out_g=sg["attn_out_g"], gmlp_out_g=sg["gmlp_out_g"],
                   norm2_g=sg["norm2_g"], final_norm_g=sg["final_norm_g"])
    packs = _small_allreduce_adamw(_pack_small(g_small), _pack_small(w_small), _pack_small(m_small),
                                   _pack_small(v_small))
    for tree, pack in zip((grads, deltas, new_m, new_v), packs):
        tree.update(_unpack_small(pack, shapes))

    order = ["norm1_g", "w_in", "sgu_ln_g", "sgu_ln_b", "sgu_w", "sgu_b", "attn_out_g", "gmlp_out_g", "w_out",
             "norm2_g", "w_ff1", "w_ff2", "final_norm_g"]
    return (loss, dx[None], *[grads[n] for n in order], *[deltas[n] for n in order],
            *[new_m[n] for n in order], *[new_v[n] for n in order])
```

```python
import functools
import math

import numpy as np
import jax
import jax.numpy as jnp
from jax import lax
from jax.experimental import pallas as pl
from jax.experimental.pallas import tpu as pltpu

F32 = jnp.float32
BF16 = jnp.bfloat16

D = 1024
NH = 12
DH = 64
A = NH * DH
NG = 4
GW = NG * DH
INW = 3 * A + 2 * GW
DFF = 4 * D
CHUNK = 128
PATTERNS = ((128, 1), (512, 4), (2048, 16))
EPS = 1e-6
SCALE = DH ** -0.5
NEG = -1e30

LR, B1, B2, AEPS, WD, STEP = 0.001, 0.9, 0.999, 1e-08, 0.01, 10

TM = 512
TMX = 256
ATT_ROWS = 1024
FF_CH = 1024
LANES = 128
NCHIP = 4
VMEM_LIMIT = 56 * 1024 * 1024
MESH = pl.DeviceIdType.MESH


def _cparams(*sem, **kw):
    return pltpu.CompilerParams(dimension_semantics=sem if sem else None,
                                vmem_limit_bytes=VMEM_LIMIT, **kw)


def _dot(a, b):
    return jnp.dot(a, b, preferred_element_type=F32)


def _dot_nt(a, b):
    return lax.dot_general(a, b, (((1,), (1,)), ((), ())), preferred_element_type=F32)


def _dot_tn(a, b):
    return lax.dot_general(a, b, (((0,), (0,)), ((), ())), preferred_element_type=F32)


def _dot_hi(a, b):
    return jnp.dot(a, b, preferred_element_type=F32, precision=lax.Precision.HIGHEST)


def _alibi_slopes(n):
    def pow2(m):
        start = 2.0 ** (-8.0 / m)
        return [start ** (i + 1) for i in range(m)]
    if math.log2(n).is_integer():
        s = pow2(n)
    else:
        c = 2 ** int(math.floor(math.log2(n)))
        s = pow2(c) + pow2(2 * c)[0::2][: n - c]
    return np.asarray(s, dtype=np.float32)


def _rms_fwd(v, g):
    r = lax.rsqrt(jnp.mean(v * v, axis=-1, keepdims=True) + EPS)
    vn = v * r
    return vn * g, vn, r


def _rms_bwd(dy, vn, r, g):
    w = dy * g
    dv = r * (w - vn * jnp.mean(w * vn, axis=-1, keepdims=True))
    return dv, jnp.sum(dy * vn, axis=0, keepdims=True)


_K0 = math.sqrt(2.0 / math.pi)
_K1 = 0.044715


def _gelu(v):
    return 0.5 * v * (1.0 + jnp.tanh(_K0 * (v + _K1 * (v * v * v))))


def _gelu_grad(v):
    t = jnp.tanh(_K0 * (v + _K1 * (v * v * v)))
    return 0.5 * (1.0 + t) + 0.5 * v * (1.0 - t * t) * (_K0 * (1.0 + 3.0 * _K1 * v * v))


def _row_spec(rows, cols):
    return pl.BlockSpec((rows, cols), lambda i: (i, 0))


def _const_spec(shape):
    nd = len(shape)
    return pl.BlockSpec(shape, lambda i: (0,) * nd, pipeline_mode=pl.Buffered(1))


DILS = tuple(d for _, d in PATTERNS)


def _fill_cols(scr, value):
    for cb in range(value.shape[1] // LANES):
        scr[cb] = value[:, cb * LANES:(cb + 1) * LANES]


def _split_residues(scr, out_ref, dil):
    nb, rows, _ = scr.shape
    for r in range(dil):
        for cb in range(nb):
            piece = scr.at[cb][pl.ds(r, rows // dil, stride=dil), :]
            out_ref[r, :, cb * LANES:(cb + 1) * LANES] = piece.astype(out_ref.dtype)


def _merge_residues(in_ref, scr, dil):
    nb, rows, _ = scr.shape
    for r in range(dil):
        for cb in range(nb):
            scr.at[cb][pl.ds(r, rows // dil, stride=dil), :] = in_ref[r, :, cb * LANES:(cb + 1) * LANES]
    return jnp.concatenate([scr[cb] for cb in range(nb)], axis=-1)


def _col_scratch(rows, width):
    return pltpu.VMEM((width // LANES, rows, LANES), F32)


def _res_spec(dil, rows, width):
    return pl.BlockSpec((dil, rows // dil, width), lambda i: (0, i, 0))


def _res_shape(s, dil, width, dtype):
    return jax.ShapeDtypeStruct((dil, s // dil, width), dtype)


def _inproj_fwd(x, g1, win_t):
    s = x.shape[0]
    nd = len(DILS)

    def body(x_ref, g_ref, w_ref, hn_ref, *rest):
        qkv_refs = rest[:3 * nd]
        u_ref, z_ref, scr = rest[3 * nd:]
        hn, _, _ = _rms_fwd(x_ref[...], g_ref[...])
        hn = hn.astype(BF16)
        hn_ref[...] = hn
        for t in range(3):
            seg = _dot_nt(hn, w_ref[t * A:(t + 1) * A, :])
            seg = seg * SCALE if t == 0 else seg
            _fill_cols(scr, seg)
            for di, dil in enumerate(DILS):
                if dil == 1:
                    qkv_refs[t * nd + di][0] = seg.astype(BF16)
                else:
                    _split_residues(scr, qkv_refs[t * nd + di], dil)
        u_ref[...] = _dot_nt(hn, w_ref[3 * A:3 * A + GW, :])
        z_ref[...] = _dot_nt(hn, w_ref[3 * A + GW:INW, :])

    res = pl.pallas_call(
        body, name="inproj_fwd", grid=(s // TM,),
        in_specs=[_row_spec(TM, D), _const_spec((1, D)), _const_spec((INW, D))],
        out_specs=[_row_spec(TM, D)] + [_res_spec(d, TM, A) for _ in range(3) for d in DILS]
                  + [_row_spec(TM, GW), _row_spec(TM, GW)],
        out_shape=[jax.ShapeDtypeStruct((s, D), BF16)] + [_res_shape(s, d, A, BF16) for _ in range(3) for d in DILS]
                  + [jax.ShapeDtypeStruct((s, GW), F32)] * 2,
        scratch_shapes=[_col_scratch(TM, A)],
        compiler_params=_cparams("arbitrary"),
    )(x, g1, win_t)
    hn1 = res[0]
    q, k, v = (res[1 + t * nd:1 + (t + 1) * nd] for t in range(3))
    return hn1, q, k, v, res[-2], res[-1]


def _att_geometry(s, dil):
    length = s // dil
    rows = min(length, ATT_ROWS)
    return length, rows, length // rows, rows // CHUNK


def _stack_heads(t):
    lane = lax.broadcasted_iota(jnp.int32, t.shape, 1)
    zero = jnp.zeros_like(t)
    return jnp.concatenate([jnp.where(lane < DH, t, zero), jnp.where(lane >= DH, t, zero)], axis=0)


def _stack_cols(t):
    return jnp.concatenate([t[:, 0:1], t[:, DH:DH + 1]], axis=0)


def _unstack_heads(t2):
    n = t2.shape[0] // 2
    lane = lax.broadcasted_iota(jnp.int32, (n, LANES), 1)
    return jnp.where(lane < DH, t2[:n], t2[n:])


def _query_window_bias(s0, s1, dil, first):
    row = lax.broadcasted_iota(jnp.int32, (2 * CHUNK, 2 * CHUNK), 0)
    col = lax.broadcasted_iota(jnp.int32, (2 * CHUNK, 2 * CHUNK), 1)
    steps = (row & (CHUNK - 1)) + CHUNK - col
    valid = (steps >= 0) & (steps <= CHUNK)
    if first:
        valid = valid & (col >= CHUNK)
    slope = jnp.where(row < CHUNK, s0, s1)
    return jnp.where(valid, -slope * (steps * dil).astype(F32), NEG)


def _key_block_bias(s0, s1, dil, last):
    key = lax.broadcasted_iota(jnp.int32, (CHUNK, 4 * CHUNK), 0)
    col = lax.broadcasted_iota(jnp.int32, (CHUNK, 4 * CHUNK), 1)
    wq = col & (2 * CHUNK - 1)
    steps = wq - key
    valid = (steps >= 0) & (steps <= CHUNK)
    if last:
        valid = valid & (wq < CHUNK)
    slope = jnp.where(col < 2 * CHUNK, s0, s1)
    return jnp.where(valid, -slope * (steps * dil).astype(F32), NEG)


def _head_rows(t):
    row = lax.broadcasted_iota(jnp.int32, (8, LANES), 0)
    lane = lax.broadcasted_iota(jnp.int32, (8, LANES), 1)
    pick = jnp.where(((row == 0) & (lane == 0)) | ((row == 1) & (lane == DH)), 1.0, 0.0).astype(BF16)
    hi = t.astype(BF16)
    rest = t - hi.astype(F32)
    mid = rest.astype(BF16)
    low = (rest - mid.astype(F32)).astype(BF16)
    return _dot_nt(pick, hi) + _dot_nt(pick, mid) + _dot_nt(pick, low)


def _att_specs(dil, rows, nsub, nblk):
    main = pl.BlockSpec((None, rows, LANES), lambda r, hp, c: (r, c, hp))
    prev = pl.BlockSpec((None, CHUNK, LANES), lambda r, hp, c: (r, jnp.maximum(c * nsub - 1, 0), hp))
    nxt = pl.BlockSpec((None, CHUNK, LANES), lambda r, hp, c: (r, jnp.minimum((c + 1) * nsub, nblk - 1), hp))
    return main, prev, nxt


def _row_start(i):
    return i * CHUNK if isinstance(i, int) else pl.multiple_of(i * CHUNK, CHUNK)


def _attn_fwd(q, k, v, slopes, dil):
    length = q.shape[1]
    _, rows, nch, nsub = _att_geometry(length * dil, dil)
    main, prev, _ = _att_specs(dil, rows, nsub, length // CHUNK)

    def body(sl_ref, q_ref, k_ref, v_ref, kh_ref, vh_ref, o_ref, lse_ref, kbuf, vbuf, bias_buf):
        hp = pl.program_id(1)
        ch = pl.program_id(2)
        kbuf[0:CHUNK, :] = kh_ref[...]
        kbuf[CHUNK:, :] = k_ref[...]
        vbuf[0:CHUNK, :] = vh_ref[...]
        vbuf[CHUNK:, :] = v_ref[...]
        s0, s1 = sl_ref[2 * hp], sl_ref[2 * hp + 1]

        def block(i, bias):
            row = _row_start(i)
            rs = pl.ds(row, CHUNK)
            q2 = _stack_heads(q_ref[rs, :])
            kw = kbuf[pl.ds(row, 2 * CHUNK), :]
            vw = vbuf[pl.ds(row, 2 * CHUNK), :]
            sc = _dot_nt(q2, kw) + bias
            m = jnp.max(sc, axis=-1, keepdims=True)
            p = jnp.exp(sc - m)
            l = jnp.sum(p, axis=-1, keepdims=True)
            o2 = _dot(p.astype(BF16), vw) * (1.0 / l)
            o_ref[rs, :] = _unstack_heads(o2)
            lse_ref[rs, :] = _unstack_heads(jnp.broadcast_to(m + jnp.log(l), (2 * CHUNK, LANES)))

        bias_buf[...] = _query_window_bias(s0, s1, dil, False)

        @pl.when(ch == 0)
        def _():
            block(0, _query_window_bias(s0, s1, dil, True))

        @pl.when(ch != 0)
        def _():
            block(0, bias_buf[...])

        for i in range(1, nsub):
            block(i, bias_buf[...])

    sd = jax.ShapeDtypeStruct((dil, length, A), F32)
    return pl.pallas_call(
        body, name=f"attn_fwd_d{dil}", grid=(dil, NH // 2, nch),
        in_specs=[pl.BlockSpec(memory_space=pltpu.SMEM), main, main, main, prev, prev],
        out_specs=[main, main], out_shape=[sd, sd],
        scratch_shapes=[pltpu.VMEM((rows + CHUNK, LANES), BF16), pltpu.VMEM((rows + CHUNK, LANES), BF16),
                        pltpu.VMEM((2 * CHUNK, 2 * CHUNK), F32)],
        compiler_params=_cparams("arbitrary", "arbitrary", "arbitrary"),
    )(slopes, q, k, v, k, v)


def _attn_bwd_dq(q, k, v, do, lse, delta, slopes, dil):
    length = q.shape[1]
    _, rows, nch, nsub = _att_geometry(length * dil, dil)
    main, prev, _ = _att_specs(dil, rows, nsub, length // CHUNK)

    def body(sl_ref, q_ref, k_ref, v_ref, do_ref, lse_ref, dl_ref, kh_ref, vh_ref, dq_ref, kbuf, vbuf, bias_buf):
        hp = pl.program_id(1)
        ch = pl.program_id(2)
        kbuf[0:CHUNK, :] = kh_ref[...]
        kbuf[CHUNK:, :] = k_ref[...]
        vbuf[0:CHUNK, :] = vh_ref[...]
        vbuf[CHUNK:, :] = v_ref[...]
        s0, s1 = sl_ref[2 * hp], sl_ref[2 * hp + 1]

        def block(i, bias):
            row = _row_start(i)
            rs = pl.ds(row, CHUNK)
            q2 = _stack_heads(q_ref[rs, :])
            do2 = _stack_heads(do_ref[rs, :])
            lse2 = _stack_cols(lse_ref[rs, :])
            dl2 = _stack_cols(dl_ref[rs, :])
            kw = kbuf[pl.ds(row, 2 * CHUNK), :]
            vw = vbuf[pl.ds(row, 2 * CHUNK), :]
            p = jnp.exp(_dot_nt(q2, kw) + bias - lse2)
            ds = p * (_dot_nt(do2, vw) - dl2)
            dq_ref[rs, :] = _unstack_heads(_dot(ds.astype(BF16), kw))

        bias_buf[...] = _query_window_bias(s0, s1, dil, False)

        @pl.when(ch == 0)
        def _():
            block(0, _query_window_bias(s0, s1, dil, True))

        @pl.when(ch != 0)
        def _():
            block(0, bias_buf[...])

        for i in range(1, nsub):
            block(i, bias_buf[...])

    return pl.pallas_call(
        body, name=f"attn_dq_d{dil}", grid=(dil, NH // 2, nch),
        in_specs=[pl.BlockSpec(memory_space=pltpu.SMEM), main, main, main, main, main, main, prev, prev],
        out_specs=main, out_shape=jax.ShapeDtypeStruct((dil, length, A), F32),
        scratch_shapes=[pltpu.VMEM((rows + CHUNK, LANES), BF16), pltpu.VMEM((rows + CHUNK, LANES), BF16),
                        pltpu.VMEM((2 * CHUNK, 2 * CHUNK), F32)],
        compiler_params=_cparams("arbitrary", "arbitrary", "arbitrary"),
    )(slopes, q, k, v, do, lse, delta, k, v)


def _attn_bwd_dkv(q, k, v, do, lse, delta, slopes, dil):
    length = q.shape[1]
    _, rows, nch, nsub = _att_geometry(length * dil, dil)
    main, _, nxt = _att_specs(dil, rows, nsub, length // CHUNK)

    def body(sl_ref, k_ref, v_ref, q_ref, do_ref, lse_ref, dl_ref, qh_ref, doh_ref, lseh_ref, dlh_ref,
             dk_ref, dv_ref, qbuf, dobuf, lse_rows, dl_rows, bias_buf):
        hp = pl.program_id(1)
        ch = pl.program_id(2)
        for buf, main_ref, halo_ref in ((qbuf, q_ref, qh_ref), (dobuf, do_ref, doh_ref)):
            buf[0:rows, :] = main_ref[...]
            buf[rows:, :] = halo_ref[...]
        for buf, main_ref, halo_ref in ((lse_rows, lse_ref, lseh_ref), (dl_rows, dl_ref, dlh_ref)):
            buf[:, 0:rows] = _head_rows(main_ref[...])
            buf[:, rows:] = _head_rows(halo_ref[...])
        s0, s1 = sl_ref[2 * hp], sl_ref[2 * hp + 1]

        def block(i, bias):
            row = _row_start(i)
            rs = pl.ds(row, CHUNK)
            win = pl.ds(row, 2 * CHUNK)
            kc = k_ref[rs, :]
            vc = v_ref[rs, :]
            q2 = _stack_heads(qbuf[win, :])
            do2 = _stack_heads(dobuf[win, :])
            cols = slice(i * CHUNK, (i + 2) * CHUNK)
            lse2 = jnp.concatenate([lse_rows[0:1, cols], lse_rows[1:2, cols]], axis=1)
            dl2 = jnp.concatenate([dl_rows[0:1, cols], dl_rows[1:2, cols]], axis=1)
            pt = jnp.exp(_dot_nt(kc, q2) + bias - lse2)
            dst = pt * (_dot_nt(vc, do2) - dl2)
            dv_ref[rs, :] = _dot(pt.astype(BF16), do2)
            dk_ref[rs, :] = _dot(dst.astype(BF16), q2)

        bias_buf[...] = _key_block_bias(s0, s1, dil, False)

        for i in range(nsub - 1):
            block(i, bias_buf[...])

        @pl.when(ch != nch - 1)
        def _():
            block(nsub - 1, bias_buf[...])

        @pl.when(ch == nch - 1)
        def _():
            block(nsub - 1, _key_block_bias(s0, s1, dil, True))

    sd = jax.ShapeDtypeStruct((dil, length, A), F32)
    return pl.pallas_call(
        body, name=f"attn_dkv_d{dil}", grid=(dil, NH // 2, nch),
        in_specs=[pl.BlockSpec(memory_space=pltpu.SMEM), main, main, main, main, main, main, nxt, nxt, nxt, nxt],
        out_specs=[main, main], out_shape=[sd, sd],
        scratch_shapes=[pltpu.VMEM((rows + CHUNK, LANES), BF16), pltpu.VMEM((rows + CHUNK, LANES), BF16),
                        pltpu.VMEM((8, rows + CHUNK), F32), pltpu.VMEM((8, rows + CHUNK), F32),
                        pltpu.VMEM((CHUNK, 4 * CHUNK), F32)],
        compiler_params=_cparams("arbitrary", "arbitrary", "arbitrary"),
    )(slopes, k, v, q, do, lse, delta, q, do, lse, delta)


def _group_masks(width):
    lane = lax.broadcasted_iota(jnp.int32, (1, width), 1)
    return [(lane >= g * DH) & (lane < (g + 1) * DH) for g in range(width // DH)]


def _group_mean_matrix():
    i = lax.broadcasted_iota(jnp.int32, (GW, GW), 0) // DH
    j = lax.broadcasted_iota(jnp.int32, (GW, GW), 1) // DH
    return jnp.where(i == j, 1.0 / DH, 0.0).astype(F32)


def _tri_mask(lower):
    t = lax.broadcasted_iota(jnp.int32, (CHUNK, CHUNK), 0)
    u = lax.broadcasted_iota(jnp.int32, (CHUNK, CHUNK), 1)
    return (u <= t) if lower else (u >= t)


def _sgu_forward(u, z, lng, lnb, w_ref, bias_t, pmat, rows):
    ug = _gelu(u)
    zg = _gelu(z)
    mu = _dot_hi(zg, pmat)
    zc = zg - mu
    var = _dot_hi(zc * zc, pmat)
    rstd = lax.rsqrt(var + EPS)
    zhat = zc * rstd
    zn = (zhat * lng + lnb).astype(BF16)
    gm = _group_masks(GW)
    tri = _tri_mask(True)
    ws = [jnp.where(tri, w_ref[g], 0.0).astype(BF16) for g in range(NG)]
    pieces = []
    for c in range(rows // CHUNK):
        znc = zn[c * CHUNK:(c + 1) * CHUNK, :]
        mix = None
        for g in range(NG):
            part = jnp.where(gm[g], _dot(ws[g], znc), 0.0)
            mix = part if mix is None else mix + part
        pieces.append(mix + bias_t)
    mixed = jnp.concatenate(pieces, axis=0) if len(pieces) > 1 else pieces[0]
    return ug * mixed, ug, zhat, rstd, zn, mixed


def _mix_fwd(os_, ls_, u, z, x, lng, lnb, sgu_w, bias_t, ga, gg, wout):
    s = x.shape[0]
    nd = len(DILS)
    nscr = sum(1 for d in DILS if d > 1)

    def body(*refs):
        o_refs, l_refs = refs[:nd], refs[nd:2 * nd]
        u_ref, z_ref, x_ref, lng_ref, lnb_ref, w_ref, bt_ref, ga_ref, gg_ref, wo_ref = refs[2 * nd:2 * nd + 10]
        attn_ref = refs[2 * nd + 10]
        lse_refs = refs[2 * nd + 11:3 * nd + 11]
        mixed_ref, h1_ref = refs[3 * nd + 11:3 * nd + 13]
        scr = refs[3 * nd + 13:]
        scr_o, scr_l, scr_lse = scr[:nscr], scr[nscr:2 * nscr], scr[2 * nscr]
        ov, lv, j = [], [], 0
        for di, dil in enumerate(DILS):
            if dil == 1:
                ov.append(o_refs[di][0])
                lv.append(l_refs[di][0])
            else:
                ov.append(_merge_residues(o_refs[di], scr_o[j], dil))
                lv.append(_merge_residues(l_refs[di], scr_l[j], dil))
                j += 1
        mx = functools.reduce(jnp.maximum, lv)
        es = [jnp.exp(l - mx) for l in lv]
        den = functools.reduce(lambda a, b: a + b, es)
        attn = functools.reduce(lambda a, b: a + b, [e * o for e, o in zip(es, ov)]) / den
        attn_ref[...] = attn
        lse = mx + jnp.log(den)
        _fill_cols(scr_lse, lse)
        for di, dil in enumerate(DILS):
            if dil == 1:
                lse_refs[di][0] = lse
            else:
                _split_residues(scr_lse, lse_refs[di], dil)
        an, _, _ = _rms_fwd(attn, ga_ref[...])
        gmv, _, _, _, _, _ = _sgu_forward(u_ref[...], z_ref[...], lng_ref[...], lnb_ref[...], w_ref,
                                          bt_ref[...], _group_mean_matrix(), TMX)
        gn, _, _ = _rms_fwd(gmv, gg_ref[...])
        mixed = jnp.concatenate([an, gn], axis=-1).astype(BF16)
        mixed_ref[...] = mixed
        h1_ref[...] = x_ref[...] + _dot(mixed, wo_ref[...])

    sd = jax.ShapeDtypeStruct
    res = pl.pallas_call(
        body, name="mix_fwd", grid=(s // TMX,),
        in_specs=[_res_spec(d, TMX, A) for d in DILS] * 2 + [_row_spec(TMX, GW), _row_spec(TMX, GW),
                  _row_spec(TMX, D), _const_spec((1, GW)), _const_spec((1, GW)), _const_spec((NG, CHUNK, CHUNK)),
                  _const_spec((CHUNK, GW)), _const_spec((1, A)), _const_spec((1, GW)), _const_spec((D, D))],
        out_specs=[_row_spec(TMX, A)] + [_res_spec(d, TMX, A) for d in DILS] + [_row_spec(TMX, D), _row_spec(TMX, D)],
        out_shape=[sd((s, A), F32)] + [_res_shape(s, d, A, F32) for d in DILS] + [sd((s, D), BF16), sd((s, D), F32)],
        scratch_shapes=[_col_scratch(TMX, A)] * (2 * nscr + 1),
        compiler_params=_cparams("arbitrary"),
    )(*os_, *ls_, u, z, x, lng, lnb, sgu_w, bias_t, ga, gg, wout)
    return res[0], res[1:1 + nd], res[1 + nd], res[2 + nd]


def _mlp_fwd(h1, g2, wff1_t, wff2, gf, target):
    s = h1.shape[0]

    def body(h1_ref, g2_ref, w1_ref, w2_ref, gf_ref, t_ref, hn_ref, rf_ref, a_ref, dh2_ref, loss_ref, dgf_ref):
        i = pl.program_id(0)
        h1v = h1_ref[...]
        hn, _, _ = _rms_fwd(h1v, g2_ref[...])
        hn = hn.astype(BF16)
        hn_ref[...] = hn
        acc = h1v
        for j in range(DFF // FF_CH):
            cols = slice(j * FF_CH, (j + 1) * FF_CH)
            rf = jnp.maximum(_dot_nt(hn, w1_ref[cols, :]), 0.0)
            act = (rf * rf).astype(BF16)
            rf_ref[:, cols] = rf.astype(BF16)
            a_ref[:, cols] = act
            acc = acc + _dot(act, w2_ref[cols, :])
        y, h2n, r3 = _rms_fwd(acc, gf_ref[...])
        err = y - t_ref[...]
        part = 0.5 * jnp.sum(jnp.mean(err * err, axis=-1, keepdims=True), axis=0, keepdims=True)
        dy = err * (1.0 / D)
        dh2, dgf = _rms_bwd(dy, h2n, r3, gf_ref[...])
        dh2_ref[...] = dh2

        @pl.when(i == 0)
        def _():
            loss_ref[...] = jnp.zeros_like(loss_ref)
            dgf_ref[...] = jnp.zeros_like(dgf_ref)

        loss_ref[...] += jnp.broadcast_to(part, loss_ref.shape)
        dgf_ref[...] += dgf

    sd = jax.ShapeDtypeStruct
    return pl.pallas_call(
        body, name="mlp_fwd", grid=(s // TM,),
        in_specs=[_row_spec(TM, D), _const_spec((1, D)), _const_spec((DFF, D)), _const_spec((DFF, D)),
                  _const_spec((1, D)), _row_spec(TM, D)],
        out_specs=[_row_spec(TM, D), _row_spec(TM, DFF), _row_spec(TM, DFF), _row_spec(TM, D),
                   _const_spec((1, LANES)), _const_spec((1, D))],
        out_shape=[sd((s, D), BF16), sd((s, DFF), BF16), sd((s, DFF), BF16), sd((s, D), F32),
                   sd((1, LANES), F32), sd((1, D), F32)],
        compiler_params=_cparams("arbitrary"),
    )(h1, g2, wff1_t, wff2, gf, target)


def _mlp_bwd(dh2, rf, h1, g2, wff1_t, wff2):
    s = h1.shape[0]

    def body(dh2_ref, rf_ref, h1_ref, g2_ref, w1_ref, w2_ref, df_ref, dh1_ref, dg2_ref):
        i = pl.program_id(0)
        dh2v = dh2_ref[...]
        dh2b = dh2v.astype(BF16)
        dhn = jnp.zeros((TM, D), F32)
        for j in range(DFF // FF_CH):
            cols = slice(j * FF_CH, (j + 1) * FF_CH)
            da = _dot_nt(dh2b, w2_ref[cols, :])
            df = (da * (2.0 * rf_ref[:, cols].astype(F32))).astype(BF16)
            df_ref[:, cols] = df
            dhn = dhn + _dot(df, w1_ref[cols, :])
        _, h1n, r2 = _rms_fwd(h1_ref[...], g2_ref[...])
        dres, dg2 = _rms_bwd(dhn, h1n, r2, g2_ref[...])
        dh1_ref[...] = dh2v + dres

        @pl.when(i == 0)
        def _():
            dg2_ref[...] = jnp.zeros_like(dg2_ref)

        dg2_ref[...] += dg2

    sd = jax.ShapeDtypeStruct
    return pl.pallas_call(
        body, name="mlp_bwd", grid=(s // TM,),
        in_specs=[_row_spec(TM, D), _row_spec(TM, DFF), _row_spec(TM, D), _const_spec((1, D)),
                  _const_spec((DFF, D)), _const_spec((DFF, D))],
        out_specs=[_row_spec(TM, DFF), _row_spec(TM, D), _const_spec((1, D))],
        out_shape=[sd((s, DFF), BF16), sd((s, D), F32), sd((1, D), F32)],
        compiler_params=_cparams("arbitrary"),
    )(dh2, rf, h1, g2, wff1_t, wff2)


def _mix_bwd(dh1, attn, u, z, lng, lnb, sgu_w, sgu_wt, bias_t, ga, gg, wout):
    s = dh1.shape[0]
    nsteps = s // TMX
    nd = len(DILS)

    def body(*refs):
        dh1_ref, attn_ref, u_ref, z_ref, lng_ref, lnb_ref, w_ref, wt_ref, bt_ref, ga_ref, gg_ref, wo_ref = refs[:12]
        do_refs, dl_refs = refs[12:12 + nd], refs[12 + nd:12 + 2 * nd]
        (du_ref, dz_ref, dga_ref, dgg_ref, dlng_ref, dlnb_ref, dws_ref, db_ref,
         dbt_acc, scr_do, scr_dl) = refs[12 + 2 * nd:]
        i = pl.program_id(0)

        @pl.when(i == 0)
        def _():
            for r in (dga_ref, dgg_ref, dlng_ref, dlnb_ref, dws_ref, db_ref, dbt_acc):
                r[...] = jnp.zeros_like(r)

        dmixed = _dot_nt(dh1_ref[...].astype(BF16), wo_ref[...])
        attn = attn_ref[...]
        _, an, ra = _rms_fwd(attn, ga_ref[...])
        dattn, dga = _rms_bwd(dmixed[:, :A], an, ra, ga_ref[...])
        dga_ref[...] += dga
        _fill_cols(scr_do, dattn)
        prod = dattn * attn
        delta = jnp.zeros_like(prod)
        for hm in _group_masks(A):
            delta = delta + jnp.where(hm, jnp.sum(jnp.where(hm, prod, 0.0), axis=-1, keepdims=True), 0.0)
        _fill_cols(scr_dl, delta)
        for di, dil in enumerate(DILS):
            if dil == 1:
                do_refs[di][0] = dattn.astype(BF16)
                dl_refs[di][0] = delta
            else:
                _split_residues(scr_do, do_refs[di], dil)
                _split_residues(scr_dl, dl_refs[di], dil)
        pmat = _group_mean_matrix()
        lng = lng_ref[...]
        uv, zv = u_ref[...], z_ref[...]
        gmv, ug, zhat, rstd, zn, mixed = _sgu_forward(uv, zv, lng, lnb_ref[...], w_ref, bt_ref[...], pmat, TMX)
        _, gmn, rg = _rms_fwd(gmv, gg_ref[...])
        dgm, dgg = _rms_bwd(dmixed[:, A:], gmn, rg, gg_ref[...])
        dgg_ref[...] += dgg
        du_ref[...] = dgm * mixed * _gelu_grad(uv)
        dmx = dgm * ug
        dmxb = dmx.astype(BF16)
        gm = _group_masks(GW)
        tri_t = _tri_mask(False)
        wst = [jnp.where(tri_t, wt_ref[g], 0.0).astype(BF16) for g in range(NG)]
        zero = jnp.zeros((CHUNK, GW), BF16)
        dzn_pieces = []
        for c in range(TMX // CHUNK):
            rs = slice(c * CHUNK, (c + 1) * CHUNK)
            dmc = dmxb[rs, :]
            znc = zn[rs, :]
            dbt_acc[...] += dmx[rs, :]
            dzn = None
            for g in range(NG):
                dws_ref[g] += _dot_nt(jnp.where(gm[g], dmc, zero), znc)
                part = jnp.where(gm[g], _dot(wst[g], dmc), 0.0)
                dzn = part if dzn is None else dzn + part
            dzn_pieces.append(dzn)
        dzn = jnp.concatenate(dzn_pieces, axis=0)
        dlng_ref[...] += jnp.sum(dzn * zhat, axis=0, keepdims=True)
        dlnb_ref[...] += jnp.sum(dzn, axis=0, keepdims=True)
        dzh = dzn * lng
        dzg = rstd * (dzh - _dot_hi(dzh, pmat) - zhat * _dot_hi(dzh * zhat, pmat))
        dz_ref[...] = dzg * _gelu_grad(zv)

        @pl.when(i == nsteps - 1)
        def _():
            tri = _tri_mask(True)
            for g in range(NG):
                dws_ref[g] = jnp.where(tri, dws_ref[g], 0.0)
            acc = dbt_acc[...]
            lane = lax.broadcasted_iota(jnp.int32, (CHUNK, LANES), 1)
            out = jnp.zeros((CHUNK, LANES), F32)
            for g in range(NG):
                sg = jnp.sum(jnp.where(gm[g], acc, 0.0), axis=-1, keepdims=True)
                out = jnp.where(lane == g, sg, out)
            db_ref[...] = out

    sd = jax.ShapeDtypeStruct
    res = pl.pallas_call(
        body, name="mix_bwd", grid=(nsteps,),
        in_specs=[_row_spec(TMX, D), _row_spec(TMX, A), _row_spec(TMX, GW), _row_spec(TMX, GW),
                  _const_spec((1, GW)), _const_spec((1, GW)), _const_spec((NG, CHUNK, CHUNK)),
                  _const_spec((NG, CHUNK, CHUNK)), _const_spec((CHUNK, GW)), _const_spec((1, A)),
                  _const_spec((1, GW)), _const_spec((D, D))],
        out_specs=[_res_spec(d, TMX, A) for d in DILS] * 2 + [_row_spec(TMX, GW), _row_spec(TMX, GW),
                   _const_spec((1, A)), _const_spec((1, GW)), _const_spec((1, GW)), _const_spec((1, GW)),
                   _const_spec((NG, CHUNK, CHUNK)), _const_spec((CHUNK, LANES))],
        out_shape=[_res_shape(s, d, A, BF16) for d in DILS] + [_res_shape(s, d, A, F32) for d in DILS]
                  + [sd((s, GW), F32), sd((s, GW), F32),
                   sd((1, A), F32), sd((1, GW), F32), sd((1, GW), F32), sd((1, GW), F32),
                   sd((NG, CHUNK, CHUNK), F32), sd((CHUNK, LANES), F32)],
        scratch_shapes=[pltpu.VMEM((CHUNK, GW), F32), _col_scratch(TMX, A), _col_scratch(TMX, A)],
        compiler_params=_cparams("arbitrary"),
    )(dh1, attn, u, z, lng, lnb, sgu_w, sgu_wt, bias_t, ga, gg, wout)
    return (res[:nd], res[nd:2 * nd]) + tuple(res[2 * nd:])


def _inproj_bwd(dqs, dks, dvs, du, dz, dh1, x, g1, win_t):
    s = x.shape[0]
    nd = len(DILS)
    nscr = sum(1 for d in DILS if d > 1)

    def body(*refs):
        parts = [refs[t * nd:(t + 1) * nd] for t in range(3)]
        du_ref, dz_ref, dh1_ref, x_ref, g_ref, w_ref, dp_ref, dx_ref, dg_ref = refs[3 * nd:3 * nd + 9]
        scr = refs[3 * nd + 9:]
        i = pl.program_id(0)
        sums = []
        for t in range(3):
            total, j = None, 0
            for di, dil in enumerate(DILS):
                if dil == 1:
                    term = parts[t][di][0]
                else:
                    term = _merge_residues(parts[t][di], scr[t * nscr + j], dil)
                    j += 1
                total = term if total is None else total + term
            sums.append(total)
        dp = jnp.concatenate([sums[0] * SCALE, sums[1], sums[2], du_ref[...], dz_ref[...]], axis=-1).astype(BF16)
        dp_ref[...] = dp
        dhn = _dot(dp, w_ref[...])
        _, xn, r1 = _rms_fwd(x_ref[...], g_ref[...])
        dres, dg = _rms_bwd(dhn, xn, r1, g_ref[...])
        dx_ref[...] = dh1_ref[...] + dres

        @pl.when(i == 0)
        def _():
            dg_ref[...] = jnp.zeros_like(dg_ref)

        dg_ref[...] += dg

    sd = jax.ShapeDtypeStruct
    return pl.pallas_call(
        body, name="inproj_bwd", grid=(s // TMX,),
        in_specs=[_res_spec(d, TMX, A) for d in DILS] * 3 + [_row_spec(TMX, GW)] * 2
                 + [_row_spec(TMX, D), _row_spec(TMX, D), _const_spec((1, D)), _const_spec((INW, D))],
        out_specs=[_row_spec(TMX, INW), _row_spec(TMX, D), _const_spec((1, D))],
        out_shape=[sd((s, INW), BF16), sd((s, D), F32), sd((1, D), F32)],
        scratch_shapes=[_col_scratch(TMX, A)] * (3 * nscr),
        compiler_params=_cparams("arbitrary"),
    )(*dqs, *dks, *dvs, du, dz, dh1, x, g1, win_t)


def _wgrad(a, b, name, bm, bn, bk=TM):
    s, m = a.shape
    n = b.shape[1]
    bm, bn = min(bm, m), min(bn, n)

    def body(a_ref, b_ref, o_ref):
        @pl.when(pl.program_id(2) == 0)
        def _():
            o_ref[...] = jnp.zeros_like(o_ref)

        o_ref[...] += _dot_tn(a_ref[...].astype(BF16), b_ref[...].astype(BF16))

    return pl.pallas_call(
        body, name=name, grid=(m // bm, n // bn, s // bk),
        in_specs=[pl.BlockSpec((bk, bm), lambda i, j, k: (k, i)), pl.BlockSpec((bk, bn), lambda i, j, k: (k, j))],
        out_specs=pl.BlockSpec((bm, bn), lambda i, j, k: (i, j)),
        out_shape=jax.ShapeDtypeStruct((m, n), F32),
        compiler_params=_cparams("arbitrary", "arbitrary", "arbitrary"),
    )(a, b)


def _adamw_math(w, g, m, v):
    m = B1 * m + (1.0 - B1) * g
    v = B2 * v + (1.0 - B2) * (g * g)
    m_hat = m / (1.0 - B1 ** STEP)
    v_hat = v / (1.0 - B2 ** STEP)
    delta = -LR * (m_hat / (jnp.sqrt(v_hat) + AEPS) + WD * w)
    return delta, m, v


def _adamw(w, g, m, v, name):
    rows, cols = w.shape
    br = min(rows, 256)
    while rows % br:
        br -= 8

    def body(w_ref, g_ref, m_ref, v_ref, d_ref, mo_ref, vo_ref):
        d, mn, vn = _adamw_math(w_ref[...], g_ref[...], m_ref[...], v_ref[...])
        d_ref[...] = d
        mo_ref[...] = mn
        vo_ref[...] = vn

    spec = _row_spec(br, cols)
    sd = jax.ShapeDtypeStruct((rows, cols), F32)
    return pl.pallas_call(
        body, name=name, grid=(rows // br,), in_specs=[spec] * 4, out_specs=[spec] * 3,
        out_shape=[sd, sd, sd], compiler_params=_cparams("arbitrary"),
    )(w, g, m, v)


def _local_step(x, target, small, win_t, rest_weights, early_grads=None, after_attention_bwd=None):
    slopes = jnp.asarray(_alibi_slopes(NH))
    hn1, q, k, v, u, z = _inproj_fwd(x, small["norm1_g"], win_t)
    outs, lses = [], []
    for i, dil in enumerate(DILS):
        o, l = _attn_fwd(q[i], k[i], v[i], slopes, dil)
        outs.append(o)
        lses.append(l)
    wout, wff1_t, wff2 = rest_weights(lses[-1])
    attn, lse, mixed, h1 = _mix_fwd(outs, lses, u, z, x, small["ln_g"], small["ln_b"], small["sgu_w"],
                                    small["bias_t"], small["attn_out_g"], small["gmlp_out_g"], wout)
    hn2, rf, act, dh2, loss, dgf = _mlp_fwd(h1, small["norm2_g"], wff1_t, wff2, small["final_norm_g"], target)
    df, dh1, dg2 = _mlp_bwd(dh2, rf, h1, small["norm2_g"], wff1_t, wff2)
    gwff1_t = _wgrad(df, hn2, "wgrad_ff1", 1024, D)
    gwff2 = _wgrad(act, dh2, "wgrad_ff2", 1024, D)
    gwout = _wgrad(mixed, dh1, "wgrad_out", D, D)
    ga, g1 = small["attn_out_g"], small["norm1_g"]
    pin = early_grads(gwff1_t, gwff2, gwout) if early_grads else None
    if pin is not None:
        ga = ga + pin
    (do, delta, du, dz, dga, dgg, dlng, dlnb, dws, db) = _mix_bwd(
        dh1, attn, u, z, small["ln_g"], small["ln_b"], small["sgu_w"], small["sgu_wt"], small["bias_t"],
        ga, small["gmlp_out_g"], wout)
    dqs, dks, dvs = [], [], []
    for i, dil in enumerate(DILS):
        dqs.append(_attn_bwd_dq(q[i], k[i], v[i], do[i], lse[i], delta[i], slopes, dil))
        dk, dv = _attn_bwd_dkv(q[i], k[i], v[i], do[i], lse[i], delta[i], slopes, dil)
        dks.append(dk)
        dvs.append(dv)
    pin = after_attention_bwd(dvs[-1]) if after_attention_bwd else None
    if pin is not None:
        g1 = g1 + pin
    dproj, dx, dg1 = _inproj_bwd(dqs, dks, dvs, du, dz, dh1, x, g1, win_t)
    gwin_t = _wgrad(dproj, hn1, "wgrad_in", INW // 2, D)
    small_grads = dict(norm1_g=dg1, ln_g=dlng, ln_b=dlnb, sgu_w=dws, sgu_b=db[:, :NG].T,
                       attn_out_g=dga, gmlp_out_g=dgg, norm2_g=dg2, final_norm_g=dgf)
    return loss[0, 0], dx, small_grads, (gwin_t, gwout, gwff1_t, gwff2)


ANY = pl.BlockSpec(memory_space=pl.ANY)
NDEV = 8


def _position():
    return lax.axis_index("x"), lax.axis_index("y"), lax.axis_index("c")


def _other_chips(x, y):
    return [(1 - x, y), (x, 1 - y), (1 - x, 1 - y)]


def _remote(src, dst, send_sem, recv_sem, device):
    return pltpu.make_async_remote_copy(src_ref=src, dst_ref=dst, send_sem=send_sem, recv_sem=recv_sem,
                                        device_id=device, device_id_type=MESH)


HBM = pl.BlockSpec(memory_space=pltpu.HBM)
SEM = pl.BlockSpec(memory_space=pltpu.SEMAPHORE)
DATAFLOW = pltpu.SideEffectType.DATAFLOW_SIDE_EFFECTING


def _in_hbm(a):
    return pltpu.with_memory_space_constraint(a, pltpu.HBM)


def _gather_start(shard, name):
    def body(w_ref, land_ref, send_sems, recv_sems, w_thru, land_thru, token):
        x, y, c = _position()
        for k, (px, py) in enumerate(_other_chips(x, y)):
            _remote(w_ref, land_ref.at[2 * x + y], send_sems.at[k], recv_sems.at[k], (px, py, c)).start()
        token[...] = jnp.zeros_like(token)

    land = jnp.broadcast_to(shard[None], (NCHIP,) + shard.shape)
    return pl.pallas_call(
        body, name=name,
        out_shape=(pltpu.SemaphoreType.DMA((3,)), pltpu.SemaphoreType.DMA((3,)),
                   pltpu.HBM(shard.shape, shard.dtype), pltpu.HBM(land.shape, land.dtype),
                   jax.ShapeDtypeStruct((8, LANES), F32)),
        in_specs=(HBM, HBM), out_specs=(SEM, SEM, HBM, HBM, pl.BlockSpec(memory_space=pltpu.VMEM)),
        input_output_aliases={0: 2, 1: 3},
        compiler_params=pltpu.CompilerParams(has_side_effects=DATAFLOW),
    )(_in_hbm(shard), _in_hbm(land))


def _gather_wait(send_sems, recv_sems, w_thru, land_thru, after, name):
    def body(w_ref, land_ref, send_sems, recv_sems, after_ref, w_dead, got_ref):
        x, y, c = _position()
        for k, (px, py) in enumerate(_other_chips(x, y)):
            cp = _remote(w_ref, land_ref.at[2 * px + py], send_sems.at[k], recv_sems.at[k], (px, py, c))
            cp.wait_send()
            cp.wait_recv()

    return pl.pallas_call(
        body, name=name,
        out_shape=(pltpu.HBM(w_thru.shape, w_thru.dtype), pltpu.HBM(land_thru.shape, land_thru.dtype)),
        in_specs=(HBM, HBM, SEM, SEM, ANY), out_specs=(HBM, HBM),
        input_output_aliases={0: 0, 1: 1},
        compiler_params=pltpu.CompilerParams(has_side_effects=DATAFLOW),
    )(w_thru, land_thru, send_sems, recv_sems, after)[1]


def _xor_peers(x, y, c):
    peers = []
    for k in range(1, NDEV):
        kx, ky, kc = (k >> 2) & 1, (k >> 1) & 1, k & 1
        peers.append((1 - x if kx else x, 1 - y if ky else y, 1 - c if kc else c))
    return peers


def _piece(part_ref, px, py, pc):
    half = part_ref.shape[1] // 2
    return part_ref.at[2 * px + py, pl.ds(pc * half, half), :]


def _split_call(body, name, operands, n_sems, extra_out=()):
    n = len(operands)
    sems = tuple(pltpu.SemaphoreType.DMA((m,)) for m in n_sems)
    thru = tuple(pltpu.HBM(a.shape, a.dtype) for a in operands)
    return pl.pallas_call(
        body, name=name, out_shape=sems + thru + tuple(extra_out),
        in_specs=(HBM,) * n,
        out_specs=(SEM,) * len(sems) + (HBM,) * n + (pl.BlockSpec(memory_space=pltpu.VMEM),) * len(extra_out),
        input_output_aliases={i: len(sems) + i for i in range(n)},
        compiler_params=pltpu.CompilerParams(has_side_effects=DATAFLOW),
    )(*[_in_hbm(a) for a in operands])


TOKEN = jax.ShapeDtypeStruct((8, LANES), F32)


def _reduce_start(parts, name):
    nw = len(parts)
    lands = [lax.empty((NDEV - 1, p.shape[1] // 2, D), F32) for p in parts]

    def body(*refs):
        part_refs, land_refs = refs[:nw], refs[nw:2 * nw]
        send_sems, recv_sems = refs[2 * nw:2 * nw + 2]
        token = refs[-1]
        x, y, c = _position()
        for w in range(nw):
            for k, peer in enumerate(_xor_peers(x, y, c)):
                n = w * (NDEV - 1) + k
                _remote(_piece(part_refs[w], *peer), land_refs[w].at[k], send_sems.at[n], recv_sems.at[n],
                        peer).start()
        token[...] = jnp.zeros_like(token)

    n = nw * (NDEV - 1)
    res = _split_call(body, name, list(parts) + lands, (n, n), (TOKEN,))
    return res[0], res[1], res[2:2 + nw], res[2 + nw:2 + 2 * nw], res[-1]


def _reduce_wait(send_sems, recv_sems, parts, lands, after, name):
    nw = len(parts)

    def body(*refs):
        part_refs, land_refs = refs[:nw], refs[nw:2 * nw]
        send_sems, recv_sems = refs[2 * nw:2 * nw + 2]
        x, y, c = _position()
        for w in range(nw):
            for k, peer in enumerate(_xor_peers(x, y, c)):
                n = w * (NDEV - 1) + k
                cp = _remote(_piece(part_refs[w], *peer), land_refs[w].at[k], send_sems.at[n], recv_sems.at[n], peer)
                cp.wait_send()
                cp.wait_recv()

    operands = list(parts) + list(lands)
    res = pl.pallas_call(
        body, name=name, out_shape=tuple(pltpu.HBM(a.shape, a.dtype) for a in operands),
        in_specs=(HBM,) * (2 * nw) + (SEM, SEM, ANY), out_specs=(HBM,) * (2 * nw),
        input_output_aliases={i: i for i in range(2 * nw)},
        compiler_params=pltpu.CompilerParams(has_side_effects=DATAFLOW),
    )(*operands, send_sems, recv_sems, after)
    return res[nw:]


def _sum_pieces(part, land, sel, name):
    half = part.shape[1] // 2
    br = 128 if half % 128 == 0 else half // 2
    nb = half // br

    def body(sel_ref, own_ref, *refs):
        acc = own_ref[...]
        for r in refs[:NDEV - 1]:
            acc = acc + r[...]
        refs[NDEV - 1][...] = acc

    own_spec = pl.BlockSpec((None, br, D), lambda i, sel_ref: (sel_ref[0], sel_ref[1] * nb + i, 0))
    slot_specs = [pl.BlockSpec((None, br, D), functools.partial(lambda i, sel_ref, k: (k, i, 0), k=k))
                  for k in range(NDEV - 1)]
    return pl.pallas_call(
        body, name=name,
        grid_spec=pltpu.PrefetchScalarGridSpec(
            num_scalar_prefetch=1, grid=(nb,), in_specs=[own_spec] + slot_specs,
            out_specs=pl.BlockSpec((br, D), lambda i, sel_ref: (i, 0))),
        out_shape=jax.ShapeDtypeStruct((half, D), F32),
        compiler_params=_cparams("arbitrary"),
    )(sel, part, *([land] * (NDEV - 1)))


def _share_start(halves, name):
    nw = len(halves)
    lands = [lax.empty(h.shape, F32) for h in halves]

    def body(*refs):
        h_refs, land_refs = refs[:nw], refs[nw:2 * nw]
        send_sems, recv_sems = refs[2 * nw:2 * nw + 2]
        token = refs[-1]
        x, y, c = _position()
        for w in range(nw):
            _remote(h_refs[w], land_refs[w], send_sems.at[w], recv_sems.at[w], (x, y, 1 - c)).start()
        token[...] = jnp.zeros_like(token)

    res = _split_call(body, name, list(halves) + lands, (nw, nw), (TOKEN,))
    return res[0], res[1], res[2:2 + nw], res[2 + nw:2 + 2 * nw], res[-1]


def _share_wait(send_sems, recv_sems, halves, lands, after, name):
    nw = len(halves)

    def body(*refs):
        h_refs, land_refs = refs[:nw], refs[nw:2 * nw]
        send_sems, recv_sems = refs[2 * nw:2 * nw + 2]
        x, y, c = _position()
        for w in range(nw):
            cp = _remote(h_refs[w], land_refs[w], send_sems.at[w], recv_sems.at[w], (x, y, 1 - c))
            cp.wait_send()
            cp.wait_recv()

    operands = list(halves) + list(lands)
    res = pl.pallas_call(
        body, name=name, out_shape=tuple(pltpu.HBM(a.shape, a.dtype) for a in operands),
        in_specs=(HBM,) * (2 * nw) + (SEM, SEM, ANY), out_specs=(HBM,) * (2 * nw),
        input_output_aliases={i: i for i in range(2 * nw)},
        compiler_params=pltpu.CompilerParams(has_side_effects=DATAFLOW),
    )(*operands, send_sems, recv_sems, after)
    return res[:nw], res[nw:]


def _join_halves(own, other, c):
    first = jnp.where(c == 0, own, other)
    second = jnp.where(c == 0, other, own)
    return jnp.concatenate([first, second], axis=0)


SMALL_SIZES = (("norm1_g", D), ("sgu_ln_g", GW), ("sgu_ln_b", GW), ("sgu_w", NG * CHUNK * CHUNK),
               ("sgu_b", NG * CHUNK), ("attn_out_g", A), ("gmlp_out_g", GW), ("norm2_g", D),
               ("final_norm_g", D))
SMALL_ROWS = sum(n for _, n in SMALL_SIZES) // LANES


def _pack_small(tree):
    return jnp.concatenate([tree[n].reshape(-1) for n, _ in SMALL_SIZES]).reshape(SMALL_ROWS, LANES)


def _unpack_small(pack, shapes):
    flat = pack.reshape(-1)
    out, off = {}, 0
    for n, size in SMALL_SIZES:
        out[n] = flat[off:off + size].reshape(shapes[n])
        off += size
    return out


def _small_allreduce_adamw(gpack, wpack, mpack, vpack):
    def body(g_ref, w_ref, m_ref, v_ref, go_ref, d_ref, mo_ref, vo_ref, slots, send_sems, recv_sems):
        x, y, c = _position()
        me = 4 * x + 2 * y + c
        slots[me] = g_ref[...]
        peers = _xor_peers(x, y, c)
        sends = []
        for k, peer in enumerate(peers):
            cp = _remote(g_ref, slots.at[me], send_sems.at[k], recv_sems.at[k], peer)
            cp.start()
            sends.append(cp)
        for k, (px, py, pc) in enumerate(peers):
            _remote(g_ref, slots.at[4 * px + 2 * py + pc], send_sems.at[k], recv_sems.at[k],
                    (px, py, pc)).wait_recv()
        for cp in sends:
            cp.wait_send()
        total = slots[0]
        for k in range(1, NDEV):
            total = total + slots[k]
        go_ref[...] = total
        d, mn, vn = _adamw_math(w_ref[...], total, m_ref[...], v_ref[...])
        d_ref[...] = d
        mo_ref[...] = mn
        vo_ref[...] = vn

    sd = jax.ShapeDtypeStruct((SMALL_ROWS, LANES), F32)
    vm = pl.BlockSpec(memory_space=pltpu.VMEM)
    return pl.pallas_call(
        body, name="small_allreduce_adamw", in_specs=[vm] * 4, out_specs=[vm] * 4, out_shape=[sd] * 4,
        scratch_shapes=[pltpu.VMEM((NDEV, SMALL_ROWS, LANES), F32), pltpu.SemaphoreType.DMA((NDEV - 1,)),
                        pltpu.SemaphoreType.DMA((NDEV - 1,))],
        compiler_params=pltpu.CompilerParams(has_side_effects=True),
    )(gpack, wpack, mpack, vpack)


def kernel(x, norm1_g, w_in, sgu_ln_g, sgu_ln_b, sgu_w, sgu_b, attn_out_g, gmlp_out_g, w_out, norm2_g, w_ff1, w_ff2, final_norm_g, loss_target, m_norm1_g, m_w_in, m_sgu_ln_g, m_sgu_ln_b, m_sgu_w, m_sgu_b, m_attn_out_g, m_gmlp_out_g, m_w_out, m_norm2_g, m_w_ff1, m_w_ff2, m_final_norm_g, v_norm1_g, v_w_in, v_sgu_ln_g, v_sgu_ln_b, v_sgu_w, v_sgu_b, v_attn_out_g, v_gmlp_out_g, v_w_out, v_norm2_g, v_w_ff1, v_w_ff2, v_final_norm_g):
    names = [n for n, _ in SMALL_SIZES]
    w_small = dict(norm1_g=norm1_g, sgu_ln_g=sgu_ln_g, sgu_ln_b=sgu_ln_b, sgu_w=sgu_w, sgu_b=sgu_b,
                   attn_out_g=attn_out_g, gmlp_out_g=gmlp_out_g, norm2_g=norm2_g, final_norm_g=final_norm_g)
    m_small = dict(norm1_g=m_norm1_g, sgu_ln_g=m_sgu_ln_g, sgu_ln_b=m_sgu_ln_b, sgu_w=m_sgu_w, sgu_b=m_sgu_b,
                   attn_out_g=m_attn_out_g, gmlp_out_g=m_gmlp_out_g, norm2_g=m_norm2_g,
                   final_norm_g=m_final_norm_g)
    v_small = dict(norm1_g=v_norm1_g, sgu_ln_g=v_sgu_ln_g, sgu_ln_b=v_sgu_ln_b, sgu_w=v_sgu_w, sgu_b=v_sgu_b,
                   attn_out_g=v_attn_out_g, gmlp_out_g=v_gmlp_out_g, norm2_g=v_norm2_g,
                   final_norm_g=v_final_norm_g)
    shapes = {n: w_small[n].shape for n in names}

    r_in, r_out, r_ff = INW // NCHIP, D // NCHIP, DFF // NCHIP
    o1, o2, o3 = r_in, r_in + r_out, r_in + r_out + r_ff
    start_in = _gather_start(w_in[0].T.astype(BF16), "gather_in_start")
    start_rest = _gather_start(jnp.concatenate([w_out[0], w_ff1[0].T, w_ff2[0]], axis=0).astype(BF16),
                               "gather_rest_start")
    win_t = _gather_wait(*start_in[:4], after=start_rest[4], name="gather_in_wait").reshape(INW, D)

    def rest_weights(after):
        rest = _gather_wait(*start_rest[:4], after=after, name="gather_rest_wait")
        return (rest[:, :r_out].reshape(D, D), rest[:, r_out:r_out + r_ff].reshape(DFF, D),
                rest[:, r_out + r_ff:].reshape(DFF, D))

    small = dict(
        norm1_g=norm1_g, ln_g=sgu_ln_g.reshape(1, GW), ln_b=sgu_ln_b.reshape(1, GW), sgu_w=sgu_w[0],
        sgu_wt=jnp.swapaxes(sgu_w[0], 1, 2), bias_t=jnp.repeat(sgu_b[0].T, DH, axis=1),
        attn_out_g=attn_out_g, gmlp_out_g=gmlp_out_g, norm2_g=norm2_g, final_norm_g=final_norm_g.reshape(1, D))
    xi, yi, ci = _position()
    sel = jnp.stack([2 * xi + yi, ci]).astype(jnp.int32)
    state = {}

    def as_slabs(g):
        return g.reshape(NCHIP, g.shape[0] // NCHIP, D)

    def early_grads(gwff1_t, gwff2, gwout):
        state["early"] = _reduce_start([as_slabs(gwff1_t), as_slabs(gwff2), as_slabs(gwout)], "reduce_early_start")
        return state["early"][4][0:1, 0:1]

    def after_attention_bwd(marker):
        send_sems, recv_sems, parts, lands, _ = state["early"]
        lands = _reduce_wait(send_sems, recv_sems, parts, lands, marker, "reduce_early_wait")
        halves = [_sum_pieces(p, l, sel, "sum_" + n) for p, l, n in zip(parts, lands, ("w_ff1", "w_ff2", "w_out"))]
        state["early_share"] = _share_start(halves, "share_early_start")
        return state["early_share"][4][0:1, 0:1]

    loss_part, dx, sg, (gwin_t, gwout, gwff1_t, gwff2) = _local_step(
        x[0], loss_target[0], small, win_t, rest_weights, early_grads, after_attention_bwd)
    loss = lax.psum(loss_part, ("x", "y", "c"))

    late = _reduce_start([as_slabs(gwin_t)], "reduce_late_start")
    send_sems, recv_sems, halves, lands, _ = state["early_share"]
    own, other = _share_wait(send_sems, recv_sems, halves, lands, late[4], "share_early_wait")
    g_big = {n: _join_halves(o, t, ci) for n, o, t in zip(("w_ff1", "w_ff2", "w_out"), own, other)}
    g_big["w_ff1"] = g_big["w_ff1"].T
    w_big = dict(w_in=(w_in, m_w_in, v_w_in), w_out=(w_out, m_w_out, v_w_out),
                 w_ff1=(w_ff1, m_w_ff1, v_w_ff1), w_ff2=(w_ff2, m_w_ff2, v_w_ff2))
    grads, deltas, new_m, new_v = {}, {}, {}, {}

    def update(n):
        w, m, v = w_big[n]
        d, mn, vn = _adamw(w[0], g_big[n], m[0], v[0], "adamw_" + n)
        grads[n], deltas[n], new_m[n], new_v[n] = g_big[n][None], d[None], mn[None], vn[None]

    for n in ("w_ff1", "w_ff2", "w_out"):
        update(n)
    late_lands = _reduce_wait(late[0], late[1], late[2], late[3], deltas["w_out"], "reduce_late_wait")
    late_share = _share_start([_sum_pieces(late[2][0], late_lands[0], sel, "sum_w_in")], "share_late_start")

    g_small = dict(norm1_g=sg["norm1_g"], sgu_ln_g=sg["ln_g"], sgu_ln_b=sg["ln_b"], sgu_w=sg["sgu_w"],
                   sgu_b=sg["sgu_b"], attn_out_g=sg["attn_out_g"], gmlp_out_g=sg["gmlp_out_g"],
                   norm2_g=sg["norm2_g"], final_norm_g=sg["final_norm_g"])
    packs = _small_allreduce_adamw(_pack_small(g_small) + late_share[4][0:1, 0:1], _pack_small(w_small),
                                   _pack_small(m_small), _pack_small(v_small))
    for tree, pack in zip((grads, deltas, new_m, new_v), packs):
        tree.update(_unpack_small(pack, shapes))
    own, other = _share_wait(late_share[0], late_share[1], late_share[2], late_share[3], packs[0], "share_late_wait")
    g_big["w_in"] = _join_halves(own[0], other[0], ci).T
    update("w_in")

    order = ["norm1_g", "w_in", "sgu_ln_g", "sgu_ln_b", "sgu_w", "sgu_b", "attn_out_g", "gmlp_out_g", "w_out",
             "norm2_g", "w_ff1", "w_ff2", "final_norm_g"]
    return (loss, dx[None], *[grads[n] for n in order], *[deltas[n] for n in order],
            *[new_m[n] for n in order], *[new_v[n] for n in order])
```

```python
import functools
import math

import numpy as np
import jax
import jax.numpy as jnp
from jax import lax
from jax.experimental import pallas as pl
from jax.experimental.pallas import tpu as pltpu

F32 = jnp.float32
BF16 = jnp.bfloat16

D = 1024
NH = 12
DH = 64
A = NH * DH
NG = 4
GW = NG * DH
INW = 3 * A + 2 * GW
DFF = 4 * D
CHUNK = 128
PATTERNS = ((128, 1), (512, 4), (2048, 16))
EPS = 1e-6
SCALE = DH ** -0.5
NEG = -1e30

LR, B1, B2, AEPS, WD, STEP = 0.001, 0.9, 0.999, 1e-08, 0.01, 10

TM = 512
TMX = 256
ATT_ROWS = 1024
FF_CH = 1024
LANES = 128
NCHIP = 4
VMEM_LIMIT = 56 * 1024 * 1024
MESH = pl.DeviceIdType.MESH


def _cparams(*sem, **kw):
    return pltpu.CompilerParams(dimension_semantics=sem if sem else None,
                                vmem_limit_bytes=VMEM_LIMIT, **kw)


def _dot(a, b):
    return jnp.dot(a, b, preferred_element_type=F32)


def _dot_nt(a, b):
    return lax.dot_general(a, b, (((1,), (1,)), ((), ())), preferred_element_type=F32)


def _dot_tn(a, b):
    return lax.dot_general(a, b, (((0,), (0,)), ((), ())), preferred_element_type=F32)


def _dot_hi(a, b):
    return jnp.dot(a, b, preferred_element_type=F32, precision=lax.Precision.HIGHEST)


def _alibi_slopes(n):
    def pow2(m):
        start = 2.0 ** (-8.0 / m)
        return [start ** (i + 1) for i in range(m)]
    if math.log2(n).is_integer():
        s = pow2(n)
    else:
        c = 2 ** int(math.floor(math.log2(n)))
        s = pow2(c) + pow2(2 * c)[0::2][: n - c]
    return np.asarray(s, dtype=np.float32)


def _rms_fwd(v, g):
    r = lax.rsqrt(jnp.mean(v * v, axis=-1, keepdims=True) + EPS)
    vn = v * r
    return vn * g, vn, r


def _rms_bwd(dy, vn, r, g):
    w = dy * g
    dv = r * (w - vn * jnp.mean(w * vn, axis=-1, keepdims=True))
    return dv, jnp.sum(dy * vn, axis=0, keepdims=True)


_K0 = math.sqrt(2.0 / math.pi)
_K1 = 0.044715


def _gelu(v):
    return 0.5 * v * (1.0 + jnp.tanh(_K0 * (v + _K1 * (v * v * v))))


def _gelu_grad(v):
    t = jnp.tanh(_K0 * (v + _K1 * (v * v * v)))
    return 0.5 * (1.0 + t) + 0.5 * v * (1.0 - t * t) * (_K0 * (1.0 + 3.0 * _K1 * v * v))


def _row_spec(rows, cols):
    return pl.BlockSpec((rows, cols), lambda i: (i, 0))


def _const_spec(shape):
    nd = len(shape)
    return pl.BlockSpec(shape, lambda i: (0,) * nd, pipeline_mode=pl.Buffered(1))


DILS = tuple(d for _, d in PATTERNS)


def _fill_cols(scr, value):
    for cb in range(value.shape[1] // LANES):
        scr[cb] = value[:, cb * LANES:(cb + 1) * LANES]


def _split_residues(scr, out_ref, dil):
    nb, rows, _ = scr.shape
    for r in range(dil):
        for cb in range(nb):
            piece = scr.at[cb][pl.ds(r, rows // dil, stride=dil), :]
            out_ref[r, :, cb * LANES:(cb + 1) * LANES] = piece.astype(out_ref.dtype)


def _merge_residues(in_ref, scr, dil):
    nb, rows, _ = scr.shape
    for r in range(dil):
        for cb in range(nb):
            scr.at[cb][pl.ds(r, rows // dil, stride=dil), :] = in_ref[r, :, cb * LANES:(cb + 1) * LANES]
    return jnp.concatenate([scr[cb] for cb in range(nb)], axis=-1)


def _col_scratch(rows, width):
    return pltpu.VMEM((width // LANES, rows, LANES), F32)


def _res_spec(dil, rows, width):
    return pl.BlockSpec((dil, rows // dil, width), lambda i: (0, i, 0))


def _res_shape(s, dil, width, dtype):
    return jax.ShapeDtypeStruct((dil, s // dil, width), dtype)


def _inproj_fwd(x, g1, win_t):
    s = x.shape[0]
    nd = len(DILS)

    def body(x_ref, g_ref, w_ref, hn_ref, *rest):
        qkv_refs = rest[:3 * nd]
        u_ref, z_ref, scr = rest[3 * nd:]
        hn, _, _ = _rms_fwd(x_ref[...], g_ref[...])
        hn = hn.astype(BF16)
        hn_ref[...] = hn
        for t in range(3):
            seg = _dot_nt(hn, w_ref[t * A:(t + 1) * A, :])
            seg = seg * SCALE if t == 0 else seg
            _fill_cols(scr, seg)
            for di, dil in enumerate(DILS):
                if dil == 1:
                    qkv_refs[t * nd + di][0] = seg.astype(BF16)
                else:
                    _split_residues(scr, qkv_refs[t * nd + di], dil)
        u_ref[...] = _dot_nt(hn, w_ref[3 * A:3 * A + GW, :])
        z_ref[...] = _dot_nt(hn, w_ref[3 * A + GW:INW, :])

    res = pl.pallas_call(
        body, name="inproj_fwd", grid=(s // TM,),
        in_specs=[_row_spec(TM, D), _const_spec((1, D)), _const_spec((INW, D))],
        out_specs=[_row_spec(TM, D)] + [_res_spec(d, TM, A) for _ in range(3) for d in DILS]
                  + [_row_spec(TM, GW), _row_spec(TM, GW)],
        out_shape=[jax.ShapeDtypeStruct((s, D), BF16)] + [_res_shape(s, d, A, BF16) for _ in range(3) for d in DILS]
                  + [jax.ShapeDtypeStruct((s, GW), F32)] * 2,
        scratch_shapes=[_col_scratch(TM, A)],
        compiler_params=_cparams("arbitrary"),
    )(x, g1, win_t)
    hn1 = res[0]
    q, k, v = (res[1 + t * nd:1 + (t + 1) * nd] for t in range(3))
    return hn1, q, k, v, res[-2], res[-1]


def _att_geometry(s, dil):
    length = s // dil
    rows = min(length, ATT_ROWS)
    return length, rows, length // rows, rows // CHUNK


def _stack_heads(t):
    lane = lax.broadcasted_iota(jnp.int32, t.shape, 1)
    zero = jnp.zeros_like(t)
    return jnp.concatenate([jnp.where(lane < DH, t, zero), jnp.where(lane >= DH, t, zero)], axis=0)


def _stack_cols(t):
    return jnp.concatenate([t[:, 0:1], t[:, DH:DH + 1]], axis=0)


def _unstack_heads(t2):
    n = t2.shape[0] // 2
    lane = lax.broadcasted_iota(jnp.int32, (n, LANES), 1)
    return jnp.where(lane < DH, t2[:n], t2[n:])


def _query_window_bias(s0, s1, dil, first):
    row = lax.broadcasted_iota(jnp.int32, (2 * CHUNK, 2 * CHUNK), 0)
    col = lax.broadcasted_iota(jnp.int32, (2 * CHUNK, 2 * CHUNK), 1)
    steps = (row & (CHUNK - 1)) + CHUNK - col
    valid = (steps >= 0) & (steps <= CHUNK)
    if first:
        valid = valid & (col >= CHUNK)
    slope = jnp.where(row < CHUNK, s0, s1)
    return jnp.where(valid, -slope * (steps * dil).astype(F32), NEG)


def _key_block_bias(s0, s1, dil, last):
    key = lax.broadcasted_iota(jnp.int32, (CHUNK, 4 * CHUNK), 0)
    col = lax.broadcasted_iota(jnp.int32, (CHUNK, 4 * CHUNK), 1)
    wq = col & (2 * CHUNK - 1)
    steps = wq - key
    valid = (steps >= 0) & (steps <= CHUNK)
    if last:
        valid = valid & (wq < CHUNK)
    slope = jnp.where(col < 2 * CHUNK, s0, s1)
    return jnp.where(valid, -slope * (steps * dil).astype(F32), NEG)


def _head_rows(t):
    row = lax.broadcasted_iota(jnp.int32, (8, LANES), 0)
    lane = lax.broadcasted_iota(jnp.int32, (8, LANES), 1)
    pick = jnp.where(((row == 0) & (lane == 0)) | ((row == 1) & (lane == DH)), 1.0, 0.0).astype(BF16)
    hi = t.astype(BF16)
    rest = t - hi.astype(F32)
    mid = rest.astype(BF16)
    low = (rest - mid.astype(F32)).astype(BF16)
    return _dot_nt(pick, hi) + _dot_nt(pick, mid) + _dot_nt(pick, low)


def _att_specs(dil, rows, nsub, nblk):
    main = pl.BlockSpec((None, rows, LANES), lambda r, hp, c: (r, c, hp))
    prev = pl.BlockSpec((None, CHUNK, LANES), lambda r, hp, c: (r, jnp.maximum(c * nsub - 1, 0), hp))
    nxt = pl.BlockSpec((None, CHUNK, LANES), lambda r, hp, c: (r, jnp.minimum((c + 1) * nsub, nblk - 1), hp))
    return main, prev, nxt


def _row_start(i):
    return i * CHUNK if isinstance(i, int) else pl.multiple_of(i * CHUNK, CHUNK)


def _attn_fwd(q, k, v, slopes, dil):
    length = q.shape[1]
    _, rows, nch, nsub = _att_geometry(length * dil, dil)
    main, prev, _ = _att_specs(dil, rows, nsub, length // CHUNK)

    def body(sl_ref, q_ref, k_ref, v_ref, kh_ref, vh_ref, o_ref, lse_ref, kbuf, vbuf, bias_buf):
        hp = pl.program_id(1)
        ch = pl.program_id(2)
        kbuf[0:CHUNK, :] = kh_ref[...]
        kbuf[CHUNK:, :] = k_ref[...]
        vbuf[0:CHUNK, :] = vh_ref[...]
        vbuf[CHUNK:, :] = v_ref[...]
        s0, s1 = sl_ref[2 * hp], sl_ref[2 * hp + 1]

        def block(i, bias):
            row = _row_start(i)
            rs = pl.ds(row, CHUNK)
            q2 = _stack_heads(q_ref[rs, :])
            kw = kbuf[pl.ds(row, 2 * CHUNK), :]
            vw = vbuf[pl.ds(row, 2 * CHUNK), :]
            sc = _dot_nt(q2, kw) + bias
            m = jnp.max(sc, axis=-1, keepdims=True)
            p = jnp.exp(sc - m)
            l = jnp.sum(p, axis=-1, keepdims=True)
            o2 = _dot(p.astype(BF16), vw) * (1.0 / l)
            o_ref[rs, :] = _unstack_heads(o2)
            lse_ref[rs, :] = _unstack_heads(jnp.broadcast_to(m + jnp.log(l), (2 * CHUNK, LANES)))

        bias_buf[...] = _query_window_bias(s0, s1, dil, False)

        @pl.when(ch == 0)
        def _():
            block(0, _query_window_bias(s0, s1, dil, True))

        @pl.when(ch != 0)
        def _():
            block(0, bias_buf[...])

        for i in range(1, nsub):
            block(i, bias_buf[...])

    sd = jax.ShapeDtypeStruct((dil, length, A), F32)
    return pl.pallas_call(
        body, name=f"attn_fwd_d{dil}", grid=(dil, NH // 2, nch),
        in_specs=[pl.BlockSpec(memory_space=pltpu.SMEM), main, main, main, prev, prev],
        out_specs=[main, main], out_shape=[sd, sd],
        scratch_shapes=[pltpu.VMEM((rows + CHUNK, LANES), BF16), pltpu.VMEM((rows + CHUNK, LANES), BF16),
                        pltpu.VMEM((2 * CHUNK, 2 * CHUNK), F32)],
        compiler_params=_cparams("arbitrary", "arbitrary", "arbitrary"),
    )(slopes, q, k, v, k, v)


def _attn_bwd_dq(q, k, v, do, lse, delta, slopes, dil):
    length = q.shape[1]
    _, rows, nch, nsub = _att_geometry(length * dil, dil)
    main, prev, _ = _att_specs(dil, rows, nsub, length // CHUNK)

    def body(sl_ref, q_ref, k_ref, v_ref, do_ref, lse_ref, dl_ref, kh_ref, vh_ref, dq_ref, kbuf, vbuf, bias_buf):
        hp = pl.program_id(1)
        ch = pl.program_id(2)
        kbuf[0:CHUNK, :] = kh_ref[...]
        kbuf[CHUNK:, :] = k_ref[...]
        vbuf[0:CHUNK, :] = vh_ref[...]
        vbuf[CHUNK:, :] = v_ref[...]
        s0, s1 = sl_ref[2 * hp], sl_ref[2 * hp + 1]

        def block(i, bias):
            row = _row_start(i)
            rs = pl.ds(row, CHUNK)
            q2 = _stack_heads(q_ref[rs, :])
            do2 = _stack_heads(do_ref[rs, :])
            lse2 = _stack_cols(lse_ref[rs, :])
            dl2 = _stack_cols(dl_ref[rs, :])
            kw = kbuf[pl.ds(row, 2 * CHUNK), :]
            vw = vbuf[pl.ds(row, 2 * CHUNK), :]
            p = jnp.exp(_dot_nt(q2, kw) + bias - lse2)
            ds = p * (_dot_nt(do2, vw) - dl2)
            dq_ref[rs, :] = _unstack_heads(_dot(ds.astype(BF16), kw))

        bias_buf[...] = _query_window_bias(s0, s1, dil, False)

        @pl.when(ch == 0)
        def _():
            block(0, _query_window_bias(s0, s1, dil, True))

        @pl.when(ch != 0)
        def _():
            block(0, bias_buf[...])

        for i in range(1, nsub):
            block(i, bias_buf[...])

    return pl.pallas_call(
        body, name=f"attn_dq_d{dil}", grid=(dil, NH // 2, nch),
        in_specs=[pl.BlockSpec(memory_space=pltpu.SMEM), main, main, main, main, main, main, prev, prev],
        out_specs=main, out_shape=jax.ShapeDtypeStruct((dil, length, A), F32),
        scratch_shapes=[pltpu.VMEM((rows + CHUNK, LANES), BF16), pltpu.VMEM((rows + CHUNK, LANES), BF16),
                        pltpu.VMEM((2 * CHUNK, 2 * CHUNK), F32)],
        compiler_params=_cparams("arbitrary", "arbitrary", "arbitrary"),
    )(slopes, q, k, v, do, lse, delta, k, v)


def _attn_bwd_dkv(q, k, v, do, lse, delta, slopes, dil):
    length = q.shape[1]
    _, rows, nch, nsub = _att_geometry(length * dil, dil)
    main, _, nxt = _att_specs(dil, rows, nsub, length // CHUNK)

    def body(sl_ref, k_ref, v_ref, q_ref, do_ref, lse_ref, dl_ref, qh_ref, doh_ref, lseh_ref, dlh_ref,
             dk_ref, dv_ref, qbuf, dobuf, lse_rows, dl_rows, bias_buf):
        hp = pl.program_id(1)
        ch = pl.program_id(2)
        for buf, main_ref, halo_ref in ((qbuf, q_ref, qh_ref), (dobuf, do_ref, doh_ref)):
            buf[0:rows, :] = main_ref[...]
            buf[rows:, :] = halo_ref[...]
        for buf, main_ref, halo_ref in ((lse_rows, lse_ref, lseh_ref), (dl_rows, dl_ref, dlh_ref)):
            buf[:, 0:rows] = _head_rows(main_ref[...])
            buf[:, rows:] = _head_rows(halo_ref[...])
        s0, s1 = sl_ref[2 * hp], sl_ref[2 * hp + 1]

        def block(i, bias):
            row = _row_start(i)
            rs = pl.ds(row, CHUNK)
            win = pl.ds(row, 2 * CHUNK)
            kc = k_ref[rs, :]
            vc = v_ref[rs, :]
            q2 = _stack_heads(qbuf[win, :])
            do2 = _stack_heads(dobuf[win, :])
            cols = slice(i * CHUNK, (i + 2) * CHUNK)
            lse2 = jnp.concatenate([lse_rows[0:1, cols], lse_rows[1:2, cols]], axis=1)
            dl2 = jnp.concatenate([dl_rows[0:1, cols], dl_rows[1:2, cols]], axis=1)
            pt = jnp.exp(_dot_nt(kc, q2) + bias - lse2)
            dst = pt * (_dot_nt(vc, do2) - dl2)
            dv_ref[rs, :] = _dot(pt.astype(BF16), do2)
            dk_ref[rs, :] = _dot(dst.astype(BF16), q2)

        bias_buf[...] = _key_block_bias(s0, s1, dil, False)

        for i in range(nsub - 1):
            block(i, bias_buf[...])

        @pl.when(ch != nch - 1)
        def _():
            block(nsub - 1, bias_buf[...])

        @pl.when(ch == nch - 1)
        def _():
            block(nsub - 1, _key_block_bias(s0, s1, dil, True))

    sd = jax.ShapeDtypeStruct((dil, length, A), F32)
    return pl.pallas_call(
        body, name=f"attn_dkv_d{dil}", grid=(dil, NH // 2, nch),
        in_specs=[pl.BlockSpec(memory_space=pltpu.SMEM), main, main, main, main, main, main, nxt, nxt, nxt, nxt],
        out_specs=[main, main], out_shape=[sd, sd],
        scratch_shapes=[pltpu.VMEM((rows + CHUNK, LANES), BF16), pltpu.VMEM((rows + CHUNK, LANES), BF16),
                        pltpu.VMEM((8, rows + CHUNK), F32), pltpu.VMEM((8, rows + CHUNK), F32),
                        pltpu.VMEM((CHUNK, 4 * CHUNK), F32)],
        compiler_params=_cparams("arbitrary", "arbitrary", "arbitrary"),
    )(slopes, k, v, q, do, lse, delta, q, do, lse, delta)


def _group_masks(width):
    lane = lax.broadcasted_iota(jnp.int32, (1, width), 1)
    return [(lane >= g * DH) & (lane < (g + 1) * DH) for g in range(width // DH)]


def _group_mean_matrix():
    i = lax.broadcasted_iota(jnp.int32, (GW, GW), 0) // DH
    j = lax.broadcasted_iota(jnp.int32, (GW, GW), 1) // DH
    return jnp.where(i == j, 1.0 / DH, 0.0).astype(F32)


def _tri_mask(lower):
    t = lax.broadcasted_iota(jnp.int32, (CHUNK, CHUNK), 0)
    u = lax.broadcasted_iota(jnp.int32, (CHUNK, CHUNK), 1)
    return (u <= t) if lower else (u >= t)


def _sgu_forward(u, z, lng, lnb, w_ref, bias_t, pmat, rows):
    ug = _gelu(u)
    zg = _gelu(z)
    mu = _dot_hi(zg, pmat)
    zc = zg - mu
    var = _dot_hi(zc * zc, pmat)
    rstd = lax.rsqrt(var + EPS)
    zhat = zc * rstd
    zn = (zhat * lng + lnb).astype(BF16)
    gm = _group_masks(GW)
    tri = _tri_mask(True)
    ws = [jnp.where(tri, w_ref[g], 0.0).astype(BF16) for g in range(NG)]
    pieces = []
    for c in range(rows // CHUNK):
        znc = zn[c * CHUNK:(c + 1) * CHUNK, :]
        mix = None
        for g in range(NG):
            part = jnp.where(gm[g], _dot(ws[g], znc), 0.0)
            mix = part if mix is None else mix + part
        pieces.append(mix + bias_t)
    mixed = jnp.concatenate(pieces, axis=0) if len(pieces) > 1 else pieces[0]
    return ug * mixed, ug, zhat, rstd, zn, mixed


def _mix_fwd(os_, ls_, u, z, x, lng, lnb, sgu_w, bias_t, ga, gg, wout):
    s = x.shape[0]
    nd = len(DILS)
    nscr = sum(1 for d in DILS if d > 1)

    def body(*refs):
        o_refs, l_refs = refs[:nd], refs[nd:2 * nd]
        u_ref, z_ref, x_ref, lng_ref, lnb_ref, w_ref, bt_ref, ga_ref, gg_ref, wo_ref = refs[2 * nd:2 * nd + 10]
        attn_ref = refs[2 * nd + 10]
        lse_refs = refs[2 * nd + 11:3 * nd + 11]
        mixed_ref, h1_ref = refs[3 * nd + 11:3 * nd + 13]
        scr = refs[3 * nd + 13:]
        scr_o, scr_l, scr_lse = scr[:nscr], scr[nscr:2 * nscr], scr[2 * nscr]
        ov, lv, j = [], [], 0
        for di, dil in enumerate(DILS):
            if dil == 1:
                ov.append(o_refs[di][0])
                lv.append(l_refs[di][0])
            else:
                ov.append(_merge_residues(o_refs[di], scr_o[j], dil))
                lv.append(_merge_residues(l_refs[di], scr_l[j], dil))
                j += 1
        mx = functools.reduce(jnp.maximum, lv)
        es = [jnp.exp(l - mx) for l in lv]
        den = functools.reduce(lambda a, b: a + b, es)
        attn = functools.reduce(lambda a, b: a + b, [e * o for e, o in zip(es, ov)]) / den
        attn_ref[...] = attn
        lse = mx + jnp.log(den)
        _fill_cols(scr_lse, lse)
        for di, dil in enumerate(DILS):
            if dil == 1:
                lse_refs[di][0] = lse
            else:
                _split_residues(scr_lse, lse_refs[di], dil)
        an, _, _ = _rms_fwd(attn, ga_ref[...])
        gmv, _, _, _, _, _ = _sgu_forward(u_ref[...], z_ref[...], lng_ref[...], lnb_ref[...], w_ref,
                                          bt_ref[...], _group_mean_matrix(), TMX)
        gn, _, _ = _rms_fwd(gmv, gg_ref[...])
        mixed = jnp.concatenate([an, gn], axis=-1).astype(BF16)
        mixed_ref[...] = mixed
        h1_ref[...] = x_ref[...] + _dot(mixed, wo_ref[...])

    sd = jax.ShapeDtypeStruct
    res = pl.pallas_call(
        body, name="mix_fwd", grid=(s // TMX,),
        in_specs=[_res_spec(d, TMX, A) for d in DILS] * 2 + [_row_spec(TMX, GW), _row_spec(TMX, GW),
                  _row_spec(TMX, D), _const_spec((1, GW)), _const_spec((1, GW)), _const_spec((NG, CHUNK, CHUNK)),
                  _const_spec((CHUNK, GW)), _const_spec((1, A)), _const_spec((1, GW)), _const_spec((D, D))],
        out_specs=[_row_spec(TMX, A)] + [_res_spec(d, TMX, A) for d in DILS] + [_row_spec(TMX, D), _row_spec(TMX, D)],
        out_shape=[sd((s, A), F32)] + [_res_shape(s, d, A, F32) for d in DILS] + [sd((s, D), BF16), sd((s, D), F32)],
        scratch_shapes=[_col_scratch(TMX, A)] * (2 * nscr + 1),
        compiler_params=_cparams("arbitrary"),
    )(*os_, *ls_, u, z, x, lng, lnb, sgu_w, bias_t, ga, gg, wout)
    return res[0], res[1:1 + nd], res[1 + nd], res[2 + nd]


def _mlp_fwd(h1, g2, wff1_t, wff2, gf, target):
    s = h1.shape[0]

    def body(h1_ref, g2_ref, w1_ref, w2_ref, gf_ref, t_ref, hn_ref, rf_ref, a_ref, dh2_ref, loss_ref, dgf_ref):
        i = pl.program_id(0)
        h1v = h1_ref[...]
        hn, _, _ = _rms_fwd(h1v, g2_ref[...])
        hn = hn.astype(BF16)
        hn_ref[...] = hn
        acc = h1v
        for j in range(DFF // FF_CH):
            cols = slice(j * FF_CH, (j + 1) * FF_CH)
            rf = jnp.maximum(_dot_nt(hn, w1_ref[cols, :]), 0.0)
            act = (rf * rf).astype(BF16)
            rf_ref[:, cols] = rf.astype(BF16)
            a_ref[:, cols] = act
            acc = acc + _dot(act, w2_ref[cols, :])
        y, h2n, r3 = _rms_fwd(acc, gf_ref[...])
        err = y - t_ref[...]
        part = 0.5 * jnp.sum(jnp.mean(err * err, axis=-1, keepdims=True), axis=0, keepdims=True)
        dy = err * (1.0 / D)
        dh2, dgf = _rms_bwd(dy, h2n, r3, gf_ref[...])
        dh2_ref[...] = dh2

        @pl.when(i == 0)
        def _():
            loss_ref[...] = jnp.zeros_like(loss_ref)
            dgf_ref[...] = jnp.zeros_like(dgf_ref)

        loss_ref[...] += jnp.broadcast_to(part, loss_ref.shape)
        dgf_ref[...] += dgf

    sd = jax.ShapeDtypeStruct
    return pl.pallas_call(
        body, name="mlp_fwd", grid=(s // TM,),
        in_specs=[_row_spec(TM, D), _const_spec((1, D)), _const_spec((DFF, D)), _const_spec((DFF, D)),
                  _const_spec((1, D)), _row_spec(TM, D)],
        out_specs=[_row_spec(TM, D), _row_spec(TM, DFF), _row_spec(TM, DFF), _row_spec(TM, D),
                   _const_spec((1, LANES)), _const_spec((1, D))],
        out_shape=[sd((s, D), BF16), sd((s, DFF), BF16), sd((s, DFF), BF16), sd((s, D), F32),
                   sd((1, LANES), F32), sd((1, D), F32)],
        compiler_params=_cparams("arbitrary"),
    )(h1, g2, wff1_t, wff2, gf, target)


def _mlp_bwd(dh2, rf, h1, g2, wff1_t, wff2):
    s = h1.shape[0]

    def body(dh2_ref, rf_ref, h1_ref, g2_ref, w1_ref, w2_ref, df_ref, dh1_ref, dg2_ref):
        i = pl.program_id(0)
        dh2v = dh2_ref[...]
        dh2b = dh2v.astype(BF16)
        dhn = jnp.zeros((TM, D), F32)
        for j in range(DFF // FF_CH):
            cols = slice(j * FF_CH, (j + 1) * FF_CH)
            da = _dot_nt(dh2b, w2_ref[cols, :])
            df = (da * (2.0 * rf_ref[:, cols].astype(F32))).astype(BF16)
            df_ref[:, cols] = df
            dhn = dhn + _dot(df, w1_ref[cols, :])
        _, h1n, r2 = _rms_fwd(h1_ref[...], g2_ref[...])
        dres, dg2 = _rms_bwd(dhn, h1n, r2, g2_ref[...])
        dh1_ref[...] = dh2v + dres

        @pl.when(i == 0)
        def _():
            dg2_ref[...] = jnp.zeros_like(dg2_ref)

        dg2_ref[...] += dg2

    sd = jax.ShapeDtypeStruct
    return pl.pallas_call(
        body, name="mlp_bwd", grid=(s // TM,),
        in_specs=[_row_spec(TM, D), _row_spec(TM, DFF), _row_spec(TM, D), _const_spec((1, D)),
                  _const_spec((DFF, D)), _const_spec((DFF, D))],
        out_specs=[_row_spec(TM, DFF), _row_spec(TM, D), _const_spec((1, D))],
        out_shape=[sd((s, DFF), BF16), sd((s, D), F32), sd((1, D), F32)],
        compiler_params=_cparams("arbitrary"),
    )(dh2, rf, h1, g2, wff1_t, wff2)


def _mix_bwd(dh1, attn, u, z, lng, lnb, sgu_w, sgu_wt, bias_t, ga, gg, wout):
    s = dh1.shape[0]
    nsteps = s // TMX
    nd = len(DILS)

    def body(*refs):
        dh1_ref, attn_ref, u_ref, z_ref, lng_ref, lnb_ref, w_ref, wt_ref, bt_ref, ga_ref, gg_ref, wo_ref = refs[:12]
        do_refs, dl_refs = refs[12:12 + nd], refs[12 + nd:12 + 2 * nd]
        (du_ref, dz_ref, dga_ref, dgg_ref, dlng_ref, dlnb_ref, dws_ref, db_ref,
         dbt_acc, scr_do, scr_dl) = refs[12 + 2 * nd:]
        i = pl.program_id(0)

        @pl.when(i == 0)
        def _():
            for r in (dga_ref, dgg_ref, dlng_ref, dlnb_ref, dws_ref, db_ref, dbt_acc):
                r[...] = jnp.zeros_like(r)

        dmixed = _dot_nt(dh1_ref[...].astype(BF16), wo_ref[...])
        attn = attn_ref[...]
        _, an, ra = _rms_fwd(attn, ga_ref[...])
        dattn, dga = _rms_bwd(dmixed[:, :A], an, ra, ga_ref[...])
        dga_ref[...] += dga
        _fill_cols(scr_do, dattn)
        prod = dattn * attn
        delta = jnp.zeros_like(prod)
        for hm in _group_masks(A):
            delta = delta + jnp.where(hm, jnp.sum(jnp.where(hm, prod, 0.0), axis=-1, keepdims=True), 0.0)
        _fill_cols(scr_dl, delta)
        for di, dil in enumerate(DILS):
            if dil == 1:
                do_refs[di][0] = dattn.astype(BF16)
                dl_refs[di][0] = delta
            else:
                _split_residues(scr_do, do_refs[di], dil)
                _split_residues(scr_dl, dl_refs[di], dil)
        pmat = _group_mean_matrix()
        lng = lng_ref[...]
        uv, zv = u_ref[...], z_ref[...]
        gmv, ug, zhat, rstd, zn, mixed = _sgu_forward(uv, zv, lng, lnb_ref[...], w_ref, bt_ref[...], pmat, TMX)
        _, gmn, rg = _rms_fwd(gmv, gg_ref[...])
        dgm, dgg = _rms_bwd(dmixed[:, A:], gmn, rg, gg_ref[...])
        dgg_ref[...] += dgg
        du_ref[...] = dgm * mixed * _gelu_grad(uv)
        dmx = dgm * ug
        dmxb = dmx.astype(BF16)
        gm = _group_masks(GW)
        tri_t = _tri_mask(False)
        wst = [jnp.where(tri_t, wt_ref[g], 0.0).astype(BF16) for g in range(NG)]
        zero = jnp.zeros((CHUNK, GW), BF16)
        dzn_pieces = []
        for c in range(TMX // CHUNK):
            rs = slice(c * CHUNK, (c + 1) * CHUNK)
            dmc = dmxb[rs, :]
            znc = zn[rs, :]
            dbt_acc[...] += dmx[rs, :]
            dzn = None
            for g in range(NG):
                dws_ref[g] += _dot_nt(jnp.where(gm[g], dmc, zero), znc)
                part = jnp.where(gm[g], _dot(wst[g], dmc), 0.0)
                dzn = part if dzn is None else dzn + part
            dzn_pieces.append(dzn)
        dzn = jnp.concatenate(dzn_pieces, axis=0)
        dlng_ref[...] += jnp.sum(dzn * zhat, axis=0, keepdims=True)
        dlnb_ref[...] += jnp.sum(dzn, axis=0, keepdims=True)
        dzh = dzn * lng
        dzg = rstd * (dzh - _dot_hi(dzh, pmat) - zhat * _dot_hi(dzh * zhat, pmat))
        dz_ref[...] = dzg * _gelu_grad(zv)

        @pl.when(i == nsteps - 1)
        def _():
            tri = _tri_mask(True)
            for g in range(NG):
                dws_ref[g] = jnp.where(tri, dws_ref[g], 0.0)
            acc = dbt_acc[...]
            lane = lax.broadcasted_iota(jnp.int32, (CHUNK, LANES), 1)
            out = jnp.zeros((CHUNK, LANES), F32)
            for g in range(NG):
                sg = jnp.sum(jnp.where(gm[g], acc, 0.0), axis=-1, keepdims=True)
                out = jnp.where(lane == g, sg, out)
            db_ref[...] = out

    sd = jax.ShapeDtypeStruct
    res = pl.pallas_call(
        body, name="mix_bwd", grid=(nsteps,),
        in_specs=[_row_spec(TMX, D), _row_spec(TMX, A), _row_spec(TMX, GW), _row_spec(TMX, GW),
                  _const_spec((1, GW)), _const_spec((1, GW)), _const_spec((NG, CHUNK, CHUNK)),
                  _const_spec((NG, CHUNK, CHUNK)), _const_spec((CHUNK, GW)), _const_spec((1, A)),
                  _const_spec((1, GW)), _const_spec((D, D))],
        out_specs=[_res_spec(d, TMX, A) for d in DILS] * 2 + [_row_spec(TMX, GW), _row_spec(TMX, GW),
                   _const_spec((1, A)), _const_spec((1, GW)), _const_spec((1, GW)), _const_spec((1, GW)),
                   _const_spec((NG, CHUNK, CHUNK)), _const_spec((CHUNK, LANES))],
        out_shape=[_res_shape(s, d, A, BF16) for d in DILS] + [_res_shape(s, d, A, F32) for d in DILS]
                  + [sd((s, GW), F32), sd((s, GW), F32),
                   sd((1, A), F32), sd((1, GW), F32), sd((1, GW), F32), sd((1, GW), F32),
                   sd((NG, CHUNK, CHUNK), F32), sd((CHUNK, LANES), F32)],
        scratch_shapes=[pltpu.VMEM((CHUNK, GW), F32), _col_scratch(TMX, A), _col_scratch(TMX, A)],
        compiler_params=_cparams("arbitrary"),
    )(dh1, attn, u, z, lng, lnb, sgu_w, sgu_wt, bias_t, ga, gg, wout)
    return (res[:nd], res[nd:2 * nd]) + tuple(res[2 * nd:])


def _dproj_merge(dqs, dks, dvs, du, dz, pin):
    s = du.shape[0]
    nd = len(DILS)
    nscr = sum(1 for d in DILS if d > 1)

    def body(*refs):
        pin_ref = refs[0]
        parts = [refs[1 + t * nd:1 + (t + 1) * nd] for t in range(3)]
        du_ref, dz_ref, dp_ref = refs[1 + 3 * nd:4 + 3 * nd]
        scr = refs[4 + 3 * nd:]
        sums = []
        for t in range(3):
            total, j = None, 0
            for di, dil in enumerate(DILS):
                if dil == 1:
                    term = parts[t][di][0]
                else:
                    term = _merge_residues(parts[t][di], scr[t * nscr + j], dil)
                    j += 1
                total = term if total is None else total + term
            sums.append(total)
        dp_ref[...] = jnp.concatenate([sums[0] * SCALE, sums[1], sums[2], du_ref[...] + pin_ref[0, 0], dz_ref[...]],
                                      axis=-1).astype(BF16)

    return pl.pallas_call(
        body, name="dproj_merge", grid=(s // TMX,),
        in_specs=[pl.BlockSpec(memory_space=pltpu.SMEM)] + [_res_spec(d, TMX, A) for d in DILS] * 3
                 + [_row_spec(TMX, GW)] * 2,
        out_specs=_row_spec(TMX, INW), out_shape=jax.ShapeDtypeStruct((s, INW), BF16),
        scratch_shapes=[_col_scratch(TMX, A)] * (3 * nscr),
        compiler_params=_cparams("arbitrary"),
    )(pin, *dqs, *dks, *dvs, du, dz)


def _inproj_bwd(dproj, dh1, x, g1, win_t):
    s = x.shape[0]

    def body(dp_ref, dh1_ref, x_ref, g_ref, w_ref, dx_ref, dg_ref):
        i = pl.program_id(0)
        dhn = _dot(dp_ref[...], w_ref[...])
        _, xn, r1 = _rms_fwd(x_ref[...], g_ref[...])
        dres, dg = _rms_bwd(dhn, xn, r1, g_ref[...])
        dx_ref[...] = dh1_ref[...] + dres

        @pl.when(i == 0)
        def _():
            dg_ref[...] = jnp.zeros_like(dg_ref)

        dg_ref[...] += dg

    sd = jax.ShapeDtypeStruct
    return pl.pallas_call(
        body, name="inproj_bwd", grid=(s // TM,),
        in_specs=[_row_spec(TM, INW), _row_spec(TM, D), _row_spec(TM, D), _const_spec((1, D)), _const_spec((INW, D))],
        out_specs=[_row_spec(TM, D), _const_spec((1, D))],
        out_shape=[sd((s, D), F32), sd((1, D), F32)],
        compiler_params=_cparams("arbitrary"),
    )(dproj, dh1, x, g1, win_t)


def _wgrad(a, b, name, bm, bn, bk=2 * TM):
    s, m = a.shape
    n = b.shape[1]
    bm, bn = min(bm, m), min(bn, n)

    def body(a_ref, b_ref, o_ref):
        @pl.when(pl.program_id(2) == 0)
        def _():
            o_ref[...] = jnp.zeros_like(o_ref)

        o_ref[...] += _dot_tn(a_ref[...].astype(BF16), b_ref[...].astype(BF16))

    return pl.pallas_call(
        body, name=name, grid=(m // bm, n // bn, s // bk),
        in_specs=[pl.BlockSpec((bk, bm), lambda i, j, k: (k, i)), pl.BlockSpec((bk, bn), lambda i, j, k: (k, j))],
        out_specs=pl.BlockSpec((bm, bn), lambda i, j, k: (i, j)),
        out_shape=jax.ShapeDtypeStruct((m, n), F32),
        compiler_params=_cparams("arbitrary", "arbitrary", "arbitrary"),
    )(a, b)


def _adamw_math(w, g, m, v):
    m = B1 * m + (1.0 - B1) * g
    v = B2 * v + (1.0 - B2) * (g * g)
    m_hat = m / (1.0 - B1 ** STEP)
    v_hat = v / (1.0 - B2 ** STEP)
    delta = -LR * (m_hat / (jnp.sqrt(v_hat) + AEPS) + WD * w)
    return delta, m, v


def _adamw(w, g, m, v, name):
    rows, cols = w.shape
    br = min(rows, 256)
    while rows % br:
        br -= 8

    def body(w_ref, g_ref, m_ref, v_ref, d_ref, mo_ref, vo_ref):
        d, mn, vn = _adamw_math(w_ref[...], g_ref[...], m_ref[...], v_ref[...])
        d_ref[...] = d
        mo_ref[...] = mn
        vo_ref[...] = vn

    spec = _row_spec(br, cols)
    sd = jax.ShapeDtypeStruct((rows, cols), F32)
    return pl.pallas_call(
        body, name=name, grid=(rows // br,), in_specs=[spec] * 4, out_specs=[spec] * 3,
        out_shape=[sd, sd, sd], compiler_params=_cparams("arbitrary"),
    )(w, g, m, v)


def _local_step(x, target, small, win_t, rest_weights, early_grads=None, after_attention_bwd=None,
                late_grads=None):
    slopes = jnp.asarray(_alibi_slopes(NH))
    hn1, q, k, v, u, z = _inproj_fwd(x, small["norm1_g"], win_t)
    outs, lses = [], []
    for i, dil in enumerate(DILS):
        o, l = _attn_fwd(q[i], k[i], v[i], slopes, dil)
        outs.append(o)
        lses.append(l)
    wout, wff1_t, wff2 = rest_weights(lses[-1])
    attn, lse, mixed, h1 = _mix_fwd(outs, lses, u, z, x, small["ln_g"], small["ln_b"], small["sgu_w"],
                                    small["bias_t"], small["attn_out_g"], small["gmlp_out_g"], wout)
    hn2, rf, act, dh2, loss, dgf = _mlp_fwd(h1, small["norm2_g"], wff1_t, wff2, small["final_norm_g"], target)
    df, dh1, dg2 = _mlp_bwd(dh2, rf, h1, small["norm2_g"], wff1_t, wff2)
    gwff1_t = _wgrad(df, hn2, "wgrad_ff1", 1024, D)
    gwff2 = _wgrad(act, dh2, "wgrad_ff2", 1024, D)
    gwout = _wgrad(mixed, dh1, "wgrad_out", D, D)
    ga, g1 = small["attn_out_g"], small["norm1_g"]
    pin = early_grads(gwff1_t, gwff2, gwout) if early_grads else None
    if pin is not None:
        ga = ga + pin
    (do, delta, du, dz, dga, dgg, dlng, dlnb, dws, db) = _mix_bwd(
        dh1, attn, u, z, small["ln_g"], small["ln_b"], small["sgu_w"], small["sgu_wt"], small["bias_t"],
        ga, small["gmlp_out_g"], wout)
    dqs, dks, dvs = [], [], []
    for i, dil in enumerate(DILS):
        dqs.append(_attn_bwd_dq(q[i], k[i], v[i], do[i], lse[i], delta[i], slopes, dil))
        dk, dv = _attn_bwd_dkv(q[i], k[i], v[i], do[i], lse[i], delta[i], slopes, dil)
        dks.append(dk)
        dvs.append(dv)
    marker = functools.reduce(lambda a, b: a + b, [t[0, 0:8, 0:LANES] for t in dqs + dks + dvs])
    pin = after_attention_bwd(marker) if after_attention_bwd else None
    dproj = _dproj_merge(dqs, dks, dvs, du, dz, jnp.zeros((1, 1), F32) if pin is None else pin)
    gwin_t = _wgrad(dproj, hn1, "wgrad_in", INW // 2, D)
    pin = late_grads(gwin_t) if late_grads else None
    if pin is not None:
        g1 = g1 + pin
    dx, dg1 = _inproj_bwd(dproj, dh1, x, g1, win_t)
    small_grads = dict(norm1_g=dg1, ln_g=dlng, ln_b=dlnb, sgu_w=dws, sgu_b=db[:, :NG].T,
                       attn_out_g=dga, gmlp_out_g=dgg, norm2_g=dg2, final_norm_g=dgf)
    return loss[0, 0], dx, small_grads, (gwin_t, gwout, gwff1_t, gwff2)


ANY = pl.BlockSpec(memory_space=pl.ANY)
NDEV = 8


def _position():
    return lax.axis_index("x"), lax.axis_index("y"), lax.axis_index("c")


def _other_chips(x, y):
    return [(1 - x, y), (x, 1 - y), (1 - x, 1 - y)]


def _remote(src, dst, send_sem, recv_sem, device):
    return pltpu.make_async_remote_copy(src_ref=src, dst_ref=dst, send_sem=send_sem, recv_sem=recv_sem,
                                        device_id=device, device_id_type=MESH)


HBM = pl.BlockSpec(memory_space=pltpu.HBM)
SEM = pl.BlockSpec(memory_space=pltpu.SEMAPHORE)
DATAFLOW = pltpu.SideEffectType.DATAFLOW_SIDE_EFFECTING


def _in_hbm(a):
    return pltpu.with_memory_space_constraint(a, pltpu.HBM)


def _gather_start(shard, name):
    def body(w_ref, land_ref, send_sems, recv_sems, w_thru, land_thru, token):
        x, y, c = _position()
        for k, (px, py) in enumerate(_other_chips(x, y)):
            _remote(w_ref, land_ref.at[2 * x + y], send_sems.at[k], recv_sems.at[k], (px, py, c)).start()
        token[...] = jnp.zeros_like(token)

    land = jnp.broadcast_to(shard[None], (NCHIP,) + shard.shape)
    return pl.pallas_call(
        body, name=name,
        out_shape=(pltpu.SemaphoreType.DMA((3,)), pltpu.SemaphoreType.DMA((3,)),
                   pltpu.HBM(shard.shape, shard.dtype), pltpu.HBM(land.shape, land.dtype),
                   jax.ShapeDtypeStruct((8, LANES), F32)),
        in_specs=(HBM, HBM), out_specs=(SEM, SEM, HBM, HBM, pl.BlockSpec(memory_space=pltpu.VMEM)),
        input_output_aliases={0: 2, 1: 3},
        compiler_params=pltpu.CompilerParams(has_side_effects=DATAFLOW),
    )(_in_hbm(shard), _in_hbm(land))


def _gather_wait(send_sems, recv_sems, w_thru, land_thru, after, name):
    def body(w_ref, land_ref, send_sems, recv_sems, after_ref, w_dead, got_ref):
        x, y, c = _position()
        for k, (px, py) in enumerate(_other_chips(x, y)):
            cp = _remote(w_ref, land_ref.at[2 * px + py], send_sems.at[k], recv_sems.at[k], (px, py, c))
            cp.wait_send()
            cp.wait_recv()

    return pl.pallas_call(
        body, name=name,
        out_shape=(pltpu.HBM(w_thru.shape, w_thru.dtype), pltpu.HBM(land_thru.shape, land_thru.dtype)),
        in_specs=(HBM, HBM, SEM, SEM, ANY), out_specs=(HBM, HBM),
        input_output_aliases={0: 0, 1: 1},
        compiler_params=pltpu.CompilerParams(has_side_effects=DATAFLOW),
    )(w_thru, land_thru, send_sems, recv_sems, after)[1]


def _xor_peers(x, y, c):
    peers = []
    for k in range(1, NDEV):
        kx, ky, kc = (k >> 2) & 1, (k >> 1) & 1, k & 1
        peers.append((1 - x if kx else x, 1 - y if ky else y, 1 - c if kc else c))
    return peers


def _piece(part_ref, px, py, pc):
    half = part_ref.shape[1] // 2
    return part_ref.at[2 * px + py, pl.ds(pc * half, half), :]


def _split_call(body, name, operands, n_sems, extra_out=()):
    n = len(operands)
    sems = tuple(pltpu.SemaphoreType.DMA((m,)) for m in n_sems)
    thru = tuple(pltpu.HBM(a.shape, a.dtype) for a in operands)
    return pl.pallas_call(
        body, name=name, out_shape=sems + thru + tuple(extra_out),
        in_specs=(HBM,) * n,
        out_specs=(SEM,) * len(sems) + (HBM,) * n + (pl.BlockSpec(memory_space=pltpu.VMEM),) * len(extra_out),
        input_output_aliases={i: len(sems) + i for i in range(n)},
        compiler_params=pltpu.CompilerParams(has_side_effects=DATAFLOW),
    )(*[_in_hbm(a) for a in operands])


TOKEN = jax.ShapeDtypeStruct((8, LANES), F32)


def _reduce_start(parts, name):
    nw = len(parts)
    lands = [lax.empty((NDEV - 1, p.shape[1] // 2, D), F32) for p in parts]

    def body(*refs):
        part_refs, land_refs = refs[:nw], refs[nw:2 * nw]
        send_sems, recv_sems = refs[2 * nw:2 * nw + 2]
        token = refs[-1]
        x, y, c = _position()
        for w in range(nw):
            for k, peer in enumerate(_xor_peers(x, y, c)):
                n = w * (NDEV - 1) + k
                _remote(_piece(part_refs[w], *peer), land_refs[w].at[k], send_sems.at[n], recv_sems.at[n],
                        peer).start()
        token[...] = jnp.zeros_like(token)

    n = nw * (NDEV - 1)
    res = _split_call(body, name, list(parts) + lands, (n, n), (TOKEN,))
    return res[0], res[1], res[2:2 + nw], res[2 + nw:2 + 2 * nw], res[-1]


def _reduce_wait(send_sems, recv_sems, parts, lands, after, name):
    nw = len(parts)

    def body(*refs):
        part_refs, land_refs = refs[:nw], refs[nw:2 * nw]
        send_sems, recv_sems = refs[2 * nw:2 * nw + 2]
        x, y, c = _position()
        for w in range(nw):
            for k, peer in enumerate(_xor_peers(x, y, c)):
                n = w * (NDEV - 1) + k
                cp = _remote(_piece(part_refs[w], *peer), land_refs[w].at[k], send_sems.at[n], recv_sems.at[n], peer)
                cp.wait_send()
                cp.wait_recv()

    operands = list(parts) + list(lands)
    res = pl.pallas_call(
        body, name=name, out_shape=tuple(pltpu.HBM(a.shape, a.dtype) for a in operands),
        in_specs=(HBM,) * (2 * nw) + (SEM, SEM, ANY), out_specs=(HBM,) * (2 * nw),
        input_output_aliases={i: i for i in range(2 * nw)},
        compiler_params=pltpu.CompilerParams(has_side_effects=DATAFLOW),
    )(*operands, send_sems, recv_sems, after)
    return res[:nw], res[nw:]


def _sum_pieces(part, land, sel, name):
    half = part.shape[1] // 2
    br = 128 if half % 128 == 0 else half // 2
    nb = half // br

    def body(sel_ref, own_ref, *refs):
        acc = own_ref[...]
        for r in refs[:NDEV - 1]:
            acc = acc + r[...]
        refs[NDEV - 1][...] = acc

    own_spec = pl.BlockSpec((None, br, D), lambda i, sel_ref: (sel_ref[0], sel_ref[1] * nb + i, 0))
    slot_specs = [pl.BlockSpec((None, br, D), functools.partial(lambda i, sel_ref, k: (k, i, 0), k=k))
                  for k in range(NDEV - 1)]
    return pl.pallas_call(
        body, name=name,
        grid_spec=pltpu.PrefetchScalarGridSpec(
            num_scalar_prefetch=1, grid=(nb,), in_specs=[own_spec] + slot_specs,
            out_specs=pl.BlockSpec((br, D), lambda i, sel_ref: (i, 0))),
        out_shape=jax.ShapeDtypeStruct((half, D), F32),
        compiler_params=_cparams("arbitrary"),
    )(sel, part, *([land] * (NDEV - 1)))


def _share_start(halves, name):
    nw = len(halves)
    lands = [lax.empty(h.shape, F32) for h in halves]

    def body(*refs):
        h_refs, land_refs = refs[:nw], refs[nw:2 * nw]
        send_sems, recv_sems = refs[2 * nw:2 * nw + 2]
        token = refs[-1]
        x, y, c = _position()
        for w in range(nw):
            _remote(h_refs[w], land_refs[w], send_sems.at[w], recv_sems.at[w], (x, y, 1 - c)).start()
        token[...] = jnp.zeros_like(token)

    res = _split_call(body, name, list(halves) + lands, (nw, nw), (TOKEN,))
    return res[0], res[1], res[2:2 + nw], res[2 + nw:2 + 2 * nw], res[-1]


def _share_wait(send_sems, recv_sems, halves, lands, after, name):
    nw = len(halves)

    def body(*refs):
        h_refs, land_refs = refs[:nw], refs[nw:2 * nw]
        send_sems, recv_sems = refs[2 * nw:2 * nw + 2]
        x, y, c = _position()
        for w in range(nw):
            cp = _remote(h_refs[w], land_refs[w], send_sems.at[w], recv_sems.at[w], (x, y, 1 - c))
            cp.wait_send()
            cp.wait_recv()

    operands = list(halves) + list(lands)
    res = pl.pallas_call(
        body, name=name, out_shape=tuple(pltpu.HBM(a.shape, a.dtype) for a in operands),
        in_specs=(HBM,) * (2 * nw) + (SEM, SEM, ANY), out_specs=(HBM,) * (2 * nw),
        input_output_aliases={i: i for i in range(2 * nw)},
        compiler_params=pltpu.CompilerParams(has_side_effects=DATAFLOW),
    )(*operands, send_sems, recv_sems, after)
    return res[:nw], res[nw:]


def _join_halves(own, other, c):
    first = jnp.where(c == 0, own, other)
    second = jnp.where(c == 0, other, own)
    return jnp.concatenate([first, second], axis=0)


SMALL_SIZES = (("norm1_g", D), ("sgu_ln_g", GW), ("sgu_ln_b", GW), ("sgu_w", NG * CHUNK * CHUNK),
               ("sgu_b", NG * CHUNK), ("attn_out_g", A), ("gmlp_out_g", GW), ("norm2_g", D),
               ("final_norm_g", D))
SMALL_ROWS = sum(n for _, n in SMALL_SIZES) // LANES


def _pack_small(tree):
    return jnp.concatenate([tree[n].reshape(-1) for n, _ in SMALL_SIZES]).reshape(SMALL_ROWS, LANES)


def _unpack_small(pack, shapes):
    flat = pack.reshape(-1)
    out, off = {}, 0
    for n, size in SMALL_SIZES:
        out[n] = flat[off:off + size].reshape(shapes[n])
        off += size
    return out


def _small_allreduce_adamw(gpack, wpack, mpack, vpack):
    def body(g_ref, w_ref, m_ref, v_ref, go_ref, d_ref, mo_ref, vo_ref, slots, send_sems, recv_sems):
        x, y, c = _position()
        me = 4 * x + 2 * y + c
        slots[me] = g_ref[...]
        peers = _xor_peers(x, y, c)
        sends = []
        for k, peer in enumerate(peers):
            cp = _remote(g_ref, slots.at[me], send_sems.at[k], recv_sems.at[k], peer)
            cp.start()
            sends.append(cp)
        for k, (px, py, pc) in enumerate(peers):
            _remote(g_ref, slots.at[4 * px + 2 * py + pc], send_sems.at[k], recv_sems.at[k],
                    (px, py, pc)).wait_recv()
        for cp in sends:
            cp.wait_send()
        total = slots[0]
        for k in range(1, NDEV):
            total = total + slots[k]
        go_ref[...] = total
        d, mn, vn = _adamw_math(w_ref[...], total, m_ref[...], v_ref[...])
        d_ref[...] = d
        mo_ref[...] = mn
        vo_ref[...] = vn

    sd = jax.ShapeDtypeStruct((SMALL_ROWS, LANES), F32)
    vm = pl.BlockSpec(memory_space=pltpu.VMEM)
    return pl.pallas_call(
        body, name="small_allreduce_adamw", in_specs=[vm] * 4, out_specs=[vm] * 4, out_shape=[sd] * 4,
        scratch_shapes=[pltpu.VMEM((NDEV, SMALL_ROWS, LANES), F32), pltpu.SemaphoreType.DMA((NDEV - 1,)),
                        pltpu.SemaphoreType.DMA((NDEV - 1,))],
        compiler_params=pltpu.CompilerParams(has_side_effects=True),
    )(gpack, wpack, mpack, vpack)


def kernel(x, norm1_g, w_in, sgu_ln_g, sgu_ln_b, sgu_w, sgu_b, attn_out_g, gmlp_out_g, w_out, norm2_g, w_ff1, w_ff2, final_norm_g, loss_target, m_norm1_g, m_w_in, m_sgu_ln_g, m_sgu_ln_b, m_sgu_w, m_sgu_b, m_attn_out_g, m_gmlp_out_g, m_w_out, m_norm2_g, m_w_ff1, m_w_ff2, m_final_norm_g, v_norm1_g, v_w_in, v_sgu_ln_g, v_sgu_ln_b, v_sgu_w, v_sgu_b, v_attn_out_g, v_gmlp_out_g, v_w_out, v_norm2_g, v_w_ff1, v_w_ff2, v_final_norm_g):
    names = [n for n, _ in SMALL_SIZES]
    w_small = dict(norm1_g=norm1_g, sgu_ln_g=sgu_ln_g, sgu_ln_b=sgu_ln_b, sgu_w=sgu_w, sgu_b=sgu_b,
                   attn_out_g=attn_out_g, gmlp_out_g=gmlp_out_g, norm2_g=norm2_g, final_norm_g=final_norm_g)
    m_small = dict(norm1_g=m_norm1_g, sgu_ln_g=m_sgu_ln_g, sgu_ln_b=m_sgu_ln_b, sgu_w=m_sgu_w, sgu_b=m_sgu_b,
                   attn_out_g=m_attn_out_g, gmlp_out_g=m_gmlp_out_g, norm2_g=m_norm2_g,
                   final_norm_g=m_final_norm_g)
    v_small = dict(norm1_g=v_norm1_g, sgu_ln_g=v_sgu_ln_g, sgu_ln_b=v_sgu_ln_b, sgu_w=v_sgu_w, sgu_b=v_sgu_b,
                   attn_out_g=v_attn_out_g, gmlp_out_g=v_gmlp_out_g, norm2_g=v_norm2_g,
                   final_norm_g=v_final_norm_g)
    shapes = {n: w_small[n].shape for n in names}

    r_in, r_out, r_ff = INW // NCHIP, D // NCHIP, DFF // NCHIP
    o1, o2, o3 = r_in, r_in + r_out, r_in + r_out + r_ff
    start_in = _gather_start(w_in[0].T.astype(BF16), "gather_in_start")
    start_rest = _gather_start(jnp.concatenate([w_out[0], w_ff1[0].T, w_ff2[0]], axis=0).astype(BF16),
                               "gather_rest_start")
    win_t = _gather_wait(*start_in[:4], after=start_rest[4], name="gather_in_wait").reshape(INW, D)

    def rest_weights(after):
        rest = _gather_wait(*start_rest[:4], after=after, name="gather_rest_wait")
        return (rest[:, :r_out].reshape(D, D), rest[:, r_out:r_out + r_ff].reshape(DFF, D),
                rest[:, r_out + r_ff:].reshape(DFF, D))

    small = dict(
        norm1_g=norm1_g, ln_g=sgu_ln_g.reshape(1, GW), ln_b=sgu_ln_b.reshape(1, GW), sgu_w=sgu_w[0],
        sgu_wt=jnp.swapaxes(sgu_w[0], 1, 2), bias_t=jnp.repeat(sgu_b[0].T, DH, axis=1),
        attn_out_g=attn_out_g, gmlp_out_g=gmlp_out_g, norm2_g=norm2_g, final_norm_g=final_norm_g.reshape(1, D))
    xi, yi, ci = _position()
    sel = jnp.stack([2 * xi + yi, ci]).astype(jnp.int32)
    state = {}

    def as_slabs(g):
        return g.reshape(NCHIP, g.shape[0] // NCHIP, D)

    def early_grads(gwff1_t, gwff2, gwout):
        state["early"] = _reduce_start([as_slabs(gwff1_t), as_slabs(gwff2), as_slabs(gwout)], "reduce_early_start")
        return state["early"][4][0:1, 0:1]

    def after_attention_bwd(marker):
        send_sems, recv_sems, parts, lands, _ = state["early"]
        parts, lands = _reduce_wait(send_sems, recv_sems, parts, lands, marker, "reduce_early_wait")
        halves = [_sum_pieces(p, l, sel, "sum_" + n) for p, l, n in zip(parts, lands, ("w_ff1", "w_ff2", "w_out"))]
        state["early_share"] = _share_start(halves, "share_early_start")
        return state["early_share"][4][0:1, 0:1]

    def late_grads(gwin_t):
        state["late"] = _reduce_start([as_slabs(gwin_t)], "reduce_late_start")
        return state["late"][4][0:1, 0:1]

    loss_part, dx, sg, _ = _local_step(
        x[0], loss_target[0], small, win_t, rest_weights, early_grads, after_attention_bwd, late_grads)
    loss = lax.psum(loss_part, ("x", "y", "c"))

    late = state["late"]
    send_sems, recv_sems, halves, lands, _ = state["early_share"]
    own, other = _share_wait(send_sems, recv_sems, halves, lands, dx, "share_early_wait")
    g_big = {n: _join_halves(o, t, ci) for n, o, t in zip(("w_ff1", "w_ff2", "w_out"), own, other)}
    g_big["w_ff1"] = g_big["w_ff1"].T
    w_big = dict(w_in=(w_in, m_w_in, v_w_in), w_out=(w_out, m_w_out, v_w_out),
                 w_ff1=(w_ff1, m_w_ff1, v_w_ff1), w_ff2=(w_ff2, m_w_ff2, v_w_ff2))
    grads, deltas, new_m, new_v = {}, {}, {}, {}

    def update(n):
        w, m, v = w_big[n]
        d, mn, vn = _adamw(w[0], g_big[n], m[0], v[0], "adamw_" + n)
        grads[n], deltas[n], new_m[n], new_v[n] = g_big[n][None], d[None], mn[None], vn[None]

    for n in ("w_ff1", "w_ff2", "w_out"):
        update(n)
    late_parts, late_lands = _reduce_wait(late[0], late[1], late[2], late[3], deltas["w_out"], "reduce_late_wait")
    late_share = _share_start([_sum_pieces(late_parts[0], late_lands[0], sel, "sum_w_in")], "share_late_start")

    g_small = dict(norm1_g=sg["norm1_g"], sgu_ln_g=sg["ln_g"], sgu_ln_b=sg["ln_b"], sgu_w=sg["sgu_w"],
                   sgu_b=sg["sgu_b"], attn_out_g=sg["attn_out_g"], gmlp_out_g=sg["gmlp_out_g"],
                   norm2_g=sg["norm2_g"], final_norm_g=sg["final_norm_g"])
    packs = _small_allreduce_adamw(_pack_small(g_small) + late_share[4][0:1, 0:1], _pack_small(w_small),
                                   _pack_small(m_small), _pack_small(v_small))
    for tree, pack in zip((grads, deltas, new_m, new_v), packs):
        tree.update(_unpack_small(pack, shapes))
    own, other = _share_wait(late_share[0], late_share[1], late_share[2], late_share[3], packs[0], "share_late_wait")
    g_big["w_in"] = _join_halves(own[0], other[0], ci).T
    update("w_in")

    order = ["norm1_g", "w_in", "sgu_ln_g", "sgu_ln_b", "sgu_w", "sgu_b", "attn_out_g", "gmlp_out_g", "w_out",
             "norm2_g", "w_ff1", "w_ff2", "final_norm_g"]
    return (loss, dx[None], *[grads[n] for n in order], *[deltas[n] for n in order],
            *[new_m[n] for n in order], *[new_v[n] for n in order])
```

```python
import functools
import math

import numpy as np
import jax
import jax.numpy as jnp
from jax import lax
from jax.experimental import pallas as pl
from jax.experimental.pallas import tpu as pltpu

F32 = jnp.float32
BF16 = jnp.bfloat16

D = 1024
NH = 12
DH = 64
A = NH * DH
NG = 4
GW = NG * DH
INW = 3 * A + 2 * GW
DFF = 4 * D
CHUNK = 128
PATTERNS = ((128, 1), (512, 4), (2048, 16))
EPS = 1e-6
SCALE = DH ** -0.5
NEG = -1e30

LR, B1, B2, AEPS, WD, STEP = 0.001, 0.9, 0.999, 1e-08, 0.01, 10

TM = 512
TMX = 256
ATT_ROWS = 1024
FF_CH = 1024
LANES = 128
NCHIP = 4
VMEM_LIMIT = 56 * 1024 * 1024
MESH = pl.DeviceIdType.MESH


def _cparams(*sem, **kw):
    return pltpu.CompilerParams(dimension_semantics=sem if sem else None,
                                vmem_limit_bytes=VMEM_LIMIT, **kw)


def _dot(a, b):
    return jnp.dot(a, b, preferred_element_type=F32)


def _dot_nt(a, b):
    return lax.dot_general(a, b, (((1,), (1,)), ((), ())), preferred_element_type=F32)


def _dot_tn(a, b):
    return lax.dot_general(a, b, (((0,), (0,)), ((), ())), preferred_element_type=F32)


def _dot_hi(a, b):
    return jnp.dot(a, b, preferred_element_type=F32, precision=lax.Precision.HIGHEST)


def _alibi_slopes(n):
    def pow2(m):
        start = 2.0 ** (-8.0 / m)
        return [start ** (i + 1) for i in range(m)]
    if math.log2(n).is_integer():
        s = pow2(n)
    else:
        c = 2 ** int(math.floor(math.log2(n)))
        s = pow2(c) + pow2(2 * c)[0::2][: n - c]
    return np.asarray(s, dtype=np.float32)


def _rms_fwd(v, g):
    r = lax.rsqrt(jnp.mean(v * v, axis=-1, keepdims=True) + EPS)
    vn = v * r
    return vn * g, vn, r


def _rms_bwd(dy, vn, r, g):
    w = dy * g
    dv = r * (w - vn * jnp.mean(w * vn, axis=-1, keepdims=True))
    return dv, jnp.sum(dy * vn, axis=0, keepdims=True)


_K0 = math.sqrt(2.0 / math.pi)
_K1 = 0.044715


def _gelu(v):
    return 0.5 * v * (1.0 + jnp.tanh(_K0 * (v + _K1 * (v * v * v))))


def _gelu_grad(v):
    t = jnp.tanh(_K0 * (v + _K1 * (v * v * v)))
    return 0.5 * (1.0 + t) + 0.5 * v * (1.0 - t * t) * (_K0 * (1.0 + 3.0 * _K1 * v * v))


def _row_spec(rows, cols):
    return pl.BlockSpec((rows, cols), lambda i: (i, 0))


def _const_spec(shape):
    nd = len(shape)
    return pl.BlockSpec(shape, lambda i: (0,) * nd, pipeline_mode=pl.Buffered(1))


DILS = tuple(d for _, d in PATTERNS)


def _fill_cols(scr, value):
    for cb in range(value.shape[1] // LANES):
        scr[cb] = value[:, cb * LANES:(cb + 1) * LANES]


def _split_residues(scr, out_ref, dil):
    nb, rows, _ = scr.shape
    for r in range(dil):
        for cb in range(nb):
            piece = scr.at[cb][pl.ds(r, rows // dil, stride=dil), :]
            out_ref[r, :, cb * LANES:(cb + 1) * LANES] = piece.astype(out_ref.dtype)


def _merge_residues(in_ref, scr, dil):
    nb, rows, _ = scr.shape
    for r in range(dil):
        for cb in range(nb):
            scr.at[cb][pl.ds(r, rows // dil, stride=dil), :] = in_ref[r, :, cb * LANES:(cb + 1) * LANES].astype(F32)
    return jnp.concatenate([scr[cb] for cb in range(nb)], axis=-1)


def _col_scratch(rows, width):
    return pltpu.VMEM((width // LANES, rows, LANES), F32)


def _res_spec(dil, rows, width):
    return pl.BlockSpec((dil, rows // dil, width), lambda i: (0, i, 0))


def _res_shape(s, dil, width, dtype):
    return jax.ShapeDtypeStruct((dil, s // dil, width), dtype)


def _inproj_fwd(x, g1, win_t):
    s = x.shape[0]
    nd = len(DILS)

    def body(x_ref, g_ref, w_ref, hn_ref, *rest):
        qkv_refs = rest[:3 * nd]
        u_ref, z_ref, scr = rest[3 * nd:]
        hn, _, _ = _rms_fwd(x_ref[...], g_ref[...])
        hn = hn.astype(BF16)
        hn_ref[...] = hn
        for t in range(3):
            seg = _dot_nt(hn, w_ref[t * A:(t + 1) * A, :])
            seg = seg * SCALE if t == 0 else seg
            _fill_cols(scr, seg)
            for di, dil in enumerate(DILS):
                if dil == 1:
                    qkv_refs[t * nd + di][0] = seg.astype(BF16)
                else:
                    _split_residues(scr, qkv_refs[t * nd + di], dil)
        u_ref[...] = _dot_nt(hn, w_ref[3 * A:3 * A + GW, :])
        z_ref[...] = _dot_nt(hn, w_ref[3 * A + GW:INW, :])

    res = pl.pallas_call(
        body, name="inproj_fwd", grid=(s // TM,),
        in_specs=[_row_spec(TM, D), _const_spec((1, D)), _const_spec((INW, D))],
        out_specs=[_row_spec(TM, D)] + [_res_spec(d, TM, A) for _ in range(3) for d in DILS]
                  + [_row_spec(TM, GW), _row_spec(TM, GW)],
        out_shape=[jax.ShapeDtypeStruct((s, D), BF16)] + [_res_shape(s, d, A, BF16) for _ in range(3) for d in DILS]
                  + [jax.ShapeDtypeStruct((s, GW), F32)] * 2,
        scratch_shapes=[_col_scratch(TM, A)],
        compiler_params=_cparams("arbitrary"),
    )(x, g1, win_t)
    hn1 = res[0]
    q, k, v = (res[1 + t * nd:1 + (t + 1) * nd] for t in range(3))
    return hn1, q, k, v, res[-2], res[-1]


def _att_geometry(s, dil):
    length = s // dil
    rows = min(length, ATT_ROWS)
    return length, rows, length // rows, rows // CHUNK


def _stack_heads(t):
    lane = lax.broadcasted_iota(jnp.int32, t.shape, 1)
    zero = jnp.zeros_like(t)
    return jnp.concatenate([jnp.where(lane < DH, t, zero), jnp.where(lane >= DH, t, zero)], axis=0)


def _stack_cols(t):
    return jnp.concatenate([t[:, 0:1], t[:, DH:DH + 1]], axis=0)


def _unstack_heads(t2):
    n = t2.shape[0] // 2
    lane = lax.broadcasted_iota(jnp.int32, (n, LANES), 1)
    return jnp.where(lane < DH, t2[:n], t2[n:])


def _query_window_bias(s0, s1, dil, first):
    row = lax.broadcasted_iota(jnp.int32, (2 * CHUNK, 2 * CHUNK), 0)
    col = lax.broadcasted_iota(jnp.int32, (2 * CHUNK, 2 * CHUNK), 1)
    steps = (row & (CHUNK - 1)) + CHUNK - col
    valid = (steps >= 0) & (steps <= CHUNK)
    if first:
        valid = valid & (col >= CHUNK)
    slope = jnp.where(row < CHUNK, s0, s1)
    return jnp.where(valid, -slope * (steps * dil).astype(F32), NEG)


def _key_block_bias(s0, s1, dil, last):
    key = lax.broadcasted_iota(jnp.int32, (CHUNK, 4 * CHUNK), 0)
    col = lax.broadcasted_iota(jnp.int32, (CHUNK, 4 * CHUNK), 1)
    wq = col & (2 * CHUNK - 1)
    steps = wq - key
    valid = (steps >= 0) & (steps <= CHUNK)
    if last:
        valid = valid & (wq < CHUNK)
    slope = jnp.where(col < 2 * CHUNK, s0, s1)
    return jnp.where(valid, -slope * (steps * dil).astype(F32), NEG)


def _head_rows(t):
    row = lax.broadcasted_iota(jnp.int32, (8, LANES), 0)
    lane = lax.broadcasted_iota(jnp.int32, (8, LANES), 1)
    pick = jnp.where(((row == 0) & (lane == 0)) | ((row == 1) & (lane == DH)), 1.0, 0.0).astype(BF16)
    hi = t.astype(BF16)
    rest = t - hi.astype(F32)
    mid = rest.astype(BF16)
    low = (rest - mid.astype(F32)).astype(BF16)
    return _dot_nt(pick, hi) + _dot_nt(pick, mid) + _dot_nt(pick, low)


def _att_specs(dil, rows, nsub, nblk):
    main = pl.BlockSpec((None, rows, LANES), lambda r, hp, c: (r, c, hp))
    prev = pl.BlockSpec((None, CHUNK, LANES), lambda r, hp, c: (r, jnp.maximum(c * nsub - 1, 0), hp))
    nxt = pl.BlockSpec((None, CHUNK, LANES), lambda r, hp, c: (r, jnp.minimum((c + 1) * nsub, nblk - 1), hp))
    return main, prev, nxt


def _row_start(i):
    return i * CHUNK if isinstance(i, int) else pl.multiple_of(i * CHUNK, CHUNK)


def _attn_fwd(q, k, v, slopes, dil):
    length = q.shape[1]
    _, rows, nch, nsub = _att_geometry(length * dil, dil)
    main, prev, _ = _att_specs(dil, rows, nsub, length // CHUNK)

    def body(sl_ref, q_ref, k_ref, v_ref, kh_ref, vh_ref, o_ref, lse_ref, kbuf, vbuf, bias_buf):
        hp = pl.program_id(1)
        ch = pl.program_id(2)
        kbuf[0:CHUNK, :] = kh_ref[...]
        kbuf[CHUNK:, :] = k_ref[...]
        vbuf[0:CHUNK, :] = vh_ref[...]
        vbuf[CHUNK:, :] = v_ref[...]
        s0, s1 = sl_ref[2 * hp], sl_ref[2 * hp + 1]

        def block(i, bias):
            row = _row_start(i)
            rs = pl.ds(row, CHUNK)
            q2 = _stack_heads(q_ref[rs, :])
            kw = kbuf[pl.ds(row, 2 * CHUNK), :]
            vw = vbuf[pl.ds(row, 2 * CHUNK), :]
            sc = _dot_nt(q2, kw) + bias
            m = jnp.max(sc, axis=-1, keepdims=True)
            p = jnp.exp(sc - m)
            l = jnp.sum(p, axis=-1, keepdims=True)
            o2 = _dot(p.astype(BF16), vw) * (1.0 / l)
            o_ref[rs, :] = _unstack_heads(o2).astype(BF16)
            lse_ref[rs, :] = _unstack_heads(jnp.broadcast_to(m + jnp.log(l), (2 * CHUNK, LANES)))

        bias_buf[...] = _query_window_bias(s0, s1, dil, False)

        @pl.when(ch == 0)
        def _():
            block(0, _query_window_bias(s0, s1, dil, True))

        @pl.when(ch != 0)
        def _():
            block(0, bias_buf[...])

        for i in range(1, nsub):
            block(i, bias_buf[...])

    sd = jax.ShapeDtypeStruct
    return pl.pallas_call(
        body, name=f"attn_fwd_d{dil}", grid=(dil, NH // 2, nch),
        in_specs=[pl.BlockSpec(memory_space=pltpu.SMEM), main, main, main, prev, prev],
        out_specs=[main, main], out_shape=[sd((dil, length, A), BF16), sd((dil, length, A), F32)],
        scratch_shapes=[pltpu.VMEM((rows + CHUNK, LANES), BF16), pltpu.VMEM((rows + CHUNK, LANES), BF16),
                        pltpu.VMEM((2 * CHUNK, 2 * CHUNK), F32)],
        compiler_params=_cparams("arbitrary", "arbitrary", "arbitrary"),
    )(slopes, q, k, v, k, v)


def _attn_bwd_dq(q, k, v, do, lse, delta, slopes, dil):
    length = q.shape[1]
    _, rows, nch, nsub = _att_geometry(length * dil, dil)
    main, prev, _ = _att_specs(dil, rows, nsub, length // CHUNK)

    def body(sl_ref, q_ref, k_ref, v_ref, do_ref, lse_ref, dl_ref, kh_ref, vh_ref, dq_ref, kbuf, vbuf, bias_buf):
        hp = pl.program_id(1)
        ch = pl.program_id(2)
        kbuf[0:CHUNK, :] = kh_ref[...]
        kbuf[CHUNK:, :] = k_ref[...]
        vbuf[0:CHUNK, :] = vh_ref[...]
        vbuf[CHUNK:, :] = v_ref[...]
        s0, s1 = sl_ref[2 * hp], sl_ref[2 * hp + 1]

        def block(i, bias):
            row = _row_start(i)
            rs = pl.ds(row, CHUNK)
            q2 = _stack_heads(q_ref[rs, :])
            do2 = _stack_heads(do_ref[rs, :])
            lse2 = _stack_cols(lse_ref[rs, :])
            dl2 = _stack_cols(dl_ref[rs, :])
            kw = kbuf[pl.ds(row, 2 * CHUNK), :]
            vw = vbuf[pl.ds(row, 2 * CHUNK), :]
            p = jnp.exp(_dot_nt(q2, kw) + bias - lse2)
            ds = p * (_dot_nt(do2, vw) - dl2)
            dq_ref[rs, :] = _unstack_heads(_dot(ds.astype(BF16), kw)).astype(BF16)

        bias_buf[...] = _query_window_bias(s0, s1, dil, False)

        @pl.when(ch == 0)
        def _():
            block(0, _query_window_bias(s0, s1, dil, True))

        @pl.when(ch != 0)
        def _():
            block(0, bias_buf[...])

        for i in range(1, nsub):
            block(i, bias_buf[...])

    return pl.pallas_call(
        body, name=f"attn_dq_d{dil}", grid=(dil, NH // 2, nch),
        in_specs=[pl.BlockSpec(memory_space=pltpu.SMEM), main, main, main, main, main, main, prev, prev],
        out_specs=main, out_shape=jax.ShapeDtypeStruct((dil, length, A), BF16),
        scratch_shapes=[pltpu.VMEM((rows + CHUNK, LANES), BF16), pltpu.VMEM((rows + CHUNK, LANES), BF16),
                        pltpu.VMEM((2 * CHUNK, 2 * CHUNK), F32)],
        compiler_params=_cparams("arbitrary", "arbitrary", "arbitrary"),
    )(slopes, q, k, v, do, lse, delta, k, v)


def _attn_bwd_dkv(q, k, v, do, lse, delta, slopes, dil):
    length = q.shape[1]
    _, rows, nch, nsub = _att_geometry(length * dil, dil)
    main, _, nxt = _att_specs(dil, rows, nsub, length // CHUNK)

    def body(sl_ref, k_ref, v_ref, q_ref, do_ref, lse_ref, dl_ref, qh_ref, doh_ref, lseh_ref, dlh_ref,
             dk_ref, dv_ref, qbuf, dobuf, lse_rows, dl_rows, bias_buf):
        hp = pl.program_id(1)
        ch = pl.program_id(2)
        for buf, main_ref, halo_ref in ((qbuf, q_ref, qh_ref), (dobuf, do_ref, doh_ref)):
            buf[0:rows, :] = main_ref[...]
            buf[rows:, :] = halo_ref[...]
        for buf, main_ref, halo_ref in ((lse_rows, lse_ref, lseh_ref), (dl_rows, dl_ref, dlh_ref)):
            buf[:, 0:rows] = _head_rows(main_ref[...])
            buf[:, rows:] = _head_rows(halo_ref[...])
        s0, s1 = sl_ref[2 * hp], sl_ref[2 * hp + 1]

        def block(i, bias):
            row = _row_start(i)
            rs = pl.ds(row, CHUNK)
            win = pl.ds(row, 2 * CHUNK)
            kc = k_ref[rs, :]
            vc = v_ref[rs, :]
            q2 = _stack_heads(qbuf[win, :])
            do2 = _stack_heads(dobuf[win, :])
            cols = slice(i * CHUNK, (i + 2) * CHUNK)
            lse2 = jnp.concatenate([lse_rows[0:1, cols], lse_rows[1:2, cols]], axis=1)
            dl2 = jnp.concatenate([dl_rows[0:1, cols], dl_rows[1:2, cols]], axis=1)
            pt = jnp.exp(_dot_nt(kc, q2) + bias - lse2)
            dst = pt * (_dot_nt(vc, do2) - dl2)
            dv_ref[rs, :] = _dot(pt.astype(BF16), do2).astype(BF16)
            dk_ref[rs, :] = _dot(dst.astype(BF16), q2).astype(BF16)

        bias_buf[...] = _key_block_bias(s0, s1, dil, False)

        for i in range(nsub - 1):
            block(i, bias_buf[...])

        @pl.when(ch != nch - 1)
        def _():
            block(nsub - 1, bias_buf[...])

        @pl.when(ch == nch - 1)
        def _():
            block(nsub - 1, _key_block_bias(s0, s1, dil, True))

    sd = jax.ShapeDtypeStruct((dil, length, A), BF16)
    return pl.pallas_call(
        body, name=f"attn_dkv_d{dil}", grid=(dil, NH // 2, nch),
        in_specs=[pl.BlockSpec(memory_space=pltpu.SMEM), main, main, main, main, main, main, nxt, nxt, nxt, nxt],
        out_specs=[main, main], out_shape=[sd, sd],
        scratch_shapes=[pltpu.VMEM((rows + CHUNK, LANES), BF16), pltpu.VMEM((rows + CHUNK, LANES), BF16),
                        pltpu.VMEM((8, rows + CHUNK), F32), pltpu.VMEM((8, rows + CHUNK), F32),
                        pltpu.VMEM((CHUNK, 4 * CHUNK), F32)],
        compiler_params=_cparams("arbitrary", "arbitrary", "arbitrary"),
    )(slopes, k, v, q, do, lse, delta, q, do, lse, delta)


def _group_masks(width):
    lane = lax.broadcasted_iota(jnp.int32, (1, width), 1)
    return [(lane >= g * DH) & (lane < (g + 1) * DH) for g in range(width // DH)]


def _group_mean_matrix():
    i = lax.broadcasted_iota(jnp.int32, (GW, GW), 0) // DH
    j = lax.broadcasted_iota(jnp.int32, (GW, GW), 1) // DH
    return jnp.where(i == j, 1.0 / DH, 0.0).astype(F32)


def _tri_mask(lower):
    t = lax.broadcasted_iota(jnp.int32, (CHUNK, CHUNK), 0)
    u = lax.broadcasted_iota(jnp.int32, (CHUNK, CHUNK), 1)
    return (u <= t) if lower else (u >= t)


def _sgu_forward(u, z, lng, lnb, w_ref, bias_t, pmat, rows):
    ug = _gelu(u)
    zg = _gelu(z)
    mu = _dot_hi(zg, pmat)
    zc = zg - mu
    var = _dot_hi(zc * zc, pmat)
    rstd = lax.rsqrt(var + EPS)
    zhat = zc * rstd
    zn = (zhat * lng + lnb).astype(BF16)
    gm = _group_masks(GW)
    tri = _tri_mask(True)
    ws = [jnp.where(tri, w_ref[g], 0.0).astype(BF16) for g in range(NG)]
    pieces = []
    for c in range(rows // CHUNK):
        znc = zn[c * CHUNK:(c + 1) * CHUNK, :]
        mix = None
        for g in range(NG):
            part = jnp.where(gm[g], _dot(ws[g], znc), 0.0)
            mix = part if mix is None else mix + part
        pieces.append(mix + bias_t)
    mixed = jnp.concatenate(pieces, axis=0) if len(pieces) > 1 else pieces[0]
    return ug * mixed, ug, zhat, rstd, zn, mixed


def _mix_fwd(os_, ls_, u, z, x, lng, lnb, sgu_w, bias_t, ga, gg, wout):
    s = x.shape[0]
    nd = len(DILS)
    nscr = sum(1 for d in DILS if d > 1)

    def body(*refs):
        o_refs, l_refs = refs[:nd], refs[nd:2 * nd]
        u_ref, z_ref, x_ref, lng_ref, lnb_ref, w_ref, bt_ref, ga_ref, gg_ref, wo_ref = refs[2 * nd:2 * nd + 10]
        attn_ref = refs[2 * nd + 10]
        lse_refs = refs[2 * nd + 11:3 * nd + 11]
        mixed_ref, h1_ref = refs[3 * nd + 11:3 * nd + 13]
        scr = refs[3 * nd + 13:]
        scr_o, scr_l, scr_lse = scr[:nscr], scr[nscr:2 * nscr], scr[2 * nscr]
        ov, lv, j = [], [], 0
        for di, dil in enumerate(DILS):
            if dil == 1:
                ov.append(o_refs[di][0].astype(F32))
                lv.append(l_refs[di][0])
            else:
                ov.append(_merge_residues(o_refs[di], scr_o[j], dil))
                lv.append(_merge_residues(l_refs[di], scr_l[j], dil))
                j += 1
        mx = functools.reduce(jnp.maximum, lv)
        es = [jnp.exp(l - mx) for l in lv]
        den = functools.reduce(lambda a, b: a + b, es)
        attn = functools.reduce(lambda a, b: a + b, [e * o for e, o in zip(es, ov)]) / den
        attn_ref[...] = attn
        lse = mx + jnp.log(den)
        _fill_cols(scr_lse, lse)
        for di, dil in enumerate(DILS):
            if dil == 1:
                lse_refs[di][0] = lse
            else:
                _split_residues(scr_lse, lse_refs[di], dil)
        an, _, _ = _rms_fwd(attn, ga_ref[...])
        gmv, _, _, _, _, _ = _sgu_forward(u_ref[...], z_ref[...], lng_ref[...], lnb_ref[...], w_ref,
                                          bt_ref[...], _group_mean_matrix(), TMX)
        gn, _, _ = _rms_fwd(gmv, gg_ref[...])
        mixed = jnp.concatenate([an, gn], axis=-1).astype(BF16)
        mixed_ref[...] = mixed
        h1_ref[...] = x_ref[...] + _dot(mixed, wo_ref[...])

    sd = jax.ShapeDtypeStruct
    res = pl.pallas_call(
        body, name="mix_fwd", grid=(s // TMX,),
        in_specs=[_res_spec(d, TMX, A) for d in DILS] * 2 + [_row_spec(TMX, GW), _row_spec(TMX, GW),
                  _row_spec(TMX, D), _const_spec((1, GW)), _const_spec((1, GW)), _const_spec((NG, CHUNK, CHUNK)),
                  _const_spec((CHUNK, GW)), _const_spec((1, A)), _const_spec((1, GW)), _const_spec((D, D))],
        out_specs=[_row_spec(TMX, A)] + [_res_spec(d, TMX, A) for d in DILS] + [_row_spec(TMX, D), _row_spec(TMX, D)],
        out_shape=[sd((s, A), F32)] + [_res_shape(s, d, A, F32) for d in DILS] + [sd((s, D), BF16), sd((s, D), F32)],
        scratch_shapes=[_col_scratch(TMX, A)] * (2 * nscr + 1),
        compiler_params=_cparams("arbitrary"),
    )(*os_, *ls_, u, z, x, lng, lnb, sgu_w, bias_t, ga, gg, wout)
    return res[0], res[1:1 + nd], res[1 + nd], res[2 + nd]


def _mlp_fwd(h1, g2, wff1_t, wff2, gf, target):
    s = h1.shape[0]

    def body(h1_ref, g2_ref, w1_ref, w2_ref, gf_ref, t_ref, hn_ref, rf_ref, a_ref, dh2_ref, loss_ref, dgf_ref):
        i = pl.program_id(0)
        h1v = h1_ref[...]
        hn, _, _ = _rms_fwd(h1v, g2_ref[...])
        hn = hn.astype(BF16)
        hn_ref[...] = hn
        acc = h1v
        for j in range(DFF // FF_CH):
            cols = slice(j * FF_CH, (j + 1) * FF_CH)
            rf = jnp.maximum(_dot_nt(hn, w1_ref[cols, :]), 0.0)
            act = (rf * rf).astype(BF16)
            rf_ref[:, cols] = rf.astype(BF16)
            a_ref[:, cols] = act
            acc = acc + _dot(act, w2_ref[cols, :])
        y, h2n, r3 = _rms_fwd(acc, gf_ref[...])
        err = y - t_ref[...]
        part = 0.5 * jnp.sum(jnp.mean(err * err, axis=-1, keepdims=True), axis=0, keepdims=True)
        dy = err * (1.0 / D)
        dh2, dgf = _rms_bwd(dy, h2n, r3, gf_ref[...])
        dh2_ref[...] = dh2

        @pl.when(i == 0)
        def _():
            loss_ref[...] = jnp.zeros_like(loss_ref)
            dgf_ref[...] = jnp.zeros_like(dgf_ref)

        loss_ref[...] += jnp.broadcast_to(part, loss_ref.shape)
        dgf_ref[...] += dgf

    sd = jax.ShapeDtypeStruct
    return pl.pallas_call(
        body, name="mlp_fwd", grid=(s // TM,),
        in_specs=[_row_spec(TM, D), _const_spec((1, D)), _const_spec((DFF, D)), _const_spec((DFF, D)),
                  _const_spec((1, D)), _row_spec(TM, D)],
        out_specs=[_row_spec(TM, D), _row_spec(TM, DFF), _row_spec(TM, DFF), _row_spec(TM, D),
                   _const_spec((1, LANES)), _const_spec((1, D))],
        out_shape=[sd((s, D), BF16), sd((s, DFF), BF16), sd((s, DFF), BF16), sd((s, D), F32),
                   sd((1, LANES), F32), sd((1, D), F32)],
        compiler_params=_cparams("arbitrary"),
    )(h1, g2, wff1_t, wff2, gf, target)


def _mlp_bwd(dh2, rf, h1, g2, wff1_t, wff2):
    s = h1.shape[0]

    def body(dh2_ref, rf_ref, h1_ref, g2_ref, w1_ref, w2_ref, df_ref, dh1_ref, dg2_ref):
        i = pl.program_id(0)
        dh2v = dh2_ref[...]
        dh2b = dh2v.astype(BF16)
        dhn = jnp.zeros((TM, D), F32)
        for j in range(DFF // FF_CH):
            cols = slice(j * FF_CH, (j + 1) * FF_CH)
            da = _dot_nt(dh2b, w2_ref[cols, :])
            df = (da * (2.0 * rf_ref[:, cols].astype(F32))).astype(BF16)
            df_ref[:, cols] = df
            dhn = dhn + _dot(df, w1_ref[cols, :])
        _, h1n, r2 = _rms_fwd(h1_ref[...], g2_ref[...])
        dres, dg2 = _rms_bwd(dhn, h1n, r2, g2_ref[...])
        dh1_ref[...] = dh2v + dres

        @pl.when(i == 0)
        def _():
            dg2_ref[...] = jnp.zeros_like(dg2_ref)

        dg2_ref[...] += dg2

    sd = jax.ShapeDtypeStruct
    return pl.pallas_call(
        body, name="mlp_bwd", grid=(s // TM,),
        in_specs=[_row_spec(TM, D), _row_spec(TM, DFF), _row_spec(TM, D), _const_spec((1, D)),
                  _const_spec((DFF, D)), _const_spec((DFF, D))],
        out_specs=[_row_spec(TM, DFF), _row_spec(TM, D), _const_spec((1, D))],
        out_shape=[sd((s, DFF), BF16), sd((s, D), F32), sd((1, D), F32)],
        compiler_params=_cparams("arbitrary"),
    )(dh2, rf, h1, g2, wff1_t, wff2)


def _mix_bwd(dh1, attn, u, z, lng, lnb, sgu_w, sgu_wt, bias_t, ga, gg, wout):
    s = dh1.shape[0]
    nsteps = s // TMX
    nd = len(DILS)

    def body(*refs):
        dh1_ref, attn_ref, u_ref, z_ref, lng_ref, lnb_ref, w_ref, wt_ref, bt_ref, ga_ref, gg_ref, wo_ref = refs[:12]
        do_refs, dl_refs = refs[12:12 + nd], refs[12 + nd:12 + 2 * nd]
        (du_ref, dz_ref, dga_ref, dgg_ref, dlng_ref, dlnb_ref, dws_ref, db_ref,
         dbt_acc, scr_do, scr_dl) = refs[12 + 2 * nd:]
        i = pl.program_id(0)

        @pl.when(i == 0)
        def _():
            for r in (dga_ref, dgg_ref, dlng_ref, dlnb_ref, dws_ref, db_ref, dbt_acc):
                r[...] = jnp.zeros_like(r)

        dmixed = _dot_nt(dh1_ref[...].astype(BF16), wo_ref[...])
        attn = attn_ref[...]
        _, an, ra = _rms_fwd(attn, ga_ref[...])
        dattn, dga = _rms_bwd(dmixed[:, :A], an, ra, ga_ref[...])
        dga_ref[...] += dga
        _fill_cols(scr_do, dattn)
        prod = dattn * attn
        delta = jnp.zeros_like(prod)
        for hm in _group_masks(A):
            delta = delta + jnp.where(hm, jnp.sum(jnp.where(hm, prod, 0.0), axis=-1, keepdims=True), 0.0)
        _fill_cols(scr_dl, delta)
        for di, dil in enumerate(DILS):
            if dil == 1:
                do_refs[di][0] = dattn.astype(BF16)
                dl_refs[di][0] = delta
            else:
                _split_residues(scr_do, do_refs[di], dil)
                _split_residues(scr_dl, dl_refs[di], dil)
        pmat = _group_mean_matrix()
        lng = lng_ref[...]
        uv, zv = u_ref[...], z_ref[...]
        gmv, ug, zhat, rstd, zn, mixed = _sgu_forward(uv, zv, lng, lnb_ref[...], w_ref, bt_ref[...], pmat, TMX)
        _, gmn, rg = _rms_fwd(gmv, gg_ref[...])
        dgm, dgg = _rms_bwd(dmixed[:, A:], gmn, rg, gg_ref[...])
        dgg_ref[...] += dgg
        du_ref[...] = dgm * mixed * _gelu_grad(uv)
        dmx = dgm * ug
        dmxb = dmx.astype(BF16)
        gm = _group_masks(GW)
        tri_t = _tri_mask(False)
        wst = [jnp.where(tri_t, wt_ref[g], 0.0).astype(BF16) for g in range(NG)]
        zero = jnp.zeros((CHUNK, GW), BF16)
        dzn_pieces = []
        for c in range(TMX // CHUNK):
            rs = slice(c * CHUNK, (c + 1) * CHUNK)
            dmc = dmxb[rs, :]
            znc = zn[rs, :]
            dbt_acc[...] += dmx[rs, :]
            dzn = None
            for g in range(NG):
                dws_ref[g] += _dot_nt(jnp.where(gm[g], dmc, zero), znc)
                part = jnp.where(gm[g], _dot(wst[g], dmc), 0.0)
                dzn = part if dzn is None else dzn + part
            dzn_pieces.append(dzn)
        dzn = jnp.concatenate(dzn_pieces, axis=0)
        dlng_ref[...] += jnp.sum(dzn * zhat, axis=0, keepdims=True)
        dlnb_ref[...] += jnp.sum(dzn, axis=0, keepdims=True)
        dzh = dzn * lng
        dzg = rstd * (dzh - _dot_hi(dzh, pmat) - zhat * _dot_hi(dzh * zhat, pmat))
        dz_ref[...] = dzg * _gelu_grad(zv)

        @pl.when(i == nsteps - 1)
        def _():
            tri = _tri_mask(True)
            for g in range(NG):
                dws_ref[g] = jnp.where(tri, dws_ref[g], 0.0)
            acc = dbt_acc[...]
            lane = lax.broadcasted_iota(jnp.int32, (CHUNK, LANES), 1)
            out = jnp.zeros((CHUNK, LANES), F32)
            for g in range(NG):
                sg = jnp.sum(jnp.where(gm[g], acc, 0.0), axis=-1, keepdims=True)
                out = jnp.where(lane == g, sg, out)
            db_ref[...] = out

    sd = jax.ShapeDtypeStruct
    res = pl.pallas_call(
        body, name="mix_bwd", grid=(nsteps,),
        in_specs=[_row_spec(TMX, D), _row_spec(TMX, A), _row_spec(TMX, GW), _row_spec(TMX, GW),
                  _const_spec((1, GW)), _const_spec((1, GW)), _const_spec((NG, CHUNK, CHUNK)),
                  _const_spec((NG, CHUNK, CHUNK)), _const_spec((CHUNK, GW)), _const_spec((1, A)),
                  _const_spec((1, GW)), _const_spec((D, D))],
        out_specs=[_res_spec(d, TMX, A) for d in DILS] * 2 + [_row_spec(TMX, GW), _row_spec(TMX, GW),
                   _const_spec((1, A)), _const_spec((1, GW)), _const_spec((1, GW)), _const_spec((1, GW)),
                   _const_spec((NG, CHUNK, CHUNK)), _const_spec((CHUNK, LANES))],
        out_shape=[_res_shape(s, d, A, BF16) for d in DILS] + [_res_shape(s, d, A, F32) for d in DILS]
                  + [sd((s, GW), F32), sd((s, GW), F32),
                   sd((1, A), F32), sd((1, GW), F32), sd((1, GW), F32), sd((1, GW), F32),
                   sd((NG, CHUNK, CHUNK), F32), sd((CHUNK, LANES), F32)],
        scratch_shapes=[pltpu.VMEM((CHUNK, GW), F32), _col_scratch(TMX, A), _col_scratch(TMX, A)],
        compiler_params=_cparams("arbitrary"),
    )(dh1, attn, u, z, lng, lnb, sgu_w, sgu_wt, bias_t, ga, gg, wout)
    return (res[:nd], res[nd:2 * nd]) + tuple(res[2 * nd:])


def _dproj_merge(dqs, dks, dvs, du, dz, pin):
    s = du.shape[0]
    nd = len(DILS)
    nscr = sum(1 for d in DILS if d > 1)

    def body(*refs):
        pin_ref = refs[0]
        parts = [refs[1 + t * nd:1 + (t + 1) * nd] for t in range(3)]
        du_ref, dz_ref, dp_ref = refs[1 + 3 * nd:4 + 3 * nd]
        scr = refs[4 + 3 * nd:]
        sums = []
        for t in range(3):
            total, j = None, 0
            for di, dil in enumerate(DILS):
                if dil == 1:
                    term = parts[t][di][0].astype(F32)
                else:
                    term = _merge_residues(parts[t][di], scr[t * nscr + j], dil)
                    j += 1
                total = term if total is None else total + term
            sums.append(total)
        dp_ref[...] = jnp.concatenate([sums[0] * SCALE, sums[1], sums[2], du_ref[...] + pin_ref[0, 0], dz_ref[...]],
                                      axis=-1).astype(BF16)

    return pl.pallas_call(
        body, name="dproj_merge", grid=(s // TMX,),
        in_specs=[pl.BlockSpec(memory_space=pltpu.SMEM)] + [_res_spec(d, TMX, A) for d in DILS] * 3
                 + [_row_spec(TMX, GW)] * 2,
        out_specs=_row_spec(TMX, INW), out_shape=jax.ShapeDtypeStruct((s, INW), BF16),
        scratch_shapes=[_col_scratch(TMX, A)] * (3 * nscr),
        compiler_params=_cparams("arbitrary"),
    )(pin, *dqs, *dks, *dvs, du, dz)


def _inproj_bwd(dproj, dh1, x, g1, win_t):
    s = x.shape[0]

    def body(dp_ref, dh1_ref, x_ref, g_ref, w_ref, dx_ref, dg_ref):
        i = pl.program_id(0)
        dhn = _dot(dp_ref[...], w_ref[...])
        _, xn, r1 = _rms_fwd(x_ref[...], g_ref[...])
        dres, dg = _rms_bwd(dhn, xn, r1, g_ref[...])
        dx_ref[...] = dh1_ref[...] + dres

        @pl.when(i == 0)
        def _():
            dg_ref[...] = jnp.zeros_like(dg_ref)

        dg_ref[...] += dg

    sd = jax.ShapeDtypeStruct
    return pl.pallas_call(
        body, name="inproj_bwd", grid=(s // TM,),
        in_specs=[_row_spec(TM, INW), _row_spec(TM, D), _row_spec(TM, D), _const_spec((1, D)), _const_spec((INW, D))],
        out_specs=[_row_spec(TM, D), _const_spec((1, D))],
        out_shape=[sd((s, D), F32), sd((1, D), F32)],
        compiler_params=_cparams("arbitrary"),
    )(dproj, dh1, x, g1, win_t)


def _wgrad(a, b, name, bm, bn, bk=2 * TM):
    s, m = a.shape
    n = b.shape[1]
    bm, bn = min(bm, m), min(bn, n)

    def body(a_ref, b_ref, o_ref):
        @pl.when(pl.program_id(2) == 0)
        def _():
            o_ref[...] = jnp.zeros_like(o_ref)

        o_ref[...] += _dot_tn(a_ref[...].astype(BF16), b_ref[...].astype(BF16))

    return pl.pallas_call(
        body, name=name, grid=(m // bm, n // bn, s // bk),
        in_specs=[pl.BlockSpec((bk, bm), lambda i, j, k: (k, i)), pl.BlockSpec((bk, bn), lambda i, j, k: (k, j))],
        out_specs=pl.BlockSpec((bm, bn), lambda i, j, k: (i, j)),
        out_shape=jax.ShapeDtypeStruct((m, n), F32),
        compiler_params=_cparams("arbitrary", "arbitrary", "arbitrary"),
    )(a, b)


def _adamw_math(w, g, m, v):
    m = B1 * m + (1.0 - B1) * g
    v = B2 * v + (1.0 - B2) * (g * g)
    m_hat = m / (1.0 - B1 ** STEP)
    v_hat = v / (1.0 - B2 ** STEP)
    delta = -LR * (m_hat / (jnp.sqrt(v_hat) + AEPS) + WD * w)
    return delta, m, v


def _adamw(w, g, m, v, name):
    rows, cols = w.shape
    br = min(rows, 256)
    while rows % br:
        br -= 8

    def body(w_ref, g_ref, m_ref, v_ref, d_ref, mo_ref, vo_ref):
        d, mn, vn = _adamw_math(w_ref[...], g_ref[...], m_ref[...], v_ref[...])
        d_ref[...] = d
        mo_ref[...] = mn
        vo_ref[...] = vn

    spec = _row_spec(br, cols)
    sd = jax.ShapeDtypeStruct((rows, cols), F32)
    return pl.pallas_call(
        body, name=name, grid=(rows // br,), in_specs=[spec] * 4, out_specs=[spec] * 3,
        out_shape=[sd, sd, sd], compiler_params=_cparams("arbitrary"),
    )(w, g, m, v)


def _local_step(x, target, small, win_t, rest_weights, early_grads=None, after_attention_bwd=None,
                late_grads=None):
    slopes = jnp.asarray(_alibi_slopes(NH))
    hn1, q, k, v, u, z = _inproj_fwd(x, small["norm1_g"], win_t)
    outs, lses = [], []
    for i, dil in enumerate(DILS):
        o, l = _attn_fwd(q[i], k[i], v[i], slopes, dil)
        outs.append(o)
        lses.append(l)
    wout, wff1_t, wff2 = rest_weights(lses[-1])
    attn, lse, mixed, h1 = _mix_fwd(outs, lses, u, z, x, small["ln_g"], small["ln_b"], small["sgu_w"],
                                    small["bias_t"], small["attn_out_g"], small["gmlp_out_g"], wout)
    hn2, rf, act, dh2, loss, dgf = _mlp_fwd(h1, small["norm2_g"], wff1_t, wff2, small["final_norm_g"], target)
    df, dh1, dg2 = _mlp_bwd(dh2, rf, h1, small["norm2_g"], wff1_t, wff2)
    gwff1_t = _wgrad(df, hn2, "wgrad_ff1", 1024, D)
    gwff2 = _wgrad(act, dh2, "wgrad_ff2", 1024, D)
    gwout = _wgrad(mixed, dh1, "wgrad_out", D, D)
    ga, g1 = small["attn_out_g"], small["norm1_g"]
    pin = early_grads(gwff1_t, gwff2, gwout) if early_grads else None
    if pin is not None:
        ga = ga + pin
    (do, delta, du, dz, dga, dgg, dlng, dlnb, dws, db) = _mix_bwd(
        dh1, attn, u, z, small["ln_g"], small["ln_b"], small["sgu_w"], small["sgu_wt"], small["bias_t"],
        ga, small["gmlp_out_g"], wout)
    dqs, dks, dvs = [], [], []
    for i, dil in enumerate(DILS):
        dqs.append(_attn_bwd_dq(q[i], k[i], v[i], do[i], lse[i], delta[i], slopes, dil))
        dk, dv = _attn_bwd_dkv(q[i], k[i], v[i], do[i], lse[i], delta[i], slopes, dil)
        dks.append(dk)
        dvs.append(dv)
    marker = functools.reduce(lambda a, b: a + b, [t[0, 0:8, 0:LANES] for t in dqs + dks + dvs])
    pin = after_attention_bwd(marker) if after_attention_bwd else None
    dproj = _dproj_merge(dqs, dks, dvs, du, dz, jnp.zeros((1, 1), F32) if pin is None else pin)
    gwin_t = _wgrad(dproj, hn1, "wgrad_in", INW // 2, D)
    pin = late_grads(gwin_t) if late_grads else None
    if pin is not None:
        g1 = g1 + pin
    dx, dg1 = _inproj_bwd(dproj, dh1, x, g1, win_t)
    small_grads = dict(norm1_g=dg1, ln_g=dlng, ln_b=dlnb, sgu_w=dws, sgu_b=db[:, :NG].T,
                       attn_out_g=dga, gmlp_out_g=dgg, norm2_g=dg2, final_norm_g=dgf)
    return loss[0, 0], dx, small_grads, (gwin_t, gwout, gwff1_t, gwff2)


ANY = pl.BlockSpec(memory_space=pl.ANY)
NDEV = 8


def _position():
    return lax.axis_index("x"), lax.axis_index("y"), lax.axis_index("c")


def _other_chips(x, y):
    return [(1 - x, y), (x, 1 - y), (1 - x, 1 - y)]


def _remote(src, dst, send_sem, recv_sem, device):
    return pltpu.make_async_remote_copy(src_ref=src, dst_ref=dst, send_sem=send_sem, recv_sem=recv_sem,
                                        device_id=device, device_id_type=MESH)


HBM = pl.BlockSpec(memory_space=pltpu.HBM)
SEM = pl.BlockSpec(memory_space=pltpu.SEMAPHORE)
DATAFLOW = pltpu.SideEffectType.DATAFLOW_SIDE_EFFECTING


def _in_hbm(a):
    return pltpu.with_memory_space_constraint(a, pltpu.HBM)


def _gather_start(shard, name):
    def body(w_ref, land_ref, send_sems, recv_sems, w_thru, land_thru, token):
        x, y, c = _position()
        for k, (px, py) in enumerate(_other_chips(x, y)):
            _remote(w_ref, land_ref.at[2 * x + y], send_sems.at[k], recv_sems.at[k], (px, py, c)).start()
        token[...] = jnp.zeros_like(token)

    land = jnp.broadcast_to(shard[None], (NCHIP,) + shard.shape)
    return pl.pallas_call(
        body, name=name,
        out_shape=(pltpu.SemaphoreType.DMA((3,)), pltpu.SemaphoreType.DMA((3,)),
                   pltpu.HBM(shard.shape, shard.dtype), pltpu.HBM(land.shape, land.dtype),
                   jax.ShapeDtypeStruct((8, LANES), F32)),
        in_specs=(HBM, HBM), out_specs=(SEM, SEM, HBM, HBM, pl.BlockSpec(memory_space=pltpu.VMEM)),
        input_output_aliases={0: 2, 1: 3},
        compiler_params=pltpu.CompilerParams(has_side_effects=DATAFLOW),
    )(_in_hbm(shard), _in_hbm(land))


def _gather_wait(send_sems, recv_sems, w_thru, land_thru, after, name):
    def body(w_ref, land_ref, send_sems, recv_sems, after_ref, w_dead, got_ref):
        x, y, c = _position()
        for k, (px, py) in enumerate(_other_chips(x, y)):
            cp = _remote(w_ref, land_ref.at[2 * px + py], send_sems.at[k], recv_sems.at[k], (px, py, c))
            cp.wait_send()
            cp.wait_recv()

    return pl.pallas_call(
        body, name=name,
        out_shape=(pltpu.HBM(w_thru.shape, w_thru.dtype), pltpu.HBM(land_thru.shape, land_thru.dtype)),
        in_specs=(HBM, HBM, SEM, SEM, ANY), out_specs=(HBM, HBM),
        input_output_aliases={0: 0, 1: 1},
        compiler_params=pltpu.CompilerParams(has_side_effects=DATAFLOW),
    )(w_thru, land_thru, send_sems, recv_sems, after)[1]


def _xor_peers(x, y, c):
    peers = []
    for k in range(1, NDEV):
        kx, ky, kc = (k >> 2) & 1, (k >> 1) & 1, k & 1
        peers.append((1 - x if kx else x, 1 - y if ky else y, 1 - c if kc else c))
    return peers


def _piece(part_ref, px, py, pc):
    half = part_ref.shape[1] // 2
    return part_ref.at[2 * px + py, pl.ds(pc * half, half), :]


def _split_call(body, name, operands, n_sems, extra_out=()):
    n = len(operands)
    sems = tuple(pltpu.SemaphoreType.DMA((m,)) for m in n_sems)
    thru = tuple(pltpu.HBM(a.shape, a.dtype) for a in operands)
    return pl.pallas_call(
        body, name=name, out_shape=sems + thru + tuple(extra_out),
        in_specs=(HBM,) * n,
        out_specs=(SEM,) * len(sems) + (HBM,) * n + (pl.BlockSpec(memory_space=pltpu.VMEM),) * len(extra_out),
        input_output_aliases={i: len(sems) + i for i in range(n)},
        compiler_params=pltpu.CompilerParams(has_side_effects=DATAFLOW),
    )(*[_in_hbm(a) for a in operands])


TOKEN = jax.ShapeDtypeStruct((8, LANES), F32)


def _reduce_start(parts, name):
    nw = len(parts)
    lands = [lax.empty((NDEV - 1, p.shape[1] // 2, D), F32) for p in parts]

    def body(*refs):
        part_refs, land_refs = refs[:nw], refs[nw:2 * nw]
        send_sems, recv_sems = refs[2 * nw:2 * nw + 2]
        token = refs[-1]
        x, y, c = _position()
        for w in range(nw):
            for k, peer in enumerate(_xor_peers(x, y, c)):
                n = w * (NDEV - 1) + k
                _remote(_piece(part_refs[w], *peer), land_refs[w].at[k], send_sems.at[n], recv_sems.at[n],
                        peer).start()
        token[...] = jnp.zeros_like(token)

    n = nw * (NDEV - 1)
    res = _split_call(body, name, list(parts) + lands, (n, n), (TOKEN,))
    return res[0], res[1], res[2:2 + nw], res[2 + nw:2 + 2 * nw], res[-1]


def _reduce_wait(send_sems, recv_sems, parts, lands, after, name):
    nw = len(parts)

    def body(*refs):
        part_refs, land_refs = refs[:nw], refs[nw:2 * nw]
        send_sems, recv_sems = refs[2 * nw:2 * nw + 2]
        x, y, c = _position()
        for w in range(nw):
            for k, peer in enumerate(_xor_peers(x, y, c)):
                n = w * (NDEV - 1) + k
                cp = _remote(_piece(part_refs[w], *peer), land_refs[w].at[k], send_sems.at[n], recv_sems.at[n], peer)
                cp.wait_send()
                cp.wait_recv()

    operands = list(parts) + list(lands)
    res = pl.pallas_call(
        body, name=name, out_shape=tuple(pltpu.HBM(a.shape, a.dtype) for a in operands),
        in_specs=(HBM,) * (2 * nw) + (SEM, SEM, ANY), out_specs=(HBM,) * (2 * nw),
        input_output_aliases={i: i for i in range(2 * nw)},
        compiler_params=pltpu.CompilerParams(has_side_effects=DATAFLOW),
    )(*operands, send_sems, recv_sems, after)
    return res[:nw], res[nw:]


def _sum_pieces(part, land, sel, name):
    half = part.shape[1] // 2
    br = 128 if half % 128 == 0 else half // 2
    nb = half // br

    def body(sel_ref, own_ref, *refs):
        acc = own_ref[...]
        for r in refs[:NDEV - 1]:
            acc = acc + r[...]
        refs[NDEV - 1][...] = acc

    own_spec = pl.BlockSpec((None, br, D), lambda i, sel_ref: (sel_ref[0], sel_ref[1] * nb + i, 0))
    slot_specs = [pl.BlockSpec((None, br, D), functools.partial(lambda i, sel_ref, k: (k, i, 0), k=k))
                  for k in range(NDEV - 1)]
    return pl.pallas_call(
        body, name=name,
        grid_spec=pltpu.PrefetchScalarGridSpec(
            num_scalar_prefetch=1, grid=(nb,), in_specs=[own_spec] + slot_specs,
            out_specs=pl.BlockSpec((br, D), lambda i, sel_ref: (i, 0))),
        out_shape=jax.ShapeDtypeStruct((half, D), F32),
        compiler_params=_cparams("arbitrary"),
    )(sel, part, *([land] * (NDEV - 1)))


def _share_start(halves, name):
    nw = len(halves)
    lands = [lax.empty(h.shape, F32) for h in halves]

    def body(*refs):
        h_refs, land_refs = refs[:nw], refs[nw:2 * nw]
        send_sems, recv_sems = refs[2 * nw:2 * nw + 2]
        token = refs[-1]
        x, y, c = _position()
        for w in range(nw):
            _remote(h_refs[w], land_refs[w], send_sems.at[w], recv_sems.at[w], (x, y, 1 - c)).start()
        token[...] = jnp.zeros_like(token)

    res = _split_call(body, name, list(halves) + lands, (nw, nw), (TOKEN,))
    return res[0], res[1], res[2:2 + nw], res[2 + nw:2 + 2 * nw], res[-1]


def _share_wait(send_sems, recv_sems, halves, lands, after, name):
    nw = len(halves)

    def body(*refs):
        h_refs, land_refs = refs[:nw], refs[nw:2 * nw]
        send_sems, recv_sems = refs[2 * nw:2 * nw + 2]
        x, y, c = _position()
        for w in range(nw):
            cp = _remote(h_refs[w], land_refs[w], send_sems.at[w], recv_sems.at[w], (x, y, 1 - c))
            cp.wait_send()
            cp.wait_recv()

    operands = list(halves) + list(lands)
    res = pl.pallas_call(
        body, name=name, out_shape=tuple(pltpu.HBM(a.shape, a.dtype) for a in operands),
        in_specs=(HBM,) * (2 * nw) + (SEM, SEM, ANY), out_specs=(HBM,) * (2 * nw),
        input_output_aliases={i: i for i in range(2 * nw)},
        compiler_params=pltpu.CompilerParams(has_side_effects=DATAFLOW),
    )(*operands, send_sems, recv_sems, after)
    return res[:nw], res[nw:]


def _join_halves(own, other, c):
    first = jnp.where(c == 0, own, other)
    second = jnp.where(c == 0, other, own)
    return jnp.concatenate([first, second], axis=0)


SMALL_SIZES = (("norm1_g", D), ("sgu_ln_g", GW), ("sgu_ln_b", GW), ("sgu_w", NG * CHUNK * CHUNK),
               ("sgu_b", NG * CHUNK), ("attn_out_g", A), ("gmlp_out_g", GW), ("norm2_g", D),
               ("final_norm_g", D))
SMALL_ROWS = sum(n for _, n in SMALL_SIZES) // LANES


def _pack_small(tree):
    return jnp.concatenate([tree[n].reshape(-1) for n, _ in SMALL_SIZES]).reshape(SMALL_ROWS, LANES)


def _unpack_small(pack, shapes):
    flat = pack.reshape(-1)
    out, off = {}, 0
    for n, size in SMALL_SIZES:
        out[n] = flat[off:off + size].reshape(shapes[n])
        off += size
    return out


def _small_allreduce_adamw(gpack, wpack, mpack, vpack):
    def body(g_ref, w_ref, m_ref, v_ref, go_ref, d_ref, mo_ref, vo_ref, slots, send_sems, recv_sems):
        x, y, c = _position()
        me = 4 * x + 2 * y + c
        slots[me] = g_ref[...]
        peers = _xor_peers(x, y, c)
        sends = []
        for k, peer in enumerate(peers):
            cp = _remote(g_ref, slots.at[me], send_sems.at[k], recv_sems.at[k], peer)
            cp.start()
            sends.append(cp)
        for k, (px, py, pc) in enumerate(peers):
            _remote(g_ref, slots.at[4 * px + 2 * py + pc], send_sems.at[k], recv_sems.at[k],
                    (px, py, pc)).wait_recv()
        for cp in sends:
            cp.wait_send()
        total = slots[0]
        for k in range(1, NDEV):
            total = total + slots[k]
        go_ref[...] = total
        d, mn, vn = _adamw_math(w_ref[...], total, m_ref[...], v_ref[...])
        d_ref[...] = d
        mo_ref[...] = mn
        vo_ref[...] = vn

    sd = jax.ShapeDtypeStruct((SMALL_ROWS, LANES), F32)
    vm = pl.BlockSpec(memory_space=pltpu.VMEM)
    return pl.pallas_call(
        body, name="small_allreduce_adamw", in_specs=[vm] * 4, out_specs=[vm] * 4, out_shape=[sd] * 4,
        scratch_shapes=[pltpu.VMEM((NDEV, SMALL_ROWS, LANES), F32), pltpu.SemaphoreType.DMA((NDEV - 1,)),
                        pltpu.SemaphoreType.DMA((NDEV - 1,))],
        compiler_params=pltpu.CompilerParams(has_side_effects=True),
    )(gpack, wpack, mpack, vpack)


def kernel(x, norm1_g, w_in, sgu_ln_g, sgu_ln_b, sgu_w, sgu_b, attn_out_g, gmlp_out_g, w_out, norm2_g, w_ff1, w_ff2, final_norm_g, loss_target, m_norm1_g, m_w_in, m_sgu_ln_g, m_sgu_ln_b, m_sgu_w, m_sgu_b, m_attn_out_g, m_gmlp_out_g, m_w_out, m_norm2_g, m_w_ff1, m_w_ff2, m_final_norm_g, v_norm1_g, v_w_in, v_sgu_ln_g, v_sgu_ln_b, v_sgu_w, v_sgu_b, v_attn_out_g, v_gmlp_out_g, v_w_out, v_norm2_g, v_w_ff1, v_w_ff2, v_final_norm_g):
    names = [n for n, _ in SMALL_SIZES]
    w_small = dict(norm1_g=norm1_g, sgu_ln_g=sgu_ln_g, sgu_ln_b=sgu_ln_b, sgu_w=sgu_w, sgu_b=sgu_b,
                   attn_out_g=attn_out_g, gmlp_out_g=gmlp_out_g, norm2_g=norm2_g, final_norm_g=final_norm_g)
    m_small = dict(norm1_g=m_norm1_g, sgu_ln_g=m_sgu_ln_g, sgu_ln_b=m_sgu_ln_b, sgu_w=m_sgu_w, sgu_b=m_sgu_b,
                   attn_out_g=m_attn_out_g, gmlp_out_g=m_gmlp_out_g, norm2_g=m_norm2_g,
                   final_norm_g=m_final_norm_g)
    v_small = dict(norm1_g=v_norm1_g, sgu_ln_g=v_sgu_ln_g, sgu_ln_b=v_sgu_ln_b, sgu_w=v_sgu_w, sgu_b=v_sgu_b,
                   attn_out_g=v_attn_out_g, gmlp_out_g=v_gmlp_out_g, norm2_g=v_norm2_g,
                   final_norm_g=v_final_norm_g)
    shapes = {n: w_small[n].shape for n in names}

    r_in, r_out, r_ff = INW // NCHIP, D // NCHIP, DFF // NCHIP
    o1, o2, o3 = r_in, r_in + r_out, r_in + r_out + r_ff
    start_in = _gather_start(w_in[0].T.astype(BF16), "gather_in_start")
    start_rest = _gather_start(jnp.concatenate([w_out[0], w_ff1[0].T, w_ff2[0]], axis=0).astype(BF16),
                               "gather_rest_start")
    win_t = _gather_wait(*start_in[:4], after=start_rest[4], name="gather_in_wait").reshape(INW, D)

    def rest_weights(after):
        rest = _gather_wait(*start_rest[:4], after=after, name="gather_rest_wait")
        return (rest[:, :r_out].reshape(D, D), rest[:, r_out:r_out + r_ff].reshape(DFF, D),
                rest[:, r_out + r_ff:].reshape(DFF, D))

    small = dict(
        norm1_g=norm1_g, ln_g=sgu_ln_g.reshape(1, GW), ln_b=sgu_ln_b.reshape(1, GW), sgu_w=sgu_w[0],
        sgu_wt=jnp.swapaxes(sgu_w[0], 1, 2), bias_t=jnp.repeat(sgu_b[0].T, DH, axis=1),
        attn_out_g=attn_out_g, gmlp_out_g=gmlp_out_g, norm2_g=norm2_g, final_norm_g=final_norm_g.reshape(1, D))
    xi, yi, ci = _position()
    sel = jnp.stack([2 * xi + yi, ci]).astype(jnp.int32)
    state = {}

    def as_slabs(g):
        return g.reshape(NCHIP, g.shape[0] // NCHIP, D)

    def early_grads(gwff1_t, gwff2, gwout):
        state["early"] = _reduce_start([as_slabs(gwff1_t), as_slabs(gwff2), as_slabs(gwout)], "reduce_early_start")
        return state["early"][4][0:1, 0:1]

    def after_attention_bwd(marker):
        send_sems, recv_sems, parts, lands, _ = state["early"]
        parts, lands = _reduce_wait(send_sems, recv_sems, parts, lands, marker, "reduce_early_wait")
        halves = [_sum_pieces(p, l, sel, "sum_" + n) for p, l, n in zip(parts, lands, ("w_ff1", "w_ff2", "w_out"))]
        state["early_share"] = _share_start(halves, "share_early_start")
        return state["early_share"][4][0:1, 0:1]

    def late_grads(gwin_t):
        state["late"] = _reduce_start([as_slabs(gwin_t)], "reduce_late_start")
        return state["late"][4][0:1, 0:1]

    loss_part, dx, sg, _ = _local_step(
        x[0], loss_target[0], small, win_t, rest_weights, early_grads, after_attention_bwd, late_grads)
    loss = lax.psum(loss_part, ("x", "y", "c"))

    late = state["late"]
    send_sems, recv_sems, halves, lands, _ = state["early_share"]
    own, other = _share_wait(send_sems, recv_sems, halves, lands, dx, "share_early_wait")
    g_big = {n: _join_halves(o, t, ci) for n, o, t in zip(("w_ff1", "w_ff2", "w_out"), own, other)}
    g_big["w_ff1"] = g_big["w_ff1"].T
    w_big = dict(w_in=(w_in, m_w_in, v_w_in), w_out=(w_out, m_w_out, v_w_out),
                 w_ff1=(w_ff1, m_w_ff1, v_w_ff1), w_ff2=(w_ff2, m_w_ff2, v_w_ff2))
    grads, deltas, new_m, new_v = {}, {}, {}, {}

    def update(n):
        w, m, v = w_big[n]
        d, mn, vn = _adamw(w[0], g_big[n], m[0], v[0], "adamw_" + n)
        grads[n], deltas[n], new_m[n], new_v[n] = g_big[n][None], d[None], mn[None], vn[None]

    for n in ("w_ff1", "w_ff2", "w_out"):
        update(n)
    updated = deltas["w_out"][0, 0:8, 0:LANES] + deltas["w_ff1"][0, 0:8, 0:LANES] + deltas["w_ff2"][0, 0:8, 0:LANES]
    late_parts, late_lands = _reduce_wait(late[0], late[1], late[2], late[3], updated, "reduce_late_wait")
    late_share = _share_start([_sum_pieces(late_parts[0], late_lands[0], sel, "sum_w_in")], "share_late_start")

    g_small = dict(norm1_g=sg["norm1_g"], sgu_ln_g=sg["ln_g"], sgu_ln_b=sg["ln_b"], sgu_w=sg["sgu_w"],
                   sgu_b=sg["sgu_b"], attn_out_g=sg["attn_out_g"], gmlp_out_g=sg["gmlp_out_g"],
                   norm2_g=sg["norm2_g"], final_norm_g=sg["final_norm_g"])
    packs = _small_allreduce_adamw(_pack_small(g_small) + late_share[4][0:1, 0:1], _pack_small(w_small),
                                   _pack_small(m_small), _pack_small(v_small))
    for tree, pack in zip((grads, deltas, new_m, new_v), packs):
        tree.update(_unpack_small(pack, shapes))
    own, other = _share_wait(late_share[0], late_share[1], late_share[2], late_share[3], packs[0], "share_late_wait")
    g_big["w_in"] = _join_halves(own[0], other[0], ci).T
    update("w_in")

    order = ["norm1_g", "w_in", "sgu_ln_g", "sgu_ln_b", "sgu_w", "sgu_b", "attn_out_g", "gmlp_out_g", "w_out",
             "norm2_g", "w_ff1", "w_ff2", "final_norm_g"]
    return (loss, dx[None], *[grads[n] for n in order], *[deltas[n] for n in order],
            *[new_m[n] for n in order], *[new_v[n] for n in order])
```

```python
import functools
import math

import numpy as np
import jax
import jax.numpy as jnp
from jax import lax
from jax.experimental import pallas as pl
from jax.experimental.pallas import tpu as pltpu

F32 = jnp.float32
BF16 = jnp.bfloat16

D = 1024
NH = 12
DH = 64
A = NH * DH
NG = 4
GW = NG * DH
INW = 3 * A + 2 * GW
DFF = 4 * D
CHUNK = 128
PATTERNS = ((128, 1), (512, 4), (2048, 16))
EPS = 1e-6
SCALE = DH ** -0.5
NEG = -1e30

LR, B1, B2, AEPS, WD, STEP = 0.001, 0.9, 0.999, 1e-08, 0.01, 10

TM = 512
TMX = 256
ATT_ROWS = 1024
FF_CH = 1024
LANES = 128
NCHIP = 4
VMEM_LIMIT = 56 * 1024 * 1024
MESH = pl.DeviceIdType.MESH


def _cparams(*sem, **kw):
    return pltpu.CompilerParams(dimension_semantics=sem if sem else None,
                                vmem_limit_bytes=VMEM_LIMIT, **kw)


def _dot(a, b):
    return jnp.dot(a, b, preferred_element_type=F32)


def _dot_nt(a, b):
    return lax.dot_general(a, b, (((1,), (1,)), ((), ())), preferred_element_type=F32)


def _dot_tn(a, b):
    return lax.dot_general(a, b, (((0,), (0,)), ((), ())), preferred_element_type=F32)


def _dot_hi(a, b):
    return jnp.dot(a, b, preferred_element_type=F32, precision=lax.Precision.HIGHEST)


def _alibi_slopes(n):
    def pow2(m):
        start = 2.0 ** (-8.0 / m)
        return [start ** (i + 1) for i in range(m)]
    if math.log2(n).is_integer():
        s = pow2(n)
    else:
        c = 2 ** int(math.floor(math.log2(n)))
        s = pow2(c) + pow2(2 * c)[0::2][: n - c]
    return np.asarray(s, dtype=np.float32)


def _rms_fwd(v, g):
    r = lax.rsqrt(jnp.mean(v * v, axis=-1, keepdims=True) + EPS)
    vn = v * r
    return vn * g, vn, r


def _rms_bwd(dy, vn, r, g):
    w = dy * g
    dv = r * (w - vn * jnp.mean(w * vn, axis=-1, keepdims=True))
    return dv, jnp.sum(dy * vn, axis=0, keepdims=True)


_K0 = math.sqrt(2.0 / math.pi)
_K1 = 0.044715


def _gelu(v):
    return 0.5 * v * (1.0 + jnp.tanh(_K0 * (v + _K1 * (v * v * v))))


def _gelu_grad(v):
    t = jnp.tanh(_K0 * (v + _K1 * (v * v * v)))
    return 0.5 * (1.0 + t) + 0.5 * v * (1.0 - t * t) * (_K0 * (1.0 + 3.0 * _K1 * v * v))


def _row_spec(rows, cols):
    return pl.BlockSpec((rows, cols), lambda i: (i, 0))


def _const_spec(shape):
    nd = len(shape)
    return pl.BlockSpec(shape, lambda i: (0,) * nd, pipeline_mode=pl.Buffered(1))


DILS = tuple(d for _, d in PATTERNS)


def _fill_cols(scr, value):
    for cb in range(value.shape[1] // LANES):
        scr[cb] = value[:, cb * LANES:(cb + 1) * LANES]


def _split_residues(scr, out_ref, dil):
    nb, rows, _ = scr.shape
    for r in range(dil):
        for cb in range(nb):
            piece = scr.at[cb][pl.ds(r, rows // dil, stride=dil), :]
            out_ref[r, :, cb * LANES:(cb + 1) * LANES] = piece.astype(out_ref.dtype)


def _merge_residues(in_ref, scr, dil):
    nb, rows, _ = scr.shape
    for r in range(dil):
        for cb in range(nb):
            scr.at[cb][pl.ds(r, rows // dil, stride=dil), :] = in_ref[r, :, cb * LANES:(cb + 1) * LANES].astype(F32)
    return jnp.concatenate([scr[cb] for cb in range(nb)], axis=-1)


def _col_scratch(rows, width):
    return pltpu.VMEM((width // LANES, rows, LANES), F32)


def _res_spec(dil, rows, width):
    return pl.BlockSpec((dil, rows // dil, width), lambda i: (0, i, 0))


def _res_shape(s, dil, width, dtype):
    return jax.ShapeDtypeStruct((dil, s // dil, width), dtype)


def _inproj_fwd(x, g1, win_t):
    s = x.shape[0]
    nd = len(DILS)

    def body(x_ref, g_ref, w_ref, hn_ref, *rest):
        qkv_refs = rest[:3 * nd]
        u_ref, z_ref, scr = rest[3 * nd:]
        hn, _, _ = _rms_fwd(x_ref[...], g_ref[...])
        hn = hn.astype(BF16)
        hn_ref[...] = hn
        for t in range(3):
            seg = _dot_nt(hn, w_ref[t * A:(t + 1) * A, :])
            seg = seg * SCALE if t == 0 else seg
            _fill_cols(scr, seg)
            for di, dil in enumerate(DILS):
                if dil == 1:
                    qkv_refs[t * nd + di][0] = seg.astype(BF16)
                else:
                    _split_residues(scr, qkv_refs[t * nd + di], dil)
        u_ref[...] = _dot_nt(hn, w_ref[3 * A:3 * A + GW, :])
        z_ref[...] = _dot_nt(hn, w_ref[3 * A + GW:INW, :])

    res = pl.pallas_call(
        body, name="inproj_fwd", grid=(s // TM,),
        in_specs=[_row_spec(TM, D), _const_spec((1, D)), _const_spec((INW, D))],
        out_specs=[_row_spec(TM, D)] + [_res_spec(d, TM, A) for _ in range(3) for d in DILS]
                  + [_row_spec(TM, GW), _row_spec(TM, GW)],
        out_shape=[jax.ShapeDtypeStruct((s, D), BF16)] + [_res_shape(s, d, A, BF16) for _ in range(3) for d in DILS]
                  + [jax.ShapeDtypeStruct((s, GW), F32)] * 2,
        scratch_shapes=[_col_scratch(TM, A)],
        compiler_params=_cparams("arbitrary"),
    )(x, g1, win_t)
    hn1 = res[0]
    q, k, v = (res[1 + t * nd:1 + (t + 1) * nd] for t in range(3))
    return hn1, q, k, v, res[-2], res[-1]


def _att_geometry(s, dil):
    length = s // dil
    rows = min(length, ATT_ROWS)
    return length, rows, length // rows, rows // CHUNK


def _stack_heads(t):
    lane = lax.broadcasted_iota(jnp.int32, t.shape, 1)
    zero = jnp.zeros_like(t)
    return jnp.concatenate([jnp.where(lane < DH, t, zero), jnp.where(lane >= DH, t, zero)], axis=0)


def _stack_cols(t):
    return jnp.concatenate([t[:, 0:1], t[:, DH:DH + 1]], axis=0)


def _unstack_heads(t2):
    n = t2.shape[0] // 2
    lane = lax.broadcasted_iota(jnp.int32, (n, LANES), 1)
    return jnp.where(lane < DH, t2[:n], t2[n:])


def _query_window_bias(s0, s1, dil, first):
    row = lax.broadcasted_iota(jnp.int32, (2 * CHUNK, 2 * CHUNK), 0)
    col = lax.broadcasted_iota(jnp.int32, (2 * CHUNK, 2 * CHUNK), 1)
    steps = (row & (CHUNK - 1)) + CHUNK - col
    valid = (steps >= 0) & (steps <= CHUNK)
    if first:
        valid = valid & (col >= CHUNK)
    slope = jnp.where(row < CHUNK, s0, s1)
    return jnp.where(valid, -slope * (steps * dil).astype(F32), NEG)


def _key_block_bias(s0, s1, dil, last):
    key = lax.broadcasted_iota(jnp.int32, (CHUNK, 4 * CHUNK), 0)
    col = lax.broadcasted_iota(jnp.int32, (CHUNK, 4 * CHUNK), 1)
    wq = col & (2 * CHUNK - 1)
    steps = wq - key
    valid = (steps >= 0) & (steps <= CHUNK)
    if last:
        valid = valid & (wq < CHUNK)
    slope = jnp.where(col < 2 * CHUNK, s0, s1)
    return jnp.where(valid, -slope * (steps * dil).astype(F32), NEG)


def _head_rows(t):
    row = lax.broadcasted_iota(jnp.int32, (8, LANES), 0)
    lane = lax.broadcasted_iota(jnp.int32, (8, LANES), 1)
    pick = jnp.where(((row == 0) & (lane == 0)) | ((row == 1) & (lane == DH)), 1.0, 0.0).astype(BF16)
    hi = t.astype(BF16)
    rest = t - hi.astype(F32)
    mid = rest.astype(BF16)
    low = (rest - mid.astype(F32)).astype(BF16)
    return _dot_nt(pick, hi) + _dot_nt(pick, mid) + _dot_nt(pick, low)


def _att_specs(dil, rows, nsub, nblk):
    main = pl.BlockSpec((None, rows, LANES), lambda r, hp, c: (r, c, hp))
    prev = pl.BlockSpec((None, CHUNK, LANES), lambda r, hp, c: (r, jnp.maximum(c * nsub - 1, 0), hp))
    nxt = pl.BlockSpec((None, CHUNK, LANES), lambda r, hp, c: (r, jnp.minimum((c + 1) * nsub, nblk - 1), hp))
    return main, prev, nxt


def _row_start(i):
    return i * CHUNK if isinstance(i, int) else pl.multiple_of(i * CHUNK, CHUNK)


def _attn_fwd(q, k, v, slopes, dil):
    length = q.shape[1]
    _, rows, nch, nsub = _att_geometry(length * dil, dil)
    main, prev, _ = _att_specs(dil, rows, nsub, length // CHUNK)

    def body(sl_ref, q_ref, k_ref, v_ref, kh_ref, vh_ref, o_ref, lse_ref, kbuf, vbuf, bias_buf):
        hp = pl.program_id(1)
        ch = pl.program_id(2)
        kbuf[0:CHUNK, :] = kh_ref[...]
        kbuf[CHUNK:, :] = k_ref[...]
        vbuf[0:CHUNK, :] = vh_ref[...]
        vbuf[CHUNK:, :] = v_ref[...]
        s0, s1 = sl_ref[2 * hp], sl_ref[2 * hp + 1]

        def block(i, bias):
            row = _row_start(i)
            rs = pl.ds(row, CHUNK)
            q2 = _stack_heads(q_ref[rs, :])
            kw = kbuf[pl.ds(row, 2 * CHUNK), :]
            vw = vbuf[pl.ds(row, 2 * CHUNK), :]
            sc = _dot_nt(q2, kw) + bias
            m = jnp.max(sc, axis=-1, keepdims=True)
            p = jnp.exp(sc - m)
            l = jnp.sum(p, axis=-1, keepdims=True)
            o2 = _dot(p.astype(BF16), vw) * (1.0 / l)
            o_ref[rs, :] = _unstack_heads(o2).astype(BF16)
            lse_ref[rs, :] = _unstack_heads(jnp.broadcast_to(m + jnp.log(l), (2 * CHUNK, LANES)))

        bias_buf[...] = _query_window_bias(s0, s1, dil, False)

        @pl.when(ch == 0)
        def _():
            block(0, _query_window_bias(s0, s1, dil, True))

        @pl.when(ch != 0)
        def _():
            block(0, bias_buf[...])

        for i in range(1, nsub):
            block(i, bias_buf[...])

    sd = jax.ShapeDtypeStruct
    return pl.pallas_call(
        body, name=f"attn_fwd_d{dil}", grid=(dil, NH // 2, nch),
        in_specs=[pl.BlockSpec(memory_space=pltpu.SMEM), main, main, main, prev, prev],
        out_specs=[main, main], out_shape=[sd((dil, length, A), BF16), sd((dil, length, A), F32)],
        scratch_shapes=[pltpu.VMEM((rows + CHUNK, LANES), BF16), pltpu.VMEM((rows + CHUNK, LANES), BF16),
                        pltpu.VMEM((2 * CHUNK, 2 * CHUNK), F32)],
        compiler_params=_cparams("arbitrary", "arbitrary", "arbitrary"),
    )(slopes, q, k, v, k, v)


def _attn_bwd_dq(q, k, v, do, lse, delta, slopes, dil):
    length = q.shape[1]
    _, rows, nch, nsub = _att_geometry(length * dil, dil)
    main, prev, _ = _att_specs(dil, rows, nsub, length // CHUNK)

    def body(sl_ref, q_ref, k_ref, v_ref, do_ref, lse_ref, dl_ref, kh_ref, vh_ref, dq_ref, kbuf, vbuf, bias_buf):
        hp = pl.program_id(1)
        ch = pl.program_id(2)
        kbuf[0:CHUNK, :] = kh_ref[...]
        kbuf[CHUNK:, :] = k_ref[...]
        vbuf[0:CHUNK, :] = vh_ref[...]
        vbuf[CHUNK:, :] = v_ref[...]
        s0, s1 = sl_ref[2 * hp], sl_ref[2 * hp + 1]

        def block(i, bias):
            row = _row_start(i)
            rs = pl.ds(row, CHUNK)
            q2 = _stack_heads(q_ref[rs, :])
            do2 = _stack_heads(do_ref[rs, :])
            lse2 = _stack_cols(lse_ref[rs, :])
            dl2 = _stack_cols(dl_ref[rs, :])
            kw = kbuf[pl.ds(row, 2 * CHUNK), :]
            vw = vbuf[pl.ds(row, 2 * CHUNK), :]
            p = jnp.exp(_dot_nt(q2, kw) + bias - lse2)
            ds = p * (_dot_nt(do2, vw) - dl2)
            dq_ref[rs, :] = _unstack_heads(_dot(ds.astype(BF16), kw)).astype(BF16)

        bias_buf[...] = _query_window_bias(s0, s1, dil, False)

        @pl.when(ch == 0)
        def _():
            block(0, _query_window_bias(s0, s1, dil, True))

        @pl.when(ch != 0)
        def _():
            block(0, bias_buf[...])

        for i in range(1, nsub):
            block(i, bias_buf[...])

    return pl.pallas_call(
        body, name=f"attn_dq_d{dil}", grid=(dil, NH // 2, nch),
        in_specs=[pl.BlockSpec(memory_space=pltpu.SMEM), main, main, main, main, main, main, prev, prev],
        out_specs=main, out_shape=jax.ShapeDtypeStruct((dil, length, A), BF16),
        scratch_shapes=[pltpu.VMEM((rows + CHUNK, LANES), BF16), pltpu.VMEM((rows + CHUNK, LANES), BF16),
                        pltpu.VMEM((2 * CHUNK, 2 * CHUNK), F32)],
        compiler_params=_cparams("arbitrary", "arbitrary", "arbitrary"),
    )(slopes, q, k, v, do, lse, delta, k, v)


def _attn_bwd_dkv(q, k, v, do, lse, delta, slopes, dil):
    length = q.shape[1]
    _, rows, nch, nsub = _att_geometry(length * dil, dil)
    main, _, nxt = _att_specs(dil, rows, nsub, length // CHUNK)

    def body(sl_ref, k_ref, v_ref, q_ref, do_ref, lse_ref, dl_ref, qh_ref, doh_ref, lseh_ref, dlh_ref,
             dk_ref, dv_ref, qbuf, dobuf, lse_rows, dl_rows, bias_buf):
        hp = pl.program_id(1)
        ch = pl.program_id(2)
        for buf, main_ref, halo_ref in ((qbuf, q_ref, qh_ref), (dobuf, do_ref, doh_ref)):
            buf[0:rows, :] = main_ref[...]
            buf[rows:, :] = halo_ref[...]
        for buf, main_ref, halo_ref in ((lse_rows, lse_ref, lseh_ref), (dl_rows, dl_ref, dlh_ref)):
            buf[:, 0:rows] = _head_rows(main_ref[...])
            buf[:, rows:] = _head_rows(halo_ref[...])
        s0, s1 = sl_ref[2 * hp], sl_ref[2 * hp + 1]

        def block(i, bias):
            row = _row_start(i)
            rs = pl.ds(row, CHUNK)
            win = pl.ds(row, 2 * CHUNK)
            kc = k_ref[rs, :]
            vc = v_ref[rs, :]
            q2 = _stack_heads(qbuf[win, :])
            do2 = _stack_heads(dobuf[win, :])
            cols = slice(i * CHUNK, (i + 2) * CHUNK)
            lse2 = jnp.concatenate([lse_rows[0:1, cols], lse_rows[1:2, cols]], axis=1)
            dl2 = jnp.concatenate([dl_rows[0:1, cols], dl_rows[1:2, cols]], axis=1)
            pt = jnp.exp(_dot_nt(kc, q2) + bias - lse2)
            dst = pt * (_dot_nt(vc, do2) - dl2)
            dv_ref[rs, :] = _dot(pt.astype(BF16), do2).astype(BF16)
            dk_ref[rs, :] = _dot(dst.astype(BF16), q2).astype(BF16)

        bias_buf[...] = _key_block_bias(s0, s1, dil, False)

        for i in range(nsub - 1):
            block(i, bias_buf[...])

        @pl.when(ch != nch - 1)
        def _():
            block(nsub - 1, bias_buf[...])

        @pl.when(ch == nch - 1)
        def _():
            block(nsub - 1, _key_block_bias(s0, s1, dil, True))

    sd = jax.ShapeDtypeStruct((dil, length, A), BF16)
    return pl.pallas_call(
        body, name=f"attn_dkv_d{dil}", grid=(dil, NH // 2, nch),
        in_specs=[pl.BlockSpec(memory_space=pltpu.SMEM), main, main, main, main, main, main, nxt, nxt, nxt, nxt],
        out_specs=[main, main], out_shape=[sd, sd],
        scratch_shapes=[pltpu.VMEM((rows + CHUNK, LANES), BF16), pltpu.VMEM((rows + CHUNK, LANES), BF16),
                        pltpu.VMEM((8, rows + CHUNK), F32), pltpu.VMEM((8, rows + CHUNK), F32),
                        pltpu.VMEM((CHUNK, 4 * CHUNK), F32)],
        compiler_params=_cparams("arbitrary", "arbitrary", "arbitrary"),
    )(slopes, k, v, q, do, lse, delta, q, do, lse, delta)


def _group_masks(width):
    lane = lax.broadcasted_iota(jnp.int32, (1, width), 1)
    return [(lane >= g * DH) & (lane < (g + 1) * DH) for g in range(width // DH)]


def _group_mean_matrix():
    i = lax.broadcasted_iota(jnp.int32, (GW, GW), 0) // DH
    j = lax.broadcasted_iota(jnp.int32, (GW, GW), 1) // DH
    return jnp.where(i == j, 1.0 / DH, 0.0).astype(F32)


def _tri_mask(lower):
    t = lax.broadcasted_iota(jnp.int32, (CHUNK, CHUNK), 0)
    u = lax.broadcasted_iota(jnp.int32, (CHUNK, CHUNK), 1)
    return (u <= t) if lower else (u >= t)


def _sgu_forward(u, z, lng, lnb, w_ref, bias_t, pmat, rows):
    ug = _gelu(u)
    zg = _gelu(z)
    mu = _dot_hi(zg, pmat)
    zc = zg - mu
    var = _dot_hi(zc * zc, pmat)
    rstd = lax.rsqrt(var + EPS)
    zhat = zc * rstd
    zn = (zhat * lng + lnb).astype(BF16)
    gm = _group_masks(GW)
    tri = _tri_mask(True)
    ws = [jnp.where(tri, w_ref[g], 0.0).astype(BF16) for g in range(NG)]
    pieces = []
    for c in range(rows // CHUNK):
        znc = zn[c * CHUNK:(c + 1) * CHUNK, :]
        mix = None
        for g in range(NG):
            part = jnp.where(gm[g], _dot(ws[g], znc), 0.0)
            mix = part if mix is None else mix + part
        pieces.append(mix + bias_t)
    mixed = jnp.concatenate(pieces, axis=0) if len(pieces) > 1 else pieces[0]
    return ug * mixed, ug, zhat, rstd, zn, mixed


def _mix_fwd(os_, ls_, u, z, x, lng, lnb, sgu_w, bias_t, ga, gg, wout):
    s = x.shape[0]
    nd = len(DILS)
    nscr = sum(1 for d in DILS if d > 1)

    def body(*refs):
        o_refs, l_refs = refs[:nd], refs[nd:2 * nd]
        u_ref, z_ref, x_ref, lng_ref, lnb_ref, w_ref, bt_ref, ga_ref, gg_ref, wo_ref = refs[2 * nd:2 * nd + 10]
        attn_ref = refs[2 * nd + 10]
        lse_refs = refs[2 * nd + 11:3 * nd + 11]
        mixed_ref, h1_ref = refs[3 * nd + 11:3 * nd + 13]
        scr = refs[3 * nd + 13:]
        scr_o, scr_l, scr_lse = scr[:nscr], scr[nscr:2 * nscr], scr[2 * nscr]
        ov, lv, j = [], [], 0
        for di, dil in enumerate(DILS):
            if dil == 1:
                ov.append(o_refs[di][0].astype(F32))
                lv.append(l_refs[di][0])
            else:
                ov.append(_merge_residues(o_refs[di], scr_o[j], dil))
                lv.append(_merge_residues(l_refs[di], scr_l[j], dil))
                j += 1
        mx = functools.reduce(jnp.maximum, lv)
        es = [jnp.exp(l - mx) for l in lv]
        den = functools.reduce(lambda a, b: a + b, es)
        attn = functools.reduce(lambda a, b: a + b, [e * o for e, o in zip(es, ov)]) / den
        attn_ref[...] = attn
        lse = mx + jnp.log(den)
        _fill_cols(scr_lse, lse)
        for di, dil in enumerate(DILS):
            if dil == 1:
                lse_refs[di][0] = lse
            else:
                _split_residues(scr_lse, lse_refs[di], dil)
        an, _, _ = _rms_fwd(attn, ga_ref[...])
        gmv, _, _, _, _, _ = _sgu_forward(u_ref[...], z_ref[...], lng_ref[...], lnb_ref[...], w_ref,
                                          bt_ref[...], _group_mean_matrix(), TMX)
        gn, _, _ = _rms_fwd(gmv, gg_ref[...])
        mixed = jnp.concatenate([an, gn], axis=-1).astype(BF16)
        mixed_ref[...] = mixed
        h1_ref[...] = x_ref[...] + _dot(mixed, wo_ref[...])

    sd = jax.ShapeDtypeStruct
    res = pl.pallas_call(
        body, name="mix_fwd", grid=(s // TMX,),
        in_specs=[_res_spec(d, TMX, A) for d in DILS] * 2 + [_row_spec(TMX, GW), _row_spec(TMX, GW),
                  _row_spec(TMX, D), _const_spec((1, GW)), _const_spec((1, GW)), _const_spec((NG, CHUNK, CHUNK)),
                  _const_spec((CHUNK, GW)), _const_spec((1, A)), _const_spec((1, GW)), _const_spec((D, D))],
        out_specs=[_row_spec(TMX, A)] + [_res_spec(d, TMX, A) for d in DILS] + [_row_spec(TMX, D), _row_spec(TMX, D)],
        out_shape=[sd((s, A), F32)] + [_res_shape(s, d, A, F32) for d in DILS] + [sd((s, D), BF16), sd((s, D), F32)],
        scratch_shapes=[_col_scratch(TMX, A)] * (2 * nscr + 1),
        compiler_params=_cparams("arbitrary"),
    )(*os_, *ls_, u, z, x, lng, lnb, sgu_w, bias_t, ga, gg, wout)
    return res[0], res[1:1 + nd], res[1 + nd], res[2 + nd]


def _mlp_fwd(h1, g2, wff1_t, wff2, gf, target):
    s = h1.shape[0]

    def body(h1_ref, g2_ref, w1_ref, w2_ref, gf_ref, t_ref, hn_ref, rf_ref, dh2_ref, loss_ref, dgf_ref):
        i = pl.program_id(0)
        h1v = h1_ref[...]
        hn, _, _ = _rms_fwd(h1v, g2_ref[...])
        hn = hn.astype(BF16)
        hn_ref[...] = hn
        acc = h1v
        for j in range(DFF // FF_CH):
            cols = slice(j * FF_CH, (j + 1) * FF_CH)
            rf = jnp.maximum(_dot_nt(hn, w1_ref[cols, :]), 0.0)
            act = (rf * rf).astype(BF16)
            rf_ref[:, cols] = rf.astype(BF16)
            acc = acc + _dot(act, w2_ref[cols, :])
        y, h2n, r3 = _rms_fwd(acc, gf_ref[...])
        err = y - t_ref[...]
        part = 0.5 * jnp.sum(jnp.mean(err * err, axis=-1, keepdims=True), axis=0, keepdims=True)
        dy = err * (1.0 / D)
        dh2, dgf = _rms_bwd(dy, h2n, r3, gf_ref[...])
        dh2_ref[...] = dh2

        @pl.when(i == 0)
        def _():
            loss_ref[...] = jnp.zeros_like(loss_ref)
            dgf_ref[...] = jnp.zeros_like(dgf_ref)

        loss_ref[...] += jnp.broadcast_to(part, loss_ref.shape)
        dgf_ref[...] += dgf

    sd = jax.ShapeDtypeStruct
    return pl.pallas_call(
        body, name="mlp_fwd", grid=(s // TM,),
        in_specs=[_row_spec(TM, D), _const_spec((1, D)), _const_spec((DFF, D)), _const_spec((DFF, D)),
                  _const_spec((1, D)), _row_spec(TM, D)],
        out_specs=[_row_spec(TM, D), _row_spec(TM, DFF), _row_spec(TM, D),
                   _const_spec((1, LANES)), _const_spec((1, D))],
        out_shape=[sd((s, D), BF16), sd((s, DFF), BF16), sd((s, D), F32),
                   sd((1, LANES), F32), sd((1, D), F32)],
        compiler_params=_cparams("arbitrary"),
    )(h1, g2, wff1_t, wff2, gf, target)


def _mlp_bwd(dh2, rf, h1, g2, wff1_t, wff2):
    s = h1.shape[0]

    def body(dh2_ref, rf_ref, h1_ref, g2_ref, w1_ref, w2_ref, df_ref, dh1_ref, dg2_ref):
        i = pl.program_id(0)
        dh2v = dh2_ref[...]
        dh2b = dh2v.astype(BF16)
        dhn = jnp.zeros((TM, D), F32)
        for j in range(DFF // FF_CH):
            cols = slice(j * FF_CH, (j + 1) * FF_CH)
            da = _dot_nt(dh2b, w2_ref[cols, :])
            df = (da * (2.0 * rf_ref[:, cols].astype(F32))).astype(BF16)
            df_ref[:, cols] = df
            dhn = dhn + _dot(df, w1_ref[cols, :])
        _, h1n, r2 = _rms_fwd(h1_ref[...], g2_ref[...])
        dres, dg2 = _rms_bwd(dhn, h1n, r2, g2_ref[...])
        dh1_ref[...] = dh2v + dres

        @pl.when(i == 0)
        def _():
            dg2_ref[...] = jnp.zeros_like(dg2_ref)

        dg2_ref[...] += dg2

    sd = jax.ShapeDtypeStruct
    return pl.pallas_call(
        body, name="mlp_bwd", grid=(s // TM,),
        in_specs=[_row_spec(TM, D), _row_spec(TM, DFF), _row_spec(TM, D), _const_spec((1, D)),
                  _const_spec((DFF, D)), _const_spec((DFF, D))],
        out_specs=[_row_spec(TM, DFF), _row_spec(TM, D), _const_spec((1, D))],
        out_shape=[sd((s, DFF), BF16), sd((s, D), F32), sd((1, D), F32)],
        compiler_params=_cparams("arbitrary"),
    )(dh2, rf, h1, g2, wff1_t, wff2)


def _mix_bwd(dh1, attn, u, z, lng, lnb, sgu_w, sgu_wt, bias_t, ga, gg, wout):
    s = dh1.shape[0]
    nsteps = s // TMX
    nd = len(DILS)

    def body(*refs):
        dh1_ref, attn_ref, u_ref, z_ref, lng_ref, lnb_ref, w_ref, wt_ref, bt_ref, ga_ref, gg_ref, wo_ref = refs[:12]
        do_refs, dl_refs = refs[12:12 + nd], refs[12 + nd:12 + 2 * nd]
        (du_ref, dz_ref, dga_ref, dgg_ref, dlng_ref, dlnb_ref, dws_ref, db_ref,
         dbt_acc, scr_do, scr_dl) = refs[12 + 2 * nd:]
        i = pl.program_id(0)

        @pl.when(i == 0)
        def _():
            for r in (dga_ref, dgg_ref, dlng_ref, dlnb_ref, dws_ref, db_ref, dbt_acc):
                r[...] = jnp.zeros_like(r)

        dmixed = _dot_nt(dh1_ref[...].astype(BF16), wo_ref[...])
        attn = attn_ref[...]
        _, an, ra = _rms_fwd(attn, ga_ref[...])
        dattn, dga = _rms_bwd(dmixed[:, :A], an, ra, ga_ref[...])
        dga_ref[...] += dga
        _fill_cols(scr_do, dattn)
        prod = dattn * attn
        delta = jnp.zeros_like(prod)
        for hm in _group_masks(A):
            delta = delta + jnp.where(hm, jnp.sum(jnp.where(hm, prod, 0.0), axis=-1, keepdims=True), 0.0)
        _fill_cols(scr_dl, delta)
        for di, dil in enumerate(DILS):
            if dil == 1:
                do_refs[di][0] = dattn.astype(BF16)
                dl_refs[di][0] = delta
            else:
                _split_residues(scr_do, do_refs[di], dil)
                _split_residues(scr_dl, dl_refs[di], dil)
        pmat = _group_mean_matrix()
        lng = lng_ref[...]
        uv, zv = u_ref[...], z_ref[...]
        gmv, ug, zhat, rstd, zn, mixed = _sgu_forward(uv, zv, lng, lnb_ref[...], w_ref, bt_ref[...], pmat, TMX)
        _, gmn, rg = _rms_fwd(gmv, gg_ref[...])
        dgm, dgg = _rms_bwd(dmixed[:, A:], gmn, rg, gg_ref[...])
        dgg_ref[...] += dgg
        du_ref[...] = dgm * mixed * _gelu_grad(uv)
        dmx = dgm * ug
        dmxb = dmx.astype(BF16)
        gm = _group_masks(GW)
        tri_t = _tri_mask(False)
        wst = [jnp.where(tri_t, wt_ref[g], 0.0).astype(BF16) for g in range(NG)]
        zero = jnp.zeros((CHUNK, GW), BF16)
        dzn_pieces = []
        for c in range(TMX // CHUNK):
            rs = slice(c * CHUNK, (c + 1) * CHUNK)
            dmc = dmxb[rs, :]
            znc = zn[rs, :]
            dbt_acc[...] += dmx[rs, :]
            dzn = None
            for g in range(NG):
                dws_ref[g] += _dot_nt(jnp.where(gm[g], dmc, zero), znc)
                part = jnp.where(gm[g], _dot(wst[g], dmc), 0.0)
                dzn = part if dzn is None else dzn + part
            dzn_pieces.append(dzn)
        dzn = jnp.concatenate(dzn_pieces, axis=0)
        dlng_ref[...] += jnp.sum(dzn * zhat, axis=0, keepdims=True)
        dlnb_ref[...] += jnp.sum(dzn, axis=0, keepdims=True)
        dzh = dzn * lng
        dzg = rstd * (dzh - _dot_hi(dzh, pmat) - zhat * _dot_hi(dzh * zhat, pmat))
        dz_ref[...] = dzg * _gelu_grad(zv)

        @pl.when(i == nsteps - 1)
        def _():
            tri = _tri_mask(True)
            for g in range(NG):
                dws_ref[g] = jnp.where(tri, dws_ref[g], 0.0)
            acc = dbt_acc[...]
            lane = lax.broadcasted_iota(jnp.int32, (CHUNK, LANES), 1)
            out = jnp.zeros((CHUNK, LANES), F32)
            for g in range(NG):
                sg = jnp.sum(jnp.where(gm[g], acc, 0.0), axis=-1, keepdims=True)
                out = jnp.where(lane == g, sg, out)
            db_ref[...] = out

    sd = jax.ShapeDtypeStruct
    res = pl.pallas_call(
        body, name="mix_bwd", grid=(nsteps,),
        in_specs=[_row_spec(TMX, D), _row_spec(TMX, A), _row_spec(TMX, GW), _row_spec(TMX, GW),
                  _const_spec((1, GW)), _const_spec((1, GW)), _const_spec((NG, CHUNK, CHUNK)),
                  _const_spec((NG, CHUNK, CHUNK)), _const_spec((CHUNK, GW)), _const_spec((1, A)),
                  _const_spec((1, GW)), _const_spec((D, D))],
        out_specs=[_res_spec(d, TMX, A) for d in DILS] * 2 + [_row_spec(TMX, GW), _row_spec(TMX, GW),
                   _const_spec((1, A)), _const_spec((1, GW)), _const_spec((1, GW)), _const_spec((1, GW)),
                   _const_spec((NG, CHUNK, CHUNK)), _const_spec((CHUNK, LANES))],
        out_shape=[_res_shape(s, d, A, BF16) for d in DILS] + [_res_shape(s, d, A, F32) for d in DILS]
                  + [sd((s, GW), F32), sd((s, GW), F32),
                   sd((1, A), F32), sd((1, GW), F32), sd((1, GW), F32), sd((1, GW), F32),
                   sd((NG, CHUNK, CHUNK), F32), sd((CHUNK, LANES), F32)],
        scratch_shapes=[pltpu.VMEM((CHUNK, GW), F32), _col_scratch(TMX, A), _col_scratch(TMX, A)],
        compiler_params=_cparams("arbitrary"),
    )(dh1, attn, u, z, lng, lnb, sgu_w, sgu_wt, bias_t, ga, gg, wout)
    return (res[:nd], res[nd:2 * nd]) + tuple(res[2 * nd:])


def _dproj_merge(dqs, dks, dvs, du, dz, pin):
    s = du.shape[0]
    nd = len(DILS)
    nscr = sum(1 for d in DILS if d > 1)

    def body(*refs):
        pin_ref = refs[0]
        parts = [refs[1 + t * nd:1 + (t + 1) * nd] for t in range(3)]
        du_ref, dz_ref, dp_ref = refs[1 + 3 * nd:4 + 3 * nd]
        scr = refs[4 + 3 * nd:]
        sums = []
        for t in range(3):
            total, j = None, 0
            for di, dil in enumerate(DILS):
                if dil == 1:
                    term = parts[t][di][0].astype(F32)
                else:
                    term = _merge_residues(parts[t][di], scr[t * nscr + j], dil)
                    j += 1
                total = term if total is None else total + term
            sums.append(total)
        dp_ref[...] = jnp.concatenate([sums[0] * SCALE, sums[1], sums[2], du_ref[...] + pin_ref[0, 0], dz_ref[...]],
                                      axis=-1).astype(BF16)

    return pl.pallas_call(
        body, name="dproj_merge", grid=(s // TMX,),
        in_specs=[pl.BlockSpec(memory_space=pltpu.SMEM)] + [_res_spec(d, TMX, A) for d in DILS] * 3
                 + [_row_spec(TMX, GW)] * 2,
        out_specs=_row_spec(TMX, INW), out_shape=jax.ShapeDtypeStruct((s, INW), BF16),
        scratch_shapes=[_col_scratch(TMX, A)] * (3 * nscr),
        compiler_params=_cparams("arbitrary"),
    )(pin, *dqs, *dks, *dvs, du, dz)


def _inproj_bwd(dproj, dh1, x, g1, win_t):
    s = x.shape[0]

    def body(dp_ref, dh1_ref, x_ref, g_ref, w_ref, dx_ref, dg_ref):
        i = pl.program_id(0)
        dhn = _dot(dp_ref[...], w_ref[...])
        _, xn, r1 = _rms_fwd(x_ref[...], g_ref[...])
        dres, dg = _rms_bwd(dhn, xn, r1, g_ref[...])
        dx_ref[...] = dh1_ref[...] + dres

        @pl.when(i == 0)
        def _():
            dg_ref[...] = jnp.zeros_like(dg_ref)

        dg_ref[...] += dg

    sd = jax.ShapeDtypeStruct
    return pl.pallas_call(
        body, name="inproj_bwd", grid=(s // TM,),
        in_specs=[_row_spec(TM, INW), _row_spec(TM, D), _row_spec(TM, D), _const_spec((1, D)), _const_spec((INW, D))],
        out_specs=[_row_spec(TM, D), _const_spec((1, D))],
        out_shape=[sd((s, D), F32), sd((1, D), F32)],
        compiler_params=_cparams("arbitrary"),
    )(dproj, dh1, x, g1, win_t)


def _wgrad(a, b, name, bm, bn, bk=2 * TM, square_a=False):
    s, m = a.shape
    n = b.shape[1]
    bm, bn = min(bm, m), min(bn, n)

    def body(a_ref, b_ref, o_ref):
        @pl.when(pl.program_id(2) == 0)
        def _():
            o_ref[...] = jnp.zeros_like(o_ref)

        av = a_ref[...]
        if square_a:
            av = av.astype(F32)
            av = av * av
        o_ref[...] += _dot_tn(av.astype(BF16), b_ref[...].astype(BF16))

    return pl.pallas_call(
        body, name=name, grid=(m // bm, n // bn, s // bk),
        in_specs=[pl.BlockSpec((bk, bm), lambda i, j, k: (k, i)), pl.BlockSpec((bk, bn), lambda i, j, k: (k, j))],
        out_specs=pl.BlockSpec((bm, bn), lambda i, j, k: (i, j)),
        out_shape=jax.ShapeDtypeStruct((m, n), F32),
        compiler_params=_cparams("arbitrary", "arbitrary", "arbitrary"),
    )(a, b)


def _adamw_math(w, g, m, v):
    m = B1 * m + (1.0 - B1) * g
    v = B2 * v + (1.0 - B2) * (g * g)
    m_hat = m / (1.0 - B1 ** STEP)
    v_hat = v / (1.0 - B2 ** STEP)
    delta = -LR * (m_hat / (jnp.sqrt(v_hat) + AEPS) + WD * w)
    return delta, m, v


def _adamw(w, g, m, v, name):
    rows, cols = w.shape
    br = min(rows, 256)
    while rows % br:
        br -= 8

    def body(w_ref, g_ref, m_ref, v_ref, d_ref, mo_ref, vo_ref):
        d, mn, vn = _adamw_math(w_ref[...], g_ref[...], m_ref[...], v_ref[...])
        d_ref[...] = d
        mo_ref[...] = mn
        vo_ref[...] = vn

    spec = _row_spec(br, cols)
    sd = jax.ShapeDtypeStruct((rows, cols), F32)
    return pl.pallas_call(
        body, name=name, grid=(rows // br,), in_specs=[spec] * 4, out_specs=[spec] * 3,
        out_shape=[sd, sd, sd], compiler_params=_cparams("arbitrary"),
    )(w, g, m, v)


def _local_step(x, target, small, win_t, rest_weights, early_grads=None, after_attention_bwd=None,
                late_grads=None):
    slopes = jnp.asarray(_alibi_slopes(NH))
    hn1, q, k, v, u, z = _inproj_fwd(x, small["norm1_g"], win_t)
    outs, lses = [], []
    for i, dil in enumerate(DILS):
        o, l = _attn_fwd(q[i], k[i], v[i], slopes, dil)
        outs.append(o)
        lses.append(l)
    wout, wff1_t, wff2 = rest_weights(lses[-1])
    attn, lse, mixed, h1 = _mix_fwd(outs, lses, u, z, x, small["ln_g"], small["ln_b"], small["sgu_w"],
                                    small["bias_t"], small["attn_out_g"], small["gmlp_out_g"], wout)
    hn2, rf, dh2, loss, dgf = _mlp_fwd(h1, small["norm2_g"], wff1_t, wff2, small["final_norm_g"], target)
    df, dh1, dg2 = _mlp_bwd(dh2, rf, h1, small["norm2_g"], wff1_t, wff2)
    gwff1_t = _wgrad(df, hn2, "wgrad_ff1", 1024, D)
    gwff2 = _wgrad(rf, dh2, "wgrad_ff2", 1024, D, square_a=True)
    gwout = _wgrad(mixed, dh1, "wgrad_out", D, D)
    ga, g1 = small["attn_out_g"], small["norm1_g"]
    pin = early_grads(gwff1_t, gwff2, gwout) if early_grads else None
    if pin is not None:
        ga = ga + pin
    (do, delta, du, dz, dga, dgg, dlng, dlnb, dws, db) = _mix_bwd(
        dh1, attn, u, z, small["ln_g"], small["ln_b"], small["sgu_w"], small["sgu_wt"], small["bias_t"],
        ga, small["gmlp_out_g"], wout)
    dqs, dks, dvs = [], [], []
    for i, dil in enumerate(DILS):
        dqs.append(_attn_bwd_dq(q[i], k[i], v[i], do[i], lse[i], delta[i], slopes, dil))
        dk, dv = _attn_bwd_dkv(q[i], k[i], v[i], do[i], lse[i], delta[i], slopes, dil)
        dks.append(dk)
        dvs.append(dv)
    marker = functools.reduce(lambda a, b: a + b, [t[0, 0:8, 0:LANES] for t in dqs + dks + dvs])
    pin = after_attention_bwd(marker) if after_attention_bwd else None
    dproj = _dproj_merge(dqs, dks, dvs, du, dz, jnp.zeros((1, 1), F32) if pin is None else pin)
    gwin_t = _wgrad(dproj, hn1, "wgrad_in", INW // 2, D)
    pin = late_grads(gwin_t) if late_grads else None
    if pin is not None:
        g1 = g1 + pin
    dx, dg1 = _inproj_bwd(dproj, dh1, x, g1, win_t)
    small_grads = dict(norm1_g=dg1, ln_g=dlng, ln_b=dlnb, sgu_w=dws, sgu_b=db[:, :NG].T,
                       attn_out_g=dga, gmlp_out_g=dgg, norm2_g=dg2, final_norm_g=dgf)
    return loss[0, 0], dx, small_grads, (gwin_t, gwout, gwff1_t, gwff2)


ANY = pl.BlockSpec(memory_space=pl.ANY)
NDEV = 8


def _position():
    return lax.axis_index("x"), lax.axis_index("y"), lax.axis_index("c")


def _other_chips(x, y):
    return [(1 - x, y), (x, 1 - y), (1 - x, 1 - y)]


def _remote(src, dst, send_sem, recv_sem, device):
    return pltpu.make_async_remote_copy(src_ref=src, dst_ref=dst, send_sem=send_sem, recv_sem=recv_sem,
                                        device_id=device, device_id_type=MESH)


HBM = pl.BlockSpec(memory_space=pltpu.HBM)
SEM = pl.BlockSpec(memory_space=pltpu.SEMAPHORE)
DATAFLOW = pltpu.SideEffectType.DATAFLOW_SIDE_EFFECTING


def _in_hbm(a):
    return pltpu.with_memory_space_constraint(a, pltpu.HBM)


def _gather_start(shard, name):
    def body(w_ref, land_ref, send_sems, recv_sems, w_thru, land_thru, token):
        x, y, c = _position()
        for k, (px, py) in enumerate(_other_chips(x, y)):
            _remote(w_ref, land_ref.at[2 * x + y], send_sems.at[k], recv_sems.at[k], (px, py, c)).start()
        token[...] = jnp.zeros_like(token)

    land = jnp.broadcast_to(shard[None], (NCHIP,) + shard.shape)
    return pl.pallas_call(
        body, name=name,
        out_shape=(pltpu.SemaphoreType.DMA((3,)), pltpu.SemaphoreType.DMA((3,)),
                   pltpu.HBM(shard.shape, shard.dtype), pltpu.HBM(land.shape, land.dtype),
                   jax.ShapeDtypeStruct((8, LANES), F32)),
        in_specs=(HBM, HBM), out_specs=(SEM, SEM, HBM, HBM, pl.BlockSpec(memory_space=pltpu.VMEM)),
        input_output_aliases={0: 2, 1: 3},
        compiler_params=pltpu.CompilerParams(has_side_effects=DATAFLOW),
    )(_in_hbm(shard), _in_hbm(land))


def _gather_wait(send_sems, recv_sems, w_thru, land_thru, after, name):
    def body(w_ref, land_ref, send_sems, recv_sems, after_ref, w_dead, got_ref):
        x, y, c = _position()
        for k, (px, py) in enumerate(_other_chips(x, y)):
            cp = _remote(w_ref, land_ref.at[2 * px + py], send_sems.at[k], recv_sems.at[k], (px, py, c))
            cp.wait_send()
            cp.wait_recv()

    return pl.pallas_call(
        body, name=name,
        out_shape=(pltpu.HBM(w_thru.shape, w_thru.dtype), pltpu.HBM(land_thru.shape, land_thru.dtype)),
        in_specs=(HBM, HBM, SEM, SEM, ANY), out_specs=(HBM, HBM),
        input_output_aliases={0: 0, 1: 1},
        compiler_params=pltpu.CompilerParams(has_side_effects=DATAFLOW),
    )(w_thru, land_thru, send_sems, recv_sems, after)[1]


def _xor_peers(x, y, c):
    peers = []
    for k in range(1, NDEV):
        kx, ky, kc = (k >> 2) & 1, (k >> 1) & 1, k & 1
        peers.append((1 - x if kx else x, 1 - y if ky else y, 1 - c if kc else c))
    return peers


def _piece(part_ref, px, py, pc):
    half = part_ref.shape[1] // 2
    return part_ref.at[2 * px + py, pl.ds(pc * half, half), :]


def _split_call(body, name, operands, n_sems, extra_out=()):
    n = len(operands)
    sems = tuple(pltpu.SemaphoreType.DMA((m,)) for m in n_sems)
    thru = tuple(pltpu.HBM(a.shape, a.dtype) for a in operands)
    return pl.pallas_call(
        body, name=name, out_shape=sems + thru + tuple(extra_out),
        in_specs=(HBM,) * n,
        out_specs=(SEM,) * len(sems) + (HBM,) * n + (pl.BlockSpec(memory_space=pltpu.VMEM),) * len(extra_out),
        input_output_aliases={i: len(sems) + i for i in range(n)},
        compiler_params=pltpu.CompilerParams(has_side_effects=DATAFLOW),
    )(*[_in_hbm(a) for a in operands])


TOKEN = jax.ShapeDtypeStruct((8, LANES), F32)


def _reduce_start(parts, name):
    nw = len(parts)
    lands = [lax.empty((NDEV - 1, p.shape[1] // 2, D), F32) for p in parts]

    def body(*refs):
        part_refs, land_refs = refs[:nw], refs[nw:2 * nw]
        send_sems, recv_sems = refs[2 * nw:2 * nw + 2]
        token = refs[-1]
        x, y, c = _position()
        for w in range(nw):
            for k, peer in enumerate(_xor_peers(x, y, c)):
                n = w * (NDEV - 1) + k
                _remote(_piece(part_refs[w], *peer), land_refs[w].at[k], send_sems.at[n], recv_sems.at[n],
                        peer).start()
        token[...] = jnp.zeros_like(token)

    n = nw * (NDEV - 1)
    res = _split_call(body, name, list(parts) + lands, (n, n), (TOKEN,))
    return res[0], res[1], res[2:2 + nw], res[2 + nw:2 + 2 * nw], res[-1]


def _reduce_wait(send_sems, recv_sems, parts, lands, after, name):
    nw = len(parts)

    def body(*refs):
        part_refs, land_refs = refs[:nw], refs[nw:2 * nw]
        send_sems, recv_sems = refs[2 * nw:2 * nw + 2]
        x, y, c = _position()
        for w in range(nw):
            for k, peer in enumerate(_xor_peers(x, y, c)):
                n = w * (NDEV - 1) + k
                cp = _remote(_piece(part_refs[w], *peer), land_refs[w].at[k], send_sems.at[n], recv_sems.at[n], peer)
                cp.wait_send()
                cp.wait_recv()

    operands = list(parts) + list(lands)
    res = pl.pallas_call(
        body, name=name, out_shape=tuple(pltpu.HBM(a.shape, a.dtype) for a in operands),
        in_specs=(HBM,) * (2 * nw) + (SEM, SEM, ANY), out_specs=(HBM,) * (2 * nw),
        input_output_aliases={i: i for i in range(2 * nw)},
        compiler_params=pltpu.CompilerParams(has_side_effects=DATAFLOW),
    )(*operands, send_sems, recv_sems, after)
    return res[:nw], res[nw:]


def _sum_pieces(part, land, sel, name):
    half = part.shape[1] // 2
    br = 128 if half % 128 == 0 else half // 2
    nb = half // br

    def body(sel_ref, own_ref, *refs):
        acc = own_ref[...]
        for r in refs[:NDEV - 1]:
            acc = acc + r[...]
        refs[NDEV - 1][...] = acc

    own_spec = pl.BlockSpec((None, br, D), lambda i, sel_ref: (sel_ref[0], sel_ref[1] * nb + i, 0))
    slot_specs = [pl.BlockSpec((None, br, D), functools.partial(lambda i, sel_ref, k: (k, i, 0), k=k))
                  for k in range(NDEV - 1)]
    return pl.pallas_call(
        body, name=name,
        grid_spec=pltpu.PrefetchScalarGridSpec(
            num_scalar_prefetch=1, grid=(nb,), in_specs=[own_spec] + slot_specs,
            out_specs=pl.BlockSpec((br, D), lambda i, sel_ref: (i, 0))),
        out_shape=jax.ShapeDtypeStruct((half, D), F32),
        compiler_params=_cparams("arbitrary"),
    )(sel, part, *([land] * (NDEV - 1)))


def _share_start(halves, name):
    nw = len(halves)
    lands = [lax.empty(h.shape, F32) for h in halves]

    def body(*refs):
        h_refs, land_refs = refs[:nw], refs[nw:2 * nw]
        send_sems, recv_sems = refs[2 * nw:2 * nw + 2]
        token = refs[-1]
        x, y, c = _position()
        for w in range(nw):
            _remote(h_refs[w], land_refs[w], send_sems.at[w], recv_sems.at[w], (x, y, 1 - c)).start()
        token[...] = jnp.zeros_like(token)

    res = _split_call(body, name, list(halves) + lands, (nw, nw), (TOKEN,))
    return res[0], res[1], res[2:2 + nw], res[2 + nw:2 + 2 * nw], res[-1]


def _share_wait(send_sems, recv_sems, halves, lands, after, name):
    nw = len(halves)

    def body(*refs):
        h_refs, land_refs = refs[:nw], refs[nw:2 * nw]
        send_sems, recv_sems = refs[2 * nw:2 * nw + 2]
        x, y, c = _position()
        for w in range(nw):
            cp = _remote(h_refs[w], land_refs[w], send_sems.at[w], recv_sems.at[w], (x, y, 1 - c))
            cp.wait_send()
            cp.wait_recv()

    operands = list(halves) + list(lands)
    res = pl.pallas_call(
        body, name=name, out_shape=tuple(pltpu.HBM(a.shape, a.dtype) for a in operands),
        in_specs=(HBM,) * (2 * nw) + (SEM, SEM, ANY), out_specs=(HBM,) * (2 * nw),
        input_output_aliases={i: i for i in range(2 * nw)},
        compiler_params=pltpu.CompilerParams(has_side_effects=DATAFLOW),
    )(*operands, send_sems, recv_sems, after)
    return res[:nw], res[nw:]


def _join_halves(own, other, c):
    first = jnp.where(c == 0, own, other)
    second = jnp.where(c == 0, other, own)
    return jnp.concatenate([first, second], axis=0)


SMALL_SIZES = (("norm1_g", D), ("sgu_ln_g", GW), ("sgu_ln_b", GW), ("sgu_w", NG * CHUNK * CHUNK),
               ("sgu_b", NG * CHUNK), ("attn_out_g", A), ("gmlp_out_g", GW), ("norm2_g", D),
               ("final_norm_g", D))
PARAM_ROWS = sum(n for _, n in SMALL_SIZES) // LANES
SMALL_ROWS = PARAM_ROWS + 8


def _pack_small(tree, first_extra=None):
    extra = jnp.zeros((8 * LANES,), F32)
    if first_extra is not None:
        extra = extra.at[0].set(first_extra)
    flat = jnp.concatenate([tree[n].reshape(-1) for n, _ in SMALL_SIZES] + [extra])
    return flat.reshape(SMALL_ROWS, LANES)


def _unpack_small(pack, shapes):
    flat = pack.reshape(-1)
    out, off = {}, 0
    for n, size in SMALL_SIZES:
        out[n] = flat[off:off + size].reshape(shapes[n])
        off += size
    return out


def _small_allreduce_adamw(gpack, wpack, mpack, vpack):
    def body(g_ref, w_ref, m_ref, v_ref, go_ref, d_ref, mo_ref, vo_ref, slots, send_sems, recv_sems):
        x, y, c = _position()
        me = 4 * x + 2 * y + c
        slots[me] = g_ref[...]
        peers = _xor_peers(x, y, c)
        sends = []
        for k, peer in enumerate(peers):
            cp = _remote(g_ref, slots.at[me], send_sems.at[k], recv_sems.at[k], peer)
            cp.start()
            sends.append(cp)
        for k, (px, py, pc) in enumerate(peers):
            _remote(g_ref, slots.at[4 * px + 2 * py + pc], send_sems.at[k], recv_sems.at[k],
                    (px, py, pc)).wait_recv()
        for cp in sends:
            cp.wait_send()
        total = slots[0]
        for k in range(1, NDEV):
            total = total + slots[k]
        go_ref[...] = total
        d, mn, vn = _adamw_math(w_ref[...], total, m_ref[...], v_ref[...])
        d_ref[...] = d
        mo_ref[...] = mn
        vo_ref[...] = vn

    sd = jax.ShapeDtypeStruct((SMALL_ROWS, LANES), F32)
    vm = pl.BlockSpec(memory_space=pltpu.VMEM)
    return pl.pallas_call(
        body, name="small_allreduce_adamw", in_specs=[vm] * 4, out_specs=[vm] * 4, out_shape=[sd] * 4,
        scratch_shapes=[pltpu.VMEM((NDEV, SMALL_ROWS, LANES), F32), pltpu.SemaphoreType.DMA((NDEV - 1,)),
                        pltpu.SemaphoreType.DMA((NDEV - 1,))],
        compiler_params=pltpu.CompilerParams(has_side_effects=True),
    )(gpack, wpack, mpack, vpack)


def kernel(x, norm1_g, w_in, sgu_ln_g, sgu_ln_b, sgu_w, sgu_b, attn_out_g, gmlp_out_g, w_out, norm2_g, w_ff1, w_ff2, final_norm_g, loss_target, m_norm1_g, m_w_in, m_sgu_ln_g, m_sgu_ln_b, m_sgu_w, m_sgu_b, m_attn_out_g, m_gmlp_out_g, m_w_out, m_norm2_g, m_w_ff1, m_w_ff2, m_final_norm_g, v_norm1_g, v_w_in, v_sgu_ln_g, v_sgu_ln_b, v_sgu_w, v_sgu_b, v_attn_out_g, v_gmlp_out_g, v_w_out, v_norm2_g, v_w_ff1, v_w_ff2, v_final_norm_g):
    names = [n for n, _ in SMALL_SIZES]
    w_small = dict(norm1_g=norm1_g, sgu_ln_g=sgu_ln_g, sgu_ln_b=sgu_ln_b, sgu_w=sgu_w, sgu_b=sgu_b,
                   attn_out_g=attn_out_g, gmlp_out_g=gmlp_out_g, norm2_g=norm2_g, final_norm_g=final_norm_g)
    m_small = dict(norm1_g=m_norm1_g, sgu_ln_g=m_sgu_ln_g, sgu_ln_b=m_sgu_ln_b, sgu_w=m_sgu_w, sgu_b=m_sgu_b,
                   attn_out_g=m_attn_out_g, gmlp_out_g=m_gmlp_out_g, norm2_g=m_norm2_g,
                   final_norm_g=m_final_norm_g)
    v_small = dict(norm1_g=v_norm1_g, sgu_ln_g=v_sgu_ln_g, sgu_ln_b=v_sgu_ln_b, sgu_w=v_sgu_w, sgu_b=v_sgu_b,
                   attn_out_g=v_attn_out_g, gmlp_out_g=v_gmlp_out_g, norm2_g=v_norm2_g,
                   final_norm_g=v_final_norm_g)
    shapes = {n: w_small[n].shape for n in names}

    r_in, r_out, r_ff = INW // NCHIP, D // NCHIP, DFF // NCHIP
    o1, o2, o3 = r_in, r_in + r_out, r_in + r_out + r_ff
    start_in = _gather_start(w_in[0].T.astype(BF16), "gather_in_start")
    start_rest = _gather_start(jnp.concatenate([w_out[0], w_ff1[0].T, w_ff2[0]], axis=0).astype(BF16),
                               "gather_rest_start")
    win_t = _gather_wait(*start_in[:4], after=start_rest[4], name="gather_in_wait").reshape(INW, D)

    def rest_weights(after):
        rest = _gather_wait(*start_rest[:4], after=after, name="gather_rest_wait")
        return (rest[:, :r_out].reshape(D, D), rest[:, r_out:r_out + r_ff].reshape(DFF, D),
                rest[:, r_out + r_ff:].reshape(DFF, D))

    small = dict(
        norm1_g=norm1_g, ln_g=sgu_ln_g.reshape(1, GW), ln_b=sgu_ln_b.reshape(1, GW), sgu_w=sgu_w[0],
        sgu_wt=jnp.swapaxes(sgu_w[0], 1, 2), bias_t=jnp.repeat(sgu_b[0].T, DH, axis=1),
        attn_out_g=attn_out_g, gmlp_out_g=gmlp_out_g, norm2_g=norm2_g, final_norm_g=final_norm_g.reshape(1, D))
    xi, yi, ci = _position()
    sel = jnp.stack([2 * xi + yi, ci]).astype(jnp.int32)
    state = {}

    def as_slabs(g):
        return g.reshape(NCHIP, g.shape[0] // NCHIP, D)

    def early_grads(gwff1_t, gwff2, gwout):
        state["early"] = _reduce_start([as_slabs(gwff1_t), as_slabs(gwff2), as_slabs(gwout)], "reduce_early_start")
        return state["early"][4][0:1, 0:1]

    def after_attention_bwd(marker):
        send_sems, recv_sems, parts, lands, _ = state["early"]
        parts, lands = _reduce_wait(send_sems, recv_sems, parts, lands, marker, "reduce_early_wait")
        halves = [_sum_pieces(p, l, sel, "sum_" + n) for p, l, n in zip(parts, lands, ("w_ff1", "w_ff2", "w_out"))]
        state["early_share"] = _share_start(halves, "share_early_start")
        return state["early_share"][4][0:1, 0:1]

    def late_grads(gwin_t):
        state["late"] = _reduce_start([as_slabs(gwin_t)], "reduce_late_start")
        return state["late"][4][0:1, 0:1]

    loss_part, dx, sg, _ = _local_step(
        x[0], loss_target[0], small, win_t, rest_weights, early_grads, after_attention_bwd, late_grads)
    late = state["late"]
    send_sems, recv_sems, halves, lands, _ = state["early_share"]
    own, other = _share_wait(send_sems, recv_sems, halves, lands, dx, "share_early_wait")
    g_big = {n: _join_halves(o, t, ci) for n, o, t in zip(("w_ff1", "w_ff2", "w_out"), own, other)}
    g_big["w_ff1"] = g_big["w_ff1"].T
    w_big = dict(w_in=(w_in, m_w_in, v_w_in), w_out=(w_out, m_w_out, v_w_out),
                 w_ff1=(w_ff1, m_w_ff1, v_w_ff1), w_ff2=(w_ff2, m_w_ff2, v_w_ff2))
    grads, deltas, new_m, new_v = {}, {}, {}, {}

    def update(n):
        w, m, v = w_big[n]
        d, mn, vn = _adamw(w[0], g_big[n], m[0], v[0], "adamw_" + n)
        grads[n], deltas[n], new_m[n], new_v[n] = g_big[n][None], d[None], mn[None], vn[None]

    for n in ("w_ff1", "w_ff2", "w_out"):
        update(n)
    updated = deltas["w_out"][0, 0:8, 0:LANES] + deltas["w_ff1"][0, 0:8, 0:LANES] + deltas["w_ff2"][0, 0:8, 0:LANES]
    late_parts, late_lands = _reduce_wait(late[0], late[1], late[2], late[3], updated, "reduce_late_wait")
    late_share = _share_start([_sum_pieces(late_parts[0], late_lands[0], sel, "sum_w_in")], "share_late_start")

    g_small = dict(norm1_g=sg["norm1_g"], sgu_ln_g=sg["ln_g"], sgu_ln_b=sg["ln_b"], sgu_w=sg["sgu_w"],
                   sgu_b=sg["sgu_b"], attn_out_g=sg["attn_out_g"], gmlp_out_g=sg["gmlp_out_g"],
                   norm2_g=sg["norm2_g"], final_norm_g=sg["final_norm_g"])
    packs = _small_allreduce_adamw(_pack_small(g_small, loss_part) + late_share[4][0:1, 0:1], _pack_small(w_small),
                                   _pack_small(m_small), _pack_small(v_small))
    loss = packs[0][PARAM_ROWS, 0]
    for tree, pack in zip((grads, deltas, new_m, new_v), packs):
        tree.update(_unpack_small(pack, shapes))
    own, other = _share_wait(late_share[0], late_share[1], late_share[2], late_share[3], packs[0], "share_late_wait")
    g_big["w_in"] = _join_halves(own[0], other[0], ci).T
    update("w_in")

    order = ["norm1_g", "w_in", "sgu_ln_g", "sgu_ln_b", "sgu_w", "sgu_b", "attn_out_g", "gmlp_out_g", "w_out",
             "norm2_g", "w_ff1", "w_ff2", "final_norm_g"]
    return (loss, dx[None], *[grads[n] for n in order], *[deltas[n] for n in order],
            *[new_m[n] for n in order], *[new_v[n] for n in order])
```

```python
import functools
import math

import numpy as np
import jax
import jax.numpy as jnp
from jax import lax
from jax.experimental import pallas as pl
from jax.experimental.pallas import tpu as pltpu

F32 = jnp.float32
BF16 = jnp.bfloat16

D = 1024
NH = 12
DH = 64
A = NH * DH
NG = 4
GW = NG * DH
INW = 3 * A + 2 * GW
DFF = 4 * D
CHUNK = 128
PATTERNS = ((128, 1), (512, 4), (2048, 16))
EPS = 1e-6
SCALE = DH ** -0.5
NEG = -1e30

LR, B1, B2, AEPS, WD, STEP = 0.001, 0.9, 0.999, 1e-08, 0.01, 10

TM = 512
TMX = 256
ATT_ROWS = 1024
FF_CH = 1024
LANES = 128
NCHIP = 4
VMEM_LIMIT = 56 * 1024 * 1024
MESH = pl.DeviceIdType.MESH


def _cparams(*sem, **kw):
    return pltpu.CompilerParams(dimension_semantics=sem if sem else None,
                                vmem_limit_bytes=VMEM_LIMIT, **kw)


def _dot(a, b):
    return jnp.dot(a, b, preferred_element_type=F32)


def _dot_nt(a, b):
    return lax.dot_general(a, b, (((1,), (1,)), ((), ())), preferred_element_type=F32)


def _dot_tn(a, b):
    return lax.dot_general(a, b, (((0,), (0,)), ((), ())), preferred_element_type=F32)


def _dot_hi(a, b):
    return jnp.dot(a, b, preferred_element_type=F32, precision=lax.Precision.HIGHEST)


def _alibi_slopes(n):
    def pow2(m):
        start = 2.0 ** (-8.0 / m)
        return [start ** (i + 1) for i in range(m)]
    if math.log2(n).is_integer():
        s = pow2(n)
    else:
        c = 2 ** int(math.floor(math.log2(n)))
        s = pow2(c) + pow2(2 * c)[0::2][: n - c]
    return np.asarray(s, dtype=np.float32)


def _rms_fwd(v, g):
    r = lax.rsqrt(jnp.mean(v * v, axis=-1, keepdims=True) + EPS)
    vn = v * r
    return vn * g, vn, r


def _rms_bwd(dy, vn, r, g):
    w = dy * g
    dv = r * (w - vn * jnp.mean(w * vn, axis=-1, keepdims=True))
    return dv, jnp.sum(dy * vn, axis=0, keepdims=True)


_K0 = math.sqrt(2.0 / math.pi)
_K1 = 0.044715


def _gelu(v):
    return 0.5 * v * (1.0 + jnp.tanh(_K0 * (v + _K1 * (v * v * v))))


def _gelu_grad(v):
    t = jnp.tanh(_K0 * (v + _K1 * (v * v * v)))
    return 0.5 * (1.0 + t) + 0.5 * v * (1.0 - t * t) * (_K0 * (1.0 + 3.0 * _K1 * v * v))


def _row_spec(rows, cols):
    return pl.BlockSpec((rows, cols), lambda i: (i, 0))


def _const_spec(shape):
    nd = len(shape)
    return pl.BlockSpec(shape, lambda i: (0,) * nd, pipeline_mode=pl.Buffered(1))


DILS = tuple(d for _, d in PATTERNS)


def _fill_cols(scr, value):
    for cb in range(value.shape[1] // LANES):
        scr[cb] = value[:, cb * LANES:(cb + 1) * LANES]


def _split_residues(scr, out_ref, dil):
    nb, rows, _ = scr.shape
    for r in range(dil):
        for cb in range(nb):
            piece = scr.at[cb][pl.ds(r, rows // dil, stride=dil), :]
            out_ref[r, :, cb * LANES:(cb + 1) * LANES] = piece.astype(out_ref.dtype)


def _merge_residues(in_ref, scr, dil):
    nb, rows, _ = scr.shape
    for r in range(dil):
        for cb in range(nb):
            scr.at[cb][pl.ds(r, rows // dil, stride=dil), :] = in_ref[r, :, cb * LANES:(cb + 1) * LANES].astype(F32)
    return jnp.concatenate([scr[cb] for cb in range(nb)], axis=-1)


def _col_scratch(rows, width):
    return pltpu.VMEM((width // LANES, rows, LANES), F32)


def _res_spec(dil, rows, width):
    return pl.BlockSpec((dil, rows // dil, width), lambda i: (0, i, 0))


def _res_shape(s, dil, width, dtype):
    return jax.ShapeDtypeStruct((dil, s // dil, width), dtype)


def _inproj_fwd(x, g1, win_t):
    s = x.shape[0]
    nd = len(DILS)

    def body(x_ref, g_ref, w_ref, hn_ref, *rest):
        qkv_refs = rest[:3 * nd]
        u_ref, z_ref, scr = rest[3 * nd:]
        hn, _, _ = _rms_fwd(x_ref[...], g_ref[...])
        hn = hn.astype(BF16)
        hn_ref[...] = hn
        for t in range(3):
            seg = _dot_nt(hn, w_ref[t * A:(t + 1) * A, :])
            seg = seg * SCALE if t == 0 else seg
            _fill_cols(scr, seg)
            for di, dil in enumerate(DILS):
                if dil == 1:
                    qkv_refs[t * nd + di][0] = seg.astype(BF16)
                else:
                    _split_residues(scr, qkv_refs[t * nd + di], dil)
        u_ref[...] = _dot_nt(hn, w_ref[3 * A:3 * A + GW, :])
        z_ref[...] = _dot_nt(hn, w_ref[3 * A + GW:INW, :])

    res = pl.pallas_call(
        body, name="inproj_fwd", grid=(s // TM,),
        in_specs=[_row_spec(TM, D), _const_spec((1, D)), _const_spec((INW, D))],
        out_specs=[_row_spec(TM, D)] + [_res_spec(d, TM, A) for _ in range(3) for d in DILS]
                  + [_row_spec(TM, GW), _row_spec(TM, GW)],
        out_shape=[jax.ShapeDtypeStruct((s, D), BF16)] + [_res_shape(s, d, A, BF16) for _ in range(3) for d in DILS]
                  + [jax.ShapeDtypeStruct((s, GW), F32)] * 2,
        scratch_shapes=[_col_scratch(TM, A)],
        compiler_params=_cparams("arbitrary"),
    )(x, g1, win_t)
    hn1 = res[0]
    q, k, v = (res[1 + t * nd:1 + (t + 1) * nd] for t in range(3))
    return hn1, q, k, v, res[-2], res[-1]


def _att_geometry(s, dil):
    length = s // dil
    rows = min(length, ATT_ROWS)
    return length, rows, length // rows, rows // CHUNK


def _stack_heads(t):
    lane = lax.broadcasted_iota(jnp.int32, t.shape, 1)
    zero = jnp.zeros_like(t)
    return jnp.concatenate([jnp.where(lane < DH, t, zero), jnp.where(lane >= DH, t, zero)], axis=0)


def _head_cols(t, hp):
    lane = lax.broadcasted_iota(jnp.int32, t.shape, 1)
    cols = [jnp.sum(jnp.where(lane == 2 * hp + h, t, 0.0), axis=-1, keepdims=True) for h in range(2)]
    return jnp.concatenate(cols, axis=0)


def _unstack_heads(t2):
    n = t2.shape[0] // 2
    lane = lax.broadcasted_iota(jnp.int32, (n, LANES), 1)
    return jnp.where(lane < DH, t2[:n], t2[n:])


def _query_window_bias(s0, s1, dil, first):
    row = lax.broadcasted_iota(jnp.int32, (2 * CHUNK, 2 * CHUNK), 0)
    col = lax.broadcasted_iota(jnp.int32, (2 * CHUNK, 2 * CHUNK), 1)
    steps = (row & (CHUNK - 1)) + CHUNK - col
    valid = (steps >= 0) & (steps <= CHUNK)
    if first:
        valid = valid & (col >= CHUNK)
    slope = jnp.where(row < CHUNK, s0, s1)
    return jnp.where(valid, -slope * (steps * dil).astype(F32), NEG)


def _key_block_bias(s0, s1, dil, last):
    key = lax.broadcasted_iota(jnp.int32, (CHUNK, 4 * CHUNK), 0)
    col = lax.broadcasted_iota(jnp.int32, (CHUNK, 4 * CHUNK), 1)
    wq = col & (2 * CHUNK - 1)
    steps = wq - key
    valid = (steps >= 0) & (steps <= CHUNK)
    if last:
        valid = valid & (wq < CHUNK)
    slope = jnp.where(col < 2 * CHUNK, s0, s1)
    return jnp.where(valid, -slope * (steps * dil).astype(F32), NEG)


def _head_rows(t, hp):
    row = lax.broadcasted_iota(jnp.int32, (8, LANES), 0)
    lane = lax.broadcasted_iota(jnp.int32, (8, LANES), 1)
    pick = jnp.where((row < 2) & (lane == 2 * hp + row), 1.0, 0.0).astype(BF16)
    hi = t.astype(BF16)
    rest = t - hi.astype(F32)
    mid = rest.astype(BF16)
    low = (rest - mid.astype(F32)).astype(BF16)
    return _dot_nt(pick, hi) + _dot_nt(pick, mid) + _dot_nt(pick, low)


def _att_specs(dil, rows, nsub, nblk):
    main = pl.BlockSpec((None, rows, LANES), lambda r, c, hp: (r, c, hp))
    prev = pl.BlockSpec((None, CHUNK, LANES), lambda r, c, hp: (r, jnp.maximum(c * nsub - 1, 0), hp))
    nxt = pl.BlockSpec((None, CHUNK, LANES), lambda r, c, hp: (r, jnp.minimum((c + 1) * nsub, nblk - 1), hp))
    main_heads = pl.BlockSpec((None, rows, LANES), lambda r, c, hp: (r, c, 0))
    nxt_heads = pl.BlockSpec((None, CHUNK, LANES), lambda r, c, hp: (r, jnp.minimum((c + 1) * nsub, nblk - 1), 0))
    return main, prev, nxt, main_heads, nxt_heads


def _row_start(i):
    return i * CHUNK if isinstance(i, int) else pl.multiple_of(i * CHUNK, CHUNK)


def _attn_fwd(q, k, v, slopes, dil):
    length = q.shape[1]
    _, rows, nch, nsub = _att_geometry(length * dil, dil)
    main, prev, _, main_heads, _ = _att_specs(dil, rows, nsub, length // CHUNK)

    def body(sl_ref, q_ref, k_ref, v_ref, kh_ref, vh_ref, o_ref, lse_ref, kbuf, vbuf, bias_buf):
        ch = pl.program_id(1)
        hp = pl.program_id(2)
        lane = lax.broadcasted_iota(jnp.int32, (CHUNK, LANES), 1)
        kbuf[0:CHUNK, :] = kh_ref[...]
        kbuf[CHUNK:, :] = k_ref[...]
        vbuf[0:CHUNK, :] = vh_ref[...]
        vbuf[CHUNK:, :] = v_ref[...]
        s0, s1 = sl_ref[2 * hp], sl_ref[2 * hp + 1]

        def block(i, bias):
            row = _row_start(i)
            rs = pl.ds(row, CHUNK)
            q2 = _stack_heads(q_ref[rs, :])
            kw = kbuf[pl.ds(row, 2 * CHUNK), :]
            vw = vbuf[pl.ds(row, 2 * CHUNK), :]
            sc = _dot_nt(q2, kw) + bias
            m = jnp.max(sc, axis=-1, keepdims=True)
            p = jnp.exp(sc - m)
            l = jnp.sum(p, axis=-1, keepdims=True)
            o2 = _dot(p.astype(BF16), vw) * (1.0 / l)
            o_ref[rs, :] = _unstack_heads(o2).astype(BF16)
            lse = m + jnp.log(l)
            seen = jnp.where(hp == 0, 0.0, lse_ref[rs, :])
            lse_ref[rs, :] = jnp.where(lane == 2 * hp, lse[:CHUNK], jnp.where(lane == 2 * hp + 1, lse[CHUNK:], seen))

        bias_buf[...] = _query_window_bias(s0, s1, dil, False)

        @pl.when(ch == 0)
        def _():
            block(0, _query_window_bias(s0, s1, dil, True))

        @pl.when(ch != 0)
        def _():
            block(0, bias_buf[...])

        for i in range(1, nsub):
            block(i, bias_buf[...])

    sd = jax.ShapeDtypeStruct
    return pl.pallas_call(
        body, name=f"attn_fwd_d{dil}", grid=(dil, nch, NH // 2),
        in_specs=[pl.BlockSpec(memory_space=pltpu.SMEM), main, main, main, prev, prev],
        out_specs=[main, main_heads], out_shape=[sd((dil, length, A), BF16), sd((dil, length, LANES), F32)],
        scratch_shapes=[pltpu.VMEM((rows + CHUNK, LANES), BF16), pltpu.VMEM((rows + CHUNK, LANES), BF16),
                        pltpu.VMEM((2 * CHUNK, 2 * CHUNK), F32)],
        compiler_params=_cparams("arbitrary", "arbitrary", "arbitrary"),
    )(slopes, q, k, v, k, v)


def _attn_bwd_dq(q, k, v, do, lse, delta, slopes, dil):
    length = q.shape[1]
    _, rows, nch, nsub = _att_geometry(length * dil, dil)
    main, prev, _, main_heads, _ = _att_specs(dil, rows, nsub, length // CHUNK)

    def body(sl_ref, q_ref, k_ref, v_ref, do_ref, lse_ref, dl_ref, kh_ref, vh_ref, dq_ref, kbuf, vbuf, bias_buf):
        ch = pl.program_id(1)
        hp = pl.program_id(2)
        kbuf[0:CHUNK, :] = kh_ref[...]
        kbuf[CHUNK:, :] = k_ref[...]
        vbuf[0:CHUNK, :] = vh_ref[...]
        vbuf[CHUNK:, :] = v_ref[...]
        s0, s1 = sl_ref[2 * hp], sl_ref[2 * hp + 1]

        def block(i, bias):
            row = _row_start(i)
            rs = pl.ds(row, CHUNK)
            q2 = _stack_heads(q_ref[rs, :])
            do2 = _stack_heads(do_ref[rs, :])
            lse2 = _head_cols(lse_ref[rs, :], hp)
            dl2 = _head_cols(dl_ref[rs, :], hp)
            kw = kbuf[pl.ds(row, 2 * CHUNK), :]
            vw = vbuf[pl.ds(row, 2 * CHUNK), :]
            p = jnp.exp(_dot_nt(q2, kw) + bias - lse2)
            ds = p * (_dot_nt(do2, vw) - dl2)
            dq_ref[rs, :] = _unstack_heads(_dot(ds.astype(BF16), kw)).astype(BF16)

        bias_buf[...] = _query_window_bias(s0, s1, dil, False)

        @pl.when(ch == 0)
        def _():
            block(0, _query_window_bias(s0, s1, dil, True))

        @pl.when(ch != 0)
        def _():
            block(0, bias_buf[...])

        for i in range(1, nsub):
            block(i, bias_buf[...])

    return pl.pallas_call(
        body, name=f"attn_dq_d{dil}", grid=(dil, nch, NH // 2),
        in_specs=[pl.BlockSpec(memory_space=pltpu.SMEM), main, main, main, main, main_heads, main_heads, prev, prev],
        out_specs=main, out_shape=jax.ShapeDtypeStruct((dil, length, A), BF16),
        scratch_shapes=[pltpu.VMEM((rows + CHUNK, LANES), BF16), pltpu.VMEM((rows + CHUNK, LANES), BF16),
                        pltpu.VMEM((2 * CHUNK, 2 * CHUNK), F32)],
        compiler_params=_cparams("arbitrary", "arbitrary", "arbitrary"),
    )(slopes, q, k, v, do, lse, delta, k, v)


def _attn_bwd_dkv(q, k, v, do, lse, delta, slopes, dil):
    length = q.shape[1]
    _, rows, nch, nsub = _att_geometry(length * dil, dil)
    main, _, nxt, main_heads, nxt_heads = _att_specs(dil, rows, nsub, length // CHUNK)

    def body(sl_ref, k_ref, v_ref, q_ref, do_ref, lse_ref, dl_ref, qh_ref, doh_ref, lseh_ref, dlh_ref,
             dk_ref, dv_ref, qbuf, dobuf, lse_rows, dl_rows, bias_buf):
        ch = pl.program_id(1)
        hp = pl.program_id(2)
        for buf, main_ref, halo_ref in ((qbuf, q_ref, qh_ref), (dobuf, do_ref, doh_ref)):
            buf[0:rows, :] = main_ref[...]
            buf[rows:, :] = halo_ref[...]
        for buf, main_ref, halo_ref in ((lse_rows, lse_ref, lseh_ref), (dl_rows, dl_ref, dlh_ref)):
            buf[:, 0:rows] = _head_rows(main_ref[...], hp)
            buf[:, rows:] = _head_rows(halo_ref[...], hp)
        s0, s1 = sl_ref[2 * hp], sl_ref[2 * hp + 1]

        def block(i, bias):
            row = _row_start(i)
            rs = pl.ds(row, CHUNK)
            win = pl.ds(row, 2 * CHUNK)
            kc = k_ref[rs, :]
            vc = v_ref[rs, :]
            q2 = _stack_heads(qbuf[win, :])
            do2 = _stack_heads(dobuf[win, :])
            cols = slice(i * CHUNK, (i + 2) * CHUNK)
            lse2 = jnp.concatenate([lse_rows[0:1, cols], lse_rows[1:2, cols]], axis=1)
            dl2 = jnp.concatenate([dl_rows[0:1, cols], dl_rows[1:2, cols]], axis=1)
            pt = jnp.exp(_dot_nt(kc, q2) + bias - lse2)
            dst = pt * (_dot_nt(vc, do2) - dl2)
            dv_ref[rs, :] = _dot(pt.astype(BF16), do2).astype(BF16)
            dk_ref[rs, :] = _dot(dst.astype(BF16), q2).astype(BF16)

        bias_buf[...] = _key_block_bias(s0, s1, dil, False)

        for i in range(nsub - 1):
            block(i, bias_buf[...])

        @pl.when(ch != nch - 1)
        def _():
            block(nsub - 1, bias_buf[...])

        @pl.when(ch == nch - 1)
        def _():
            block(nsub - 1, _key_block_bias(s0, s1, dil, True))

    sd = jax.ShapeDtypeStruct((dil, length, A), BF16)
    return pl.pallas_call(
        body, name=f"attn_dkv_d{dil}", grid=(dil, nch, NH // 2),
        in_specs=[pl.BlockSpec(memory_space=pltpu.SMEM), main, main, main, main, main_heads, main_heads,
                  nxt, nxt, nxt_heads, nxt_heads],
        out_specs=[main, main], out_shape=[sd, sd],
        scratch_shapes=[pltpu.VMEM((rows + CHUNK, LANES), BF16), pltpu.VMEM((rows + CHUNK, LANES), BF16),
                        pltpu.VMEM((8, rows + CHUNK), F32), pltpu.VMEM((8, rows + CHUNK), F32),
                        pltpu.VMEM((CHUNK, 4 * CHUNK), F32)],
        compiler_params=_cparams("arbitrary", "arbitrary", "arbitrary"),
    )(slopes, k, v, q, do, lse, delta, q, do, lse, delta)


def _group_masks(width):
    lane = lax.broadcasted_iota(jnp.int32, (1, width), 1)
    return [(lane >= g * DH) & (lane < (g + 1) * DH) for g in range(width // DH)]


def _group_mean_matrix():
    i = lax.broadcasted_iota(jnp.int32, (GW, GW), 0) // DH
    j = lax.broadcasted_iota(jnp.int32, (GW, GW), 1) // DH
    return jnp.where(i == j, 1.0 / DH, 0.0).astype(F32)


def _tri_mask(lower):
    t = lax.broadcasted_iota(jnp.int32, (CHUNK, CHUNK), 0)
    u = lax.broadcasted_iota(jnp.int32, (CHUNK, CHUNK), 1)
    return (u <= t) if lower else (u >= t)


def _sgu_forward(u, z, lng, lnb, w_ref, bias_t, pmat, rows):
    ug = _gelu(u)
    zg = _gelu(z)
    mu = _dot_hi(zg, pmat)
    zc = zg - mu
    var = _dot_hi(zc * zc, pmat)
    rstd = lax.rsqrt(var + EPS)
    zhat = zc * rstd
    zn = (zhat * lng + lnb).astype(BF16)
    gm = _group_masks(GW)
    tri = _tri_mask(True)
    ws = [jnp.where(tri, w_ref[g], 0.0).astype(BF16) for g in range(NG)]
    pieces = []
    for c in range(rows // CHUNK):
        znc = zn[c * CHUNK:(c + 1) * CHUNK, :]
        mix = None
        for g in range(NG):
            part = jnp.where(gm[g], _dot(ws[g], znc), 0.0)
            mix = part if mix is None else mix + part
        pieces.append(mix + bias_t)
    mixed = jnp.concatenate(pieces, axis=0) if len(pieces) > 1 else pieces[0]
    return ug * mixed, ug, zhat, rstd, zn, mixed


def _head_spread():
    h = lax.broadcasted_iota(jnp.int32, (LANES, A), 0)
    lane = lax.broadcasted_iota(jnp.int32, (LANES, A), 1)
    return jnp.where(lane // DH == h, 1.0, 0.0).astype(BF16)


def _bf16_pieces(t, n):
    pieces = []
    for _ in range(n):
        piece = t.astype(BF16)
        pieces.append(piece)
        t = t - piece.astype(F32)
    return pieces


def _mix_fwd(os_, ls_, u, z, x, lng, lnb, sgu_w, bias_t, ga, gg, wout):
    s = x.shape[0]
    nd = len(DILS)
    nscr = sum(1 for d in DILS if d > 1)

    def body(*refs):
        o_refs, l_refs = refs[:nd], refs[nd:2 * nd]
        u_ref, z_ref, x_ref, lng_ref, lnb_ref, w_ref, bt_ref, ga_ref, gg_ref, wo_ref = refs[2 * nd:2 * nd + 10]
        attn_ref = refs[2 * nd + 10]
        lse_refs = refs[2 * nd + 11:3 * nd + 11]
        mixed_ref, h1_ref = refs[3 * nd + 11:3 * nd + 13]
        scr = refs[3 * nd + 13:]
        scr_o, scr_l, scr_lse = scr[:nscr], scr[nscr:2 * nscr], scr[2 * nscr]
        ov, lv, j = [], [], 0
        for di, dil in enumerate(DILS):
            if dil == 1:
                ov.append(o_refs[di][0].astype(F32))
                lv.append(l_refs[di][0])
            else:
                ov.append(_merge_residues(o_refs[di], scr_o[j], dil))
                lv.append(_merge_residues(l_refs[di], scr_l[j], dil))
                j += 1
        mx = functools.reduce(jnp.maximum, lv)
        es = [jnp.exp(l - mx) for l in lv]
        den = functools.reduce(lambda a, b: a + b, es)
        spread = _head_spread()
        attn = None
        for e, o in zip(es, ov):
            wide = functools.reduce(lambda a, b: a + b, [_dot(piece, spread) for piece in _bf16_pieces(e / den, 2)])
            attn = wide * o if attn is None else attn + wide * o
        attn_ref[...] = attn
        lse = mx + jnp.log(den)
        _fill_cols(scr_lse, lse)
        for di, dil in enumerate(DILS):
            if dil == 1:
                lse_refs[di][0] = lse
            else:
                _split_residues(scr_lse, lse_refs[di], dil)
        an, _, _ = _rms_fwd(attn, ga_ref[...])
        gmv, _, _, _, _, _ = _sgu_forward(u_ref[...], z_ref[...], lng_ref[...], lnb_ref[...], w_ref,
                                          bt_ref[...], _group_mean_matrix(), TMX)
        gn, _, _ = _rms_fwd(gmv, gg_ref[...])
        mixed = jnp.concatenate([an, gn], axis=-1).astype(BF16)
        mixed_ref[...] = mixed
        h1_ref[...] = x_ref[...] + _dot(mixed, wo_ref[...])

    sd = jax.ShapeDtypeStruct
    res = pl.pallas_call(
        body, name="mix_fwd", grid=(s // TMX,),
        in_specs=[_res_spec(d, TMX, A) for d in DILS] + [_res_spec(d, TMX, LANES) for d in DILS]
                 + [_row_spec(TMX, GW), _row_spec(TMX, GW),
                    _row_spec(TMX, D), _const_spec((1, GW)), _const_spec((1, GW)), _const_spec((NG, CHUNK, CHUNK)),
                    _const_spec((CHUNK, GW)), _const_spec((1, A)), _const_spec((1, GW)), _const_spec((D, D))],
        out_specs=[_row_spec(TMX, A)] + [_res_spec(d, TMX, LANES) for d in DILS]
                  + [_row_spec(TMX, D), _row_spec(TMX, D)],
        out_shape=[sd((s, A), F32)] + [_res_shape(s, d, LANES, F32) for d in DILS]
                  + [sd((s, D), BF16), sd((s, D), F32)],
        scratch_shapes=[_col_scratch(TMX, A)] * nscr + [_col_scratch(TMX, LANES)] * (nscr + 1),
        compiler_params=_cparams("arbitrary"),
    )(*os_, *ls_, u, z, x, lng, lnb, sgu_w, bias_t, ga, gg, wout)
    return res[0], res[1:1 + nd], res[1 + nd], res[2 + nd]


def _mlp_fwd(h1, g2, wff1_t, wff2, gf, target):
    s = h1.shape[0]

    def body(h1_ref, g2_ref, w1_ref, w2_ref, gf_ref, t_ref, hn_ref, rf_ref, dh2_ref, loss_ref, dgf_ref):
        i = pl.program_id(0)
        h1v = h1_ref[...]
        hn, _, _ = _rms_fwd(h1v, g2_ref[...])
        hn = hn.astype(BF16)
        hn_ref[...] = hn
        acc = h1v
        for j in range(DFF // FF_CH):
            cols = slice(j * FF_CH, (j + 1) * FF_CH)
            rf = jnp.maximum(_dot_nt(hn, w1_ref[cols, :]), 0.0)
            act = (rf * rf).astype(BF16)
            rf_ref[:, cols] = rf.astype(BF16)
            acc = acc + _dot(act, w2_ref[cols, :])
        y, h2n, r3 = _rms_fwd(acc, gf_ref[...])
        err = y - t_ref[...]
        part = 0.5 * jnp.sum(jnp.mean(err * err, axis=-1, keepdims=True), axis=0, keepdims=True)
        dy = err * (1.0 / D)
        dh2, dgf = _rms_bwd(dy, h2n, r3, gf_ref[...])
        dh2_ref[...] = dh2

        @pl.when(i == 0)
        def _():
            loss_ref[...] = jnp.zeros_like(loss_ref)
            dgf_ref[...] = jnp.zeros_like(dgf_ref)

        loss_ref[...] += jnp.broadcast_to(part, loss_ref.shape)
        dgf_ref[...] += dgf

    sd = jax.ShapeDtypeStruct
    return pl.pallas_call(
        body, name="mlp_fwd", grid=(s // TM,),
        in_specs=[_row_spec(TM, D), _const_spec((1, D)), _const_spec((DFF, D)), _const_spec((DFF, D)),
                  _const_spec((1, D)), _row_spec(TM, D)],
        out_specs=[_row_spec(TM, D), _row_spec(TM, DFF), _row_spec(TM, D),
                   _const_spec((1, LANES)), _const_spec((1, D))],
        out_shape=[sd((s, D), BF16), sd((s, DFF), BF16), sd((s, D), F32),
                   sd((1, LANES), F32), sd((1, D), F32)],
        compiler_params=_cparams("arbitrary"),
    )(h1, g2, wff1_t, wff2, gf, target)


def _mlp_bwd(dh2, rf, h1, g2, wff1_t, wff2):
    s = h1.shape[0]

    def body(dh2_ref, rf_ref, h1_ref, g2_ref, w1_ref, w2_ref, df_ref, dh1_ref, dg2_ref):
        i = pl.program_id(0)
        dh2v = dh2_ref[...]
        dh2b = dh2v.astype(BF16)
        dhn = jnp.zeros((TM, D), F32)
        for j in range(DFF // FF_CH):
            cols = slice(j * FF_CH, (j + 1) * FF_CH)
            da = _dot_nt(dh2b, w2_ref[cols, :])
            df = (da * (2.0 * rf_ref[:, cols].astype(F32))).astype(BF16)
            df_ref[:, cols] = df
            dhn = dhn + _dot(df, w1_ref[cols, :])
        _, h1n, r2 = _rms_fwd(h1_ref[...], g2_ref[...])
        dres, dg2 = _rms_bwd(dhn, h1n, r2, g2_ref[...])
        dh1_ref[...] = dh2v + dres

        @pl.when(i == 0)
        def _():
            dg2_ref[...] = jnp.zeros_like(dg2_ref)

        dg2_ref[...] += dg2

    sd = jax.ShapeDtypeStruct
    return pl.pallas_call(
        body, name="mlp_bwd", grid=(s // TM,),
        in_specs=[_row_spec(TM, D), _row_spec(TM, DFF), _row_spec(TM, D), _const_spec((1, D)),
                  _const_spec((DFF, D)), _const_spec((DFF, D))],
        out_specs=[_row_spec(TM, DFF), _row_spec(TM, D), _const_spec((1, D))],
        out_shape=[sd((s, DFF), BF16), sd((s, D), F32), sd((1, D), F32)],
        compiler_params=_cparams("arbitrary"),
    )(dh2, rf, h1, g2, wff1_t, wff2)


def _mix_bwd(dh1, attn, u, z, lng, lnb, sgu_w, sgu_wt, bias_t, ga, gg, wout):
    s = dh1.shape[0]
    nsteps = s // TMX
    nd = len(DILS)

    def body(*refs):
        dh1_ref, attn_ref, u_ref, z_ref, lng_ref, lnb_ref, w_ref, wt_ref, bt_ref, ga_ref, gg_ref, wo_ref = refs[:12]
        do_refs, dl_refs = refs[12:12 + nd], refs[12 + nd:12 + 2 * nd]
        (du_ref, dz_ref, dga_ref, dgg_ref, dlng_ref, dlnb_ref, dws_ref, db_ref,
         dbt_acc, scr_do, scr_dl) = refs[12 + 2 * nd:]
        i = pl.program_id(0)

        @pl.when(i == 0)
        def _():
            for r in (dga_ref, dgg_ref, dlng_ref, dlnb_ref, dws_ref, db_ref, dbt_acc):
                r[...] = jnp.zeros_like(r)

        dmixed = _dot_nt(dh1_ref[...].astype(BF16), wo_ref[...])
        attn = attn_ref[...]
        _, an, ra = _rms_fwd(attn, ga_ref[...])
        dattn, dga = _rms_bwd(dmixed[:, :A], an, ra, ga_ref[...])
        dga_ref[...] += dga
        _fill_cols(scr_do, dattn)
        spread = _head_spread()
        delta = functools.reduce(lambda a, b: a + b, [_dot_nt(piece, spread) for piece in _bf16_pieces(dattn * attn, 3)])
        _fill_cols(scr_dl, delta)
        for di, dil in enumerate(DILS):
            if dil == 1:
                do_refs[di][0] = dattn.astype(BF16)
                dl_refs[di][0] = delta
            else:
                _split_residues(scr_do, do_refs[di], dil)
                _split_residues(scr_dl, dl_refs[di], dil)
        pmat = _group_mean_matrix()
        lng = lng_ref[...]
        uv, zv = u_ref[...], z_ref[...]
        gmv, ug, zhat, rstd, zn, mixed = _sgu_forward(uv, zv, lng, lnb_ref[...], w_ref, bt_ref[...], pmat, TMX)
        _, gmn, rg = _rms_fwd(gmv, gg_ref[...])
        dgm, dgg = _rms_bwd(dmixed[:, A:], gmn, rg, gg_ref[...])
        dgg_ref[...] += dgg
        du_ref[...] = dgm * mixed * _gelu_grad(uv)
        dmx = dgm * ug
        dmxb = dmx.astype(BF16)
        gm = _group_masks(GW)
        tri_t = _tri_mask(False)
        wst = [jnp.where(tri_t, wt_ref[g], 0.0).astype(BF16) for g in range(NG)]
        zero = jnp.zeros((CHUNK, GW), BF16)
        dzn_pieces = []
        for c in range(TMX // CHUNK):
            rs = slice(c * CHUNK, (c + 1) * CHUNK)
            dmc = dmxb[rs, :]
            znc = zn[rs, :]
            dbt_acc[...] += dmx[rs, :]
            dzn = None
            for g in range(NG):
                dws_ref[g] += _dot_nt(jnp.where(gm[g], dmc, zero), znc)
                part = jnp.where(gm[g], _dot(wst[g], dmc), 0.0)
                dzn = part if dzn is None else dzn + part
            dzn_pieces.append(dzn)
        dzn = jnp.concatenate(dzn_pieces, axis=0)
        dlng_ref[...] += jnp.sum(dzn * zhat, axis=0, keepdims=True)
        dlnb_ref[...] += jnp.sum(dzn, axis=0, keepdims=True)
        dzh = dzn * lng
        dzg = rstd * (dzh - _dot_hi(dzh, pmat) - zhat * _dot_hi(dzh * zhat, pmat))
        dz_ref[...] = dzg * _gelu_grad(zv)

        @pl.when(i == nsteps - 1)
        def _():
            tri = _tri_mask(True)
            for g in range(NG):
                dws_ref[g] = jnp.where(tri, dws_ref[g], 0.0)
            acc = dbt_acc[...]
            lane = lax.broadcasted_iota(jnp.int32, (CHUNK, LANES), 1)
            out = jnp.zeros((CHUNK, LANES), F32)
            for g in range(NG):
                sg = jnp.sum(jnp.where(gm[g], acc, 0.0), axis=-1, keepdims=True)
                out = jnp.where(lane == g, sg, out)
            db_ref[...] = out

    sd = jax.ShapeDtypeStruct
    res = pl.pallas_call(
        body, name="mix_bwd", grid=(nsteps,),
        in_specs=[_row_spec(TMX, D), _row_spec(TMX, A), _row_spec(TMX, GW), _row_spec(TMX, GW),
                  _const_spec((1, GW)), _const_spec((1, GW)), _const_spec((NG, CHUNK, CHUNK)),
                  _const_spec((NG, CHUNK, CHUNK)), _const_spec((CHUNK, GW)), _const_spec((1, A)),
                  _const_spec((1, GW)), _const_spec((D, D))],
        out_specs=[_res_spec(d, TMX, A) for d in DILS] + [_res_spec(d, TMX, LANES) for d in DILS]
                  + [_row_spec(TMX, GW), _row_spec(TMX, GW),
                   _const_spec((1, A)), _const_spec((1, GW)), _const_spec((1, GW)), _const_spec((1, GW)),
                   _const_spec((NG, CHUNK, CHUNK)), _const_spec((CHUNK, LANES))],
        out_shape=[_res_shape(s, d, A, BF16) for d in DILS] + [_res_shape(s, d, LANES, F32) for d in DILS]
                  + [sd((s, GW), F32), sd((s, GW), F32),
                   sd((1, A), F32), sd((1, GW), F32), sd((1, GW), F32), sd((1, GW), F32),
                   sd((NG, CHUNK, CHUNK), F32), sd((CHUNK, LANES), F32)],
        scratch_shapes=[pltpu.VMEM((CHUNK, GW), F32), _col_scratch(TMX, A), _col_scratch(TMX, LANES)],
        compiler_params=_cparams("arbitrary"),
    )(dh1, attn, u, z, lng, lnb, sgu_w, sgu_wt, bias_t, ga, gg, wout)
    return (res[:nd], res[nd:2 * nd]) + tuple(res[2 * nd:])


def _dproj_merge(dqs, dks, dvs, du, dz, pin):
    s = du.shape[0]
    nd = len(DILS)
    nscr = sum(1 for d in DILS if d > 1)

    def body(*refs):
        pin_ref = refs[0]
        parts = [refs[1 + t * nd:1 + (t + 1) * nd] for t in range(3)]
        du_ref, dz_ref, dp_ref = refs[1 + 3 * nd:4 + 3 * nd]
        scr = refs[4 + 3 * nd:]
        sums = []
        for t in range(3):
            total, j = None, 0
            for di, dil in enumerate(DILS):
                if dil == 1:
                    term = parts[t][di][0].astype(F32)
                else:
                    term = _merge_residues(parts[t][di], scr[t * nscr + j], dil)
                    j += 1
                total = term if total is None else total + term
            sums.append(total)
        dp_ref[...] = jnp.concatenate([sums[0] * SCALE, sums[1], sums[2], du_ref[...] + pin_ref[0, 0], dz_ref[...]],
                                      axis=-1).astype(BF16)

    return pl.pallas_call(
        body, name="dproj_merge", grid=(s // TMX,),
        in_specs=[pl.BlockSpec(memory_space=pltpu.SMEM)] + [_res_spec(d, TMX, A) for d in DILS] * 3
                 + [_row_spec(TMX, GW)] * 2,
        out_specs=_row_spec(TMX, INW), out_shape=jax.ShapeDtypeStruct((s, INW), BF16),
        scratch_shapes=[_col_scratch(TMX, A)] * (3 * nscr),
        compiler_params=_cparams("arbitrary"),
    )(pin, *dqs, *dks, *dvs, du, dz)


def _inproj_bwd(dproj, dh1, x, g1, win_t):
    s = x.shape[0]

    def body(dp_ref, dh1_ref, x_ref, g_ref, w_ref, dx_ref, dg_ref):
        i = pl.program_id(0)
        dhn = _dot(dp_ref[...], w_ref[...])
        _, xn, r1 = _rms_fwd(x_ref[...], g_ref[...])
        dres, dg = _rms_bwd(dhn, xn, r1, g_ref[...])
        dx_ref[...] = dh1_ref[...] + dres

        @pl.when(i == 0)
        def _():
            dg_ref[...] = jnp.zeros_like(dg_ref)

        dg_ref[...] += dg

    sd = jax.ShapeDtypeStruct
    return pl.pallas_call(
        body, name="inproj_bwd", grid=(s // TM,),
        in_specs=[_row_spec(TM, INW), _row_spec(TM, D), _row_spec(TM, D), _const_spec((1, D)), _const_spec((INW, D))],
        out_specs=[_row_spec(TM, D), _const_spec((1, D))],
        out_shape=[sd((s, D), F32), sd((1, D), F32)],
        compiler_params=_cparams("arbitrary"),
    )(dproj, dh1, x, g1, win_t)


def _wgrad(a, b, name, bm, bn, bk=2 * TM, square_a=False):
    s, m = a.shape
    n = b.shape[1]
    bm, bn = min(bm, m), min(bn, n)

    def body(a_ref, b_ref, o_ref):
        @pl.when(pl.program_id(2) == 0)
        def _():
            o_ref[...] = jnp.zeros_like(o_ref)

        av = a_ref[...]
        if square_a:
            av = av.astype(F32)
            av = av * av
        o_ref[...] += _dot_tn(av.astype(BF16), b_ref[...].astype(BF16))

    return pl.pallas_call(
        body, name=name, grid=(m // bm, n // bn, s // bk),
        in_specs=[pl.BlockSpec((bk, bm), lambda i, j, k: (k, i)), pl.BlockSpec((bk, bn), lambda i, j, k: (k, j))],
        out_specs=pl.BlockSpec((bm, bn), lambda i, j, k: (i, j)),
        out_shape=jax.ShapeDtypeStruct((m, n), F32),
        compiler_params=_cparams("arbitrary", "arbitrary", "arbitrary"),
    )(a, b)


def _adamw_math(w, g, m, v):
    m = B1 * m + (1.0 - B1) * g
    v = B2 * v + (1.0 - B2) * (g * g)
    m_hat = m / (1.0 - B1 ** STEP)
    v_hat = v / (1.0 - B2 ** STEP)
    delta = -LR * (m_hat / (jnp.sqrt(v_hat) + AEPS) + WD * w)
    return delta, m, v


def _adamw(w, g, m, v, name):
    rows, cols = w.shape
    br = min(rows, 256)
    while rows % br:
        br -= 8

    def body(w_ref, g_ref, m_ref, v_ref, d_ref, mo_ref, vo_ref):
        d, mn, vn = _adamw_math(w_ref[...], g_ref[...], m_ref[...], v_ref[...])
        d_ref[...] = d
        mo_ref[...] = mn
        vo_ref[...] = vn

    spec = _row_spec(br, cols)
    sd = jax.ShapeDtypeStruct((rows, cols), F32)
    return pl.pallas_call(
        body, name=name, grid=(rows // br,), in_specs=[spec] * 4, out_specs=[spec] * 3,
        out_shape=[sd, sd, sd], compiler_params=_cparams("arbitrary"),
    )(w, g, m, v)


def _local_step(x, target, small, win_t, rest_weights, early_grads=None, after_attention_bwd=None,
                late_grads=None):
    slopes = jnp.asarray(_alibi_slopes(NH))
    hn1, q, k, v, u, z = _inproj_fwd(x, small["norm1_g"], win_t)
    outs, lses = [], []
    for i, dil in enumerate(DILS):
        o, l = _attn_fwd(q[i], k[i], v[i], slopes, dil)
        outs.append(o)
        lses.append(l)
    wout, wff1_t, wff2 = rest_weights(lses[-1])
    attn, lse, mixed, h1 = _mix_fwd(outs, lses, u, z, x, small["ln_g"], small["ln_b"], small["sgu_w"],
                                    small["bias_t"], small["attn_out_g"], small["gmlp_out_g"], wout)
    hn2, rf, dh2, loss, dgf = _mlp_fwd(h1, small["norm2_g"], wff1_t, wff2, small["final_norm_g"], target)
    df, dh1, dg2 = _mlp_bwd(dh2, rf, h1, small["norm2_g"], wff1_t, wff2)
    gwff1_t = _wgrad(df, hn2, "wgrad_ff1", 1024, D)
    gwff2 = _wgrad(rf, dh2, "wgrad_ff2", 1024, D, square_a=True)
    gwout = _wgrad(mixed, dh1, "wgrad_out", D, D)
    ga, g1 = small["attn_out_g"], small["norm1_g"]
    pin = early_grads(gwff1_t, gwff2, gwout) if early_grads else None
    if pin is not None:
        ga = ga + pin
    (do, delta, du, dz, dga, dgg, dlng, dlnb, dws, db) = _mix_bwd(
        dh1, attn, u, z, small["ln_g"], small["ln_b"], small["sgu_w"], small["sgu_wt"], small["bias_t"],
        ga, small["gmlp_out_g"], wout)
    dqs, dks, dvs = [], [], []
    for i, dil in enumerate(DILS):
        dqs.append(_attn_bwd_dq(q[i], k[i], v[i], do[i], lse[i], delta[i], slopes, dil))
        dk, dv = _attn_bwd_dkv(q[i], k[i], v[i], do[i], lse[i], delta[i], slopes, dil)
        dks.append(dk)
        dvs.append(dv)
    marker = functools.reduce(lambda a, b: a + b, [t[0, 0:8, 0:LANES] for t in dqs + dks + dvs])
    pin = after_attention_bwd(marker) if after_attention_bwd else None
    dproj = _dproj_merge(dqs, dks, dvs, du, dz, jnp.zeros((1, 1), F32) if pin is None else pin)
    gwin_t = _wgrad(dproj, hn1, "wgrad_in", INW // 2, D)
    pin = late_grads(gwin_t) if late_grads else None
    if pin is not None:
        g1 = g1 + pin
    dx, dg1 = _inproj_bwd(dproj, dh1, x, g1, win_t)
    small_grads = dict(norm1_g=dg1, ln_g=dlng, ln_b=dlnb, sgu_w=dws, sgu_b=db[:, :NG].T,
                       attn_out_g=dga, gmlp_out_g=dgg, norm2_g=dg2, final_norm_g=dgf)
    return loss[0, 0], dx, small_grads, (gwin_t, gwout, gwff1_t, gwff2)


ANY = pl.BlockSpec(memory_space=pl.ANY)
NDEV = 8


def _position():
    return lax.axis_index("x"), lax.axis_index("y"), lax.axis_index("c")


def _other_chips(x, y):
    return [(1 - x, y), (x, 1 - y), (1 - x, 1 - y)]


def _remote(src, dst, send_sem, recv_sem, device):
    return pltpu.make_async_remote_copy(src_ref=src, dst_ref=dst, send_sem=send_sem, recv_sem=recv_sem,
                                        device_id=device, device_id_type=MESH)


HBM = pl.BlockSpec(memory_space=pltpu.HBM)
SEM = pl.BlockSpec(memory_space=pltpu.SEMAPHORE)
DATAFLOW = pltpu.SideEffectType.DATAFLOW_SIDE_EFFECTING


def _in_hbm(a):
    return pltpu.with_memory_space_constraint(a, pltpu.HBM)


def _gather_start(shard, name):
    def body(w_ref, land_ref, send_sems, recv_sems, w_thru, land_thru, token):
        x, y, c = _position()
        for k, (px, py) in enumerate(_other_chips(x, y)):
            _remote(w_ref, land_ref.at[2 * x + y], send_sems.at[k], recv_sems.at[k], (px, py, c)).start()
        token[...] = jnp.zeros_like(token)

    land = jnp.broadcast_to(shard[None], (NCHIP,) + shard.shape)
    return pl.pallas_call(
        body, name=name,
        out_shape=(pltpu.SemaphoreType.DMA((3,)), pltpu.SemaphoreType.DMA((3,)),
                   pltpu.HBM(shard.shape, shard.dtype), pltpu.HBM(land.shape, land.dtype),
                   jax.ShapeDtypeStruct((8, LANES), F32)),
        in_specs=(HBM, HBM), out_specs=(SEM, SEM, HBM, HBM, pl.BlockSpec(memory_space=pltpu.VMEM)),
        input_output_aliases={0: 2, 1: 3},
        compiler_params=pltpu.CompilerParams(has_side_effects=DATAFLOW),
    )(_in_hbm(shard), _in_hbm(land))


def _gather_wait(send_sems, recv_sems, w_thru, land_thru, after, name):
    def body(w_ref, land_ref, send_sems, recv_sems, after_ref, w_dead, got_ref):
        x, y, c = _position()
        for k, (px, py) in enumerate(_other_chips(x, y)):
            cp = _remote(w_ref, land_ref.at[2 * px + py], send_sems.at[k], recv_sems.at[k], (px, py, c))
            cp.wait_send()
            cp.wait_recv()

    return pl.pallas_call(
        body, name=name,
        out_shape=(pltpu.HBM(w_thru.shape, w_thru.dtype), pltpu.HBM(land_thru.shape, land_thru.dtype)),
        in_specs=(HBM, HBM, SEM, SEM, ANY), out_specs=(HBM, HBM),
        input_output_aliases={0: 0, 1: 1},
        compiler_params=pltpu.CompilerParams(has_side_effects=DATAFLOW),
    )(w_thru, land_thru, send_sems, recv_sems, after)[1]


def _xor_peers(x, y, c):
    peers = []
    for k in range(1, NDEV):
        kx, ky, kc = (k >> 2) & 1, (k >> 1) & 1, k & 1
        peers.append((1 - x if kx else x, 1 - y if ky else y, 1 - c if kc else c))
    return peers


def _piece(part_ref, px, py, pc):
    half = part_ref.shape[1] // 2
    return part_ref.at[2 * px + py, pl.ds(pc * half, half), :]


def _split_call(body, name, operands, n_sems, extra_out=()):
    n = len(operands)
    sems = tuple(pltpu.SemaphoreType.DMA((m,)) for m in n_sems)
    thru = tuple(pltpu.HBM(a.shape, a.dtype) for a in operands)
    return pl.pallas_call(
        body, name=name, out_shape=sems + thru + tuple(extra_out),
        in_specs=(HBM,) * n,
        out_specs=(SEM,) * len(sems) + (HBM,) * n + (pl.BlockSpec(memory_space=pltpu.VMEM),) * len(extra_out),
        input_output_aliases={i: len(sems) + i for i in range(n)},
        compiler_params=pltpu.CompilerParams(has_side_effects=DATAFLOW),
    )(*[_in_hbm(a) for a in operands])


TOKEN = jax.ShapeDtypeStruct((8, LANES), F32)


def _reduce_start(parts, name):
    nw = len(parts)
    lands = [lax.empty((NDEV - 1, p.shape[1] // 2, D), F32) for p in parts]

    def body(*refs):
        part_refs, land_refs = refs[:nw], refs[nw:2 * nw]
        send_sems, recv_sems = refs[2 * nw:2 * nw + 2]
        token = refs[-1]
        x, y, c = _position()
        for w in range(nw):
            for k, peer in enumerate(_xor_peers(x, y, c)):
                n = w * (NDEV - 1) + k
                _remote(_piece(part_refs[w], *peer), land_refs[w].at[k], send_sems.at[n], recv_sems.at[n],
                        peer).start()
        token[...] = jnp.zeros_like(token)

    n = nw * (NDEV - 1)
    res = _split_call(body, name, list(parts) + lands, (n, n), (TOKEN,))
    return res[0], res[1], res[2:2 + nw], res[2 + nw:2 + 2 * nw], res[-1]


def _reduce_wait(send_sems, recv_sems, parts, lands, after, name):
    nw = len(parts)

    def body(*refs):
        part_refs, land_refs = refs[:nw], refs[nw:2 * nw]
        send_sems, recv_sems = refs[2 * nw:2 * nw + 2]
        x, y, c = _position()
        for w in range(nw):
            for k, peer in enumerate(_xor_peers(x, y, c)):
                n = w * (NDEV - 1) + k
                cp = _remote(_piece(part_refs[w], *peer), land_refs[w].at[k], send_sems.at[n], recv_sems.at[n], peer)
                cp.wait_send()
                cp.wait_recv()

    operands = list(parts) + list(lands)
    res = pl.pallas_call(
        body, name=name, out_shape=tuple(pltpu.HBM(a.shape, a.dtype) for a in operands),
        in_specs=(HBM,) * (2 * nw) + (SEM, SEM, ANY), out_specs=(HBM,) * (2 * nw),
        input_output_aliases={i: i for i in range(2 * nw)},
        compiler_params=pltpu.CompilerParams(has_side_effects=DATAFLOW),
    )(*operands, send_sems, recv_sems, after)
    return res[:nw], res[nw:]


def _sum_pieces(part, land, sel, name):
    half = part.shape[1] // 2
    br = 128 if half % 128 == 0 else half // 2
    nb = half // br

    def body(sel_ref, own_ref, *refs):
        acc = own_ref[...]
        for r in refs[:NDEV - 1]:
            acc = acc + r[...]
        refs[NDEV - 1][...] = acc

    own_spec = pl.BlockSpec((None, br, D), lambda i, sel_ref: (sel_ref[0], sel_ref[1] * nb + i, 0))
    slot_specs = [pl.BlockSpec((None, br, D), functools.partial(lambda i, sel_ref, k: (k, i, 0), k=k))
                  for k in range(NDEV - 1)]
    return pl.pallas_call(
        body, name=name,
        grid_spec=pltpu.PrefetchScalarGridSpec(
            num_scalar_prefetch=1, grid=(nb,), in_specs=[own_spec] + slot_specs,
            out_specs=pl.BlockSpec((br, D), lambda i, sel_ref: (i, 0))),
        out_shape=jax.ShapeDtypeStruct((half, D), F32),
        compiler_params=_cparams("arbitrary"),
    )(sel, part, *([land] * (NDEV - 1)))


def _share_start(halves, name):
    nw = len(halves)
    lands = [lax.empty(h.shape, F32) for h in halves]

    def body(*refs):
        h_refs, land_refs = refs[:nw], refs[nw:2 * nw]
        send_sems, recv_sems = refs[2 * nw:2 * nw + 2]
        token = refs[-1]
        x, y, c = _position()
        for w in range(nw):
            _remote(h_refs[w], land_refs[w], send_sems.at[w], recv_sems.at[w], (x, y, 1 - c)).start()
        token[...] = jnp.zeros_like(token)

    res = _split_call(body, name, list(halves) + lands, (nw, nw), (TOKEN,))
    return res[0], res[1], res[2:2 + nw], res[2 + nw:2 + 2 * nw], res[-1]


def _share_wait(send_sems, recv_sems, halves, lands, after, name):
    nw = len(halves)

    def body(*refs):
        h_refs, land_refs = refs[:nw], refs[nw:2 * nw]
        send_sems, recv_sems = refs[2 * nw:2 * nw + 2]
        x, y, c = _position()
        for w in range(nw):
            cp = _remote(h_refs[w], land_refs[w], send_sems.at[w], recv_sems.at[w], (x, y, 1 - c))
            cp.wait_send()
            cp.wait_recv()

    operands = list(halves) + list(lands)
    res = pl.pallas_call(
        body, name=name, out_shape=tuple(pltpu.HBM(a.shape, a.dtype) for a in operands),
        in_specs=(HBM,) * (2 * nw) + (SEM, SEM, ANY), out_specs=(HBM,) * (2 * nw),
        input_output_aliases={i: i for i in range(2 * nw)},
        compiler_params=pltpu.CompilerParams(has_side_effects=DATAFLOW),
    )(*operands, send_sems, recv_sems, after)
    return res[:nw], res[nw:]


def _join_halves(own, other, c):
    first = jnp.where(c == 0, own, other)
    second = jnp.where(c == 0, other, own)
    return jnp.concatenate([first, second], axis=0)


SMALL_SIZES = (("norm1_g", D), ("sgu_ln_g", GW), ("sgu_ln_b", GW), ("sgu_w", NG * CHUNK * CHUNK),
               ("sgu_b", NG * CHUNK), ("attn_out_g", A), ("gmlp_out_g", GW), ("norm2_g", D),
               ("final_norm_g", D))
PARAM_ROWS = sum(n for _, n in SMALL_SIZES) // LANES
SMALL_ROWS = PARAM_ROWS + 8


def _pack_small(tree, first_extra=None):
    extra = jnp.zeros((8 * LANES,), F32)
    if first_extra is not None:
        extra = extra.at[0].set(first_extra)
    flat = jnp.concatenate([tree[n].reshape(-1) for n, _ in SMALL_SIZES] + [extra])
    return flat.reshape(SMALL_ROWS, LANES)


def _unpack_small(pack, shapes):
    flat = pack.reshape(-1)
    out, off = {}, 0
    for n, size in SMALL_SIZES:
        out[n] = flat[off:off + size].reshape(shapes[n])
        off += size
    return out


def _small_allreduce_adamw(gpack, wpack, mpack, vpack):
    def body(g_ref, w_ref, m_ref, v_ref, go_ref, d_ref, mo_ref, vo_ref, slots, send_sems, recv_sems):
        x, y, c = _position()
        me = 4 * x + 2 * y + c
        slots[me] = g_ref[...]
        peers = _xor_peers(x, y, c)
        sends = []
        for k, peer in enumerate(peers):
            cp = _remote(g_ref, slots.at[me], send_sems.at[k], recv_sems.at[k], peer)
            cp.start()
            sends.append(cp)
        for k, (px, py, pc) in enumerate(peers):
            _remote(g_ref, slots.at[4 * px + 2 * py + pc], send_sems.at[k], recv_sems.at[k],
                    (px, py, pc)).wait_recv()
        for cp in sends:
            cp.wait_send()
        total = slots[0]
        for k in range(1, NDEV):
            total = total + slots[k]
        go_ref[...] = total
        d, mn, vn = _adamw_math(w_ref[...], total, m_ref[...], v_ref[...])
        d_ref[...] = d
        mo_ref[...] = mn
        vo_ref[...] = vn

    sd = jax.ShapeDtypeStruct((SMALL_ROWS, LANES), F32)
    vm = pl.BlockSpec(memory_space=pltpu.VMEM)
    return pl.pallas_call(
        body, name="small_allreduce_adamw", in_specs=[vm] * 4, out_specs=[vm] * 4, out_shape=[sd] * 4,
        scratch_shapes=[pltpu.VMEM((NDEV, SMALL_ROWS, LANES), F32), pltpu.SemaphoreType.DMA((NDEV - 1,)),
                        pltpu.SemaphoreType.DMA((NDEV - 1,))],
        compiler_params=pltpu.CompilerParams(has_side_effects=True),
    )(gpack, wpack, mpack, vpack)


def kernel(x, norm1_g, w_in, sgu_ln_g, sgu_ln_b, sgu_w, sgu_b, attn_out_g, gmlp_out_g, w_out, norm2_g, w_ff1, w_ff2, final_norm_g, loss_target, m_norm1_g, m_w_in, m_sgu_ln_g, m_sgu_ln_b, m_sgu_w, m_sgu_b, m_attn_out_g, m_gmlp_out_g, m_w_out, m_norm2_g, m_w_ff1, m_w_ff2, m_final_norm_g, v_norm1_g, v_w_in, v_sgu_ln_g, v_sgu_ln_b, v_sgu_w, v_sgu_b, v_attn_out_g, v_gmlp_out_g, v_w_out, v_norm2_g, v_w_ff1, v_w_ff2, v_final_norm_g):
    names = [n for n, _ in SMALL_SIZES]
    w_small = dict(norm1_g=norm1_g, sgu_ln_g=sgu_ln_g, sgu_ln_b=sgu_ln_b, sgu_w=sgu_w, sgu_b=sgu_b,
                   attn_out_g=attn_out_g, gmlp_out_g=gmlp_out_g, norm2_g=norm2_g, final_norm_g=final_norm_g)
    m_small = dict(norm1_g=m_norm1_g, sgu_ln_g=m_sgu_ln_g, sgu_ln_b=m_sgu_ln_b, sgu_w=m_sgu_w, sgu_b=m_sgu_b,
                   attn_out_g=m_attn_out_g, gmlp_out_g=m_gmlp_out_g, norm2_g=m_norm2_g,
                   final_norm_g=m_final_norm_g)
    v_small = dict(norm1_g=v_norm1_g, sgu_ln_g=v_sgu_ln_g, sgu_ln_b=v_sgu_ln_b, sgu_w=v_sgu_w, sgu_b=v_sgu_b,
                   attn_out_g=v_attn_out_g, gmlp_out_g=v_gmlp_out_g, norm2_g=v_norm2_g,
                   final_norm_g=v_final_norm_g)
    shapes = {n: w_small[n].shape for n in names}

    r_in, r_out, r_ff = INW // NCHIP, D // NCHIP, DFF // NCHIP
    o1, o2, o3 = r_in, r_in + r_out, r_in + r_out + r_ff
    start_in = _gather_start(w_in[0].T.astype(BF16), "gather_in_start")
    start_rest = _gather_start(jnp.concatenate([w_out[0], w_ff1[0].T, w_ff2[0]], axis=0).astype(BF16),
                               "gather_rest_start")
    win_t = _gather_wait(*start_in[:4], after=start_rest[4], name="gather_in_wait").reshape(INW, D)

    def rest_weights(after):
        rest = _gather_wait(*start_rest[:4], after=after, name="gather_rest_wait")
        return (rest[:, :r_out].reshape(D, D), rest[:, r_out:r_out + r_ff].reshape(DFF, D),
                rest[:, r_out + r_ff:].reshape(DFF, D))

    small = dict(
        norm1_g=norm1_g, ln_g=sgu_ln_g.reshape(1, GW), ln_b=sgu_ln_b.reshape(1, GW), sgu_w=sgu_w[0],
        sgu_wt=jnp.swapaxes(sgu_w[0], 1, 2), bias_t=jnp.repeat(sgu_b[0].T, DH, axis=1),
        attn_out_g=attn_out_g, gmlp_out_g=gmlp_out_g, norm2_g=norm2_g, final_norm_g=final_norm_g.reshape(1, D))
    xi, yi, ci = _position()
    sel = jnp.stack([2 * xi + yi, ci]).astype(jnp.int32)
    state = {}

    def as_slabs(g):
        return g.reshape(NCHIP, g.shape[0] // NCHIP, D)

    def early_grads(gwff1_t, gwff2, gwout):
        state["early"] = _reduce_start([as_slabs(gwff1_t), as_slabs(gwff2), as_slabs(gwout)], "reduce_early_start")
        return state["early"][4][0:1, 0:1]

    def after_attention_bwd(marker):
        send_sems, recv_sems, parts, lands, _ = state["early"]
        parts, lands = _reduce_wait(send_sems, recv_sems, parts, lands, marker, "reduce_early_wait")
        halves = [_sum_pieces(p, l, sel, "sum_" + n) for p, l, n in zip(parts, lands, ("w_ff1", "w_ff2", "w_out"))]
        state["early_share"] = _share_start(halves, "share_early_start")
        return state["early_share"][4][0:1, 0:1]

    def late_grads(gwin_t):
        state["late"] = _reduce_start([as_slabs(gwin_t)], "reduce_late_start")
        return state["late"][4][0:1, 0:1]

    loss_part, dx, sg, _ = _local_step(
        x[0], loss_target[0], small, win_t, rest_weights, early_grads, after_attention_bwd, late_grads)
    late = state["late"]
    send_sems, recv_sems, halves, lands, _ = state["early_share"]
    own, other = _share_wait(send_sems, recv_sems, halves, lands, dx, "share_early_wait")
    g_big = {n: _join_halves(o, t, ci) for n, o, t in zip(("w_ff1", "w_ff2", "w_out"), own, other)}
    g_big["w_ff1"] = g_big["w_ff1"].T
    w_big = dict(w_in=(w_in, m_w_in, v_w_in), w_out=(w_out, m_w_out, v_w_out),
                 w_ff1=(w_ff1, m_w_ff1, v_w_ff1), w_ff2=(w_ff2, m_w_ff2, v_w_ff2))
    grads, deltas, new_m, new_v = {}, {}, {}, {}

    def update(n):
        w, m, v = w_big[n]
        d, mn, vn = _adamw(w[0], g_big[n], m[0], v[0], "adamw_" + n)
        grads[n], deltas[n], new_m[n], new_v[n] = g_big[n][None], d[None], mn[None], vn[None]

    for n in ("w_ff1", "w_ff2", "w_out"):
        update(n)
    updated = deltas["w_out"][0, 0:8, 0:LANES] + deltas["w_ff1"][0, 0:8, 0:LANES] + deltas["w_ff2"][0, 0:8, 0:LANES]
    late_parts, late_lands = _reduce_wait(late[0], late[1], late[2], late[3], updated, "reduce_late_wait")
    late_share = _share_start([_sum_pieces(late_parts[0], late_lands[0], sel, "sum_w_in")], "share_late_start")

    g_small = dict(norm1_g=sg["norm1_g"], sgu_ln_g=sg["ln_g"], sgu_ln_b=sg["ln_b"], sgu_w=sg["sgu_w"],
                   sgu_b=sg["sgu_b"], attn_out_g=sg["attn_out_g"], gmlp_out_g=sg["gmlp_out_g"],
                   norm2_g=sg["norm2_g"], final_norm_g=sg["final_norm_g"])
    packs = _small_allreduce_adamw(_pack_small(g_small, loss_part) + late_share[4][0:1, 0:1], _pack_small(w_small),
                                   _pack_small(m_small), _pack_small(v_small))
    loss = packs[0][PARAM_ROWS, 0]
    for tree, pack in zip((grads, deltas, new_m, new_v), packs):
        tree.update(_unpack_small(pack, shapes))
    own, other = _share_wait(late_share[0], late_share[1], late_share[2], late_share[3], packs[0], "share_late_wait")
    g_big["w_in"] = _join_halves(own[0], other[0], ci).T
    update("w_in")

    order = ["norm1_g", "w_in", "sgu_ln_g", "sgu_ln_b", "sgu_w", "sgu_b", "attn_out_g", "gmlp_out_g", "w_out",
             "norm2_g", "w_ff1", "w_ff2", "final_norm_g"]
    return (loss, dx[None], *[grads[n] for n in order], *[deltas[n] for n in order],
            *[new_m[n] for n in order], *[new_v[n] for n in order])
```

```python
import functools
import math

import numpy as np
import jax
import jax.numpy as jnp
from jax import lax
from jax.experimental import pallas as pl
from jax.experimental.pallas import tpu as pltpu

F32 = jnp.float32
BF16 = jnp.bfloat16

D = 1024
NH = 12
DH = 64
A = NH * DH
NG = 4
GW = NG * DH
INW = 3 * A + 2 * GW
DFF = 4 * D
CHUNK = 128
PATTERNS = ((128, 1), (512, 4), (2048, 16))
EPS = 1e-6
SCALE = DH ** -0.5
NEG = -1e30

LR, B1, B2, AEPS, WD, STEP = 0.001, 0.9, 0.999, 1e-08, 0.01, 10

TM = 512
TMX = 256
ATT_ROWS = 1024
FF_CH = 1024
LANES = 128
NCHIP = 4
VMEM_LIMIT = 56 * 1024 * 1024
MESH = pl.DeviceIdType.MESH


def _cparams(*sem, **kw):
    return pltpu.CompilerParams(dimension_semantics=sem if sem else None,
                                vmem_limit_bytes=VMEM_LIMIT, **kw)


def _dot(a, b):
    return jnp.dot(a, b, preferred_element_type=F32)


def _dot_nt(a, b):
    return lax.dot_general(a, b, (((1,), (1,)), ((), ())), preferred_element_type=F32)


def _dot_tn(a, b):
    return lax.dot_general(a, b, (((0,), (0,)), ((), ())), preferred_element_type=F32)


def _dot_hi(a, b):
    return jnp.dot(a, b, preferred_element_type=F32, precision=lax.Precision.HIGHEST)


def _alibi_slopes(n):
    def pow2(m):
        start = 2.0 ** (-8.0 / m)
        return [start ** (i + 1) for i in range(m)]
    if math.log2(n).is_integer():
        s = pow2(n)
    else:
        c = 2 ** int(math.floor(math.log2(n)))
        s = pow2(c) + pow2(2 * c)[0::2][: n - c]
    return np.asarray(s, dtype=np.float32)


def _rms_fwd(v, g):
    r = lax.rsqrt(jnp.mean(v * v, axis=-1, keepdims=True) + EPS)
    vn = v * r
    return vn * g, vn, r


def _rms_bwd(dy, vn, r, g):
    w = dy * g
    dv = r * (w - vn * jnp.mean(w * vn, axis=-1, keepdims=True))
    return dv, jnp.sum(dy * vn, axis=0, keepdims=True)


_K0 = math.sqrt(2.0 / math.pi)
_K1 = 0.044715


def _gelu(v):
    return 0.5 * v * (1.0 + jnp.tanh(_K0 * (v + _K1 * (v * v * v))))


def _gelu_grad(v):
    t = jnp.tanh(_K0 * (v + _K1 * (v * v * v)))
    return 0.5 * (1.0 + t) + 0.5 * v * (1.0 - t * t) * (_K0 * (1.0 + 3.0 * _K1 * v * v))


def _row_spec(rows, cols):
    return pl.BlockSpec((rows, cols), lambda i: (i, 0))


def _const_spec(shape):
    nd = len(shape)
    return pl.BlockSpec(shape, lambda i: (0,) * nd, pipeline_mode=pl.Buffered(1))


DILS = tuple(d for _, d in PATTERNS)


def _fill_cols(scr, value):
    for cb in range(value.shape[1] // LANES):
        scr[cb] = value[:, cb * LANES:(cb + 1) * LANES]


def _split_residues(scr, out_ref, dil):
    nb, rows, _ = scr.shape
    for r in range(dil):
        for cb in range(nb):
            piece = scr.at[cb][pl.ds(r, rows // dil, stride=dil), :]
            out_ref[r, :, cb * LANES:(cb + 1) * LANES] = piece.astype(out_ref.dtype)


def _merge_residues(in_ref, scr, dil):
    nb, rows, _ = scr.shape
    for r in range(dil):
        for cb in range(nb):
            scr.at[cb][pl.ds(r, rows // dil, stride=dil), :] = in_ref[r, :, cb * LANES:(cb + 1) * LANES].astype(F32)
    return jnp.concatenate([scr[cb] for cb in range(nb)], axis=-1)


def _col_scratch(rows, width):
    return pltpu.VMEM((width // LANES, rows, LANES), F32)


def _res_spec(dil, rows, width):
    return pl.BlockSpec((dil, rows // dil, width), lambda i: (0, i, 0))


def _res_shape(s, dil, width, dtype):
    return jax.ShapeDtypeStruct((dil, s // dil, width), dtype)


def _norm1(x, g1):
    s = x.shape[0]

    def body(x_ref, g_ref, hn_ref):
        hn, _, _ = _rms_fwd(x_ref[...], g_ref[...])
        hn_ref[...] = hn.astype(BF16)

    return pl.pallas_call(
        body, name="norm1", grid=(s // TM,), in_specs=[_row_spec(TM, D), _const_spec((1, D))],
        out_specs=_row_spec(TM, D), out_shape=jax.ShapeDtypeStruct((s, D), BF16),
        compiler_params=_cparams("arbitrary"),
    )(x, g1)


def _inproj_fwd(hn1, win_t):
    s = hn1.shape[0]
    nd = len(DILS)

    def body(hn_ref, w_ref, *rest):
        qkv_refs = rest[:3 * nd]
        u_ref, z_ref, scr = rest[3 * nd:]
        hn = hn_ref[...]
        for t in range(3):
            seg = _dot_nt(hn, w_ref[t * A:(t + 1) * A, :])
            seg = seg * SCALE if t == 0 else seg
            _fill_cols(scr, seg)
            for di, dil in enumerate(DILS):
                if dil == 1:
                    qkv_refs[t * nd + di][0] = seg.astype(BF16)
                else:
                    _split_residues(scr, qkv_refs[t * nd + di], dil)
        u_ref[...] = _dot_nt(hn, w_ref[3 * A:3 * A + GW, :])
        z_ref[...] = _dot_nt(hn, w_ref[3 * A + GW:INW, :])

    res = pl.pallas_call(
        body, name="inproj_fwd", grid=(s // TM,),
        in_specs=[_row_spec(TM, D), _const_spec((INW, D))],
        out_specs=[_res_spec(d, TM, A) for _ in range(3) for d in DILS] + [_row_spec(TM, GW), _row_spec(TM, GW)],
        out_shape=[_res_shape(s, d, A, BF16) for _ in range(3) for d in DILS]
                  + [jax.ShapeDtypeStruct((s, GW), F32)] * 2,
        scratch_shapes=[_col_scratch(TM, A)],
        compiler_params=_cparams("arbitrary"),
    )(hn1, win_t)
    q, k, v = (res[t * nd:(t + 1) * nd] for t in range(3))
    return q, k, v, res[-2], res[-1]


def _att_geometry(s, dil):
    length = s // dil
    rows = min(length, ATT_ROWS)
    return length, rows, length // rows, rows // CHUNK


def _stack_heads(t):
    lane = lax.broadcasted_iota(jnp.int32, t.shape, 1)
    zero = jnp.zeros_like(t)
    return jnp.concatenate([jnp.where(lane < DH, t, zero), jnp.where(lane >= DH, t, zero)], axis=0)


def _head_cols(t, hp):
    lane = lax.broadcasted_iota(jnp.int32, t.shape, 1)
    cols = [jnp.sum(jnp.where(lane == 2 * hp + h, t, 0.0), axis=-1, keepdims=True) for h in range(2)]
    return jnp.concatenate(cols, axis=0)


def _unstack_heads(t2):
    n = t2.shape[0] // 2
    lane = lax.broadcasted_iota(jnp.int32, (n, LANES), 1)
    return jnp.where(lane < DH, t2[:n], t2[n:])


def _query_window_bias(s0, s1, dil, first):
    row = lax.broadcasted_iota(jnp.int32, (2 * CHUNK, 2 * CHUNK), 0)
    col = lax.broadcasted_iota(jnp.int32, (2 * CHUNK, 2 * CHUNK), 1)
    steps = (row & (CHUNK - 1)) + CHUNK - col
    valid = (steps >= 0) & (steps <= CHUNK)
    if first:
        valid = valid & (col >= CHUNK)
    slope = jnp.where(row < CHUNK, s0, s1)
    return jnp.where(valid, -slope * (steps * dil).astype(F32), NEG)


def _key_block_bias(s0, s1, dil, last):
    key = lax.broadcasted_iota(jnp.int32, (CHUNK, 4 * CHUNK), 0)
    col = lax.broadcasted_iota(jnp.int32, (CHUNK, 4 * CHUNK), 1)
    wq = col & (2 * CHUNK - 1)
    steps = wq - key
    valid = (steps >= 0) & (steps <= CHUNK)
    if last:
        valid = valid & (wq < CHUNK)
    slope = jnp.where(col < 2 * CHUNK, s0, s1)
    return jnp.where(valid, -slope * (steps * dil).astype(F32), NEG)


def _head_rows(t, hp):
    row = lax.broadcasted_iota(jnp.int32, (8, LANES), 0)
    lane = lax.broadcasted_iota(jnp.int32, (8, LANES), 1)
    pick = jnp.where((row < 2) & (lane == 2 * hp + row), 1.0, 0.0).astype(BF16)
    hi = t.astype(BF16)
    rest = t - hi.astype(F32)
    mid = rest.astype(BF16)
    low = (rest - mid.astype(F32)).astype(BF16)
    return _dot_nt(pick, hi) + _dot_nt(pick, mid) + _dot_nt(pick, low)


def _att_specs(dil, rows, nsub, nblk):
    main = pl.BlockSpec((None, rows, LANES), lambda r, c, hp: (r, c, hp))
    prev = pl.BlockSpec((None, CHUNK, LANES), lambda r, c, hp: (r, jnp.maximum(c * nsub - 1, 0), hp))
    nxt = pl.BlockSpec((None, CHUNK, LANES), lambda r, c, hp: (r, jnp.minimum((c + 1) * nsub, nblk - 1), hp))
    main_heads = pl.BlockSpec((None, rows, LANES), lambda r, c, hp: (r, c, 0))
    nxt_heads = pl.BlockSpec((None, CHUNK, LANES), lambda r, c, hp: (r, jnp.minimum((c + 1) * nsub, nblk - 1), 0))
    return main, prev, nxt, main_heads, nxt_heads


def _row_start(i):
    return i * CHUNK if isinstance(i, int) else pl.multiple_of(i * CHUNK, CHUNK)


def _attn_fwd(q, k, v, slopes, dil):
    length = q.shape[1]
    _, rows, nch, nsub = _att_geometry(length * dil, dil)
    main, prev, _, main_heads, _ = _att_specs(dil, rows, nsub, length // CHUNK)

    def body(sl_ref, q_ref, k_ref, v_ref, kh_ref, vh_ref, o_ref, lse_ref, kbuf, vbuf, bias_buf):
        ch = pl.program_id(1)
        hp = pl.program_id(2)
        lane = lax.broadcasted_iota(jnp.int32, (CHUNK, LANES), 1)
        kbuf[0:CHUNK, :] = kh_ref[...]
        kbuf[CHUNK:, :] = k_ref[...]
        vbuf[0:CHUNK, :] = vh_ref[...]
        vbuf[CHUNK:, :] = v_ref[...]
        s0, s1 = sl_ref[2 * hp], sl_ref[2 * hp + 1]

        def block(i, bias):
            row = _row_start(i)
            rs = pl.ds(row, CHUNK)
            q2 = _stack_heads(q_ref[rs, :])
            kw = kbuf[pl.ds(row, 2 * CHUNK), :]
            vw = vbuf[pl.ds(row, 2 * CHUNK), :]
            sc = _dot_nt(q2, kw) + bias
            m = jnp.max(sc, axis=-1, keepdims=True)
            p = jnp.exp(sc - m)
            l = jnp.sum(p, axis=-1, keepdims=True)
            o2 = _dot(p.astype(BF16), vw) * (1.0 / l)
            o_ref[rs, :] = _unstack_heads(o2).astype(BF16)
            lse = m + jnp.log(l)
            seen = jnp.where(hp == 0, 0.0, lse_ref[rs, :])
            lse_ref[rs, :] = jnp.where(lane == 2 * hp, lse[:CHUNK], jnp.where(lane == 2 * hp + 1, lse[CHUNK:], seen))

        bias_buf[...] = _query_window_bias(s0, s1, dil, False)

        @pl.when(ch == 0)
        def _():
            block(0, _query_window_bias(s0, s1, dil, True))

        @pl.when(ch != 0)
        def _():
            block(0, bias_buf[...])

        for i in range(1, nsub):
            block(i, bias_buf[...])

    sd = jax.ShapeDtypeStruct
    return pl.pallas_call(
        body, name=f"attn_fwd_d{dil}", grid=(dil, nch, NH // 2),
        in_specs=[pl.BlockSpec(memory_space=pltpu.SMEM), main, main, main, prev, prev],
        out_specs=[main, main_heads], out_shape=[sd((dil, length, A), BF16), sd((dil, length, LANES), F32)],
        scratch_shapes=[pltpu.VMEM((rows + CHUNK, LANES), BF16), pltpu.VMEM((rows + CHUNK, LANES), BF16),
                        pltpu.VMEM((2 * CHUNK, 2 * CHUNK), F32)],
        compiler_params=_cparams("arbitrary", "arbitrary", "arbitrary"),
    )(slopes, q, k, v, k, v)


def _attn_bwd_dq(q, k, v, do, lse, delta, slopes, dil):
    length = q.shape[1]
    _, rows, nch, nsub = _att_geometry(length * dil, dil)
    main, prev, _, main_heads, _ = _att_specs(dil, rows, nsub, length // CHUNK)

    def body(sl_ref, q_ref, k_ref, v_ref, do_ref, lse_ref, dl_ref, kh_ref, vh_ref, dq_ref, kbuf, vbuf, bias_buf):
        ch = pl.program_id(1)
        hp = pl.program_id(2)
        kbuf[0:CHUNK, :] = kh_ref[...]
        kbuf[CHUNK:, :] = k_ref[...]
        vbuf[0:CHUNK, :] = vh_ref[...]
        vbuf[CHUNK:, :] = v_ref[...]
        s0, s1 = sl_ref[2 * hp], sl_ref[2 * hp + 1]

        def block(i, bias):
            row = _row_start(i)
            rs = pl.ds(row, CHUNK)
            q2 = _stack_heads(q_ref[rs, :])
            do2 = _stack_heads(do_ref[rs, :])
            lse2 = _head_cols(lse_ref[rs, :], hp)
            dl2 = _head_cols(dl_ref[rs, :], hp)
            kw = kbuf[pl.ds(row, 2 * CHUNK), :]
            vw = vbuf[pl.ds(row, 2 * CHUNK), :]
            p = jnp.exp(_dot_nt(q2, kw) + bias - lse2)
            ds = p * (_dot_nt(do2, vw) - dl2)
            dq_ref[rs, :] = _unstack_heads(_dot(ds.astype(BF16), kw)).astype(BF16)

        bias_buf[...] = _query_window_bias(s0, s1, dil, False)

        @pl.when(ch == 0)
        def _():
            block(0, _query_window_bias(s0, s1, dil, True))

        @pl.when(ch != 0)
        def _():
            block(0, bias_buf[...])

        for i in range(1, nsub):
            block(i, bias_buf[...])

    return pl.pallas_call(
        body, name=f"attn_dq_d{dil}", grid=(dil, nch, NH // 2),
        in_specs=[pl.BlockSpec(memory_space=pltpu.SMEM), main, main, main, main, main_heads, main_heads, prev, prev],
        out_specs=main, out_shape=jax.ShapeDtypeStruct((dil, length, A), BF16),
        scratch_shapes=[pltpu.VMEM((rows + CHUNK, LANES), BF16), pltpu.VMEM((rows + CHUNK, LANES), BF16),
                        pltpu.VMEM((2 * CHUNK, 2 * CHUNK), F32)],
        compiler_params=_cparams("arbitrary", "arbitrary", "arbitrary"),
    )(slopes, q, k, v, do, lse, delta, k, v)


def _attn_bwd_dkv(q, k, v, do, lse, delta, slopes, dil):
    length = q.shape[1]
    _, rows, nch, nsub = _att_geometry(length * dil, dil)
    main, _, nxt, main_heads, nxt_heads = _att_specs(dil, rows, nsub, length // CHUNK)

    def body(sl_ref, k_ref, v_ref, q_ref, do_ref, lse_ref, dl_ref, qh_ref, doh_ref, lseh_ref, dlh_ref,
             dk_ref, dv_ref, qbuf, dobuf, lse_rows, dl_rows, bias_buf):
        ch = pl.program_id(1)
        hp = pl.program_id(2)
        for buf, main_ref, halo_ref in ((qbuf, q_ref, qh_ref), (dobuf, do_ref, doh_ref)):
            buf[0:rows, :] = main_ref[...]
            buf[rows:, :] = halo_ref[...]
        for buf, main_ref, halo_ref in ((lse_rows, lse_ref, lseh_ref), (dl_rows, dl_ref, dlh_ref)):
            buf[:, 0:rows] = _head_rows(main_ref[...], hp)
            buf[:, rows:] = _head_rows(halo_ref[...], hp)
        s0, s1 = sl_ref[2 * hp], sl_ref[2 * hp + 1]

        def block(i, bias):
            row = _row_start(i)
            rs = pl.ds(row, CHUNK)
            win = pl.ds(row, 2 * CHUNK)
            kc = k_ref[rs, :]
            vc = v_ref[rs, :]
            q2 = _stack_heads(qbuf[win, :])
            do2 = _stack_heads(dobuf[win, :])
            cols = slice(i * CHUNK, (i + 2) * CHUNK)
            lse2 = jnp.concatenate([lse_rows[0:1, cols], lse_rows[1:2, cols]], axis=1)
            dl2 = jnp.concatenate([dl_rows[0:1, cols], dl_rows[1:2, cols]], axis=1)
            pt = jnp.exp(_dot_nt(kc, q2) + bias - lse2)
            dst = pt * (_dot_nt(vc, do2) - dl2)
            dv_ref[rs, :] = _dot(pt.astype(BF16), do2).astype(BF16)
            dk_ref[rs, :] = _dot(dst.astype(BF16), q2).astype(BF16)

        bias_buf[...] = _key_block_bias(s0, s1, dil, False)

        for i in range(nsub - 1):
            block(i, bias_buf[...])

        @pl.when(ch != nch - 1)
        def _():
            block(nsub - 1, bias_buf[...])

        @pl.when(ch == nch - 1)
        def _():
            block(nsub - 1, _key_block_bias(s0, s1, dil, True))

    sd = jax.ShapeDtypeStruct((dil, length, A), BF16)
    return pl.pallas_call(
        body, name=f"attn_dkv_d{dil}", grid=(dil, nch, NH // 2),
        in_specs=[pl.BlockSpec(memory_space=pltpu.SMEM), main, main, main, main, main_heads, main_heads,
                  nxt, nxt, nxt_heads, nxt_heads],
        out_specs=[main, main], out_shape=[sd, sd],
        scratch_shapes=[pltpu.VMEM((rows + CHUNK, LANES), BF16), pltpu.VMEM((rows + CHUNK, LANES), BF16),
                        pltpu.VMEM((8, rows + CHUNK), F32), pltpu.VMEM((8, rows + CHUNK), F32),
                        pltpu.VMEM((CHUNK, 4 * CHUNK), F32)],
        compiler_params=_cparams("arbitrary", "arbitrary", "arbitrary"),
    )(slopes, k, v, q, do, lse, delta, q, do, lse, delta)


def _group_masks(width):
    lane = lax.broadcasted_iota(jnp.int32, (1, width), 1)
    return [(lane >= g * DH) & (lane < (g + 1) * DH) for g in range(width // DH)]


def _group_mean_matrix():
    i = lax.broadcasted_iota(jnp.int32, (GW, GW), 0) // DH
    j = lax.broadcasted_iota(jnp.int32, (GW, GW), 1) // DH
    return jnp.where(i == j, 1.0 / DH, 0.0).astype(F32)


def _tri_mask(lower):
    t = lax.broadcasted_iota(jnp.int32, (CHUNK, CHUNK), 0)
    u = lax.broadcasted_iota(jnp.int32, (CHUNK, CHUNK), 1)
    return (u <= t) if lower else (u >= t)


def _sgu_forward(u, z, lng, lnb, w_ref, bias_t, pmat, rows):
    ug = _gelu(u)
    zg = _gelu(z)
    mu = _dot_hi(zg, pmat)
    zc = zg - mu
    var = _dot_hi(zc * zc, pmat)
    rstd = lax.rsqrt(var + EPS)
    zhat = zc * rstd
    zn = (zhat * lng + lnb).astype(BF16)
    gm = _group_masks(GW)
    tri = _tri_mask(True)
    ws = [jnp.where(tri, w_ref[g], 0.0).astype(BF16) for g in range(NG)]
    pieces = []
    for c in range(rows // CHUNK):
        znc = zn[c * CHUNK:(c + 1) * CHUNK, :]
        mix = None
        for g in range(NG):
            part = jnp.where(gm[g], _dot(ws[g], znc), 0.0)
            mix = part if mix is None else mix + part
        pieces.append(mix + bias_t)
    mixed = jnp.concatenate(pieces, axis=0) if len(pieces) > 1 else pieces[0]
    return ug * mixed, ug, zhat, rstd, zn, mixed


def _head_spread():
    h = lax.broadcasted_iota(jnp.int32, (LANES, A), 0)
    lane = lax.broadcasted_iota(jnp.int32, (LANES, A), 1)
    return jnp.where(lane // DH == h, 1.0, 0.0).astype(BF16)


def _bf16_pieces(t, n):
    pieces = []
    for _ in range(n):
        piece = t.astype(BF16)
        pieces.append(piece)
        t = t - piece.astype(F32)
    return pieces


def _mix_fwd(os_, ls_, u, z, x, lng, lnb, sgu_w, bias_t, ga, gg, wout):
    s = x.shape[0]
    nd = len(DILS)
    nscr = sum(1 for d in DILS if d > 1)

    def body(*refs):
        o_refs, l_refs = refs[:nd], refs[nd:2 * nd]
        u_ref, z_ref, x_ref, lng_ref, lnb_ref, w_ref, bt_ref, ga_ref, gg_ref, wo_ref = refs[2 * nd:2 * nd + 10]
        attn_ref = refs[2 * nd + 10]
        lse_refs = refs[2 * nd + 11:3 * nd + 11]
        mixed_ref, h1_ref = refs[3 * nd + 11:3 * nd + 13]
        scr = refs[3 * nd + 13:]
        scr_o, scr_l, scr_lse = scr[:nscr], scr[nscr:2 * nscr], scr[2 * nscr]
        ov, lv, j = [], [], 0
        for di, dil in enumerate(DILS):
            if dil == 1:
                ov.append(o_refs[di][0].astype(F32))
                lv.append(l_refs[di][0])
            else:
                ov.append(_merge_residues(o_refs[di], scr_o[j], dil))
                lv.append(_merge_residues(l_refs[di], scr_l[j], dil))
                j += 1
        mx = functools.reduce(jnp.maximum, lv)
        es = [jnp.exp(l - mx) for l in lv]
        den = functools.reduce(lambda a, b: a + b, es)
        spread = _head_spread()
        attn = None
        for e, o in zip(es, ov):
            wide = functools.reduce(lambda a, b: a + b, [_dot(piece, spread) for piece in _bf16_pieces(e / den, 2)])
            attn = wide * o if attn is None else attn + wide * o
        attn_ref[...] = attn
        lse = mx + jnp.log(den)
        _fill_cols(scr_lse, lse)
        for di, dil in enumerate(DILS):
            if dil == 1:
                lse_refs[di][0] = lse
            else:
                _split_residues(scr_lse, lse_refs[di], dil)
        an, _, _ = _rms_fwd(attn, ga_ref[...])
        gmv, _, _, _, _, _ = _sgu_forward(u_ref[...], z_ref[...], lng_ref[...], lnb_ref[...], w_ref,
                                          bt_ref[...], _group_mean_matrix(), TMX)
        gn, _, _ = _rms_fwd(gmv, gg_ref[...])
        mixed = jnp.concatenate([an, gn], axis=-1).astype(BF16)
        mixed_ref[...] = mixed
        h1_ref[...] = x_ref[...] + _dot(mixed, wo_ref[...])

    sd = jax.ShapeDtypeStruct
    res = pl.pallas_call(
        body, name="mix_fwd", grid=(s // TMX,),
        in_specs=[_res_spec(d, TMX, A) for d in DILS] + [_res_spec(d, TMX, LANES) for d in DILS]
                 + [_row_spec(TMX, GW), _row_spec(TMX, GW),
                    _row_spec(TMX, D), _const_spec((1, GW)), _const_spec((1, GW)), _const_spec((NG, CHUNK, CHUNK)),
                    _const_spec((CHUNK, GW)), _const_spec((1, A)), _const_spec((1, GW)), _const_spec((D, D))],
        out_specs=[_row_spec(TMX, A)] + [_res_spec(d, TMX, LANES) for d in DILS]
                  + [_row_spec(TMX, D), _row_spec(TMX, D)],
        out_shape=[sd((s, A), F32)] + [_res_shape(s, d, LANES, F32) for d in DILS]
                  + [sd((s, D), BF16), sd((s, D), F32)],
        scratch_shapes=[_col_scratch(TMX, A)] * nscr + [_col_scratch(TMX, LANES)] * (nscr + 1),
        compiler_params=_cparams("arbitrary"),
    )(*os_, *ls_, u, z, x, lng, lnb, sgu_w, bias_t, ga, gg, wout)
    return res[0], res[1:1 + nd], res[1 + nd], res[2 + nd]


def _mlp_fwd(h1, g2, wff1, wff2, gf, target):
    s = h1.shape[0]

    def body(h1_ref, g2_ref, w1_ref, w2_ref, gf_ref, t_ref, hn_ref, rf_ref, dh2_ref, loss_ref, dgf_ref):
        i = pl.program_id(0)
        h1v = h1_ref[...]
        hn, _, _ = _rms_fwd(h1v, g2_ref[...])
        hn = hn.astype(BF16)
        hn_ref[...] = hn
        acc = h1v
        for j in range(DFF // FF_CH):
            cols = slice(j * FF_CH, (j + 1) * FF_CH)
            rf = jnp.maximum(_dot(hn, w1_ref[j]), 0.0)
            act = (rf * rf).astype(BF16)
            rf_ref[:, cols] = rf.astype(BF16)
            acc = acc + _dot(act, w2_ref[cols, :])
        y, h2n, r3 = _rms_fwd(acc, gf_ref[...])
        err = y - t_ref[...]
        part = 0.5 * jnp.sum(jnp.mean(err * err, axis=-1, keepdims=True), axis=0, keepdims=True)
        dy = err * (1.0 / D)
        dh2, dgf = _rms_bwd(dy, h2n, r3, gf_ref[...])
        dh2_ref[...] = dh2

        @pl.when(i == 0)
        def _():
            loss_ref[...] = jnp.zeros_like(loss_ref)
            dgf_ref[...] = jnp.zeros_like(dgf_ref)

        loss_ref[...] += jnp.broadcast_to(part, loss_ref.shape)
        dgf_ref[...] += dgf

    sd = jax.ShapeDtypeStruct
    return pl.pallas_call(
        body, name="mlp_fwd", grid=(s // TM,),
        in_specs=[_row_spec(TM, D), _const_spec((1, D)), _const_spec((DFF // FF_CH, D, FF_CH)), _const_spec((DFF, D)),
                  _const_spec((1, D)), _row_spec(TM, D)],
        out_specs=[_row_spec(TM, D), _row_spec(TM, DFF), _row_spec(TM, D),
                   _const_spec((1, LANES)), _const_spec((1, D))],
        out_shape=[sd((s, D), BF16), sd((s, DFF), BF16), sd((s, D), F32),
                   sd((1, LANES), F32), sd((1, D), F32)],
        compiler_params=_cparams("arbitrary"),
    )(h1, g2, wff1, wff2, gf, target)


def _mlp_bwd(dh2, rf, h1, g2, wff1, wff2):
    s = h1.shape[0]

    def body(dh2_ref, rf_ref, h1_ref, g2_ref, w1_ref, w2_ref, df_ref, dh1_ref, dg2_ref):
        i = pl.program_id(0)
        dh2v = dh2_ref[...]
        dh2b = dh2v.astype(BF16)
        dhn = jnp.zeros((TM, D), F32)
        for j in range(DFF // FF_CH):
            cols = slice(j * FF_CH, (j + 1) * FF_CH)
            da = _dot_nt(dh2b, w2_ref[cols, :])
            df = (da * (2.0 * rf_ref[:, cols].astype(F32))).astype(BF16)
            df_ref[:, cols] = df
            dhn = dhn + _dot_nt(df, w1_ref[j])
        _, h1n, r2 = _rms_fwd(h1_ref[...], g2_ref[...])
        dres, dg2 = _rms_bwd(dhn, h1n, r2, g2_ref[...])
        dh1_ref[...] = dh2v + dres

        @pl.when(i == 0)
        def _():
            dg2_ref[...] = jnp.zeros_like(dg2_ref)

        dg2_ref[...] += dg2

    sd = jax.ShapeDtypeStruct
    return pl.pallas_call(
        body, name="mlp_bwd", grid=(s // TM,),
        in_specs=[_row_spec(TM, D), _row_spec(TM, DFF), _row_spec(TM, D), _const_spec((1, D)),
                  _const_spec((DFF // FF_CH, D, FF_CH)), _const_spec((DFF, D))],
        out_specs=[_row_spec(TM, DFF), _row_spec(TM, D), _const_spec((1, D))],
        out_shape=[sd((s, DFF), BF16), sd((s, D), F32), sd((1, D), F32)],
        compiler_params=_cparams("arbitrary"),
    )(dh2, rf, h1, g2, wff1, wff2)


def _mix_bwd(dh1, attn, u, z, lng, lnb, sgu_w, sgu_wt, bias_t, ga, gg, wout):
    s = dh1.shape[0]
    nsteps = s // TMX
    nd = len(DILS)

    def body(*refs):
        dh1_ref, attn_ref, u_ref, z_ref, lng_ref, lnb_ref, w_ref, wt_ref, bt_ref, ga_ref, gg_ref, wo_ref = refs[:12]
        do_refs, dl_refs = refs[12:12 + nd], refs[12 + nd:12 + 2 * nd]
        (du_ref, dz_ref, dga_ref, dgg_ref, dlng_ref, dlnb_ref, dws_ref, db_ref,
         dbt_acc, scr_do, scr_dl) = refs[12 + 2 * nd:]
        i = pl.program_id(0)

        @pl.when(i == 0)
        def _():
            for r in (dga_ref, dgg_ref, dlng_ref, dlnb_ref, dws_ref, db_ref, dbt_acc):
                r[...] = jnp.zeros_like(r)

        dmixed = _dot_nt(dh1_ref[...].astype(BF16), wo_ref[...])
        attn = attn_ref[...]
        _, an, ra = _rms_fwd(attn, ga_ref[...])
        dattn, dga = _rms_bwd(dmixed[:, :A], an, ra, ga_ref[...])
        dga_ref[...] += dga
        _fill_cols(scr_do, dattn)
        spread = _head_spread()
        delta = functools.reduce(lambda a, b: a + b, [_dot_nt(piece, spread) for piece in _bf16_pieces(dattn * attn, 3)])
        _fill_cols(scr_dl, delta)
        for di, dil in enumerate(DILS):
            if dil == 1:
                do_refs[di][0] = dattn.astype(BF16)
                dl_refs[di][0] = delta
            else:
                _split_residues(scr_do, do_refs[di], dil)
                _split_residues(scr_dl, dl_refs[di], dil)
        pmat = _group_mean_matrix()
        lng = lng_ref[...]
        uv, zv = u_ref[...], z_ref[...]
        gmv, ug, zhat, rstd, zn, mixed = _sgu_forward(uv, zv, lng, lnb_ref[...], w_ref, bt_ref[...], pmat, TMX)
        _, gmn, rg = _rms_fwd(gmv, gg_ref[...])
        dgm, dgg = _rms_bwd(dmixed[:, A:], gmn, rg, gg_ref[...])
        dgg_ref[...] += dgg
        du_ref[...] = dgm * mixed * _gelu_grad(uv)
        dmx = dgm * ug
        dmxb = dmx.astype(BF16)
        gm = _group_masks(GW)
        tri_t = _tri_mask(False)
        wst = [jnp.where(tri_t, wt_ref[g], 0.0).astype(BF16) for g in range(NG)]
        zero = jnp.zeros((CHUNK, GW), BF16)
        dzn_pieces = []
        for c in range(TMX // CHUNK):
            rs = slice(c * CHUNK, (c + 1) * CHUNK)
            dmc = dmxb[rs, :]
            znc = zn[rs, :]
            dbt_acc[...] += dmx[rs, :]
            dzn = None
            for g in range(NG):
                dws_ref[g] += _dot_nt(jnp.where(gm[g], dmc, zero), znc)
                part = jnp.where(gm[g], _dot(wst[g], dmc), 0.0)
                dzn = part if dzn is None else dzn + part
            dzn_pieces.append(dzn)
        dzn = jnp.concatenate(dzn_pieces, axis=0)
        dlng_ref[...] += jnp.sum(dzn * zhat, axis=0, keepdims=True)
        dlnb_ref[...] += jnp.sum(dzn, axis=0, keepdims=True)
        dzh = dzn * lng
        dzg = rstd * (dzh - _dot_hi(dzh, pmat) - zhat * _dot_hi(dzh * zhat, pmat))
        dz_ref[...] = dzg * _gelu_grad(zv)

        @pl.when(i == nsteps - 1)
        def _():
            tri = _tri_mask(True)
            for g in range(NG):
                dws_ref[g] = jnp.where(tri, dws_ref[g], 0.0)
            acc = dbt_acc[...]
            lane = lax.broadcasted_iota(jnp.int32, (CHUNK, LANES), 1)
            out = jnp.zeros((CHUNK, LANES), F32)
            for g in range(NG):
                sg = jnp.sum(jnp.where(gm[g], acc, 0.0), axis=-1, keepdims=True)
                out = jnp.where(lane == g, sg, out)
            db_ref[...] = out

    sd = jax.ShapeDtypeStruct
    res = pl.pallas_call(
        body, name="mix_bwd", grid=(nsteps,),
        in_specs=[_row_spec(TMX, D), _row_spec(TMX, A), _row_spec(TMX, GW), _row_spec(TMX, GW),
                  _const_spec((1, GW)), _const_spec((1, GW)), _const_spec((NG, CHUNK, CHUNK)),
                  _const_spec((NG, CHUNK, CHUNK)), _const_spec((CHUNK, GW)), _const_spec((1, A)),
                  _const_spec((1, GW)), _const_spec((D, D))],
        out_specs=[_res_spec(d, TMX, A) for d in DILS] + [_res_spec(d, TMX, LANES) for d in DILS]
                  + [_row_spec(TMX, GW), _row_spec(TMX, GW),
                   _const_spec((1, A)), _const_spec((1, GW)), _const_spec((1, GW)), _const_spec((1, GW)),
                   _const_spec((NG, CHUNK, CHUNK)), _const_spec((CHUNK, LANES))],
        out_shape=[_res_shape(s, d, A, BF16) for d in DILS] + [_res_shape(s, d, LANES, F32) for d in DILS]
                  + [sd((s, GW), F32), sd((s, GW), F32),
                   sd((1, A), F32), sd((1, GW), F32), sd((1, GW), F32), sd((1, GW), F32),
                   sd((NG, CHUNK, CHUNK), F32), sd((CHUNK, LANES), F32)],
        scratch_shapes=[pltpu.VMEM((CHUNK, GW), F32), _col_scratch(TMX, A), _col_scratch(TMX, LANES)],
        compiler_params=_cparams("arbitrary"),
    )(dh1, attn, u, z, lng, lnb, sgu_w, sgu_wt, bias_t, ga, gg, wout)
    return (res[:nd], res[nd:2 * nd]) + tuple(res[2 * nd:])


def _dproj_merge(dqs, dks, dvs, du, dz, pin):
    s = du.shape[0]
    nd = len(DILS)
    nscr = sum(1 for d in DILS if d > 1)

    def body(*refs):
        pin_ref = refs[0]
        parts = [refs[1 + t * nd:1 + (t + 1) * nd] for t in range(3)]
        du_ref, dz_ref, dp_ref = refs[1 + 3 * nd:4 + 3 * nd]
        scr = refs[4 + 3 * nd:]
        sums = []
        for t in range(3):
            total, j = None, 0
            for di, dil in enumerate(DILS):
                if dil == 1:
                    term = parts[t][di][0].astype(F32)
                else:
                    term = _merge_residues(parts[t][di], scr[t * nscr + j], dil)
                    j += 1
                total = term if total is None else total + term
            sums.append(total)
        dp_ref[...] = jnp.concatenate([sums[0] * SCALE, sums[1], sums[2], du_ref[...] + pin_ref[0, 0], dz_ref[...]],
                                      axis=-1).astype(BF16)

    return pl.pallas_call(
        body, name="dproj_merge", grid=(s // TMX,),
        in_specs=[pl.BlockSpec(memory_space=pltpu.SMEM)] + [_res_spec(d, TMX, A) for d in DILS] * 3
                 + [_row_spec(TMX, GW)] * 2,
        out_specs=_row_spec(TMX, INW), out_shape=jax.ShapeDtypeStruct((s, INW), BF16),
        scratch_shapes=[_col_scratch(TMX, A)] * (3 * nscr),
        compiler_params=_cparams("arbitrary"),
    )(pin, *dqs, *dks, *dvs, du, dz)


def _inproj_bwd(dproj, dh1, x, g1, win_t):
    s = x.shape[0]

    def body(dp_ref, dh1_ref, x_ref, g_ref, w_ref, dx_ref, dg_ref):
        i = pl.program_id(0)
        dhn = _dot(dp_ref[...], w_ref[...])
        _, xn, r1 = _rms_fwd(x_ref[...], g_ref[...])
        dres, dg = _rms_bwd(dhn, xn, r1, g_ref[...])
        dx_ref[...] = dh1_ref[...] + dres

        @pl.when(i == 0)
        def _():
            dg_ref[...] = jnp.zeros_like(dg_ref)

        dg_ref[...] += dg

    sd = jax.ShapeDtypeStruct
    return pl.pallas_call(
        body, name="inproj_bwd", grid=(s // TM,),
        in_specs=[_row_spec(TM, INW), _row_spec(TM, D), _row_spec(TM, D), _const_spec((1, D)), _const_spec((INW, D))],
        out_specs=[_row_spec(TM, D), _const_spec((1, D))],
        out_shape=[sd((s, D), F32), sd((1, D), F32)],
        compiler_params=_cparams("arbitrary"),
    )(dproj, dh1, x, g1, win_t)


def _wgrad(a, b, name, bm, bn, bk=2 * TM, square_a=False):
    s, m = a.shape
    n = b.shape[1]
    bm, bn = min(bm, m), min(bn, n)

    def body(a_ref, b_ref, o_ref):
        @pl.when(pl.program_id(2) == 0)
        def _():
            o_ref[...] = jnp.zeros_like(o_ref)

        av = a_ref[...]
        if square_a:
            av = av.astype(F32)
            av = av * av
        o_ref[...] += _dot_tn(av.astype(BF16), b_ref[...].astype(BF16))

    return pl.pallas_call(
        body, name=name, grid=(m // bm, n // bn, s // bk),
        in_specs=[pl.BlockSpec((bk, bm), lambda i, j, k: (k, i)), pl.BlockSpec((bk, bn), lambda i, j, k: (k, j))],
        out_specs=pl.BlockSpec((bm, bn), lambda i, j, k: (i, j)),
        out_shape=jax.ShapeDtypeStruct((m, n), F32),
        compiler_params=_cparams("arbitrary", "arbitrary", "arbitrary"),
    )(a, b)


def _adamw_math(w, g, m, v):
    m = B1 * m + (1.0 - B1) * g
    v = B2 * v + (1.0 - B2) * (g * g)
    m_hat = m / (1.0 - B1 ** STEP)
    v_hat = v / (1.0 - B2 ** STEP)
    delta = -LR * (m_hat / (jnp.sqrt(v_hat) + AEPS) + WD * w)
    return delta, m, v


def _adamw(w, g, m, v, name):
    rows, cols = w.shape
    br = min(rows, 256)
    while rows % br:
        br -= 8

    def body(w_ref, g_ref, m_ref, v_ref, d_ref, mo_ref, vo_ref):
        d, mn, vn = _adamw_math(w_ref[...], g_ref[...], m_ref[...], v_ref[...])
        d_ref[...] = d
        mo_ref[...] = mn
        vo_ref[...] = vn

    spec = _row_spec(br, cols)
    sd = jax.ShapeDtypeStruct((rows, cols), F32)
    return pl.pallas_call(
        body, name=name, grid=(rows // br,), in_specs=[spec] * 4, out_specs=[spec] * 3,
        out_shape=[sd, sd, sd], compiler_params=_cparams("arbitrary"),
    )(w, g, m, v)


def _local_step(x, hn1, target, small, win_t, rest_weights, early_grads=None, after_attention_bwd=None,
                late_grads=None):
    slopes = jnp.asarray(_alibi_slopes(NH))
    q, k, v, u, z = _inproj_fwd(hn1, win_t)
    outs, lses = [], []
    for i, dil in enumerate(DILS):
        o, l = _attn_fwd(q[i], k[i], v[i], slopes, dil)
        outs.append(o)
        lses.append(l)
    wout, wff1, wff2 = rest_weights(lses[-1])
    attn, lse, mixed, h1 = _mix_fwd(outs, lses, u, z, x, small["ln_g"], small["ln_b"], small["sgu_w"],
                                    small["bias_t"], small["attn_out_g"], small["gmlp_out_g"], wout)
    hn2, rf, dh2, loss, dgf = _mlp_fwd(h1, small["norm2_g"], wff1, wff2, small["final_norm_g"], target)
    df, dh1, dg2 = _mlp_bwd(dh2, rf, h1, small["norm2_g"], wff1, wff2)
    gwff1 = _wgrad(hn2, df, "wgrad_ff1", D, 1024)
    gwff2 = _wgrad(rf, dh2, "wgrad_ff2", 1024, D, square_a=True)
    gwout = _wgrad(mixed, dh1, "wgrad_out", D, D)
    ga, g1 = small["attn_out_g"], small["norm1_g"]
    pin = early_grads(gwff1, gwff2, gwout) if early_grads else None
    if pin is not None:
        ga = ga + pin
    (do, delta, du, dz, dga, dgg, dlng, dlnb, dws, db) = _mix_bwd(
        dh1, attn, u, z, small["ln_g"], small["ln_b"], small["sgu_w"], small["sgu_wt"], small["bias_t"],
        ga, small["gmlp_out_g"], wout)
    dqs, dks, dvs = [], [], []
    for i, dil in enumerate(DILS):
        dqs.append(_attn_bwd_dq(q[i], k[i], v[i], do[i], lse[i], delta[i], slopes, dil))
        dk, dv = _attn_bwd_dkv(q[i], k[i], v[i], do[i], lse[i], delta[i], slopes, dil)
        dks.append(dk)
        dvs.append(dv)
    marker = functools.reduce(lambda a, b: a + b, [t[0, 0:8, 0:LANES] for t in dqs + dks + dvs])
    pin = after_attention_bwd(marker) if after_attention_bwd else None
    dproj = _dproj_merge(dqs, dks, dvs, du, dz, jnp.zeros((1, 1), F32) if pin is None else pin)
    gwin_t = _wgrad(dproj, hn1, "wgrad_in", INW // 2, D)
    pin = late_grads(gwin_t) if late_grads else None
    if pin is not None:
        g1 = g1 + pin
    dx, dg1 = _inproj_bwd(dproj, dh1, x, g1, win_t)
    small_grads = dict(norm1_g=dg1, ln_g=dlng, ln_b=dlnb, sgu_w=dws, sgu_b=db[:, :NG].T,
                       attn_out_g=dga, gmlp_out_g=dgg, norm2_g=dg2, final_norm_g=dgf)
    return loss[0, 0], dx, small_grads, (gwin_t, gwout, gwff1, gwff2)


ANY = pl.BlockSpec(memory_space=pl.ANY)
NDEV = 8


def _position():
    return lax.axis_index("x"), lax.axis_index("y"), lax.axis_index("c")


def _other_chips(x, y):
    return [(1 - x, y), (x, 1 - y), (1 - x, 1 - y)]


def _remote(src, dst, send_sem, recv_sem, device):
    return pltpu.make_async_remote_copy(src_ref=src, dst_ref=dst, send_sem=send_sem, recv_sem=recv_sem,
                                        device_id=device, device_id_type=MESH)


HBM = pl.BlockSpec(memory_space=pltpu.HBM)
SEM = pl.BlockSpec(memory_space=pltpu.SEMAPHORE)
DATAFLOW = pltpu.SideEffectType.DATAFLOW_SIDE_EFFECTING


def _in_hbm(a):
    return pltpu.with_memory_space_constraint(a, pltpu.HBM)


def _gather_start(shards, name):
    n = len(shards)
    lands = [jnp.broadcast_to(sh[None], (NCHIP,) + sh.shape) for sh in shards]

    def body(*refs):
        w_refs, land_refs = refs[:n], refs[n:2 * n]
        send_sems, recv_sems = refs[2 * n:2 * n + 2]
        token = refs[-1]
        x, y, c = _position()
        for w in range(n):
            for k, (px, py) in enumerate(_other_chips(x, y)):
                m = 3 * w + k
                _remote(w_refs[w], land_refs[w].at[2 * x + y], send_sems.at[m], recv_sems.at[m], (px, py, c)).start()
        token[...] = jnp.zeros_like(token)

    res = _split_call(body, name, list(shards) + lands, (3 * n, 3 * n), (TOKEN,))
    return res[0], res[1], res[2:2 + n], res[2 + n:2 + 2 * n], res[-1]


def _gather_wait(send_sems, recv_sems, shards, lands, after, name):
    n = len(shards)

    def body(*refs):
        w_refs, land_refs = refs[:n], refs[n:2 * n]
        send_sems, recv_sems = refs[2 * n:2 * n + 2]
        x, y, c = _position()
        for w in range(n):
            for k, (px, py) in enumerate(_other_chips(x, y)):
                m = 3 * w + k
                cp = _remote(w_refs[w], land_refs[w].at[2 * px + py], send_sems.at[m], recv_sems.at[m], (px, py, c))
                cp.wait_send()
                cp.wait_recv()

    operands = list(shards) + list(lands)
    res = pl.pallas_call(
        body, name=name, out_shape=tuple(pltpu.HBM(a.shape, a.dtype) for a in operands),
        in_specs=(HBM,) * (2 * n) + (SEM, SEM, ANY), out_specs=(HBM,) * (2 * n),
        input_output_aliases={i: i for i in range(2 * n)},
        compiler_params=pltpu.CompilerParams(has_side_effects=DATAFLOW),
    )(*operands, send_sems, recv_sems, after)
    return res[n:]


def _xor_peers(x, y, c):
    peers = []
    for k in range(1, NDEV):
        kx, ky, kc = (k >> 2) & 1, (k >> 1) & 1, k & 1
        peers.append((1 - x if kx else x, 1 - y if ky else y, 1 - c if kc else c))
    return peers


def _piece(part_ref, px, py, pc):
    slab = 2 * px + py
    if len(part_ref.shape) == 3:
        half = part_ref.shape[1] // 2
        return part_ref.at[slab, pl.ds(pc * half, half), :]
    half = part_ref.shape[0] // 2
    return part_ref.at[pl.ds(pc * half, half), pl.ds(pl.multiple_of(slab * D, D), D)]


def _split_call(body, name, operands, n_sems, extra_out=()):
    n = len(operands)
    sems = tuple(pltpu.SemaphoreType.DMA((m,)) for m in n_sems)
    thru = tuple(pltpu.HBM(a.shape, a.dtype) for a in operands)
    return pl.pallas_call(
        body, name=name, out_shape=sems + thru + tuple(extra_out),
        in_specs=(HBM,) * n,
        out_specs=(SEM,) * len(sems) + (HBM,) * n + (pl.BlockSpec(memory_space=pltpu.VMEM),) * len(extra_out),
        input_output_aliases={i: len(sems) + i for i in range(n)},
        compiler_params=pltpu.CompilerParams(has_side_effects=DATAFLOW),
    )(*[_in_hbm(a) for a in operands])


TOKEN = jax.ShapeDtypeStruct((8, LANES), F32)


def _reduce_start(parts, name):
    nw = len(parts)
    lands = [lax.empty((NDEV - 1, p.shape[-2] // 2, D), F32) for p in parts]

    def body(*refs):
        part_refs, land_refs = refs[:nw], refs[nw:2 * nw]
        send_sems, recv_sems = refs[2 * nw:2 * nw + 2]
        token = refs[-1]
        x, y, c = _position()
        for w in range(nw):
            for k, peer in enumerate(_xor_peers(x, y, c)):
                n = w * (NDEV - 1) + k
                _remote(_piece(part_refs[w], *peer), land_refs[w].at[k], send_sems.at[n], recv_sems.at[n],
                        peer).start()
        token[...] = jnp.zeros_like(token)

    n = nw * (NDEV - 1)
    res = _split_call(body, name, list(parts) + lands, (n, n), (TOKEN,))
    return res[0], res[1], res[2:2 + nw], res[2 + nw:2 + 2 * nw], res[-1]


def _reduce_wait(send_sems, recv_sems, parts, lands, after, name):
    nw = len(parts)

    def body(*refs):
        part_refs, land_refs = refs[:nw], refs[nw:2 * nw]
        send_sems, recv_sems = refs[2 * nw:2 * nw + 2]
        x, y, c = _position()
        for w in range(nw):
            for k, peer in enumerate(_xor_peers(x, y, c)):
                n = w * (NDEV - 1) + k
                cp = _remote(_piece(part_refs[w], *peer), land_refs[w].at[k], send_sems.at[n], recv_sems.at[n], peer)
                cp.wait_send()
                cp.wait_recv()

    operands = list(parts) + list(lands)
    res = pl.pallas_call(
        body, name=name, out_shape=tuple(pltpu.HBM(a.shape, a.dtype) for a in operands),
        in_specs=(HBM,) * (2 * nw) + (SEM, SEM, ANY), out_specs=(HBM,) * (2 * nw),
        input_output_aliases={i: i for i in range(2 * nw)},
        compiler_params=pltpu.CompilerParams(has_side_effects=DATAFLOW),
    )(*operands, send_sems, recv_sems, after)
    return res[:nw], res[nw:]


def _sum_pieces(part, land, sel, name):
    half = part.shape[-2] // 2
    br = 128 if half % 128 == 0 else half // 2
    nb = half // br

    def body(sel_ref, own_ref, *refs):
        acc = own_ref[...]
        for r in refs[:NDEV - 1]:
            acc = acc + r[...]
        refs[NDEV - 1][...] = acc

    if part.ndim == 3:
        own_spec = pl.BlockSpec((None, br, D), lambda i, sel_ref: (sel_ref[0], sel_ref[1] * nb + i, 0))
    else:
        own_spec = pl.BlockSpec((br, D), lambda i, sel_ref: (sel_ref[1] * nb + i, sel_ref[0]))
    slot_specs = [pl.BlockSpec((None, br, D), functools.partial(lambda i, sel_ref, k: (k, i, 0), k=k))
                  for k in range(NDEV - 1)]
    return pl.pallas_call(
        body, name=name,
        grid_spec=pltpu.PrefetchScalarGridSpec(
            num_scalar_prefetch=1, grid=(nb,), in_specs=[own_spec] + slot_specs,
            out_specs=pl.BlockSpec((br, D), lambda i, sel_ref: (i, 0))),
        out_shape=jax.ShapeDtypeStruct((half, D), F32),
        compiler_params=_cparams("arbitrary"),
    )(sel, part, *([land] * (NDEV - 1)))


def _share_start(halves, name):
    nw = len(halves)
    lands = [lax.empty(h.shape, F32) for h in halves]

    def body(*refs):
        h_refs, land_refs = refs[:nw], refs[nw:2 * nw]
        send_sems, recv_sems = refs[2 * nw:2 * nw + 2]
        token = refs[-1]
        x, y, c = _position()
        for w in range(nw):
            _remote(h_refs[w], land_refs[w], send_sems.at[w], recv_sems.at[w], (x, y, 1 - c)).start()
        token[...] = jnp.zeros_like(token)

    res = _split_call(body, name, list(halves) + lands, (nw, nw), (TOKEN,))
    return res[0], res[1], res[2:2 + nw], res[2 + nw:2 + 2 * nw], res[-1]


def _share_wait(send_sems, recv_sems, halves, lands, after, name):
    nw = len(halves)

    def body(*refs):
        h_refs, land_refs = refs[:nw], refs[nw:2 * nw]
        send_sems, recv_sems = refs[2 * nw:2 * nw + 2]
        x, y, c = _position()
        for w in range(nw):
            cp = _remote(h_refs[w], land_refs[w], send_sems.at[w], recv_sems.at[w], (x, y, 1 - c))
            cp.wait_send()
            cp.wait_recv()

    operands = list(halves) + list(lands)
    res = pl.pallas_call(
        body, name=name, out_shape=tuple(pltpu.HBM(a.shape, a.dtype) for a in operands),
        in_specs=(HBM,) * (2 * nw) + (SEM, SEM, ANY), out_specs=(HBM,) * (2 * nw),
        input_output_aliases={i: i for i in range(2 * nw)},
        compiler_params=pltpu.CompilerParams(has_side_effects=DATAFLOW),
    )(*operands, send_sems, recv_sems, after)
    return res[:nw], res[nw:]


def _join_halves(own, other, c):
    first = jnp.where(c == 0, own, other)
    second = jnp.where(c == 0, other, own)
    return jnp.concatenate([first, second], axis=0)


SMALL_SIZES = (("norm1_g", D), ("sgu_ln_g", GW), ("sgu_ln_b", GW), ("sgu_w", NG * CHUNK * CHUNK),
               ("sgu_b", NG * CHUNK), ("attn_out_g", A), ("gmlp_out_g", GW), ("norm2_g", D),
               ("final_norm_g", D))
PARAM_ROWS = sum(n for _, n in SMALL_SIZES) // LANES
SMALL_ROWS = PARAM_ROWS + 8


def _pack_small(tree, first_extra=None):
    extra = jnp.zeros((8 * LANES,), F32)
    if first_extra is not None:
        extra = extra.at[0].set(first_extra)
    flat = jnp.concatenate([tree[n].reshape(-1) for n, _ in SMALL_SIZES] + [extra])
    return flat.reshape(SMALL_ROWS, LANES)


def _unpack_small(pack, shapes):
    flat = pack.reshape(-1)
    out, off = {}, 0
    for n, size in SMALL_SIZES:
        out[n] = flat[off:off + size].reshape(shapes[n])
        off += size
    return out


def _small_allreduce_adamw(gpack, wpack, mpack, vpack):
    def body(g_ref, w_ref, m_ref, v_ref, go_ref, d_ref, mo_ref, vo_ref, slots, send_sems, recv_sems):
        x, y, c = _position()
        me = 4 * x + 2 * y + c
        slots[me] = g_ref[...]
        peers = _xor_peers(x, y, c)
        sends = []
        for k, peer in enumerate(peers):
            cp = _remote(g_ref, slots.at[me], send_sems.at[k], recv_sems.at[k], peer)
            cp.start()
            sends.append(cp)
        for k, (px, py, pc) in enumerate(peers):
            _remote(g_ref, slots.at[4 * px + 2 * py + pc], send_sems.at[k], recv_sems.at[k],
                    (px, py, pc)).wait_recv()
        for cp in sends:
            cp.wait_send()
        total = slots[0]
        for k in range(1, NDEV):
            total = total + slots[k]
        go_ref[...] = total
        d, mn, vn = _adamw_math(w_ref[...], total, m_ref[...], v_ref[...])
        d_ref[...] = d
        mo_ref[...] = mn
        vo_ref[...] = vn

    sd = jax.ShapeDtypeStruct((SMALL_ROWS, LANES), F32)
    vm = pl.BlockSpec(memory_space=pltpu.VMEM)
    return pl.pallas_call(
        body, name="small_allreduce_adamw", in_specs=[vm] * 4, out_specs=[vm] * 4, out_shape=[sd] * 4,
        scratch_shapes=[pltpu.VMEM((NDEV, SMALL_ROWS, LANES), F32), pltpu.SemaphoreType.DMA((NDEV - 1,)),
                        pltpu.SemaphoreType.DMA((NDEV - 1,))],
        compiler_params=pltpu.CompilerParams(has_side_effects=True),
    )(gpack, wpack, mpack, vpack)


def kernel(x, norm1_g, w_in, sgu_ln_g, sgu_ln_b, sgu_w, sgu_b, attn_out_g, gmlp_out_g, w_out, norm2_g, w_ff1, w_ff2, final_norm_g, loss_target, m_norm1_g, m_w_in, m_sgu_ln_g, m_sgu_ln_b, m_sgu_w, m_sgu_b, m_attn_out_g, m_gmlp_out_g, m_w_out, m_norm2_g, m_w_ff1, m_w_ff2, m_final_norm_g, v_norm1_g, v_w_in, v_sgu_ln_g, v_sgu_ln_b, v_sgu_w, v_sgu_b, v_attn_out_g, v_gmlp_out_g, v_w_out, v_norm2_g, v_w_ff1, v_w_ff2, v_final_norm_g):
    names = [n for n, _ in SMALL_SIZES]
    w_small = dict(norm1_g=norm1_g, sgu_ln_g=sgu_ln_g, sgu_ln_b=sgu_ln_b, sgu_w=sgu_w, sgu_b=sgu_b,
                   attn_out_g=attn_out_g, gmlp_out_g=gmlp_out_g, norm2_g=norm2_g, final_norm_g=final_norm_g)
    m_small = dict(norm1_g=m_norm1_g, sgu_ln_g=m_sgu_ln_g, sgu_ln_b=m_sgu_ln_b, sgu_w=m_sgu_w, sgu_b=m_sgu_b,
                   attn_out_g=m_attn_out_g, gmlp_out_g=m_gmlp_out_g, norm2_g=m_norm2_g,
                   final_norm_g=m_final_norm_g)
    v_small = dict(norm1_g=v_norm1_g, sgu_ln_g=v_sgu_ln_g, sgu_ln_b=v_sgu_ln_b, sgu_w=v_sgu_w, sgu_b=v_sgu_b,
                   attn_out_g=v_attn_out_g, gmlp_out_g=v_gmlp_out_g, norm2_g=v_norm2_g,
                   final_norm_g=v_final_norm_g)
    shapes = {n: w_small[n].shape for n in names}

    start_in = _gather_start([w_in[0].T.astype(BF16)], "gather_in_start")
    start_rest = _gather_start([w_out[0].astype(BF16), w_ff1[0].astype(BF16), w_ff2[0].astype(BF16)],
                               "gather_rest_start")
    hn1 = _norm1(x[0], norm1_g + start_rest[4][0:1, 0:1])
    win_t = _gather_wait(*start_in[:4], after=hn1, name="gather_in_wait")[0].reshape(INW, D)

    def rest_weights(after):
        wout, wff1, wff2 = _gather_wait(*start_rest[:4], after=after, name="gather_rest_wait")
        return wout.reshape(D, D), wff1, wff2.reshape(DFF, D)

    small = dict(
        norm1_g=norm1_g, ln_g=sgu_ln_g.reshape(1, GW), ln_b=sgu_ln_b.reshape(1, GW), sgu_w=sgu_w[0],
        sgu_wt=jnp.swapaxes(sgu_w[0], 1, 2), bias_t=jnp.repeat(sgu_b[0].T, DH, axis=1),
        attn_out_g=attn_out_g, gmlp_out_g=gmlp_out_g, norm2_g=norm2_g, final_norm_g=final_norm_g.reshape(1, D))
    xi, yi, ci = _position()
    sel = jnp.stack([2 * xi + yi, ci]).astype(jnp.int32)
    state = {}

    def as_slabs(g):
        return g.reshape(NCHIP, g.shape[0] // NCHIP, D)

    def early_grads(gwff1, gwff2, gwout):
        state["early"] = _reduce_start([gwff1, as_slabs(gwff2), as_slabs(gwout)], "reduce_early_start")
        return state["early"][4][0:1, 0:1]

    def after_attention_bwd(marker):
        send_sems, recv_sems, parts, lands, _ = state["early"]
        parts, lands = _reduce_wait(send_sems, recv_sems, parts, lands, marker, "reduce_early_wait")
        halves = [_sum_pieces(p, l, sel, "sum_" + n) for p, l, n in zip(parts, lands, ("w_ff1", "w_ff2", "w_out"))]
        state["early_share"] = _share_start(halves, "share_early_start")
        return state["early_share"][4][0:1, 0:1]

    def late_grads(gwin_t):
        state["late"] = _reduce_start([as_slabs(gwin_t)], "reduce_late_start")
        return state["late"][4][0:1, 0:1]

    loss_part, dx, sg, _ = _local_step(
        x[0], hn1, loss_target[0], small, win_t, rest_weights, early_grads, after_attention_bwd, late_grads)
    late = state["late"]
    send_sems, recv_sems, halves, lands, _ = state["early_share"]
    own, other = _share_wait(send_sems, recv_sems, halves, lands, dx, "share_early_wait")
    g_big = {n: _join_halves(o, t, ci) for n, o, t in zip(("w_ff1", "w_ff2", "w_out"), own, other)}
    w_big = dict(w_in=(w_in, m_w_in, v_w_in), w_out=(w_out, m_w_out, v_w_out),
                 w_ff1=(w_ff1, m_w_ff1, v_w_ff1), w_ff2=(w_ff2, m_w_ff2, v_w_ff2))
    grads, deltas, new_m, new_v = {}, {}, {}, {}

    def update(n):
        w, m, v = w_big[n]
        d, mn, vn = _adamw(w[0], g_big[n], m[0], v[0], "adamw_" + n)
        grads[n], deltas[n], new_m[n], new_v[n] = g_big[n][None], d[None], mn[None], vn[None]

    for n in ("w_ff1", "w_ff2", "w_out"):
        update(n)
    updated = deltas["w_out"][0, 0:8, 0:LANES] + deltas["w_ff1"][0, 0:8, 0:LANES] + deltas["w_ff2"][0, 0:8, 0:LANES]
    late_parts, late_lands = _reduce_wait(late[0], late[1], late[2], late[3], updated, "reduce_late_wait")
    late_share = _share_start([_sum_pieces(late_parts[0], late_lands[0], sel, "sum_w_in")], "share_late_start")

    g_small = dict(norm1_g=sg["norm1_g"], sgu_ln_g=sg["ln_g"], sgu_ln_b=sg["ln_b"], sgu_w=sg["sgu_w"],
                   sgu_b=sg["sgu_b"], attn_out_g=sg["attn_out_g"], gmlp_out_g=sg["gmlp_out_g"],
                   norm2_g=sg["norm2_g"], final_norm_g=sg["final_norm_g"])
    packs = _small_allreduce_adamw(_pack_small(g_small, loss_part) + late_share[4][0:1, 0:1], _pack_small(w_small),
                                   _pack_small(m_small), _pack_small(v_small))
    loss = packs[0][PARAM_ROWS, 0]
    for tree, pack in zip((grads, deltas, new_m, new_v), packs):
        tree.update(_unpack_small(pack, shapes))
    own, other = _share_wait(late_share[0], late_share[1], late_share[2], late_share[3], packs[0], "share_late_wait")
    g_big["w_in"] = _join_halves(own[0], other[0], ci).T
    update("w_in")

    order = ["norm1_g", "w_in", "sgu_ln_g", "sgu_ln_b", "sgu_w", "sgu_b", "attn_out_g", "gmlp_out_g", "w_out",
             "norm2_g", "w_ff1", "w_ff2", "final_norm_g"]
    return (loss, dx[None], *[grads[n] for n in order], *[deltas[n] for n in order],
            *[new_m[n] for n in order], *[new_v[n] for n in order])
```

```python
import functools
import math

import numpy as np
import jax
import jax.numpy as jnp
from jax import lax
from jax.experimental import pallas as pl
from jax.experimental.pallas import tpu as pltpu

F32 = jnp.float32
BF16 = jnp.bfloat16

D = 1024
NH = 12
DH = 64
A = NH * DH
NG = 4
GW = NG * DH
INW = 3 * A + 2 * GW
DFF = 4 * D
CHUNK = 128
PATTERNS = ((128, 1), (512, 4), (2048, 16))
EPS = 1e-6
SCALE = DH ** -0.5
NEG = -1e30

LR, B1, B2, AEPS, WD, STEP = 0.001, 0.9, 0.999, 1e-08, 0.01, 10

TM = 512
TMX = 256
ATT_ROWS = 1024
FF_CH = 1024
LANES = 128
NCHIP = 4
VMEM_LIMIT = 56 * 1024 * 1024
MESH = pl.DeviceIdType.MESH


def _cparams(*sem, **kw):
    return pltpu.CompilerParams(dimension_semantics=sem if sem else None,
                                vmem_limit_bytes=VMEM_LIMIT, **kw)


def _dot(a, b):
    return jnp.dot(a, b, preferred_element_type=F32)


def _dot_nt(a, b):
    return lax.dot_general(a, b, (((1,), (1,)), ((), ())), preferred_element_type=F32)


def _dot_tn(a, b):
    return lax.dot_general(a, b, (((0,), (0,)), ((), ())), preferred_element_type=F32)


def _dot_hi(a, b):
    return jnp.dot(a, b, preferred_element_type=F32, precision=lax.Precision.HIGHEST)


def _alibi_slopes(n):
    def pow2(m):
        start = 2.0 ** (-8.0 / m)
        return [start ** (i + 1) for i in range(m)]
    if math.log2(n).is_integer():
        s = pow2(n)
    else:
        c = 2 ** int(math.floor(math.log2(n)))
        s = pow2(c) + pow2(2 * c)[0::2][: n - c]
    return np.asarray(s, dtype=np.float32)


def _rms_fwd(v, g):
    r = lax.rsqrt(jnp.mean(v * v, axis=-1, keepdims=True) + EPS)
    vn = v * r
    return vn * g, vn, r


def _rms_bwd(dy, vn, r, g):
    w = dy * g
    dv = r * (w - vn * jnp.mean(w * vn, axis=-1, keepdims=True))
    return dv, jnp.sum(dy * vn, axis=0, keepdims=True)


_K0 = math.sqrt(2.0 / math.pi)
_K1 = 0.044715


def _gelu(v):
    return 0.5 * v * (1.0 + jnp.tanh(_K0 * (v + _K1 * (v * v * v))))


def _gelu_grad(v):
    t = jnp.tanh(_K0 * (v + _K1 * (v * v * v)))
    return 0.5 * (1.0 + t) + 0.5 * v * (1.0 - t * t) * (_K0 * (1.0 + 3.0 * _K1 * v * v))


def _row_spec(rows, cols):
    return pl.BlockSpec((rows, cols), lambda i: (i, 0))


def _const_spec(shape):
    nd = len(shape)
    return pl.BlockSpec(shape, lambda i: (0,) * nd, pipeline_mode=pl.Buffered(1))


DILS = tuple(d for _, d in PATTERNS)


def _fill_cols(scr, value):
    for cb in range(value.shape[1] // LANES):
        scr[cb] = value[:, cb * LANES:(cb + 1) * LANES]


def _split_residues(scr, out_ref, dil):
    nb, rows, _ = scr.shape
    for r in range(dil):
        for cb in range(nb):
            piece = scr.at[cb][pl.ds(r, rows // dil, stride=dil), :]
            out_ref[r, :, cb * LANES:(cb + 1) * LANES] = piece.astype(out_ref.dtype)


def _merge_residues(in_ref, scr, dil):
    nb, rows, _ = scr.shape
    for r in range(dil):
        for cb in range(nb):
            scr.at[cb][pl.ds(r, rows // dil, stride=dil), :] = in_ref[r, :, cb * LANES:(cb + 1) * LANES].astype(F32)
    return jnp.concatenate([scr[cb] for cb in range(nb)], axis=-1)


def _col_scratch(rows, width):
    return pltpu.VMEM((width // LANES, rows, LANES), F32)


def _res_spec(dil, rows, width):
    return pl.BlockSpec((dil, rows // dil, width), lambda i: (0, i, 0))


def _res_shape(s, dil, width, dtype):
    return jax.ShapeDtypeStruct((dil, s // dil, width), dtype)


def _norm1(x, g1):
    s = x.shape[0]

    def body(x_ref, g_ref, hn_ref):
        hn, _, _ = _rms_fwd(x_ref[...], g_ref[...])
        hn_ref[...] = hn.astype(BF16)

    return pl.pallas_call(
        body, name="norm1", grid=(s // TM,), in_specs=[_row_spec(TM, D), _const_spec((1, D))],
        out_specs=_row_spec(TM, D), out_shape=jax.ShapeDtypeStruct((s, D), BF16),
        compiler_params=_cparams("arbitrary"),
    )(x, g1)


def _inproj_fwd(hn1, win_t):
    s = hn1.shape[0]
    nd = len(DILS)

    def body(hn_ref, w_ref, *rest):
        qkv_refs = rest[:3 * nd]
        u_ref, z_ref, scr = rest[3 * nd:]
        hn = hn_ref[...]
        for t in range(3):
            seg = _dot_nt(hn, w_ref[t * A:(t + 1) * A, :])
            seg = seg * SCALE if t == 0 else seg
            _fill_cols(scr, seg)
            for di, dil in enumerate(DILS):
                if dil == 1:
                    qkv_refs[t * nd + di][0] = seg.astype(BF16)
                else:
                    _split_residues(scr, qkv_refs[t * nd + di], dil)
        u_ref[...] = _dot_nt(hn, w_ref[3 * A:3 * A + GW, :])
        z_ref[...] = _dot_nt(hn, w_ref[3 * A + GW:INW, :])

    res = pl.pallas_call(
        body, name="inproj_fwd", grid=(s // TM,),
        in_specs=[_row_spec(TM, D), _const_spec((INW, D))],
        out_specs=[_res_spec(d, TM, A) for _ in range(3) for d in DILS] + [_row_spec(TM, GW), _row_spec(TM, GW)],
        out_shape=[_res_shape(s, d, A, BF16) for _ in range(3) for d in DILS]
                  + [jax.ShapeDtypeStruct((s, GW), F32)] * 2,
        scratch_shapes=[_col_scratch(TM, A)],
        compiler_params=_cparams("arbitrary"),
    )(hn1, win_t)
    q, k, v = (res[t * nd:(t + 1) * nd] for t in range(3))
    return q, k, v, res[-2], res[-1]


def _att_geometry(s, dil):
    length = s // dil
    rows = min(length, ATT_ROWS)
    return length, rows, length // rows, rows // CHUNK


def _stack_heads(t):
    lane = lax.broadcasted_iota(jnp.int32, t.shape, 1)
    zero = jnp.zeros_like(t)
    return jnp.concatenate([jnp.where(lane < DH, t, zero), jnp.where(lane >= DH, t, zero)], axis=0)


def _head_cols(t, hp):
    lane = lax.broadcasted_iota(jnp.int32, t.shape, 1)
    cols = [jnp.sum(jnp.where(lane == 2 * hp + h, t, 0.0), axis=-1, keepdims=True) for h in range(2)]
    return jnp.concatenate(cols, axis=0)


def _unstack_heads(t2):
    n = t2.shape[0] // 2
    lane = lax.broadcasted_iota(jnp.int32, (n, LANES), 1)
    return jnp.where(lane < DH, t2[:n], t2[n:])


def _query_window_bias(s0, s1, dil, first):
    row = lax.broadcasted_iota(jnp.int32, (2 * CHUNK, 2 * CHUNK), 0)
    col = lax.broadcasted_iota(jnp.int32, (2 * CHUNK, 2 * CHUNK), 1)
    steps = (row & (CHUNK - 1)) + CHUNK - col
    valid = (steps >= 0) & (steps <= CHUNK)
    if first:
        valid = valid & (col >= CHUNK)
    slope = jnp.where(row < CHUNK, s0, s1)
    return jnp.where(valid, -slope * (steps * dil).astype(F32), NEG)


def _key_block_bias(s0, s1, dil, last):
    key = lax.broadcasted_iota(jnp.int32, (CHUNK, 4 * CHUNK), 0)
    col = lax.broadcasted_iota(jnp.int32, (CHUNK, 4 * CHUNK), 1)
    wq = col & (2 * CHUNK - 1)
    steps = wq - key
    valid = (steps >= 0) & (steps <= CHUNK)
    if last:
        valid = valid & (wq < CHUNK)
    slope = jnp.where(col < 2 * CHUNK, s0, s1)
    return jnp.where(valid, -slope * (steps * dil).astype(F32), NEG)


def _head_rows(t, hp):
    row = lax.broadcasted_iota(jnp.int32, (8, LANES), 0)
    lane = lax.broadcasted_iota(jnp.int32, (8, LANES), 1)
    pick = jnp.where((row < 2) & (lane == 2 * hp + row), 1.0, 0.0).astype(BF16)
    hi = t.astype(BF16)
    rest = t - hi.astype(F32)
    mid = rest.astype(BF16)
    low = (rest - mid.astype(F32)).astype(BF16)
    return _dot_nt(pick, hi) + _dot_nt(pick, mid) + _dot_nt(pick, low)


def _att_specs(dil, rows, nsub, nblk):
    main = pl.BlockSpec((None, rows, LANES), lambda r, c, hp: (r, c, hp))
    prev = pl.BlockSpec((None, CHUNK, LANES), lambda r, c, hp: (r, jnp.maximum(c * nsub - 1, 0), hp))
    nxt = pl.BlockSpec((None, CHUNK, LANES), lambda r, c, hp: (r, jnp.minimum((c + 1) * nsub, nblk - 1), hp))
    main_heads = pl.BlockSpec((None, rows, LANES), lambda r, c, hp: (r, c, 0))
    nxt_heads = pl.BlockSpec((None, CHUNK, LANES), lambda r, c, hp: (r, jnp.minimum((c + 1) * nsub, nblk - 1), 0))
    return main, prev, nxt, main_heads, nxt_heads


def _row_start(i):
    return i * CHUNK if isinstance(i, int) else pl.multiple_of(i * CHUNK, CHUNK)


def _attn_fwd(q, k, v, slopes, dil):
    length = q.shape[1]
    _, rows, nch, nsub = _att_geometry(length * dil, dil)
    main, prev, _, main_heads, _ = _att_specs(dil, rows, nsub, length // CHUNK)

    def body(sl_ref, q_ref, k_ref, v_ref, kh_ref, vh_ref, o_ref, lse_ref, kbuf, vbuf, bias_buf):
        ch = pl.program_id(1)
        hp = pl.program_id(2)
        lane = lax.broadcasted_iota(jnp.int32, (CHUNK, LANES), 1)
        kbuf[0:CHUNK, :] = kh_ref[...]
        kbuf[CHUNK:, :] = k_ref[...]
        vbuf[0:CHUNK, :] = vh_ref[...]
        vbuf[CHUNK:, :] = v_ref[...]
        s0, s1 = sl_ref[2 * hp], sl_ref[2 * hp + 1]

        def block(i, bias):
            row = _row_start(i)
            rs = pl.ds(row, CHUNK)
            q2 = _stack_heads(q_ref[rs, :])
            kw = kbuf[pl.ds(row, 2 * CHUNK), :]
            vw = vbuf[pl.ds(row, 2 * CHUNK), :]
            sc = _dot_nt(q2, kw) + bias
            m = jnp.max(sc, axis=-1, keepdims=True)
            p = jnp.exp(sc - m)
            l = jnp.sum(p, axis=-1, keepdims=True)
            o2 = _dot(p.astype(BF16), vw) * (1.0 / l)
            o_ref[rs, :] = _unstack_heads(o2).astype(BF16)
            lse = m + jnp.log(l)
            seen = jnp.where(hp == 0, 0.0, lse_ref[rs, :])
            lse_ref[rs, :] = jnp.where(lane == 2 * hp, lse[:CHUNK], jnp.where(lane == 2 * hp + 1, lse[CHUNK:], seen))

        bias_buf[...] = _query_window_bias(s0, s1, dil, False)

        @pl.when(ch == 0)
        def _():
            block(0, _query_window_bias(s0, s1, dil, True))

        @pl.when(ch != 0)
        def _():
            block(0, bias_buf[...])

        for i in range(1, nsub):
            block(i, bias_buf[...])

    sd = jax.ShapeDtypeStruct
    return pl.pallas_call(
        body, name=f"attn_fwd_d{dil}", grid=(dil, nch, NH // 2),
        in_specs=[pl.BlockSpec(memory_space=pltpu.SMEM), main, main, main, prev, prev],
        out_specs=[main, main_heads], out_shape=[sd((dil, length, A), BF16), sd((dil, length, LANES), F32)],
        scratch_shapes=[pltpu.VMEM((rows + CHUNK, LANES), BF16), pltpu.VMEM((rows + CHUNK, LANES), BF16),
                        pltpu.VMEM((2 * CHUNK, 2 * CHUNK), F32)],
        compiler_params=_cparams("arbitrary", "arbitrary", "arbitrary"),
    )(slopes, q, k, v, k, v)


def _attn_bwd_dq(q, k, v, do, lse, delta, slopes, dil):
    length = q.shape[1]
    _, rows, nch, nsub = _att_geometry(length * dil, dil)
    main, prev, _, main_heads, _ = _att_specs(dil, rows, nsub, length // CHUNK)

    def body(sl_ref, q_ref, k_ref, v_ref, do_ref, lse_ref, dl_ref, kh_ref, vh_ref, dq_ref, kbuf, vbuf, bias_buf):
        ch = pl.program_id(1)
        hp = pl.program_id(2)
        kbuf[0:CHUNK, :] = kh_ref[...]
        kbuf[CHUNK:, :] = k_ref[...]
        vbuf[0:CHUNK, :] = vh_ref[...]
        vbuf[CHUNK:, :] = v_ref[...]
        s0, s1 = sl_ref[2 * hp], sl_ref[2 * hp + 1]

        def block(i, bias):
            row = _row_start(i)
            rs = pl.ds(row, CHUNK)
            q2 = _stack_heads(q_ref[rs, :])
            do2 = _stack_heads(do_ref[rs, :])
            lse2 = _head_cols(lse_ref[rs, :], hp)
            dl2 = _head_cols(dl_ref[rs, :], hp)
            kw = kbuf[pl.ds(row, 2 * CHUNK), :]
            vw = vbuf[pl.ds(row, 2 * CHUNK), :]
            p = jnp.exp(_dot_nt(q2, kw) + bias - lse2)
            ds = p * (_dot_nt(do2, vw) - dl2)
            dq_ref[rs, :] = _unstack_heads(_dot(ds.astype(BF16), kw)).astype(BF16)

        bias_buf[...] = _query_window_bias(s0, s1, dil, False)

        @pl.when(ch == 0)
        def _():
            block(0, _query_window_bias(s0, s1, dil, True))

        @pl.when(ch != 0)
        def _():
            block(0, bias_buf[...])

        for i in range(1, nsub):
            block(i, bias_buf[...])

    return pl.pallas_call(
        body, name=f"attn_dq_d{dil}", grid=(dil, nch, NH // 2),
        in_specs=[pl.BlockSpec(memory_space=pltpu.SMEM), main, main, main, main, main_heads, main_heads, prev, prev],
        out_specs=main, out_shape=jax.ShapeDtypeStruct((dil, length, A), BF16),
        scratch_shapes=[pltpu.VMEM((rows + CHUNK, LANES), BF16), pltpu.VMEM((rows + CHUNK, LANES), BF16),
                        pltpu.VMEM((2 * CHUNK, 2 * CHUNK), F32)],
        compiler_params=_cparams("arbitrary", "arbitrary", "arbitrary"),
    )(slopes, q, k, v, do, lse, delta, k, v)


def _attn_bwd_dkv(q, k, v, do, lse, delta, slopes, dil):
    length = q.shape[1]
    _, rows, nch, nsub = _att_geometry(length * dil, dil)
    main, _, nxt, main_heads, nxt_heads = _att_specs(dil, rows, nsub, length // CHUNK)

    def body(sl_ref, k_ref, v_ref, q_ref, do_ref, lse_ref, dl_ref, qh_ref, doh_ref, lseh_ref, dlh_ref,
             dk_ref, dv_ref, qbuf, dobuf, lse_rows, dl_rows, bias_buf):
        ch = pl.program_id(1)
        hp = pl.program_id(2)
        for buf, main_ref, halo_ref in ((qbuf, q_ref, qh_ref), (dobuf, do_ref, doh_ref)):
            buf[0:rows, :] = main_ref[...]
            buf[rows:, :] = halo_ref[...]
        for buf, main_ref, halo_ref in ((lse_rows, lse_ref, lseh_ref), (dl_rows, dl_ref, dlh_ref)):
            buf[:, 0:rows] = _head_rows(main_ref[...], hp)
            buf[:, rows:] = _head_rows(halo_ref[...], hp)
        s0, s1 = sl_ref[2 * hp], sl_ref[2 * hp + 1]

        def block(i, bias):
            row = _row_start(i)
            rs = pl.ds(row, CHUNK)
            win = pl.ds(row, 2 * CHUNK)
            kc = k_ref[rs, :]
            vc = v_ref[rs, :]
            q2 = _stack_heads(qbuf[win, :])
            do2 = _stack_heads(dobuf[win, :])
            cols = slice(i * CHUNK, (i + 2) * CHUNK)
            lse2 = jnp.concatenate([lse_rows[0:1, cols], lse_rows[1:2, cols]], axis=1)
            dl2 = jnp.concatenate([dl_rows[0:1, cols], dl_rows[1:2, cols]], axis=1)
            pt = jnp.exp(_dot_nt(kc, q2) + bias - lse2)
            dst = pt * (_dot_nt(vc, do2) - dl2)
            dv_ref[rs, :] = _dot(pt.astype(BF16), do2).astype(BF16)
            dk_ref[rs, :] = _dot(dst.astype(BF16), q2).astype(BF16)

        bias_buf[...] = _key_block_bias(s0, s1, dil, False)

        for i in range(nsub - 1):
            block(i, bias_buf[...])

        @pl.when(ch != nch - 1)
        def _():
            block(nsub - 1, bias_buf[...])

        @pl.when(ch == nch - 1)
        def _():
            block(nsub - 1, _key_block_bias(s0, s1, dil, True))

    sd = jax.ShapeDtypeStruct((dil, length, A), BF16)
    return pl.pallas_call(
        body, name=f"attn_dkv_d{dil}", grid=(dil, nch, NH // 2),
        in_specs=[pl.BlockSpec(memory_space=pltpu.SMEM), main, main, main, main, main_heads, main_heads,
                  nxt, nxt, nxt_heads, nxt_heads],
        out_specs=[main, main], out_shape=[sd, sd],
        scratch_shapes=[pltpu.VMEM((rows + CHUNK, LANES), BF16), pltpu.VMEM((rows + CHUNK, LANES), BF16),
                        pltpu.VMEM((8, rows + CHUNK), F32), pltpu.VMEM((8, rows + CHUNK), F32),
                        pltpu.VMEM((CHUNK, 4 * CHUNK), F32)],
        compiler_params=_cparams("arbitrary", "arbitrary", "arbitrary"),
    )(slopes, k, v, q, do, lse, delta, q, do, lse, delta)


def _group_masks(width):
    lane = lax.broadcasted_iota(jnp.int32, (1, width), 1)
    return [(lane >= g * DH) & (lane < (g + 1) * DH) for g in range(width // DH)]


def _group_mean_matrix():
    i = lax.broadcasted_iota(jnp.int32, (GW, GW), 0) // DH
    j = lax.broadcasted_iota(jnp.int32, (GW, GW), 1) // DH
    return jnp.where(i == j, 1.0 / DH, 0.0).astype(F32)


def _tri_mask(lower):
    t = lax.broadcasted_iota(jnp.int32, (CHUNK, CHUNK), 0)
    u = lax.broadcasted_iota(jnp.int32, (CHUNK, CHUNK), 1)
    return (u <= t) if lower else (u >= t)


def _sgu_forward(u, z, lng, lnb, w_ref, bias_t, pmat, rows):
    ug = _gelu(u)
    zg = _gelu(z)
    mu = _dot_hi(zg, pmat)
    zc = zg - mu
    var = _dot_hi(zc * zc, pmat)
    rstd = lax.rsqrt(var + EPS)
    zhat = zc * rstd
    zn = (zhat * lng + lnb).astype(BF16)
    gm = _group_masks(GW)
    tri = _tri_mask(True)
    ws = [jnp.where(tri, w_ref[g], 0.0).astype(BF16) for g in range(NG)]
    pieces = []
    for c in range(rows // CHUNK):
        znc = zn[c * CHUNK:(c + 1) * CHUNK, :]
        mix = None
        for g in range(NG):
            part = jnp.where(gm[g], _dot(ws[g], znc), 0.0)
            mix = part if mix is None else mix + part
        pieces.append(mix + bias_t)
    mixed = jnp.concatenate(pieces, axis=0) if len(pieces) > 1 else pieces[0]
    return ug * mixed, ug, zhat, rstd, zn, mixed


def _head_spread():
    h = lax.broadcasted_iota(jnp.int32, (LANES, A), 0)
    lane = lax.broadcasted_iota(jnp.int32, (LANES, A), 1)
    return jnp.where(lane // DH == h, 1.0, 0.0).astype(BF16)


def _bf16_pieces(t, n):
    pieces = []
    for _ in range(n):
        piece = t.astype(BF16)
        pieces.append(piece)
        t = t - piece.astype(F32)
    return pieces


def _mix_fwd(os_, ls_, u, z, x, lng, lnb, sgu_w, bias_t, ga, gg, wout):
    s = x.shape[0]
    nd = len(DILS)
    nscr = sum(1 for d in DILS if d > 1)

    def body(*refs):
        o_refs, l_refs = refs[:nd], refs[nd:2 * nd]
        u_ref, z_ref, x_ref, lng_ref, lnb_ref, w_ref, bt_ref, ga_ref, gg_ref, wo_ref = refs[2 * nd:2 * nd + 10]
        attn_ref = refs[2 * nd + 10]
        lse_refs = refs[2 * nd + 11:3 * nd + 11]
        mixed_ref, h1_ref = refs[3 * nd + 11:3 * nd + 13]
        scr = refs[3 * nd + 13:]
        scr_o, scr_l, scr_lse = scr[:nscr], scr[nscr:2 * nscr], scr[2 * nscr]
        ov, lv, j = [], [], 0
        for di, dil in enumerate(DILS):
            if dil == 1:
                ov.append(o_refs[di][0].astype(F32))
                lv.append(l_refs[di][0])
            else:
                ov.append(_merge_residues(o_refs[di], scr_o[j], dil))
                lv.append(_merge_residues(l_refs[di], scr_l[j], dil))
                j += 1
        mx = functools.reduce(jnp.maximum, lv)
        es = [jnp.exp(l - mx) for l in lv]
        den = functools.reduce(lambda a, b: a + b, es)
        spread = _head_spread()
        attn = None
        for e, o in zip(es, ov):
            wide = functools.reduce(lambda a, b: a + b, [_dot(piece, spread) for piece in _bf16_pieces(e / den, 2)])
            attn = wide * o if attn is None else attn + wide * o
        attn_ref[...] = attn
        lse = mx + jnp.log(den)
        _fill_cols(scr_lse, lse)
        for di, dil in enumerate(DILS):
            if dil == 1:
                lse_refs[di][0] = lse
            else:
                _split_residues(scr_lse, lse_refs[di], dil)
        an, _, _ = _rms_fwd(attn, ga_ref[...])
        gmv, _, _, _, _, _ = _sgu_forward(u_ref[...], z_ref[...], lng_ref[...], lnb_ref[...], w_ref,
                                          bt_ref[...], _group_mean_matrix(), TMX)
        gn, _, _ = _rms_fwd(gmv, gg_ref[...])
        mixed = jnp.concatenate([an, gn], axis=-1).astype(BF16)
        mixed_ref[...] = mixed
        h1_ref[...] = x_ref[...] + _dot(mixed, wo_ref[...])

    sd = jax.ShapeDtypeStruct
    res = pl.pallas_call(
        body, name="mix_fwd", grid=(s // TMX,),
        in_specs=[_res_spec(d, TMX, A) for d in DILS] + [_res_spec(d, TMX, LANES) for d in DILS]
                 + [_row_spec(TMX, GW), _row_spec(TMX, GW),
                    _row_spec(TMX, D), _const_spec((1, GW)), _const_spec((1, GW)), _const_spec((NG, CHUNK, CHUNK)),
                    _const_spec((CHUNK, GW)), _const_spec((1, A)), _const_spec((1, GW)), _const_spec((D, D))],
        out_specs=[_row_spec(TMX, A)] + [_res_spec(d, TMX, LANES) for d in DILS]
                  + [_row_spec(TMX, D), _row_spec(TMX, D)],
        out_shape=[sd((s, A), F32)] + [_res_shape(s, d, LANES, F32) for d in DILS]
                  + [sd((s, D), BF16), sd((s, D), F32)],
        scratch_shapes=[_col_scratch(TMX, A)] * nscr + [_col_scratch(TMX, LANES)] * (nscr + 1),
        compiler_params=_cparams("arbitrary"),
    )(*os_, *ls_, u, z, x, lng, lnb, sgu_w, bias_t, ga, gg, wout)
    return res[0], res[1:1 + nd], res[1 + nd], res[2 + nd]


def _mlp_fwd(h1, g2, wff1, wff2, gf, target):
    s = h1.shape[0]

    def body(h1_ref, g2_ref, w1_ref, w2_ref, gf_ref, t_ref, hn_ref, rf_ref, dh2_ref, loss_ref, dgf_ref):
        i = pl.program_id(0)
        h1v = h1_ref[...]
        hn, _, _ = _rms_fwd(h1v, g2_ref[...])
        hn = hn.astype(BF16)
        hn_ref[...] = hn
        acc = h1v
        for j in range(DFF // FF_CH):
            cols = slice(j * FF_CH, (j + 1) * FF_CH)
            rf = jnp.maximum(_dot(hn, w1_ref[j]), 0.0)
            act = (rf * rf).astype(BF16)
            rf_ref[:, cols] = rf.astype(BF16)
            acc = acc + _dot(act, w2_ref[cols, :])
        y, h2n, r3 = _rms_fwd(acc, gf_ref[...])
        err = y - t_ref[...]
        part = 0.5 * jnp.sum(jnp.mean(err * err, axis=-1, keepdims=True), axis=0, keepdims=True)
        dy = err * (1.0 / D)
        dh2, dgf = _rms_bwd(dy, h2n, r3, gf_ref[...])
        dh2_ref[...] = dh2

        @pl.when(i == 0)
        def _():
            loss_ref[...] = jnp.zeros_like(loss_ref)
            dgf_ref[...] = jnp.zeros_like(dgf_ref)

        loss_ref[...] += jnp.broadcast_to(part, loss_ref.shape)
        dgf_ref[...] += dgf

    sd = jax.ShapeDtypeStruct
    return pl.pallas_call(
        body, name="mlp_fwd", grid=(s // TM,),
        in_specs=[_row_spec(TM, D), _const_spec((1, D)), _const_spec((DFF // FF_CH, D, FF_CH)), _const_spec((DFF, D)),
                  _const_spec((1, D)), _row_spec(TM, D)],
        out_specs=[_row_spec(TM, D), _row_spec(TM, DFF), _row_spec(TM, D),
                   _const_spec((1, LANES)), _const_spec((1, D))],
        out_shape=[sd((s, D), BF16), sd((s, DFF), BF16), sd((s, D), F32),
                   sd((1, LANES), F32), sd((1, D), F32)],
        compiler_params=_cparams("arbitrary"),
    )(h1, g2, wff1, wff2, gf, target)


def _mlp_bwd(dh2, rf, h1, g2, wff1, wff2):
    s = h1.shape[0]

    def body(dh2_ref, rf_ref, h1_ref, g2_ref, w1_ref, w2_ref, df_ref, dh1_ref, dg2_ref):
        i = pl.program_id(0)
        dh2v = dh2_ref[...]
        dh2b = dh2v.astype(BF16)
        dhn = jnp.zeros((TM, D), F32)
        for j in range(DFF // FF_CH):
            cols = slice(j * FF_CH, (j + 1) * FF_CH)
            da = _dot_nt(dh2b, w2_ref[cols, :])
            df = (da * (2.0 * rf_ref[:, cols].astype(F32))).astype(BF16)
            df_ref[:, cols] = df
            dhn = dhn + _dot_nt(df, w1_ref[j])
        _, h1n, r2 = _rms_fwd(h1_ref[...], g2_ref[...])
        dres, dg2 = _rms_bwd(dhn, h1n, r2, g2_ref[...])
        dh1_ref[...] = dh2v + dres

        @pl.when(i == 0)
        def _():
            dg2_ref[...] = jnp.zeros_like(dg2_ref)

        dg2_ref[...] += dg2

    sd = jax.ShapeDtypeStruct
    return pl.pallas_call(
        body, name="mlp_bwd", grid=(s // TM,),
        in_specs=[_row_spec(TM, D), _row_spec(TM, DFF), _row_spec(TM, D), _const_spec((1, D)),
                  _const_spec((DFF // FF_CH, D, FF_CH)), _const_spec((DFF, D))],
        out_specs=[_row_spec(TM, DFF), _row_spec(TM, D), _const_spec((1, D))],
        out_shape=[sd((s, DFF), BF16), sd((s, D), F32), sd((1, D), F32)],
        compiler_params=_cparams("arbitrary"),
    )(dh2, rf, h1, g2, wff1, wff2)


def _mix_bwd(dh1, attn, u, z, lng, lnb, sgu_w, sgu_wt, bias_t, ga, gg, wout):
    s = dh1.shape[0]
    nsteps = s // TMX
    nd = len(DILS)

    def body(*refs):
        dh1_ref, attn_ref, u_ref, z_ref, lng_ref, lnb_ref, w_ref, wt_ref, bt_ref, ga_ref, gg_ref, wo_ref = refs[:12]
        do_refs, dl_refs = refs[12:12 + nd], refs[12 + nd:12 + 2 * nd]
        (du_ref, dz_ref, dga_ref, dgg_ref, dlng_ref, dlnb_ref, dws_ref, db_ref,
         dbt_acc, scr_do, scr_dl) = refs[12 + 2 * nd:]
        i = pl.program_id(0)

        @pl.when(i == 0)
        def _():
            for r in (dga_ref, dgg_ref, dlng_ref, dlnb_ref, dws_ref, db_ref, dbt_acc):
                r[...] = jnp.zeros_like(r)

        dmixed = _dot_nt(dh1_ref[...].astype(BF16), wo_ref[...])
        attn = attn_ref[...]
        _, an, ra = _rms_fwd(attn, ga_ref[...])
        dattn, dga = _rms_bwd(dmixed[:, :A], an, ra, ga_ref[...])
        dga_ref[...] += dga
        _fill_cols(scr_do, dattn)
        spread = _head_spread()
        delta = functools.reduce(lambda a, b: a + b, [_dot_nt(piece, spread) for piece in _bf16_pieces(dattn * attn, 3)])
        _fill_cols(scr_dl, delta)
        for di, dil in enumerate(DILS):
            if dil == 1:
                do_refs[di][0] = dattn.astype(BF16)
                dl_refs[di][0] = delta
            else:
                _split_residues(scr_do, do_refs[di], dil)
                _split_residues(scr_dl, dl_refs[di], dil)
        pmat = _group_mean_matrix()
        lng = lng_ref[...]
        uv, zv = u_ref[...], z_ref[...]
        gmv, ug, zhat, rstd, zn, mixed = _sgu_forward(uv, zv, lng, lnb_ref[...], w_ref, bt_ref[...], pmat, TMX)
        _, gmn, rg = _rms_fwd(gmv, gg_ref[...])
        dgm, dgg = _rms_bwd(dmixed[:, A:], gmn, rg, gg_ref[...])
        dgg_ref[...] += dgg
        du_ref[...] = dgm * mixed * _gelu_grad(uv)
        dmx = dgm * ug
        dmxb = dmx.astype(BF16)
        gm = _group_masks(GW)
        tri_t = _tri_mask(False)
        wst = [jnp.where(tri_t, wt_ref[g], 0.0).astype(BF16) for g in range(NG)]
        zero = jnp.zeros((CHUNK, GW), BF16)
        dzn_pieces = []
        for c in range(TMX // CHUNK):
            rs = slice(c * CHUNK, (c + 1) * CHUNK)
            dmc = dmxb[rs, :]
            znc = zn[rs, :]
            dbt_acc[...] += dmx[rs, :]
            dzn = None
            for g in range(NG):
                dws_ref[g] += _dot_nt(jnp.where(gm[g], dmc, zero), znc)
                part = jnp.where(gm[g], _dot(wst[g], dmc), 0.0)
                dzn = part if dzn is None else dzn + part
            dzn_pieces.append(dzn)
        dzn = jnp.concatenate(dzn_pieces, axis=0)
        dlng_ref[...] += jnp.sum(dzn * zhat, axis=0, keepdims=True)
        dlnb_ref[...] += jnp.sum(dzn, axis=0, keepdims=True)
        dzh = dzn * lng
        dzg = rstd * (dzh - _dot_hi(dzh, pmat) - zhat * _dot_hi(dzh * zhat, pmat))
        dz_ref[...] = dzg * _gelu_grad(zv)

        @pl.when(i == nsteps - 1)
        def _():
            tri = _tri_mask(True)
            for g in range(NG):
                dws_ref[g] = jnp.where(tri, dws_ref[g], 0.0)
            acc = dbt_acc[...]
            lane = lax.broadcasted_iota(jnp.int32, (CHUNK, LANES), 1)
            out = jnp.zeros((CHUNK, LANES), F32)
            for g in range(NG):
                sg = jnp.sum(jnp.where(gm[g], acc, 0.0), axis=-1, keepdims=True)
                out = jnp.where(lane == g, sg, out)
            db_ref[...] = out

    sd = jax.ShapeDtypeStruct
    res = pl.pallas_call(
        body, name="mix_bwd", grid=(nsteps,),
        in_specs=[_row_spec(TMX, D), _row_spec(TMX, A), _row_spec(TMX, GW), _row_spec(TMX, GW),
                  _const_spec((1, GW)), _const_spec((1, GW)), _const_spec((NG, CHUNK, CHUNK)),
                  _const_spec((NG, CHUNK, CHUNK)), _const_spec((CHUNK, GW)), _const_spec((1, A)),
                  _const_spec((1, GW)), _const_spec((D, D))],
        out_specs=[_res_spec(d, TMX, A) for d in DILS] + [_res_spec(d, TMX, LANES) for d in DILS]
                  + [_row_spec(TMX, GW), _row_spec(TMX, GW),
                   _const_spec((1, A)), _const_spec((1, GW)), _const_spec((1, GW)), _const_spec((1, GW)),
                   _const_spec((NG, CHUNK, CHUNK)), _const_spec((CHUNK, LANES))],
        out_shape=[_res_shape(s, d, A, BF16) for d in DILS] + [_res_shape(s, d, LANES, F32) for d in DILS]
                  + [sd((s, GW), F32), sd((s, GW), F32),
                   sd((1, A), F32), sd((1, GW), F32), sd((1, GW), F32), sd((1, GW), F32),
                   sd((NG, CHUNK, CHUNK), F32), sd((CHUNK, LANES), F32)],
        scratch_shapes=[pltpu.VMEM((CHUNK, GW), F32), _col_scratch(TMX, A), _col_scratch(TMX, LANES)],
        compiler_params=_cparams("arbitrary"),
    )(dh1, attn, u, z, lng, lnb, sgu_w, sgu_wt, bias_t, ga, gg, wout)
    return (res[:nd], res[nd:2 * nd]) + tuple(res[2 * nd:])


def _dproj_merge(dqs, dks, dvs, du, dz, pin):
    s = du.shape[0]
    nd = len(DILS)
    nscr = sum(1 for d in DILS if d > 1)

    def body(*refs):
        pin_ref = refs[0]
        parts = [refs[1 + t * nd:1 + (t + 1) * nd] for t in range(3)]
        du_ref, dz_ref, dp_ref = refs[1 + 3 * nd:4 + 3 * nd]
        scr = refs[4 + 3 * nd:]
        sums = []
        for t in range(3):
            total, j = None, 0
            for di, dil in enumerate(DILS):
                if dil == 1:
                    term = parts[t][di][0].astype(F32)
                else:
                    term = _merge_residues(parts[t][di], scr[t * nscr + j], dil)
                    j += 1
                total = term if total is None else total + term
            sums.append(total)
        dp_ref[...] = jnp.concatenate([sums[0] * SCALE, sums[1], sums[2], du_ref[...] + pin_ref[0, 0], dz_ref[...]],
                                      axis=-1).astype(BF16)

    return pl.pallas_call(
        body, name="dproj_merge", grid=(s // TMX,),
        in_specs=[pl.BlockSpec(memory_space=pltpu.SMEM)] + [_res_spec(d, TMX, A) for d in DILS] * 3
                 + [_row_spec(TMX, GW)] * 2,
        out_specs=_row_spec(TMX, INW), out_shape=jax.ShapeDtypeStruct((s, INW), BF16),
        scratch_shapes=[_col_scratch(TMX, A)] * (3 * nscr),
        compiler_params=_cparams("arbitrary"),
    )(pin, *dqs, *dks, *dvs, du, dz)


def _inproj_bwd(dproj, dh1, x, g1, win_t):
    s = x.shape[0]

    def body(dp_ref, dh1_ref, x_ref, g_ref, w_ref, dx_ref, dg_ref):
        i = pl.program_id(0)
        dhn = _dot(dp_ref[...], w_ref[...])
        _, xn, r1 = _rms_fwd(x_ref[...], g_ref[...])
        dres, dg = _rms_bwd(dhn, xn, r1, g_ref[...])
        dx_ref[...] = dh1_ref[...] + dres

        @pl.when(i == 0)
        def _():
            dg_ref[...] = jnp.zeros_like(dg_ref)

        dg_ref[...] += dg

    sd = jax.ShapeDtypeStruct
    return pl.pallas_call(
        body, name="inproj_bwd", grid=(s // TM,),
        in_specs=[_row_spec(TM, INW), _row_spec(TM, D), _row_spec(TM, D), _const_spec((1, D)), _const_spec((INW, D))],
        out_specs=[_row_spec(TM, D), _const_spec((1, D))],
        out_shape=[sd((s, D), F32), sd((1, D), F32)],
        compiler_params=_cparams("arbitrary"),
    )(dproj, dh1, x, g1, win_t)


def _wgrad(a, b, name, bm, bn, bk=2 * TM, square_a=False):
    s, m = a.shape
    n = b.shape[1]
    bm, bn = min(bm, m), min(bn, n)

    def body(a_ref, b_ref, o_ref):
        @pl.when(pl.program_id(2) == 0)
        def _():
            o_ref[...] = jnp.zeros_like(o_ref)

        av = a_ref[...]
        if square_a:
            av = av.astype(F32)
            av = av * av
        o_ref[...] += _dot_tn(av.astype(BF16), b_ref[...].astype(BF16))

    return pl.pallas_call(
        body, name=name, grid=(m // bm, n // bn, s // bk),
        in_specs=[pl.BlockSpec((bk, bm), lambda i, j, k: (k, i)), pl.BlockSpec((bk, bn), lambda i, j, k: (k, j))],
        out_specs=pl.BlockSpec((bm, bn), lambda i, j, k: (i, j)),
        out_shape=jax.ShapeDtypeStruct((m, n), F32),
        compiler_params=_cparams("arbitrary", "arbitrary", "arbitrary"),
    )(a, b)


def _adamw_math(w, g, m, v):
    m = B1 * m + (1.0 - B1) * g
    v = B2 * v + (1.0 - B2) * (g * g)
    m_hat = m / (1.0 - B1 ** STEP)
    v_hat = v / (1.0 - B2 ** STEP)
    delta = -LR * (m_hat / (jnp.sqrt(v_hat) + AEPS) + WD * w)
    return delta, m, v


def _adamw(w, g, m, v, name):
    rows, cols = w.shape
    br = min(rows, 256)
    while rows % br:
        br -= 8

    def body(w_ref, g_ref, m_ref, v_ref, d_ref, mo_ref, vo_ref):
        d, mn, vn = _adamw_math(w_ref[...], g_ref[...], m_ref[...], v_ref[...])
        d_ref[...] = d
        mo_ref[...] = mn
        vo_ref[...] = vn

    spec = _row_spec(br, cols)
    sd = jax.ShapeDtypeStruct((rows, cols), F32)
    return pl.pallas_call(
        body, name=name, grid=(rows // br,), in_specs=[spec] * 4, out_specs=[spec] * 3,
        out_shape=[sd, sd, sd], compiler_params=_cparams("arbitrary"),
    )(w, g, m, v)


def _local_step(x, hn1, target, small, win_t, rest_weights, early_grads=None, after_attention_bwd=None,
                late_grads=None):
    slopes = jnp.asarray(_alibi_slopes(NH))
    q, k, v, u, z = _inproj_fwd(hn1, win_t)
    outs, lses = [], []
    for i, dil in enumerate(DILS):
        o, l = _attn_fwd(q[i], k[i], v[i], slopes, dil)
        outs.append(o)
        lses.append(l)
    wout, wff1, wff2 = rest_weights(lses[-1])
    attn, lse, mixed, h1 = _mix_fwd(outs, lses, u, z, x, small["ln_g"], small["ln_b"], small["sgu_w"],
                                    small["bias_t"], small["attn_out_g"], small["gmlp_out_g"], wout)
    hn2, rf, dh2, loss, dgf = _mlp_fwd(h1, small["norm2_g"], wff1, wff2, small["final_norm_g"], target)
    df, dh1, dg2 = _mlp_bwd(dh2, rf, h1, small["norm2_g"], wff1, wff2)
    gwff1 = _wgrad(hn2, df, "wgrad_ff1", D, 1024)
    gwff2 = _wgrad(rf, dh2, "wgrad_ff2", 1024, D, square_a=True)
    gwout = _wgrad(mixed, dh1, "wgrad_out", D, D)
    ga, g1 = small["attn_out_g"], small["norm1_g"]
    pin = early_grads(gwff1, gwff2, gwout) if early_grads else None
    if pin is not None:
        ga = ga + pin
    (do, delta, du, dz, dga, dgg, dlng, dlnb, dws, db) = _mix_bwd(
        dh1, attn, u, z, small["ln_g"], small["ln_b"], small["sgu_w"], small["sgu_wt"], small["bias_t"],
        ga, small["gmlp_out_g"], wout)
    dqs, dks, dvs = [], [], []
    for i, dil in enumerate(DILS):
        dqs.append(_attn_bwd_dq(q[i], k[i], v[i], do[i], lse[i], delta[i], slopes, dil))
        dk, dv = _attn_bwd_dkv(q[i], k[i], v[i], do[i], lse[i], delta[i], slopes, dil)
        dks.append(dk)
        dvs.append(dv)
    marker = functools.reduce(lambda a, b: a + b, [t[0, 0:8, 0:LANES] for t in dqs + dks + dvs])
    pin = after_attention_bwd(marker) if after_attention_bwd else None
    dproj = _dproj_merge(dqs, dks, dvs, du, dz, jnp.zeros((1, 1), F32) if pin is None else pin)
    gwin_t = _wgrad(dproj, hn1, "wgrad_in", INW // 2, D)
    pin = late_grads(gwin_t) if late_grads else None
    if pin is not None:
        g1 = g1 + pin
    dx, dg1 = _inproj_bwd(dproj, dh1, x, g1, win_t)
    small_grads = dict(norm1_g=dg1, ln_g=dlng, ln_b=dlnb, sgu_w=dws, sgu_b=db[:, :NG].T,
                       attn_out_g=dga, gmlp_out_g=dgg, norm2_g=dg2, final_norm_g=dgf)
    return loss[0, 0], dx, small_grads, (gwin_t, gwout, gwff1, gwff2)


ANY = pl.BlockSpec(memory_space=pl.ANY)
NDEV = 8


def _position():
    return lax.axis_index("x"), lax.axis_index("y"), lax.axis_index("c")


def _other_chips(x, y):
    return [(1 - x, y), (x, 1 - y), (1 - x, 1 - y)]


def _remote(src, dst, send_sem, recv_sem, device):
    return pltpu.make_async_remote_copy(src_ref=src, dst_ref=dst, send_sem=send_sem, recv_sem=recv_sem,
                                        device_id=device, device_id_type=MESH)


HBM = pl.BlockSpec(memory_space=pltpu.HBM)
SEM = pl.BlockSpec(memory_space=pltpu.SEMAPHORE)
DATAFLOW = pltpu.SideEffectType.DATAFLOW_SIDE_EFFECTING


def _in_hbm(a):
    return pltpu.with_memory_space_constraint(a, pltpu.HBM)


def _gather_start(shards, name):
    n = len(shards)
    lands = [jnp.broadcast_to(sh[None], (NCHIP,) + sh.shape) for sh in shards]

    def body(*refs):
        w_refs, land_refs = refs[:n], refs[n:2 * n]
        send_sems, recv_sems = refs[2 * n:2 * n + 2]
        token = refs[-1]
        x, y, c = _position()
        for w in range(n):
            for k, (px, py) in enumerate(_other_chips(x, y)):
                m = 3 * w + k
                _remote(w_refs[w], land_refs[w].at[2 * x + y], send_sems.at[m], recv_sems.at[m], (px, py, c)).start()
        token[...] = jnp.zeros_like(token)

    res = _split_call(body, name, list(shards) + lands, (3 * n, 3 * n), (TOKEN,))
    return res[0], res[1], res[2:2 + n], res[2 + n:2 + 2 * n], res[-1]


def _gather_wait(send_sems, recv_sems, shards, lands, after, name):
    n = len(shards)

    def body(*refs):
        w_refs, land_refs = refs[:n], refs[n:2 * n]
        send_sems, recv_sems = refs[2 * n:2 * n + 2]
        x, y, c = _position()
        for w in range(n):
            for k, (px, py) in enumerate(_other_chips(x, y)):
                m = 3 * w + k
                cp = _remote(w_refs[w], land_refs[w].at[2 * px + py], send_sems.at[m], recv_sems.at[m], (px, py, c))
                cp.wait_send()
                cp.wait_recv()

    operands = list(shards) + list(lands)
    res = pl.pallas_call(
        body, name=name, out_shape=tuple(pltpu.HBM(a.shape, a.dtype) for a in operands),
        in_specs=(HBM,) * (2 * n) + (SEM, SEM, ANY), out_specs=(HBM,) * (2 * n),
        input_output_aliases={i: i for i in range(2 * n)},
        compiler_params=pltpu.CompilerParams(has_side_effects=DATAFLOW),
    )(*operands, send_sems, recv_sems, after)
    return res[n:]


def _xor_peers(x, y, c):
    peers = []
    for k in range(1, NDEV):
        kx, ky, kc = (k >> 2) & 1, (k >> 1) & 1, k & 1
        peers.append((1 - x if kx else x, 1 - y if ky else y, 1 - c if kc else c))
    return peers


def _piece(part_ref, px, py, pc):
    slab = 2 * px + py
    if len(part_ref.shape) == 3:
        half = part_ref.shape[1] // 2
        return part_ref.at[slab, pl.ds(pc * half, half), :]
    half = part_ref.shape[0] // 2
    return part_ref.at[pl.ds(pc * half, half), pl.ds(pl.multiple_of(slab * D, D), D)]


def _split_call(body, name, operands, n_sems, extra_out=()):
    n = len(operands)
    sems = tuple(pltpu.SemaphoreType.DMA((m,)) for m in n_sems)
    thru = tuple(pltpu.HBM(a.shape, a.dtype) for a in operands)
    return pl.pallas_call(
        body, name=name, out_shape=sems + thru + tuple(extra_out),
        in_specs=(HBM,) * n,
        out_specs=(SEM,) * len(sems) + (HBM,) * n + (pl.BlockSpec(memory_space=pltpu.VMEM),) * len(extra_out),
        input_output_aliases={i: len(sems) + i for i in range(n)},
        compiler_params=pltpu.CompilerParams(has_side_effects=DATAFLOW),
    )(*[_in_hbm(a) for a in operands])


TOKEN = jax.ShapeDtypeStruct((8, LANES), F32)


def _reduce_start(parts, name):
    nw = len(parts)
    lands = [lax.empty((NDEV - 1, p.shape[-2] // 2, D), F32) for p in parts]

    def body(*refs):
        part_refs, land_refs = refs[:nw], refs[nw:2 * nw]
        send_sems, recv_sems = refs[2 * nw:2 * nw + 2]
        token = refs[-1]
        x, y, c = _position()
        for w in range(nw):
            for k, peer in enumerate(_xor_peers(x, y, c)):
                n = w * (NDEV - 1) + k
                _remote(_piece(part_refs[w], *peer), land_refs[w].at[k], send_sems.at[n], recv_sems.at[n],
                        peer).start()
        token[...] = jnp.zeros_like(token)

    n = nw * (NDEV - 1)
    res = _split_call(body, name, list(parts) + lands, (n, n), (TOKEN,))
    return res[0], res[1], res[2:2 + nw], res[2 + nw:2 + 2 * nw], res[-1]


def _reduce_wait(send_sems, recv_sems, parts, lands, after, name):
    nw = len(parts)

    def body(*refs):
        part_refs, land_refs = refs[:nw], refs[nw:2 * nw]
        send_sems, recv_sems = refs[2 * nw:2 * nw + 2]
        x, y, c = _position()
        for w in range(nw):
            for k, peer in enumerate(_xor_peers(x, y, c)):
                n = w * (NDEV - 1) + k
                cp = _remote(_piece(part_refs[w], *peer), land_refs[w].at[k], send_sems.at[n], recv_sems.at[n], peer)
                cp.wait_send()
                cp.wait_recv()

    operands = list(parts) + list(lands)
    res = pl.pallas_call(
        body, name=name, out_shape=tuple(pltpu.HBM(a.shape, a.dtype) for a in operands),
        in_specs=(HBM,) * (2 * nw) + (SEM, SEM, ANY), out_specs=(HBM,) * (2 * nw),
        input_output_aliases={i: i for i in range(2 * nw)},
        compiler_params=pltpu.CompilerParams(has_side_effects=DATAFLOW),
    )(*operands, send_sems, recv_sems, after)
    return res[:nw], res[nw:]


def _sum_pieces(part, land, sel, name):
    half = part.shape[-2] // 2
    br = 128 if half % 128 == 0 else half // 2
    nb = half // br

    def body(sel_ref, own_ref, *refs):
        acc = own_ref[...]
        for r in refs[:NDEV - 1]:
            acc = acc + r[...]
        refs[NDEV - 1][...] = acc

    if part.ndim == 3:
        own_spec = pl.BlockSpec((None, br, D), lambda i, sel_ref: (sel_ref[0], sel_ref[1] * nb + i, 0))
    else:
        own_spec = pl.BlockSpec((br, D), lambda i, sel_ref: (sel_ref[1] * nb + i, sel_ref[0]))
    slot_specs = [pl.BlockSpec((None, br, D), functools.partial(lambda i, sel_ref, k: (k, i, 0), k=k))
                  for k in range(NDEV - 1)]
    return pl.pallas_call(
        body, name=name,
        grid_spec=pltpu.PrefetchScalarGridSpec(
            num_scalar_prefetch=1, grid=(nb,), in_specs=[own_spec] + slot_specs,
            out_specs=pl.BlockSpec((br, D), lambda i, sel_ref: (i, 0))),
        out_shape=jax.ShapeDtypeStruct((half, D), F32),
        compiler_params=_cparams("arbitrary"),
    )(sel, part, *([land] * (NDEV - 1)))


def _share_start(halves, name):
    nw = len(halves)
    lands = [lax.empty(h.shape, F32) for h in halves]

    def body(*refs):
        h_refs, land_refs = refs[:nw], refs[nw:2 * nw]
        send_sems, recv_sems = refs[2 * nw:2 * nw + 2]
        token = refs[-1]
        x, y, c = _position()
        for w in range(nw):
            _remote(h_refs[w], land_refs[w], send_sems.at[w], recv_sems.at[w], (x, y, 1 - c)).start()
        token[...] = jnp.zeros_like(token)

    res = _split_call(body, name, list(halves) + lands, (nw, nw), (TOKEN,))
    return res[0], res[1], res[2:2 + nw], res[2 + nw:2 + 2 * nw], res[-1]


def _share_wait(send_sems, recv_sems, halves, lands, after, name):
    nw = len(halves)

    def body(*refs):
        h_refs, land_refs = refs[:nw], refs[nw:2 * nw]
        send_sems, recv_sems = refs[2 * nw:2 * nw + 2]
        x, y, c = _position()
        for w in range(nw):
            cp = _remote(h_refs[w], land_refs[w], send_sems.at[w], recv_sems.at[w], (x, y, 1 - c))
            cp.wait_send()
            cp.wait_recv()

    operands = list(halves) + list(lands)
    res = pl.pallas_call(
        body, name=name, out_shape=tuple(pltpu.HBM(a.shape, a.dtype) for a in operands),
        in_specs=(HBM,) * (2 * nw) + (SEM, SEM, ANY), out_specs=(HBM,) * (2 * nw),
        input_output_aliases={i: i for i in range(2 * nw)},
        compiler_params=pltpu.CompilerParams(has_side_effects=DATAFLOW),
    )(*operands, send_sems, recv_sems, after)
    return res[:nw], res[nw:]


def _join_halves(own, other, c):
    first = jnp.where(c == 0, own, other)
    second = jnp.where(c == 0, other, own)
    return jnp.concatenate([first, second], axis=0)


SMALL_SIZES = (("norm1_g", D), ("sgu_ln_g", GW), ("sgu_ln_b", GW), ("sgu_w", NG * CHUNK * CHUNK),
               ("sgu_b", NG * CHUNK), ("attn_out_g", A), ("gmlp_out_g", GW), ("norm2_g", D),
               ("final_norm_g", D))
PARAM_ROWS = sum(n for _, n in SMALL_SIZES) // LANES
SMALL_ROWS = PARAM_ROWS + 8


def _pack_small(tree, first_extra=None):
    extra = jnp.zeros((8 * LANES,), F32)
    if first_extra is not None:
        extra = extra.at[0].set(first_extra)
    flat = jnp.concatenate([tree[n].reshape(-1) for n, _ in SMALL_SIZES] + [extra])
    return flat.reshape(SMALL_ROWS, LANES)


def _unpack_small(pack, shapes):
    flat = pack.reshape(-1)
    out, off = {}, 0
    for n, size in SMALL_SIZES:
        out[n] = flat[off:off + size].reshape(shapes[n])
        off += size
    return out


def _small_allreduce_adamw(gpack, wpack, mpack, vpack):
    def body(g_ref, w_ref, m_ref, v_ref, go_ref, d_ref, mo_ref, vo_ref, slots, send_sems, recv_sems):
        x, y, c = _position()
        me = 4 * x + 2 * y + c
        slots[me] = g_ref[...]
        peers = _xor_peers(x, y, c)
        sends = []
        for k, peer in enumerate(peers):
            cp = _remote(g_ref, slots.at[me], send_sems.at[k], recv_sems.at[k], peer)
            cp.start()
            sends.append(cp)
        for k, (px, py, pc) in enumerate(peers):
            _remote(g_ref, slots.at[4 * px + 2 * py + pc], send_sems.at[k], recv_sems.at[k],
                    (px, py, pc)).wait_recv()
        for cp in sends:
            cp.wait_send()
        total = slots[0]
        for k in range(1, NDEV):
            total = total + slots[k]
        go_ref[...] = total
        d, mn, vn = _adamw_math(w_ref[...], total, m_ref[...], v_ref[...])
        d_ref[...] = d
        mo_ref[...] = mn
        vo_ref[...] = vn

    sd = jax.ShapeDtypeStruct((SMALL_ROWS, LANES), F32)
    vm = pl.BlockSpec(memory_space=pltpu.VMEM)
    return pl.pallas_call(
        body, name="small_allreduce_adamw", in_specs=[vm] * 4, out_specs=[vm] * 4, out_shape=[sd] * 4,
        scratch_shapes=[pltpu.VMEM((NDEV, SMALL_ROWS, LANES), F32), pltpu.SemaphoreType.DMA((NDEV - 1,)),
                        pltpu.SemaphoreType.DMA((NDEV - 1,))],
        compiler_params=pltpu.CompilerParams(has_side_effects=True),
    )(gpack, wpack, mpack, vpack)


def kernel(x, norm1_g, w_in, sgu_ln_g, sgu_ln_b, sgu_w, sgu_b, attn_out_g, gmlp_out_g, w_out, norm2_g, w_ff1, w_ff2, final_norm_g, loss_target, m_norm1_g, m_w_in, m_sgu_ln_g, m_sgu_ln_b, m_sgu_w, m_sgu_b, m_attn_out_g, m_gmlp_out_g, m_w_out, m_norm2_g, m_w_ff1, m_w_ff2, m_final_norm_g, v_norm1_g, v_w_in, v_sgu_ln_g, v_sgu_ln_b, v_sgu_w, v_sgu_b, v_attn_out_g, v_gmlp_out_g, v_w_out, v_norm2_g, v_w_ff1, v_w_ff2, v_final_norm_g):
    names = [n for n, _ in SMALL_SIZES]
    w_small = dict(norm1_g=norm1_g, sgu_ln_g=sgu_ln_g, sgu_ln_b=sgu_ln_b, sgu_w=sgu_w, sgu_b=sgu_b,
                   attn_out_g=attn_out_g, gmlp_out_g=gmlp_out_g, norm2_g=norm2_g, final_norm_g=final_norm_g)
    m_small = dict(norm1_g=m_norm1_g, sgu_ln_g=m_sgu_ln_g, sgu_ln_b=m_sgu_ln_b, sgu_w=m_sgu_w, sgu_b=m_sgu_b,
                   attn_out_g=m_attn_out_g, gmlp_out_g=m_gmlp_out_g, norm2_g=m_norm2_g,
                   final_norm_g=m_final_norm_g)
    v_small = dict(norm1_g=v_norm1_g, sgu_ln_g=v_sgu_ln_g, sgu_ln_b=v_sgu_ln_b, sgu_w=v_sgu_w, sgu_b=v_sgu_b,
                   attn_out_g=v_attn_out_g, gmlp_out_g=v_gmlp_out_g, norm2_g=v_norm2_g,
                   final_norm_g=v_final_norm_g)
    shapes = {n: w_small[n].shape for n in names}

    start_in = _gather_start([w_in[0].T.astype(BF16)], "gather_in_start")
    issued = start_in[4][0:1, 0:1]
    start_rest = _gather_start([(w_out[0] + issued).astype(BF16), w_ff1[0].astype(BF16), w_ff2[0].astype(BF16)],
                               "gather_rest_start")
    hn1 = _norm1(x[0], norm1_g + start_rest[4][0:1, 0:1])
    win_t = _gather_wait(*start_in[:4], after=hn1, name="gather_in_wait")[0].reshape(INW, D)

    def rest_weights(after):
        wout, wff1, wff2 = _gather_wait(*start_rest[:4], after=after, name="gather_rest_wait")
        return wout.reshape(D, D), wff1, wff2.reshape(DFF, D)

    small = dict(
        norm1_g=norm1_g, ln_g=sgu_ln_g.reshape(1, GW), ln_b=sgu_ln_b.reshape(1, GW), sgu_w=sgu_w[0],
        sgu_wt=jnp.swapaxes(sgu_w[0], 1, 2), bias_t=jnp.repeat(sgu_b[0].T, DH, axis=1),
        attn_out_g=attn_out_g, gmlp_out_g=gmlp_out_g, norm2_g=norm2_g, final_norm_g=final_norm_g.reshape(1, D))
    xi, yi, ci = _position()
    sel = jnp.stack([2 * xi + yi, ci]).astype(jnp.int32)
    state = {}

    def as_slabs(g):
        return g.reshape(NCHIP, g.shape[0] // NCHIP, D)

    def early_grads(gwff1, gwff2, gwout):
        state["early"] = _reduce_start([gwff1, as_slabs(gwff2), as_slabs(gwout)], "reduce_early_start")
        return state["early"][4][0:1, 0:1]

    def after_attention_bwd(marker):
        send_sems, recv_sems, parts, lands, _ = state["early"]
        parts, lands = _reduce_wait(send_sems, recv_sems, parts, lands, marker, "reduce_early_wait")
        halves = [_sum_pieces(p, l, sel, "sum_" + n) for p, l, n in zip(parts, lands, ("w_ff1", "w_ff2", "w_out"))]
        state["early_share"] = _share_start(halves, "share_early_start")
        return state["early_share"][4][0:1, 0:1]

    def late_grads(gwin_t):
        state["late"] = _reduce_start([as_slabs(gwin_t)], "reduce_late_start")
        return state["late"][4][0:1, 0:1]

    loss_part, dx, sg, _ = _local_step(
        x[0], hn1, loss_target[0], small, win_t, rest_weights, early_grads, after_attention_bwd, late_grads)
    late = state["late"]
    send_sems, recv_sems, halves, lands, _ = state["early_share"]
    own, other = _share_wait(send_sems, recv_sems, halves, lands, dx, "share_early_wait")
    g_big = {n: _join_halves(o, t, ci) for n, o, t in zip(("w_ff1", "w_ff2", "w_out"), own, other)}
    w_big = dict(w_in=(w_in, m_w_in, v_w_in), w_out=(w_out, m_w_out, v_w_out),
                 w_ff1=(w_ff1, m_w_ff1, v_w_ff1), w_ff2=(w_ff2, m_w_ff2, v_w_ff2))
    grads, deltas, new_m, new_v = {}, {}, {}, {}

    def update(n):
        w, m, v = w_big[n]
        d, mn, vn = _adamw(w[0], g_big[n], m[0], v[0], "adamw_" + n)
        grads[n], deltas[n], new_m[n], new_v[n] = g_big[n][None], d[None], mn[None], vn[None]

    for n in ("w_ff1", "w_ff2", "w_out"):
        update(n)
    updated = deltas["w_out"][0, 0:8, 0:LANES] + deltas["w_ff1"][0, 0:8, 0:LANES] + deltas["w_ff2"][0, 0:8, 0:LANES]
    late_parts, late_lands = _reduce_wait(late[0], late[1], late[2], late[3], updated, "reduce_late_wait")
    late_share = _share_start([_sum_pieces(late_parts[0], late_lands[0], sel, "sum_w_in")], "share_late_start")

    g_small = dict(norm1_g=sg["norm1_g"], sgu_ln_g=sg["ln_g"], sgu_ln_b=sg["ln_b"], sgu_w=sg["sgu_w"],
                   sgu_b=sg["sgu_b"], attn_out_g=sg["attn_out_g"], gmlp_out_g=sg["gmlp_out_g"],
                   norm2_g=sg["norm2_g"], final_norm_g=sg["final_norm_g"])
    packs = _small_allreduce_adamw(_pack_small(g_small, loss_part) + late_share[4][0:1, 0:1], _pack_small(w_small),
                                   _pack_small(m_small), _pack_small(v_small))
    loss = packs[0][PARAM_ROWS, 0]
    for tree, pack in zip((grads, deltas, new_m, new_v), packs):
        tree.update(_unpack_small(pack, shapes))
    own, other = _share_wait(late_share[0], late_share[1], late_share[2], late_share[3], packs[0], "share_late_wait")
    g_big["w_in"] = _join_halves(own[0], other[0], ci).T
    update("w_in")

    order = ["norm1_g", "w_in", "sgu_ln_g", "sgu_ln_b", "sgu_w", "sgu_b", "attn_out_g", "gmlp_out_g", "w_out",
             "norm2_g", "w_ff1", "w_ff2", "final_norm_g"]
    return (loss, dx[None], *[grads[n] for n in order], *[deltas[n] for n in order],
            *[new_m[n] for n in order], *[new_v[n] for n in order])
```

```python
import functools
import math

import numpy as np
import jax
import jax.numpy as jnp
from jax import lax
from jax.experimental import pallas as pl
from jax.experimental.pallas import tpu as pltpu

F32 = jnp.float32
BF16 = jnp.bfloat16

D = 1024
NH = 12
DH = 64
A = NH * DH
NG = 4
GW = NG * DH
INW = 3 * A + 2 * GW
DFF = 4 * D
CHUNK = 128
PATTERNS = ((128, 1), (512, 4), (2048, 16))
EPS = 1e-6
SCALE = DH ** -0.5
NEG = -1e30

LR, B1, B2, AEPS, WD, STEP = 0.001, 0.9, 0.999, 1e-08, 0.01, 10

TM = 512
TMX = 512
ATT_ROWS = 1024
FF_CH = 1024
LANES = 128
NCHIP = 4
VMEM_LIMIT = 56 * 1024 * 1024
MESH = pl.DeviceIdType.MESH


def _cparams(*sem, **kw):
    return pltpu.CompilerParams(dimension_semantics=sem if sem else None,
                                vmem_limit_bytes=VMEM_LIMIT, **kw)


def _dot(a, b):
    return jnp.dot(a, b, preferred_element_type=F32)


def _dot_nt(a, b):
    return lax.dot_general(a, b, (((1,), (1,)), ((), ())), preferred_element_type=F32)


def _dot_tn(a, b):
    return lax.dot_general(a, b, (((0,), (0,)), ((), ())), preferred_element_type=F32)


def _dot_hi(a, b):
    return jnp.dot(a, b, preferred_element_type=F32, precision=lax.Precision.HIGHEST)


def _alibi_slopes(n):
    def pow2(m):
        start = 2.0 ** (-8.0 / m)
        return [start ** (i + 1) for i in range(m)]
    if math.log2(n).is_integer():
        s = pow2(n)
    else:
        c = 2 ** int(math.floor(math.log2(n)))
        s = pow2(c) + pow2(2 * c)[0::2][: n - c]
    return np.asarray(s, dtype=np.float32)


def _rms_fwd(v, g):
    r = lax.rsqrt(jnp.mean(v * v, axis=-1, keepdims=True) + EPS)
    vn = v * r
    return vn * g, vn, r


def _rms_bwd(dy, vn, r, g):
    w = dy * g
    dv = r * (w - vn * jnp.mean(w * vn, axis=-1, keepdims=True))
    return dv, jnp.sum(dy * vn, axis=0, keepdims=True)


_K0 = math.sqrt(2.0 / math.pi)
_K1 = 0.044715


def _gelu(v):
    return 0.5 * v * (1.0 + jnp.tanh(_K0 * (v + _K1 * (v * v * v))))


def _gelu_grad(v):
    t = jnp.tanh(_K0 * (v + _K1 * (v * v * v)))
    return 0.5 * (1.0 + t) + 0.5 * v * (1.0 - t * t) * (_K0 * (1.0 + 3.0 * _K1 * v * v))


def _row_spec(rows, cols):
    return pl.BlockSpec((rows, cols), lambda i: (i, 0))


def _const_spec(shape):
    nd = len(shape)
    return pl.BlockSpec(shape, lambda i: (0,) * nd, pipeline_mode=pl.Buffered(1))


DILS = tuple(d for _, d in PATTERNS)


def _fill_cols(scr, value):
    for cb in range(value.shape[1] // LANES):
        scr[cb] = value[:, cb * LANES:(cb + 1) * LANES]


def _split_residues(scr, out_ref, dil):
    nb, rows, _ = scr.shape
    for r in range(dil):
        for cb in range(nb):
            piece = scr.at[cb][pl.ds(r, rows // dil, stride=dil), :]
            out_ref[r, :, cb * LANES:(cb + 1) * LANES] = piece.astype(out_ref.dtype)


def _merge_residues(in_ref, scr, dil):
    nb, rows, _ = scr.shape
    for r in range(dil):
        for cb in range(nb):
            scr.at[cb][pl.ds(r, rows // dil, stride=dil), :] = in_ref[r, :, cb * LANES:(cb + 1) * LANES].astype(F32)
    return jnp.concatenate([scr[cb] for cb in range(nb)], axis=-1)


def _col_scratch(rows, width):
    return pltpu.VMEM((width // LANES, rows, LANES), F32)


def _res_spec(dil, rows, width):
    return pl.BlockSpec((dil, rows // dil, width), lambda i: (0, i, 0))


def _res_shape(s, dil, width, dtype):
    return jax.ShapeDtypeStruct((dil, s // dil, width), dtype)


def _norm1(x, g1):
    s = x.shape[0]

    def body(x_ref, g_ref, hn_ref):
        hn, _, _ = _rms_fwd(x_ref[...], g_ref[...])
        hn_ref[...] = hn.astype(BF16)

    return pl.pallas_call(
        body, name="norm1", grid=(s // TM,), in_specs=[_row_spec(TM, D), _const_spec((1, D))],
        out_specs=_row_spec(TM, D), out_shape=jax.ShapeDtypeStruct((s, D), BF16),
        compiler_params=_cparams("arbitrary"),
    )(x, g1)


def _inproj_fwd(hn1, win_t):
    s = hn1.shape[0]
    nd = len(DILS)

    def body(hn_ref, w_ref, *rest):
        qkv_refs = rest[:3 * nd]
        u_ref, z_ref, scr = rest[3 * nd:]
        hn = hn_ref[...]
        for t in range(3):
            seg = _dot_nt(hn, w_ref[t * A:(t + 1) * A, :])
            seg = seg * SCALE if t == 0 else seg
            _fill_cols(scr, seg)
            for di, dil in enumerate(DILS):
                if dil == 1:
                    qkv_refs[t * nd + di][0] = seg.astype(BF16)
                else:
                    _split_residues(scr, qkv_refs[t * nd + di], dil)
        u_ref[...] = _dot_nt(hn, w_ref[3 * A:3 * A + GW, :])
        z_ref[...] = _dot_nt(hn, w_ref[3 * A + GW:INW, :])

    res = pl.pallas_call(
        body, name="inproj_fwd", grid=(s // TM,),
        in_specs=[_row_spec(TM, D), _const_spec((INW, D))],
        out_specs=[_res_spec(d, TM, A) for _ in range(3) for d in DILS] + [_row_spec(TM, GW), _row_spec(TM, GW)],
        out_shape=[_res_shape(s, d, A, BF16) for _ in range(3) for d in DILS]
                  + [jax.ShapeDtypeStruct((s, GW), F32)] * 2,
        scratch_shapes=[_col_scratch(TM, A)],
        compiler_params=_cparams("arbitrary"),
    )(hn1, win_t)
    q, k, v = (res[t * nd:(t + 1) * nd] for t in range(3))
    return q, k, v, res[-2], res[-1]


def _att_geometry(length, dil):
    merge = max(1, ATT_ROWS // length)
    rows = min(length * merge, ATT_ROWS)
    nsub = rows // CHUNK
    return merge, rows, length * merge // rows, nsub, min(length // CHUNK, nsub)


def _merged(t, merge):
    return t.reshape(t.shape[0] // merge, t.shape[1] * merge, t.shape[2])


def _stack_heads(t):
    lane = lax.broadcasted_iota(jnp.int32, t.shape, 1)
    zero = jnp.zeros_like(t)
    return jnp.concatenate([jnp.where(lane < DH, t, zero), jnp.where(lane >= DH, t, zero)], axis=0)


def _head_cols(t, hp):
    lane = lax.broadcasted_iota(jnp.int32, t.shape, 1)
    cols = [jnp.sum(jnp.where(lane == 2 * hp + h, t, 0.0), axis=-1, keepdims=True) for h in range(2)]
    return jnp.concatenate(cols, axis=0)


def _unstack_heads(t2):
    n = t2.shape[0] // 2
    lane = lax.broadcasted_iota(jnp.int32, (n, LANES), 1)
    return jnp.where(lane < DH, t2[:n], t2[n:])


def _query_window_bias(s0, s1, dil, first):
    row = lax.broadcasted_iota(jnp.int32, (2 * CHUNK, 2 * CHUNK), 0)
    col = lax.broadcasted_iota(jnp.int32, (2 * CHUNK, 2 * CHUNK), 1)
    steps = (row & (CHUNK - 1)) + CHUNK - col
    valid = (steps >= 0) & (steps <= CHUNK)
    if first:
        valid = valid & (col >= CHUNK)
    slope = jnp.where(row < CHUNK, s0, s1)
    return jnp.where(valid, -slope * (steps * dil).astype(F32), NEG)


def _key_block_bias(s0, s1, dil, last):
    key = lax.broadcasted_iota(jnp.int32, (CHUNK, 4 * CHUNK), 0)
    col = lax.broadcasted_iota(jnp.int32, (CHUNK, 4 * CHUNK), 1)
    wq = col & (2 * CHUNK - 1)
    steps = wq - key
    valid = (steps >= 0) & (steps <= CHUNK)
    if last:
        valid = valid & (wq < CHUNK)
    slope = jnp.where(col < 2 * CHUNK, s0, s1)
    return jnp.where(valid, -slope * (steps * dil).astype(F32), NEG)


def _head_rows(t, hp):
    row = lax.broadcasted_iota(jnp.int32, (8, LANES), 0)
    lane = lax.broadcasted_iota(jnp.int32, (8, LANES), 1)
    pick = jnp.where((row < 2) & (lane == 2 * hp + row), 1.0, 0.0).astype(BF16)
    hi = t.astype(BF16)
    rest = t - hi.astype(F32)
    mid = rest.astype(BF16)
    low = (rest - mid.astype(F32)).astype(BF16)
    return _dot_nt(pick, hi) + _dot_nt(pick, mid) + _dot_nt(pick, low)


def _att_specs(dil, rows, nsub, nblk):
    main = pl.BlockSpec((None, rows, LANES), lambda r, c, hp: (r, c, hp))
    prev = pl.BlockSpec((None, CHUNK, LANES), lambda r, c, hp: (r, jnp.maximum(c * nsub - 1, 0), hp))
    nxt = pl.BlockSpec((None, CHUNK, LANES), lambda r, c, hp: (r, jnp.minimum((c + 1) * nsub, nblk - 1), hp))
    main_heads = pl.BlockSpec((None, rows, LANES), lambda r, c, hp: (r, c, 0))
    nxt_heads = pl.BlockSpec((None, CHUNK, LANES), lambda r, c, hp: (r, jnp.minimum((c + 1) * nsub, nblk - 1), 0))
    return main, prev, nxt, main_heads, nxt_heads


def _row_start(i):
    return i * CHUNK if isinstance(i, int) else pl.multiple_of(i * CHUNK, CHUNK)


def _first_blocks(block, nsub, seg, nch, ch, first_bias, bias_buf):
    for i in range(nsub):
        if i % seg:
            block(i, bias_buf[...])
        elif nch == 1:
            block(i, first_bias())
        else:
            block(i, jnp.where(ch == 0, first_bias(), bias_buf[...]))


def _last_blocks(block, nsub, seg, nch, ch, last_bias, bias_buf):
    for i in range(nsub):
        if (i + 1) % seg:
            block(i, bias_buf[...])
        elif nch == 1:
            block(i, last_bias())
        else:
            block(i, jnp.where(ch == nch - 1, last_bias(), bias_buf[...]))


def _attn_fwd(q, k, v, slopes, dil):
    length = q.shape[1]
    merge, rows, nch, nsub, seg = _att_geometry(length, dil)
    main, prev, _, main_heads, _ = _att_specs(dil, rows, nsub, length * merge // CHUNK)
    q, k, v = (_merged(t, merge) for t in (q, k, v))

    def body(sl_ref, q_ref, k_ref, v_ref, kh_ref, vh_ref, o_ref, lse_ref, kbuf, vbuf, bias_buf):
        ch = pl.program_id(1)
        hp = pl.program_id(2)
        lane = lax.broadcasted_iota(jnp.int32, (CHUNK, LANES), 1)
        kbuf[0:CHUNK, :] = kh_ref[...]
        kbuf[CHUNK:, :] = k_ref[...]
        vbuf[0:CHUNK, :] = vh_ref[...]
        vbuf[CHUNK:, :] = v_ref[...]
        s0, s1 = sl_ref[2 * hp], sl_ref[2 * hp + 1]

        def block(i, bias):
            row = _row_start(i)
            rs = pl.ds(row, CHUNK)
            q2 = _stack_heads(q_ref[rs, :])
            kw = kbuf[pl.ds(row, 2 * CHUNK), :]
            vw = vbuf[pl.ds(row, 2 * CHUNK), :]
            sc = _dot_nt(q2, kw) + bias
            m = jnp.max(sc, axis=-1, keepdims=True)
            p = jnp.exp(sc - m)
            l = jnp.sum(p, axis=-1, keepdims=True)
            o2 = _dot(p.astype(BF16), vw) * (1.0 / l)
            o_ref[rs, :] = _unstack_heads(o2).astype(BF16)
            lse = m + jnp.log(l)
            seen = jnp.where(hp == 0, 0.0, lse_ref[rs, :])
            lse_ref[rs, :] = jnp.where(lane == 2 * hp, lse[:CHUNK], jnp.where(lane == 2 * hp + 1, lse[CHUNK:], seen))

        bias_buf[...] = _query_window_bias(s0, s1, dil, False)
        _first_blocks(block, nsub, seg, nch, ch, lambda: _query_window_bias(s0, s1, dil, True), bias_buf)

    sd = jax.ShapeDtypeStruct
    o, lse = pl.pallas_call(
        body, name=f"attn_fwd_d{dil}", grid=(dil // merge, nch, NH // 2),
        in_specs=[pl.BlockSpec(memory_space=pltpu.SMEM), main, main, main, prev, prev],
        out_specs=[main, main_heads],
        out_shape=[sd((dil // merge, length * merge, A), BF16), sd((dil // merge, length * merge, LANES), F32)],
        scratch_shapes=[pltpu.VMEM((rows + CHUNK, LANES), BF16), pltpu.VMEM((rows + CHUNK, LANES), BF16),
                        pltpu.VMEM((2 * CHUNK, 2 * CHUNK), F32)],
        compiler_params=_cparams("arbitrary", "arbitrary", "arbitrary"),
    )(slopes, q, k, v, k, v)
    return o.reshape(dil, length, A), lse.reshape(dil, length, LANES)


def _attn_bwd_dq(q, k, v, do, lse, delta, slopes, dil):
    length = q.shape[1]
    merge, rows, nch, nsub, seg = _att_geometry(length, dil)
    main, prev, _, main_heads, _ = _att_specs(dil, rows, nsub, length * merge // CHUNK)
    q, k, v, do, lse, delta = (_merged(t, merge) for t in (q, k, v, do, lse, delta))

    def body(sl_ref, q_ref, k_ref, v_ref, do_ref, lse_ref, dl_ref, kh_ref, vh_ref, dq_ref, kbuf, vbuf, bias_buf):
        ch = pl.program_id(1)
        hp = pl.program_id(2)
        kbuf[0:CHUNK, :] = kh_ref[...]
        kbuf[CHUNK:, :] = k_ref[...]
        vbuf[0:CHUNK, :] = vh_ref[...]
        vbuf[CHUNK:, :] = v_ref[...]
        s0, s1 = sl_ref[2 * hp], sl_ref[2 * hp + 1]

        def block(i, bias):
            row = _row_start(i)
            rs = pl.ds(row, CHUNK)
            q2 = _stack_heads(q_ref[rs, :])
            do2 = _stack_heads(do_ref[rs, :])
            lse2 = _head_cols(lse_ref[rs, :], hp)
            dl2 = _head_cols(dl_ref[rs, :], hp)
            kw = kbuf[pl.ds(row, 2 * CHUNK), :]
            vw = vbuf[pl.ds(row, 2 * CHUNK), :]
            p = jnp.exp(_dot_nt(q2, kw) + bias - lse2)
            ds = p * (_dot_nt(do2, vw) - dl2)
            dq_ref[rs, :] = _unstack_heads(_dot(ds.astype(BF16), kw)).astype(BF16)

        bias_buf[...] = _query_window_bias(s0, s1, dil, False)
        _first_blocks(block, nsub, seg, nch, ch, lambda: _query_window_bias(s0, s1, dil, True), bias_buf)

    dq = pl.pallas_call(
        body, name=f"attn_dq_d{dil}", grid=(dil // merge, nch, NH // 2),
        in_specs=[pl.BlockSpec(memory_space=pltpu.SMEM), main, main, main, main, main_heads, main_heads, prev, prev],
        out_specs=main, out_shape=jax.ShapeDtypeStruct((dil // merge, length * merge, A), BF16),
        scratch_shapes=[pltpu.VMEM((rows + CHUNK, LANES), BF16), pltpu.VMEM((rows + CHUNK, LANES), BF16),
                        pltpu.VMEM((2 * CHUNK, 2 * CHUNK), F32)],
        compiler_params=_cparams("arbitrary", "arbitrary", "arbitrary"),
    )(slopes, q, k, v, do, lse, delta, k, v)
    return dq.reshape(dil, length, A)


def _attn_bwd_dkv(q, k, v, do, lse, delta, slopes, dil):
    length = q.shape[1]
    merge, rows, nch, nsub, seg = _att_geometry(length, dil)
    main, _, nxt, main_heads, nxt_heads = _att_specs(dil, rows, nsub, length * merge // CHUNK)
    q, k, v, do, lse, delta = (_merged(t, merge) for t in (q, k, v, do, lse, delta))

    def body(sl_ref, k_ref, v_ref, q_ref, do_ref, lse_ref, dl_ref, qh_ref, doh_ref, lseh_ref, dlh_ref,
             dk_ref, dv_ref, qbuf, dobuf, lse_rows, dl_rows, bias_buf):
        ch = pl.program_id(1)
        hp = pl.program_id(2)
        for buf, main_ref, halo_ref in ((qbuf, q_ref, qh_ref), (dobuf, do_ref, doh_ref)):
            buf[0:rows, :] = main_ref[...]
            buf[rows:, :] = halo_ref[...]
        for buf, main_ref, halo_ref in ((lse_rows, lse_ref, lseh_ref), (dl_rows, dl_ref, dlh_ref)):
            buf[:, 0:rows] = _head_rows(main_ref[...], hp)
            buf[:, rows:] = _head_rows(halo_ref[...], hp)
        s0, s1 = sl_ref[2 * hp], sl_ref[2 * hp + 1]

        def block(i, bias):
            row = _row_start(i)
            rs = pl.ds(row, CHUNK)
            win = pl.ds(row, 2 * CHUNK)
            kc = k_ref[rs, :]
            vc = v_ref[rs, :]
            q2 = _stack_heads(qbuf[win, :])
            do2 = _stack_heads(dobuf[win, :])
            cols = slice(i * CHUNK, (i + 2) * CHUNK)
            lse2 = jnp.concatenate([lse_rows[0:1, cols], lse_rows[1:2, cols]], axis=1)
            dl2 = jnp.concatenate([dl_rows[0:1, cols], dl_rows[1:2, cols]], axis=1)
            pt = jnp.exp(_dot_nt(kc, q2) + bias - lse2)
            dst = pt * (_dot_nt(vc, do2) - dl2)
            dv_ref[rs, :] = _dot(pt.astype(BF16), do2).astype(BF16)
            dk_ref[rs, :] = _dot(dst.astype(BF16), q2).astype(BF16)

        bias_buf[...] = _key_block_bias(s0, s1, dil, False)
        _last_blocks(block, nsub, seg, nch, ch, lambda: _key_block_bias(s0, s1, dil, True), bias_buf)

    sd = jax.ShapeDtypeStruct((dil // merge, length * merge, A), BF16)
    dk, dv = pl.pallas_call(
        body, name=f"attn_dkv_d{dil}", grid=(dil // merge, nch, NH // 2),
        in_specs=[pl.BlockSpec(memory_space=pltpu.SMEM), main, main, main, main, main_heads, main_heads,
                  nxt, nxt, nxt_heads, nxt_heads],
        out_specs=[main, main], out_shape=[sd, sd],
        scratch_shapes=[pltpu.VMEM((rows + CHUNK, LANES), BF16), pltpu.VMEM((rows + CHUNK, LANES), BF16),
                        pltpu.VMEM((8, rows + CHUNK), F32), pltpu.VMEM((8, rows + CHUNK), F32),
                        pltpu.VMEM((CHUNK, 4 * CHUNK), F32)],
        compiler_params=_cparams("arbitrary", "arbitrary", "arbitrary"),
    )(slopes, k, v, q, do, lse, delta, q, do, lse, delta)
    return dk.reshape(dil, length, A), dv.reshape(dil, length, A)


def _group_masks(width):
    lane = lax.broadcasted_iota(jnp.int32, (1, width), 1)
    return [(lane >= g * DH) & (lane < (g + 1) * DH) for g in range(width // DH)]


def _group_mean_matrix():
    i = lax.broadcasted_iota(jnp.int32, (GW, GW), 0) // DH
    j = lax.broadcasted_iota(jnp.int32, (GW, GW), 1) // DH
    return jnp.where(i == j, 1.0 / DH, 0.0).astype(F32)


def _tri_mask(lower):
    t = lax.broadcasted_iota(jnp.int32, (CHUNK, CHUNK), 0)
    u = lax.broadcasted_iota(jnp.int32, (CHUNK, CHUNK), 1)
    return (u <= t) if lower else (u >= t)


def _sgu_forward(u, z, lng, lnb, w_ref, bias_t, pmat, rows):
    ug = _gelu(u)
    zg = _gelu(z)
    mu = _dot_hi(zg, pmat)
    zc = zg - mu
    var = _dot_hi(zc * zc, pmat)
    rstd = lax.rsqrt(var + EPS)
    zhat = zc * rstd
    zn = (zhat * lng + lnb).astype(BF16)
    gm = _group_masks(GW)
    tri = _tri_mask(True)
    ws = [jnp.where(tri, w_ref[g], 0.0).astype(BF16) for g in range(NG)]
    pieces = []
    for c in range(rows // CHUNK):
        znc = zn[c * CHUNK:(c + 1) * CHUNK, :]
        mix = None
        for g in range(NG):
            part = jnp.where(gm[g], _dot(ws[g], znc), 0.0)
            mix = part if mix is None else mix + part
        pieces.append(mix + bias_t)
    mixed = jnp.concatenate(pieces, axis=0) if len(pieces) > 1 else pieces[0]
    return ug * mixed, ug, zhat, rstd, zn, mixed


def _head_spread():
    h = lax.broadcasted_iota(jnp.int32, (LANES, A), 0)
    lane = lax.broadcasted_iota(jnp.int32, (LANES, A), 1)
    return jnp.where(lane // DH == h, 1.0, 0.0).astype(BF16)


def _bf16_pieces(t, n):
    pieces = []
    for _ in range(n):
        piece = t.astype(BF16)
        pieces.append(piece)
        t = t - piece.astype(F32)
    return pieces


def _mix_fwd(os_, ls_, u, z, x, lng, lnb, sgu_w, bias_t, ga, gg, wout):
    s = x.shape[0]
    nd = len(DILS)
    nscr = sum(1 for d in DILS if d > 1)

    def body(*refs):
        o_refs, l_refs = refs[:nd], refs[nd:2 * nd]
        u_ref, z_ref, x_ref, lng_ref, lnb_ref, w_ref, bt_ref, ga_ref, gg_ref, wo_ref = refs[2 * nd:2 * nd + 10]
        attn_ref = refs[2 * nd + 10]
        lse_refs = refs[2 * nd + 11:3 * nd + 11]
        mixed_ref, h1_ref = refs[3 * nd + 11:3 * nd + 13]
        scr = refs[3 * nd + 13:]
        scr_o, scr_l, scr_lse = scr[:nscr], scr[nscr:2 * nscr], scr[2 * nscr]
        ov, lv, j = [], [], 0
        for di, dil in enumerate(DILS):
            if dil == 1:
                ov.append(o_refs[di][0].astype(F32))
                lv.append(l_refs[di][0])
            else:
                ov.append(_merge_residues(o_refs[di], scr_o[j], dil))
                lv.append(_merge_residues(l_refs[di], scr_l[j], dil))
                j += 1
        mx = functools.reduce(jnp.maximum, lv)
        es = [jnp.exp(l - mx) for l in lv]
        den = functools.reduce(lambda a, b: a + b, es)
        spread = _head_spread()
        attn = None
        for e, o in zip(es, ov):
            wide = functools.reduce(lambda a, b: a + b, [_dot(piece, spread) for piece in _bf16_pieces(e / den, 2)])
            attn = wide * o if attn is None else attn + wide * o
        attn_ref[...] = attn
        lse = mx + jnp.log(den)
        _fill_cols(scr_lse, lse)
        for di, dil in enumerate(DILS):
            if dil == 1:
                lse_refs[di][0] = lse
            else:
                _split_residues(scr_lse, lse_refs[di], dil)
        an, _, _ = _rms_fwd(attn, ga_ref[...])
        gmv, _, _, _, _, _ = _sgu_forward(u_ref[...], z_ref[...], lng_ref[...], lnb_ref[...], w_ref,
                                          bt_ref[...], _group_mean_matrix(), TMX)
        gn, _, _ = _rms_fwd(gmv, gg_ref[...])
        mixed = jnp.concatenate([an, gn], axis=-1).astype(BF16)
        mixed_ref[...] = mixed
        h1_ref[...] = x_ref[...] + _dot(mixed, wo_ref[...])

    sd = jax.ShapeDtypeStruct
    res = pl.pallas_call(
        body, name="mix_fwd", grid=(s // TMX,),
        in_specs=[_res_spec(d, TMX, A) for d in DILS] + [_res_spec(d, TMX, LANES) for d in DILS]
                 + [_row_spec(TMX, GW), _row_spec(TMX, GW),
                    _row_spec(TMX, D), _const_spec((1, GW)), _const_spec((1, GW)), _const_spec((NG, CHUNK, CHUNK)),
                    _const_spec((CHUNK, GW)), _const_spec((1, A)), _const_spec((1, GW)), _const_spec((D, D))],
        out_specs=[_row_spec(TMX, A)] + [_res_spec(d, TMX, LANES) for d in DILS]
                  + [_row_spec(TMX, D), _row_spec(TMX, D)],
        out_shape=[sd((s, A), F32)] + [_res_shape(s, d, LANES, F32) for d in DILS]
                  + [sd((s, D), BF16), sd((s, D), F32)],
        scratch_shapes=[_col_scratch(TMX, A)] * nscr + [_col_scratch(TMX, LANES)] * (nscr + 1),
        compiler_params=_cparams("arbitrary"),
    )(*os_, *ls_, u, z, x, lng, lnb, sgu_w, bias_t, ga, gg, wout)
    return res[0], res[1:1 + nd], res[1 + nd], res[2 + nd]


def _mlp_fwd(h1, g2, wff1, wff2, gf, target):
    s = h1.shape[0]

    def body(h1_ref, g2_ref, w1_ref, w2_ref, gf_ref, t_ref, hn_ref, rf_ref, dh2_ref, loss_ref, dgf_ref):
        i = pl.program_id(0)
        h1v = h1_ref[...]
        hn, _, _ = _rms_fwd(h1v, g2_ref[...])
        hn = hn.astype(BF16)
        hn_ref[...] = hn
        acc = h1v
        for j in range(DFF // FF_CH):
            cols = slice(j * FF_CH, (j + 1) * FF_CH)
            rf = jnp.maximum(_dot(hn, w1_ref[j]), 0.0)
            act = (rf * rf).astype(BF16)
            rf_ref[:, cols] = rf.astype(BF16)
            acc = acc + _dot(act, w2_ref[cols, :])
        y, h2n, r3 = _rms_fwd(acc, gf_ref[...])
        err = y - t_ref[...]
        part = 0.5 * jnp.sum(jnp.mean(err * err, axis=-1, keepdims=True), axis=0, keepdims=True)
        dy = err * (1.0 / D)
        dh2, dgf = _rms_bwd(dy, h2n, r3, gf_ref[...])
        dh2_ref[...] = dh2

        @pl.when(i == 0)
        def _():
            loss_ref[...] = jnp.zeros_like(loss_ref)
            dgf_ref[...] = jnp.zeros_like(dgf_ref)

        loss_ref[...] += jnp.broadcast_to(part, loss_ref.shape)
        dgf_ref[...] += dgf

    sd = jax.ShapeDtypeStruct
    return pl.pallas_call(
        body, name="mlp_fwd", grid=(s // TM,),
        in_specs=[_row_spec(TM, D), _const_spec((1, D)), _const_spec((DFF // FF_CH, D, FF_CH)), _const_spec((DFF, D)),
                  _const_spec((1, D)), _row_spec(TM, D)],
        out_specs=[_row_spec(TM, D), _row_spec(TM, DFF), _row_spec(TM, D),
                   _const_spec((1, LANES)), _const_spec((1, D))],
        out_shape=[sd((s, D), BF16), sd((s, DFF), BF16), sd((s, D), F32),
                   sd((1, LANES), F32), sd((1, D), F32)],
        compiler_params=_cparams("arbitrary"),
    )(h1, g2, wff1, wff2, gf, target)


def _mlp_bwd(dh2, rf, h1, g2, wff1, wff2):
    s = h1.shape[0]

    def body(dh2_ref, rf_ref, h1_ref, g2_ref, w1_ref, w2_ref, df_ref, dh1_ref, dg2_ref):
        i = pl.program_id(0)
        dh2v = dh2_ref[...]
        dh2b = dh2v.astype(BF16)
        dhn = jnp.zeros((TM, D), F32)
        for j in range(DFF // FF_CH):
            cols = slice(j * FF_CH, (j + 1) * FF_CH)
            da = _dot_nt(dh2b, w2_ref[cols, :])
            df = (da * (2.0 * rf_ref[:, cols].astype(F32))).astype(BF16)
            df_ref[:, cols] = df
            dhn = dhn + _dot_nt(df, w1_ref[j])
        _, h1n, r2 = _rms_fwd(h1_ref[...], g2_ref[...])
        dres, dg2 = _rms_bwd(dhn, h1n, r2, g2_ref[...])
        dh1_ref[...] = dh2v + dres

        @pl.when(i == 0)
        def _():
            dg2_ref[...] = jnp.zeros_like(dg2_ref)

        dg2_ref[...] += dg2

    sd = jax.ShapeDtypeStruct
    return pl.pallas_call(
        body, name="mlp_bwd", grid=(s // TM,),
        in_specs=[_row_spec(TM, D), _row_spec(TM, DFF), _row_spec(TM, D), _const_spec((1, D)),
                  _const_spec((DFF // FF_CH, D, FF_CH)), _const_spec((DFF, D))],
        out_specs=[_row_spec(TM, DFF), _row_spec(TM, D), _const_spec((1, D))],
        out_shape=[sd((s, DFF), BF16), sd((s, D), F32), sd((1, D), F32)],
        compiler_params=_cparams("arbitrary"),
    )(dh2, rf, h1, g2, wff1, wff2)


def _mix_bwd(dh1, attn, u, z, lng, lnb, sgu_w, sgu_wt, bias_t, ga, gg, wout):
    s = dh1.shape[0]
    nsteps = s // TMX
    nd = len(DILS)

    def body(*refs):
        dh1_ref, attn_ref, u_ref, z_ref, lng_ref, lnb_ref, w_ref, wt_ref, bt_ref, ga_ref, gg_ref, wo_ref = refs[:12]
        do_refs, dl_refs = refs[12:12 + nd], refs[12 + nd:12 + 2 * nd]
        (du_ref, dz_ref, dga_ref, dgg_ref, dlng_ref, dlnb_ref, dws_ref, db_ref,
         dbt_acc, scr_do, scr_dl) = refs[12 + 2 * nd:]
        i = pl.program_id(0)

        @pl.when(i == 0)
        def _():
            for r in (dga_ref, dgg_ref, dlng_ref, dlnb_ref, dws_ref, db_ref, dbt_acc):
                r[...] = jnp.zeros_like(r)

        dmixed = _dot_nt(dh1_ref[...].astype(BF16), wo_ref[...])
        attn = attn_ref[...]
        _, an, ra = _rms_fwd(attn, ga_ref[...])
        dattn, dga = _rms_bwd(dmixed[:, :A], an, ra, ga_ref[...])
        dga_ref[...] += dga
        _fill_cols(scr_do, dattn)
        spread = _head_spread()
        delta = functools.reduce(lambda a, b: a + b, [_dot_nt(piece, spread) for piece in _bf16_pieces(dattn * attn, 3)])
        _fill_cols(scr_dl, delta)
        for di, dil in enumerate(DILS):
            if dil == 1:
                do_refs[di][0] = dattn.astype(BF16)
                dl_refs[di][0] = delta
            else:
                _split_residues(scr_do, do_refs[di], dil)
                _split_residues(scr_dl, dl_refs[di], dil)
        pmat = _group_mean_matrix()
        lng = lng_ref[...]
        uv, zv = u_ref[...], z_ref[...]
        gmv, ug, zhat, rstd, zn, mixed = _sgu_forward(uv, zv, lng, lnb_ref[...], w_ref, bt_ref[...], pmat, TMX)
        _, gmn, rg = _rms_fwd(gmv, gg_ref[...])
        dgm, dgg = _rms_bwd(dmixed[:, A:], gmn, rg, gg_ref[...])
        dgg_ref[...] += dgg
        du_ref[...] = dgm * mixed * _gelu_grad(uv)
        dmx = dgm * ug
        dmxb = dmx.astype(BF16)
        gm = _group_masks(GW)
        tri_t = _tri_mask(False)
        wst = [jnp.where(tri_t, wt_ref[g], 0.0).astype(BF16) for g in range(NG)]
        zero = jnp.zeros((CHUNK, GW), BF16)
        dzn_pieces = []
        for c in range(TMX // CHUNK):
            rs = slice(c * CHUNK, (c + 1) * CHUNK)
            dmc = dmxb[rs, :]
            znc = zn[rs, :]
            dbt_acc[...] += dmx[rs, :]
            dzn = None
            for g in range(NG):
                dws_ref[g] += _dot_nt(jnp.where(gm[g], dmc, zero), znc)
                part = jnp.where(gm[g], _dot(wst[g], dmc), 0.0)
                dzn = part if dzn is None else dzn + part
            dzn_pieces.append(dzn)
        dzn = jnp.concatenate(dzn_pieces, axis=0)
        dlng_ref[...] += jnp.sum(dzn * zhat, axis=0, keepdims=True)
        dlnb_ref[...] += jnp.sum(dzn, axis=0, keepdims=True)
        dzh = dzn * lng
        dzg = rstd * (dzh - _dot_hi(dzh, pmat) - zhat * _dot_hi(dzh * zhat, pmat))
        dz_ref[...] = dzg * _gelu_grad(zv)

        @pl.when(i == nsteps - 1)
        def _():
            tri = _tri_mask(True)
            for g in range(NG):
                dws_ref[g] = jnp.where(tri, dws_ref[g], 0.0)
            acc = dbt_acc[...]
            lane = lax.broadcasted_iota(jnp.int32, (CHUNK, LANES), 1)
            out = jnp.zeros((CHUNK, LANES), F32)
            for g in range(NG):
                sg = jnp.sum(jnp.where(gm[g], acc, 0.0), axis=-1, keepdims=True)
                out = jnp.where(lane == g, sg, out)
            db_ref[...] = out

    sd = jax.ShapeDtypeStruct
    res = pl.pallas_call(
        body, name="mix_bwd", grid=(nsteps,),
        in_specs=[_row_spec(TMX, D), _row_spec(TMX, A), _row_spec(TMX, GW), _row_spec(TMX, GW),
                  _const_spec((1, GW)), _const_spec((1, GW)), _const_spec((NG, CHUNK, CHUNK)),
                  _const_spec((NG, CHUNK, CHUNK)), _const_spec((CHUNK, GW)), _const_spec((1, A)),
                  _const_spec((1, GW)), _const_spec((D, D))],
        out_specs=[_res_spec(d, TMX, A) for d in DILS] + [_res_spec(d, TMX, LANES) for d in DILS]
                  + [_row_spec(TMX, GW), _row_spec(TMX, GW),
                   _const_spec((1, A)), _const_spec((1, GW)), _const_spec((1, GW)), _const_spec((1, GW)),
                   _const_spec((NG, CHUNK, CHUNK)), _const_spec((CHUNK, LANES))],
        out_shape=[_res_shape(s, d, A, BF16) for d in DILS] + [_res_shape(s, d, LANES, F32) for d in DILS]
                  + [sd((s, GW), F32), sd((s, GW), F32),
                   sd((1, A), F32), sd((1, GW), F32), sd((1, GW), F32), sd((1, GW), F32),
                   sd((NG, CHUNK, CHUNK), F32), sd((CHUNK, LANES), F32)],
        scratch_shapes=[pltpu.VMEM((CHUNK, GW), F32), _col_scratch(TMX, A), _col_scratch(TMX, LANES)],
        compiler_params=_cparams("arbitrary"),
    )(dh1, attn, u, z, lng, lnb, sgu_w, sgu_wt, bias_t, ga, gg, wout)
    return (res[:nd], res[nd:2 * nd]) + tuple(res[2 * nd:])


def _dproj_merge(dqs, dks, dvs, du, dz, pin):
    s = du.shape[0]
    nd = len(DILS)
    nscr = sum(1 for d in DILS if d > 1)

    def body(*refs):
        pin_ref = refs[0]
        parts = [refs[1 + t * nd:1 + (t + 1) * nd] for t in range(3)]
        du_ref, dz_ref, dp_ref = refs[1 + 3 * nd:4 + 3 * nd]
        scr = refs[4 + 3 * nd:]
        sums = []
        for t in range(3):
            total, j = None, 0
            for di, dil in enumerate(DILS):
                if dil == 1:
                    term = parts[t][di][0].astype(F32)
                else:
                    term = _merge_residues(parts[t][di], scr[t * nscr + j], dil)
                    j += 1
                total = term if total is None else total + term
            sums.append(total)
        dp_ref[...] = jnp.concatenate([sums[0] * SCALE, sums[1], sums[2], du_ref[...] + pin_ref[0, 0], dz_ref[...]],
                                      axis=-1).astype(BF16)

    return pl.pallas_call(
        body, name="dproj_merge", grid=(s // TMX,),
        in_specs=[pl.BlockSpec(memory_space=pltpu.SMEM)] + [_res_spec(d, TMX, A) for d in DILS] * 3
                 + [_row_spec(TMX, GW)] * 2,
        out_specs=_row_spec(TMX, INW), out_shape=jax.ShapeDtypeStruct((s, INW), BF16),
        scratch_shapes=[_col_scratch(TMX, A)] * (3 * nscr),
        compiler_params=_cparams("arbitrary"),
    )(pin, *dqs, *dks, *dvs, du, dz)


def _inproj_bwd(dproj, dh1, x, g1, win_t):
    s = x.shape[0]

    def body(dp_ref, dh1_ref, x_ref, g_ref, w_ref, dx_ref, dg_ref):
        i = pl.program_id(0)
        dhn = _dot(dp_ref[...], w_ref[...])
        _, xn, r1 = _rms_fwd(x_ref[...], g_ref[...])
        dres, dg = _rms_bwd(dhn, xn, r1, g_ref[...])
        dx_ref[...] = dh1_ref[...] + dres

        @pl.when(i == 0)
        def _():
            dg_ref[...] = jnp.zeros_like(dg_ref)

        dg_ref[...] += dg

    sd = jax.ShapeDtypeStruct
    return pl.pallas_call(
        body, name="inproj_bwd", grid=(s // TM,),
        in_specs=[_row_spec(TM, INW), _row_spec(TM, D), _row_spec(TM, D), _const_spec((1, D)), _const_spec((INW, D))],
        out_specs=[_row_spec(TM, D), _const_spec((1, D))],
        out_shape=[sd((s, D), F32), sd((1, D), F32)],
        compiler_params=_cparams("arbitrary"),
    )(dproj, dh1, x, g1, win_t)


def _wgrad(a, b, name, bm, bn, bk=2 * TM, square_a=False):
    s, m = a.shape
    n = b.shape[1]
    bm, bn = min(bm, m), min(bn, n)

    def body(a_ref, b_ref, o_ref):
        @pl.when(pl.program_id(2) == 0)
        def _():
            o_ref[...] = jnp.zeros_like(o_ref)

        av = a_ref[...]
        if square_a:
            av = av.astype(F32)
            av = av * av
        o_ref[...] += _dot_tn(av.astype(BF16), b_ref[...].astype(BF16))

    return pl.pallas_call(
        body, name=name, grid=(m // bm, n // bn, s // bk),
        in_specs=[pl.BlockSpec((bk, bm), lambda i, j, k: (k, i)), pl.BlockSpec((bk, bn), lambda i, j, k: (k, j))],
        out_specs=pl.BlockSpec((bm, bn), lambda i, j, k: (i, j)),
        out_shape=jax.ShapeDtypeStruct((m, n), F32),
        compiler_params=_cparams("arbitrary", "arbitrary", "arbitrary"),
    )(a, b)


def _adamw_math(w, g, m, v):
    m = B1 * m + (1.0 - B1) * g
    v = B2 * v + (1.0 - B2) * (g * g)
    m_hat = m / (1.0 - B1 ** STEP)
    v_hat = v / (1.0 - B2 ** STEP)
    delta = -LR * (m_hat / (jnp.sqrt(v_hat) + AEPS) + WD * w)
    return delta, m, v


def _adamw(w, g, m, v, name):
    rows, cols = w.shape
    br = min(rows, 256)
    while rows % br:
        br -= 8

    def body(w_ref, g_ref, m_ref, v_ref, d_ref, mo_ref, vo_ref):
        d, mn, vn = _adamw_math(w_ref[...], g_ref[...], m_ref[...], v_ref[...])
        d_ref[...] = d
        mo_ref[...] = mn
        vo_ref[...] = vn

    spec = _row_spec(br, cols)
    sd = jax.ShapeDtypeStruct((rows, cols), F32)
    return pl.pallas_call(
        body, name=name, grid=(rows // br,), in_specs=[spec] * 4, out_specs=[spec] * 3,
        out_shape=[sd, sd, sd], compiler_params=_cparams("arbitrary"),
    )(w, g, m, v)


def _local_step(x, hn1, target, small, win_t, rest_weights, early_grads=None, after_attention_bwd=None,
                late_grads=None):
    slopes = jnp.asarray(_alibi_slopes(NH))
    q, k, v, u, z = _inproj_fwd(hn1, win_t)
    outs, lses = [], []
    for i, dil in enumerate(DILS):
        o, l = _attn_fwd(q[i], k[i], v[i], slopes, dil)
        outs.append(o)
        lses.append(l)
    wout, wff1, wff2 = rest_weights(lses[-1])
    attn, lse, mixed, h1 = _mix_fwd(outs, lses, u, z, x, small["ln_g"], small["ln_b"], small["sgu_w"],
                                    small["bias_t"], small["attn_out_g"], small["gmlp_out_g"], wout)
    hn2, rf, dh2, loss, dgf = _mlp_fwd(h1, small["norm2_g"], wff1, wff2, small["final_norm_g"], target)
    df, dh1, dg2 = _mlp_bwd(dh2, rf, h1, small["norm2_g"], wff1, wff2)
    gwff1 = _wgrad(hn2, df, "wgrad_ff1", D, 1024)
    gwff2 = _wgrad(rf, dh2, "wgrad_ff2", 1024, D, square_a=True)
    gwout = _wgrad(mixed, dh1, "wgrad_out", D, D)
    ga, g1 = small["attn_out_g"], small["norm1_g"]
    pin = early_grads(gwff1, gwff2, gwout) if early_grads else None
    if pin is not None:
        ga = ga + pin
    (do, delta, du, dz, dga, dgg, dlng, dlnb, dws, db) = _mix_bwd(
        dh1, attn, u, z, small["ln_g"], small["ln_b"], small["sgu_w"], small["sgu_wt"], small["bias_t"],
        ga, small["gmlp_out_g"], wout)
    dqs, dks, dvs = [], [], []
    for i, dil in enumerate(DILS):
        dqs.append(_attn_bwd_dq(q[i], k[i], v[i], do[i], lse[i], delta[i], slopes, dil))
        dk, dv = _attn_bwd_dkv(q[i], k[i], v[i], do[i], lse[i], delta[i], slopes, dil)
        dks.append(dk)
        dvs.append(dv)
    marker = functools.reduce(lambda a, b: a + b, [t[0, 0:8, 0:LANES] for t in dqs + dks + dvs])
    pin = after_attention_bwd(marker) if after_attention_bwd else None
    dproj = _dproj_merge(dqs, dks, dvs, du, dz, jnp.zeros((1, 1), F32) if pin is None else pin)
    gwin_t = _wgrad(dproj, hn1, "wgrad_in", INW // 2, D)
    pin = late_grads(gwin_t) if late_grads else None
    if pin is not None:
        g1 = g1 + pin
    dx, dg1 = _inproj_bwd(dproj, dh1, x, g1, win_t)
    small_grads = dict(norm1_g=dg1, ln_g=dlng, ln_b=dlnb, sgu_w=dws, sgu_b=db[:, :NG].T,
                       attn_out_g=dga, gmlp_out_g=dgg, norm2_g=dg2, final_norm_g=dgf)
    return loss[0, 0], dx, small_grads, (gwin_t, gwout, gwff1, gwff2)


ANY = pl.BlockSpec(memory_space=pl.ANY)
NDEV = 8


def _position():
    return lax.axis_index("x"), lax.axis_index("y"), lax.axis_index("c")


def _other_chips(x, y):
    return [(1 - x, y), (x, 1 - y), (1 - x, 1 - y)]


def _remote(src, dst, send_sem, recv_sem, device):
    return pltpu.make_async_remote_copy(src_ref=src, dst_ref=dst, send_sem=send_sem, recv_sem=recv_sem,
                                        device_id=device, device_id_type=MESH)


HBM = pl.BlockSpec(memory_space=pltpu.HBM)
SEM = pl.BlockSpec(memory_space=pltpu.SEMAPHORE)
DATAFLOW = pltpu.SideEffectType.DATAFLOW_SIDE_EFFECTING


def _in_hbm(a):
    return pltpu.with_memory_space_constraint(a, pltpu.HBM)


def _gather_start(shards, name):
    n = len(shards)
    lands = [jnp.broadcast_to(sh[None], (NCHIP,) + sh.shape) for sh in shards]

    def body(*refs):
        w_refs, land_refs = refs[:n], refs[n:2 * n]
        send_sems, recv_sems = refs[2 * n:2 * n + 2]
        token = refs[-1]
        x, y, c = _position()
        for w in range(n):
            for k, (px, py) in enumerate(_other_chips(x, y)):
                m = 3 * w + k
                _remote(w_refs[w], land_refs[w].at[2 * x + y], send_sems.at[m], recv_sems.at[m], (px, py, c)).start()
        token[...] = jnp.zeros_like(token)

    res = _split_call(body, name, list(shards) + lands, (3 * n, 3 * n), (TOKEN,))
    return res[0], res[1], res[2:2 + n], res[2 + n:2 + 2 * n], res[-1]


def _gather_wait(send_sems, recv_sems, shards, lands, after, name):
    n = len(shards)

    def body(*refs):
        w_refs, land_refs = refs[:n], refs[n:2 * n]
        send_sems, recv_sems = refs[2 * n:2 * n + 2]
        x, y, c = _position()
        for w in range(n):
            for k, (px, py) in enumerate(_other_chips(x, y)):
                m = 3 * w + k
                cp = _remote(w_refs[w], land_refs[w].at[2 * px + py], send_sems.at[m], recv_sems.at[m], (px, py, c))
                cp.wait_send()
                cp.wait_recv()

    operands = list(shards) + list(lands)
    res = pl.pallas_call(
        body, name=name, out_shape=tuple(pltpu.HBM(a.shape, a.dtype) for a in operands),
        in_specs=(HBM,) * (2 * n) + (SEM, SEM, ANY), out_specs=(HBM,) * (2 * n),
        input_output_aliases={i: i for i in range(2 * n)},
        compiler_params=pltpu.CompilerParams(has_side_effects=DATAFLOW),
    )(*operands, send_sems, recv_sems, after)
    return res[n:]


def _xor_peers(x, y, c):
    peers = []
    for k in range(1, NDEV):
        kx, ky, kc = (k >> 2) & 1, (k >> 1) & 1, k & 1
        peers.append((1 - x if kx else x, 1 - y if ky else y, 1 - c if kc else c))
    return peers


def _piece(part_ref, px, py, pc):
    slab = 2 * px + py
    if len(part_ref.shape) == 3:
        half = part_ref.shape[1] // 2
        return part_ref.at[slab, pl.ds(pc * half, half), :]
    half = part_ref.shape[0] // 2
    return part_ref.at[pl.ds(pc * half, half), pl.ds(pl.multiple_of(slab * D, D), D)]


def _split_call(body, name, operands, n_sems, extra_out=()):
    n = len(operands)
    sems = tuple(pltpu.SemaphoreType.DMA((m,)) for m in n_sems)
    thru = tuple(pltpu.HBM(a.shape, a.dtype) for a in operands)
    return pl.pallas_call(
        body, name=name, out_shape=sems + thru + tuple(extra_out),
        in_specs=(HBM,) * n,
        out_specs=(SEM,) * len(sems) + (HBM,) * n + (pl.BlockSpec(memory_space=pltpu.VMEM),) * len(extra_out),
        input_output_aliases={i: len(sems) + i for i in range(n)},
        compiler_params=pltpu.CompilerParams(has_side_effects=DATAFLOW),
    )(*[_in_hbm(a) for a in operands])


TOKEN = jax.ShapeDtypeStruct((8, LANES), F32)


def _reduce_start(parts, name):
    nw = len(parts)
    lands = [lax.empty((NDEV - 1, p.shape[-2] // 2, D), F32) for p in parts]

    def body(*refs):
        part_refs, land_refs = refs[:nw], refs[nw:2 * nw]
        send_sems, recv_sems = refs[2 * nw:2 * nw + 2]
        token = refs[-1]
        x, y, c = _position()
        for w in range(nw):
            for k, peer in enumerate(_xor_peers(x, y, c)):
                n = w * (NDEV - 1) + k
                _remote(_piece(part_refs[w], *peer), land_refs[w].at[k], send_sems.at[n], recv_sems.at[n],
                        peer).start()
        token[...] = jnp.zeros_like(token)

    n = nw * (NDEV - 1)
    res = _split_call(body, name, list(parts) + lands, (n, n), (TOKEN,))
    return res[0], res[1], res[2:2 + nw], res[2 + nw:2 + 2 * nw], res[-1]


def _reduce_wait(send_sems, recv_sems, parts, lands, after, name):
    nw = len(parts)

    def body(*refs):
        part_refs, land_refs = refs[:nw], refs[nw:2 * nw]
        send_sems, recv_sems = refs[2 * nw:2 * nw + 2]
        x, y, c = _position()
        for w in range(nw):
            for k, peer in enumerate(_xor_peers(x, y, c)):
                n = w * (NDEV - 1) + k
                cp = _remote(_piece(part_refs[w], *peer), land_refs[w].at[k], send_sems.at[n], recv_sems.at[n], peer)
                cp.wait_send()
                cp.wait_recv()

    operands = list(parts) + list(lands)
    res = pl.pallas_call(
        body, name=name, out_shape=tuple(pltpu.HBM(a.shape, a.dtype) for a in operands),
        in_specs=(HBM,) * (2 * nw) + (SEM, SEM, ANY), out_specs=(HBM,) * (2 * nw),
        input_output_aliases={i: i for i in range(2 * nw)},
        compiler_params=pltpu.CompilerParams(has_side_effects=DATAFLOW),
    )(*operands, send_sems, recv_sems, after)
    return res[:nw], res[nw:]


def _sum_pieces(part, land, sel, name):
    half = part.shape[-2] // 2
    br = 128 if half % 128 == 0 else half // 2
    nb = half // br

    def body(sel_ref, own_ref, *refs):
        acc = own_ref[...]
        for r in refs[:NDEV - 1]:
            acc = acc + r[...]
        refs[NDEV - 1][...] = acc

    if part.ndim == 3:
        own_spec = pl.BlockSpec((None, br, D), lambda i, sel_ref: (sel_ref[0], sel_ref[1] * nb + i, 0))
    else:
        own_spec = pl.BlockSpec((br, D), lambda i, sel_ref: (sel_ref[1] * nb + i, sel_ref[0]))
    slot_specs = [pl.BlockSpec((None, br, D), functools.partial(lambda i, sel_ref, k: (k, i, 0), k=k))
                  for k in range(NDEV - 1)]
    return pl.pallas_call(
        body, name=name,
        grid_spec=pltpu.PrefetchScalarGridSpec(
            num_scalar_prefetch=1, grid=(nb,), in_specs=[own_spec] + slot_specs,
            out_specs=pl.BlockSpec((br, D), lambda i, sel_ref: (i, 0))),
        out_shape=jax.ShapeDtypeStruct((half, D), F32),
        compiler_params=_cparams("arbitrary"),
    )(sel, part, *([land] * (NDEV - 1)))


def _share_start(halves, name):
    nw = len(halves)
    lands = [lax.empty(h.shape, F32) for h in halves]

    def body(*refs):
        h_refs, land_refs = refs[:nw], refs[nw:2 * nw]
        send_sems, recv_sems = refs[2 * nw:2 * nw + 2]
        token = refs[-1]
        x, y, c = _position()
        for w in range(nw):
            _remote(h_refs[w], land_refs[w], send_sems.at[w], recv_sems.at[w], (x, y, 1 - c)).start()
        token[...] = jnp.zeros_like(token)

    res = _split_call(body, name, list(halves) + lands, (nw, nw), (TOKEN,))
    return res[0], res[1], res[2:2 + nw], res[2 + nw:2 + 2 * nw], res[-1]


def _share_wait(send_sems, recv_sems, halves, lands, after, name):
    nw = len(halves)

    def body(*refs):
        h_refs, land_refs = refs[:nw], refs[nw:2 * nw]
        send_sems, recv_sems = refs[2 * nw:2 * nw + 2]
        x, y, c = _position()
        for w in range(nw):
            cp = _remote(h_refs[w], land_refs[w], send_sems.at[w], recv_sems.at[w], (x, y, 1 - c))
            cp.wait_send()
            cp.wait_recv()

    operands = list(halves) + list(lands)
    res = pl.pallas_call(
        body, name=name, out_shape=tuple(pltpu.HBM(a.shape, a.dtype) for a in operands),
        in_specs=(HBM,) * (2 * nw) + (SEM, SEM, ANY), out_specs=(HBM,) * (2 * nw),
        input_output_aliases={i: i for i in range(2 * nw)},
        compiler_params=pltpu.CompilerParams(has_side_effects=DATAFLOW),
    )(*operands, send_sems, recv_sems, after)
    return res[:nw], res[nw:]


def _join_halves(own, other, c):
    first = jnp.where(c == 0, own, other)
    second = jnp.where(c == 0, other, own)
    return jnp.concatenate([first, second], axis=0)


SMALL_SIZES = (("norm1_g", D), ("sgu_ln_g", GW), ("sgu_ln_b", GW), ("sgu_w", NG * CHUNK * CHUNK),
               ("sgu_b", NG * CHUNK), ("attn_out_g", A), ("gmlp_out_g", GW), ("norm2_g", D),
               ("final_norm_g", D))
PARAM_ROWS = sum(n for _, n in SMALL_SIZES) // LANES
SMALL_ROWS = PARAM_ROWS + 8


def _pack_small(tree, first_extra=None):
    extra = jnp.zeros((8 * LANES,), F32)
    if first_extra is not None:
        extra = extra.at[0].set(first_extra)
    flat = jnp.concatenate([tree[n].reshape(-1) for n, _ in SMALL_SIZES] + [extra])
    return flat.reshape(SMALL_ROWS, LANES)


def _unpack_small(pack, shapes):
    flat = pack.reshape(-1)
    out, off = {}, 0
    for n, size in SMALL_SIZES:
        out[n] = flat[off:off + size].reshape(shapes[n])
        off += size
    return out


def _small_allreduce_adamw(gpack, wpack, mpack, vpack):
    def body(g_ref, w_ref, m_ref, v_ref, go_ref, d_ref, mo_ref, vo_ref, slots, send_sems, recv_sems):
        x, y, c = _position()
        me = 4 * x + 2 * y + c
        slots[me] = g_ref[...]
        peers = _xor_peers(x, y, c)
        sends = []
        for k, peer in enumerate(peers):
            cp = _remote(g_ref, slots.at[me], send_sems.at[k], recv_sems.at[k], peer)
            cp.start()
            sends.append(cp)
        for k, (px, py, pc) in enumerate(peers):
            _remote(g_ref, slots.at[4 * px + 2 * py + pc], send_sems.at[k], recv_sems.at[k],
                    (px, py, pc)).wait_recv()
        for cp in sends:
            cp.wait_send()
        total = slots[0]
        for k in range(1, NDEV):
            total = total + slots[k]
        go_ref[...] = total
        d, mn, vn = _adamw_math(w_ref[...], total, m_ref[...], v_ref[...])
        d_ref[...] = d
        mo_ref[...] = mn
        vo_ref[...] = vn

    sd = jax.ShapeDtypeStruct((SMALL_ROWS, LANES), F32)
    vm = pl.BlockSpec(memory_space=pltpu.VMEM)
    return pl.pallas_call(
        body, name="small_allreduce_adamw", in_specs=[vm] * 4, out_specs=[vm] * 4, out_shape=[sd] * 4,
        scratch_shapes=[pltpu.VMEM((NDEV, SMALL_ROWS, LANES), F32), pltpu.SemaphoreType.DMA((NDEV - 1,)),
                        pltpu.SemaphoreType.DMA((NDEV - 1,))],
        compiler_params=pltpu.CompilerParams(has_side_effects=True),
    )(gpack, wpack, mpack, vpack)


def kernel(x, norm1_g, w_in, sgu_ln_g, sgu_ln_b, sgu_w, sgu_b, attn_out_g, gmlp_out_g, w_out, norm2_g, w_ff1, w_ff2, final_norm_g, loss_target, m_norm1_g, m_w_in, m_sgu_ln_g, m_sgu_ln_b, m_sgu_w, m_sgu_b, m_attn_out_g, m_gmlp_out_g, m_w_out, m_norm2_g, m_w_ff1, m_w_ff2, m_final_norm_g, v_norm1_g, v_w_in, v_sgu_ln_g, v_sgu_ln_b, v_sgu_w, v_sgu_b, v_attn_out_g, v_gmlp_out_g, v_w_out, v_norm2_g, v_w_ff1, v_w_ff2, v_final_norm_g):
    names = [n for n, _ in SMALL_SIZES]
    w_small = dict(norm1_g=norm1_g, sgu_ln_g=sgu_ln_g, sgu_ln_b=sgu_ln_b, sgu_w=sgu_w, sgu_b=sgu_b,
                   attn_out_g=attn_out_g, gmlp_out_g=gmlp_out_g, norm2_g=norm2_g, final_norm_g=final_norm_g)
    m_small = dict(norm1_g=m_norm1_g, sgu_ln_g=m_sgu_ln_g, sgu_ln_b=m_sgu_ln_b, sgu_w=m_sgu_w, sgu_b=m_sgu_b,
                   attn_out_g=m_attn_out_g, gmlp_out_g=m_gmlp_out_g, norm2_g=m_norm2_g,
                   final_norm_g=m_final_norm_g)
    v_small = dict(norm1_g=v_norm1_g, sgu_ln_g=v_sgu_ln_g, sgu_ln_b=v_sgu_ln_b, sgu_w=v_sgu_w, sgu_b=v_sgu_b,
                   attn_out_g=v_attn_out_g, gmlp_out_g=v_gmlp_out_g, norm2_g=v_norm2_g,
                   final_norm_g=v_final_norm_g)
    shapes = {n: w_small[n].shape for n in names}

    start_in = _gather_start([w_in[0].T.astype(BF16)], "gather_in_start")
    issued = start_in[4][0:1, 0:1]
    start_rest = _gather_start([(w_out[0] + issued).astype(BF16), w_ff1[0].astype(BF16), w_ff2[0].astype(BF16)],
                               "gather_rest_start")
    hn1 = _norm1(x[0], norm1_g + start_rest[4][0:1, 0:1])
    win_t = _gather_wait(*start_in[:4], after=hn1, name="gather_in_wait")[0].reshape(INW, D)

    def rest_weights(after):
        wout, wff1, wff2 = _gather_wait(*start_rest[:4], after=after, name="gather_rest_wait")
        return wout.reshape(D, D), wff1, wff2.reshape(DFF, D)

    small = dict(
        norm1_g=norm1_g, ln_g=sgu_ln_g.reshape(1, GW), ln_b=sgu_ln_b.reshape(1, GW), sgu_w=sgu_w[0],
        sgu_wt=jnp.swapaxes(sgu_w[0], 1, 2), bias_t=jnp.repeat(sgu_b[0].T, DH, axis=1),
        attn_out_g=attn_out_g, gmlp_out_g=gmlp_out_g, norm2_g=norm2_g, final_norm_g=final_norm_g.reshape(1, D))
    xi, yi, ci = _position()
    sel = jnp.stack([2 * xi + yi, ci]).astype(jnp.int32)
    state = {}

    def as_slabs(g):
        return g.reshape(NCHIP, g.shape[0] // NCHIP, D)

    def early_grads(gwff1, gwff2, gwout):
        state["early"] = _reduce_start([gwff1, as_slabs(gwff2), as_slabs(gwout)], "reduce_early_start")
        return state["early"][4][0:1, 0:1]

    def after_attention_bwd(marker):
        send_sems, recv_sems, parts, lands, _ = state["early"]
        parts, lands = _reduce_wait(send_sems, recv_sems, parts, lands, marker, "reduce_early_wait")
        halves = [_sum_pieces(p, l, sel, "sum_" + n) for p, l, n in zip(parts, lands, ("w_ff1", "w_ff2", "w_out"))]
        state["early_share"] = _share_start(halves, "share_early_start")
        return state["early_share"][4][0:1, 0:1]

    def late_grads(gwin_t):
        state["late"] = _reduce_start([as_slabs(gwin_t)], "reduce_late_start")
        return state["late"][4][0:1, 0:1]

    loss_part, dx, sg, _ = _local_step(
        x[0], hn1, loss_target[0], small, win_t, rest_weights, early_grads, after_attention_bwd, late_grads)
    late = state["late"]
    send_sems, recv_sems, halves, lands, _ = state["early_share"]
    own, other = _share_wait(send_sems, recv_sems, halves, lands, dx, "share_early_wait")
    g_big = {n: _join_halves(o, t, ci) for n, o, t in zip(("w_ff1", "w_ff2", "w_out"), own, other)}
    w_big = dict(w_in=(w_in, m_w_in, v_w_in), w_out=(w_out, m_w_out, v_w_out),
                 w_ff1=(w_ff1, m_w_ff1, v_w_ff1), w_ff2=(w_ff2, m_w_ff2, v_w_ff2))
    grads, deltas, new_m, new_v = {}, {}, {}, {}

    def update(n):
        w, m, v = w_big[n]
        d, mn, vn = _adamw(w[0], g_big[n], m[0], v[0], "adamw_" + n)
        grads[n], deltas[n], new_m[n], new_v[n] = g_big[n][None], d[None], mn[None], vn[None]

    for n in ("w_ff1", "w_ff2", "w_out"):
        update(n)
    updated = deltas["w_out"][0, 0:8, 0:LANES] + deltas["w_ff1"][0, 0:8, 0:LANES] + deltas["w_ff2"][0, 0:8, 0:LANES]
    late_parts, late_lands = _reduce_wait(late[0], late[1], late[2], late[3], updated, "reduce_late_wait")
    late_share = _share_start([_sum_pieces(late_parts[0], late_lands[0], sel, "sum_w_in")], "share_late_start")

    g_small = dict(norm1_g=sg["norm1_g"], sgu_ln_g=sg["ln_g"], sgu_ln_b=sg["ln_b"], sgu_w=sg["sgu_w"],
                   sgu_b=sg["sgu_b"], attn_out_g=sg["attn_out_g"], gmlp_out_g=sg["gmlp_out_g"],
                   norm2_g=sg["norm2_g"], final_norm_g=sg["final_norm_g"])
    packs = _small_allreduce_adamw(_pack_small(g_small, loss_part) + late_share[4][0:1, 0:1], _pack_small(w_small),
                                   _pack_small(m_small), _pack_small(v_small))
    loss = packs[0][PARAM_ROWS, 0]
    for tree, pack in zip((grads, deltas, new_m, new_v), packs):
        tree.update(_unpack_small(pack, shapes))
    own, other = _share_wait(late_share[0], late_share[1], late_share[2], late_share[3], packs[0], "share_late_wait")
    g_big["w_in"] = _join_halves(own[0], other[0], ci).T
    update("w_in")

    order = ["norm1_g", "w_in", "sgu_ln_g", "sgu_ln_b", "sgu_w", "sgu_b", "attn_out_g", "gmlp_out_g", "w_out",
             "norm2_g", "w_ff1", "w_ff2", "final_norm_g"]
    return (loss, dx[None], *[grads[n] for n in order], *[deltas[n] for n in order],
            *[new_m[n] for n in order], *[new_v[n] for n in order])
```

```python
import functools
import math

import numpy as np
import jax
import jax.numpy as jnp
from jax import lax
from jax.experimental import pallas as pl
from jax.experimental.pallas import tpu as pltpu

F32 = jnp.float32
BF16 = jnp.bfloat16

D = 1024
NH = 12
DH = 64
A = NH * DH
NG = 4
GW = NG * DH
INW = 3 * A + 2 * GW
DFF = 4 * D
CHUNK = 128
PATTERNS = ((128, 1), (512, 4), (2048, 16))
EPS = 1e-6
SCALE = DH ** -0.5
NEG = -1e30

LR, B1, B2, AEPS, WD, STEP = 0.001, 0.9, 0.999, 1e-08, 0.01, 10

TM = 512
TMX = 512
ATT_ROWS = 2048
FF_CH = 1024
LANES = 128
NCHIP = 4
VMEM_LIMIT = 56 * 1024 * 1024
MESH = pl.DeviceIdType.MESH


def _cparams(*sem, **kw):
    return pltpu.CompilerParams(dimension_semantics=sem if sem else None,
                                vmem_limit_bytes=VMEM_LIMIT, **kw)


def _dot(a, b):
    return jnp.dot(a, b, preferred_element_type=F32)


def _dot_nt(a, b):
    return lax.dot_general(a, b, (((1,), (1,)), ((), ())), preferred_element_type=F32)


def _dot_tn(a, b):
    return lax.dot_general(a, b, (((0,), (0,)), ((), ())), preferred_element_type=F32)


def _dot_hi(a, b):
    return jnp.dot(a, b, preferred_element_type=F32, precision=lax.Precision.HIGHEST)


def _alibi_slopes(n):
    def pow2(m):
        start = 2.0 ** (-8.0 / m)
        return [start ** (i + 1) for i in range(m)]
    if math.log2(n).is_integer():
        s = pow2(n)
    else:
        c = 2 ** int(math.floor(math.log2(n)))
        s = pow2(c) + pow2(2 * c)[0::2][: n - c]
    return np.asarray(s, dtype=np.float32)


def _rms_fwd(v, g):
    r = lax.rsqrt(jnp.mean(v * v, axis=-1, keepdims=True) + EPS)
    vn = v * r
    return vn * g, vn, r


def _rms_bwd(dy, vn, r, g):
    w = dy * g
    dv = r * (w - vn * jnp.mean(w * vn, axis=-1, keepdims=True))
    return dv, jnp.sum(dy * vn, axis=0, keepdims=True)


_K0 = math.sqrt(2.0 / math.pi)
_K1 = 0.044715


def _gelu(v):
    return 0.5 * v * (1.0 + jnp.tanh(_K0 * (v + _K1 * (v * v * v))))


def _gelu_grad(v):
    t = jnp.tanh(_K0 * (v + _K1 * (v * v * v)))
    return 0.5 * (1.0 + t) + 0.5 * v * (1.0 - t * t) * (_K0 * (1.0 + 3.0 * _K1 * v * v))


def _row_spec(rows, cols):
    return pl.BlockSpec((rows, cols), lambda i: (i, 0))


def _const_spec(shape):
    nd = len(shape)
    return pl.BlockSpec(shape, lambda i: (0,) * nd, pipeline_mode=pl.Buffered(1))


DILS = tuple(d for _, d in PATTERNS)


def _fill_cols(scr, value):
    for cb in range(value.shape[1] // LANES):
        scr[cb] = value[:, cb * LANES:(cb + 1) * LANES]


def _split_residues(scr, out_ref, dil):
    nb, rows, _ = scr.shape
    for r in range(dil):
        for cb in range(nb):
            piece = scr.at[cb][pl.ds(r, rows // dil, stride=dil), :]
            out_ref[r, :, cb * LANES:(cb + 1) * LANES] = piece.astype(out_ref.dtype)


def _merge_residues(in_ref, scr, dil):
    nb, rows, _ = scr.shape
    for r in range(dil):
        for cb in range(nb):
            scr.at[cb][pl.ds(r, rows // dil, stride=dil), :] = in_ref[r, :, cb * LANES:(cb + 1) * LANES].astype(F32)
    return jnp.concatenate([scr[cb] for cb in range(nb)], axis=-1)


def _col_scratch(rows, width):
    return pltpu.VMEM((width // LANES, rows, LANES), F32)


def _res_spec(dil, rows, width):
    return pl.BlockSpec((dil, rows // dil, width), lambda i: (0, i, 0))


def _res_shape(s, dil, width, dtype):
    return jax.ShapeDtypeStruct((dil, s // dil, width), dtype)


def _norm1(x, g1):
    s = x.shape[0]

    def body(x_ref, g_ref, hn_ref):
        hn, _, _ = _rms_fwd(x_ref[...], g_ref[...])
        hn_ref[...] = hn.astype(BF16)

    return pl.pallas_call(
        body, name="norm1", grid=(s // TM,), in_specs=[_row_spec(TM, D), _const_spec((1, D))],
        out_specs=_row_spec(TM, D), out_shape=jax.ShapeDtypeStruct((s, D), BF16),
        compiler_params=_cparams("arbitrary"),
    )(x, g1)


def _inproj_fwd(hn1, win_t):
    s = hn1.shape[0]
    nd = len(DILS)

    def body(hn_ref, w_ref, *rest):
        qkv_refs = rest[:3 * nd]
        u_ref, z_ref, scr = rest[3 * nd:]
        hn = hn_ref[...]
        for t in range(3):
            seg = _dot_nt(hn, w_ref[t * A:(t + 1) * A, :])
            seg = seg * SCALE if t == 0 else seg
            _fill_cols(scr, seg)
            for di, dil in enumerate(DILS):
                if dil == 1:
                    qkv_refs[t * nd + di][0] = seg.astype(BF16)
                else:
                    _split_residues(scr, qkv_refs[t * nd + di], dil)
        u_ref[...] = _dot_nt(hn, w_ref[3 * A:3 * A + GW, :])
        z_ref[...] = _dot_nt(hn, w_ref[3 * A + GW:INW, :])

    res = pl.pallas_call(
        body, name="inproj_fwd", grid=(s // TM,),
        in_specs=[_row_spec(TM, D), _const_spec((INW, D))],
        out_specs=[_res_spec(d, TM, A) for _ in range(3) for d in DILS] + [_row_spec(TM, GW), _row_spec(TM, GW)],
        out_shape=[_res_shape(s, d, A, BF16) for _ in range(3) for d in DILS]
                  + [jax.ShapeDtypeStruct((s, GW), F32)] * 2,
        scratch_shapes=[_col_scratch(TM, A)],
        compiler_params=_cparams("arbitrary"),
    )(hn1, win_t)
    q, k, v = (res[t * nd:(t + 1) * nd] for t in range(3))
    return q, k, v, res[-2], res[-1]


def _att_geometry(length, dil):
    merge = max(1, ATT_ROWS // length)
    rows = min(length * merge, ATT_ROWS)
    nsub = rows // CHUNK
    return merge, rows, length * merge // rows, nsub, min(length // CHUNK, nsub)


def _merged(t, merge):
    return t.reshape(t.shape[0] // merge, t.shape[1] * merge, t.shape[2])


def _stack_heads(t):
    lane = lax.broadcasted_iota(jnp.int32, t.shape, 1)
    zero = jnp.zeros_like(t)
    return jnp.concatenate([jnp.where(lane < DH, t, zero), jnp.where(lane >= DH, t, zero)], axis=0)


def _head_cols(t, hp):
    lane = lax.broadcasted_iota(jnp.int32, t.shape, 1)
    cols = [jnp.sum(jnp.where(lane == 2 * hp + h, t, 0.0), axis=-1, keepdims=True) for h in range(2)]
    return jnp.concatenate(cols, axis=0)


def _unstack_heads(t2):
    n = t2.shape[0] // 2
    lane = lax.broadcasted_iota(jnp.int32, (n, LANES), 1)
    return jnp.where(lane < DH, t2[:n], t2[n:])


def _query_window_bias(s0, s1, dil, first):
    row = lax.broadcasted_iota(jnp.int32, (2 * CHUNK, 2 * CHUNK), 0)
    col = lax.broadcasted_iota(jnp.int32, (2 * CHUNK, 2 * CHUNK), 1)
    steps = (row & (CHUNK - 1)) + CHUNK - col
    valid = (steps >= 0) & (steps <= CHUNK)
    if first:
        valid = valid & (col >= CHUNK)
    slope = jnp.where(row < CHUNK, s0, s1)
    return jnp.where(valid, -slope * (steps * dil).astype(F32), NEG)


def _key_block_bias(s0, s1, dil, last):
    key = lax.broadcasted_iota(jnp.int32, (CHUNK, 4 * CHUNK), 0)
    col = lax.broadcasted_iota(jnp.int32, (CHUNK, 4 * CHUNK), 1)
    wq = col & (2 * CHUNK - 1)
    steps = wq - key
    valid = (steps >= 0) & (steps <= CHUNK)
    if last:
        valid = valid & (wq < CHUNK)
    slope = jnp.where(col < 2 * CHUNK, s0, s1)
    return jnp.where(valid, -slope * (steps * dil).astype(F32), NEG)


def _head_rows(t, hp):
    row = lax.broadcasted_iota(jnp.int32, (8, LANES), 0)
    lane = lax.broadcasted_iota(jnp.int32, (8, LANES), 1)
    pick = jnp.where((row < 2) & (lane == 2 * hp + row), 1.0, 0.0).astype(BF16)
    hi = t.astype(BF16)
    rest = t - hi.astype(F32)
    mid = rest.astype(BF16)
    low = (rest - mid.astype(F32)).astype(BF16)
    return _dot_nt(pick, hi) + _dot_nt(pick, mid) + _dot_nt(pick, low)


def _att_specs(dil, rows, nsub, nblk):
    main = pl.BlockSpec((None, rows, LANES), lambda r, c, hp: (r, c, hp))
    prev = pl.BlockSpec((None, CHUNK, LANES), lambda r, c, hp: (r, jnp.maximum(c * nsub - 1, 0), hp))
    nxt = pl.BlockSpec((None, CHUNK, LANES), lambda r, c, hp: (r, jnp.minimum((c + 1) * nsub, nblk - 1), hp))
    main_heads = pl.BlockSpec((None, rows, LANES), lambda r, c, hp: (r, c, 0))
    nxt_heads = pl.BlockSpec((None, CHUNK, LANES), lambda r, c, hp: (r, jnp.minimum((c + 1) * nsub, nblk - 1), 0))
    return main, prev, nxt, main_heads, nxt_heads


def _row_start(i):
    return i * CHUNK if isinstance(i, int) else pl.multiple_of(i * CHUNK, CHUNK)


def _first_blocks(block, nsub, seg, nch, ch, first_bias, bias_buf):
    for i in range(nsub):
        if i % seg:
            block(i, bias_buf[...])
        elif nch == 1:
            block(i, first_bias())
        else:
            block(i, jnp.where(ch == 0, first_bias(), bias_buf[...]))


def _last_blocks(block, nsub, seg, nch, ch, last_bias, bias_buf):
    for i in range(nsub):
        if (i + 1) % seg:
            block(i, bias_buf[...])
        elif nch == 1:
            block(i, last_bias())
        else:
            block(i, jnp.where(ch == nch - 1, last_bias(), bias_buf[...]))


def _attn_fwd(q, k, v, slopes, dil):
    length = q.shape[1]
    merge, rows, nch, nsub, seg = _att_geometry(length, dil)
    main, prev, _, main_heads, _ = _att_specs(dil, rows, nsub, length * merge // CHUNK)
    q, k, v = (_merged(t, merge) for t in (q, k, v))

    def body(sl_ref, q_ref, k_ref, v_ref, kh_ref, vh_ref, o_ref, lse_ref, kbuf, vbuf, bias_buf):
        ch = pl.program_id(1)
        hp = pl.program_id(2)
        lane = lax.broadcasted_iota(jnp.int32, (CHUNK, LANES), 1)
        kbuf[0:CHUNK, :] = kh_ref[...]
        kbuf[CHUNK:, :] = k_ref[...]
        vbuf[0:CHUNK, :] = vh_ref[...]
        vbuf[CHUNK:, :] = v_ref[...]
        s0, s1 = sl_ref[2 * hp], sl_ref[2 * hp + 1]

        def block(i, bias):
            row = _row_start(i)
            rs = pl.ds(row, CHUNK)
            q2 = _stack_heads(q_ref[rs, :])
            kw = kbuf[pl.ds(row, 2 * CHUNK), :]
            vw = vbuf[pl.ds(row, 2 * CHUNK), :]
            sc = _dot_nt(q2, kw) + bias
            m = jnp.max(sc, axis=-1, keepdims=True)
            p = jnp.exp(sc - m)
            l = jnp.sum(p, axis=-1, keepdims=True)
            o2 = _dot(p.astype(BF16), vw) * (1.0 / l)
            o_ref[rs, :] = _unstack_heads(o2).astype(BF16)
            lse = m + jnp.log(l)
            seen = jnp.where(hp == 0, 0.0, lse_ref[rs, :])
            lse_ref[rs, :] = jnp.where(lane == 2 * hp, lse[:CHUNK], jnp.where(lane == 2 * hp + 1, lse[CHUNK:], seen))

        bias_buf[...] = _query_window_bias(s0, s1, dil, False)
        _first_blocks(block, nsub, seg, nch, ch, lambda: _query_window_bias(s0, s1, dil, True), bias_buf)

    sd = jax.ShapeDtypeStruct
    o, lse = pl.pallas_call(
        body, name=f"attn_fwd_d{dil}", grid=(dil // merge, nch, NH // 2),
        in_specs=[pl.BlockSpec(memory_space=pltpu.SMEM), main, main, main, prev, prev],
        out_specs=[main, main_heads],
        out_shape=[sd((dil // merge, length * merge, A), BF16), sd((dil // merge, length * merge, LANES), F32)],
        scratch_shapes=[pltpu.VMEM((rows + CHUNK, LANES), BF16), pltpu.VMEM((rows + CHUNK, LANES), BF16),
                        pltpu.VMEM((2 * CHUNK, 2 * CHUNK), F32)],
        compiler_params=_cparams("arbitrary", "arbitrary", "arbitrary"),
    )(slopes, q, k, v, k, v)
    return o.reshape(dil, length, A), lse.reshape(dil, length, LANES)


def _attn_bwd_dq(q, k, v, do, lse, delta, slopes, dil):
    length = q.shape[1]
    merge, rows, nch, nsub, seg = _att_geometry(length, dil)
    main, prev, _, main_heads, _ = _att_specs(dil, rows, nsub, length * merge // CHUNK)
    q, k, v, do, lse, delta = (_merged(t, merge) for t in (q, k, v, do, lse, delta))

    def body(sl_ref, q_ref, k_ref, v_ref, do_ref, lse_ref, dl_ref, kh_ref, vh_ref, dq_ref, kbuf, vbuf, bias_buf):
        ch = pl.program_id(1)
        hp = pl.program_id(2)
        kbuf[0:CHUNK, :] = kh_ref[...]
        kbuf[CHUNK:, :] = k_ref[...]
        vbuf[0:CHUNK, :] = vh_ref[...]
        vbuf[CHUNK:, :] = v_ref[...]
        s0, s1 = sl_ref[2 * hp], sl_ref[2 * hp + 1]

        def block(i, bias):
            row = _row_start(i)
            rs = pl.ds(row, CHUNK)
            q2 = _stack_heads(q_ref[rs, :])
            do2 = _stack_heads(do_ref[rs, :])
            lse2 = _head_cols(lse_ref[rs, :], hp)
            dl2 = _head_cols(dl_ref[rs, :], hp)
            kw = kbuf[pl.ds(row, 2 * CHUNK), :]
            vw = vbuf[pl.ds(row, 2 * CHUNK), :]
            p = jnp.exp(_dot_nt(q2, kw) + bias - lse2)
            ds = p * (_dot_nt(do2, vw) - dl2)
            dq_ref[rs, :] = _unstack_heads(_dot(ds.astype(BF16), kw)).astype(BF16)

        bias_buf[...] = _query_window_bias(s0, s1, dil, False)
        _first_blocks(block, nsub, seg, nch, ch, lambda: _query_window_bias(s0, s1, dil, True), bias_buf)

    dq = pl.pallas_call(
        body, name=f"attn_dq_d{dil}", grid=(dil // merge, nch, NH // 2),
        in_specs=[pl.BlockSpec(memory_space=pltpu.SMEM), main, main, main, main, main_heads, main_heads, prev, prev],
        out_specs=main, out_shape=jax.ShapeDtypeStruct((dil // merge, length * merge, A), BF16),
        scratch_shapes=[pltpu.VMEM((rows + CHUNK, LANES), BF16), pltpu.VMEM((rows + CHUNK, LANES), BF16),
                        pltpu.VMEM((2 * CHUNK, 2 * CHUNK), F32)],
        compiler_params=_cparams("arbitrary", "arbitrary", "arbitrary"),
    )(slopes, q, k, v, do, lse, delta, k, v)
    return dq.reshape(dil, length, A)


def _attn_bwd_dkv(q, k, v, do, lse, delta, slopes, dil):
    length = q.shape[1]
    merge, rows, nch, nsub, seg = _att_geometry(length, dil)
    main, _, nxt, main_heads, nxt_heads = _att_specs(dil, rows, nsub, length * merge // CHUNK)
    q, k, v, do, lse, delta = (_merged(t, merge) for t in (q, k, v, do, lse, delta))

    def body(sl_ref, k_ref, v_ref, q_ref, do_ref, lse_ref, dl_ref, qh_ref, doh_ref, lseh_ref, dlh_ref,
             dk_ref, dv_ref, qbuf, dobuf, lse_rows, dl_rows, bias_buf):
        ch = pl.program_id(1)
        hp = pl.program_id(2)
        for buf, main_ref, halo_ref in ((qbuf, q_ref, qh_ref), (dobuf, do_ref, doh_ref)):
            buf[0:rows, :] = main_ref[...]
            buf[rows:, :] = halo_ref[...]
        for buf, main_ref, halo_ref in ((lse_rows, lse_ref, lseh_ref), (dl_rows, dl_ref, dlh_ref)):
            buf[:, 0:rows] = _head_rows(main_ref[...], hp)
            buf[:, rows:] = _head_rows(halo_ref[...], hp)
        s0, s1 = sl_ref[2 * hp], sl_ref[2 * hp + 1]

        def block(i, bias):
            row = _row_start(i)
            rs = pl.ds(row, CHUNK)
            win = pl.ds(row, 2 * CHUNK)
            kc = k_ref[rs, :]
            vc = v_ref[rs, :]
            q2 = _stack_heads(qbuf[win, :])
            do2 = _stack_heads(dobuf[win, :])
            cols = slice(i * CHUNK, (i + 2) * CHUNK)
            lse2 = jnp.concatenate([lse_rows[0:1, cols], lse_rows[1:2, cols]], axis=1)
            dl2 = jnp.concatenate([dl_rows[0:1, cols], dl_rows[1:2, cols]], axis=1)
            pt = jnp.exp(_dot_nt(kc, q2) + bias - lse2)
            dst = pt * (_dot_nt(vc, do2) - dl2)
            dv_ref[rs, :] = _dot(pt.astype(BF16), do2).astype(BF16)
            dk_ref[rs, :] = _dot(dst.astype(BF16), q2).astype(BF16)

        bias_buf[...] = _key_block_bias(s0, s1, dil, False)
        _last_blocks(block, nsub, seg, nch, ch, lambda: _key_block_bias(s0, s1, dil, True), bias_buf)

    sd = jax.ShapeDtypeStruct((dil // merge, length * merge, A), BF16)
    dk, dv = pl.pallas_call(
        body, name=f"attn_dkv_d{dil}", grid=(dil // merge, nch, NH // 2),
        in_specs=[pl.BlockSpec(memory_space=pltpu.SMEM), main, main, main, main, main_heads, main_heads,
                  nxt, nxt, nxt_heads, nxt_heads],
        out_specs=[main, main], out_shape=[sd, sd],
        scratch_shapes=[pltpu.VMEM((rows + CHUNK, LANES), BF16), pltpu.VMEM((rows + CHUNK, LANES), BF16),
                        pltpu.VMEM((8, rows + CHUNK), F32), pltpu.VMEM((8, rows + CHUNK), F32),
                        pltpu.VMEM((CHUNK, 4 * CHUNK), F32)],
        compiler_params=_cparams("arbitrary", "arbitrary", "arbitrary"),
    )(slopes, k, v, q, do, lse, delta, q, do, lse, delta)
    return dk.reshape(dil, length, A), dv.reshape(dil, length, A)


def _group_masks(width):
    lane = lax.broadcasted_iota(jnp.int32, (1, width), 1)
    return [(lane >= g * DH) & (lane < (g + 1) * DH) for g in range(width // DH)]


def _group_mean_matrix():
    i = lax.broadcasted_iota(jnp.int32, (GW, GW), 0) // DH
    j = lax.broadcasted_iota(jnp.int32, (GW, GW), 1) // DH
    return jnp.where(i == j, 1.0 / DH, 0.0).astype(F32)


def _tri_mask(lower):
    t = lax.broadcasted_iota(jnp.int32, (CHUNK, CHUNK), 0)
    u = lax.broadcasted_iota(jnp.int32, (CHUNK, CHUNK), 1)
    return (u <= t) if lower else (u >= t)


def _sgu_forward(u, z, lng, lnb, w_ref, bias_t, pmat, rows):
    ug = _gelu(u)
    zg = _gelu(z)
    mu = _dot_hi(zg, pmat)
    zc = zg - mu
    var = _dot_hi(zc * zc, pmat)
    rstd = lax.rsqrt(var + EPS)
    zhat = zc * rstd
    zn = (zhat * lng + lnb).astype(BF16)
    gm = _group_masks(GW)
    tri = _tri_mask(True)
    ws = [jnp.where(tri, w_ref[g], 0.0).astype(BF16) for g in range(NG)]
    pieces = []
    for c in range(rows // CHUNK):
        znc = zn[c * CHUNK:(c + 1) * CHUNK, :]
        mix = None
        for g in range(NG):
            part = jnp.where(gm[g], _dot(ws[g], znc), 0.0)
            mix = part if mix is None else mix + part
        pieces.append(mix + bias_t)
    mixed = jnp.concatenate(pieces, axis=0) if len(pieces) > 1 else pieces[0]
    return ug * mixed, ug, zhat, rstd, zn, mixed


def _head_spread():
    h = lax.broadcasted_iota(jnp.int32, (LANES, A), 0)
    lane = lax.broadcasted_iota(jnp.int32, (LANES, A), 1)
    return jnp.where(lane // DH == h, 1.0, 0.0).astype(BF16)


def _bf16_pieces(t, n):
    pieces = []
    for _ in range(n):
        piece = t.astype(BF16)
        pieces.append(piece)
        t = t - piece.astype(F32)
    return pieces


def _mix_fwd(os_, ls_, u, z, x, lng, lnb, sgu_w, bias_t, ga, gg, wout):
    s = x.shape[0]
    nd = len(DILS)
    nscr = sum(1 for d in DILS if d > 1)

    def body(*refs):
        o_refs, l_refs = refs[:nd], refs[nd:2 * nd]
        u_ref, z_ref, x_ref, lng_ref, lnb_ref, w_ref, bt_ref, ga_ref, gg_ref, wo_ref = refs[2 * nd:2 * nd + 10]
        attn_ref = refs[2 * nd + 10]
        lse_refs = refs[2 * nd + 11:3 * nd + 11]
        mixed_ref, h1_ref = refs[3 * nd + 11:3 * nd + 13]
        scr = refs[3 * nd + 13:]
        scr_o, scr_l, scr_lse = scr[:nscr], scr[nscr:2 * nscr], scr[2 * nscr]
        ov, lv, j = [], [], 0
        for di, dil in enumerate(DILS):
            if dil == 1:
                ov.append(o_refs[di][0].astype(F32))
                lv.append(l_refs[di][0])
            else:
                ov.append(_merge_residues(o_refs[di], scr_o[j], dil))
                lv.append(_merge_residues(l_refs[di], scr_l[j], dil))
                j += 1
        mx = functools.reduce(jnp.maximum, lv)
        es = [jnp.exp(l - mx) for l in lv]
        den = functools.reduce(lambda a, b: a + b, es)
        spread = _head_spread()
        attn = None
        for e, o in zip(es, ov):
            wide = functools.reduce(lambda a, b: a + b, [_dot(piece, spread) for piece in _bf16_pieces(e / den, 2)])
            attn = wide * o if attn is None else attn + wide * o
        attn_ref[...] = attn
        lse = mx + jnp.log(den)
        _fill_cols(scr_lse, lse)
        for di, dil in enumerate(DILS):
            if dil == 1:
                lse_refs[di][0] = lse
            else:
                _split_residues(scr_lse, lse_refs[di], dil)
        an, _, _ = _rms_fwd(attn, ga_ref[...])
        gmv, _, _, _, _, _ = _sgu_forward(u_ref[...], z_ref[...], lng_ref[...], lnb_ref[...], w_ref,
                                          bt_ref[...], _group_mean_matrix(), TMX)
        gn, _, _ = _rms_fwd(gmv, gg_ref[...])
        mixed = jnp.concatenate([an, gn], axis=-1).astype(BF16)
        mixed_ref[...] = mixed
        h1_ref[...] = x_ref[...] + _dot(mixed, wo_ref[...])

    sd = jax.ShapeDtypeStruct
    res = pl.pallas_call(
        body, name="mix_fwd", grid=(s // TMX,),
        in_specs=[_res_spec(d, TMX, A) for d in DILS] + [_res_spec(d, TMX, LANES) for d in DILS]
                 + [_row_spec(TMX, GW), _row_spec(TMX, GW),
                    _row_spec(TMX, D), _const_spec((1, GW)), _const_spec((1, GW)), _const_spec((NG, CHUNK, CHUNK)),
                    _const_spec((CHUNK, GW)), _const_spec((1, A)), _const_spec((1, GW)), _const_spec((D, D))],
        out_specs=[_row_spec(TMX, A)] + [_res_spec(d, TMX, LANES) for d in DILS]
                  + [_row_spec(TMX, D), _row_spec(TMX, D)],
        out_shape=[sd((s, A), F32)] + [_res_shape(s, d, LANES, F32) for d in DILS]
                  + [sd((s, D), BF16), sd((s, D), F32)],
        scratch_shapes=[_col_scratch(TMX, A)] * nscr + [_col_scratch(TMX, LANES)] * (nscr + 1),
        compiler_params=_cparams("arbitrary"),
    )(*os_, *ls_, u, z, x, lng, lnb, sgu_w, bias_t, ga, gg, wout)
    return res[0], res[1:1 + nd], res[1 + nd], res[2 + nd]


def _mlp_fwd(h1, g2, wff1, wff2, gf, target):
    s = h1.shape[0]

    def body(h1_ref, g2_ref, w1_ref, w2_ref, gf_ref, t_ref, hn_ref, rf_ref, dh2_ref, loss_ref, dgf_ref):
        i = pl.program_id(0)
        h1v = h1_ref[...]
        hn, _, _ = _rms_fwd(h1v, g2_ref[...])
        hn = hn.astype(BF16)
        hn_ref[...] = hn
        acc = h1v
        for j in range(DFF // FF_CH):
            cols = slice(j * FF_CH, (j + 1) * FF_CH)
            rf = jnp.maximum(_dot(hn, w1_ref[j]), 0.0)
            act = (rf * rf).astype(BF16)
            rf_ref[:, cols] = rf.astype(BF16)
            acc = acc + _dot(act, w2_ref[cols, :])
        y, h2n, r3 = _rms_fwd(acc, gf_ref[...])
        err = y - t_ref[...]
        part = 0.5 * jnp.sum(jnp.mean(err * err, axis=-1, keepdims=True), axis=0, keepdims=True)
        dy = err * (1.0 / D)
        dh2, dgf = _rms_bwd(dy, h2n, r3, gf_ref[...])
        dh2_ref[...] = dh2

        @pl.when(i == 0)
        def _():
            loss_ref[...] = jnp.zeros_like(loss_ref)
            dgf_ref[...] = jnp.zeros_like(dgf_ref)

        loss_ref[...] += jnp.broadcast_to(part, loss_ref.shape)
        dgf_ref[...] += dgf

    sd = jax.ShapeDtypeStruct
    return pl.pallas_call(
        body, name="mlp_fwd", grid=(s // TM,),
        in_specs=[_row_spec(TM, D), _const_spec((1, D)), _const_spec((DFF // FF_CH, D, FF_CH)), _const_spec((DFF, D)),
                  _const_spec((1, D)), _row_spec(TM, D)],
        out_specs=[_row_spec(TM, D), _row_spec(TM, DFF), _row_spec(TM, D),
                   _const_spec((1, LANES)), _const_spec((1, D))],
        out_shape=[sd((s, D), BF16), sd((s, DFF), BF16), sd((s, D), F32),
                   sd((1, LANES), F32), sd((1, D), F32)],
        compiler_params=_cparams("arbitrary"),
    )(h1, g2, wff1, wff2, gf, target)


def _mlp_bwd(dh2, rf, h1, g2, wff1, wff2):
    s = h1.shape[0]

    def body(dh2_ref, rf_ref, h1_ref, g2_ref, w1_ref, w2_ref, df_ref, dh1_ref, dg2_ref):
        i = pl.program_id(0)
        dh2v = dh2_ref[...]
        dh2b = dh2v.astype(BF16)
        dhn = jnp.zeros((TM, D), F32)
        for j in range(DFF // FF_CH):
            cols = slice(j * FF_CH, (j + 1) * FF_CH)
            da = _dot_nt(dh2b, w2_ref[cols, :])
            df = (da * (2.0 * rf_ref[:, cols].astype(F32))).astype(BF16)
            df_ref[:, cols] = df
            dhn = dhn + _dot_nt(df, w1_ref[j])
        _, h1n, r2 = _rms_fwd(h1_ref[...], g2_ref[...])
        dres, dg2 = _rms_bwd(dhn, h1n, r2, g2_ref[...])
        dh1_ref[...] = dh2v + dres

        @pl.when(i == 0)
        def _():
            dg2_ref[...] = jnp.zeros_like(dg2_ref)

        dg2_ref[...] += dg2

    sd = jax.ShapeDtypeStruct
    return pl.pallas_call(
        body, name="mlp_bwd", grid=(s // TM,),
        in_specs=[_row_spec(TM, D), _row_spec(TM, DFF), _row_spec(TM, D), _const_spec((1, D)),
                  _const_spec((DFF // FF_CH, D, FF_CH)), _const_spec((DFF, D))],
        out_specs=[_row_spec(TM, DFF), _row_spec(TM, D), _const_spec((1, D))],
        out_shape=[sd((s, DFF), BF16), sd((s, D), F32), sd((1, D), F32)],
        compiler_params=_cparams("arbitrary"),
    )(dh2, rf, h1, g2, wff1, wff2)


def _mix_bwd(dh1, attn, u, z, lng, lnb, sgu_w, sgu_wt, bias_t, ga, gg, wout):
    s = dh1.shape[0]
    nsteps = s // TMX
    nd = len(DILS)

    def body(*refs):
        dh1_ref, attn_ref, u_ref, z_ref, lng_ref, lnb_ref, w_ref, wt_ref, bt_ref, ga_ref, gg_ref, wo_ref = refs[:12]
        do_refs, dl_refs = refs[12:12 + nd], refs[12 + nd:12 + 2 * nd]
        (du_ref, dz_ref, dga_ref, dgg_ref, dlng_ref, dlnb_ref, dws_ref, db_ref,
         dbt_acc, scr_do, scr_dl) = refs[12 + 2 * nd:]
        i = pl.program_id(0)

        @pl.when(i == 0)
        def _():
            for r in (dga_ref, dgg_ref, dlng_ref, dlnb_ref, dws_ref, db_ref, dbt_acc):
                r[...] = jnp.zeros_like(r)

        dmixed = _dot_nt(dh1_ref[...].astype(BF16), wo_ref[...])
        attn = attn_ref[...]
        _, an, ra = _rms_fwd(attn, ga_ref[...])
        dattn, dga = _rms_bwd(dmixed[:, :A], an, ra, ga_ref[...])
        dga_ref[...] += dga
        _fill_cols(scr_do, dattn)
        spread = _head_spread()
        delta = functools.reduce(lambda a, b: a + b, [_dot_nt(piece, spread) for piece in _bf16_pieces(dattn * attn, 3)])
        _fill_cols(scr_dl, delta)
        for di, dil in enumerate(DILS):
            if dil == 1:
                do_refs[di][0] = dattn.astype(BF16)
                dl_refs[di][0] = delta
            else:
                _split_residues(scr_do, do_refs[di], dil)
                _split_residues(scr_dl, dl_refs[di], dil)
        pmat = _group_mean_matrix()
        lng = lng_ref[...]
        uv, zv = u_ref[...], z_ref[...]
        gmv, ug, zhat, rstd, zn, mixed = _sgu_forward(uv, zv, lng, lnb_ref[...], w_ref, bt_ref[...], pmat, TMX)
        _, gmn, rg = _rms_fwd(gmv, gg_ref[...])
        dgm, dgg = _rms_bwd(dmixed[:, A:], gmn, rg, gg_ref[...])
        dgg_ref[...] += dgg
        du_ref[...] = dgm * mixed * _gelu_grad(uv)
        dmx = dgm * ug
        dmxb = dmx.astype(BF16)
        gm = _group_masks(GW)
        tri_t = _tri_mask(False)
        wst = [jnp.where(tri_t, wt_ref[g], 0.0).astype(BF16) for g in range(NG)]
        zero = jnp.zeros((CHUNK, GW), BF16)
        dzn_pieces = []
        for c in range(TMX // CHUNK):
            rs = slice(c * CHUNK, (c + 1) * CHUNK)
            dmc = dmxb[rs, :]
            znc = zn[rs, :]
            dbt_acc[...] += dmx[rs, :]
            dzn = None
            for g in range(NG):
                dws_ref[g] += _dot_nt(jnp.where(gm[g], dmc, zero), znc)
                part = jnp.where(gm[g], _dot(wst[g], dmc), 0.0)
                dzn = part if dzn is None else dzn + part
            dzn_pieces.append(dzn)
        dzn = jnp.concatenate(dzn_pieces, axis=0)
        dlng_ref[...] += jnp.sum(dzn * zhat, axis=0, keepdims=True)
        dlnb_ref[...] += jnp.sum(dzn, axis=0, keepdims=True)
        dzh = dzn * lng
        dzg = rstd * (dzh - _dot_hi(dzh, pmat) - zhat * _dot_hi(dzh * zhat, pmat))
        dz_ref[...] = dzg * _gelu_grad(zv)

        @pl.when(i == nsteps - 1)
        def _():
            tri = _tri_mask(True)
            for g in range(NG):
                dws_ref[g] = jnp.where(tri, dws_ref[g], 0.0)
            acc = dbt_acc[...]
            lane = lax.broadcasted_iota(jnp.int32, (CHUNK, LANES), 1)
            out = jnp.zeros((CHUNK, LANES), F32)
            for g in range(NG):
                sg = jnp.sum(jnp.where(gm[g], acc, 0.0), axis=-1, keepdims=True)
                out = jnp.where(lane == g, sg, out)
            db_ref[...] = out

    sd = jax.ShapeDtypeStruct
    res = pl.pallas_call(
        body, name="mix_bwd", grid=(nsteps,),
        in_specs=[_row_spec(TMX, D), _row_spec(TMX, A), _row_spec(TMX, GW), _row_spec(TMX, GW),
                  _const_spec((1, GW)), _const_spec((1, GW)), _const_spec((NG, CHUNK, CHUNK)),
                  _const_spec((NG, CHUNK, CHUNK)), _const_spec((CHUNK, GW)), _const_spec((1, A)),
                  _const_spec((1, GW)), _const_spec((D, D))],
        out_specs=[_res_spec(d, TMX, A) for d in DILS] + [_res_spec(d, TMX, LANES) for d in DILS]
                  + [_row_spec(TMX, GW), _row_spec(TMX, GW),
                   _const_spec((1, A)), _const_spec((1, GW)), _const_spec((1, GW)), _const_spec((1, GW)),
                   _const_spec((NG, CHUNK, CHUNK)), _const_spec((CHUNK, LANES))],
        out_shape=[_res_shape(s, d, A, BF16) for d in DILS] + [_res_shape(s, d, LANES, F32) for d in DILS]
                  + [sd((s, GW), F32), sd((s, GW), F32),
                   sd((1, A), F32), sd((1, GW), F32), sd((1, GW), F32), sd((1, GW), F32),
                   sd((NG, CHUNK, CHUNK), F32), sd((CHUNK, LANES), F32)],
        scratch_shapes=[pltpu.VMEM((CHUNK, GW), F32), _col_scratch(TMX, A), _col_scratch(TMX, LANES)],
        compiler_params=_cparams("arbitrary"),
    )(dh1, attn, u, z, lng, lnb, sgu_w, sgu_wt, bias_t, ga, gg, wout)
    return (res[:nd], res[nd:2 * nd]) + tuple(res[2 * nd:])


def _dproj_merge(dqs, dks, dvs, du, dz, pin):
    s = du.shape[0]
    nd = len(DILS)
    nscr = sum(1 for d in DILS if d > 1)

    def body(*refs):
        pin_ref = refs[0]
        parts = [refs[1 + t * nd:1 + (t + 1) * nd] for t in range(3)]
        du_ref, dz_ref, dp_ref = refs[1 + 3 * nd:4 + 3 * nd]
        scr = refs[4 + 3 * nd:]
        sums = []
        for t in range(3):
            total, j = None, 0
            for di, dil in enumerate(DILS):
                if dil == 1:
                    term = parts[t][di][0].astype(F32)
                else:
                    term = _merge_residues(parts[t][di], scr[t * nscr + j], dil)
                    j += 1
                total = term if total is None else total + term
            sums.append(total)
        dp_ref[...] = jnp.concatenate([sums[0] * SCALE, sums[1], sums[2], du_ref[...] + pin_ref[0, 0], dz_ref[...]],
                                      axis=-1).astype(BF16)

    return pl.pallas_call(
        body, name="dproj_merge", grid=(s // TMX,),
        in_specs=[pl.BlockSpec(memory_space=pltpu.SMEM)] + [_res_spec(d, TMX, A) for d in DILS] * 3
                 + [_row_spec(TMX, GW)] * 2,
        out_specs=_row_spec(TMX, INW), out_shape=jax.ShapeDtypeStruct((s, INW), BF16),
        scratch_shapes=[_col_scratch(TMX, A)] * (3 * nscr),
        compiler_params=_cparams("arbitrary"),
    )(pin, *dqs, *dks, *dvs, du, dz)


def _inproj_bwd(dproj, dh1, x, g1, win_t):
    s = x.shape[0]

    def body(dp_ref, dh1_ref, x_ref, g_ref, w_ref, dx_ref, dg_ref):
        i = pl.program_id(0)
        dhn = _dot(dp_ref[...], w_ref[...])
        _, xn, r1 = _rms_fwd(x_ref[...], g_ref[...])
        dres, dg = _rms_bwd(dhn, xn, r1, g_ref[...])
        dx_ref[...] = dh1_ref[...] + dres

        @pl.when(i == 0)
        def _():
            dg_ref[...] = jnp.zeros_like(dg_ref)

        dg_ref[...] += dg

    sd = jax.ShapeDtypeStruct
    return pl.pallas_call(
        body, name="inproj_bwd", grid=(s // TM,),
        in_specs=[_row_spec(TM, INW), _row_spec(TM, D), _row_spec(TM, D), _const_spec((1, D)), _const_spec((INW, D))],
        out_specs=[_row_spec(TM, D), _const_spec((1, D))],
        out_shape=[sd((s, D), F32), sd((1, D), F32)],
        compiler_params=_cparams("arbitrary"),
    )(dproj, dh1, x, g1, win_t)


def _wgrad(a, b, name, bm, bn, bk=2 * TM, square_a=False):
    s, m = a.shape
    n = b.shape[1]
    bm, bn = min(bm, m), min(bn, n)

    def body(a_ref, b_ref, o_ref):
        @pl.when(pl.program_id(2) == 0)
        def _():
            o_ref[...] = jnp.zeros_like(o_ref)

        av = a_ref[...]
        if square_a:
            av = av.astype(F32)
            av = av * av
        o_ref[...] += _dot_tn(av.astype(BF16), b_ref[...].astype(BF16))

    return pl.pallas_call(
        body, name=name, grid=(m // bm, n // bn, s // bk),
        in_specs=[pl.BlockSpec((bk, bm), lambda i, j, k: (k, i)), pl.BlockSpec((bk, bn), lambda i, j, k: (k, j))],
        out_specs=pl.BlockSpec((bm, bn), lambda i, j, k: (i, j)),
        out_shape=jax.ShapeDtypeStruct((m, n), F32),
        compiler_params=_cparams("arbitrary", "arbitrary", "arbitrary"),
    )(a, b)


def _adamw_math(w, g, m, v):
    m = B1 * m + (1.0 - B1) * g
    v = B2 * v + (1.0 - B2) * (g * g)
    m_hat = m / (1.0 - B1 ** STEP)
    v_hat = v / (1.0 - B2 ** STEP)
    delta = -LR * (m_hat / (jnp.sqrt(v_hat) + AEPS) + WD * w)
    return delta, m, v


def _adamw(w, g, m, v, name):
    rows, cols = w.shape
    br = min(rows, 256)
    while rows % br:
        br -= 8

    def body(w_ref, g_ref, m_ref, v_ref, d_ref, mo_ref, vo_ref):
        d, mn, vn = _adamw_math(w_ref[...], g_ref[...], m_ref[...], v_ref[...])
        d_ref[...] = d
        mo_ref[...] = mn
        vo_ref[...] = vn

    spec = _row_spec(br, cols)
    sd = jax.ShapeDtypeStruct((rows, cols), F32)
    return pl.pallas_call(
        body, name=name, grid=(rows // br,), in_specs=[spec] * 4, out_specs=[spec] * 3,
        out_shape=[sd, sd, sd], compiler_params=_cparams("arbitrary"),
    )(w, g, m, v)


def _local_step(x, hn1, target, small, win_t, rest_weights, early_grads=None, after_attention_bwd=None,
                late_grads=None):
    slopes = jnp.asarray(_alibi_slopes(NH))
    q, k, v, u, z = _inproj_fwd(hn1, win_t)
    outs, lses = [], []
    for i, dil in enumerate(DILS):
        o, l = _attn_fwd(q[i], k[i], v[i], slopes, dil)
        outs.append(o)
        lses.append(l)
    wout, wff1, wff2 = rest_weights(functools.reduce(lambda a, b: a + b, [l[0, 0:8, :] for l in lses]))
    attn, lse, mixed, h1 = _mix_fwd(outs, lses, u, z, x, small["ln_g"], small["ln_b"], small["sgu_w"],
                                    small["bias_t"], small["attn_out_g"], small["gmlp_out_g"], wout)
    hn2, rf, dh2, loss, dgf = _mlp_fwd(h1, small["norm2_g"], wff1, wff2, small["final_norm_g"], target)
    df, dh1, dg2 = _mlp_bwd(dh2, rf, h1, small["norm2_g"], wff1, wff2)
    gwff1 = _wgrad(hn2, df, "wgrad_ff1", D, 1024)
    gwff2 = _wgrad(rf, dh2, "wgrad_ff2", 1024, D, square_a=True)
    gwout = _wgrad(mixed, dh1, "wgrad_out", D, D)
    ga, g1 = small["attn_out_g"], small["norm1_g"]
    pin = early_grads(gwff1, gwff2, gwout) if early_grads else None
    if pin is not None:
        ga = ga + pin
    (do, delta, du, dz, dga, dgg, dlng, dlnb, dws, db) = _mix_bwd(
        dh1, attn, u, z, small["ln_g"], small["ln_b"], small["sgu_w"], small["sgu_wt"], small["bias_t"],
        ga, small["gmlp_out_g"], wout)
    dqs, dks, dvs = [], [], []
    for i, dil in enumerate(DILS):
        dqs.append(_attn_bwd_dq(q[i], k[i], v[i], do[i], lse[i], delta[i], slopes, dil))
        dk, dv = _attn_bwd_dkv(q[i], k[i], v[i], do[i], lse[i], delta[i], slopes, dil)
        dks.append(dk)
        dvs.append(dv)
    marker = functools.reduce(lambda a, b: a + b, [t[0, 0:8, 0:LANES] for t in dqs + dks + dvs])
    pin = after_attention_bwd(marker) if after_attention_bwd else None
    dproj = _dproj_merge(dqs, dks, dvs, du, dz, jnp.zeros((1, 1), F32) if pin is None else pin)
    gwin_t = _wgrad(dproj, hn1, "wgrad_in", INW // 2, D)
    pin = late_grads(gwin_t) if late_grads else None
    if pin is not None:
        g1 = g1 + pin
    dx, dg1 = _inproj_bwd(dproj, dh1, x, g1, win_t)
    small_grads = dict(norm1_g=dg1, ln_g=dlng, ln_b=dlnb, sgu_w=dws, sgu_b=db[:, :NG].T,
                       attn_out_g=dga, gmlp_out_g=dgg, norm2_g=dg2, final_norm_g=dgf)
    return loss[0, 0], dx, small_grads, (gwin_t, gwout, gwff1, gwff2)


ANY = pl.BlockSpec(memory_space=pl.ANY)
NDEV = 8


def _position():
    return lax.axis_index("x"), lax.axis_index("y"), lax.axis_index("c")


def _other_chips(x, y):
    return [(1 - x, y), (x, 1 - y), (1 - x, 1 - y)]


def _remote(src, dst, send_sem, recv_sem, device):
    return pltpu.make_async_remote_copy(src_ref=src, dst_ref=dst, send_sem=send_sem, recv_sem=recv_sem,
                                        device_id=device, device_id_type=MESH)


HBM = pl.BlockSpec(memory_space=pltpu.HBM)
SEM = pl.BlockSpec(memory_space=pltpu.SEMAPHORE)
DATAFLOW = pltpu.SideEffectType.DATAFLOW_SIDE_EFFECTING


def _in_hbm(a):
    return pltpu.with_memory_space_constraint(a, pltpu.HBM)


def _gather_start(shards, name):
    n = len(shards)
    lands = [jnp.broadcast_to(sh[None], (NCHIP,) + sh.shape) for sh in shards]

    def body(*refs):
        w_refs, land_refs = refs[:n], refs[n:2 * n]
        send_sems, recv_sems = refs[2 * n:2 * n + 2]
        token = refs[-1]
        x, y, c = _position()
        for w in range(n):
            for k, (px, py) in enumerate(_other_chips(x, y)):
                m = 3 * w + k
                _remote(w_refs[w], land_refs[w].at[2 * x + y], send_sems.at[m], recv_sems.at[m], (px, py, c)).start()
        token[...] = jnp.zeros_like(token)

    res = _split_call(body, name, list(shards) + lands, (3 * n, 3 * n), (TOKEN,))
    return res[0], res[1], res[2:2 + n], res[2 + n:2 + 2 * n], res[-1]


def _gather_wait(send_sems, recv_sems, shards, lands, after, name):
    n = len(shards)

    def body(*refs):
        w_refs, land_refs = refs[:n], refs[n:2 * n]
        send_sems, recv_sems = refs[2 * n:2 * n + 2]
        x, y, c = _position()
        for w in range(n):
            for k, (px, py) in enumerate(_other_chips(x, y)):
                m = 3 * w + k
                cp = _remote(w_refs[w], land_refs[w].at[2 * px + py], send_sems.at[m], recv_sems.at[m], (px, py, c))
                cp.wait_send()
                cp.wait_recv()

    operands = list(shards) + list(lands)
    res = pl.pallas_call(
        body, name=name, out_shape=tuple(pltpu.HBM(a.shape, a.dtype) for a in operands),
        in_specs=(HBM,) * (2 * n) + (SEM, SEM, ANY), out_specs=(HBM,) * (2 * n),
        input_output_aliases={i: i for i in range(2 * n)},
        compiler_params=pltpu.CompilerParams(has_side_effects=DATAFLOW),
    )(*operands, send_sems, recv_sems, after)
    return res[n:]


def _xor_peers(x, y, c):
    peers = []
    for k in range(1, NDEV):
        kx, ky, kc = (k >> 2) & 1, (k >> 1) & 1, k & 1
        peers.append((1 - x if kx else x, 1 - y if ky else y, 1 - c if kc else c))
    return peers


def _piece(part_ref, px, py, pc):
    slab = 2 * px + py
    if len(part_ref.shape) == 3:
        half = part_ref.shape[1] // 2
        return part_ref.at[slab, pl.ds(pc * half, half), :]
    half = part_ref.shape[0] // 2
    return part_ref.at[pl.ds(pc * half, half), pl.ds(pl.multiple_of(slab * D, D), D)]


def _split_call(body, name, operands, n_sems, extra_out=()):
    n = len(operands)
    sems = tuple(pltpu.SemaphoreType.DMA((m,)) for m in n_sems)
    thru = tuple(pltpu.HBM(a.shape, a.dtype) for a in operands)
    return pl.pallas_call(
        body, name=name, out_shape=sems + thru + tuple(extra_out),
        in_specs=(HBM,) * n,
        out_specs=(SEM,) * len(sems) + (HBM,) * n + (pl.BlockSpec(memory_space=pltpu.VMEM),) * len(extra_out),
        input_output_aliases={i: len(sems) + i for i in range(n)},
        compiler_params=pltpu.CompilerParams(has_side_effects=DATAFLOW),
    )(*[_in_hbm(a) for a in operands])


TOKEN = jax.ShapeDtypeStruct((8, LANES), F32)


def _reduce_start(parts, name):
    nw = len(parts)
    lands = [lax.empty((NDEV - 1, p.shape[-2] // 2, D), F32) for p in parts]

    def body(*refs):
        part_refs, land_refs = refs[:nw], refs[nw:2 * nw]
        send_sems, recv_sems = refs[2 * nw:2 * nw + 2]
        token = refs[-1]
        x, y, c = _position()
        for w in range(nw):
            for k, peer in enumerate(_xor_peers(x, y, c)):
                n = w * (NDEV - 1) + k
                _remote(_piece(part_refs[w], *peer), land_refs[w].at[k], send_sems.at[n], recv_sems.at[n],
                        peer).start()
        token[...] = jnp.zeros_like(token)

    n = nw * (NDEV - 1)
    res = _split_call(body, name, list(parts) + lands, (n, n), (TOKEN,))
    return res[0], res[1], res[2:2 + nw], res[2 + nw:2 + 2 * nw], res[-1]


def _reduce_wait(send_sems, recv_sems, parts, lands, after, name):
    nw = len(parts)

    def body(*refs):
        part_refs, land_refs = refs[:nw], refs[nw:2 * nw]
        send_sems, recv_sems = refs[2 * nw:2 * nw + 2]
        x, y, c = _position()
        for w in range(nw):
            for k, peer in enumerate(_xor_peers(x, y, c)):
                n = w * (NDEV - 1) + k
                cp = _remote(_piece(part_refs[w], *peer), land_refs[w].at[k], send_sems.at[n], recv_sems.at[n], peer)
                cp.wait_send()
                cp.wait_recv()

    operands = list(parts) + list(lands)
    res = pl.pallas_call(
        body, name=name, out_shape=tuple(pltpu.HBM(a.shape, a.dtype) for a in operands),
        in_specs=(HBM,) * (2 * nw) + (SEM, SEM, ANY), out_specs=(HBM,) * (2 * nw),
        input_output_aliases={i: i for i in range(2 * nw)},
        compiler_params=pltpu.CompilerParams(has_side_effects=DATAFLOW),
    )(*operands, send_sems, recv_sems, after)
    return res[:nw], res[nw:]


def _sum_pieces(part, land, sel, name):
    half = part.shape[-2] // 2
    br = 128 if half % 128 == 0 else half // 2
    nb = half // br

    def body(sel_ref, own_ref, *refs):
        acc = own_ref[...]
        for r in refs[:NDEV - 1]:
            acc = acc + r[...]
        refs[NDEV - 1][...] = acc

    if part.ndim == 3:
        own_spec = pl.BlockSpec((None, br, D), lambda i, sel_ref: (sel_ref[0], sel_ref[1] * nb + i, 0))
    else:
        own_spec = pl.BlockSpec((br, D), lambda i, sel_ref: (sel_ref[1] * nb + i, sel_ref[0]))
    slot_specs = [pl.BlockSpec((None, br, D), functools.partial(lambda i, sel_ref, k: (k, i, 0), k=k))
                  for k in range(NDEV - 1)]
    return pl.pallas_call(
        body, name=name,
        grid_spec=pltpu.PrefetchScalarGridSpec(
            num_scalar_prefetch=1, grid=(nb,), in_specs=[own_spec] + slot_specs,
            out_specs=pl.BlockSpec((br, D), lambda i, sel_ref: (i, 0))),
        out_shape=jax.ShapeDtypeStruct((half, D), F32),
        compiler_params=_cparams("arbitrary"),
    )(sel, part, *([land] * (NDEV - 1)))


def _share_start(halves, name):
    nw = len(halves)
    lands = [lax.empty(h.shape, F32) for h in halves]

    def body(*refs):
        h_refs, land_refs = refs[:nw], refs[nw:2 * nw]
        send_sems, recv_sems = refs[2 * nw:2 * nw + 2]
        token = refs[-1]
        x, y, c = _position()
        for w in range(nw):
            _remote(h_refs[w], land_refs[w], send_sems.at[w], recv_sems.at[w], (x, y, 1 - c)).start()
        token[...] = jnp.zeros_like(token)

    res = _split_call(body, name, list(halves) + lands, (nw, nw), (TOKEN,))
    return res[0], res[1], res[2:2 + nw], res[2 + nw:2 + 2 * nw], res[-1]


def _share_wait(send_sems, recv_sems, halves, lands, after, name):
    nw = len(halves)

    def body(*refs):
        h_refs, land_refs = refs[:nw], refs[nw:2 * nw]
        send_sems, recv_sems = refs[2 * nw:2 * nw + 2]
        x, y, c = _position()
        for w in range(nw):
            cp = _remote(h_refs[w], land_refs[w], send_sems.at[w], recv_sems.at[w], (x, y, 1 - c))
            cp.wait_send()
            cp.wait_recv()

    operands = list(halves) + list(lands)
    res = pl.pallas_call(
        body, name=name, out_shape=tuple(pltpu.HBM(a.shape, a.dtype) for a in operands),
        in_specs=(HBM,) * (2 * nw) + (SEM, SEM, ANY), out_specs=(HBM,) * (2 * nw),
        input_output_aliases={i: i for i in range(2 * nw)},
        compiler_params=pltpu.CompilerParams(has_side_effects=DATAFLOW),
    )(*operands, send_sems, recv_sems, after)
    return res[:nw], res[nw:]


def _join_halves(own, other, c):
    first = jnp.where(c == 0, own, other)
    second = jnp.where(c == 0, other, own)
    return jnp.concatenate([first, second], axis=0)


SMALL_SIZES = (("norm1_g", D), ("sgu_ln_g", GW), ("sgu_ln_b", GW), ("sgu_w", NG * CHUNK * CHUNK),
               ("sgu_b", NG * CHUNK), ("attn_out_g", A), ("gmlp_out_g", GW), ("norm2_g", D),
               ("final_norm_g", D))
PARAM_ROWS = sum(n for _, n in SMALL_SIZES) // LANES
SMALL_ROWS = PARAM_ROWS + 8


def _pack_small(tree, first_extra=None):
    extra = jnp.zeros((8 * LANES,), F32)
    if first_extra is not None:
        extra = extra.at[0].set(first_extra)
    flat = jnp.concatenate([tree[n].reshape(-1) for n, _ in SMALL_SIZES] + [extra])
    return flat.reshape(SMALL_ROWS, LANES)


def _unpack_small(pack, shapes):
    flat = pack.reshape(-1)
    out, off = {}, 0
    for n, size in SMALL_SIZES:
        out[n] = flat[off:off + size].reshape(shapes[n])
        off += size
    return out


def _small_allreduce_adamw(gpack, wpack, mpack, vpack):
    def body(g_ref, w_ref, m_ref, v_ref, go_ref, d_ref, mo_ref, vo_ref, slots, send_sems, recv_sems):
        x, y, c = _position()
        me = 4 * x + 2 * y + c
        slots[me] = g_ref[...]
        peers = _xor_peers(x, y, c)
        sends = []
        for k, peer in enumerate(peers):
            cp = _remote(g_ref, slots.at[me], send_sems.at[k], recv_sems.at[k], peer)
            cp.start()
            sends.append(cp)
        for k, (px, py, pc) in enumerate(peers):
            _remote(g_ref, slots.at[4 * px + 2 * py + pc], send_sems.at[k], recv_sems.at[k],
                    (px, py, pc)).wait_recv()
        for cp in sends:
            cp.wait_send()
        total = slots[0]
        for k in range(1, NDEV):
            total = total + slots[k]
        go_ref[...] = total
        d, mn, vn = _adamw_math(w_ref[...], total, m_ref[...], v_ref[...])
        d_ref[...] = d
        mo_ref[...] = mn
        vo_ref[...] = vn

    sd = jax.ShapeDtypeStruct((SMALL_ROWS, LANES), F32)
    vm = pl.BlockSpec(memory_space=pltpu.VMEM)
    return pl.pallas_call(
        body, name="small_allreduce_adamw", in_specs=[vm] * 4, out_specs=[vm] * 4, out_shape=[sd] * 4,
        scratch_shapes=[pltpu.VMEM((NDEV, SMALL_ROWS, LANES), F32), pltpu.SemaphoreType.DMA((NDEV - 1,)),
                        pltpu.SemaphoreType.DMA((NDEV - 1,))],
        compiler_params=pltpu.CompilerParams(has_side_effects=True),
    )(gpack, wpack, mpack, vpack)


def kernel(x, norm1_g, w_in, sgu_ln_g, sgu_ln_b, sgu_w, sgu_b, attn_out_g, gmlp_out_g, w_out, norm2_g, w_ff1, w_ff2, final_norm_g, loss_target, m_norm1_g, m_w_in, m_sgu_ln_g, m_sgu_ln_b, m_sgu_w, m_sgu_b, m_attn_out_g, m_gmlp_out_g, m_w_out, m_norm2_g, m_w_ff1, m_w_ff2, m_final_norm_g, v_norm1_g, v_w_in, v_sgu_ln_g, v_sgu_ln_b, v_sgu_w, v_sgu_b, v_attn_out_g, v_gmlp_out_g, v_w_out, v_norm2_g, v_w_ff1, v_w_ff2, v_final_norm_g):
    names = [n for n, _ in SMALL_SIZES]
    w_small = dict(norm1_g=norm1_g, sgu_ln_g=sgu_ln_g, sgu_ln_b=sgu_ln_b, sgu_w=sgu_w, sgu_b=sgu_b,
                   attn_out_g=attn_out_g, gmlp_out_g=gmlp_out_g, norm2_g=norm2_g, final_norm_g=final_norm_g)
    m_small = dict(norm1_g=m_norm1_g, sgu_ln_g=m_sgu_ln_g, sgu_ln_b=m_sgu_ln_b, sgu_w=m_sgu_w, sgu_b=m_sgu_b,
                   attn_out_g=m_attn_out_g, gmlp_out_g=m_gmlp_out_g, norm2_g=m_norm2_g,
                   final_norm_g=m_final_norm_g)
    v_small = dict(norm1_g=v_norm1_g, sgu_ln_g=v_sgu_ln_g, sgu_ln_b=v_sgu_ln_b, sgu_w=v_sgu_w, sgu_b=v_sgu_b,
                   attn_out_g=v_attn_out_g, gmlp_out_g=v_gmlp_out_g, norm2_g=v_norm2_g,
                   final_norm_g=v_final_norm_g)
    shapes = {n: w_small[n].shape for n in names}

    start_in = _gather_start([w_in[0].T.astype(BF16)], "gather_in_start")
    issued = start_in[4][0:1, 0:1]
    start_rest = _gather_start([(w_out[0] + issued).astype(BF16), w_ff1[0].astype(BF16), w_ff2[0].astype(BF16)],
                               "gather_rest_start")
    hn1 = _norm1(x[0], norm1_g + start_rest[4][0:1, 0:1])
    win_t = _gather_wait(*start_in[:4], after=hn1, name="gather_in_wait")[0].reshape(INW, D)

    def rest_weights(after):
        wout, wff1, wff2 = _gather_wait(*start_rest[:4], after=after, name="gather_rest_wait")
        return wout.reshape(D, D), wff1, wff2.reshape(DFF, D)

    small = dict(
        norm1_g=norm1_g, ln_g=sgu_ln_g.reshape(1, GW), ln_b=sgu_ln_b.reshape(1, GW), sgu_w=sgu_w[0],
        sgu_wt=jnp.swapaxes(sgu_w[0], 1, 2), bias_t=jnp.repeat(sgu_b[0].T, DH, axis=1),
        attn_out_g=attn_out_g, gmlp_out_g=gmlp_out_g, norm2_g=norm2_g, final_norm_g=final_norm_g.reshape(1, D))
    xi, yi, ci = _position()
    sel = jnp.stack([2 * xi + yi, ci]).astype(jnp.int32)
    state = {}

    def as_slabs(g):
        return g.reshape(NCHIP, g.shape[0] // NCHIP, D)

    def early_grads(gwff1, gwff2, gwout):
        state["early"] = _reduce_start([gwff1, as_slabs(gwff2), as_slabs(gwout)], "reduce_early_start")
        return state["early"][4][0:1, 0:1]

    def after_attention_bwd(marker):
        send_sems, recv_sems, parts, lands, _ = state["early"]
        parts, lands = _reduce_wait(send_sems, recv_sems, parts, lands, marker, "reduce_early_wait")
        halves = [_sum_pieces(p, l, sel, "sum_" + n) for p, l, n in zip(parts, lands, ("w_ff1", "w_ff2", "w_out"))]
        state["early_share"] = _share_start(halves, "share_early_start")
        return state["early_share"][4][0:1, 0:1]

    def late_grads(gwin_t):
        state["late"] = _reduce_start([as_slabs(gwin_t)], "reduce_late_start")
        return state["late"][4][0:1, 0:1]

    loss_part, dx, sg, _ = _local_step(
        x[0], hn1, loss_target[0], small, win_t, rest_weights, early_grads, after_attention_bwd, late_grads)
    late = state["late"]
    send_sems, recv_sems, halves, lands, _ = state["early_share"]
    own, other = _share_wait(send_sems, recv_sems, halves, lands, dx, "share_early_wait")
    g_big = {n: _join_halves(o, t, ci) for n, o, t in zip(("w_ff1", "w_ff2", "w_out"), own, other)}
    w_big = dict(w_in=(w_in, m_w_in, v_w_in), w_out=(w_out, m_w_out, v_w_out),
                 w_ff1=(w_ff1, m_w_ff1, v_w_ff1), w_ff2=(w_ff2, m_w_ff2, v_w_ff2))
    grads, deltas, new_m, new_v = {}, {}, {}, {}

    def update(n):
        w, m, v = w_big[n]
        d, mn, vn = _adamw(w[0], g_big[n], m[0], v[0], "adamw_" + n)
        grads[n], deltas[n], new_m[n], new_v[n] = g_big[n][None], d[None], mn[None], vn[None]

    for n in ("w_ff1", "w_ff2", "w_out"):
        update(n)
    updated = deltas["w_out"][0, 0:8, 0:LANES] + deltas["w_ff1"][0, 0:8, 0:LANES] + deltas["w_ff2"][0, 0:8, 0:LANES]
    late_parts, late_lands = _reduce_wait(late[0], late[1], late[2], late[3], updated, "reduce_late_wait")
    late_share = _share_start([_sum_pieces(late_parts[0], late_lands[0], sel, "sum_w_in")], "share_late_start")

    g_small = dict(norm1_g=sg["norm1_g"], sgu_ln_g=sg["ln_g"], sgu_ln_b=sg["ln_b"], sgu_w=sg["sgu_w"],
                   sgu_b=sg["sgu_b"], attn_out_g=sg["attn_out_g"], gmlp_out_g=sg["gmlp_out_g"],
                   norm2_g=sg["norm2_g"], final_norm_g=sg["final_norm_g"])
    packs = _small_allreduce_adamw(_pack_small(g_small, loss_part) + late_share[4][0:1, 0:1], _pack_small(w_small),
                                   _pack_small(m_small), _pack_small(v_small))
    loss = packs[0][PARAM_ROWS, 0]
    for tree, pack in zip((grads, deltas, new_m, new_v), packs):
        tree.update(_unpack_small(pack, shapes))
    own, other = _share_wait(late_share[0], late_share[1], late_share[2], late_share[3], packs[0], "share_late_wait")
    g_big["w_in"] = _join_halves(own[0], other[0], ci).T
    update("w_in")

    order = ["norm1_g", "w_in", "sgu_ln_g", "sgu_ln_b", "sgu_w", "sgu_b", "attn_out_g", "gmlp_out_g", "w_out",
             "norm2_g", "w_ff1", "w_ff2", "final_norm_g"]
    return (loss, dx[None], *[grads[n] for n in order], *[deltas[n] for n in order],
            *[new_m[n] for n in order], *[new_v[n] for n in order])
```

```python
import functools
import math

import numpy as np
import jax
import jax.numpy as jnp
from jax import lax
from jax.experimental import pallas as pl
from jax.experimental.pallas import tpu as pltpu

F32 = jnp.float32
BF16 = jnp.bfloat16

D = 1024
NH = 12
DH = 64
A = NH * DH
NG = 4
GW = NG * DH
INW = 3 * A + 2 * GW
DFF = 4 * D
CHUNK = 128
PATTERNS = ((128, 1), (512, 4), (2048, 16))
EPS = 1e-6
SCALE = DH ** -0.5
NEG = -1e30

LR, B1, B2, AEPS, WD, STEP = 0.001, 0.9, 0.999, 1e-08, 0.01, 10

TM = 512
TMX = 512
ATT_ROWS = 4096
FF_CH = 1024
LANES = 128
NCHIP = 4
VMEM_LIMIT = 56 * 1024 * 1024
MESH = pl.DeviceIdType.MESH


def _cparams(*sem, **kw):
    return pltpu.CompilerParams(dimension_semantics=sem if sem else None,
                                vmem_limit_bytes=VMEM_LIMIT, **kw)


def _dot(a, b):
    return jnp.dot(a, b, preferred_element_type=F32)


def _dot_nt(a, b):
    return lax.dot_general(a, b, (((1,), (1,)), ((), ())), preferred_element_type=F32)


def _dot_tn(a, b):
    return lax.dot_general(a, b, (((0,), (0,)), ((), ())), preferred_element_type=F32)


def _dot_hi(a, b):
    return jnp.dot(a, b, preferred_element_type=F32, precision=lax.Precision.HIGHEST)


def _alibi_slopes(n):
    def pow2(m):
        start = 2.0 ** (-8.0 / m)
        return [start ** (i + 1) for i in range(m)]
    if math.log2(n).is_integer():
        s = pow2(n)
    else:
        c = 2 ** int(math.floor(math.log2(n)))
        s = pow2(c) + pow2(2 * c)[0::2][: n - c]
    return np.asarray(s, dtype=np.float32)


def _rms_fwd(v, g):
    r = lax.rsqrt(jnp.mean(v * v, axis=-1, keepdims=True) + EPS)
    vn = v * r
    return vn * g, vn, r


def _rms_bwd(dy, vn, r, g):
    w = dy * g
    dv = r * (w - vn * jnp.mean(w * vn, axis=-1, keepdims=True))
    return dv, jnp.sum(dy * vn, axis=0, keepdims=True)


_K0 = math.sqrt(2.0 / math.pi)
_K1 = 0.044715


def _gelu(v):
    return 0.5 * v * (1.0 + jnp.tanh(_K0 * (v + _K1 * (v * v * v))))


def _gelu_grad(v):
    t = jnp.tanh(_K0 * (v + _K1 * (v * v * v)))
    return 0.5 * (1.0 + t) + 0.5 * v * (1.0 - t * t) * (_K0 * (1.0 + 3.0 * _K1 * v * v))


def _row_spec(rows, cols):
    return pl.BlockSpec((rows, cols), lambda i: (i, 0))


def _const_spec(shape):
    nd = len(shape)
    return pl.BlockSpec(shape, lambda i: (0,) * nd, pipeline_mode=pl.Buffered(1))


DILS = tuple(d for _, d in PATTERNS)


def _fill_cols(scr, value):
    for cb in range(value.shape[1] // LANES):
        scr[cb] = value[:, cb * LANES:(cb + 1) * LANES]


def _split_residues(scr, out_ref, dil):
    nb, rows, _ = scr.shape
    for r in range(dil):
        for cb in range(nb):
            piece = scr.at[cb][pl.ds(r, rows // dil, stride=dil), :]
            out_ref[r, :, cb * LANES:(cb + 1) * LANES] = piece.astype(out_ref.dtype)


def _merge_residues(in_ref, scr, dil):
    nb, rows, _ = scr.shape
    for r in range(dil):
        for cb in range(nb):
            scr.at[cb][pl.ds(r, rows // dil, stride=dil), :] = in_ref[r, :, cb * LANES:(cb + 1) * LANES].astype(F32)
    return jnp.concatenate([scr[cb] for cb in range(nb)], axis=-1)


def _col_scratch(rows, width):
    return pltpu.VMEM((width // LANES, rows, LANES), F32)


def _res_spec(dil, rows, width):
    return pl.BlockSpec((dil, rows // dil, width), lambda i: (0, i, 0))


def _res_shape(s, dil, width, dtype):
    return jax.ShapeDtypeStruct((dil, s // dil, width), dtype)


def _norm1(x, g1):
    s = x.shape[0]

    def body(x_ref, g_ref, hn_ref):
        hn, _, _ = _rms_fwd(x_ref[...], g_ref[...])
        hn_ref[...] = hn.astype(BF16)

    return pl.pallas_call(
        body, name="norm1", grid=(s // TM,), in_specs=[_row_spec(TM, D), _const_spec((1, D))],
        out_specs=_row_spec(TM, D), out_shape=jax.ShapeDtypeStruct((s, D), BF16),
        compiler_params=_cparams("arbitrary"),
    )(x, g1)


def _inproj_fwd(hn1, win_t):
    s = hn1.shape[0]
    nd = len(DILS)

    def body(hn_ref, w_ref, *rest):
        qkv_refs = rest[:3 * nd]
        u_ref, z_ref, scr = rest[3 * nd:]
        hn = hn_ref[...]
        for t in range(3):
            seg = _dot_nt(hn, w_ref[t * A:(t + 1) * A, :])
            seg = seg * SCALE if t == 0 else seg
            _fill_cols(scr, seg)
            for di, dil in enumerate(DILS):
                if dil == 1:
                    qkv_refs[t * nd + di][0] = seg.astype(BF16)
                else:
                    _split_residues(scr, qkv_refs[t * nd + di], dil)
        u_ref[...] = _dot_nt(hn, w_ref[3 * A:3 * A + GW, :])
        z_ref[...] = _dot_nt(hn, w_ref[3 * A + GW:INW, :])

    res = pl.pallas_call(
        body, name="inproj_fwd", grid=(s // TM,),
        in_specs=[_row_spec(TM, D), _const_spec((INW, D))],
        out_specs=[_res_spec(d, TM, A) for _ in range(3) for d in DILS] + [_row_spec(TM, GW), _row_spec(TM, GW)],
        out_shape=[_res_shape(s, d, A, BF16) for _ in range(3) for d in DILS]
                  + [jax.ShapeDtypeStruct((s, GW), F32)] * 2,
        scratch_shapes=[_col_scratch(TM, A)],
        compiler_params=_cparams("arbitrary"),
    )(hn1, win_t)
    q, k, v = (res[t * nd:(t + 1) * nd] for t in range(3))
    return q, k, v, res[-2], res[-1]


def _att_geometry(length, dil):
    merge = max(1, ATT_ROWS // length)
    rows = min(length * merge, ATT_ROWS)
    nsub = rows // CHUNK
    return merge, rows, length * merge // rows, nsub, min(length // CHUNK, nsub)


def _merged(t, merge):
    return t.reshape(t.shape[0] // merge, t.shape[1] * merge, t.shape[2])


def _stack_heads(t):
    lane = lax.broadcasted_iota(jnp.int32, t.shape, 1)
    zero = jnp.zeros_like(t)
    return jnp.concatenate([jnp.where(lane < DH, t, zero), jnp.where(lane >= DH, t, zero)], axis=0)


def _head_cols(t, hp):
    lane = lax.broadcasted_iota(jnp.int32, t.shape, 1)
    cols = [jnp.sum(jnp.where(lane == 2 * hp + h, t, 0.0), axis=-1, keepdims=True) for h in range(2)]
    return jnp.concatenate(cols, axis=0)


def _unstack_heads(t2):
    n = t2.shape[0] // 2
    lane = lax.broadcasted_iota(jnp.int32, (n, LANES), 1)
    return jnp.where(lane < DH, t2[:n], t2[n:])


def _query_window_bias(s0, s1, dil, first):
    row = lax.broadcasted_iota(jnp.int32, (2 * CHUNK, 2 * CHUNK), 0)
    col = lax.broadcasted_iota(jnp.int32, (2 * CHUNK, 2 * CHUNK), 1)
    steps = (row & (CHUNK - 1)) + CHUNK - col
    valid = (steps >= 0) & (steps <= CHUNK)
    if first:
        valid = valid & (col >= CHUNK)
    slope = jnp.where(row < CHUNK, s0, s1)
    return jnp.where(valid, -slope * (steps * dil).astype(F32), NEG)


def _key_block_bias(s0, s1, dil, last):
    key = lax.broadcasted_iota(jnp.int32, (CHUNK, 4 * CHUNK), 0)
    col = lax.broadcasted_iota(jnp.int32, (CHUNK, 4 * CHUNK), 1)
    wq = col & (2 * CHUNK - 1)
    steps = wq - key
    valid = (steps >= 0) & (steps <= CHUNK)
    if last:
        valid = valid & (wq < CHUNK)
    slope = jnp.where(col < 2 * CHUNK, s0, s1)
    return jnp.where(valid, -slope * (steps * dil).astype(F32), NEG)


def _head_rows(t, hp):
    row = lax.broadcasted_iota(jnp.int32, (8, LANES), 0)
    lane = lax.broadcasted_iota(jnp.int32, (8, LANES), 1)
    pick = jnp.where((row < 2) & (lane == 2 * hp + row), 1.0, 0.0).astype(BF16)
    hi = t.astype(BF16)
    rest = t - hi.astype(F32)
    mid = rest.astype(BF16)
    low = (rest - mid.astype(F32)).astype(BF16)
    return _dot_nt(pick, hi) + _dot_nt(pick, mid) + _dot_nt(pick, low)


def _att_specs(dil, rows, nsub, nblk):
    main = pl.BlockSpec((None, rows, LANES), lambda r, c, hp: (r, c, hp))
    prev = pl.BlockSpec((None, CHUNK, LANES), lambda r, c, hp: (r, jnp.maximum(c * nsub - 1, 0), hp))
    nxt = pl.BlockSpec((None, CHUNK, LANES), lambda r, c, hp: (r, jnp.minimum((c + 1) * nsub, nblk - 1), hp))
    main_heads = pl.BlockSpec((None, rows, LANES), lambda r, c, hp: (r, c, 0))
    nxt_heads = pl.BlockSpec((None, CHUNK, LANES), lambda r, c, hp: (r, jnp.minimum((c + 1) * nsub, nblk - 1), 0))
    return main, prev, nxt, main_heads, nxt_heads


def _row_start(i):
    return i * CHUNK if isinstance(i, int) else pl.multiple_of(i * CHUNK, CHUNK)


def _first_blocks(block, nsub, seg, nch, ch, first_bias, bias_buf):
    for i in range(nsub):
        if i % seg:
            block(i, bias_buf[...])
        elif nch == 1:
            block(i, first_bias())
        else:
            block(i, jnp.where(ch == 0, first_bias(), bias_buf[...]))


def _last_blocks(block, nsub, seg, nch, ch, last_bias, bias_buf):
    for i in range(nsub):
        if (i + 1) % seg:
            block(i, bias_buf[...])
        elif nch == 1:
            block(i, last_bias())
        else:
            block(i, jnp.where(ch == nch - 1, last_bias(), bias_buf[...]))


def _attn_fwd(q, k, v, slopes, dil):
    length = q.shape[1]
    merge, rows, nch, nsub, seg = _att_geometry(length, dil)
    main, prev, _, main_heads, _ = _att_specs(dil, rows, nsub, length * merge // CHUNK)
    q, k, v = (_merged(t, merge) for t in (q, k, v))

    def body(sl_ref, q_ref, k_ref, v_ref, kh_ref, vh_ref, o_ref, lse_ref, kbuf, vbuf, bias_buf):
        ch = pl.program_id(1)
        hp = pl.program_id(2)
        lane = lax.broadcasted_iota(jnp.int32, (CHUNK, LANES), 1)
        kbuf[0:CHUNK, :] = kh_ref[...]
        kbuf[CHUNK:, :] = k_ref[...]
        vbuf[0:CHUNK, :] = vh_ref[...]
        vbuf[CHUNK:, :] = v_ref[...]
        s0, s1 = sl_ref[2 * hp], sl_ref[2 * hp + 1]

        def block(i, bias):
            row = _row_start(i)
            rs = pl.ds(row, CHUNK)
            q2 = _stack_heads(q_ref[rs, :])
            kw = kbuf[pl.ds(row, 2 * CHUNK), :]
            vw = vbuf[pl.ds(row, 2 * CHUNK), :]
            sc = _dot_nt(q2, kw) + bias
            m = jnp.max(sc, axis=-1, keepdims=True)
            p = jnp.exp(sc - m)
            l = jnp.sum(p, axis=-1, keepdims=True)
            o2 = _dot(p.astype(BF16), vw) * (1.0 / l)
            o_ref[rs, :] = _unstack_heads(o2).astype(BF16)
            lse = m + jnp.log(l)
            seen = jnp.where(hp == 0, 0.0, lse_ref[rs, :])
            lse_ref[rs, :] = jnp.where(lane == 2 * hp, lse[:CHUNK], jnp.where(lane == 2 * hp + 1, lse[CHUNK:], seen))

        bias_buf[...] = _query_window_bias(s0, s1, dil, False)
        _first_blocks(block, nsub, seg, nch, ch, lambda: _query_window_bias(s0, s1, dil, True), bias_buf)

    sd = jax.ShapeDtypeStruct
    o, lse = pl.pallas_call(
        body, name=f"attn_fwd_d{dil}", grid=(dil // merge, nch, NH // 2),
        in_specs=[pl.BlockSpec(memory_space=pltpu.SMEM), main, main, main, prev, prev],
        out_specs=[main, main_heads],
        out_shape=[sd((dil // merge, length * merge, A), BF16), sd((dil // merge, length * merge, LANES), F32)],
        scratch_shapes=[pltpu.VMEM((rows + CHUNK, LANES), BF16), pltpu.VMEM((rows + CHUNK, LANES), BF16),
                        pltpu.VMEM((2 * CHUNK, 2 * CHUNK), F32)],
        compiler_params=_cparams("arbitrary", "arbitrary", "arbitrary"),
    )(slopes, q, k, v, k, v)
    return o.reshape(dil, length, A), lse.reshape(dil, length, LANES)


def _attn_bwd_dq(q, k, v, do, lse, delta, slopes, dil):
    length = q.shape[1]
    merge, rows, nch, nsub, seg = _att_geometry(length, dil)
    main, prev, _, main_heads, _ = _att_specs(dil, rows, nsub, length * merge // CHUNK)
    q, k, v, do, lse, delta = (_merged(t, merge) for t in (q, k, v, do, lse, delta))

    def body(sl_ref, q_ref, k_ref, v_ref, do_ref, lse_ref, dl_ref, kh_ref, vh_ref, dq_ref, kbuf, vbuf, bias_buf):
        ch = pl.program_id(1)
        hp = pl.program_id(2)
        kbuf[0:CHUNK, :] = kh_ref[...]
        kbuf[CHUNK:, :] = k_ref[...]
        vbuf[0:CHUNK, :] = vh_ref[...]
        vbuf[CHUNK:, :] = v_ref[...]
        s0, s1 = sl_ref[2 * hp], sl_ref[2 * hp + 1]

        def block(i, bias):
            row = _row_start(i)
            rs = pl.ds(row, CHUNK)
            q2 = _stack_heads(q_ref[rs, :])
            do2 = _stack_heads(do_ref[rs, :])
            lse2 = _head_cols(lse_ref[rs, :], hp)
            dl2 = _head_cols(dl_ref[rs, :], hp)
            kw = kbuf[pl.ds(row, 2 * CHUNK), :]
            vw = vbuf[pl.ds(row, 2 * CHUNK), :]
            p = jnp.exp(_dot_nt(q2, kw) + bias - lse2)
            ds = p * (_dot_nt(do2, vw) - dl2)
            dq_ref[rs, :] = _unstack_heads(_dot(ds.astype(BF16), kw)).astype(BF16)

        bias_buf[...] = _query_window_bias(s0, s1, dil, False)
        _first_blocks(block, nsub, seg, nch, ch, lambda: _query_window_bias(s0, s1, dil, True), bias_buf)

    dq = pl.pallas_call(
        body, name=f"attn_dq_d{dil}", grid=(dil // merge, nch, NH // 2),
        in_specs=[pl.BlockSpec(memory_space=pltpu.SMEM), main, main, main, main, main_heads, main_heads, prev, prev],
        out_specs=main, out_shape=jax.ShapeDtypeStruct((dil // merge, length * merge, A), BF16),
        scratch_shapes=[pltpu.VMEM((rows + CHUNK, LANES), BF16), pltpu.VMEM((rows + CHUNK, LANES), BF16),
                        pltpu.VMEM((2 * CHUNK, 2 * CHUNK), F32)],
        compiler_params=_cparams("arbitrary", "arbitrary", "arbitrary"),
    )(slopes, q, k, v, do, lse, delta, k, v)
    return dq.reshape(dil, length, A)


def _attn_bwd_dkv(q, k, v, do, lse, delta, slopes, dil):
    length = q.shape[1]
    merge, rows, nch, nsub, seg = _att_geometry(length, dil)
    main, _, nxt, main_heads, nxt_heads = _att_specs(dil, rows, nsub, length * merge // CHUNK)
    q, k, v, do, lse, delta = (_merged(t, merge) for t in (q, k, v, do, lse, delta))

    def body(sl_ref, k_ref, v_ref, q_ref, do_ref, lse_ref, dl_ref, qh_ref, doh_ref, lseh_ref, dlh_ref,
             dk_ref, dv_ref, qbuf, dobuf, lse_rows, dl_rows, bias_buf):
        ch = pl.program_id(1)
        hp = pl.program_id(2)
        for buf, main_ref, halo_ref in ((qbuf, q_ref, qh_ref), (dobuf, do_ref, doh_ref)):
            buf[0:rows, :] = main_ref[...]
            buf[rows:, :] = halo_ref[...]
        for buf, main_ref, halo_ref in ((lse_rows, lse_ref, lseh_ref), (dl_rows, dl_ref, dlh_ref)):
            buf[:, 0:rows] = _head_rows(main_ref[...], hp)
            buf[:, rows:] = _head_rows(halo_ref[...], hp)
        s0, s1 = sl_ref[2 * hp], sl_ref[2 * hp + 1]

        def block(i, bias):
            row = _row_start(i)
            rs = pl.ds(row, CHUNK)
            win = pl.ds(row, 2 * CHUNK)
            kc = k_ref[rs, :]
            vc = v_ref[rs, :]
            q2 = _stack_heads(qbuf[win, :])
            do2 = _stack_heads(dobuf[win, :])
            cols = slice(i * CHUNK, (i + 2) * CHUNK)
            lse2 = jnp.concatenate([lse_rows[0:1, cols], lse_rows[1:2, cols]], axis=1)
            dl2 = jnp.concatenate([dl_rows[0:1, cols], dl_rows[1:2, cols]], axis=1)
            pt = jnp.exp(_dot_nt(kc, q2) + bias - lse2)
            dst = pt * (_dot_nt(vc, do2) - dl2)
            dv_ref[rs, :] = _dot(pt.astype(BF16), do2).astype(BF16)
            dk_ref[rs, :] = _dot(dst.astype(BF16), q2).astype(BF16)

        bias_buf[...] = _key_block_bias(s0, s1, dil, False)
        _last_blocks(block, nsub, seg, nch, ch, lambda: _key_block_bias(s0, s1, dil, True), bias_buf)

    sd = jax.ShapeDtypeStruct((dil // merge, length * merge, A), BF16)
    dk, dv = pl.pallas_call(
        body, name=f"attn_dkv_d{dil}", grid=(dil // merge, nch, NH // 2),
        in_specs=[pl.BlockSpec(memory_space=pltpu.SMEM), main, main, main, main, main_heads, main_heads,
                  nxt, nxt, nxt_heads, nxt_heads],
        out_specs=[main, main], out_shape=[sd, sd],
        scratch_shapes=[pltpu.VMEM((rows + CHUNK, LANES), BF16), pltpu.VMEM((rows + CHUNK, LANES), BF16),
                        pltpu.VMEM((8, rows + CHUNK), F32), pltpu.VMEM((8, rows + CHUNK), F32),
                        pltpu.VMEM((CHUNK, 4 * CHUNK), F32)],
        compiler_params=_cparams("arbitrary", "arbitrary", "arbitrary"),
    )(slopes, k, v, q, do, lse, delta, q, do, lse, delta)
    return dk.reshape(dil, length, A), dv.reshape(dil, length, A)


def _group_masks(width):
    lane = lax.broadcasted_iota(jnp.int32, (1, width), 1)
    return [(lane >= g * DH) & (lane < (g + 1) * DH) for g in range(width // DH)]


def _group_mean_matrix():
    i = lax.broadcasted_iota(jnp.int32, (GW, GW), 0) // DH
    j = lax.broadcasted_iota(jnp.int32, (GW, GW), 1) // DH
    return jnp.where(i == j, 1.0 / DH, 0.0).astype(F32)


def _tri_mask(lower):
    t = lax.broadcasted_iota(jnp.int32, (CHUNK, CHUNK), 0)
    u = lax.broadcasted_iota(jnp.int32, (CHUNK, CHUNK), 1)
    return (u <= t) if lower else (u >= t)


def _sgu_forward(u, z, lng, lnb, w_ref, bias_t, pmat, rows):
    ug = _gelu(u)
    zg = _gelu(z)
    mu = _dot_hi(zg, pmat)
    zc = zg - mu
    var = _dot_hi(zc * zc, pmat)
    rstd = lax.rsqrt(var + EPS)
    zhat = zc * rstd
    zn = (zhat * lng + lnb).astype(BF16)
    gm = _group_masks(GW)
    tri = _tri_mask(True)
    ws = [jnp.where(tri, w_ref[g], 0.0).astype(BF16) for g in range(NG)]
    pieces = []
    for c in range(rows // CHUNK):
        znc = zn[c * CHUNK:(c + 1) * CHUNK, :]
        mix = None
        for g in range(NG):
            part = jnp.where(gm[g], _dot(ws[g], znc), 0.0)
            mix = part if mix is None else mix + part
        pieces.append(mix + bias_t)
    mixed = jnp.concatenate(pieces, axis=0) if len(pieces) > 1 else pieces[0]
    return ug * mixed, ug, zhat, rstd, zn, mixed


def _head_spread():
    h = lax.broadcasted_iota(jnp.int32, (LANES, A), 0)
    lane = lax.broadcasted_iota(jnp.int32, (LANES, A), 1)
    return jnp.where(lane // DH == h, 1.0, 0.0).astype(BF16)


def _bf16_pieces(t, n):
    pieces = []
    for _ in range(n):
        piece = t.astype(BF16)
        pieces.append(piece)
        t = t - piece.astype(F32)
    return pieces


def _mix_fwd(os_, ls_, u, z, x, lng, lnb, sgu_w, bias_t, ga, gg, wout):
    s = x.shape[0]
    nd = len(DILS)
    nscr = sum(1 for d in DILS if d > 1)

    def body(*refs):
        o_refs, l_refs = refs[:nd], refs[nd:2 * nd]
        u_ref, z_ref, x_ref, lng_ref, lnb_ref, w_ref, bt_ref, ga_ref, gg_ref, wo_ref = refs[2 * nd:2 * nd + 10]
        attn_ref = refs[2 * nd + 10]
        lse_refs = refs[2 * nd + 11:3 * nd + 11]
        mixed_ref, h1_ref = refs[3 * nd + 11:3 * nd + 13]
        scr = refs[3 * nd + 13:]
        scr_o, scr_l, scr_lse = scr[:nscr], scr[nscr:2 * nscr], scr[2 * nscr]
        ov, lv, j = [], [], 0
        for di, dil in enumerate(DILS):
            if dil == 1:
                ov.append(o_refs[di][0].astype(F32))
                lv.append(l_refs[di][0])
            else:
                ov.append(_merge_residues(o_refs[di], scr_o[j], dil))
                lv.append(_merge_residues(l_refs[di], scr_l[j], dil))
                j += 1
        mx = functools.reduce(jnp.maximum, lv)
        es = [jnp.exp(l - mx) for l in lv]
        den = functools.reduce(lambda a, b: a + b, es)
        spread = _head_spread()
        attn = None
        for e, o in zip(es, ov):
            wide = functools.reduce(lambda a, b: a + b, [_dot(piece, spread) for piece in _bf16_pieces(e / den, 2)])
            attn = wide * o if attn is None else attn + wide * o
        attn_ref[...] = attn
        lse = mx + jnp.log(den)
        _fill_cols(scr_lse, lse)
        for di, dil in enumerate(DILS):
            if dil == 1:
                lse_refs[di][0] = lse
            else:
                _split_residues(scr_lse, lse_refs[di], dil)
        an, _, _ = _rms_fwd(attn, ga_ref[...])
        gmv, _, _, _, _, _ = _sgu_forward(u_ref[...], z_ref[...], lng_ref[...], lnb_ref[...], w_ref,
                                          bt_ref[...], _group_mean_matrix(), TMX)
        gn, _, _ = _rms_fwd(gmv, gg_ref[...])
        mixed = jnp.concatenate([an, gn], axis=-1).astype(BF16)
        mixed_ref[...] = mixed
        h1_ref[...] = x_ref[...] + _dot(mixed, wo_ref[...])

    sd = jax.ShapeDtypeStruct
    res = pl.pallas_call(
        body, name="mix_fwd", grid=(s // TMX,),
        in_specs=[_res_spec(d, TMX, A) for d in DILS] + [_res_spec(d, TMX, LANES) for d in DILS]
                 + [_row_spec(TMX, GW), _row_spec(TMX, GW),
                    _row_spec(TMX, D), _const_spec((1, GW)), _const_spec((1, GW)), _const_spec((NG, CHUNK, CHUNK)),
                    _const_spec((CHUNK, GW)), _const_spec((1, A)), _const_spec((1, GW)), _const_spec((D, D))],
        out_specs=[_row_spec(TMX, A)] + [_res_spec(d, TMX, LANES) for d in DILS]
                  + [_row_spec(TMX, D), _row_spec(TMX, D)],
        out_shape=[sd((s, A), F32)] + [_res_shape(s, d, LANES, F32) for d in DILS]
                  + [sd((s, D), BF16), sd((s, D), F32)],
        scratch_shapes=[_col_scratch(TMX, A)] * nscr + [_col_scratch(TMX, LANES)] * (nscr + 1),
        compiler_params=_cparams("arbitrary"),
    )(*os_, *ls_, u, z, x, lng, lnb, sgu_w, bias_t, ga, gg, wout)
    return res[0], res[1:1 + nd], res[1 + nd], res[2 + nd]


def _mlp_fwd(h1, g2, wff1, wff2, gf, target):
    s = h1.shape[0]

    def body(h1_ref, g2_ref, w1_ref, w2_ref, gf_ref, t_ref, hn_ref, rf_ref, dh2_ref, loss_ref, dgf_ref):
        i = pl.program_id(0)
        h1v = h1_ref[...]
        hn, _, _ = _rms_fwd(h1v, g2_ref[...])
        hn = hn.astype(BF16)
        hn_ref[...] = hn
        acc = h1v
        for j in range(DFF // FF_CH):
            cols = slice(j * FF_CH, (j + 1) * FF_CH)
            rf = jnp.maximum(_dot(hn, w1_ref[j]), 0.0)
            act = (rf * rf).astype(BF16)
            rf_ref[:, cols] = rf.astype(BF16)
            acc = acc + _dot(act, w2_ref[cols, :])
        y, h2n, r3 = _rms_fwd(acc, gf_ref[...])
        err = y - t_ref[...]
        part = 0.5 * jnp.sum(jnp.mean(err * err, axis=-1, keepdims=True), axis=0, keepdims=True)
        dy = err * (1.0 / D)
        dh2, dgf = _rms_bwd(dy, h2n, r3, gf_ref[...])
        dh2_ref[...] = dh2

        @pl.when(i == 0)
        def _():
            loss_ref[...] = jnp.zeros_like(loss_ref)
            dgf_ref[...] = jnp.zeros_like(dgf_ref)

        loss_ref[...] += jnp.broadcast_to(part, loss_ref.shape)
        dgf_ref[...] += dgf

    sd = jax.ShapeDtypeStruct
    return pl.pallas_call(
        body, name="mlp_fwd", grid=(s // TM,),
        in_specs=[_row_spec(TM, D), _const_spec((1, D)), _const_spec((DFF // FF_CH, D, FF_CH)), _const_spec((DFF, D)),
                  _const_spec((1, D)), _row_spec(TM, D)],
        out_specs=[_row_spec(TM, D), _row_spec(TM, DFF), _row_spec(TM, D),
                   _const_spec((1, LANES)), _const_spec((1, D))],
        out_shape=[sd((s, D), BF16), sd((s, DFF), BF16), sd((s, D), F32),
                   sd((1, LANES), F32), sd((1, D), F32)],
        compiler_params=_cparams("arbitrary"),
    )(h1, g2, wff1, wff2, gf, target)


def _mlp_bwd(dh2, rf, h1, g2, wff1, wff2):
    s = h1.shape[0]

    def body(dh2_ref, rf_ref, h1_ref, g2_ref, w1_ref, w2_ref, df_ref, dh1_ref, dg2_ref):
        i = pl.program_id(0)
        dh2v = dh2_ref[...]
        dh2b = dh2v.astype(BF16)
        dhn = jnp.zeros((TM, D), F32)
        for j in range(DFF // FF_CH):
            cols = slice(j * FF_CH, (j + 1) * FF_CH)
            da = _dot_nt(dh2b, w2_ref[cols, :])
            df = (da * (2.0 * rf_ref[:, cols].astype(F32))).astype(BF16)
            df_ref[:, cols] = df
            dhn = dhn + _dot_nt(df, w1_ref[j])
        _, h1n, r2 = _rms_fwd(h1_ref[...], g2_ref[...])
        dres, dg2 = _rms_bwd(dhn, h1n, r2, g2_ref[...])
        dh1_ref[...] = dh2v + dres

        @pl.when(i == 0)
        def _():
            dg2_ref[...] = jnp.zeros_like(dg2_ref)

        dg2_ref[...] += dg2

    sd = jax.ShapeDtypeStruct
    return pl.pallas_call(
        body, name="mlp_bwd", grid=(s // TM,),
        in_specs=[_row_spec(TM, D), _row_spec(TM, DFF), _row_spec(TM, D), _const_spec((1, D)),
                  _const_spec((DFF // FF_CH, D, FF_CH)), _const_spec((DFF, D))],
        out_specs=[_row_spec(TM, DFF), _row_spec(TM, D), _const_spec((1, D))],
        out_shape=[sd((s, DFF), BF16), sd((s, D), F32), sd((1, D), F32)],
        compiler_params=_cparams("arbitrary"),
    )(dh2, rf, h1, g2, wff1, wff2)


def _mix_bwd(dh1, attn, u, z, lng, lnb, sgu_w, sgu_wt, bias_t, ga, gg, wout):
    s = dh1.shape[0]
    nsteps = s // TMX
    nd = len(DILS)

    def body(*refs):
        dh1_ref, attn_ref, u_ref, z_ref, lng_ref, lnb_ref, w_ref, wt_ref, bt_ref, ga_ref, gg_ref, wo_ref = refs[:12]
        do_refs, dl_refs = refs[12:12 + nd], refs[12 + nd:12 + 2 * nd]
        (du_ref, dz_ref, dga_ref, dgg_ref, dlng_ref, dlnb_ref, dws_ref, db_ref,
         dbt_acc, scr_do, scr_dl) = refs[12 + 2 * nd:]
        i = pl.program_id(0)

        @pl.when(i == 0)
        def _():
            for r in (dga_ref, dgg_ref, dlng_ref, dlnb_ref, dws_ref, db_ref, dbt_acc):
                r[...] = jnp.zeros_like(r)

        dmixed = _dot_nt(dh1_ref[...].astype(BF16), wo_ref[...])
        attn = attn_ref[...]
        _, an, ra = _rms_fwd(attn, ga_ref[...])
        dattn, dga = _rms_bwd(dmixed[:, :A], an, ra, ga_ref[...])
        dga_ref[...] += dga
        _fill_cols(scr_do, dattn)
        spread = _head_spread()
        delta = functools.reduce(lambda a, b: a + b, [_dot_nt(piece, spread) for piece in _bf16_pieces(dattn * attn, 3)])
        _fill_cols(scr_dl, delta)
        for di, dil in enumerate(DILS):
            if dil == 1:
                do_refs[di][0] = dattn.astype(BF16)
                dl_refs[di][0] = delta
            else:
                _split_residues(scr_do, do_refs[di], dil)
                _split_residues(scr_dl, dl_refs[di], dil)
        pmat = _group_mean_matrix()
        lng = lng_ref[...]
        uv, zv = u_ref[...], z_ref[...]
        gmv, ug, zhat, rstd, zn, mixed = _sgu_forward(uv, zv, lng, lnb_ref[...], w_ref, bt_ref[...], pmat, TMX)
        _, gmn, rg = _rms_fwd(gmv, gg_ref[...])
        dgm, dgg = _rms_bwd(dmixed[:, A:], gmn, rg, gg_ref[...])
        dgg_ref[...] += dgg
        du_ref[...] = dgm * mixed * _gelu_grad(uv)
        dmx = dgm * ug
        dmxb = dmx.astype(BF16)
        gm = _group_masks(GW)
        tri_t = _tri_mask(False)
        wst = [jnp.where(tri_t, wt_ref[g], 0.0).astype(BF16) for g in range(NG)]
        zero = jnp.zeros((CHUNK, GW), BF16)
        dzn_pieces = []
        for c in range(TMX // CHUNK):
            rs = slice(c * CHUNK, (c + 1) * CHUNK)
            dmc = dmxb[rs, :]
            znc = zn[rs, :]
            dbt_acc[...] += dmx[rs, :]
            dzn = None
            for g in range(NG):
                dws_ref[g] += _dot_nt(jnp.where(gm[g], dmc, zero), znc)
                part = jnp.where(gm[g], _dot(wst[g], dmc), 0.0)
                dzn = part if dzn is None else dzn + part
            dzn_pieces.append(dzn)
        dzn = jnp.concatenate(dzn_pieces, axis=0)
        dlng_ref[...] += jnp.sum(dzn * zhat, axis=0, keepdims=True)
        dlnb_ref[...] += jnp.sum(dzn, axis=0, keepdims=True)
        dzh = dzn * lng
        dzg = rstd * (dzh - _dot_hi(dzh, pmat) - zhat * _dot_hi(dzh * zhat, pmat))
        dz_ref[...] = dzg * _gelu_grad(zv)

        @pl.when(i == nsteps - 1)
        def _():
            tri = _tri_mask(True)
            for g in range(NG):
                dws_ref[g] = jnp.where(tri, dws_ref[g], 0.0)
            acc = dbt_acc[...]
            lane = lax.broadcasted_iota(jnp.int32, (CHUNK, LANES), 1)
            out = jnp.zeros((CHUNK, LANES), F32)
            for g in range(NG):
                sg = jnp.sum(jnp.where(gm[g], acc, 0.0), axis=-1, keepdims=True)
                out = jnp.where(lane == g, sg, out)
            db_ref[...] = out

    sd = jax.ShapeDtypeStruct
    res = pl.pallas_call(
        body, name="mix_bwd", grid=(nsteps,),
        in_specs=[_row_spec(TMX, D), _row_spec(TMX, A), _row_spec(TMX, GW), _row_spec(TMX, GW),
                  _const_spec((1, GW)), _const_spec((1, GW)), _const_spec((NG, CHUNK, CHUNK)),
                  _const_spec((NG, CHUNK, CHUNK)), _const_spec((CHUNK, GW)), _const_spec((1, A)),
                  _const_spec((1, GW)), _const_spec((D, D))],
        out_specs=[_res_spec(d, TMX, A) for d in DILS] + [_res_spec(d, TMX, LANES) for d in DILS]
                  + [_row_spec(TMX, GW), _row_spec(TMX, GW),
                   _const_spec((1, A)), _const_spec((1, GW)), _const_spec((1, GW)), _const_spec((1, GW)),
                   _const_spec((NG, CHUNK, CHUNK)), _const_spec((CHUNK, LANES))],
        out_shape=[_res_shape(s, d, A, BF16) for d in DILS] + [_res_shape(s, d, LANES, F32) for d in DILS]
                  + [sd((s, GW), F32), sd((s, GW), F32),
                   sd((1, A), F32), sd((1, GW), F32), sd((1, GW), F32), sd((1, GW), F32),
                   sd((NG, CHUNK, CHUNK), F32), sd((CHUNK, LANES), F32)],
        scratch_shapes=[pltpu.VMEM((CHUNK, GW), F32), _col_scratch(TMX, A), _col_scratch(TMX, LANES)],
        compiler_params=_cparams("arbitrary"),
    )(dh1, attn, u, z, lng, lnb, sgu_w, sgu_wt, bias_t, ga, gg, wout)
    return (res[:nd], res[nd:2 * nd]) + tuple(res[2 * nd:])


def _dproj_merge(dqs, dks, dvs, du, dz, pin):
    s = du.shape[0]
    nd = len(DILS)
    nscr = sum(1 for d in DILS if d > 1)

    def body(*refs):
        pin_ref = refs[0]
        parts = [refs[1 + t * nd:1 + (t + 1) * nd] for t in range(3)]
        du_ref, dz_ref, dp_ref = refs[1 + 3 * nd:4 + 3 * nd]
        scr = refs[4 + 3 * nd:]
        sums = []
        for t in range(3):
            total, j = None, 0
            for di, dil in enumerate(DILS):
                if dil == 1:
                    term = parts[t][di][0].astype(F32)
                else:
                    term = _merge_residues(parts[t][di], scr[t * nscr + j], dil)
                    j += 1
                total = term if total is None else total + term
            sums.append(total)
        dp_ref[...] = jnp.concatenate([sums[0] * SCALE, sums[1], sums[2], du_ref[...] + pin_ref[0, 0], dz_ref[...]],
                                      axis=-1).astype(BF16)

    return pl.pallas_call(
        body, name="dproj_merge", grid=(s // TMX,),
        in_specs=[pl.BlockSpec(memory_space=pltpu.SMEM)] + [_res_spec(d, TMX, A) for d in DILS] * 3
                 + [_row_spec(TMX, GW)] * 2,
        out_specs=_row_spec(TMX, INW), out_shape=jax.ShapeDtypeStruct((s, INW), BF16),
        scratch_shapes=[_col_scratch(TMX, A)] * (3 * nscr),
        compiler_params=_cparams("arbitrary"),
    )(pin, *dqs, *dks, *dvs, du, dz)


def _inproj_bwd(dproj, dh1, x, g1, win_t):
    s = x.shape[0]

    def body(dp_ref, dh1_ref, x_ref, g_ref, w_ref, dx_ref, dg_ref):
        i = pl.program_id(0)
        dhn = _dot(dp_ref[...], w_ref[...])
        _, xn, r1 = _rms_fwd(x_ref[...], g_ref[...])
        dres, dg = _rms_bwd(dhn, xn, r1, g_ref[...])
        dx_ref[...] = dh1_ref[...] + dres

        @pl.when(i == 0)
        def _():
            dg_ref[...] = jnp.zeros_like(dg_ref)

        dg_ref[...] += dg

    sd = jax.ShapeDtypeStruct
    return pl.pallas_call(
        body, name="inproj_bwd", grid=(s // TM,),
        in_specs=[_row_spec(TM, INW), _row_spec(TM, D), _row_spec(TM, D), _const_spec((1, D)), _const_spec((INW, D))],
        out_specs=[_row_spec(TM, D), _const_spec((1, D))],
        out_shape=[sd((s, D), F32), sd((1, D), F32)],
        compiler_params=_cparams("arbitrary"),
    )(dproj, dh1, x, g1, win_t)


def _wgrad(a, b, name, bm, bn, bk=2 * TM, square_a=False):
    s, m = a.shape
    n = b.shape[1]
    bm, bn = min(bm, m), min(bn, n)

    def body(a_ref, b_ref, o_ref):
        @pl.when(pl.program_id(2) == 0)
        def _():
            o_ref[...] = jnp.zeros_like(o_ref)

        av = a_ref[...]
        if square_a:
            av = av.astype(F32)
            av = av * av
        o_ref[...] += _dot_tn(av.astype(BF16), b_ref[...].astype(BF16))

    return pl.pallas_call(
        body, name=name, grid=(m // bm, n // bn, s // bk),
        in_specs=[pl.BlockSpec((bk, bm), lambda i, j, k: (k, i)), pl.BlockSpec((bk, bn), lambda i, j, k: (k, j))],
        out_specs=pl.BlockSpec((bm, bn), lambda i, j, k: (i, j)),
        out_shape=jax.ShapeDtypeStruct((m, n), F32),
        compiler_params=_cparams("arbitrary", "arbitrary", "arbitrary"),
    )(a, b)


def _adamw_math(w, g, m, v):
    m = B1 * m + (1.0 - B1) * g
    v = B2 * v + (1.0 - B2) * (g * g)
    m_hat = m / (1.0 - B1 ** STEP)
    v_hat = v / (1.0 - B2 ** STEP)
    delta = -LR * (m_hat / (jnp.sqrt(v_hat) + AEPS) + WD * w)
    return delta, m, v


def _adamw(w, g, m, v, name):
    rows, cols = w.shape
    br = min(rows, 256)
    while rows % br:
        br -= 8

    def body(w_ref, g_ref, m_ref, v_ref, d_ref, mo_ref, vo_ref):
        d, mn, vn = _adamw_math(w_ref[...], g_ref[...], m_ref[...], v_ref[...])
        d_ref[...] = d
        mo_ref[...] = mn
        vo_ref[...] = vn

    spec = _row_spec(br, cols)
    sd = jax.ShapeDtypeStruct((rows, cols), F32)
    return pl.pallas_call(
        body, name=name, grid=(rows // br,), in_specs=[spec] * 4, out_specs=[spec] * 3,
        out_shape=[sd, sd, sd], compiler_params=_cparams("arbitrary"),
    )(w, g, m, v)


def _local_step(x, hn1, target, small, win_t, rest_weights, early_grads=None, after_attention_bwd=None,
                late_grads=None):
    slopes = jnp.asarray(_alibi_slopes(NH))
    q, k, v, u, z = _inproj_fwd(hn1, win_t)
    outs, lses = [], []
    for i, dil in enumerate(DILS):
        o, l = _attn_fwd(q[i], k[i], v[i], slopes, dil)
        outs.append(o)
        lses.append(l)
    wout, wff1, wff2 = rest_weights(functools.reduce(lambda a, b: a + b, [l[0, 0:8, :] for l in lses]))
    attn, lse, mixed, h1 = _mix_fwd(outs, lses, u, z, x, small["ln_g"], small["ln_b"], small["sgu_w"],
                                    small["bias_t"], small["attn_out_g"], small["gmlp_out_g"], wout)
    hn2, rf, dh2, loss, dgf = _mlp_fwd(h1, small["norm2_g"], wff1, wff2, small["final_norm_g"], target)
    df, dh1, dg2 = _mlp_bwd(dh2, rf, h1, small["norm2_g"], wff1, wff2)
    gwff1 = _wgrad(hn2, df, "wgrad_ff1", D, 1024)
    gwff2 = _wgrad(rf, dh2, "wgrad_ff2", 1024, D, square_a=True)
    gwout = _wgrad(mixed, dh1, "wgrad_out", D, D)
    ga, g1 = small["attn_out_g"], small["norm1_g"]
    pin = early_grads(gwff1, gwff2, gwout) if early_grads else None
    if pin is not None:
        ga = ga + pin
    (do, delta, du, dz, dga, dgg, dlng, dlnb, dws, db) = _mix_bwd(
        dh1, attn, u, z, small["ln_g"], small["ln_b"], small["sgu_w"], small["sgu_wt"], small["bias_t"],
        ga, small["gmlp_out_g"], wout)
    dqs, dks, dvs = [], [], []
    for i, dil in enumerate(DILS):
        dqs.append(_attn_bwd_dq(q[i], k[i], v[i], do[i], lse[i], delta[i], slopes, dil))
        dk, dv = _attn_bwd_dkv(q[i], k[i], v[i], do[i], lse[i], delta[i], slopes, dil)
        dks.append(dk)
        dvs.append(dv)
    marker = functools.reduce(lambda a, b: a + b, [t[0, 0:8, 0:LANES] for t in dqs + dks + dvs])
    pin = after_attention_bwd(marker) if after_attention_bwd else None
    dproj = _dproj_merge(dqs, dks, dvs, du, dz, jnp.zeros((1, 1), F32) if pin is None else pin)
    gwin_t = _wgrad(dproj, hn1, "wgrad_in", INW // 2, D)
    pin = late_grads(gwin_t) if late_grads else None
    if pin is not None:
        g1 = g1 + pin
    dx, dg1 = _inproj_bwd(dproj, dh1, x, g1, win_t)
    small_grads = dict(norm1_g=dg1, ln_g=dlng, ln_b=dlnb, sgu_w=dws, sgu_b=db[:, :NG].T,
                       attn_out_g=dga, gmlp_out_g=dgg, norm2_g=dg2, final_norm_g=dgf)
    return loss[0, 0], dx, small_grads, (gwin_t, gwout, gwff1, gwff2)


ANY = pl.BlockSpec(memory_space=pl.ANY)
NDEV = 8


def _position():
    return lax.axis_index("x"), lax.axis_index("y"), lax.axis_index("c")


def _other_chips(x, y):
    return [(1 - x, y), (x, 1 - y), (1 - x, 1 - y)]


def _remote(src, dst, send_sem, recv_sem, device):
    return pltpu.make_async_remote_copy(src_ref=src, dst_ref=dst, send_sem=send_sem, recv_sem=recv_sem,
                                        device_id=device, device_id_type=MESH)


HBM = pl.BlockSpec(memory_space=pltpu.HBM)
SEM = pl.BlockSpec(memory_space=pltpu.SEMAPHORE)
DATAFLOW = pltpu.SideEffectType.DATAFLOW_SIDE_EFFECTING


def _in_hbm(a):
    return pltpu.with_memory_space_constraint(a, pltpu.HBM)


def _gather_start(shards, name):
    n = len(shards)
    lands = [jnp.broadcast_to(sh[None], (NCHIP,) + sh.shape) for sh in shards]

    def body(*refs):
        w_refs, land_refs = refs[:n], refs[n:2 * n]
        send_sems, recv_sems = refs[2 * n:2 * n + 2]
        token = refs[-1]
        x, y, c = _position()
        for w in range(n):
            for k, (px, py) in enumerate(_other_chips(x, y)):
                m = 3 * w + k
                _remote(w_refs[w], land_refs[w].at[2 * x + y], send_sems.at[m], recv_sems.at[m], (px, py, c)).start()
        token[...] = jnp.zeros_like(token)

    res = _split_call(body, name, list(shards) + lands, (3 * n, 3 * n), (TOKEN,))
    return res[0], res[1], res[2:2 + n], res[2 + n:2 + 2 * n], res[-1]


def _gather_wait(send_sems, recv_sems, shards, lands, after, name):
    n = len(shards)

    def body(*refs):
        w_refs, land_refs = refs[:n], refs[n:2 * n]
        send_sems, recv_sems = refs[2 * n:2 * n + 2]
        x, y, c = _position()
        for w in range(n):
            for k, (px, py) in enumerate(_other_chips(x, y)):
                m = 3 * w + k
                cp = _remote(w_refs[w], land_refs[w].at[2 * px + py], send_sems.at[m], recv_sems.at[m], (px, py, c))
                cp.wait_send()
                cp.wait_recv()

    operands = list(shards) + list(lands)
    res = pl.pallas_call(
        body, name=name, out_shape=tuple(pltpu.HBM(a.shape, a.dtype) for a in operands),
        in_specs=(HBM,) * (2 * n) + (SEM, SEM, ANY), out_specs=(HBM,) * (2 * n),
        input_output_aliases={i: i for i in range(2 * n)},
        compiler_params=pltpu.CompilerParams(has_side_effects=DATAFLOW),
    )(*operands, send_sems, recv_sems, after)
    return res[n:]


def _xor_peers(x, y, c):
    peers = []
    for k in range(1, NDEV):
        kx, ky, kc = (k >> 2) & 1, (k >> 1) & 1, k & 1
        peers.append((1 - x if kx else x, 1 - y if ky else y, 1 - c if kc else c))
    return peers


def _piece(part_ref, px, py, pc):
    slab = 2 * px + py
    if len(part_ref.shape) == 3:
        half = part_ref.shape[1] // 2
        return part_ref.at[slab, pl.ds(pc * half, half), :]
    half = part_ref.shape[0] // 2
    return part_ref.at[pl.ds(pc * half, half), pl.ds(pl.multiple_of(slab * D, D), D)]


def _split_call(body, name, operands, n_sems, extra_out=()):
    n = len(operands)
    sems = tuple(pltpu.SemaphoreType.DMA((m,)) for m in n_sems)
    thru = tuple(pltpu.HBM(a.shape, a.dtype) for a in operands)
    return pl.pallas_call(
        body, name=name, out_shape=sems + thru + tuple(extra_out),
        in_specs=(HBM,) * n,
        out_specs=(SEM,) * len(sems) + (HBM,) * n + (pl.BlockSpec(memory_space=pltpu.VMEM),) * len(extra_out),
        input_output_aliases={i: len(sems) + i for i in range(n)},
        compiler_params=pltpu.CompilerParams(has_side_effects=DATAFLOW),
    )(*[_in_hbm(a) for a in operands])


TOKEN = jax.ShapeDtypeStruct((8, LANES), F32)


def _reduce_start(parts, name):
    nw = len(parts)
    lands = [lax.empty((NDEV - 1, p.shape[-2] // 2, D), F32) for p in parts]

    def body(*refs):
        part_refs, land_refs = refs[:nw], refs[nw:2 * nw]
        send_sems, recv_sems = refs[2 * nw:2 * nw + 2]
        token = refs[-1]
        x, y, c = _position()
        for w in range(nw):
            for k, peer in enumerate(_xor_peers(x, y, c)):
                n = w * (NDEV - 1) + k
                _remote(_piece(part_refs[w], *peer), land_refs[w].at[k], send_sems.at[n], recv_sems.at[n],
                        peer).start()
        token[...] = jnp.zeros_like(token)

    n = nw * (NDEV - 1)
    res = _split_call(body, name, list(parts) + lands, (n, n), (TOKEN,))
    return res[0], res[1], res[2:2 + nw], res[2 + nw:2 + 2 * nw], res[-1]


def _reduce_wait(send_sems, recv_sems, parts, lands, after, name):
    nw = len(parts)

    def body(*refs):
        part_refs, land_refs = refs[:nw], refs[nw:2 * nw]
        send_sems, recv_sems = refs[2 * nw:2 * nw + 2]
        x, y, c = _position()
        for w in range(nw):
            for k, peer in enumerate(_xor_peers(x, y, c)):
                n = w * (NDEV - 1) + k
                cp = _remote(_piece(part_refs[w], *peer), land_refs[w].at[k], send_sems.at[n], recv_sems.at[n], peer)
                cp.wait_send()
                cp.wait_recv()

    operands = list(parts) + list(lands)
    res = pl.pallas_call(
        body, name=name, out_shape=tuple(pltpu.HBM(a.shape, a.dtype) for a in operands),
        in_specs=(HBM,) * (2 * nw) + (SEM, SEM, ANY), out_specs=(HBM,) * (2 * nw),
        input_output_aliases={i: i for i in range(2 * nw)},
        compiler_params=pltpu.CompilerParams(has_side_effects=DATAFLOW),
    )(*operands, send_sems, recv_sems, after)
    return res[:nw], res[nw:]


def _sum_pieces(part, land, sel, name):
    half = part.shape[-2] // 2
    br = 128 if half % 128 == 0 else half // 2
    nb = half // br

    def body(sel_ref, own_ref, *refs):
        acc = own_ref[...]
        for r in refs[:NDEV - 1]:
            acc = acc + r[...]
        refs[NDEV - 1][...] = acc

    if part.ndim == 3:
        own_spec = pl.BlockSpec((None, br, D), lambda i, sel_ref: (sel_ref[0], sel_ref[1] * nb + i, 0))
    else:
        own_spec = pl.BlockSpec((br, D), lambda i, sel_ref: (sel_ref[1] * nb + i, sel_ref[0]))
    slot_specs = [pl.BlockSpec((None, br, D), functools.partial(lambda i, sel_ref, k: (k, i, 0), k=k))
                  for k in range(NDEV - 1)]
    return pl.pallas_call(
        body, name=name,
        grid_spec=pltpu.PrefetchScalarGridSpec(
            num_scalar_prefetch=1, grid=(nb,), in_specs=[own_spec] + slot_specs,
            out_specs=pl.BlockSpec((br, D), lambda i, sel_ref: (i, 0))),
        out_shape=jax.ShapeDtypeStruct((half, D), F32),
        compiler_params=_cparams("arbitrary"),
    )(sel, part, *([land] * (NDEV - 1)))


def _share_start(halves, name):
    nw = len(halves)
    lands = [lax.empty(h.shape, F32) for h in halves]

    def body(*refs):
        h_refs, land_refs = refs[:nw], refs[nw:2 * nw]
        send_sems, recv_sems = refs[2 * nw:2 * nw + 2]
        token = refs[-1]
        x, y, c = _position()
        for w in range(nw):
            _remote(h_refs[w], land_refs[w], send_sems.at[w], recv_sems.at[w], (x, y, 1 - c)).start()
        token[...] = jnp.zeros_like(token)

    res = _split_call(body, name, list(halves) + lands, (nw, nw), (TOKEN,))
    return res[0], res[1], res[2:2 + nw], res[2 + nw:2 + 2 * nw], res[-1]


def _share_wait(send_sems, recv_sems, halves, lands, after, name):
    nw = len(halves)

    def body(*refs):
        h_refs, land_refs = refs[:nw], refs[nw:2 * nw]
        send_sems, recv_sems = refs[2 * nw:2 * nw + 2]
        x, y, c = _position()
        for w in range(nw):
            cp = _remote(h_refs[w], land_refs[w], send_sems.at[w], recv_sems.at[w], (x, y, 1 - c))
            cp.wait_send()
            cp.wait_recv()

    operands = list(halves) + list(lands)
    res = pl.pallas_call(
        body, name=name, out_shape=tuple(pltpu.HBM(a.shape, a.dtype) for a in operands),
        in_specs=(HBM,) * (2 * nw) + (SEM, SEM, ANY), out_specs=(HBM,) * (2 * nw),
        input_output_aliases={i: i for i in range(2 * nw)},
        compiler_params=pltpu.CompilerParams(has_side_effects=DATAFLOW),
    )(*operands, send_sems, recv_sems, after)
    return res[:nw], res[nw:]


def _join_halves(own, other, c):
    first = jnp.where(c == 0, own, other)
    second = jnp.where(c == 0, other, own)
    return jnp.concatenate([first, second], axis=0)


SMALL_SIZES = (("norm1_g", D), ("sgu_ln_g", GW), ("sgu_ln_b", GW), ("sgu_w", NG * CHUNK * CHUNK),
               ("sgu_b", NG * CHUNK), ("attn_out_g", A), ("gmlp_out_g", GW), ("norm2_g", D),
               ("final_norm_g", D))
PARAM_ROWS = sum(n for _, n in SMALL_SIZES) // LANES
SMALL_ROWS = PARAM_ROWS + 8


def _pack_small(tree, first_extra=None):
    extra = jnp.zeros((8 * LANES,), F32)
    if first_extra is not None:
        extra = extra.at[0].set(first_extra)
    flat = jnp.concatenate([tree[n].reshape(-1) for n, _ in SMALL_SIZES] + [extra])
    return flat.reshape(SMALL_ROWS, LANES)


def _unpack_small(pack, shapes):
    flat = pack.reshape(-1)
    out, off = {}, 0
    for n, size in SMALL_SIZES:
        out[n] = flat[off:off + size].reshape(shapes[n])
        off += size
    return out


def _small_allreduce_adamw(gpack, wpack, mpack, vpack):
    def body(g_ref, w_ref, m_ref, v_ref, go_ref, d_ref, mo_ref, vo_ref, slots, send_sems, recv_sems):
        x, y, c = _position()
        me = 4 * x + 2 * y + c
        slots[me] = g_ref[...]
        peers = _xor_peers(x, y, c)
        sends = []
        for k, peer in enumerate(peers):
            cp = _remote(g_ref, slots.at[me], send_sems.at[k], recv_sems.at[k], peer)
            cp.start()
            sends.append(cp)
        for k, (px, py, pc) in enumerate(peers):
            _remote(g_ref, slots.at[4 * px + 2 * py + pc], send_sems.at[k], recv_sems.at[k],
                    (px, py, pc)).wait_recv()
        for cp in sends:
            cp.wait_send()
        total = slots[0]
        for k in range(1, NDEV):
            total = total + slots[k]
        go_ref[...] = total
        d, mn, vn = _adamw_math(w_ref[...], total, m_ref[...], v_ref[...])
        d_ref[...] = d
        mo_ref[...] = mn
        vo_ref[...] = vn

    sd = jax.ShapeDtypeStruct((SMALL_ROWS, LANES), F32)
    vm = pl.BlockSpec(memory_space=pltpu.VMEM)
    return pl.pallas_call(
        body, name="small_allreduce_adamw", in_specs=[vm] * 4, out_specs=[vm] * 4, out_shape=[sd] * 4,
        scratch_shapes=[pltpu.VMEM((NDEV, SMALL_ROWS, LANES), F32), pltpu.SemaphoreType.DMA((NDEV - 1,)),
                        pltpu.SemaphoreType.DMA((NDEV - 1,))],
        compiler_params=pltpu.CompilerParams(has_side_effects=True),
    )(gpack, wpack, mpack, vpack)


def kernel(x, norm1_g, w_in, sgu_ln_g, sgu_ln_b, sgu_w, sgu_b, attn_out_g, gmlp_out_g, w_out, norm2_g, w_ff1, w_ff2, final_norm_g, loss_target, m_norm1_g, m_w_in, m_sgu_ln_g, m_sgu_ln_b, m_sgu_w, m_sgu_b, m_attn_out_g, m_gmlp_out_g, m_w_out, m_norm2_g, m_w_ff1, m_w_ff2, m_final_norm_g, v_norm1_g, v_w_in, v_sgu_ln_g, v_sgu_ln_b, v_sgu_w, v_sgu_b, v_attn_out_g, v_gmlp_out_g, v_w_out, v_norm2_g, v_w_ff1, v_w_ff2, v_final_norm_g):
    names = [n for n, _ in SMALL_SIZES]
    w_small = dict(norm1_g=norm1_g, sgu_ln_g=sgu_ln_g, sgu_ln_b=sgu_ln_b, sgu_w=sgu_w, sgu_b=sgu_b,
                   attn_out_g=attn_out_g, gmlp_out_g=gmlp_out_g, norm2_g=norm2_g, final_norm_g=final_norm_g)
    m_small = dict(norm1_g=m_norm1_g, sgu_ln_g=m_sgu_ln_g, sgu_ln_b=m_sgu_ln_b, sgu_w=m_sgu_w, sgu_b=m_sgu_b,
                   attn_out_g=m_attn_out_g, gmlp_out_g=m_gmlp_out_g, norm2_g=m_norm2_g,
                   final_norm_g=m_final_norm_g)
    v_small = dict(norm1_g=v_norm1_g, sgu_ln_g=v_sgu_ln_g, sgu_ln_b=v_sgu_ln_b, sgu_w=v_sgu_w, sgu_b=v_sgu_b,
                   attn_out_g=v_attn_out_g, gmlp_out_g=v_gmlp_out_g, norm2_g=v_norm2_g,
                   final_norm_g=v_final_norm_g)
    shapes = {n: w_small[n].shape for n in names}

    start_in = _gather_start([w_in[0].T.astype(BF16)], "gather_in_start")
    issued = start_in[4][0:1, 0:1]
    start_rest = _gather_start([(w_out[0] + issued).astype(BF16), w_ff1[0].astype(BF16), w_ff2[0].astype(BF16)],
                               "gather_rest_start")
    hn1 = _norm1(x[0], norm1_g + start_rest[4][0:1, 0:1])
    win_t = _gather_wait(*start_in[:4], after=hn1, name="gather_in_wait")[0].reshape(INW, D)

    def rest_weights(after):
        wout, wff1, wff2 = _gather_wait(*start_rest[:4], after=after, name="gather_rest_wait")
        return wout.reshape(D, D), wff1, wff2.reshape(DFF, D)

    small = dict(
        norm1_g=norm1_g, ln_g=sgu_ln_g.reshape(1, GW), ln_b=sgu_ln_b.reshape(1, GW), sgu_w=sgu_w[0],
        sgu_wt=jnp.swapaxes(sgu_w[0], 1, 2), bias_t=jnp.repeat(sgu_b[0].T, DH, axis=1),
        attn_out_g=attn_out_g, gmlp_out_g=gmlp_out_g, norm2_g=norm2_g, final_norm_g=final_norm_g.reshape(1, D))
    xi, yi, ci = _position()
    sel = jnp.stack([2 * xi + yi, ci]).astype(jnp.int32)
    state = {}

    def as_slabs(g):
        return g.reshape(NCHIP, g.shape[0] // NCHIP, D)

    def early_grads(gwff1, gwff2, gwout):
        state["early"] = _reduce_start([gwff1, as_slabs(gwff2), as_slabs(gwout)], "reduce_early_start")
        return state["early"][4][0:1, 0:1]

    def after_attention_bwd(marker):
        send_sems, recv_sems, parts, lands, _ = state["early"]
        parts, lands = _reduce_wait(send_sems, recv_sems, parts, lands, marker, "reduce_early_wait")
        halves = [_sum_pieces(p, l, sel, "sum_" + n) for p, l, n in zip(parts, lands, ("w_ff1", "w_ff2", "w_out"))]
        state["early_share"] = _share_start(halves, "share_early_start")
        return state["early_share"][4][0:1, 0:1]

    def late_grads(gwin_t):
        state["late"] = _reduce_start([as_slabs(gwin_t)], "reduce_late_start")
        return state["late"][4][0:1, 0:1]

    loss_part, dx, sg, _ = _local_step(
        x[0], hn1, loss_target[0], small, win_t, rest_weights, early_grads, after_attention_bwd, late_grads)
    late = state["late"]
    send_sems, recv_sems, halves, lands, _ = state["early_share"]
    own, other = _share_wait(send_sems, recv_sems, halves, lands, dx, "share_early_wait")
    g_big = {n: _join_halves(o, t, ci) for n, o, t in zip(("w_ff1", "w_ff2", "w_out"), own, other)}
    w_big = dict(w_in=(w_in, m_w_in, v_w_in), w_out=(w_out, m_w_out, v_w_out),
                 w_ff1=(w_ff1, m_w_ff1, v_w_ff1), w_ff2=(w_ff2, m_w_ff2, v_w_ff2))
    grads, deltas, new_m, new_v = {}, {}, {}, {}

    def update(n):
        w, m, v = w_big[n]
        d, mn, vn = _adamw(w[0], g_big[n], m[0], v[0], "adamw_" + n)
        grads[n], deltas[n], new_m[n], new_v[n] = g_big[n][None], d[None], mn[None], vn[None]

    for n in ("w_ff1", "w_ff2", "w_out"):
        update(n)
    updated = deltas["w_out"][0, 0:8, 0:LANES] + deltas["w_ff1"][0, 0:8, 0:LANES] + deltas["w_ff2"][0, 0:8, 0:LANES]
    late_parts, late_lands = _reduce_wait(late[0], late[1], late[2], late[3], updated, "reduce_late_wait")
    late_share = _share_start([_sum_pieces(late_parts[0], late_lands[0], sel, "sum_w_in")], "share_late_start")

    g_small = dict(norm1_g=sg["norm1_g"], sgu_ln_g=sg["ln_g"], sgu_ln_b=sg["ln_b"], sgu_w=sg["sgu_w"],
                   sgu_b=sg["sgu_b"], attn_out_g=sg["attn_out_g"], gmlp_out_g=sg["gmlp_out_g"],
                   norm2_g=sg["norm2_g"], final_norm_g=sg["final_norm_g"])
    packs = _small_allreduce_adamw(_pack_small(g_small, loss_part) + late_share[4][0:1, 0:1], _pack_small(w_small),
                                   _pack_small(m_small), _pack_small(v_small))
    loss = packs[0][PARAM_ROWS, 0]
    for tree, pack in zip((grads, deltas, new_m, new_v), packs):
        tree.update(_unpack_small(pack, shapes))
    own, other = _share_wait(late_share[0], late_share[1], late_share[2], late_share[3], packs[0], "share_late_wait")
    g_big["w_in"] = _join_halves(own[0], other[0], ci).T
    update("w_in")

    order = ["norm1_g", "w_in", "sgu_ln_g", "sgu_ln_b", "sgu_w", "sgu_b", "attn_out_g", "gmlp_out_g", "w_out",
             "norm2_g", "w_ff1", "w_ff2", "final_norm_g"]
    return (loss, dx[None], *[grads[n] for n in order], *[deltas[n] for n in order],
            *[new_m[n] for n in order], *[new_v[n] for n in order])
```

```python
import functools
import math

import numpy as np
import jax
import jax.numpy as jnp
from jax import lax
from jax.experimental import pallas as pl
from jax.experimental.pallas import tpu as pltpu

F32 = jnp.float32
BF16 = jnp.bfloat16

D = 1024
NH = 12
DH = 64
A = NH * DH
NG = 4
GW = NG * DH
INW = 3 * A + 2 * GW
DFF = 4 * D
CHUNK = 128
PATTERNS = ((128, 1), (512, 4), (2048, 16))
EPS = 1e-6
SCALE = DH ** -0.5
NEG = -1e30

LR, B1, B2, AEPS, WD, STEP = 0.001, 0.9, 0.999, 1e-08, 0.01, 10

TM = 512
TMX = 512
ATT_ROWS = 8192
FF_CH = 1024
LANES = 128
NCHIP = 4
VMEM_LIMIT = 56 * 1024 * 1024
MESH = pl.DeviceIdType.MESH


def _cparams(*sem, **kw):
    return pltpu.CompilerParams(dimension_semantics=sem if sem else None,
                                vmem_limit_bytes=VMEM_LIMIT, **kw)


def _dot(a, b):
    return jnp.dot(a, b, preferred_element_type=F32)


def _dot_nt(a, b):
    return lax.dot_general(a, b, (((1,), (1,)), ((), ())), preferred_element_type=F32)


def _dot_tn(a, b):
    return lax.dot_general(a, b, (((0,), (0,)), ((), ())), preferred_element_type=F32)


def _dot_hi(a, b):
    return jnp.dot(a, b, preferred_element_type=F32, precision=lax.Precision.HIGHEST)


def _alibi_slopes(n):
    def pow2(m):
        start = 2.0 ** (-8.0 / m)
        return [start ** (i + 1) for i in range(m)]
    if math.log2(n).is_integer():
        s = pow2(n)
    else:
        c = 2 ** int(math.floor(math.log2(n)))
        s = pow2(c) + pow2(2 * c)[0::2][: n - c]
    return np.asarray(s, dtype=np.float32)


def _rms_fwd(v, g):
    r = lax.rsqrt(jnp.mean(v * v, axis=-1, keepdims=True) + EPS)
    vn = v * r
    return vn * g, vn, r


def _rms_bwd(dy, vn, r, g):
    w = dy * g
    dv = r * (w - vn * jnp.mean(w * vn, axis=-1, keepdims=True))
    return dv, jnp.sum(dy * vn, axis=0, keepdims=True)


_K0 = math.sqrt(2.0 / math.pi)
_K1 = 0.044715


def _gelu(v):
    return 0.5 * v * (1.0 + jnp.tanh(_K0 * (v + _K1 * (v * v * v))))


def _gelu_grad(v):
    t = jnp.tanh(_K0 * (v + _K1 * (v * v * v)))
    return 0.5 * (1.0 + t) + 0.5 * v * (1.0 - t * t) * (_K0 * (1.0 + 3.0 * _K1 * v * v))


def _row_spec(rows, cols):
    return pl.BlockSpec((rows, cols), lambda i: (i, 0))


def _const_spec(shape):
    nd = len(shape)
    return pl.BlockSpec(shape, lambda i: (0,) * nd, pipeline_mode=pl.Buffered(1))


DILS = tuple(d for _, d in PATTERNS)


def _fill_cols(scr, value):
    for cb in range(value.shape[1] // LANES):
        scr[cb] = value[:, cb * LANES:(cb + 1) * LANES]


def _split_residues(scr, out_ref, dil):
    nb, rows, _ = scr.shape
    for r in range(dil):
        for cb in range(nb):
            piece = scr.at[cb][pl.ds(r, rows // dil, stride=dil), :]
            out_ref[r, :, cb * LANES:(cb + 1) * LANES] = piece.astype(out_ref.dtype)


def _merge_residues(in_ref, scr, dil):
    nb, rows, _ = scr.shape
    for r in range(dil):
        for cb in range(nb):
            scr.at[cb][pl.ds(r, rows // dil, stride=dil), :] = in_ref[r, :, cb * LANES:(cb + 1) * LANES].astype(F32)
    return jnp.concatenate([scr[cb] for cb in range(nb)], axis=-1)


def _col_scratch(rows, width):
    return pltpu.VMEM((width // LANES, rows, LANES), F32)


def _res_spec(dil, rows, width):
    return pl.BlockSpec((dil, rows // dil, width), lambda i: (0, i, 0))


def _res_shape(s, dil, width, dtype):
    return jax.ShapeDtypeStruct((dil, s // dil, width), dtype)


def _norm1(x, g1):
    s = x.shape[0]

    def body(x_ref, g_ref, hn_ref):
        hn, _, _ = _rms_fwd(x_ref[...], g_ref[...])
        hn_ref[...] = hn.astype(BF16)

    return pl.pallas_call(
        body, name="norm1", grid=(s // TM,), in_specs=[_row_spec(TM, D), _const_spec((1, D))],
        out_specs=_row_spec(TM, D), out_shape=jax.ShapeDtypeStruct((s, D), BF16),
        compiler_params=_cparams("arbitrary"),
    )(x, g1)


def _inproj_fwd(hn1, win_t):
    s = hn1.shape[0]
    nd = len(DILS)

    def body(hn_ref, w_ref, *rest):
        qkv_refs = rest[:3 * nd]
        u_ref, z_ref, scr = rest[3 * nd:]
        hn = hn_ref[...]
        for t in range(3):
            seg = _dot_nt(hn, w_ref[t * A:(t + 1) * A, :])
            seg = seg * SCALE if t == 0 else seg
            _fill_cols(scr, seg)
            for di, dil in enumerate(DILS):
                if dil == 1:
                    qkv_refs[t * nd + di][0] = seg.astype(BF16)
                else:
                    _split_residues(scr, qkv_refs[t * nd + di], dil)
        u_ref[...] = _dot_nt(hn, w_ref[3 * A:3 * A + GW, :])
        z_ref[...] = _dot_nt(hn, w_ref[3 * A + GW:INW, :])

    res = pl.pallas_call(
        body, name="inproj_fwd", grid=(s // TM,),
        in_specs=[_row_spec(TM, D), _const_spec((INW, D))],
        out_specs=[_res_spec(d, TM, A) for _ in range(3) for d in DILS] + [_row_spec(TM, GW), _row_spec(TM, GW)],
        out_shape=[_res_shape(s, d, A, BF16) for _ in range(3) for d in DILS]
                  + [jax.ShapeDtypeStruct((s, GW), F32)] * 2,
        scratch_shapes=[_col_scratch(TM, A)],
        compiler_params=_cparams("arbitrary"),
    )(hn1, win_t)
    q, k, v = (res[t * nd:(t + 1) * nd] for t in range(3))
    return q, k, v, res[-2], res[-1]


def _att_geometry(length, dil):
    merge = max(1, min(dil, ATT_ROWS // length))
    rows = min(length * merge, ATT_ROWS)
    nsub = rows // CHUNK
    return merge, rows, length * merge // rows, nsub, min(length // CHUNK, nsub)


def _merged(t, merge):
    return t.reshape(t.shape[0] // merge, t.shape[1] * merge, t.shape[2])


def _stack_heads(t):
    lane = lax.broadcasted_iota(jnp.int32, t.shape, 1)
    zero = jnp.zeros_like(t)
    return jnp.concatenate([jnp.where(lane < DH, t, zero), jnp.where(lane >= DH, t, zero)], axis=0)


def _head_cols(t, hp):
    lane = lax.broadcasted_iota(jnp.int32, t.shape, 1)
    cols = [jnp.sum(jnp.where(lane == 2 * hp + h, t, 0.0), axis=-1, keepdims=True) for h in range(2)]
    return jnp.concatenate(cols, axis=0)


def _unstack_heads(t2):
    n = t2.shape[0] // 2
    lane = lax.broadcasted_iota(jnp.int32, (n, LANES), 1)
    return jnp.where(lane < DH, t2[:n], t2[n:])


def _query_window_bias(s0, s1, dil, first):
    row = lax.broadcasted_iota(jnp.int32, (2 * CHUNK, 2 * CHUNK), 0)
    col = lax.broadcasted_iota(jnp.int32, (2 * CHUNK, 2 * CHUNK), 1)
    steps = (row & (CHUNK - 1)) + CHUNK - col
    valid = (steps >= 0) & (steps <= CHUNK)
    if first:
        valid = valid & (col >= CHUNK)
    slope = jnp.where(row < CHUNK, s0, s1)
    return jnp.where(valid, -slope * (steps * dil).astype(F32), NEG)


def _key_block_bias(s0, s1, dil, last):
    key = lax.broadcasted_iota(jnp.int32, (CHUNK, 4 * CHUNK), 0)
    col = lax.broadcasted_iota(jnp.int32, (CHUNK, 4 * CHUNK), 1)
    wq = col & (2 * CHUNK - 1)
    steps = wq - key
    valid = (steps >= 0) & (steps <= CHUNK)
    if last:
        valid = valid & (wq < CHUNK)
    slope = jnp.where(col < 2 * CHUNK, s0, s1)
    return jnp.where(valid, -slope * (steps * dil).astype(F32), NEG)


def _head_rows(t, hp):
    row = lax.broadcasted_iota(jnp.int32, (8, LANES), 0)
    lane = lax.broadcasted_iota(jnp.int32, (8, LANES), 1)
    pick = jnp.where((row < 2) & (lane == 2 * hp + row), 1.0, 0.0).astype(BF16)
    hi = t.astype(BF16)
    rest = t - hi.astype(F32)
    mid = rest.astype(BF16)
    low = (rest - mid.astype(F32)).astype(BF16)
    return _dot_nt(pick, hi) + _dot_nt(pick, mid) + _dot_nt(pick, low)


def _att_specs(dil, rows, nsub, nblk):
    main = pl.BlockSpec((None, rows, LANES), lambda r, c, hp: (r, c, hp))
    prev = pl.BlockSpec((None, CHUNK, LANES), lambda r, c, hp: (r, jnp.maximum(c * nsub - 1, 0), hp))
    nxt = pl.BlockSpec((None, CHUNK, LANES), lambda r, c, hp: (r, jnp.minimum((c + 1) * nsub, nblk - 1), hp))
    main_heads = pl.BlockSpec((None, rows, LANES), lambda r, c, hp: (r, c, 0))
    nxt_heads = pl.BlockSpec((None, CHUNK, LANES), lambda r, c, hp: (r, jnp.minimum((c + 1) * nsub, nblk - 1), 0))
    return main, prev, nxt, main_heads, nxt_heads


def _row_start(i):
    return i * CHUNK if isinstance(i, int) else pl.multiple_of(i * CHUNK, CHUNK)


def _first_blocks(block, nsub, seg, nch, ch, first_bias, bias_buf):
    for i in range(nsub):
        if i % seg:
            block(i, bias_buf[...])
        elif nch == 1:
            block(i, first_bias())
        else:
            block(i, jnp.where(ch == 0, first_bias(), bias_buf[...]))


def _last_blocks(block, nsub, seg, nch, ch, last_bias, bias_buf):
    for i in range(nsub):
        if (i + 1) % seg:
            block(i, bias_buf[...])
        elif nch == 1:
            block(i, last_bias())
        else:
            block(i, jnp.where(ch == nch - 1, last_bias(), bias_buf[...]))


def _attn_fwd(q, k, v, slopes, dil):
    length = q.shape[1]
    merge, rows, nch, nsub, seg = _att_geometry(length, dil)
    main, prev, _, main_heads, _ = _att_specs(dil, rows, nsub, length * merge // CHUNK)
    q, k, v = (_merged(t, merge) for t in (q, k, v))

    def body(sl_ref, q_ref, k_ref, v_ref, kh_ref, vh_ref, o_ref, lse_ref, kbuf, vbuf, bias_buf):
        ch = pl.program_id(1)
        hp = pl.program_id(2)
        lane = lax.broadcasted_iota(jnp.int32, (CHUNK, LANES), 1)
        kbuf[0:CHUNK, :] = kh_ref[...]
        kbuf[CHUNK:, :] = k_ref[...]
        vbuf[0:CHUNK, :] = vh_ref[...]
        vbuf[CHUNK:, :] = v_ref[...]
        s0, s1 = sl_ref[2 * hp], sl_ref[2 * hp + 1]

        def block(i, bias):
            row = _row_start(i)
            rs = pl.ds(row, CHUNK)
            q2 = _stack_heads(q_ref[rs, :])
            kw = kbuf[pl.ds(row, 2 * CHUNK), :]
            vw = vbuf[pl.ds(row, 2 * CHUNK), :]
            sc = _dot_nt(q2, kw) + bias
            m = jnp.max(sc, axis=-1, keepdims=True)
            p = jnp.exp(sc - m)
            l = jnp.sum(p, axis=-1, keepdims=True)
            o2 = _dot(p.astype(BF16), vw) * (1.0 / l)
            o_ref[rs, :] = _unstack_heads(o2).astype(BF16)
            lse = m + jnp.log(l)
            seen = jnp.where(hp == 0, 0.0, lse_ref[rs, :])
            lse_ref[rs, :] = jnp.where(lane == 2 * hp, lse[:CHUNK], jnp.where(lane == 2 * hp + 1, lse[CHUNK:], seen))

        bias_buf[...] = _query_window_bias(s0, s1, dil, False)
        _first_blocks(block, nsub, seg, nch, ch, lambda: _query_window_bias(s0, s1, dil, True), bias_buf)

    sd = jax.ShapeDtypeStruct
    o, lse = pl.pallas_call(
        body, name=f"attn_fwd_d{dil}", grid=(dil // merge, nch, NH // 2),
        in_specs=[pl.BlockSpec(memory_space=pltpu.SMEM), main, main, main, prev, prev],
        out_specs=[main, main_heads],
        out_shape=[sd((dil // merge, length * merge, A), BF16), sd((dil // merge, length * merge, LANES), F32)],
        scratch_shapes=[pltpu.VMEM((rows + CHUNK, LANES), BF16), pltpu.VMEM((rows + CHUNK, LANES), BF16),
                        pltpu.VMEM((2 * CHUNK, 2 * CHUNK), F32)],
        compiler_params=_cparams("arbitrary", "arbitrary", "arbitrary"),
    )(slopes, q, k, v, k, v)
    return o.reshape(dil, length, A), lse.reshape(dil, length, LANES)


def _attn_bwd_dq(q, k, v, do, lse, delta, slopes, dil):
    length = q.shape[1]
    merge, rows, nch, nsub, seg = _att_geometry(length, dil)
    main, prev, _, main_heads, _ = _att_specs(dil, rows, nsub, length * merge // CHUNK)
    q, k, v, do, lse, delta = (_merged(t, merge) for t in (q, k, v, do, lse, delta))

    def body(sl_ref, q_ref, k_ref, v_ref, do_ref, lse_ref, dl_ref, kh_ref, vh_ref, dq_ref, kbuf, vbuf, bias_buf):
        ch = pl.program_id(1)
        hp = pl.program_id(2)
        kbuf[0:CHUNK, :] = kh_ref[...]
        kbuf[CHUNK:, :] = k_ref[...]
        vbuf[0:CHUNK, :] = vh_ref[...]
        vbuf[CHUNK:, :] = v_ref[...]
        s0, s1 = sl_ref[2 * hp], sl_ref[2 * hp + 1]

        def block(i, bias):
            row = _row_start(i)
            rs = pl.ds(row, CHUNK)
            q2 = _stack_heads(q_ref[rs, :])
            do2 = _stack_heads(do_ref[rs, :])
            lse2 = _head_cols(lse_ref[rs, :], hp)
            dl2 = _head_cols(dl_ref[rs, :], hp)
            kw = kbuf[pl.ds(row, 2 * CHUNK), :]
            vw = vbuf[pl.ds(row, 2 * CHUNK), :]
            p = jnp.exp(_dot_nt(q2, kw) + bias - lse2)
            ds = p * (_dot_nt(do2, vw) - dl2)
            dq_ref[rs, :] = _unstack_heads(_dot(ds.astype(BF16), kw)).astype(BF16)

        bias_buf[...] = _query_window_bias(s0, s1, dil, False)
        _first_blocks(block, nsub, seg, nch, ch, lambda: _query_window_bias(s0, s1, dil, True), bias_buf)

    dq = pl.pallas_call(
        body, name=f"attn_dq_d{dil}", grid=(dil // merge, nch, NH // 2),
        in_specs=[pl.BlockSpec(memory_space=pltpu.SMEM), main, main, main, main, main_heads, main_heads, prev, prev],
        out_specs=main, out_shape=jax.ShapeDtypeStruct((dil // merge, length * merge, A), BF16),
        scratch_shapes=[pltpu.VMEM((rows + CHUNK, LANES), BF16), pltpu.VMEM((rows + CHUNK, LANES), BF16),
                        pltpu.VMEM((2 * CHUNK, 2 * CHUNK), F32)],
        compiler_params=_cparams("arbitrary", "arbitrary", "arbitrary"),
    )(slopes, q, k, v, do, lse, delta, k, v)
    return dq.reshape(dil, length, A)


def _attn_bwd_dkv(q, k, v, do, lse, delta, slopes, dil):
    length = q.shape[1]
    merge, rows, nch, nsub, seg = _att_geometry(length, dil)
    main, _, nxt, main_heads, nxt_heads = _att_specs(dil, rows, nsub, length * merge // CHUNK)
    q, k, v, do, lse, delta = (_merged(t, merge) for t in (q, k, v, do, lse, delta))

    def body(sl_ref, k_ref, v_ref, q_ref, do_ref, lse_ref, dl_ref, qh_ref, doh_ref, lseh_ref, dlh_ref,
             dk_ref, dv_ref, qbuf, dobuf, lse_rows, dl_rows, bias_buf):
        ch = pl.program_id(1)
        hp = pl.program_id(2)
        for buf, main_ref, halo_ref in ((qbuf, q_ref, qh_ref), (dobuf, do_ref, doh_ref)):
            buf[0:rows, :] = main_ref[...]
            buf[rows:, :] = halo_ref[...]
        for buf, main_ref, halo_ref in ((lse_rows, lse_ref, lseh_ref), (dl_rows, dl_ref, dlh_ref)):
            buf[:, 0:rows] = _head_rows(main_ref[...], hp)
            buf[:, rows:] = _head_rows(halo_ref[...], hp)
        s0, s1 = sl_ref[2 * hp], sl_ref[2 * hp + 1]

        def block(i, bias):
            row = _row_start(i)
            rs = pl.ds(row, CHUNK)
            win = pl.ds(row, 2 * CHUNK)
            kc = k_ref[rs, :]
            vc = v_ref[rs, :]
            q2 = _stack_heads(qbuf[win, :])
            do2 = _stack_heads(dobuf[win, :])
            cols = slice(i * CHUNK, (i + 2) * CHUNK)
            lse2 = jnp.concatenate([lse_rows[0:1, cols], lse_rows[1:2, cols]], axis=1)
            dl2 = jnp.concatenate([dl_rows[0:1, cols], dl_rows[1:2, cols]], axis=1)
            pt = jnp.exp(_dot_nt(kc, q2) + bias - lse2)
            dst = pt * (_dot_nt(vc, do2) - dl2)
            dv_ref[rs, :] = _dot(pt.astype(BF16), do2).astype(BF16)
            dk_ref[rs, :] = _dot(dst.astype(BF16), q2).astype(BF16)

        bias_buf[...] = _key_block_bias(s0, s1, dil, False)
        _last_blocks(block, nsub, seg, nch, ch, lambda: _key_block_bias(s0, s1, dil, True), bias_buf)

    sd = jax.ShapeDtypeStruct((dil // merge, length * merge, A), BF16)
    dk, dv = pl.pallas_call(
        body, name=f"attn_dkv_d{dil}", grid=(dil // merge, nch, NH // 2),
        in_specs=[pl.BlockSpec(memory_space=pltpu.SMEM), main, main, main, main, main_heads, main_heads,
                  nxt, nxt, nxt_heads, nxt_heads],
        out_specs=[main, main], out_shape=[sd, sd],
        scratch_shapes=[pltpu.VMEM((rows + CHUNK, LANES), BF16), pltpu.VMEM((rows + CHUNK, LANES), BF16),
                        pltpu.VMEM((8, rows + CHUNK), F32), pltpu.VMEM((8, rows + CHUNK), F32),
                        pltpu.VMEM((CHUNK, 4 * CHUNK), F32)],
        compiler_params=_cparams("arbitrary", "arbitrary", "arbitrary"),
    )(slopes, k, v, q, do, lse, delta, q, do, lse, delta)
    return dk.reshape(dil, length, A), dv.reshape(dil, length, A)


def _group_masks(width):
    lane = lax.broadcasted_iota(jnp.int32, (1, width), 1)
    return [(lane >= g * DH) & (lane < (g + 1) * DH) for g in range(width // DH)]


def _group_mean_matrix():
    i = lax.broadcasted_iota(jnp.int32, (GW, GW), 0) // DH
    j = lax.broadcasted_iota(jnp.int32, (GW, GW), 1) // DH
    return jnp.where(i == j, 1.0 / DH, 0.0).astype(F32)


def _tri_mask(lower):
    t = lax.broadcasted_iota(jnp.int32, (CHUNK, CHUNK), 0)
    u = lax.broadcasted_iota(jnp.int32, (CHUNK, CHUNK), 1)
    return (u <= t) if lower else (u >= t)


def _sgu_forward(u, z, lng, lnb, w_ref, bias_t, pmat, rows):
    ug = _gelu(u)
    zg = _gelu(z)
    mu = _dot_hi(zg, pmat)
    zc = zg - mu
    var = _dot_hi(zc * zc, pmat)
    rstd = lax.rsqrt(var + EPS)
    zhat = zc * rstd
    zn = (zhat * lng + lnb).astype(BF16)
    gm = _group_masks(GW)
    tri = _tri_mask(True)
    ws = [jnp.where(tri, w_ref[g], 0.0).astype(BF16) for g in range(NG)]
    pieces = []
    for c in range(rows // CHUNK):
        znc = zn[c * CHUNK:(c + 1) * CHUNK, :]
        mix = None
        for g in range(NG):
            part = jnp.where(gm[g], _dot(ws[g], znc), 0.0)
            mix = part if mix is None else mix + part
        pieces.append(mix + bias_t)
    mixed = jnp.concatenate(pieces, axis=0) if len(pieces) > 1 else pieces[0]
    return ug * mixed, ug, zhat, rstd, zn, mixed


def _head_spread():
    h = lax.broadcasted_iota(jnp.int32, (LANES, A), 0)
    lane = lax.broadcasted_iota(jnp.int32, (LANES, A), 1)
    return jnp.where(lane // DH == h, 1.0, 0.0).astype(BF16)


def _bf16_pieces(t, n):
    pieces = []
    for _ in range(n):
        piece = t.astype(BF16)
        pieces.append(piece)
        t = t - piece.astype(F32)
    return pieces


def _mix_fwd(os_, ls_, u, z, x, lng, lnb, sgu_w, bias_t, ga, gg, wout):
    s = x.shape[0]
    nd = len(DILS)
    nscr = sum(1 for d in DILS if d > 1)

    def body(*refs):
        o_refs, l_refs = refs[:nd], refs[nd:2 * nd]
        u_ref, z_ref, x_ref, lng_ref, lnb_ref, w_ref, bt_ref, ga_ref, gg_ref, wo_ref = refs[2 * nd:2 * nd + 10]
        attn_ref = refs[2 * nd + 10]
        lse_refs = refs[2 * nd + 11:3 * nd + 11]
        mixed_ref, h1_ref = refs[3 * nd + 11:3 * nd + 13]
        scr = refs[3 * nd + 13:]
        scr_o, scr_l, scr_lse = scr[:nscr], scr[nscr:2 * nscr], scr[2 * nscr]
        ov, lv, j = [], [], 0
        for di, dil in enumerate(DILS):
            if dil == 1:
                ov.append(o_refs[di][0].astype(F32))
                lv.append(l_refs[di][0])
            else:
                ov.append(_merge_residues(o_refs[di], scr_o[j], dil))
                lv.append(_merge_residues(l_refs[di], scr_l[j], dil))
                j += 1
        mx = functools.reduce(jnp.maximum, lv)
        es = [jnp.exp(l - mx) for l in lv]
        den = functools.reduce(lambda a, b: a + b, es)
        spread = _head_spread()
        attn = None
        for e, o in zip(es, ov):
            wide = functools.reduce(lambda a, b: a + b, [_dot(piece, spread) for piece in _bf16_pieces(e / den, 2)])
            attn = wide * o if attn is None else attn + wide * o
        attn_ref[...] = attn
        lse = mx + jnp.log(den)
        _fill_cols(scr_lse, lse)
        for di, dil in enumerate(DILS):
            if dil == 1:
                lse_refs[di][0] = lse
            else:
                _split_residues(scr_lse, lse_refs[di], dil)
        an, _, _ = _rms_fwd(attn, ga_ref[...])
        gmv, _, _, _, _, _ = _sgu_forward(u_ref[...], z_ref[...], lng_ref[...], lnb_ref[...], w_ref,
                                          bt_ref[...], _group_mean_matrix(), TMX)
        gn, _, _ = _rms_fwd(gmv, gg_ref[...])
        mixed = jnp.concatenate([an, gn], axis=-1).astype(BF16)
        mixed_ref[...] = mixed
        h1_ref[...] = x_ref[...] + _dot(mixed, wo_ref[...])

    sd = jax.ShapeDtypeStruct
    res = pl.pallas_call(
        body, name="mix_fwd", grid=(s // TMX,),
        in_specs=[_res_spec(d, TMX, A) for d in DILS] + [_res_spec(d, TMX, LANES) for d in DILS]
                 + [_row_spec(TMX, GW), _row_spec(TMX, GW),
                    _row_spec(TMX, D), _const_spec((1, GW)), _const_spec((1, GW)), _const_spec((NG, CHUNK, CHUNK)),
                    _const_spec((CHUNK, GW)), _const_spec((1, A)), _const_spec((1, GW)), _const_spec((D, D))],
        out_specs=[_row_spec(TMX, A)] + [_res_spec(d, TMX, LANES) for d in DILS]
                  + [_row_spec(TMX, D), _row_spec(TMX, D)],
        out_shape=[sd((s, A), F32)] + [_res_shape(s, d, LANES, F32) for d in DILS]
                  + [sd((s, D), BF16), sd((s, D), F32)],
        scratch_shapes=[_col_scratch(TMX, A)] * nscr + [_col_scratch(TMX, LANES)] * (nscr + 1),
        compiler_params=_cparams("arbitrary"),
    )(*os_, *ls_, u, z, x, lng, lnb, sgu_w, bias_t, ga, gg, wout)
    return res[0], res[1:1 + nd], res[1 + nd], res[2 + nd]


def _mlp_fwd(h1, g2, wff1, wff2, gf, target):
    s = h1.shape[0]

    def body(h1_ref, g2_ref, w1_ref, w2_ref, gf_ref, t_ref, hn_ref, rf_ref, dh2_ref, loss_ref, dgf_ref):
        i = pl.program_id(0)
        h1v = h1_ref[...]
        hn, _, _ = _rms_fwd(h1v, g2_ref[...])
        hn = hn.astype(BF16)
        hn_ref[...] = hn
        acc = h1v
        for j in range(DFF // FF_CH):
            cols = slice(j * FF_CH, (j + 1) * FF_CH)
            rf = jnp.maximum(_dot(hn, w1_ref[j]), 0.0)
            act = (rf * rf).astype(BF16)
            rf_ref[:, cols] = rf.astype(BF16)
            acc = acc + _dot(act, w2_ref[cols, :])
        y, h2n, r3 = _rms_fwd(acc, gf_ref[...])
        err = y - t_ref[...]
        part = 0.5 * jnp.sum(jnp.mean(err * err, axis=-1, keepdims=True), axis=0, keepdims=True)
        dy = err * (1.0 / D)
        dh2, dgf = _rms_bwd(dy, h2n, r3, gf_ref[...])
        dh2_ref[...] = dh2

        @pl.when(i == 0)
        def _():
            loss_ref[...] = jnp.zeros_like(loss_ref)
            dgf_ref[...] = jnp.zeros_like(dgf_ref)

        loss_ref[...] += jnp.broadcast_to(part, loss_ref.shape)
        dgf_ref[...] += dgf

    sd = jax.ShapeDtypeStruct
    return pl.pallas_call(
        body, name="mlp_fwd", grid=(s // TM,),
        in_specs=[_row_spec(TM, D), _const_spec((1, D)), _const_spec((DFF // FF_CH, D, FF_CH)), _const_spec((DFF, D)),
                  _const_spec((1, D)), _row_spec(TM, D)],
        out_specs=[_row_spec(TM, D), _row_spec(TM, DFF), _row_spec(TM, D),
                   _const_spec((1, LANES)), _const_spec((1, D))],
        out_shape=[sd((s, D), BF16), sd((s, DFF), BF16), sd((s, D), F32),
                   sd((1, LANES), F32), sd((1, D), F32)],
        compiler_params=_cparams("arbitrary"),
    )(h1, g2, wff1, wff2, gf, target)


def _mlp_bwd(dh2, rf, h1, g2, wff1, wff2):
    s = h1.shape[0]

    def body(dh2_ref, rf_ref, h1_ref, g2_ref, w1_ref, w2_ref, df_ref, dh1_ref, dg2_ref):
        i = pl.program_id(0)
        dh2v = dh2_ref[...]
        dh2b = dh2v.astype(BF16)
        dhn = jnp.zeros((TM, D), F32)
        for j in range(DFF // FF_CH):
            cols = slice(j * FF_CH, (j + 1) * FF_CH)
            da = _dot_nt(dh2b, w2_ref[cols, :])
            df = (da * (2.0 * rf_ref[:, cols].astype(F32))).astype(BF16)
            df_ref[:, cols] = df
            dhn = dhn + _dot_nt(df, w1_ref[j])
        _, h1n, r2 = _rms_fwd(h1_ref[...], g2_ref[...])
        dres, dg2 = _rms_bwd(dhn, h1n, r2, g2_ref[...])
        dh1_ref[...] = dh2v + dres

        @pl.when(i == 0)
        def _():
            dg2_ref[...] = jnp.zeros_like(dg2_ref)

        dg2_ref[...] += dg2

    sd = jax.ShapeDtypeStruct
    return pl.pallas_call(
        body, name="mlp_bwd", grid=(s // TM,),
        in_specs=[_row_spec(TM, D), _row_spec(TM, DFF), _row_spec(TM, D), _const_spec((1, D)),
                  _const_spec((DFF // FF_CH, D, FF_CH)), _const_spec((DFF, D))],
        out_specs=[_row_spec(TM, DFF), _row_spec(TM, D), _const_spec((1, D))],
        out_shape=[sd((s, DFF), BF16), sd((s, D), F32), sd((1, D), F32)],
        compiler_params=_cparams("arbitrary"),
    )(dh2, rf, h1, g2, wff1, wff2)


def _mix_bwd(dh1, attn, u, z, lng, lnb, sgu_w, sgu_wt, bias_t, ga, gg, wout):
    s = dh1.shape[0]
    nsteps = s // TMX
    nd = len(DILS)

    def body(*refs):
        dh1_ref, attn_ref, u_ref, z_ref, lng_ref, lnb_ref, w_ref, wt_ref, bt_ref, ga_ref, gg_ref, wo_ref = refs[:12]
        do_refs, dl_refs = refs[12:12 + nd], refs[12 + nd:12 + 2 * nd]
        (du_ref, dz_ref, dga_ref, dgg_ref, dlng_ref, dlnb_ref, dws_ref, db_ref,
         dbt_acc, scr_do, scr_dl) = refs[12 + 2 * nd:]
        i = pl.program_id(0)

        @pl.when(i == 0)
        def _():
            for r in (dga_ref, dgg_ref, dlng_ref, dlnb_ref, dws_ref, db_ref, dbt_acc):
                r[...] = jnp.zeros_like(r)

        dmixed = _dot_nt(dh1_ref[...].astype(BF16), wo_ref[...])
        attn = attn_ref[...]
        _, an, ra = _rms_fwd(attn, ga_ref[...])
        dattn, dga = _rms_bwd(dmixed[:, :A], an, ra, ga_ref[...])
        dga_ref[...] += dga
        _fill_cols(scr_do, dattn)
        spread = _head_spread()
        delta = functools.reduce(lambda a, b: a + b, [_dot_nt(piece, spread) for piece in _bf16_pieces(dattn * attn, 3)])
        _fill_cols(scr_dl, delta)
        for di, dil in enumerate(DILS):
            if dil == 1:
                do_refs[di][0] = dattn.astype(BF16)
                dl_refs[di][0] = delta
            else:
                _split_residues(scr_do, do_refs[di], dil)
                _split_residues(scr_dl, dl_refs[di], dil)
        pmat = _group_mean_matrix()
        lng = lng_ref[...]
        uv, zv = u_ref[...], z_ref[...]
        gmv, ug, zhat, rstd, zn, mixed = _sgu_forward(uv, zv, lng, lnb_ref[...], w_ref, bt_ref[...], pmat, TMX)
        _, gmn, rg = _rms_fwd(gmv, gg_ref[...])
        dgm, dgg = _rms_bwd(dmixed[:, A:], gmn, rg, gg_ref[...])
        dgg_ref[...] += dgg
        du_ref[...] = (dgm * mixed * _gelu_grad(uv)).astype(BF16)
        dmx = dgm * ug
        dmxb = dmx.astype(BF16)
        gm = _group_masks(GW)
        tri_t = _tri_mask(False)
        wst = [jnp.where(tri_t, wt_ref[g], 0.0).astype(BF16) for g in range(NG)]
        zero = jnp.zeros((CHUNK, GW), BF16)
        dzn_pieces = []
        for c in range(TMX // CHUNK):
            rs = slice(c * CHUNK, (c + 1) * CHUNK)
            dmc = dmxb[rs, :]
            znc = zn[rs, :]
            dbt_acc[...] += dmx[rs, :]
            dzn = None
            for g in range(NG):
                dws_ref[g] += _dot_nt(jnp.where(gm[g], dmc, zero), znc)
                part = jnp.where(gm[g], _dot(wst[g], dmc), 0.0)
                dzn = part if dzn is None else dzn + part
            dzn_pieces.append(dzn)
        dzn = jnp.concatenate(dzn_pieces, axis=0)
        dlng_ref[...] += jnp.sum(dzn * zhat, axis=0, keepdims=True)
        dlnb_ref[...] += jnp.sum(dzn, axis=0, keepdims=True)
        dzh = dzn * lng
        dzg = rstd * (dzh - _dot_hi(dzh, pmat) - zhat * _dot_hi(dzh * zhat, pmat))
        dz_ref[...] = (dzg * _gelu_grad(zv)).astype(BF16)

        @pl.when(i == nsteps - 1)
        def _():
            tri = _tri_mask(True)
            for g in range(NG):
                dws_ref[g] = jnp.where(tri, dws_ref[g], 0.0)
            acc = dbt_acc[...]
            lane = lax.broadcasted_iota(jnp.int32, (CHUNK, LANES), 1)
            out = jnp.zeros((CHUNK, LANES), F32)
            for g in range(NG):
                sg = jnp.sum(jnp.where(gm[g], acc, 0.0), axis=-1, keepdims=True)
                out = jnp.where(lane == g, sg, out)
            db_ref[...] = out

    sd = jax.ShapeDtypeStruct
    res = pl.pallas_call(
        body, name="mix_bwd", grid=(nsteps,),
        in_specs=[_row_spec(TMX, D), _row_spec(TMX, A), _row_spec(TMX, GW), _row_spec(TMX, GW),
                  _const_spec((1, GW)), _const_spec((1, GW)), _const_spec((NG, CHUNK, CHUNK)),
                  _const_spec((NG, CHUNK, CHUNK)), _const_spec((CHUNK, GW)), _const_spec((1, A)),
                  _const_spec((1, GW)), _const_spec((D, D))],
        out_specs=[_res_spec(d, TMX, A) for d in DILS] + [_res_spec(d, TMX, LANES) for d in DILS]
                  + [_row_spec(TMX, GW), _row_spec(TMX, GW),
                   _const_spec((1, A)), _const_spec((1, GW)), _const_spec((1, GW)), _const_spec((1, GW)),
                   _const_spec((NG, CHUNK, CHUNK)), _const_spec((CHUNK, LANES))],
        out_shape=[_res_shape(s, d, A, BF16) for d in DILS] + [_res_shape(s, d, LANES, F32) for d in DILS]
                  + [sd((s, GW), BF16), sd((s, GW), BF16),
                   sd((1, A), F32), sd((1, GW), F32), sd((1, GW), F32), sd((1, GW), F32),
                   sd((NG, CHUNK, CHUNK), F32), sd((CHUNK, LANES), F32)],
        scratch_shapes=[pltpu.VMEM((CHUNK, GW), F32), _col_scratch(TMX, A), _col_scratch(TMX, LANES)],
        compiler_params=_cparams("arbitrary"),
    )(dh1, attn, u, z, lng, lnb, sgu_w, sgu_wt, bias_t, ga, gg, wout)
    return (res[:nd], res[nd:2 * nd]) + tuple(res[2 * nd:])


def _dproj_merge(dqs, dks, dvs, du, dz, pin):
    s = du.shape[0]
    nd = len(DILS)
    nscr = sum(1 for d in DILS if d > 1)

    def body(*refs):
        pin_ref = refs[0]
        parts = [refs[1 + t * nd:1 + (t + 1) * nd] for t in range(3)]
        du_ref, dz_ref, dp_ref = refs[1 + 3 * nd:4 + 3 * nd]
        scr = refs[4 + 3 * nd:]
        sums = []
        for t in range(3):
            total, j = None, 0
            for di, dil in enumerate(DILS):
                if dil == 1:
                    term = parts[t][di][0].astype(F32)
                else:
                    term = _merge_residues(parts[t][di], scr[t * nscr + j], dil)
                    j += 1
                total = term if total is None else total + term
            sums.append(total)
        dp_ref[...] = jnp.concatenate([sums[0] * SCALE, sums[1], sums[2], du_ref[...].astype(F32) + pin_ref[0, 0],
                                       dz_ref[...].astype(F32)], axis=-1).astype(BF16)

    return pl.pallas_call(
        body, name="dproj_merge", grid=(s // TMX,),
        in_specs=[pl.BlockSpec(memory_space=pltpu.SMEM)] + [_res_spec(d, TMX, A) for d in DILS] * 3
                 + [_row_spec(TMX, GW)] * 2,
        out_specs=_row_spec(TMX, INW), out_shape=jax.ShapeDtypeStruct((s, INW), BF16),
        scratch_shapes=[_col_scratch(TMX, A)] * (3 * nscr),
        compiler_params=_cparams("arbitrary"),
    )(pin, *dqs, *dks, *dvs, du, dz)


def _inproj_bwd(dproj, dh1, x, g1, win_t):
    s = x.shape[0]

    def body(dp_ref, dh1_ref, x_ref, g_ref, w_ref, dx_ref, dg_ref):
        i = pl.program_id(0)
        dhn = _dot(dp_ref[...], w_ref[...])
        _, xn, r1 = _rms_fwd(x_ref[...], g_ref[...])
        dres, dg = _rms_bwd(dhn, xn, r1, g_ref[...])
        dx_ref[...] = dh1_ref[...] + dres

        @pl.when(i == 0)
        def _():
            dg_ref[...] = jnp.zeros_like(dg_ref)

        dg_ref[...] += dg

    sd = jax.ShapeDtypeStruct
    return pl.pallas_call(
        body, name="inproj_bwd", grid=(s // TM,),
        in_specs=[_row_spec(TM, INW), _row_spec(TM, D), _row_spec(TM, D), _const_spec((1, D)), _const_spec((INW, D))],
        out_specs=[_row_spec(TM, D), _const_spec((1, D))],
        out_shape=[sd((s, D), F32), sd((1, D), F32)],
        compiler_params=_cparams("arbitrary"),
    )(dproj, dh1, x, g1, win_t)


def _wgrad(a, b, name, bm, bn, bk=2 * TM, square_a=False):
    s, m = a.shape
    n = b.shape[1]
    bm, bn = min(bm, m), min(bn, n)

    def body(a_ref, b_ref, o_ref):
        @pl.when(pl.program_id(2) == 0)
        def _():
            o_ref[...] = jnp.zeros_like(o_ref)

        av = a_ref[...]
        if square_a:
            av = av.astype(F32)
            av = av * av
        o_ref[...] += _dot_tn(av.astype(BF16), b_ref[...].astype(BF16))

    return pl.pallas_call(
        body, name=name, grid=(m // bm, n // bn, s // bk),
        in_specs=[pl.BlockSpec((bk, bm), lambda i, j, k: (k, i)), pl.BlockSpec((bk, bn), lambda i, j, k: (k, j))],
        out_specs=pl.BlockSpec((bm, bn), lambda i, j, k: (i, j)),
        out_shape=jax.ShapeDtypeStruct((m, n), F32),
        compiler_params=_cparams("arbitrary", "arbitrary", "arbitrary"),
    )(a, b)


def _adamw_math(w, g, m, v):
    m = B1 * m + (1.0 - B1) * g
    v = B2 * v + (1.0 - B2) * (g * g)
    m_hat = m / (1.0 - B1 ** STEP)
    v_hat = v / (1.0 - B2 ** STEP)
    delta = -LR * (m_hat / (jnp.sqrt(v_hat) + AEPS) + WD * w)
    return delta, m, v


def _adamw(w, g, m, v, name):
    rows, cols = w.shape
    br = min(rows, 256)
    while rows % br:
        br -= 8

    def body(w_ref, g_ref, m_ref, v_ref, d_ref, mo_ref, vo_ref):
        d, mn, vn = _adamw_math(w_ref[...], g_ref[...], m_ref[...], v_ref[...])
        d_ref[...] = d
        mo_ref[...] = mn
        vo_ref[...] = vn

    spec = _row_spec(br, cols)
    sd = jax.ShapeDtypeStruct((rows, cols), F32)
    return pl.pallas_call(
        body, name=name, grid=(rows // br,), in_specs=[spec] * 4, out_specs=[spec] * 3,
        out_shape=[sd, sd, sd], compiler_params=_cparams("arbitrary"),
    )(w, g, m, v)


def _local_step(x, hn1, target, small, win_t, rest_weights, early_grads=None, after_attention_bwd=None,
                late_grads=None):
    slopes = jnp.asarray(_alibi_slopes(NH))
    q, k, v, u, z = _inproj_fwd(hn1, win_t)
    outs, lses = [], []
    for i, dil in enumerate(DILS):
        o, l = _attn_fwd(q[i], k[i], v[i], slopes, dil)
        outs.append(o)
        lses.append(l)
    wout, wff1, wff2 = rest_weights(functools.reduce(lambda a, b: a + b, [l[0, 0:8, :] for l in lses]))
    attn, lse, mixed, h1 = _mix_fwd(outs, lses, u, z, x, small["ln_g"], small["ln_b"], small["sgu_w"],
                                    small["bias_t"], small["attn_out_g"], small["gmlp_out_g"], wout)
    hn2, rf, dh2, loss, dgf = _mlp_fwd(h1, small["norm2_g"], wff1, wff2, small["final_norm_g"], target)
    df, dh1, dg2 = _mlp_bwd(dh2, rf, h1, small["norm2_g"], wff1, wff2)
    gwff1 = _wgrad(hn2, df, "wgrad_ff1", D, 1024)
    gwff2 = _wgrad(rf, dh2, "wgrad_ff2", 1024, D, square_a=True)
    gwout = _wgrad(mixed, dh1, "wgrad_out", D, D)
    ga, g1 = small["attn_out_g"], small["norm1_g"]
    pin = early_grads(gwff1, gwff2, gwout) if early_grads else None
    if pin is not None:
        ga = ga + pin
    (do, delta, du, dz, dga, dgg, dlng, dlnb, dws, db) = _mix_bwd(
        dh1, attn, u, z, small["ln_g"], small["ln_b"], small["sgu_w"], small["sgu_wt"], small["bias_t"],
        ga, small["gmlp_out_g"], wout)
    dqs, dks, dvs = [], [], []
    for i, dil in enumerate(DILS):
        dqs.append(_attn_bwd_dq(q[i], k[i], v[i], do[i], lse[i], delta[i], slopes, dil))
        dk, dv = _attn_bwd_dkv(q[i], k[i], v[i], do[i], lse[i], delta[i], slopes, dil)
        dks.append(dk)
        dvs.append(dv)
    marker = functools.reduce(lambda a, b: a + b, [t[0, 0:8, 0:LANES] for t in dqs + dks + dvs])
    pin = after_attention_bwd(marker) if after_attention_bwd else None
    dproj = _dproj_merge(dqs, dks, dvs, du, dz, jnp.zeros((1, 1), F32) if pin is None else pin)
    gwin_t = _wgrad(dproj, hn1, "wgrad_in", INW // 2, D)
    pin = late_grads(gwin_t) if late_grads else None
    if pin is not None:
        g1 = g1 + pin
    dx, dg1 = _inproj_bwd(dproj, dh1, x, g1, win_t)
    small_grads = dict(norm1_g=dg1, ln_g=dlng, ln_b=dlnb, sgu_w=dws, sgu_b=db[:, :NG].T,
                       attn_out_g=dga, gmlp_out_g=dgg, norm2_g=dg2, final_norm_g=dgf)
    return loss[0, 0], dx, small_grads, (gwin_t, gwout, gwff1, gwff2)


ANY = pl.BlockSpec(memory_space=pl.ANY)
NDEV = 8


def _position():
    return lax.axis_index("x"), lax.axis_index("y"), lax.axis_index("c")


def _other_chips(x, y):
    return [(1 - x, y), (x, 1 - y), (1 - x, 1 - y)]


def _remote(src, dst, send_sem, recv_sem, device):
    return pltpu.make_async_remote_copy(src_ref=src, dst_ref=dst, send_sem=send_sem, recv_sem=recv_sem,
                                        device_id=device, device_id_type=MESH)


HBM = pl.BlockSpec(memory_space=pltpu.HBM)
SEM = pl.BlockSpec(memory_space=pltpu.SEMAPHORE)
DATAFLOW = pltpu.SideEffectType.DATAFLOW_SIDE_EFFECTING


def _in_hbm(a):
    return pltpu.with_memory_space_constraint(a, pltpu.HBM)


def _gather_start(shards, name):
    n = len(shards)
    lands = [jnp.broadcast_to(sh[None], (NCHIP,) + sh.shape) for sh in shards]

    def body(*refs):
        w_refs, land_refs = refs[:n], refs[n:2 * n]
        send_sems, recv_sems = refs[2 * n:2 * n + 2]
        token = refs[-1]
        x, y, c = _position()
        for w in range(n):
            for k, (px, py) in enumerate(_other_chips(x, y)):
                m = 3 * w + k
                _remote(w_refs[w], land_refs[w].at[2 * x + y], send_sems.at[m], recv_sems.at[m], (px, py, c)).start()
        token[...] = jnp.zeros_like(token)

    res = _split_call(body, name, list(shards) + lands, (3 * n, 3 * n), (TOKEN,))
    return res[0], res[1], res[2:2 + n], res[2 + n:2 + 2 * n], res[-1]


def _gather_wait(send_sems, recv_sems, shards, lands, after, name):
    n = len(shards)

    def body(*refs):
        w_refs, land_refs = refs[:n], refs[n:2 * n]
        send_sems, recv_sems = refs[2 * n:2 * n + 2]
        x, y, c = _position()
        for w in range(n):
            for k, (px, py) in enumerate(_other_chips(x, y)):
                m = 3 * w + k
                cp = _remote(w_refs[w], land_refs[w].at[2 * px + py], send_sems.at[m], recv_sems.at[m], (px, py, c))
                cp.wait_send()
                cp.wait_recv()

    operands = list(shards) + list(lands)
    res = pl.pallas_call(
        body, name=name, out_shape=tuple(pltpu.HBM(a.shape, a.dtype) for a in operands),
        in_specs=(HBM,) * (2 * n) + (SEM, SEM, ANY), out_specs=(HBM,) * (2 * n),
        input_output_aliases={i: i for i in range(2 * n)},
        compiler_params=pltpu.CompilerParams(has_side_effects=DATAFLOW),
    )(*operands, send_sems, recv_sems, after)
    return res[n:]


def _xor_peers(x, y, c):
    peers = []
    for k in range(1, NDEV):
        kx, ky, kc = (k >> 2) & 1, (k >> 1) & 1, k & 1
        peers.append((1 - x if kx else x, 1 - y if ky else y, 1 - c if kc else c))
    return peers


def _piece(part_ref, px, py, pc):
    slab = 2 * px + py
    if len(part_ref.shape) == 3:
        half = part_ref.shape[1] // 2
        return part_ref.at[slab, pl.ds(pc * half, half), :]
    half = part_ref.shape[0] // 2
    return part_ref.at[pl.ds(pc * half, half), pl.ds(pl.multiple_of(slab * D, D), D)]


def _split_call(body, name, operands, n_sems, extra_out=()):
    n = len(operands)
    sems = tuple(pltpu.SemaphoreType.DMA((m,)) for m in n_sems)
    thru = tuple(pltpu.HBM(a.shape, a.dtype) for a in operands)
    return pl.pallas_call(
        body, name=name, out_shape=sems + thru + tuple(extra_out),
        in_specs=(HBM,) * n,
        out_specs=(SEM,) * len(sems) + (HBM,) * n + (pl.BlockSpec(memory_space=pltpu.VMEM),) * len(extra_out),
        input_output_aliases={i: len(sems) + i for i in range(n)},
        compiler_params=pltpu.CompilerParams(has_side_effects=DATAFLOW),
    )(*[_in_hbm(a) for a in operands])


TOKEN = jax.ShapeDtypeStruct((8, LANES), F32)


def _reduce_start(parts, name):
    nw = len(parts)
    lands = [lax.empty((NDEV - 1, p.shape[-2] // 2, D), F32) for p in parts]

    def body(*refs):
        part_refs, land_refs = refs[:nw], refs[nw:2 * nw]
        send_sems, recv_sems = refs[2 * nw:2 * nw + 2]
        token = refs[-1]
        x, y, c = _position()
        for w in range(nw):
            for k, peer in enumerate(_xor_peers(x, y, c)):
                n = w * (NDEV - 1) + k
                _remote(_piece(part_refs[w], *peer), land_refs[w].at[k], send_sems.at[n], recv_sems.at[n],
                        peer).start()
        token[...] = jnp.zeros_like(token)

    n = nw * (NDEV - 1)
    res = _split_call(body, name, list(parts) + lands, (n, n), (TOKEN,))
    return res[0], res[1], res[2:2 + nw], res[2 + nw:2 + 2 * nw], res[-1]


def _reduce_wait(send_sems, recv_sems, parts, lands, after, name):
    nw = len(parts)

    def body(*refs):
        part_refs, land_refs = refs[:nw], refs[nw:2 * nw]
        send_sems, recv_sems = refs[2 * nw:2 * nw + 2]
        x, y, c = _position()
        for w in range(nw):
            for k, peer in enumerate(_xor_peers(x, y, c)):
                n = w * (NDEV - 1) + k
                cp = _remote(_piece(part_refs[w], *peer), land_refs[w].at[k], send_sems.at[n], recv_sems.at[n], peer)
                cp.wait_send()
                cp.wait_recv()

    operands = list(parts) + list(lands)
    res = pl.pallas_call(
        body, name=name, out_shape=tuple(pltpu.HBM(a.shape, a.dtype) for a in operands),
        in_specs=(HBM,) * (2 * nw) + (SEM, SEM, ANY), out_specs=(HBM,) * (2 * nw),
        input_output_aliases={i: i for i in range(2 * nw)},
        compiler_params=pltpu.CompilerParams(has_side_effects=DATAFLOW),
    )(*operands, send_sems, recv_sems, after)
    return res[:nw], res[nw:]


def _sum_pieces(part, land, sel, name):
    half = part.shape[-2] // 2
    br = 128 if half % 128 == 0 else half // 2
    nb = half // br

    def body(sel_ref, own_ref, *refs):
        acc = own_ref[...]
        for r in refs[:NDEV - 1]:
            acc = acc + r[...]
        refs[NDEV - 1][...] = acc

    if part.ndim == 3:
        own_spec = pl.BlockSpec((None, br, D), lambda i, sel_ref: (sel_ref[0], sel_ref[1] * nb + i, 0))
    else:
        own_spec = pl.BlockSpec((br, D), lambda i, sel_ref: (sel_ref[1] * nb + i, sel_ref[0]))
    slot_specs = [pl.BlockSpec((None, br, D), functools.partial(lambda i, sel_ref, k: (k, i, 0), k=k))
                  for k in range(NDEV - 1)]
    return pl.pallas_call(
        body, name=name,
        grid_spec=pltpu.PrefetchScalarGridSpec(
            num_scalar_prefetch=1, grid=(nb,), in_specs=[own_spec] + slot_specs,
            out_specs=pl.BlockSpec((br, D), lambda i, sel_ref: (i, 0))),
        out_shape=jax.ShapeDtypeStruct((half, D), F32),
        compiler_params=_cparams("arbitrary"),
    )(sel, part, *([land] * (NDEV - 1)))


def _share_start(halves, name):
    nw = len(halves)
    lands = [lax.empty(h.shape, F32) for h in halves]

    def body(*refs):
        h_refs, land_refs = refs[:nw], refs[nw:2 * nw]
        send_sems, recv_sems = refs[2 * nw:2 * nw + 2]
        token = refs[-1]
        x, y, c = _position()
        for w in range(nw):
            _remote(h_refs[w], land_refs[w], send_sems.at[w], recv_sems.at[w], (x, y, 1 - c)).start()
        token[...] = jnp.zeros_like(token)

    res = _split_call(body, name, list(halves) + lands, (nw, nw), (TOKEN,))
    return res[0], res[1], res[2:2 + nw], res[2 + nw:2 + 2 * nw], res[-1]


def _share_wait(send_sems, recv_sems, halves, lands, after, name):
    nw = len(halves)

    def body(*refs):
        h_refs, land_refs = refs[:nw], refs[nw:2 * nw]
        send_sems, recv_sems = refs[2 * nw:2 * nw + 2]
        x, y, c = _position()
        for w in range(nw):
            cp = _remote(h_refs[w], land_refs[w], send_sems.at[w], recv_sems.at[w], (x, y, 1 - c))
            cp.wait_send()
            cp.wait_recv()

    operands = list(halves) + list(lands)
    res = pl.pallas_call(
        body, name=name, out_shape=tuple(pltpu.HBM(a.shape, a.dtype) for a in operands),
        in_specs=(HBM,) * (2 * nw) + (SEM, SEM, ANY), out_specs=(HBM,) * (2 * nw),
        input_output_aliases={i: i for i in range(2 * nw)},
        compiler_params=pltpu.CompilerParams(has_side_effects=DATAFLOW),
    )(*operands, send_sems, recv_sems, after)
    return res[:nw], res[nw:]


def _join_halves(own, other, c):
    first = jnp.where(c == 0, own, other)
    second = jnp.where(c == 0, other, own)
    return jnp.concatenate([first, second], axis=0)


SMALL_SIZES = (("norm1_g", D), ("sgu_ln_g", GW), ("sgu_ln_b", GW), ("sgu_w", NG * CHUNK * CHUNK),
               ("sgu_b", NG * CHUNK), ("attn_out_g", A), ("gmlp_out_g", GW), ("norm2_g", D),
               ("final_norm_g", D))
PARAM_ROWS = sum(n for _, n in SMALL_SIZES) // LANES
SMALL_ROWS = PARAM_ROWS + 8


def _pack_small(tree, first_extra=None):
    extra = jnp.zeros((8 * LANES,), F32)
    if first_extra is not None:
        extra = extra.at[0].set(first_extra)
    flat = jnp.concatenate([tree[n].reshape(-1) for n, _ in SMALL_SIZES] + [extra])
    return flat.reshape(SMALL_ROWS, LANES)


def _unpack_small(pack, shapes):
    flat = pack.reshape(-1)
    out, off = {}, 0
    for n, size in SMALL_SIZES:
        out[n] = flat[off:off + size].reshape(shapes[n])
        off += size
    return out


def _small_allreduce_adamw(gpack, wpack, mpack, vpack):
    def body(g_ref, w_ref, m_ref, v_ref, go_ref, d_ref, mo_ref, vo_ref, slots, send_sems, recv_sems):
        x, y, c = _position()
        me = 4 * x + 2 * y + c
        slots[me] = g_ref[...]
        peers = _xor_peers(x, y, c)
        sends = []
        for k, peer in enumerate(peers):
            cp = _remote(g_ref, slots.at[me], send_sems.at[k], recv_sems.at[k], peer)
            cp.start()
            sends.append(cp)
        for k, (px, py, pc) in enumerate(peers):
            _remote(g_ref, slots.at[4 * px + 2 * py + pc], send_sems.at[k], recv_sems.at[k],
                    (px, py, pc)).wait_recv()
        for cp in sends:
            cp.wait_send()
        total = slots[0]
        for k in range(1, NDEV):
            total = total + slots[k]
        go_ref[...] = total
        d, mn, vn = _adamw_math(w_ref[...], total, m_ref[...], v_ref[...])
        d_ref[...] = d
        mo_ref[...] = mn
        vo_ref[...] = vn

    sd = jax.ShapeDtypeStruct((SMALL_ROWS, LANES), F32)
    vm = pl.BlockSpec(memory_space=pltpu.VMEM)
    return pl.pallas_call(
        body, name="small_allreduce_adamw", in_specs=[vm] * 4, out_specs=[vm] * 4, out_shape=[sd] * 4,
        scratch_shapes=[pltpu.VMEM((NDEV, SMALL_ROWS, LANES), F32), pltpu.SemaphoreType.DMA((NDEV - 1,)),
                        pltpu.SemaphoreType.DMA((NDEV - 1,))],
        compiler_params=pltpu.CompilerParams(has_side_effects=True),
    )(gpack, wpack, mpack, vpack)


def kernel(x, norm1_g, w_in, sgu_ln_g, sgu_ln_b, sgu_w, sgu_b, attn_out_g, gmlp_out_g, w_out, norm2_g, w_ff1, w_ff2, final_norm_g, loss_target, m_norm1_g, m_w_in, m_sgu_ln_g, m_sgu_ln_b, m_sgu_w, m_sgu_b, m_attn_out_g, m_gmlp_out_g, m_w_out, m_norm2_g, m_w_ff1, m_w_ff2, m_final_norm_g, v_norm1_g, v_w_in, v_sgu_ln_g, v_sgu_ln_b, v_sgu_w, v_sgu_b, v_attn_out_g, v_gmlp_out_g, v_w_out, v_norm2_g, v_w_ff1, v_w_ff2, v_final_norm_g):
    names = [n for n, _ in SMALL_SIZES]
    w_small = dict(norm1_g=norm1_g, sgu_ln_g=sgu_ln_g, sgu_ln_b=sgu_ln_b, sgu_w=sgu_w, sgu_b=sgu_b,
                   attn_out_g=attn_out_g, gmlp_out_g=gmlp_out_g, norm2_g=norm2_g, final_norm_g=final_norm_g)
    m_small = dict(norm1_g=m_norm1_g, sgu_ln_g=m_sgu_ln_g, sgu_ln_b=m_sgu_ln_b, sgu_w=m_sgu_w, sgu_b=m_sgu_b,
                   attn_out_g=m_attn_out_g, gmlp_out_g=m_gmlp_out_g, norm2_g=m_norm2_g,
                   final_norm_g=m_final_norm_g)
    v_small = dict(norm1_g=v_norm1_g, sgu_ln_g=v_sgu_ln_g, sgu_ln_b=v_sgu_ln_b, sgu_w=v_sgu_w, sgu_b=v_sgu_b,
                   attn_out_g=v_attn_out_g, gmlp_out_g=v_gmlp_out_g, norm2_g=v_norm2_g,
                   final_norm_g=v_final_norm_g)
    shapes = {n: w_small[n].shape for n in names}

    start_in = _gather_start([w_in[0].T.astype(BF16)], "gather_in_start")
    issued = start_in[4][0:1, 0:1]
    start_rest = _gather_start([(w_out[0] + issued).astype(BF16), w_ff1[0].astype(BF16), w_ff2[0].astype(BF16)],
                               "gather_rest_start")
    hn1 = _norm1(x[0], norm1_g + start_rest[4][0:1, 0:1])
    win_t = _gather_wait(*start_in[:4], after=hn1, name="gather_in_wait")[0].reshape(INW, D)

    def rest_weights(after):
        wout, wff1, wff2 = _gather_wait(*start_rest[:4], after=after, name="gather_rest_wait")
        return wout.reshape(D, D), wff1, wff2.reshape(DFF, D)

    small = dict(
        norm1_g=norm1_g, ln_g=sgu_ln_g.reshape(1, GW), ln_b=sgu_ln_b.reshape(1, GW), sgu_w=sgu_w[0],
        sgu_wt=jnp.swapaxes(sgu_w[0], 1, 2), bias_t=jnp.repeat(sgu_b[0].T, DH, axis=1),
        attn_out_g=attn_out_g, gmlp_out_g=gmlp_out_g, norm2_g=norm2_g, final_norm_g=final_norm_g.reshape(1, D))
    xi, yi, ci = _position()
    sel = jnp.stack([2 * xi + yi, ci]).astype(jnp.int32)
    state = {}

    def as_slabs(g):
        return g.reshape(NCHIP, g.shape[0] // NCHIP, D)

    def early_grads(gwff1, gwff2, gwout):
        state["early"] = _reduce_start([gwff1, as_slabs(gwff2), as_slabs(gwout)], "reduce_early_start")
        return state["early"][4][0:1, 0:1]

    def after_attention_bwd(marker):
        send_sems, recv_sems, parts, lands, _ = state["early"]
        parts, lands = _reduce_wait(send_sems, recv_sems, parts, lands, marker, "reduce_early_wait")
        halves = [_sum_pieces(p, l, sel, "sum_" + n) for p, l, n in zip(parts, lands, ("w_ff1", "w_ff2", "w_out"))]
        state["early_share"] = _share_start(halves, "share_early_start")
        return state["early_share"][4][0:1, 0:1]

    def late_grads(gwin_t):
        state["late"] = _reduce_start([as_slabs(gwin_t)], "reduce_late_start")
        return state["late"][4][0:1, 0:1]

    loss_part, dx, sg, _ = _local_step(
        x[0], hn1, loss_target[0], small, win_t, rest_weights, early_grads, after_attention_bwd, late_grads)
    late = state["late"]
    send_sems, recv_sems, halves, lands, _ = state["early_share"]
    own, other = _share_wait(send_sems, recv_sems, halves, lands, dx, "share_early_wait")
    g_big = {n: _join_halves(o, t, ci) for n, o, t in zip(("w_ff1", "w_ff2", "w_out"), own, other)}
    w_big = dict(w_in=(w_in, m_w_in, v_w_in), w_out=(w_out, m_w_out, v_w_out),
                 w_ff1=(w_ff1, m_w_ff1, v_w_ff1), w_ff2=(w_ff2, m_w_ff2, v_w_ff2))
    grads, deltas, new_m, new_v = {}, {}, {}, {}

    def update(n):
        w, m, v = w_big[n]
        d, mn, vn = _adamw(w[0], g_big[n], m[0], v[0], "adamw_" + n)
        grads[n], deltas[n], new_m[n], new_v[n] = g_big[n][None], d[None], mn[None], vn[None]

    for n in ("w_ff1", "w_ff2", "w_out"):
        update(n)
    updated = deltas["w_out"][0, 0:8, 0:LANES] + deltas["w_ff1"][0, 0:8, 0:LANES] + deltas["w_ff2"][0, 0:8, 0:LANES]
    late_parts, late_lands = _reduce_wait(late[0], late[1], late[2], late[3], updated, "reduce_late_wait")
    late_share = _share_start([_sum_pieces(late_parts[0], late_lands[0], sel, "sum_w_in")], "share_late_start")

    g_small = dict(norm1_g=sg["norm1_g"], sgu_ln_g=sg["ln_g"], sgu_ln_b=sg["ln_b"], sgu_w=sg["sgu_w"],
                   sgu_b=sg["sgu_b"], attn_out_g=sg["attn_out_g"], gmlp_out_g=sg["gmlp_out_g"],
                   norm2_g=sg["norm2_g"], final_norm_g=sg["final_norm_g"])
    packs = _small_allreduce_adamw(_pack_small(g_small, loss_part) + late_share[4][0:1, 0:1], _pack_small(w_small),
                                   _pack_small(m_small), _pack_small(v_small))
    loss = packs[0][PARAM_ROWS, 0]
    for tree, pack in zip((grads, deltas, new_m, new_v), packs):
        tree.update(_unpack_small(pack, shapes))
    own, other = _share_wait(late_share[0], late_share[1], late_share[2], late_share[3], packs[0], "share_late_wait")
    g_big["w_in"] = _join_halves(own[0], other[0], ci).T
    update("w_in")

    order = ["norm1_g", "w_in", "sgu_ln_g", "sgu_ln_b", "sgu_w", "sgu_b", "attn_out_g", "gmlp_out_g", "w_out",
             "norm2_g", "w_ff1", "w_ff2", "final_norm_g"]
    return (loss, dx[None], *[grads[n] for n in order], *[deltas[n] for n in order],
            *[new_m[n] for n in order], *[new_v[n] for n in order])
```

```python
import functools
import math

import numpy as np
import jax
import jax.numpy as jnp
from jax import lax
from jax.experimental import pallas as pl
from jax.experimental.pallas import tpu as pltpu

F32 = jnp.float32
BF16 = jnp.bfloat16

D = 1024
NH = 12
DH = 64
A = NH * DH
NG = 4
GW = NG * DH
INW = 3 * A + 2 * GW
DFF = 4 * D
CHUNK = 128
PATTERNS = ((128, 1), (512, 4), (2048, 16))
EPS = 1e-6
SCALE = DH ** -0.5
NEG = -1e30

LR, B1, B2, AEPS, WD, STEP = 0.001, 0.9, 0.999, 1e-08, 0.01, 10

TM = 512
TMX = 512
ATT_ROWS = 4096
FF_CH = 1024
LANES = 128
NCHIP = 4
VMEM_LIMIT = 56 * 1024 * 1024
MESH = pl.DeviceIdType.MESH


def _cparams(*sem, **kw):
    return pltpu.CompilerParams(dimension_semantics=sem if sem else None,
                                vmem_limit_bytes=VMEM_LIMIT, **kw)


def _dot(a, b):
    return jnp.dot(a, b, preferred_element_type=F32)


def _dot_nt(a, b):
    return lax.dot_general(a, b, (((1,), (1,)), ((), ())), preferred_element_type=F32)


def _dot_tn(a, b):
    return lax.dot_general(a, b, (((0,), (0,)), ((), ())), preferred_element_type=F32)


def _dot_hi(a, b):
    return jnp.dot(a, b, preferred_element_type=F32, precision=lax.Precision.HIGHEST)


def _alibi_slopes(n):
    def pow2(m):
        start = 2.0 ** (-8.0 / m)
        return [start ** (i + 1) for i in range(m)]
    if math.log2(n).is_integer():
        s = pow2(n)
    else:
        c = 2 ** int(math.floor(math.log2(n)))
        s = pow2(c) + pow2(2 * c)[0::2][: n - c]
    return np.asarray(s, dtype=np.float32)


def _rms_fwd(v, g):
    r = lax.rsqrt(jnp.mean(v * v, axis=-1, keepdims=True) + EPS)
    vn = v * r
    return vn * g, vn, r


def _rms_bwd(dy, vn, r, g):
    w = dy * g
    dv = r * (w - vn * jnp.mean(w * vn, axis=-1, keepdims=True))
    return dv, jnp.sum(dy * vn, axis=0, keepdims=True)


_K0 = math.sqrt(2.0 / math.pi)
_K1 = 0.044715


def _gelu(v):
    return 0.5 * v * (1.0 + jnp.tanh(_K0 * (v + _K1 * (v * v * v))))


def _gelu_grad(v):
    t = jnp.tanh(_K0 * (v + _K1 * (v * v * v)))
    return 0.5 * (1.0 + t) + 0.5 * v * (1.0 - t * t) * (_K0 * (1.0 + 3.0 * _K1 * v * v))


def _row_spec(rows, cols):
    return pl.BlockSpec((rows, cols), lambda i: (i, 0))


def _const_spec(shape):
    nd = len(shape)
    return pl.BlockSpec(shape, lambda i: (0,) * nd, pipeline_mode=pl.Buffered(1))


DILS = tuple(d for _, d in PATTERNS)


def _fill_cols(scr, value):
    for cb in range(value.shape[1] // LANES):
        scr[cb] = value[:, cb * LANES:(cb + 1) * LANES]


def _split_residues(scr, out_ref, dil):
    nb, rows, _ = scr.shape
    for r in range(dil):
        for cb in range(nb):
            piece = scr.at[cb][pl.ds(r, rows // dil, stride=dil), :]
            out_ref[r, :, cb * LANES:(cb + 1) * LANES] = piece.astype(out_ref.dtype)


def _merge_residues(in_ref, scr, dil):
    nb, rows, _ = scr.shape
    for r in range(dil):
        for cb in range(nb):
            scr.at[cb][pl.ds(r, rows // dil, stride=dil), :] = in_ref[r, :, cb * LANES:(cb + 1) * LANES].astype(F32)
    return jnp.concatenate([scr[cb] for cb in range(nb)], axis=-1)


def _col_scratch(rows, width):
    return pltpu.VMEM((width // LANES, rows, LANES), F32)


def _res_spec(dil, rows, width):
    return pl.BlockSpec((dil, rows // dil, width), lambda i: (0, i, 0))


def _res_shape(s, dil, width, dtype):
    return jax.ShapeDtypeStruct((dil, s // dil, width), dtype)


def _norm1(x, g1):
    s = x.shape[0]

    def body(x_ref, g_ref, hn_ref):
        hn, _, _ = _rms_fwd(x_ref[...], g_ref[...])
        hn_ref[...] = hn.astype(BF16)

    return pl.pallas_call(
        body, name="norm1", grid=(s // TM,), in_specs=[_row_spec(TM, D), _const_spec((1, D))],
        out_specs=_row_spec(TM, D), out_shape=jax.ShapeDtypeStruct((s, D), BF16),
        compiler_params=_cparams("arbitrary"),
    )(x, g1)


def _inproj_fwd(hn1, win_t):
    s = hn1.shape[0]
    nd = len(DILS)

    def body(hn_ref, w_ref, *rest):
        qkv_refs = rest[:3 * nd]
        u_ref, z_ref, scr = rest[3 * nd:]
        hn = hn_ref[...]
        for t in range(3):
            seg = _dot_nt(hn, w_ref[t * A:(t + 1) * A, :])
            seg = seg * SCALE if t == 0 else seg
            _fill_cols(scr, seg)
            for di, dil in enumerate(DILS):
                if dil == 1:
                    qkv_refs[t * nd + di][0] = seg.astype(BF16)
                else:
                    _split_residues(scr, qkv_refs[t * nd + di], dil)
        u_ref[...] = _dot_nt(hn, w_ref[3 * A:3 * A + GW, :])
        z_ref[...] = _dot_nt(hn, w_ref[3 * A + GW:INW, :])

    res = pl.pallas_call(
        body, name="inproj_fwd", grid=(s // TM,),
        in_specs=[_row_spec(TM, D), _const_spec((INW, D))],
        out_specs=[_res_spec(d, TM, A) for _ in range(3) for d in DILS] + [_row_spec(TM, GW), _row_spec(TM, GW)],
        out_shape=[_res_shape(s, d, A, BF16) for _ in range(3) for d in DILS]
                  + [jax.ShapeDtypeStruct((s, GW), F32)] * 2,
        scratch_shapes=[_col_scratch(TM, A)],
        compiler_params=_cparams("arbitrary"),
    )(hn1, win_t)
    q, k, v = (res[t * nd:(t + 1) * nd] for t in range(3))
    return q, k, v, res[-2], res[-1]


def _att_geometry(length, dil):
    merge = max(1, min(dil, ATT_ROWS // length))
    rows = min(length * merge, ATT_ROWS)
    nsub = rows // CHUNK
    return merge, rows, length * merge // rows, nsub, min(length // CHUNK, nsub)


def _merged(t, merge):
    return t.reshape(t.shape[0] // merge, t.shape[1] * merge, t.shape[2])


def _stack_heads(t):
    lane = lax.broadcasted_iota(jnp.int32, t.shape, 1)
    zero = jnp.zeros_like(t)
    return jnp.concatenate([jnp.where(lane < DH, t, zero), jnp.where(lane >= DH, t, zero)], axis=0)


def _head_cols(t, hp):
    lane = lax.broadcasted_iota(jnp.int32, t.shape, 1)
    cols = [jnp.sum(jnp.where(lane == 2 * hp + h, t, 0.0), axis=-1, keepdims=True) for h in range(2)]
    return jnp.concatenate(cols, axis=0)


def _unstack_heads(t2):
    n = t2.shape[0] // 2
    lane = lax.broadcasted_iota(jnp.int32, (n, LANES), 1)
    return jnp.where(lane < DH, t2[:n], t2[n:])


def _query_window_bias(s0, s1, dil, first):
    row = lax.broadcasted_iota(jnp.int32, (2 * CHUNK, 2 * CHUNK), 0)
    col = lax.broadcasted_iota(jnp.int32, (2 * CHUNK, 2 * CHUNK), 1)
    steps = (row & (CHUNK - 1)) + CHUNK - col
    valid = (steps >= 0) & (steps <= CHUNK)
    if first:
        valid = valid & (col >= CHUNK)
    slope = jnp.where(row < CHUNK, s0, s1)
    return jnp.where(valid, -slope * (steps * dil).astype(F32), NEG)


def _key_block_bias(s0, s1, dil, last):
    key = lax.broadcasted_iota(jnp.int32, (CHUNK, 4 * CHUNK), 0)
    col = lax.broadcasted_iota(jnp.int32, (CHUNK, 4 * CHUNK), 1)
    wq = col & (2 * CHUNK - 1)
    steps = wq - key
    valid = (steps >= 0) & (steps <= CHUNK)
    if last:
        valid = valid & (wq < CHUNK)
    slope = jnp.where(col < 2 * CHUNK, s0, s1)
    return jnp.where(valid, -slope * (steps * dil).astype(F32), NEG)


def _head_rows(t, hp):
    row = lax.broadcasted_iota(jnp.int32, (8, LANES), 0)
    lane = lax.broadcasted_iota(jnp.int32, (8, LANES), 1)
    pick = jnp.where((row < 2) & (lane == 2 * hp + row), 1.0, 0.0).astype(BF16)
    hi = t.astype(BF16)
    rest = t - hi.astype(F32)
    mid = rest.astype(BF16)
    low = (rest - mid.astype(F32)).astype(BF16)
    return _dot_nt(pick, hi) + _dot_nt(pick, mid) + _dot_nt(pick, low)


def _att_specs(dil, rows, nsub, nblk):
    main = pl.BlockSpec((None, rows, LANES), lambda r, c, hp: (r, c, hp))
    prev = pl.BlockSpec((None, CHUNK, LANES), lambda r, c, hp: (r, jnp.maximum(c * nsub - 1, 0), hp))
    nxt = pl.BlockSpec((None, CHUNK, LANES), lambda r, c, hp: (r, jnp.minimum((c + 1) * nsub, nblk - 1), hp))
    main_heads = pl.BlockSpec((None, rows, LANES), lambda r, c, hp: (r, c, 0))
    nxt_heads = pl.BlockSpec((None, CHUNK, LANES), lambda r, c, hp: (r, jnp.minimum((c + 1) * nsub, nblk - 1), 0))
    return main, prev, nxt, main_heads, nxt_heads


def _row_start(i):
    return i * CHUNK if isinstance(i, int) else pl.multiple_of(i * CHUNK, CHUNK)


def _first_blocks(block, nsub, seg, nch, ch, first_bias, bias_buf):
    for i in range(nsub):
        if i % seg:
            block(i, bias_buf[...])
        elif nch == 1:
            block(i, first_bias())
        else:
            block(i, jnp.where(ch == 0, first_bias(), bias_buf[...]))


def _last_blocks(block, nsub, seg, nch, ch, last_bias, bias_buf):
    for i in range(nsub):
        if (i + 1) % seg:
            block(i, bias_buf[...])
        elif nch == 1:
            block(i, last_bias())
        else:
            block(i, jnp.where(ch == nch - 1, last_bias(), bias_buf[...]))


def _attn_fwd(q, k, v, slopes, dil):
    length = q.shape[1]
    merge, rows, nch, nsub, seg = _att_geometry(length, dil)
    main, prev, _, main_heads, _ = _att_specs(dil, rows, nsub, length * merge // CHUNK)
    q, k, v = (_merged(t, merge) for t in (q, k, v))

    def body(sl_ref, q_ref, k_ref, v_ref, kh_ref, vh_ref, o_ref, lse_ref, kbuf, vbuf, bias_buf):
        ch = pl.program_id(1)
        hp = pl.program_id(2)
        lane = lax.broadcasted_iota(jnp.int32, (CHUNK, LANES), 1)
        kbuf[0:CHUNK, :] = kh_ref[...]
        kbuf[CHUNK:, :] = k_ref[...]
        vbuf[0:CHUNK, :] = vh_ref[...]
        vbuf[CHUNK:, :] = v_ref[...]
        s0, s1 = sl_ref[2 * hp], sl_ref[2 * hp + 1]

        def block(i, bias):
            row = _row_start(i)
            rs = pl.ds(row, CHUNK)
            q2 = _stack_heads(q_ref[rs, :])
            kw = kbuf[pl.ds(row, 2 * CHUNK), :]
            vw = vbuf[pl.ds(row, 2 * CHUNK), :]
            sc = _dot_nt(q2, kw) + bias
            m = jnp.max(sc, axis=-1, keepdims=True)
            p = jnp.exp(sc - m)
            l = jnp.sum(p, axis=-1, keepdims=True)
            o2 = _dot(p.astype(BF16), vw) * (1.0 / l)
            o_ref[rs, :] = _unstack_heads(o2).astype(BF16)
            lse = m + jnp.log(l)
            seen = jnp.where(hp == 0, 0.0, lse_ref[rs, :])
            lse_ref[rs, :] = jnp.where(lane == 2 * hp, lse[:CHUNK], jnp.where(lane == 2 * hp + 1, lse[CHUNK:], seen))

        bias_buf[...] = _query_window_bias(s0, s1, dil, False)
        _first_blocks(block, nsub, seg, nch, ch, lambda: _query_window_bias(s0, s1, dil, True), bias_buf)

    sd = jax.ShapeDtypeStruct
    o, lse = pl.pallas_call(
        body, name=f"attn_fwd_d{dil}", grid=(dil // merge, nch, NH // 2),
        in_specs=[pl.BlockSpec(memory_space=pltpu.SMEM), main, main, main, prev, prev],
        out_specs=[main, main_heads],
        out_shape=[sd((dil // merge, length * merge, A), BF16), sd((dil // merge, length * merge, LANES), F32)],
        scratch_shapes=[pltpu.VMEM((rows + CHUNK, LANES), BF16), pltpu.VMEM((rows + CHUNK, LANES), BF16),
                        pltpu.VMEM((2 * CHUNK, 2 * CHUNK), F32)],
        compiler_params=_cparams("arbitrary", "arbitrary", "arbitrary"),
    )(slopes, q, k, v, k, v)
    return o.reshape(dil, length, A), lse.reshape(dil, length, LANES)


def _attn_bwd_dq(q, k, v, do, lse, delta, slopes, dil):
    length = q.shape[1]
    merge, rows, nch, nsub, seg = _att_geometry(length, dil)
    main, prev, _, main_heads, _ = _att_specs(dil, rows, nsub, length * merge // CHUNK)
    q, k, v, do, lse, delta = (_merged(t, merge) for t in (q, k, v, do, lse, delta))

    def body(sl_ref, q_ref, k_ref, v_ref, do_ref, lse_ref, dl_ref, kh_ref, vh_ref, dq_ref, kbuf, vbuf, bias_buf):
        ch = pl.program_id(1)
        hp = pl.program_id(2)
        kbuf[0:CHUNK, :] = kh_ref[...]
        kbuf[CHUNK:, :] = k_ref[...]
        vbuf[0:CHUNK, :] = vh_ref[...]
        vbuf[CHUNK:, :] = v_ref[...]
        s0, s1 = sl_ref[2 * hp], sl_ref[2 * hp + 1]

        def block(i, bias):
            row = _row_start(i)
            rs = pl.ds(row, CHUNK)
            q2 = _stack_heads(q_ref[rs, :])
            do2 = _stack_heads(do_ref[rs, :])
            lse2 = _head_cols(lse_ref[rs, :], hp)
            dl2 = _head_cols(dl_ref[rs, :], hp)
            kw = kbuf[pl.ds(row, 2 * CHUNK), :]
            vw = vbuf[pl.ds(row, 2 * CHUNK), :]
            p = jnp.exp(_dot_nt(q2, kw) + bias - lse2)
            ds = p * (_dot_nt(do2, vw) - dl2)
            dq_ref[rs, :] = _unstack_heads(_dot(ds.astype(BF16), kw)).astype(BF16)

        bias_buf[...] = _query_window_bias(s0, s1, dil, False)
        _first_blocks(block, nsub, seg, nch, ch, lambda: _query_window_bias(s0, s1, dil, True), bias_buf)

    dq = pl.pallas_call(
        body, name=f"attn_dq_d{dil}", grid=(dil // merge, nch, NH // 2),
        in_specs=[pl.BlockSpec(memory_space=pltpu.SMEM), main, main, main, main, main_heads, main_heads, prev, prev],
        out_specs=main, out_shape=jax.ShapeDtypeStruct((dil // merge, length * merge, A), BF16),
        scratch_shapes=[pltpu.VMEM((rows + CHUNK, LANES), BF16), pltpu.VMEM((rows + CHUNK, LANES), BF16),
                        pltpu.VMEM((2 * CHUNK, 2 * CHUNK), F32)],
        compiler_params=_cparams("arbitrary", "arbitrary", "arbitrary"),
    )(slopes, q, k, v, do, lse, delta, k, v)
    return dq.reshape(dil, length, A)


def _attn_bwd_dkv(q, k, v, do, lse, delta, slopes, dil):
    length = q.shape[1]
    merge, rows, nch, nsub, seg = _att_geometry(length, dil)
    main, _, nxt, main_heads, nxt_heads = _att_specs(dil, rows, nsub, length * merge // CHUNK)
    q, k, v, do, lse, delta = (_merged(t, merge) for t in (q, k, v, do, lse, delta))

    def body(sl_ref, k_ref, v_ref, q_ref, do_ref, lse_ref, dl_ref, qh_ref, doh_ref, lseh_ref, dlh_ref,
             dk_ref, dv_ref, qbuf, dobuf, lse_rows, dl_rows, bias_buf):
        ch = pl.program_id(1)
        hp = pl.program_id(2)
        for buf, main_ref, halo_ref in ((qbuf, q_ref, qh_ref), (dobuf, do_ref, doh_ref)):
            buf[0:rows, :] = main_ref[...]
            buf[rows:, :] = halo_ref[...]
        for buf, main_ref, halo_ref in ((lse_rows, lse_ref, lseh_ref), (dl_rows, dl_ref, dlh_ref)):
            buf[:, 0:rows] = _head_rows(main_ref[...], hp)
            buf[:, rows:] = _head_rows(halo_ref[...], hp)
        s0, s1 = sl_ref[2 * hp], sl_ref[2 * hp + 1]

        def block(i, bias):
            row = _row_start(i)
            rs = pl.ds(row, CHUNK)
            win = pl.ds(row, 2 * CHUNK)
            kc = k_ref[rs, :]
            vc = v_ref[rs, :]
            q2 = _stack_heads(qbuf[win, :])
            do2 = _stack_heads(dobuf[win, :])
            cols = slice(i * CHUNK, (i + 2) * CHUNK)
            lse2 = jnp.concatenate([lse_rows[0:1, cols], lse_rows[1:2, cols]], axis=1)
            dl2 = jnp.concatenate([dl_rows[0:1, cols], dl_rows[1:2, cols]], axis=1)
            pt = jnp.exp(_dot_nt(kc, q2) + bias - lse2)
            dst = pt * (_dot_nt(vc, do2) - dl2)
            dv_ref[rs, :] = _dot(pt.astype(BF16), do2).astype(BF16)
            dk_ref[rs, :] = _dot(dst.astype(BF16), q2).astype(BF16)

        bias_buf[...] = _key_block_bias(s0, s1, dil, False)
        _last_blocks(block, nsub, seg, nch, ch, lambda: _key_block_bias(s0, s1, dil, True), bias_buf)

    sd = jax.ShapeDtypeStruct((dil // merge, length * merge, A), BF16)
    dk, dv = pl.pallas_call(
        body, name=f"attn_dkv_d{dil}", grid=(dil // merge, nch, NH // 2),
        in_specs=[pl.BlockSpec(memory_space=pltpu.SMEM), main, main, main, main, main_heads, main_heads,
                  nxt, nxt, nxt_heads, nxt_heads],
        out_specs=[main, main], out_shape=[sd, sd],
        scratch_shapes=[pltpu.VMEM((rows + CHUNK, LANES), BF16), pltpu.VMEM((rows + CHUNK, LANES), BF16),
                        pltpu.VMEM((8, rows + CHUNK), F32), pltpu.VMEM((8, rows + CHUNK), F32),
                        pltpu.VMEM((CHUNK, 4 * CHUNK), F32)],
        compiler_params=_cparams("arbitrary", "arbitrary", "arbitrary"),
    )(slopes, k, v, q, do, lse, delta, q, do, lse, delta)
    return dk.reshape(dil, length, A), dv.reshape(dil, length, A)


def _group_masks(width):
    lane = lax.broadcasted_iota(jnp.int32, (1, width), 1)
    return [(lane >= g * DH) & (lane < (g + 1) * DH) for g in range(width // DH)]


def _group_mean_matrix():
    i = lax.broadcasted_iota(jnp.int32, (GW, GW), 0) // DH
    j = lax.broadcasted_iota(jnp.int32, (GW, GW), 1) // DH
    return jnp.where(i == j, 1.0 / DH, 0.0).astype(F32)


def _tri_mask(lower):
    t = lax.broadcasted_iota(jnp.int32, (CHUNK, CHUNK), 0)
    u = lax.broadcasted_iota(jnp.int32, (CHUNK, CHUNK), 1)
    return (u <= t) if lower else (u >= t)


def _sgu_forward(u, z, lng, lnb, w_ref, bias_t, pmat, rows):
    ug = _gelu(u)
    zg = _gelu(z)
    mu = _dot_hi(zg, pmat)
    zc = zg - mu
    var = _dot_hi(zc * zc, pmat)
    rstd = lax.rsqrt(var + EPS)
    zhat = zc * rstd
    zn = (zhat * lng + lnb).astype(BF16)
    gm = _group_masks(GW)
    tri = _tri_mask(True)
    ws = [jnp.where(tri, w_ref[g], 0.0).astype(BF16) for g in range(NG)]
    pieces = []
    for c in range(rows // CHUNK):
        znc = zn[c * CHUNK:(c + 1) * CHUNK, :]
        mix = None
        for g in range(NG):
            part = jnp.where(gm[g], _dot(ws[g], znc), 0.0)
            mix = part if mix is None else mix + part
        pieces.append(mix + bias_t)
    mixed = jnp.concatenate(pieces, axis=0) if len(pieces) > 1 else pieces[0]
    return ug * mixed, ug, zhat, rstd, zn, mixed


def _head_spread():
    h = lax.broadcasted_iota(jnp.int32, (LANES, A), 0)
    lane = lax.broadcasted_iota(jnp.int32, (LANES, A), 1)
    return jnp.where(lane // DH == h, 1.0, 0.0).astype(BF16)


def _bf16_pieces(t, n):
    pieces = []
    for _ in range(n):
        piece = t.astype(BF16)
        pieces.append(piece)
        t = t - piece.astype(F32)
    return pieces


def _mix_fwd(os_, ls_, u, z, x, lng, lnb, sgu_w, bias_t, ga, gg, wout):
    s = x.shape[0]
    nd = len(DILS)
    nscr = sum(1 for d in DILS if d > 1)

    def body(*refs):
        o_refs, l_refs = refs[:nd], refs[nd:2 * nd]
        u_ref, z_ref, x_ref, lng_ref, lnb_ref, w_ref, bt_ref, ga_ref, gg_ref, wo_ref = refs[2 * nd:2 * nd + 10]
        attn_ref = refs[2 * nd + 10]
        lse_refs = refs[2 * nd + 11:3 * nd + 11]
        mixed_ref, h1_ref = refs[3 * nd + 11:3 * nd + 13]
        scr = refs[3 * nd + 13:]
        scr_o, scr_l, scr_lse = scr[:nscr], scr[nscr:2 * nscr], scr[2 * nscr]
        ov, lv, j = [], [], 0
        for di, dil in enumerate(DILS):
            if dil == 1:
                ov.append(o_refs[di][0].astype(F32))
                lv.append(l_refs[di][0])
            else:
                ov.append(_merge_residues(o_refs[di], scr_o[j], dil))
                lv.append(_merge_residues(l_refs[di], scr_l[j], dil))
                j += 1
        mx = functools.reduce(jnp.maximum, lv)
        es = [jnp.exp(l - mx) for l in lv]
        den = functools.reduce(lambda a, b: a + b, es)
        spread = _head_spread()
        attn = None
        for e, o in zip(es, ov):
            wide = functools.reduce(lambda a, b: a + b, [_dot(piece, spread) for piece in _bf16_pieces(e / den, 2)])
            attn = wide * o if attn is None else attn + wide * o
        attn_ref[...] = attn
        lse = mx + jnp.log(den)
        _fill_cols(scr_lse, lse)
        for di, dil in enumerate(DILS):
            if dil == 1:
                lse_refs[di][0] = lse
            else:
                _split_residues(scr_lse, lse_refs[di], dil)
        an, _, _ = _rms_fwd(attn, ga_ref[...])
        gmv, _, _, _, _, _ = _sgu_forward(u_ref[...], z_ref[...], lng_ref[...], lnb_ref[...], w_ref,
                                          bt_ref[...], _group_mean_matrix(), TMX)
        gn, _, _ = _rms_fwd(gmv, gg_ref[...])
        mixed = jnp.concatenate([an, gn], axis=-1).astype(BF16)
        mixed_ref[...] = mixed
        h1_ref[...] = x_ref[...] + _dot(mixed, wo_ref[...])

    sd = jax.ShapeDtypeStruct
    res = pl.pallas_call(
        body, name="mix_fwd", grid=(s // TMX,),
        in_specs=[_res_spec(d, TMX, A) for d in DILS] + [_res_spec(d, TMX, LANES) for d in DILS]
                 + [_row_spec(TMX, GW), _row_spec(TMX, GW),
                    _row_spec(TMX, D), _const_spec((1, GW)), _const_spec((1, GW)), _const_spec((NG, CHUNK, CHUNK)),
                    _const_spec((CHUNK, GW)), _const_spec((1, A)), _const_spec((1, GW)), _const_spec((D, D))],
        out_specs=[_row_spec(TMX, A)] + [_res_spec(d, TMX, LANES) for d in DILS]
                  + [_row_spec(TMX, D), _row_spec(TMX, D)],
        out_shape=[sd((s, A), F32)] + [_res_shape(s, d, LANES, F32) for d in DILS]
                  + [sd((s, D), BF16), sd((s, D), F32)],
        scratch_shapes=[_col_scratch(TMX, A)] * nscr + [_col_scratch(TMX, LANES)] * (nscr + 1),
        compiler_params=_cparams("arbitrary"),
    )(*os_, *ls_, u, z, x, lng, lnb, sgu_w, bias_t, ga, gg, wout)
    return res[0], res[1:1 + nd], res[1 + nd], res[2 + nd]


def _mlp_fwd(h1, g2, wff1, wff2, gf, target):
    s = h1.shape[0]

    def body(h1_ref, g2_ref, w1_ref, w2_ref, gf_ref, t_ref, hn_ref, rf_ref, dh2_ref, loss_ref, dgf_ref):
        i = pl.program_id(0)
        h1v = h1_ref[...]
        hn, _, _ = _rms_fwd(h1v, g2_ref[...])
        hn = hn.astype(BF16)
        hn_ref[...] = hn
        acc = h1v
        for j in range(DFF // FF_CH):
            cols = slice(j * FF_CH, (j + 1) * FF_CH)
            rf = jnp.maximum(_dot(hn, w1_ref[j]), 0.0)
            act = (rf * rf).astype(BF16)
            rf_ref[:, cols] = rf.astype(BF16)
            acc = acc + _dot(act, w2_ref[cols, :])
        y, h2n, r3 = _rms_fwd(acc, gf_ref[...])
        err = y - t_ref[...]
        part = 0.5 * jnp.sum(jnp.mean(err * err, axis=-1, keepdims=True), axis=0, keepdims=True)
        dy = err * (1.0 / D)
        dh2, dgf = _rms_bwd(dy, h2n, r3, gf_ref[...])
        dh2_ref[...] = dh2

        @pl.when(i == 0)
        def _():
            loss_ref[...] = jnp.zeros_like(loss_ref)
            dgf_ref[...] = jnp.zeros_like(dgf_ref)

        loss_ref[...] += jnp.broadcast_to(part, loss_ref.shape)
        dgf_ref[...] += dgf

    sd = jax.ShapeDtypeStruct
    return pl.pallas_call(
        body, name="mlp_fwd", grid=(s // TM,),
        in_specs=[_row_spec(TM, D), _const_spec((1, D)), _const_spec((DFF // FF_CH, D, FF_CH)), _const_spec((DFF, D)),
                  _const_spec((1, D)), _row_spec(TM, D)],
        out_specs=[_row_spec(TM, D), _row_spec(TM, DFF), _row_spec(TM, D),
                   _const_spec((1, LANES)), _const_spec((1, D))],
        out_shape=[sd((s, D), BF16), sd((s, DFF), BF16), sd((s, D), F32),
                   sd((1, LANES), F32), sd((1, D), F32)],
        compiler_params=_cparams("arbitrary"),
    )(h1, g2, wff1, wff2, gf, target)


def _mlp_bwd(dh2, rf, h1, g2, wff1, wff2):
    s = h1.shape[0]

    def body(dh2_ref, rf_ref, h1_ref, g2_ref, w1_ref, w2_ref, df_ref, dh1_ref, dg2_ref):
        i = pl.program_id(0)
        dh2v = dh2_ref[...]
        dh2b = dh2v.astype(BF16)
        dhn = jnp.zeros((TM, D), F32)
        for j in range(DFF // FF_CH):
            cols = slice(j * FF_CH, (j + 1) * FF_CH)
            da = _dot_nt(dh2b, w2_ref[cols, :])
            df = (da * (2.0 * rf_ref[:, cols].astype(F32))).astype(BF16)
            df_ref[:, cols] = df
            dhn = dhn + _dot_nt(df, w1_ref[j])
        _, h1n, r2 = _rms_fwd(h1_ref[...], g2_ref[...])
        dres, dg2 = _rms_bwd(dhn, h1n, r2, g2_ref[...])
        dh1_ref[...] = dh2v + dres

        @pl.when(i == 0)
        def _():
            dg2_ref[...] = jnp.zeros_like(dg2_ref)

        dg2_ref[...] += dg2

    sd = jax.ShapeDtypeStruct
    return pl.pallas_call(
        body, name="mlp_bwd", grid=(s // TM,),
        in_specs=[_row_spec(TM, D), _row_spec(TM, DFF), _row_spec(TM, D), _const_spec((1, D)),
                  _const_spec((DFF // FF_CH, D, FF_CH)), _const_spec((DFF, D))],
        out_specs=[_row_spec(TM, DFF), _row_spec(TM, D), _const_spec((1, D))],
        out_shape=[sd((s, DFF), BF16), sd((s, D), F32), sd((1, D), F32)],
        compiler_params=_cparams("arbitrary"),
    )(dh2, rf, h1, g2, wff1, wff2)


def _mix_bwd(dh1, attn, u, z, lng, lnb, sgu_w, sgu_wt, bias_t, ga, gg, wout):
    s = dh1.shape[0]
    nsteps = s // TMX
    nd = len(DILS)

    def body(*refs):
        dh1_ref, attn_ref, u_ref, z_ref, lng_ref, lnb_ref, w_ref, wt_ref, bt_ref, ga_ref, gg_ref, wo_ref = refs[:12]
        do_refs, dl_refs = refs[12:12 + nd], refs[12 + nd:12 + 2 * nd]
        (du_ref, dz_ref, dga_ref, dgg_ref, dlng_ref, dlnb_ref, dws_ref, db_ref,
         dbt_acc, scr_do, scr_dl) = refs[12 + 2 * nd:]
        i = pl.program_id(0)

        @pl.when(i == 0)
        def _():
            for r in (dga_ref, dgg_ref, dlng_ref, dlnb_ref, dws_ref, db_ref, dbt_acc):
                r[...] = jnp.zeros_like(r)

        dmixed = _dot_nt(dh1_ref[...].astype(BF16), wo_ref[...])
        attn = attn_ref[...]
        _, an, ra = _rms_fwd(attn, ga_ref[...])
        dattn, dga = _rms_bwd(dmixed[:, :A], an, ra, ga_ref[...])
        dga_ref[...] += dga
        _fill_cols(scr_do, dattn)
        spread = _head_spread()
        delta = functools.reduce(lambda a, b: a + b, [_dot_nt(piece, spread) for piece in _bf16_pieces(dattn * attn, 3)])
        _fill_cols(scr_dl, delta)
        for di, dil in enumerate(DILS):
            if dil == 1:
                do_refs[di][0] = dattn.astype(BF16)
                dl_refs[di][0] = delta
            else:
                _split_residues(scr_do, do_refs[di], dil)
                _split_residues(scr_dl, dl_refs[di], dil)
        pmat = _group_mean_matrix()
        lng = lng_ref[...]
        uv, zv = u_ref[...], z_ref[...]
        gmv, ug, zhat, rstd, zn, mixed = _sgu_forward(uv, zv, lng, lnb_ref[...], w_ref, bt_ref[...], pmat, TMX)
        _, gmn, rg = _rms_fwd(gmv, gg_ref[...])
        dgm, dgg = _rms_bwd(dmixed[:, A:], gmn, rg, gg_ref[...])
        dgg_ref[...] += dgg
        du_ref[...] = (dgm * mixed * _gelu_grad(uv)).astype(BF16)
        dmx = dgm * ug
        dmxb = dmx.astype(BF16)
        gm = _group_masks(GW)
        tri_t = _tri_mask(False)
        wst = [jnp.where(tri_t, wt_ref[g], 0.0).astype(BF16) for g in range(NG)]
        zero = jnp.zeros((CHUNK, GW), BF16)
        dzn_pieces = []
        for c in range(TMX // CHUNK):
            rs = slice(c * CHUNK, (c + 1) * CHUNK)
            dmc = dmxb[rs, :]
            znc = zn[rs, :]
            dbt_acc[...] += dmx[rs, :]
            dzn = None
            for g in range(NG):
                dws_ref[g] += _dot_nt(jnp.where(gm[g], dmc, zero), znc)
                part = jnp.where(gm[g], _dot(wst[g], dmc), 0.0)
                dzn = part if dzn is None else dzn + part
            dzn_pieces.append(dzn)
        dzn = jnp.concatenate(dzn_pieces, axis=0)
        dlng_ref[...] += jnp.sum(dzn * zhat, axis=0, keepdims=True)
        dlnb_ref[...] += jnp.sum(dzn, axis=0, keepdims=True)
        dzh = dzn * lng
        dzg = rstd * (dzh - _dot_hi(dzh, pmat) - zhat * _dot_hi(dzh * zhat, pmat))
        dz_ref[...] = (dzg * _gelu_grad(zv)).astype(BF16)

        @pl.when(i == nsteps - 1)
        def _():
            tri = _tri_mask(True)
            for g in range(NG):
                dws_ref[g] = jnp.where(tri, dws_ref[g], 0.0)
            acc = dbt_acc[...]
            lane = lax.broadcasted_iota(jnp.int32, (CHUNK, LANES), 1)
            out = jnp.zeros((CHUNK, LANES), F32)
            for g in range(NG):
                sg = jnp.sum(jnp.where(gm[g], acc, 0.0), axis=-1, keepdims=True)
                out = jnp.where(lane == g, sg, out)
            db_ref[...] = out

    sd = jax.ShapeDtypeStruct
    res = pl.pallas_call(
        body, name="mix_bwd", grid=(nsteps,),
        in_specs=[_row_spec(TMX, D), _row_spec(TMX, A), _row_spec(TMX, GW), _row_spec(TMX, GW),
                  _const_spec((1, GW)), _const_spec((1, GW)), _const_spec((NG, CHUNK, CHUNK)),
                  _const_spec((NG, CHUNK, CHUNK)), _const_spec((CHUNK, GW)), _const_spec((1, A)),
                  _const_spec((1, GW)), _const_spec((D, D))],
        out_specs=[_res_spec(d, TMX, A) for d in DILS] + [_res_spec(d, TMX, LANES) for d in DILS]
                  + [_row_spec(TMX, GW), _row_spec(TMX, GW),
                   _const_spec((1, A)), _const_spec((1, GW)), _const_spec((1, GW)), _const_spec((1, GW)),
                   _const_spec((NG, CHUNK, CHUNK)), _const_spec((CHUNK, LANES))],
        out_shape=[_res_shape(s, d, A, BF16) for d in DILS] + [_res_shape(s, d, LANES, F32) for d in DILS]
                  + [sd((s, GW), BF16), sd((s, GW), BF16),
                   sd((1, A), F32), sd((1, GW), F32), sd((1, GW), F32), sd((1, GW), F32),
                   sd((NG, CHUNK, CHUNK), F32), sd((CHUNK, LANES), F32)],
        scratch_shapes=[pltpu.VMEM((CHUNK, GW), F32), _col_scratch(TMX, A), _col_scratch(TMX, LANES)],
        compiler_params=_cparams("arbitrary"),
    )(dh1, attn, u, z, lng, lnb, sgu_w, sgu_wt, bias_t, ga, gg, wout)
    return (res[:nd], res[nd:2 * nd]) + tuple(res[2 * nd:])


def _dproj_merge(dqs, dks, dvs, du, dz, pin):
    s = du.shape[0]
    nd = len(DILS)
    nscr = sum(1 for d in DILS if d > 1)

    def body(*refs):
        pin_ref = refs[0]
        parts = [refs[1 + t * nd:1 + (t + 1) * nd] for t in range(3)]
        du_ref, dz_ref, dp_ref = refs[1 + 3 * nd:4 + 3 * nd]
        scr = refs[4 + 3 * nd:]
        sums = []
        for t in range(3):
            total, j = None, 0
            for di, dil in enumerate(DILS):
                if dil == 1:
                    term = parts[t][di][0].astype(F32)
                else:
                    term = _merge_residues(parts[t][di], scr[t * nscr + j], dil)
                    j += 1
                total = term if total is None else total + term
            sums.append(total)
        dp_ref[...] = jnp.concatenate([sums[0] * SCALE, sums[1], sums[2], du_ref[...].astype(F32) + pin_ref[0, 0],
                                       dz_ref[...].astype(F32)], axis=-1).astype(BF16)

    return pl.pallas_call(
        body, name="dproj_merge", grid=(s // TMX,),
        in_specs=[pl.BlockSpec(memory_space=pltpu.SMEM)] + [_res_spec(d, TMX, A) for d in DILS] * 3
                 + [_row_spec(TMX, GW)] * 2,
        out_specs=_row_spec(TMX, INW), out_shape=jax.ShapeDtypeStruct((s, INW), BF16),
        scratch_shapes=[_col_scratch(TMX, A)] * (3 * nscr),
        compiler_params=_cparams("arbitrary"),
    )(pin, *dqs, *dks, *dvs, du, dz)


def _inproj_bwd(dproj, dh1, x, g1, win_t):
    s = x.shape[0]

    def body(dp_ref, dh1_ref, x_ref, g_ref, w_ref, dx_ref, dg_ref):
        i = pl.program_id(0)
        dhn = _dot(dp_ref[...], w_ref[...])
        _, xn, r1 = _rms_fwd(x_ref[...], g_ref[...])
        dres, dg = _rms_bwd(dhn, xn, r1, g_ref[...])
        dx_ref[...] = dh1_ref[...] + dres

        @pl.when(i == 0)
        def _():
            dg_ref[...] = jnp.zeros_like(dg_ref)

        dg_ref[...] += dg

    sd = jax.ShapeDtypeStruct
    return pl.pallas_call(
        body, name="inproj_bwd", grid=(s // TM,),
        in_specs=[_row_spec(TM, INW), _row_spec(TM, D), _row_spec(TM, D), _const_spec((1, D)), _const_spec((INW, D))],
        out_specs=[_row_spec(TM, D), _const_spec((1, D))],
        out_shape=[sd((s, D), F32), sd((1, D), F32)],
        compiler_params=_cparams("arbitrary"),
    )(dproj, dh1, x, g1, win_t)


def _wgrad(a, b, name, bm, bn, bk=2 * TM, square_a=False):
    s, m = a.shape
    n = b.shape[1]
    bm, bn = min(bm, m), min(bn, n)

    def body(a_ref, b_ref, o_ref):
        @pl.when(pl.program_id(2) == 0)
        def _():
            o_ref[...] = jnp.zeros_like(o_ref)

        av = a_ref[...]
        if square_a:
            av = av.astype(F32)
            av = av * av
        o_ref[...] += _dot_tn(av.astype(BF16), b_ref[...].astype(BF16))

    return pl.pallas_call(
        body, name=name, grid=(m // bm, n // bn, s // bk),
        in_specs=[pl.BlockSpec((bk, bm), lambda i, j, k: (k, i)), pl.BlockSpec((bk, bn), lambda i, j, k: (k, j))],
        out_specs=pl.BlockSpec((bm, bn), lambda i, j, k: (i, j)),
        out_shape=jax.ShapeDtypeStruct((m, n), F32),
        compiler_params=_cparams("arbitrary", "arbitrary", "arbitrary"),
    )(a, b)


def _adamw_math(w, g, m, v):
    m = B1 * m + (1.0 - B1) * g
    v = B2 * v + (1.0 - B2) * (g * g)
    m_hat = m / (1.0 - B1 ** STEP)
    v_hat = v / (1.0 - B2 ** STEP)
    delta = -LR * (m_hat / (jnp.sqrt(v_hat) + AEPS) + WD * w)
    return delta, m, v


def _adamw(w, g, m, v, name):
    rows, cols = w.shape
    br = min(rows, 256)
    while rows % br:
        br -= 8

    def body(w_ref, g_ref, m_ref, v_ref, d_ref, mo_ref, vo_ref):
        d, mn, vn = _adamw_math(w_ref[...], g_ref[...], m_ref[...], v_ref[...])
        d_ref[...] = d
        mo_ref[...] = mn
        vo_ref[...] = vn

    spec = _row_spec(br, cols)
    sd = jax.ShapeDtypeStruct((rows, cols), F32)
    return pl.pallas_call(
        body, name=name, grid=(rows // br,), in_specs=[spec] * 4, out_specs=[spec] * 3,
        out_shape=[sd, sd, sd], compiler_params=_cparams("arbitrary"),
    )(w, g, m, v)


def _local_step(x, hn1, target, small, win_t, rest_weights, early_grads=None, after_attention_bwd=None,
                late_grads=None):
    slopes = jnp.asarray(_alibi_slopes(NH))
    q, k, v, u, z = _inproj_fwd(hn1, win_t)
    outs, lses = [], []
    for i, dil in enumerate(DILS):
        o, l = _attn_fwd(q[i], k[i], v[i], slopes, dil)
        outs.append(o)
        lses.append(l)
    wout, wff1, wff2 = rest_weights(functools.reduce(lambda a, b: a + b, [l[0, 0:8, :] for l in lses]))
    attn, lse, mixed, h1 = _mix_fwd(outs, lses, u, z, x, small["ln_g"], small["ln_b"], small["sgu_w"],
                                    small["bias_t"], small["attn_out_g"], small["gmlp_out_g"], wout)
    hn2, rf, dh2, loss, dgf = _mlp_fwd(h1, small["norm2_g"], wff1, wff2, small["final_norm_g"], target)
    df, dh1, dg2 = _mlp_bwd(dh2, rf, h1, small["norm2_g"], wff1, wff2)
    gwff1 = _wgrad(hn2, df, "wgrad_ff1", D, 1024)
    gwff2 = _wgrad(rf, dh2, "wgrad_ff2", 1024, D, square_a=True)
    gwout = _wgrad(mixed, dh1, "wgrad_out", D, D)
    ga, g1 = small["attn_out_g"], small["norm1_g"]
    pin = early_grads(gwff1, gwff2, gwout) if early_grads else None
    if pin is not None:
        ga = ga + pin
    (do, delta, du, dz, dga, dgg, dlng, dlnb, dws, db) = _mix_bwd(
        dh1, attn, u, z, small["ln_g"], small["ln_b"], small["sgu_w"], small["sgu_wt"], small["bias_t"],
        ga, small["gmlp_out_g"], wout)
    dqs, dks, dvs = [], [], []
    for i, dil in enumerate(DILS):
        dqs.append(_attn_bwd_dq(q[i], k[i], v[i], do[i], lse[i], delta[i], slopes, dil))
        dk, dv = _attn_bwd_dkv(q[i], k[i], v[i], do[i], lse[i], delta[i], slopes, dil)
        dks.append(dk)
        dvs.append(dv)
    marker = functools.reduce(lambda a, b: a + b, [t[0, 0:8, 0:LANES] for t in dqs + dks + dvs])
    pin = after_attention_bwd(marker) if after_attention_bwd else None
    dproj = _dproj_merge(dqs, dks, dvs, du, dz, jnp.zeros((1, 1), F32) if pin is None else pin)
    gwin_t = _wgrad(dproj, hn1, "wgrad_in", INW // 2, D)
    pin = late_grads(gwin_t) if late_grads else None
    if pin is not None:
        g1 = g1 + pin
    dx, dg1 = _inproj_bwd(dproj, dh1, x, g1, win_t)
    small_grads = dict(norm1_g=dg1, ln_g=dlng, ln_b=dlnb, sgu_w=dws, sgu_b=db[:, :NG].T,
                       attn_out_g=dga, gmlp_out_g=dgg, norm2_g=dg2, final_norm_g=dgf)
    return loss[0, 0], dx, small_grads, (gwin_t, gwout, gwff1, gwff2)


ANY = pl.BlockSpec(memory_space=pl.ANY)
NDEV = 8


def _position():
    return lax.axis_index("x"), lax.axis_index("y"), lax.axis_index("c")


def _other_chips(x, y):
    return [(1 - x, y), (x, 1 - y), (1 - x, 1 - y)]


def _remote(src, dst, send_sem, recv_sem, device):
    return pltpu.make_async_remote_copy(src_ref=src, dst_ref=dst, send_sem=send_sem, recv_sem=recv_sem,
                                        device_id=device, device_id_type=MESH)


HBM = pl.BlockSpec(memory_space=pltpu.HBM)
SEM = pl.BlockSpec(memory_space=pltpu.SEMAPHORE)
DATAFLOW = pltpu.SideEffectType.DATAFLOW_SIDE_EFFECTING


def _in_hbm(a):
    return pltpu.with_memory_space_constraint(a, pltpu.HBM)


def _gather_start(shards, name):
    n = len(shards)
    lands = [jnp.broadcast_to(sh[None], (NCHIP,) + sh.shape) for sh in shards]

    def body(*refs):
        w_refs, land_refs = refs[:n], refs[n:2 * n]
        send_sems, recv_sems = refs[2 * n:2 * n + 2]
        token = refs[-1]
        x, y, c = _position()
        for w in range(n):
            for k, (px, py) in enumerate(_other_chips(x, y)):
                m = 3 * w + k
                _remote(w_refs[w], land_refs[w].at[2 * x + y], send_sems.at[m], recv_sems.at[m], (px, py, c)).start()
        token[...] = jnp.zeros_like(token)

    res = _split_call(body, name, list(shards) + lands, (3 * n, 3 * n), (TOKEN,))
    return res[0], res[1], res[2:2 + n], res[2 + n:2 + 2 * n], res[-1]


def _gather_wait(send_sems, recv_sems, shards, lands, after, name):
    n = len(shards)

    def body(*refs):
        w_refs, land_refs = refs[:n], refs[n:2 * n]
        send_sems, recv_sems = refs[2 * n:2 * n + 2]
        x, y, c = _position()
        for w in range(n):
            for k, (px, py) in enumerate(_other_chips(x, y)):
                m = 3 * w + k
                cp = _remote(w_refs[w], land_refs[w].at[2 * px + py], send_sems.at[m], recv_sems.at[m], (px, py, c))
                cp.wait_send()
                cp.wait_recv()

    operands = list(shards) + list(lands)
    res = pl.pallas_call(
        body, name=name, out_shape=tuple(pltpu.HBM(a.shape, a.dtype) for a in operands),
        in_specs=(HBM,) * (2 * n) + (SEM, SEM, ANY), out_specs=(HBM,) * (2 * n),
        input_output_aliases={i: i for i in range(2 * n)},
        compiler_params=pltpu.CompilerParams(has_side_effects=DATAFLOW),
    )(*operands, send_sems, recv_sems, after)
    return res[n:]


def _xor_peers(x, y, c):
    peers = []
    for k in range(1, NDEV):
        kx, ky, kc = (k >> 2) & 1, (k >> 1) & 1, k & 1
        peers.append((1 - x if kx else x, 1 - y if ky else y, 1 - c if kc else c))
    return peers


def _piece(part_ref, px, py, pc):
    slab = 2 * px + py
    if len(part_ref.shape) == 3:
        half = part_ref.shape[1] // 2
        return part_ref.at[slab, pl.ds(pc * half, half), :]
    half = part_ref.shape[0] // 2
    return part_ref.at[pl.ds(pc * half, half), pl.ds(pl.multiple_of(slab * D, D), D)]


def _split_call(body, name, operands, n_sems, extra_out=()):
    n = len(operands)
    sems = tuple(pltpu.SemaphoreType.DMA((m,)) for m in n_sems)
    thru = tuple(pltpu.HBM(a.shape, a.dtype) for a in operands)
    return pl.pallas_call(
        body, name=name, out_shape=sems + thru + tuple(extra_out),
        in_specs=(HBM,) * n,
        out_specs=(SEM,) * len(sems) + (HBM,) * n + (pl.BlockSpec(memory_space=pltpu.VMEM),) * len(extra_out),
        input_output_aliases={i: len(sems) + i for i in range(n)},
        compiler_params=pltpu.CompilerParams(has_side_effects=DATAFLOW),
    )(*[_in_hbm(a) for a in operands])


TOKEN = jax.ShapeDtypeStruct((8, LANES), F32)


def _reduce_start(parts, name):
    nw = len(parts)
    lands = [lax.empty((NDEV - 1, p.shape[-2] // 2, D), F32) for p in parts]

    def body(*refs):
        part_refs, land_refs = refs[:nw], refs[nw:2 * nw]
        send_sems, recv_sems = refs[2 * nw:2 * nw + 2]
        token = refs[-1]
        x, y, c = _position()
        for w in range(nw):
            for k, peer in enumerate(_xor_peers(x, y, c)):
                n = w * (NDEV - 1) + k
                _remote(_piece(part_refs[w], *peer), land_refs[w].at[k], send_sems.at[n], recv_sems.at[n],
                        peer).start()
        token[...] = jnp.zeros_like(token)

    n = nw * (NDEV - 1)
    res = _split_call(body, name, list(parts) + lands, (n, n), (TOKEN,))
    return res[0], res[1], res[2:2 + nw], res[2 + nw:2 + 2 * nw], res[-1]


def _reduce_wait(send_sems, recv_sems, parts, lands, after, name):
    nw = len(parts)

    def body(*refs):
        part_refs, land_refs = refs[:nw], refs[nw:2 * nw]
        send_sems, recv_sems = refs[2 * nw:2 * nw + 2]
        x, y, c = _position()
        for w in range(nw):
            for k, peer in enumerate(_xor_peers(x, y, c)):
                n = w * (NDEV - 1) + k
                cp = _remote(_piece(part_refs[w], *peer), land_refs[w].at[k], send_sems.at[n], recv_sems.at[n], peer)
                cp.wait_send()
                cp.wait_recv()

    operands = list(parts) + list(lands)
    res = pl.pallas_call(
        body, name=name, out_shape=tuple(pltpu.HBM(a.shape, a.dtype) for a in operands),
        in_specs=(HBM,) * (2 * nw) + (SEM, SEM, ANY), out_specs=(HBM,) * (2 * nw),
        input_output_aliases={i: i for i in range(2 * nw)},
        compiler_params=pltpu.CompilerParams(has_side_effects=DATAFLOW),
    )(*operands, send_sems, recv_sems, after)
    return res[:nw], res[nw:]


def _sum_pieces(part, land, sel, name):
    half = part.shape[-2] // 2
    br = 128 if half % 128 == 0 else half // 2
    nb = half // br

    def body(sel_ref, own_ref, *refs):
        acc = own_ref[...]
        for r in refs[:NDEV - 1]:
            acc = acc + r[...]
        refs[NDEV - 1][...] = acc

    if part.ndim == 3:
        own_spec = pl.BlockSpec((None, br, D), lambda i, sel_ref: (sel_ref[0], sel_ref[1] * nb + i, 0))
    else:
        own_spec = pl.BlockSpec((br, D), lambda i, sel_ref: (sel_ref[1] * nb + i, sel_ref[0]))
    slot_specs = [pl.BlockSpec((None, br, D), functools.partial(lambda i, sel_ref, k: (k, i, 0), k=k))
                  for k in range(NDEV - 1)]
    return pl.pallas_call(
        body, name=name,
        grid_spec=pltpu.PrefetchScalarGridSpec(
            num_scalar_prefetch=1, grid=(nb,), in_specs=[own_spec] + slot_specs,
            out_specs=pl.BlockSpec((br, D), lambda i, sel_ref: (i, 0))),
        out_shape=jax.ShapeDtypeStruct((half, D), F32),
        compiler_params=_cparams("arbitrary"),
    )(sel, part, *([land] * (NDEV - 1)))


def _share_start(halves, name):
    nw = len(halves)
    lands = [lax.empty(h.shape, F32) for h in halves]

    def body(*refs):
        h_refs, land_refs = refs[:nw], refs[nw:2 * nw]
        send_sems, recv_sems = refs[2 * nw:2 * nw + 2]
        token = refs[-1]
        x, y, c = _position()
        for w in range(nw):
            _remote(h_refs[w], land_refs[w], send_sems.at[w], recv_sems.at[w], (x, y, 1 - c)).start()
        token[...] = jnp.zeros_like(token)

    res = _split_call(body, name, list(halves) + lands, (nw, nw), (TOKEN,))
    return res[0], res[1], res[2:2 + nw], res[2 + nw:2 + 2 * nw], res[-1]


def _share_wait(send_sems, recv_sems, halves, lands, after, name):
    nw = len(halves)

    def body(*refs):
        h_refs, land_refs = refs[:nw], refs[nw:2 * nw]
        send_sems, recv_sems = refs[2 * nw:2 * nw + 2]
        x, y, c = _position()
        for w in range(nw):
            cp = _remote(h_refs[w], land_refs[w], send_sems.at[w], recv_sems.at[w], (x, y, 1 - c))
            cp.wait_send()
            cp.wait_recv()

    operands = list(halves) + list(lands)
    res = pl.pallas_call(
        body, name=name, out_shape=tuple(pltpu.HBM(a.shape, a.dtype) for a in operands),
        in_specs=(HBM,) * (2 * nw) + (SEM, SEM, ANY), out_specs=(HBM,) * (2 * nw),
        input_output_aliases={i: i for i in range(2 * nw)},
        compiler_params=pltpu.CompilerParams(has_side_effects=DATAFLOW),
    )(*operands, send_sems, recv_sems, after)
    return res[:nw], res[nw:]


def _join_halves(own, other, c):
    first = jnp.where(c == 0, own, other)
    second = jnp.where(c == 0, other, own)
    return jnp.concatenate([first, second], axis=0)


SMALL_SIZES = (("norm1_g", D), ("sgu_ln_g", GW), ("sgu_ln_b", GW), ("sgu_w", NG * CHUNK * CHUNK),
               ("sgu_b", NG * CHUNK), ("attn_out_g", A), ("gmlp_out_g", GW), ("norm2_g", D),
               ("final_norm_g", D))
PARAM_ROWS = sum(n for _, n in SMALL_SIZES) // LANES
SMALL_ROWS = PARAM_ROWS + 8


def _pack_small(tree, first_extra=None):
    extra = jnp.zeros((8 * LANES,), F32)
    if first_extra is not None:
        extra = extra.at[0].set(first_extra)
    flat = jnp.concatenate([tree[n].reshape(-1) for n, _ in SMALL_SIZES] + [extra])
    return flat.reshape(SMALL_ROWS, LANES)


def _unpack_small(pack, shapes):
    flat = pack.reshape(-1)
    out, off = {}, 0
    for n, size in SMALL_SIZES:
        out[n] = flat[off:off + size].reshape(shapes[n])
        off += size
    return out


def _small_allreduce_adamw(gpack, wpack, mpack, vpack):
    def body(g_ref, w_ref, m_ref, v_ref, go_ref, d_ref, mo_ref, vo_ref, slots, send_sems, recv_sems):
        x, y, c = _position()
        me = 4 * x + 2 * y + c
        slots[me] = g_ref[...]
        peers = _xor_peers(x, y, c)
        sends = []
        for k, peer in enumerate(peers):
            cp = _remote(g_ref, slots.at[me], send_sems.at[k], recv_sems.at[k], peer)
            cp.start()
            sends.append(cp)
        for k, (px, py, pc) in enumerate(peers):
            _remote(g_ref, slots.at[4 * px + 2 * py + pc], send_sems.at[k], recv_sems.at[k],
                    (px, py, pc)).wait_recv()
        for cp in sends:
            cp.wait_send()
        total = slots[0]
        for k in range(1, NDEV):
            total = total + slots[k]
        go_ref[...] = total
        d, mn, vn = _adamw_math(w_ref[...], total, m_ref[...], v_ref[...])
        d_ref[...] = d
        mo_ref[...] = mn
        vo_ref[...] = vn

    sd = jax.ShapeDtypeStruct((SMALL_ROWS, LANES), F32)
    vm = pl.BlockSpec(memory_space=pltpu.VMEM)
    return pl.pallas_call(
        body, name="small_allreduce_adamw", in_specs=[vm] * 4, out_specs=[vm] * 4, out_shape=[sd] * 4,
        scratch_shapes=[pltpu.VMEM((NDEV, SMALL_ROWS, LANES), F32), pltpu.SemaphoreType.DMA((NDEV - 1,)),
                        pltpu.SemaphoreType.DMA((NDEV - 1,))],
        compiler_params=pltpu.CompilerParams(has_side_effects=True),
    )(gpack, wpack, mpack, vpack)


def kernel(x, norm1_g, w_in, sgu_ln_g, sgu_ln_b, sgu_w, sgu_b, attn_out_g, gmlp_out_g, w_out, norm2_g, w_ff1, w_ff2, final_norm_g, loss_target, m_norm1_g, m_w_in, m_sgu_ln_g, m_sgu_ln_b, m_sgu_w, m_sgu_b, m_attn_out_g, m_gmlp_out_g, m_w_out, m_norm2_g, m_w_ff1, m_w_ff2, m_final_norm_g, v_norm1_g, v_w_in, v_sgu_ln_g, v_sgu_ln_b, v_sgu_w, v_sgu_b, v_attn_out_g, v_gmlp_out_g, v_w_out, v_norm2_g, v_w_ff1, v_w_ff2, v_final_norm_g):
    names = [n for n, _ in SMALL_SIZES]
    w_small = dict(norm1_g=norm1_g, sgu_ln_g=sgu_ln_g, sgu_ln_b=sgu_ln_b, sgu_w=sgu_w, sgu_b=sgu_b,
                   attn_out_g=attn_out_g, gmlp_out_g=gmlp_out_g, norm2_g=norm2_g, final_norm_g=final_norm_g)
    m_small = dict(norm1_g=m_norm1_g, sgu_ln_g=m_sgu_ln_g, sgu_ln_b=m_sgu_ln_b, sgu_w=m_sgu_w, sgu_b=m_sgu_b,
                   attn_out_g=m_attn_out_g, gmlp_out_g=m_gmlp_out_g, norm2_g=m_norm2_g,
                   final_norm_g=m_final_norm_g)
    v_small = dict(norm1_g=v_norm1_g, sgu_ln_g=v_sgu_ln_g, sgu_ln_b=v_sgu_ln_b, sgu_w=v_sgu_w, sgu_b=v_sgu_b,
                   attn_out_g=v_attn_out_g, gmlp_out_g=v_gmlp_out_g, norm2_g=v_norm2_g,
                   final_norm_g=v_final_norm_g)
    shapes = {n: w_small[n].shape for n in names}

    start_in = _gather_start([w_in[0].T.astype(BF16)], "gather_in_start")
    issued = start_in[4][0:1, 0:1]
    start_rest = _gather_start([(w_out[0] + issued).astype(BF16), w_ff1[0].astype(BF16), w_ff2[0].astype(BF16)],
                               "gather_rest_start")
    hn1 = _norm1(x[0], norm1_g + start_rest[4][0:1, 0:1])
    win_t = _gather_wait(*start_in[:4], after=hn1, name="gather_in_wait")[0].reshape(INW, D)

    def rest_weights(after):
        wout, wff1, wff2 = _gather_wait(*start_rest[:4], after=after, name="gather_rest_wait")
        return wout.reshape(D, D), wff1, wff2.reshape(DFF, D)

    small = dict(
        norm1_g=norm1_g, ln_g=sgu_ln_g.reshape(1, GW), ln_b=sgu_ln_b.reshape(1, GW), sgu_w=sgu_w[0],
        sgu_wt=jnp.swapaxes(sgu_w[0], 1, 2), bias_t=jnp.repeat(sgu_b[0].T, DH, axis=1),
        attn_out_g=attn_out_g, gmlp_out_g=gmlp_out_g, norm2_g=norm2_g, final_norm_g=final_norm_g.reshape(1, D))
    xi, yi, ci = _position()
    sel = jnp.stack([2 * xi + yi, ci]).astype(jnp.int32)
    state = {}

    def as_slabs(g):
        return g.reshape(NCHIP, g.shape[0] // NCHIP, D)

    def early_grads(gwff1, gwff2, gwout):
        state["early"] = _reduce_start([gwff1, as_slabs(gwff2), as_slabs(gwout)], "reduce_early_start")
        return state["early"][4][0:1, 0:1]

    def after_attention_bwd(marker):
        send_sems, recv_sems, parts, lands, _ = state["early"]
        parts, lands = _reduce_wait(send_sems, recv_sems, parts, lands, marker, "reduce_early_wait")
        halves = [_sum_pieces(p, l, sel, "sum_" + n) for p, l, n in zip(parts, lands, ("w_ff1", "w_ff2", "w_out"))]
        state["early_share"] = _share_start(halves, "share_early_start")
        return state["early_share"][4][0:1, 0:1]

    def late_grads(gwin_t):
        state["late"] = _reduce_start([as_slabs(gwin_t)], "reduce_late_start")
        return state["late"][4][0:1, 0:1]

    loss_part, dx, sg, _ = _local_step(
        x[0], hn1, loss_target[0], small, win_t, rest_weights, early_grads, after_attention_bwd, late_grads)
    late = state["late"]
    send_sems, recv_sems, halves, lands, _ = state["early_share"]
    own, other = _share_wait(send_sems, recv_sems, halves, lands, dx, "share_early_wait")
    g_big = {n: _join_halves(o, t, ci) for n, o, t in zip(("w_ff1", "w_ff2", "w_out"), own, other)}
    w_big = dict(w_in=(w_in, m_w_in, v_w_in), w_out=(w_out, m_w_out, v_w_out),
                 w_ff1=(w_ff1, m_w_ff1, v_w_ff1), w_ff2=(w_ff2, m_w_ff2, v_w_ff2))
    grads, deltas, new_m, new_v = {}, {}, {}, {}

    def update(n):
        w, m, v = w_big[n]
        d, mn, vn = _adamw(w[0], g_big[n], m[0], v[0], "adamw_" + n)
        grads[n], deltas[n], new_m[n], new_v[n] = g_big[n][None], d[None], mn[None], vn[None]

    for n in ("w_ff1", "w_ff2", "w_out"):
        update(n)
    updated = deltas["w_out"][0, 0:8, 0:LANES] + deltas["w_ff1"][0, 0:8, 0:LANES] + deltas["w_ff2"][0, 0:8, 0:LANES]
    late_parts, late_lands = _reduce_wait(late[0], late[1], late[2], late[3], updated, "reduce_late_wait")
    late_share = _share_start([_sum_pieces(late_parts[0], late_lands[0], sel, "sum_w_in")], "share_late_start")

    g_small = dict(norm1_g=sg["norm1_g"], sgu_ln_g=sg["ln_g"], sgu_ln_b=sg["ln_b"], sgu_w=sg["sgu_w"],
                   sgu_b=sg["sgu_b"], attn_out_g=sg["attn_out_g"], gmlp_out_g=sg["gmlp_out_g"],
                   norm2_g=sg["norm2_g"], final_norm_g=sg["final_norm_g"])
    packs = _small_allreduce_adamw(_pack_small(g_small, loss_part) + late_share[4][0:1, 0:1], _pack_small(w_small),
                                   _pack_small(m_small), _pack_small(v_small))
    loss = packs[0][PARAM_ROWS, 0]
    for tree, pack in zip((grads, deltas, new_m, new_v), packs):
        tree.update(_unpack_small(pack, shapes))
    own, other = _share_wait(late_share[0], late_share[1], late_share[2], late_share[3], packs[0], "share_late_wait")
    g_big["w_in"] = _join_halves(own[0], other[0], ci).T
    update("w_in")

    order = ["norm1_g", "w_in", "sgu_ln_g", "sgu_ln_b", "sgu_w", "sgu_b", "attn_out_g", "gmlp_out_g", "w_out",
             "norm2_g", "w_ff1", "w_ff2", "final_norm_g"]
    return (loss, dx[None], *[grads[n] for n in order], *[deltas[n] for n in order],
            *[new_m[n] for n in order], *[new_v[n] for n in order])
```

```python
import functools
import math

import numpy as np
import jax
import jax.numpy as jnp
from jax import lax
from jax.experimental import pallas as pl
from jax.experimental.pallas import tpu as pltpu

F32 = jnp.float32
BF16 = jnp.bfloat16

D = 1024
NH = 12
DH = 64
A = NH * DH
NG = 4
GW = NG * DH
INW = 3 * A + 2 * GW
DFF = 4 * D
CHUNK = 128
PATTERNS = ((128, 1), (512, 4), (2048, 16))
EPS = 1e-6
SCALE = DH ** -0.5
NEG = -1e30

LR, B1, B2, AEPS, WD, STEP = 0.001, 0.9, 0.999, 1e-08, 0.01, 10

TM = 512
TMX = 512
ATT_ROWS = 4096
FF_CH = 1024
LANES = 128
NCHIP = 4
VMEM_LIMIT = 56 * 1024 * 1024
MESH = pl.DeviceIdType.MESH


def _cparams(*sem, **kw):
    return pltpu.CompilerParams(dimension_semantics=sem if sem else None,
                                vmem_limit_bytes=VMEM_LIMIT, **kw)


def _dot(a, b):
    return jnp.dot(a, b, preferred_element_type=F32)


def _dot_nt(a, b):
    return lax.dot_general(a, b, (((1,), (1,)), ((), ())), preferred_element_type=F32)


def _dot_tn(a, b):
    return lax.dot_general(a, b, (((0,), (0,)), ((), ())), preferred_element_type=F32)


def _dot_hi(a, b):
    return jnp.dot(a, b, preferred_element_type=F32, precision=lax.Precision.HIGHEST)


def _alibi_slopes(n):
    def pow2(m):
        start = 2.0 ** (-8.0 / m)
        return [start ** (i + 1) for i in range(m)]
    if math.log2(n).is_integer():
        s = pow2(n)
    else:
        c = 2 ** int(math.floor(math.log2(n)))
        s = pow2(c) + pow2(2 * c)[0::2][: n - c]
    return np.asarray(s, dtype=np.float32)


def _rms_fwd(v, g):
    r = lax.rsqrt(jnp.mean(v * v, axis=-1, keepdims=True) + EPS)
    vn = v * r
    return vn * g, vn, r


def _rms_bwd(dy, vn, r, g):
    w = dy * g
    dv = r * (w - vn * jnp.mean(w * vn, axis=-1, keepdims=True))
    return dv, jnp.sum(dy * vn, axis=0, keepdims=True)


_K0 = math.sqrt(2.0 / math.pi)
_K1 = 0.044715


def _gelu(v):
    return 0.5 * v * (1.0 + jnp.tanh(_K0 * (v + _K1 * (v * v * v))))


def _gelu_grad(v):
    t = jnp.tanh(_K0 * (v + _K1 * (v * v * v)))
    return 0.5 * (1.0 + t) + 0.5 * v * (1.0 - t * t) * (_K0 * (1.0 + 3.0 * _K1 * v * v))


def _row_spec(rows, cols):
    return pl.BlockSpec((rows, cols), lambda i: (i, 0))


def _const_spec(shape):
    nd = len(shape)
    return pl.BlockSpec(shape, lambda i: (0,) * nd, pipeline_mode=pl.Buffered(1))


DILS = tuple(d for _, d in PATTERNS)


def _fill_cols(scr, value):
    for cb in range(value.shape[1] // LANES):
        scr[cb] = value[:, cb * LANES:(cb + 1) * LANES]


def _split_residues(scr, out_ref, dil):
    nb, rows, _ = scr.shape
    for r in range(dil):
        for cb in range(nb):
            piece = scr.at[cb][pl.ds(r, rows // dil, stride=dil), :]
            out_ref[r, :, cb * LANES:(cb + 1) * LANES] = piece.astype(out_ref.dtype)


def _merge_residues(in_ref, scr, dil):
    nb, rows, _ = scr.shape
    for r in range(dil):
        for cb in range(nb):
            scr.at[cb][pl.ds(r, rows // dil, stride=dil), :] = in_ref[r, :, cb * LANES:(cb + 1) * LANES].astype(F32)
    return jnp.concatenate([scr[cb] for cb in range(nb)], axis=-1)


def _col_scratch(rows, width):
    return pltpu.VMEM((width // LANES, rows, LANES), F32)


def _res_spec(dil, rows, width):
    return pl.BlockSpec((dil, rows // dil, width), lambda i: (0, i, 0))


def _res_shape(s, dil, width, dtype):
    return jax.ShapeDtypeStruct((dil, s // dil, width), dtype)


def _norm1(x, g1):
    s = x.shape[0]

    def body(x_ref, g_ref, hn_ref):
        hn, _, _ = _rms_fwd(x_ref[...], g_ref[...])
        hn_ref[...] = hn.astype(BF16)

    return pl.pallas_call(
        body, name="norm1", grid=(s // TM,), in_specs=[_row_spec(TM, D), _const_spec((1, D))],
        out_specs=_row_spec(TM, D), out_shape=jax.ShapeDtypeStruct((s, D), BF16),
        compiler_params=_cparams("arbitrary"),
    )(x, g1)


def _inproj_fwd(hn1, win_t):
    s = hn1.shape[0]
    nd = len(DILS)

    def body(hn_ref, w_ref, *rest):
        qkv_refs = rest[:3 * nd]
        u_ref, z_ref, scr = rest[3 * nd:]
        hn = hn_ref[...]
        for t in range(3):
            seg = _dot_nt(hn, w_ref[t * A:(t + 1) * A, :])
            seg = seg * SCALE if t == 0 else seg
            _fill_cols(scr, seg)
            for di, dil in enumerate(DILS):
                if dil == 1:
                    qkv_refs[t * nd + di][0] = seg.astype(BF16)
                else:
                    _split_residues(scr, qkv_refs[t * nd + di], dil)
        u_ref[...] = _dot_nt(hn, w_ref[3 * A:3 * A + GW, :])
        z_ref[...] = _dot_nt(hn, w_ref[3 * A + GW:INW, :])

    res = pl.pallas_call(
        body, name="inproj_fwd", grid=(s // TM,),
        in_specs=[_row_spec(TM, D), _const_spec((INW, D))],
        out_specs=[_res_spec(d, TM, A) for _ in range(3) for d in DILS] + [_row_spec(TM, GW), _row_spec(TM, GW)],
        out_shape=[_res_shape(s, d, A, BF16) for _ in range(3) for d in DILS]
                  + [jax.ShapeDtypeStruct((s, GW), F32)] * 2,
        scratch_shapes=[_col_scratch(TM, A)],
        compiler_params=_cparams("arbitrary"),
    )(hn1, win_t)
    q, k, v = (res[t * nd:(t + 1) * nd] for t in range(3))
    return q, k, v, res[-2], res[-1]


def _att_geometry(length, dil):
    merge = max(1, min(dil, ATT_ROWS // length))
    rows = min(length * merge, ATT_ROWS)
    nsub = rows // CHUNK
    return merge, rows, length * merge // rows, nsub, min(length // CHUNK, nsub)


def _merged(t, merge):
    return t.reshape(t.shape[0] // merge, t.shape[1] * merge, t.shape[2])


def _stack_heads(t):
    lane = lax.broadcasted_iota(jnp.int32, t.shape, 1)
    zero = jnp.zeros_like(t)
    return jnp.concatenate([jnp.where(lane < DH, t, zero), jnp.where(lane >= DH, t, zero)], axis=0)


def _head_cols(t, hp):
    lane = lax.broadcasted_iota(jnp.int32, t.shape, 1)
    cols = [jnp.sum(jnp.where(lane == 2 * hp + h, t, 0.0), axis=-1, keepdims=True) for h in range(2)]
    return jnp.concatenate(cols, axis=0)


def _unstack_heads(t2):
    n = t2.shape[0] // 2
    lane = lax.broadcasted_iota(jnp.int32, (n, LANES), 1)
    return jnp.where(lane < DH, t2[:n], t2[n:])


def _query_window_bias(s0, s1, dil, first):
    row = lax.broadcasted_iota(jnp.int32, (2 * CHUNK, 2 * CHUNK), 0)
    col = lax.broadcasted_iota(jnp.int32, (2 * CHUNK, 2 * CHUNK), 1)
    steps = (row & (CHUNK - 1)) + CHUNK - col
    valid = (steps >= 0) & (steps <= CHUNK)
    if first:
        valid = valid & (col >= CHUNK)
    slope = jnp.where(row < CHUNK, s0, s1)
    return jnp.where(valid, -slope * (steps * dil).astype(F32), NEG)


def _key_block_bias(s0, s1, dil, last):
    key = lax.broadcasted_iota(jnp.int32, (CHUNK, 4 * CHUNK), 0)
    col = lax.broadcasted_iota(jnp.int32, (CHUNK, 4 * CHUNK), 1)
    wq = col & (2 * CHUNK - 1)
    steps = wq - key
    valid = (steps >= 0) & (steps <= CHUNK)
    if last:
        valid = valid & (wq < CHUNK)
    slope = jnp.where(col < 2 * CHUNK, s0, s1)
    return jnp.where(valid, -slope * (steps * dil).astype(F32), NEG)


def _head_rows(t, hp):
    row = lax.broadcasted_iota(jnp.int32, (8, LANES), 0)
    lane = lax.broadcasted_iota(jnp.int32, (8, LANES), 1)
    pick = jnp.where((row < 2) & (lane == 2 * hp + row), 1.0, 0.0).astype(BF16)
    hi = t.astype(BF16)
    rest = t - hi.astype(F32)
    mid = rest.astype(BF16)
    low = (rest - mid.astype(F32)).astype(BF16)
    return _dot_nt(pick, hi) + _dot_nt(pick, mid) + _dot_nt(pick, low)


def _att_specs(dil, rows, nsub, nblk):
    main = pl.BlockSpec((None, rows, LANES), lambda r, c, hp: (r, c, hp))
    prev = pl.BlockSpec((None, CHUNK, LANES), lambda r, c, hp: (r, jnp.maximum(c * nsub - 1, 0), hp))
    nxt = pl.BlockSpec((None, CHUNK, LANES), lambda r, c, hp: (r, jnp.minimum((c + 1) * nsub, nblk - 1), hp))
    main_heads = pl.BlockSpec((None, rows, LANES), lambda r, c, hp: (r, c, 0))
    nxt_heads = pl.BlockSpec((None, CHUNK, LANES), lambda r, c, hp: (r, jnp.minimum((c + 1) * nsub, nblk - 1), 0))
    return main, prev, nxt, main_heads, nxt_heads


def _row_start(i):
    return i * CHUNK if isinstance(i, int) else pl.multiple_of(i * CHUNK, CHUNK)


def _first_blocks(block, nsub, seg, nch, ch, first_bias, bias_buf):
    for i in range(nsub):
        if i % seg:
            block(i, bias_buf[...])
        elif nch == 1:
            block(i, first_bias())
        else:
            block(i, jnp.where(ch == 0, first_bias(), bias_buf[...]))


def _last_blocks(block, nsub, seg, nch, ch, last_bias, bias_buf):
    for i in range(nsub):
        if (i + 1) % seg:
            block(i, bias_buf[...])
        elif nch == 1:
            block(i, last_bias())
        else:
            block(i, jnp.where(ch == nch - 1, last_bias(), bias_buf[...]))


def _attn_fwd(q, k, v, slopes, dil):
    length = q.shape[1]
    merge, rows, nch, nsub, seg = _att_geometry(length, dil)
    main, prev, _, main_heads, _ = _att_specs(dil, rows, nsub, length * merge // CHUNK)
    q, k, v = (_merged(t, merge) for t in (q, k, v))

    def body(sl_ref, q_ref, k_ref, v_ref, kh_ref, vh_ref, o_ref, lse_ref, kbuf, vbuf, bias_buf):
        ch = pl.program_id(1)
        hp = pl.program_id(2)
        lane = lax.broadcasted_iota(jnp.int32, (CHUNK, LANES), 1)
        kbuf[0:CHUNK, :] = kh_ref[...]
        kbuf[CHUNK:, :] = k_ref[...]
        vbuf[0:CHUNK, :] = vh_ref[...]
        vbuf[CHUNK:, :] = v_ref[...]
        s0, s1 = sl_ref[2 * hp], sl_ref[2 * hp + 1]

        def block(i, bias):
            row = _row_start(i)
            rs = pl.ds(row, CHUNK)
            q2 = _stack_heads(q_ref[rs, :])
            kw = kbuf[pl.ds(row, 2 * CHUNK), :]
            vw = vbuf[pl.ds(row, 2 * CHUNK), :]
            sc = _dot_nt(q2, kw) + bias
            m = jnp.max(sc, axis=-1, keepdims=True)
            p = jnp.exp(sc - m)
            l = jnp.sum(p, axis=-1, keepdims=True)
            o2 = _dot(p.astype(BF16), vw) * (1.0 / l)
            o_ref[rs, :] = _unstack_heads(o2).astype(BF16)
            lse = m + jnp.log(l)
            seen = jnp.where(hp == 0, 0.0, lse_ref[rs, :])
            lse_ref[rs, :] = jnp.where(lane == 2 * hp, lse[:CHUNK], jnp.where(lane == 2 * hp + 1, lse[CHUNK:], seen))

        bias_buf[...] = _query_window_bias(s0, s1, dil, False)
        _first_blocks(block, nsub, seg, nch, ch, lambda: _query_window_bias(s0, s1, dil, True), bias_buf)

    sd = jax.ShapeDtypeStruct
    o, lse = pl.pallas_call(
        body, name=f"attn_fwd_d{dil}", grid=(dil // merge, nch, NH // 2),
        in_specs=[pl.BlockSpec(memory_space=pltpu.SMEM), main, main, main, prev, prev],
        out_specs=[main, main_heads],
        out_shape=[sd((dil // merge, length * merge, A), BF16), sd((dil // merge, length * merge, LANES), F32)],
        scratch_shapes=[pltpu.VMEM((rows + CHUNK, LANES), BF16), pltpu.VMEM((rows + CHUNK, LANES), BF16),
                        pltpu.VMEM((2 * CHUNK, 2 * CHUNK), F32)],
        compiler_params=_cparams("arbitrary", "arbitrary", "arbitrary"),
    )(slopes, q, k, v, k, v)
    return o.reshape(dil, length, A), lse.reshape(dil, length, LANES)


def _attn_bwd_dq(q, k, v, do, lse, delta, slopes, dil):
    length = q.shape[1]
    merge, rows, nch, nsub, seg = _att_geometry(length, dil)
    main, prev, _, main_heads, _ = _att_specs(dil, rows, nsub, length * merge // CHUNK)
    q, k, v, do, lse, delta = (_merged(t, merge) for t in (q, k, v, do, lse, delta))

    def body(sl_ref, q_ref, k_ref, v_ref, do_ref, lse_ref, dl_ref, kh_ref, vh_ref, dq_ref, kbuf, vbuf, bias_buf):
        ch = pl.program_id(1)
        hp = pl.program_id(2)
        kbuf[0:CHUNK, :] = kh_ref[...]
        kbuf[CHUNK:, :] = k_ref[...]
        vbuf[0:CHUNK, :] = vh_ref[...]
        vbuf[CHUNK:, :] = v_ref[...]
        s0, s1 = sl_ref[2 * hp], sl_ref[2 * hp + 1]

        def block(i, bias):
            row = _row_start(i)
            rs = pl.ds(row, CHUNK)
            q2 = _stack_heads(q_ref[rs, :])
            do2 = _stack_heads(do_ref[rs, :])
            lse2 = _head_cols(lse_ref[rs, :], hp)
            dl2 = _head_cols(dl_ref[rs, :], hp)
            kw = kbuf[pl.ds(row, 2 * CHUNK), :]
            vw = vbuf[pl.ds(row, 2 * CHUNK), :]
            p = jnp.exp(_dot_nt(q2, kw) + bias - lse2)
            ds = p * (_dot_nt(do2, vw) - dl2)
            dq_ref[rs, :] = _unstack_heads(_dot(ds.astype(BF16), kw)).astype(BF16)

        bias_buf[...] = _query_window_bias(s0, s1, dil, False)
        _first_blocks(block, nsub, seg, nch, ch, lambda: _query_window_bias(s0, s1, dil, True), bias_buf)

    dq = pl.pallas_call(
        body, name=f"attn_dq_d{dil}", grid=(dil // merge, nch, NH // 2),
        in_specs=[pl.BlockSpec(memory_space=pltpu.SMEM), main, main, main, main, main_heads, main_heads, prev, prev],
        out_specs=main, out_shape=jax.ShapeDtypeStruct((dil // merge, length * merge, A), BF16),
        scratch_shapes=[pltpu.VMEM((rows + CHUNK, LANES), BF16), pltpu.VMEM((rows + CHUNK, LANES), BF16),
                        pltpu.VMEM((2 * CHUNK, 2 * CHUNK), F32)],
        compiler_params=_cparams("arbitrary", "arbitrary", "arbitrary"),
    )(slopes, q, k, v, do, lse, delta, k, v)
    return dq.reshape(dil, length, A)


def _attn_bwd_dkv(q, k, v, do, lse, delta, slopes, dil):
    length = q.shape[1]
    merge, rows, nch, nsub, seg = _att_geometry(length, dil)
    main, _, nxt, main_heads, nxt_heads = _att_specs(dil, rows, nsub, length * merge // CHUNK)
    q, k, v, do, lse, delta = (_merged(t, merge) for t in (q, k, v, do, lse, delta))

    def body(sl_ref, k_ref, v_ref, q_ref, do_ref, lse_ref, dl_ref, qh_ref, doh_ref, lseh_ref, dlh_ref,
             dk_ref, dv_ref, qbuf, dobuf, lse_rows, dl_rows, bias_buf):
        ch = pl.program_id(1)
        hp = pl.program_id(2)
        for buf, main_ref, halo_ref in ((qbuf, q_ref, qh_ref), (dobuf, do_ref, doh_ref)):
            buf[0:rows, :] = main_ref[...]
            buf[rows:, :] = halo_ref[...]
        for buf, main_ref, halo_ref in ((lse_rows, lse_ref, lseh_ref), (dl_rows, dl_ref, dlh_ref)):
            buf[:, 0:rows] = _head_rows(main_ref[...], hp)
            buf[:, rows:] = _head_rows(halo_ref[...], hp)
        s0, s1 = sl_ref[2 * hp], sl_ref[2 * hp + 1]

        def block(i, bias):
            row = _row_start(i)
            rs = pl.ds(row, CHUNK)
            win = pl.ds(row, 2 * CHUNK)
            kc = k_ref[rs, :]
            vc = v_ref[rs, :]
            q2 = _stack_heads(qbuf[win, :])
            do2 = _stack_heads(dobuf[win, :])
            cols = slice(i * CHUNK, (i + 2) * CHUNK)
            lse2 = jnp.concatenate([lse_rows[0:1, cols], lse_rows[1:2, cols]], axis=1)
            dl2 = jnp.concatenate([dl_rows[0:1, cols], dl_rows[1:2, cols]], axis=1)
            pt = jnp.exp(_dot_nt(kc, q2) + bias - lse2)
            dst = pt * (_dot_nt(vc, do2) - dl2)
            dv_ref[rs, :] = _dot(pt.astype(BF16), do2).astype(BF16)
            dk_ref[rs, :] = _dot(dst.astype(BF16), q2).astype(BF16)

        bias_buf[...] = _key_block_bias(s0, s1, dil, False)
        _last_blocks(block, nsub, seg, nch, ch, lambda: _key_block_bias(s0, s1, dil, True), bias_buf)

    sd = jax.ShapeDtypeStruct((dil // merge, length * merge, A), BF16)
    dk, dv = pl.pallas_call(
        body, name=f"attn_dkv_d{dil}", grid=(dil // merge, nch, NH // 2),
        in_specs=[pl.BlockSpec(memory_space=pltpu.SMEM), main, main, main, main, main_heads, main_heads,
                  nxt, nxt, nxt_heads, nxt_heads],
        out_specs=[main, main], out_shape=[sd, sd],
        scratch_shapes=[pltpu.VMEM((rows + CHUNK, LANES), BF16), pltpu.VMEM((rows + CHUNK, LANES), BF16),
                        pltpu.VMEM((8, rows + CHUNK), F32), pltpu.VMEM((8, rows + CHUNK), F32),
                        pltpu.VMEM((CHUNK, 4 * CHUNK), F32)],
        compiler_params=_cparams("arbitrary", "arbitrary", "arbitrary"),
    )(slopes, k, v, q, do, lse, delta, q, do, lse, delta)
    return dk.reshape(dil, length, A), dv.reshape(dil, length, A)


def _group_masks(width):
    lane = lax.broadcasted_iota(jnp.int32, (1, width), 1)
    return [(lane >= g * DH) & (lane < (g + 1) * DH) for g in range(width // DH)]


def _group_mean_matrix():
    i = lax.broadcasted_iota(jnp.int32, (GW, GW), 0) // DH
    j = lax.broadcasted_iota(jnp.int32, (GW, GW), 1) // DH
    return jnp.where(i == j, 1.0 / DH, 0.0).astype(F32)


def _tri_mask(lower):
    t = lax.broadcasted_iota(jnp.int32, (CHUNK, CHUNK), 0)
    u = lax.broadcasted_iota(jnp.int32, (CHUNK, CHUNK), 1)
    return (u <= t) if lower else (u >= t)


def _sgu_forward(u, z, lng, lnb, w_ref, bias_t, pmat, rows):
    ug = _gelu(u)
    zg = _gelu(z)
    mu = _dot_hi(zg, pmat)
    zc = zg - mu
    var = _dot_hi(zc * zc, pmat)
    rstd = lax.rsqrt(var + EPS)
    zhat = zc * rstd
    zn = (zhat * lng + lnb).astype(BF16)
    gm = _group_masks(GW)
    tri = _tri_mask(True)
    ws = [jnp.where(tri, w_ref[g], 0.0).astype(BF16) for g in range(NG)]
    pieces = []
    for c in range(rows // CHUNK):
        znc = zn[c * CHUNK:(c + 1) * CHUNK, :]
        mix = None
        for g in range(NG):
            part = jnp.where(gm[g], _dot(ws[g], znc), 0.0)
            mix = part if mix is None else mix + part
        pieces.append(mix + bias_t)
    mixed = jnp.concatenate(pieces, axis=0) if len(pieces) > 1 else pieces[0]
    return ug * mixed, ug, zhat, rstd, zn, mixed


def _head_spread():
    h = lax.broadcasted_iota(jnp.int32, (LANES, A), 0)
    lane = lax.broadcasted_iota(jnp.int32, (LANES, A), 1)
    return jnp.where(lane // DH == h, 1.0, 0.0).astype(BF16)


def _bf16_pieces(t, n):
    pieces = []
    for _ in range(n):
        piece = t.astype(BF16)
        pieces.append(piece)
        t = t - piece.astype(F32)
    return pieces


def _mix_fwd(os_, ls_, u, z, x, lng, lnb, sgu_w, bias_t, ga, gg, wout):
    s = x.shape[0]
    nd = len(DILS)
    nscr = sum(1 for d in DILS if d > 1)

    def body(*refs):
        o_refs, l_refs = refs[:nd], refs[nd:2 * nd]
        u_ref, z_ref, x_ref, lng_ref, lnb_ref, w_ref, bt_ref, ga_ref, gg_ref, wo_ref = refs[2 * nd:2 * nd + 10]
        attn_ref = refs[2 * nd + 10]
        lse_refs = refs[2 * nd + 11:3 * nd + 11]
        mixed_ref, h1_ref = refs[3 * nd + 11:3 * nd + 13]
        scr = refs[3 * nd + 13:]
        scr_o, scr_l, scr_lse = scr[:nscr], scr[nscr:2 * nscr], scr[2 * nscr]
        ov, lv, j = [], [], 0
        for di, dil in enumerate(DILS):
            if dil == 1:
                ov.append(o_refs[di][0].astype(F32))
                lv.append(l_refs[di][0])
            else:
                ov.append(_merge_residues(o_refs[di], scr_o[j], dil))
                lv.append(_merge_residues(l_refs[di], scr_l[j], dil))
                j += 1
        mx = functools.reduce(jnp.maximum, lv)
        es = [jnp.exp(l - mx) for l in lv]
        den = functools.reduce(lambda a, b: a + b, es)
        spread = _head_spread()
        attn = None
        for e, o in zip(es, ov):
            wide = functools.reduce(lambda a, b: a + b, [_dot(piece, spread) for piece in _bf16_pieces(e / den, 2)])
            attn = wide * o if attn is None else attn + wide * o
        attn_ref[...] = attn
        lse = mx + jnp.log(den)
        _fill_cols(scr_lse, lse)
        for di, dil in enumerate(DILS):
            if dil == 1:
                lse_refs[di][0] = lse
            else:
                _split_residues(scr_lse, lse_refs[di], dil)
        an, _, _ = _rms_fwd(attn, ga_ref[...])
        gmv, _, _, _, _, _ = _sgu_forward(u_ref[...], z_ref[...], lng_ref[...], lnb_ref[...], w_ref,
                                          bt_ref[...], _group_mean_matrix(), TMX)
        gn, _, _ = _rms_fwd(gmv, gg_ref[...])
        mixed = jnp.concatenate([an, gn], axis=-1).astype(BF16)
        mixed_ref[...] = mixed
        h1_ref[...] = x_ref[...] + _dot(mixed, wo_ref[...])

    sd = jax.ShapeDtypeStruct
    res = pl.pallas_call(
        body, name="mix_fwd", grid=(s // TMX,),
        in_specs=[_res_spec(d, TMX, A) for d in DILS] + [_res_spec(d, TMX, LANES) for d in DILS]
                 + [_row_spec(TMX, GW), _row_spec(TMX, GW),
                    _row_spec(TMX, D), _const_spec((1, GW)), _const_spec((1, GW)), _const_spec((NG, CHUNK, CHUNK)),
                    _const_spec((CHUNK, GW)), _const_spec((1, A)), _const_spec((1, GW)), _const_spec((D, D))],
        out_specs=[_row_spec(TMX, A)] + [_res_spec(d, TMX, LANES) for d in DILS]
                  + [_row_spec(TMX, D), _row_spec(TMX, D)],
        out_shape=[sd((s, A), F32)] + [_res_shape(s, d, LANES, F32) for d in DILS]
                  + [sd((s, D), BF16), sd((s, D), F32)],
        scratch_shapes=[_col_scratch(TMX, A)] * nscr + [_col_scratch(TMX, LANES)] * (nscr + 1),
        compiler_params=_cparams("arbitrary"),
    )(*os_, *ls_, u, z, x, lng, lnb, sgu_w, bias_t, ga, gg, wout)
    return res[0], res[1:1 + nd], res[1 + nd], res[2 + nd]


def _mlp_fwd(h1, g2, wff1, wff2, gf, target):
    s = h1.shape[0]

    def body(h1_ref, g2_ref, w1_ref, w2_ref, gf_ref, t_ref, hn_ref, rf_ref, dh2_ref, loss_ref, dgf_ref):
        i = pl.program_id(0)
        h1v = h1_ref[...]
        hn, _, _ = _rms_fwd(h1v, g2_ref[...])
        hn = hn.astype(BF16)
        hn_ref[...] = hn
        acc = h1v
        for j in range(DFF // FF_CH):
            cols = slice(j * FF_CH, (j + 1) * FF_CH)
            rf = jnp.maximum(_dot(hn, w1_ref[j]), 0.0)
            act = (rf * rf).astype(BF16)
            rf_ref[:, cols] = rf.astype(BF16)
            acc = acc + _dot(act, w2_ref[cols, :])
        y, h2n, r3 = _rms_fwd(acc, gf_ref[...])
        err = y - t_ref[...]
        part = 0.5 * jnp.sum(jnp.mean(err * err, axis=-1, keepdims=True), axis=0, keepdims=True)
        dy = err * (1.0 / D)
        dh2, dgf = _rms_bwd(dy, h2n, r3, gf_ref[...])
        dh2_ref[...] = dh2

        @pl.when(i == 0)
        def _():
            loss_ref[...] = jnp.zeros_like(loss_ref)
            dgf_ref[...] = jnp.zeros_like(dgf_ref)

        loss_ref[...] += jnp.broadcast_to(part, loss_ref.shape)
        dgf_ref[...] += dgf

    sd = jax.ShapeDtypeStruct
    return pl.pallas_call(
        body, name="mlp_fwd", grid=(s // TM,),
        in_specs=[_row_spec(TM, D), _const_spec((1, D)), _const_spec((DFF // FF_CH, D, FF_CH)), _const_spec((DFF, D)),
                  _const_spec((1, D)), _row_spec(TM, D)],
        out_specs=[_row_spec(TM, D), _row_spec(TM, DFF), _row_spec(TM, D),
                   _const_spec((1, LANES)), _const_spec((1, D))],
        out_shape=[sd((s, D), BF16), sd((s, DFF), BF16), sd((s, D), F32),
                   sd((1, LANES), F32), sd((1, D), F32)],
        compiler_params=_cparams("arbitrary"),
    )(h1, g2, wff1, wff2, gf, target)


def _mlp_bwd(dh2, rf, h1, g2, wff1, wff2):
    s = h1.shape[0]

    def body(dh2_ref, rf_ref, h1_ref, g2_ref, w1_ref, w2_ref, df_ref, dh1_ref, dg2_ref):
        i = pl.program_id(0)
        dh2v = dh2_ref[...]
        dh2b = dh2v.astype(BF16)
        dhn = jnp.zeros((TM, D), F32)
        for j in range(DFF // FF_CH):
            cols = slice(j * FF_CH, (j + 1) * FF_CH)
            da = _dot_nt(dh2b, w2_ref[cols, :])
            df = (da * (2.0 * rf_ref[:, cols].astype(F32))).astype(BF16)
            df_ref[:, cols] = df
            dhn = dhn + _dot_nt(df, w1_ref[j])
        _, h1n, r2 = _rms_fwd(h1_ref[...], g2_ref[...])
        dres, dg2 = _rms_bwd(dhn, h1n, r2, g2_ref[...])
        dh1_ref[...] = dh2v + dres

        @pl.when(i == 0)
        def _():
            dg2_ref[...] = jnp.zeros_like(dg2_ref)

        dg2_ref[...] += dg2

    sd = jax.ShapeDtypeStruct
    return pl.pallas_call(
        body, name="mlp_bwd", grid=(s // TM,),
        in_specs=[_row_spec(TM, D), _row_spec(TM, DFF), _row_spec(TM, D), _const_spec((1, D)),
                  _const_spec((DFF // FF_CH, D, FF_CH)), _const_spec((DFF, D))],
        out_specs=[_row_spec(TM, DFF), _row_spec(TM, D), _const_spec((1, D))],
        out_shape=[sd((s, DFF), BF16), sd((s, D), F32), sd((1, D), F32)],
        compiler_params=_cparams("arbitrary"),
    )(dh2, rf, h1, g2, wff1, wff2)


def _mix_bwd(dh1, attn, u, z, lng, lnb, sgu_w, sgu_wt, bias_t, ga, gg, wout):
    s = dh1.shape[0]
    nsteps = s // TMX
    nd = len(DILS)

    def body(*refs):
        dh1_ref, attn_ref, u_ref, z_ref, lng_ref, lnb_ref, w_ref, wt_ref, bt_ref, ga_ref, gg_ref, wo_ref = refs[:12]
        do_refs, dl_refs = refs[12:12 + nd], refs[12 + nd:12 + 2 * nd]
        (du_ref, dz_ref, dga_ref, dgg_ref, dlng_ref, dlnb_ref, dws_ref, db_ref,
         dbt_acc, scr_do, scr_dl) = refs[12 + 2 * nd:]
        i = pl.program_id(0)

        @pl.when(i == 0)
        def _():
            for r in (dga_ref, dgg_ref, dlng_ref, dlnb_ref, dws_ref, db_ref, dbt_acc):
                r[...] = jnp.zeros_like(r)

        dmixed = _dot_nt(dh1_ref[...].astype(BF16), wo_ref[...])
        attn = attn_ref[...]
        _, an, ra = _rms_fwd(attn, ga_ref[...])
        dattn, dga = _rms_bwd(dmixed[:, :A], an, ra, ga_ref[...])
        dga_ref[...] += dga
        _fill_cols(scr_do, dattn)
        spread = _head_spread()
        delta = functools.reduce(lambda a, b: a + b, [_dot_nt(piece, spread) for piece in _bf16_pieces(dattn * attn, 3)])
        _fill_cols(scr_dl, delta)
        for di, dil in enumerate(DILS):
            if dil == 1:
                do_refs[di][0] = dattn.astype(BF16)
                dl_refs[di][0] = delta
            else:
                _split_residues(scr_do, do_refs[di], dil)
                _split_residues(scr_dl, dl_refs[di], dil)
        pmat = _group_mean_matrix()
        lng = lng_ref[...]
        uv, zv = u_ref[...], z_ref[...]
        gmv, ug, zhat, rstd, zn, mixed = _sgu_forward(uv, zv, lng, lnb_ref[...], w_ref, bt_ref[...], pmat, TMX)
        _, gmn, rg = _rms_fwd(gmv, gg_ref[...])
        dgm, dgg = _rms_bwd(dmixed[:, A:], gmn, rg, gg_ref[...])
        dgg_ref[...] += dgg
        du_ref[...] = (dgm * mixed * _gelu_grad(uv)).astype(BF16)
        dmx = dgm * ug
        dmxb = dmx.astype(BF16)
        gm = _group_masks(GW)
        tri_t = _tri_mask(False)
        wst = [jnp.where(tri_t, wt_ref[g], 0.0).astype(BF16) for g in range(NG)]
        zero = jnp.zeros((CHUNK, GW), BF16)
        dzn_pieces = []
        for c in range(TMX // CHUNK):
            rs = slice(c * CHUNK, (c + 1) * CHUNK)
            dmc = dmxb[rs, :]
            znc = zn[rs, :]
            dbt_acc[...] += dmx[rs, :]
            dzn = None
            for g in range(NG):
                dws_ref[g] += _dot_nt(jnp.where(gm[g], dmc, zero), znc)
                part = jnp.where(gm[g], _dot(wst[g], dmc), 0.0)
                dzn = part if dzn is None else dzn + part
            dzn_pieces.append(dzn)
        dzn = jnp.concatenate(dzn_pieces, axis=0)
        dlng_ref[...] += jnp.sum(dzn * zhat, axis=0, keepdims=True)
        dlnb_ref[...] += jnp.sum(dzn, axis=0, keepdims=True)
        dzh = dzn * lng
        dzg = rstd * (dzh - _dot_hi(dzh, pmat) - zhat * _dot_hi(dzh * zhat, pmat))
        dz_ref[...] = (dzg * _gelu_grad(zv)).astype(BF16)

        @pl.when(i == nsteps - 1)
        def _():
            tri = _tri_mask(True)
            for g in range(NG):
                dws_ref[g] = jnp.where(tri, dws_ref[g], 0.0)
            acc = dbt_acc[...]
            lane = lax.broadcasted_iota(jnp.int32, (CHUNK, LANES), 1)
            out = jnp.zeros((CHUNK, LANES), F32)
            for g in range(NG):
                sg = jnp.sum(jnp.where(gm[g], acc, 0.0), axis=-1, keepdims=True)
                out = jnp.where(lane == g, sg, out)
            db_ref[...] = out

    sd = jax.ShapeDtypeStruct
    res = pl.pallas_call(
        body, name="mix_bwd", grid=(nsteps,),
        in_specs=[_row_spec(TMX, D), _row_spec(TMX, A), _row_spec(TMX, GW), _row_spec(TMX, GW),
                  _const_spec((1, GW)), _const_spec((1, GW)), _const_spec((NG, CHUNK, CHUNK)),
                  _const_spec((NG, CHUNK, CHUNK)), _const_spec((CHUNK, GW)), _const_spec((1, A)),
                  _const_spec((1, GW)), _const_spec((D, D))],
        out_specs=[_res_spec(d, TMX, A) for d in DILS] + [_res_spec(d, TMX, LANES) for d in DILS]
                  + [_row_spec(TMX, GW), _row_spec(TMX, GW),
                   _const_spec((1, A)), _const_spec((1, GW)), _const_spec((1, GW)), _const_spec((1, GW)),
                   _const_spec((NG, CHUNK, CHUNK)), _const_spec((CHUNK, LANES))],
        out_shape=[_res_shape(s, d, A, BF16) for d in DILS] + [_res_shape(s, d, LANES, F32) for d in DILS]
                  + [sd((s, GW), BF16), sd((s, GW), BF16),
                   sd((1, A), F32), sd((1, GW), F32), sd((1, GW), F32), sd((1, GW), F32),
                   sd((NG, CHUNK, CHUNK), F32), sd((CHUNK, LANES), F32)],
        scratch_shapes=[pltpu.VMEM((CHUNK, GW), F32), _col_scratch(TMX, A), _col_scratch(TMX, LANES)],
        compiler_params=_cparams("arbitrary"),
    )(dh1, attn, u, z, lng, lnb, sgu_w, sgu_wt, bias_t, ga, gg, wout)
    return (res[:nd], res[nd:2 * nd]) + tuple(res[2 * nd:])


def _dproj_merge(dqs, dks, dvs, du, dz, pin):
    s = du.shape[0]
    nd = len(DILS)
    nscr = sum(1 for d in DILS if d > 1)

    def body(*refs):
        pin_ref = refs[0]
        parts = [refs[1 + t * nd:1 + (t + 1) * nd] for t in range(3)]
        du_ref, dz_ref, dp_ref = refs[1 + 3 * nd:4 + 3 * nd]
        scr = refs[4 + 3 * nd:]
        sums = []
        for t in range(3):
            total, j = None, 0
            for di, dil in enumerate(DILS):
                if dil == 1:
                    term = parts[t][di][0].astype(F32)
                else:
                    term = _merge_residues(parts[t][di], scr[t * nscr + j], dil)
                    j += 1
                total = term if total is None else total + term
            sums.append(total)
        dp_ref[...] = jnp.concatenate([sums[0] * SCALE, sums[1], sums[2], du_ref[...].astype(F32) + pin_ref[0, 0],
                                       dz_ref[...].astype(F32)], axis=-1).astype(BF16)

    return pl.pallas_call(
        body, name="dproj_merge", grid=(s // TMX,),
        in_specs=[pl.BlockSpec(memory_space=pltpu.SMEM)] + [_res_spec(d, TMX, A) for d in DILS] * 3
                 + [_row_spec(TMX, GW)] * 2,
        out_specs=_row_spec(TMX, INW), out_shape=jax.ShapeDtypeStruct((s, INW), BF16),
        scratch_shapes=[_col_scratch(TMX, A)] * (3 * nscr),
        compiler_params=_cparams("arbitrary"),
    )(pin, *dqs, *dks, *dvs, du, dz)


def _inproj_bwd(dproj, dh1, x, g1, win_t):
    s = x.shape[0]

    def body(dp_ref, dh1_ref, x_ref, g_ref, w_ref, dx_ref, dg_ref):
        i = pl.program_id(0)
        dhn = _dot(dp_ref[...], w_ref[...])
        _, xn, r1 = _rms_fwd(x_ref[...], g_ref[...])
        dres, dg = _rms_bwd(dhn, xn, r1, g_ref[...])
        dx_ref[...] = dh1_ref[...] + dres

        @pl.when(i == 0)
        def _():
            dg_ref[...] = jnp.zeros_like(dg_ref)

        dg_ref[...] += dg

    sd = jax.ShapeDtypeStruct
    return pl.pallas_call(
        body, name="inproj_bwd", grid=(s // TM,),
        in_specs=[_row_spec(TM, INW), _row_spec(TM, D), _row_spec(TM, D), _const_spec((1, D)), _const_spec((INW, D))],
        out_specs=[_row_spec(TM, D), _const_spec((1, D))],
        out_shape=[sd((s, D), F32), sd((1, D), F32)],
        compiler_params=_cparams("arbitrary"),
    )(dproj, dh1, x, g1, win_t)


def _wgrad(a, b, name, bm, bn, bk=2 * TM, square_a=False, also_bf16=False):
    s, m = a.shape
    n = b.shape[1]
    bm, bn = min(bm, m), min(bn, n)
    nk = s // bk

    def body(a_ref, b_ref, o_ref, *low):
        @pl.when(pl.program_id(2) == 0)
        def _():
            o_ref[...] = jnp.zeros_like(o_ref)

        av = a_ref[...]
        if square_a:
            av = av.astype(F32)
            av = av * av
        o_ref[...] += _dot_tn(av.astype(BF16), b_ref[...].astype(BF16))
        if also_bf16:
            @pl.when(pl.program_id(2) == nk - 1)
            def _():
                low[0][...] = o_ref[...].astype(BF16)

    out_spec = pl.BlockSpec((bm, bn), lambda i, j, k: (i, j))
    res = pl.pallas_call(
        body, name=name, grid=(m // bm, n // bn, nk),
        in_specs=[pl.BlockSpec((bk, bm), lambda i, j, k: (k, i)), pl.BlockSpec((bk, bn), lambda i, j, k: (k, j))],
        out_specs=[out_spec, out_spec] if also_bf16 else out_spec,
        out_shape=([jax.ShapeDtypeStruct((m, n), F32), jax.ShapeDtypeStruct((m, n), BF16)] if also_bf16
                   else jax.ShapeDtypeStruct((m, n), F32)),
        compiler_params=_cparams("arbitrary", "arbitrary", "arbitrary"),
    )(a, b)
    return res


def _adamw_math(w, g, m, v):
    m = B1 * m + (1.0 - B1) * g
    v = B2 * v + (1.0 - B2) * (g * g)
    m_hat = m / (1.0 - B1 ** STEP)
    v_hat = v / (1.0 - B2 ** STEP)
    delta = -LR * (m_hat / (jnp.sqrt(v_hat) + AEPS) + WD * w)
    return delta, m, v


def _adamw(w, g, m, v, name):
    rows, cols = w.shape
    br = min(rows, 256)
    while rows % br:
        br -= 8

    def body(w_ref, g_ref, m_ref, v_ref, d_ref, mo_ref, vo_ref):
        d, mn, vn = _adamw_math(w_ref[...], g_ref[...], m_ref[...], v_ref[...])
        d_ref[...] = d
        mo_ref[...] = mn
        vo_ref[...] = vn

    spec = _row_spec(br, cols)
    sd = jax.ShapeDtypeStruct((rows, cols), F32)
    return pl.pallas_call(
        body, name=name, grid=(rows // br,), in_specs=[spec] * 4, out_specs=[spec] * 3,
        out_shape=[sd, sd, sd], compiler_params=_cparams("arbitrary"),
    )(w, g, m, v)


def _local_step(x, hn1, target, small, win_t, rest_weights, early_grads=None, after_attention_bwd=None,
                late_grads=None):
    slopes = jnp.asarray(_alibi_slopes(NH))
    q, k, v, u, z = _inproj_fwd(hn1, win_t)
    outs, lses = [], []
    for i, dil in enumerate(DILS):
        o, l = _attn_fwd(q[i], k[i], v[i], slopes, dil)
        outs.append(o)
        lses.append(l)
    wout, wff1, wff2 = rest_weights(functools.reduce(lambda a, b: a + b, [l[0, 0:8, :] for l in lses]))
    attn, lse, mixed, h1 = _mix_fwd(outs, lses, u, z, x, small["ln_g"], small["ln_b"], small["sgu_w"],
                                    small["bias_t"], small["attn_out_g"], small["gmlp_out_g"], wout)
    hn2, rf, dh2, loss, dgf = _mlp_fwd(h1, small["norm2_g"], wff1, wff2, small["final_norm_g"], target)
    df, dh1, dg2 = _mlp_bwd(dh2, rf, h1, small["norm2_g"], wff1, wff2)
    gwff1 = _wgrad(hn2, df, "wgrad_ff1", D, 1024)
    gwff2 = _wgrad(rf, dh2, "wgrad_ff2", 1024, D, square_a=True)
    gwout = _wgrad(mixed, dh1, "wgrad_out", D, D)
    ga, g1 = small["attn_out_g"], small["norm1_g"]
    pin = early_grads(gwff1, gwff2, gwout) if early_grads else None
    if pin is not None:
        ga = ga + pin
    (do, delta, du, dz, dga, dgg, dlng, dlnb, dws, db) = _mix_bwd(
        dh1, attn, u, z, small["ln_g"], small["ln_b"], small["sgu_w"], small["sgu_wt"], small["bias_t"],
        ga, small["gmlp_out_g"], wout)
    dqs, dks, dvs = [], [], []
    for i, dil in enumerate(DILS):
        dqs.append(_attn_bwd_dq(q[i], k[i], v[i], do[i], lse[i], delta[i], slopes, dil))
        dk, dv = _attn_bwd_dkv(q[i], k[i], v[i], do[i], lse[i], delta[i], slopes, dil)
        dks.append(dk)
        dvs.append(dv)
    marker = functools.reduce(lambda a, b: a + b, [t[0, 0:8, 0:LANES] for t in dqs + dks + dvs])
    pin = after_attention_bwd(marker) if after_attention_bwd else None
    dproj = _dproj_merge(dqs, dks, dvs, du, dz, jnp.zeros((1, 1), F32) if pin is None else pin)
    gwin_t, gwin_low = _wgrad(dproj, hn1, "wgrad_in", INW // 2, D, also_bf16=True)
    partial = dict(ln_g=dlng, ln_b=dlnb, sgu_w=dws, sgu_b=db[:, :NG].T, attn_out_g=dga, gmlp_out_g=dgg,
                   norm2_g=dg2, final_norm_g=dgf)
    pin = late_grads(gwin_t, gwin_low, partial, loss[0, 0]) if late_grads else None
    if pin is not None:
        g1 = g1 + pin
    dx, dg1 = _inproj_bwd(dproj, dh1, x, g1, win_t)
    small_grads = dict(partial, norm1_g=dg1)
    return loss[0, 0], dx, small_grads, (gwin_t, gwout, gwff1, gwff2)


ANY = pl.BlockSpec(memory_space=pl.ANY)
NDEV = 8


def _position():
    return lax.axis_index("x"), lax.axis_index("y"), lax.axis_index("c")


def _other_chips(x, y):
    return [(1 - x, y), (x, 1 - y), (1 - x, 1 - y)]


def _remote(src, dst, send_sem, recv_sem, device):
    return pltpu.make_async_remote_copy(src_ref=src, dst_ref=dst, send_sem=send_sem, recv_sem=recv_sem,
                                        device_id=device, device_id_type=MESH)


HBM = pl.BlockSpec(memory_space=pltpu.HBM)
SEM = pl.BlockSpec(memory_space=pltpu.SEMAPHORE)
DATAFLOW = pltpu.SideEffectType.DATAFLOW_SIDE_EFFECTING


def _in_hbm(a):
    return pltpu.with_memory_space_constraint(a, pltpu.HBM)


def _gather_start(shards, name):
    n = len(shards)
    lands = [jnp.broadcast_to(sh[None], (NCHIP,) + sh.shape) for sh in shards]

    def body(*refs):
        w_refs, land_refs = refs[:n], refs[n:2 * n]
        send_sems, recv_sems = refs[2 * n:2 * n + 2]
        token = refs[-1]
        x, y, c = _position()
        for w in range(n):
            for k, (px, py) in enumerate(_other_chips(x, y)):
                m = 3 * w + k
                _remote(w_refs[w], land_refs[w].at[2 * x + y], send_sems.at[m], recv_sems.at[m], (px, py, c)).start()
        token[...] = jnp.zeros_like(token)

    res = _split_call(body, name, list(shards) + lands, (3 * n, 3 * n), (TOKEN,))
    return res[0], res[1], res[2:2 + n], res[2 + n:2 + 2 * n], res[-1]


def _gather_wait(send_sems, recv_sems, shards, lands, after, name):
    n = len(shards)

    def body(*refs):
        w_refs, land_refs = refs[:n], refs[n:2 * n]
        send_sems, recv_sems = refs[2 * n:2 * n + 2]
        x, y, c = _position()
        for w in range(n):
            for k, (px, py) in enumerate(_other_chips(x, y)):
                m = 3 * w + k
                cp = _remote(w_refs[w], land_refs[w].at[2 * px + py], send_sems.at[m], recv_sems.at[m], (px, py, c))
                cp.wait_send()
                cp.wait_recv()

    operands = list(shards) + list(lands)
    res = pl.pallas_call(
        body, name=name, out_shape=tuple(pltpu.HBM(a.shape, a.dtype) for a in operands),
        in_specs=(HBM,) * (2 * n) + (SEM, SEM, ANY), out_specs=(HBM,) * (2 * n),
        input_output_aliases={i: i for i in range(2 * n)},
        compiler_params=pltpu.CompilerParams(has_side_effects=DATAFLOW),
    )(*operands, send_sems, recv_sems, after)
    return res[n:]


def _xor_peers(x, y, c):
    peers = []
    for k in range(1, NDEV):
        kx, ky, kc = (k >> 2) & 1, (k >> 1) & 1, k & 1
        peers.append((1 - x if kx else x, 1 - y if ky else y, 1 - c if kc else c))
    return peers


def _piece(part_ref, px, py, pc):
    slab = 2 * px + py
    if len(part_ref.shape) == 3:
        half = part_ref.shape[1] // 2
        return part_ref.at[slab, pl.ds(pc * half, half), :]
    half = part_ref.shape[0] // 2
    return part_ref.at[pl.ds(pc * half, half), pl.ds(pl.multiple_of(slab * D, D), D)]


def _split_call(body, name, operands, n_sems, extra_out=()):
    n = len(operands)
    sems = tuple(pltpu.SemaphoreType.DMA((m,)) for m in n_sems)
    thru = tuple(pltpu.HBM(a.shape, a.dtype) for a in operands)
    return pl.pallas_call(
        body, name=name, out_shape=sems + thru + tuple(extra_out),
        in_specs=(HBM,) * n,
        out_specs=(SEM,) * len(sems) + (HBM,) * n + (pl.BlockSpec(memory_space=pltpu.VMEM),) * len(extra_out),
        input_output_aliases={i: len(sems) + i for i in range(n)},
        compiler_params=pltpu.CompilerParams(has_side_effects=DATAFLOW),
    )(*[_in_hbm(a) for a in operands])


TOKEN = jax.ShapeDtypeStruct((8, LANES), F32)


def _pack_copies(pack_ref, land_ref, send_sems, recv_sems, base, position, start):
    x, y, c = position
    for k, (px, py, pc) in enumerate(_xor_peers(x, y, c)):
        if start:
            _remote(pack_ref, land_ref.at[4 * x + 2 * y + c], send_sems.at[base + k], recv_sems.at[base + k],
                    (px, py, pc)).start()
        else:
            cp = _remote(pack_ref, land_ref.at[4 * px + 2 * py + pc], send_sems.at[base + k], recv_sems.at[base + k],
                         (px, py, pc))
            cp.wait_send()
            cp.wait_recv()


def _pack_landing(pack):
    return jnp.broadcast_to(pack[None], (NDEV,) + pack.shape)


def _reduce_start(parts, name, pack=None):
    nw = len(parts)
    lands = [lax.empty((NDEV - 1, p.shape[-2] // 2, D), p.dtype) for p in parts]
    operands = list(parts) + lands + ([pack, _pack_landing(pack)] if pack is not None else [])
    nops = len(operands)

    def body(*refs):
        part_refs, land_refs = refs[:nw], refs[nw:2 * nw]
        send_sems, recv_sems = refs[nops:nops + 2]
        token = refs[-1]
        x, y, c = _position()
        for w in range(nw):
            for k, peer in enumerate(_xor_peers(x, y, c)):
                n = w * (NDEV - 1) + k
                _remote(_piece(part_refs[w], *peer), land_refs[w].at[k], send_sems.at[n], recv_sems.at[n],
                        peer).start()
        if pack is not None:
            _pack_copies(refs[2 * nw], refs[2 * nw + 1], send_sems, recv_sems, nw * (NDEV - 1), (x, y, c), True)
        token[...] = jnp.zeros_like(token)

    n = (nw + (pack is not None)) * (NDEV - 1)
    res = _split_call(body, name, operands, (n, n), (TOKEN,))
    return res[0], res[1], res[2:2 + nops], res[-1]


def _reduce_wait(send_sems, recv_sems, operands, nw, after, name):
    nops = len(operands)
    has_pack = nops > 2 * nw

    def body(*refs):
        part_refs, land_refs = refs[:nw], refs[nw:2 * nw]
        send_sems, recv_sems = refs[nops:nops + 2]
        x, y, c = _position()
        for w in range(nw):
            for k, peer in enumerate(_xor_peers(x, y, c)):
                n = w * (NDEV - 1) + k
                cp = _remote(_piece(part_refs[w], *peer), land_refs[w].at[k], send_sems.at[n], recv_sems.at[n], peer)
                cp.wait_send()
                cp.wait_recv()
        if has_pack:
            _pack_copies(refs[2 * nw], refs[2 * nw + 1], send_sems, recv_sems, nw * (NDEV - 1), (x, y, c), False)

    res = pl.pallas_call(
        body, name=name, out_shape=tuple(pltpu.HBM(a.shape, a.dtype) for a in operands),
        in_specs=(HBM,) * nops + (SEM, SEM, ANY), out_specs=(HBM,) * nops,
        input_output_aliases={i: i for i in range(nops)},
        compiler_params=pltpu.CompilerParams(has_side_effects=DATAFLOW),
    )(*operands, send_sems, recv_sems, after)
    return res[:nw], res[nw:2 * nw], (res[2 * nw + 1] if has_pack else None)


def _sum_pieces(part, land, sel, name):
    half = part.shape[-2] // 2
    br = 128 if half % 128 == 0 else half // 2
    nb = half // br

    def body(sel_ref, own_ref, *refs):
        acc = own_ref[...]
        for r in refs[:NDEV - 1]:
            acc = acc + r[...].astype(F32)
        refs[NDEV - 1][...] = acc

    if part.ndim == 3:
        own_spec = pl.BlockSpec((None, br, D), lambda i, sel_ref: (sel_ref[0], sel_ref[1] * nb + i, 0))
    else:
        own_spec = pl.BlockSpec((br, D), lambda i, sel_ref: (sel_ref[1] * nb + i, sel_ref[0]))
    slot_specs = [pl.BlockSpec((None, br, D), functools.partial(lambda i, sel_ref, k: (k, i, 0), k=k))
                  for k in range(NDEV - 1)]
    return pl.pallas_call(
        body, name=name,
        grid_spec=pltpu.PrefetchScalarGridSpec(
            num_scalar_prefetch=1, grid=(nb,), in_specs=[own_spec] + slot_specs,
            out_specs=pl.BlockSpec((br, D), lambda i, sel_ref: (i, 0))),
        out_shape=jax.ShapeDtypeStruct((half, D), F32),
        compiler_params=_cparams("arbitrary"),
    )(sel, part, *([land] * (NDEV - 1)))


def _share_start(halves, name, pack=None):
    nw = len(halves)
    lands = [lax.empty(h.shape, F32) for h in halves]
    operands = list(halves) + lands + ([pack, _pack_landing(pack)] if pack is not None else [])
    nops = len(operands)

    def body(*refs):
        h_refs, land_refs = refs[:nw], refs[nw:2 * nw]
        send_sems, recv_sems = refs[nops:nops + 2]
        token = refs[-1]
        x, y, c = _position()
        for w in range(nw):
            _remote(h_refs[w], land_refs[w], send_sems.at[w], recv_sems.at[w], (x, y, 1 - c)).start()
        if pack is not None:
            _pack_copies(refs[2 * nw], refs[2 * nw + 1], send_sems, recv_sems, nw, (x, y, c), True)
        token[...] = jnp.zeros_like(token)

    n = nw + (NDEV - 1 if pack is not None else 0)
    res = _split_call(body, name, operands, (n, n), (TOKEN,))
    return res[0], res[1], res[2:2 + nops], res[-1]


def _share_wait(send_sems, recv_sems, operands, nw, after, name):
    nops = len(operands)
    has_pack = nops > 2 * nw

    def body(*refs):
        h_refs, land_refs = refs[:nw], refs[nw:2 * nw]
        send_sems, recv_sems = refs[nops:nops + 2]
        x, y, c = _position()
        for w in range(nw):
            cp = _remote(h_refs[w], land_refs[w], send_sems.at[w], recv_sems.at[w], (x, y, 1 - c))
            cp.wait_send()
            cp.wait_recv()
        if has_pack:
            _pack_copies(refs[2 * nw], refs[2 * nw + 1], send_sems, recv_sems, nw, (x, y, c), False)

    res = pl.pallas_call(
        body, name=name, out_shape=tuple(pltpu.HBM(a.shape, a.dtype) for a in operands),
        in_specs=(HBM,) * nops + (SEM, SEM, ANY), out_specs=(HBM,) * nops,
        input_output_aliases={i: i for i in range(nops)},
        compiler_params=pltpu.CompilerParams(has_side_effects=DATAFLOW),
    )(*operands, send_sems, recv_sems, after)
    return res[:nw], res[nw:2 * nw], (res[2 * nw + 1] if has_pack else None)


def _join_halves(own, other, c):
    first = jnp.where(c == 0, own, other)
    second = jnp.where(c == 0, other, own)
    return jnp.concatenate([first, second], axis=0)


SMALL_SIZES = (("norm1_g", D), ("sgu_ln_g", GW), ("sgu_ln_b", GW), ("sgu_w", NG * CHUNK * CHUNK),
               ("sgu_b", NG * CHUNK), ("attn_out_g", A), ("gmlp_out_g", GW), ("norm2_g", D),
               ("final_norm_g", D))
PARAM_ROWS = sum(n for _, n in SMALL_SIZES) // LANES
SMALL_ROWS = PARAM_ROWS + 8


def _pack_small(tree, first_extra=None):
    extra = jnp.zeros((8 * LANES,), F32)
    if first_extra is not None:
        extra = extra.at[0].set(first_extra)
    flat = jnp.concatenate([tree[n].reshape(-1) for n, _ in SMALL_SIZES] + [extra])
    return flat.reshape(SMALL_ROWS, LANES)


def _unpack_small(pack, shapes):
    flat = pack.reshape(-1)
    out, off = {}, 0
    for n, size in SMALL_SIZES:
        out[n] = flat[off:off + size].reshape(shapes[n])
        off += size
    return out


def _small_finish(pack_land, norm_land, wpack, mpack, vpack):
    def body(p_ref, n_ref, w_ref, m_ref, v_ref, go_ref, d_ref, mo_ref, vo_ref):
        total = p_ref[0]
        late = n_ref[0]
        for k in range(1, NDEV):
            total = total + p_ref[k]
            late = late + n_ref[k]
        go_ref[...] = total
        go_ref[0:8, :] = total[0:8, :] + late
        d, mn, vn = _adamw_math(w_ref[...], go_ref[...], m_ref[...], v_ref[...])
        d_ref[...] = d
        mo_ref[...] = mn
        vo_ref[...] = vn

    sd = jax.ShapeDtypeStruct((SMALL_ROWS, LANES), F32)
    vm = pl.BlockSpec(memory_space=pltpu.VMEM)
    return pl.pallas_call(
        body, name="small_finish", in_specs=[vm] * 5, out_specs=[vm] * 4, out_shape=[sd] * 4,
        compiler_params=_cparams(),
    )(pack_land, norm_land, wpack, mpack, vpack)


def kernel(x, norm1_g, w_in, sgu_ln_g, sgu_ln_b, sgu_w, sgu_b, attn_out_g, gmlp_out_g, w_out, norm2_g, w_ff1, w_ff2, final_norm_g, loss_target, m_norm1_g, m_w_in, m_sgu_ln_g, m_sgu_ln_b, m_sgu_w, m_sgu_b, m_attn_out_g, m_gmlp_out_g, m_w_out, m_norm2_g, m_w_ff1, m_w_ff2, m_final_norm_g, v_norm1_g, v_w_in, v_sgu_ln_g, v_sgu_ln_b, v_sgu_w, v_sgu_b, v_attn_out_g, v_gmlp_out_g, v_w_out, v_norm2_g, v_w_ff1, v_w_ff2, v_final_norm_g):
    names = [n for n, _ in SMALL_SIZES]
    w_small = dict(norm1_g=norm1_g, sgu_ln_g=sgu_ln_g, sgu_ln_b=sgu_ln_b, sgu_w=sgu_w, sgu_b=sgu_b,
                   attn_out_g=attn_out_g, gmlp_out_g=gmlp_out_g, norm2_g=norm2_g, final_norm_g=final_norm_g)
    m_small = dict(norm1_g=m_norm1_g, sgu_ln_g=m_sgu_ln_g, sgu_ln_b=m_sgu_ln_b, sgu_w=m_sgu_w, sgu_b=m_sgu_b,
                   attn_out_g=m_attn_out_g, gmlp_out_g=m_gmlp_out_g, norm2_g=m_norm2_g,
                   final_norm_g=m_final_norm_g)
    v_small = dict(norm1_g=v_norm1_g, sgu_ln_g=v_sgu_ln_g, sgu_ln_b=v_sgu_ln_b, sgu_w=v_sgu_w, sgu_b=v_sgu_b,
                   attn_out_g=v_attn_out_g, gmlp_out_g=v_gmlp_out_g, norm2_g=v_norm2_g,
                   final_norm_g=v_final_norm_g)
    shapes = {n: w_small[n].shape for n in names}

    start_in = _gather_start([w_in[0].T.astype(BF16)], "gather_in_start")
    issued = start_in[4][0:1, 0:1]
    start_rest = _gather_start([(w_out[0] + issued).astype(BF16), w_ff1[0].astype(BF16), w_ff2[0].astype(BF16)],
                               "gather_rest_start")
    hn1 = _norm1(x[0], norm1_g + start_rest[4][0:1, 0:1])
    win_t = _gather_wait(*start_in[:4], after=hn1, name="gather_in_wait")[0].reshape(INW, D)

    def rest_weights(after):
        wout, wff1, wff2 = _gather_wait(*start_rest[:4], after=after, name="gather_rest_wait")
        return wout.reshape(D, D), wff1, wff2.reshape(DFF, D)

    small = dict(
        norm1_g=norm1_g, ln_g=sgu_ln_g.reshape(1, GW), ln_b=sgu_ln_b.reshape(1, GW), sgu_w=sgu_w[0],
        sgu_wt=jnp.swapaxes(sgu_w[0], 1, 2), bias_t=jnp.repeat(sgu_b[0].T, DH, axis=1),
        attn_out_g=attn_out_g, gmlp_out_g=gmlp_out_g, norm2_g=norm2_g, final_norm_g=final_norm_g.reshape(1, D))
    xi, yi, ci = _position()
    sel = jnp.stack([2 * xi + yi, ci]).astype(jnp.int32)
    state = {}

    def as_slabs(g):
        return g.reshape(NCHIP, g.shape[0] // NCHIP, D)

    def early_grads(gwff1, gwff2, gwout):
        state["early"] = _reduce_start([gwff1, as_slabs(gwff2), as_slabs(gwout)], "reduce_early_start")
        return state["early"][3][0:1, 0:1]

    def after_attention_bwd(marker):
        send_sems, recv_sems, operands, _ = state["early"]
        parts, lands, _ = _reduce_wait(send_sems, recv_sems, operands, 3, marker, "reduce_early_wait")
        halves = [_sum_pieces(p, l, sel, "sum_" + n) for p, l, n in zip(parts, lands, ("w_ff1", "w_ff2", "w_out"))]
        state["early_share"] = _share_start(halves, "share_early_start")
        return state["early_share"][3][0:1, 0:1]

    def late_grads(gwin_t, gwin_low, partial, loss_part):
        pack = _pack_small(dict(partial, norm1_g=jnp.zeros((1, D), F32), sgu_ln_g=partial["ln_g"],
                                sgu_ln_b=partial["ln_b"]), loss_part)
        state["late"] = _reduce_start([as_slabs(gwin_low)], "reduce_late_start", pack)
        state["late_own"] = as_slabs(gwin_t)
        return state["late"][3][0:1, 0:1]

    _, dx, sg, _ = _local_step(
        x[0], hn1, loss_target[0], small, win_t, rest_weights, early_grads, after_attention_bwd, late_grads)
    send_sems, recv_sems, operands, _ = state["early_share"]
    own, other, _ = _share_wait(send_sems, recv_sems, operands, 3, dx, "share_early_wait")
    send_sems, recv_sems, operands, _ = state["late"]
    _, late_lands, pack_land = _reduce_wait(send_sems, recv_sems, operands, 1, dx, "reduce_late_wait")
    late_share = _share_start([_sum_pieces(state["late_own"], late_lands[0], sel, "sum_w_in")], "share_late_start",
                              sg["norm1_g"].reshape(8, LANES))
    issued = late_share[3][0:1, 0:1]
    g_big = {n: _join_halves(o, t, ci) + issued for n, o, t in zip(("w_ff1", "w_ff2", "w_out"), own, other)}
    w_big = dict(w_in=(w_in, m_w_in, v_w_in), w_out=(w_out, m_w_out, v_w_out),
                 w_ff1=(w_ff1, m_w_ff1, v_w_ff1), w_ff2=(w_ff2, m_w_ff2, v_w_ff2))
    grads, deltas, new_m, new_v = {}, {}, {}, {}

    def update(n):
        w, m, v = w_big[n]
        d, mn, vn = _adamw(w[0], g_big[n], m[0], v[0], "adamw_" + n)
        grads[n], deltas[n], new_m[n], new_v[n] = g_big[n][None], d[None], mn[None], vn[None]

    for n in ("w_ff1", "w_ff2", "w_out"):
        update(n)
    updated = deltas["w_out"][0, 0:8, 0:LANES] + deltas["w_ff1"][0, 0:8, 0:LANES] + deltas["w_ff2"][0, 0:8, 0:LANES]
    own, other, norm_land = _share_wait(late_share[0], late_share[1], late_share[2], 1, updated, "share_late_wait")
    g_big["w_in"] = _join_halves(own[0], other[0], ci).T
    update("w_in")

    packs = _small_finish(pack_land, norm_land, _pack_small(w_small), _pack_small(m_small), _pack_small(v_small))
    loss = packs[0][PARAM_ROWS, 0]
    for tree, pack in zip((grads, deltas, new_m, new_v), packs):
        tree.update(_unpack_small(pack, shapes))

    order = ["norm1_g", "w_in", "sgu_ln_g", "sgu_ln_b", "sgu_w", "sgu_b", "attn_out_g", "gmlp_out_g", "w_out",
             "norm2_g", "w_ff1", "w_ff2", "final_norm_g"]
    return (loss, dx[None], *[grads[n] for n in order], *[deltas[n] for n in order],
            *[new_m[n] for n in order], *[new_v[n] for n in order])
```

```python
import functools
import math

import numpy as np
import jax
import jax.numpy as jnp
from jax import lax
from jax.experimental import pallas as pl
from jax.experimental.pallas import tpu as pltpu

F32 = jnp.float32
BF16 = jnp.bfloat16

D = 1024
NH = 12
DH = 64
A = NH * DH
NG = 4
GW = NG * DH
INW = 3 * A + 2 * GW
DFF = 4 * D
CHUNK = 128
PATTERNS = ((128, 1), (512, 4), (2048, 16))
EPS = 1e-6
SCALE = DH ** -0.5
NEG = -1e30

LR, B1, B2, AEPS, WD, STEP = 0.001, 0.9, 0.999, 1e-08, 0.01, 10

TM = 512
TMX = 512
ATT_ROWS = 4096
FF_CH = 1024
LANES = 128
NCHIP = 4
VMEM_LIMIT = 56 * 1024 * 1024
MESH = pl.DeviceIdType.MESH


def _cparams(*sem, **kw):
    return pltpu.CompilerParams(dimension_semantics=sem if sem else None,
                                vmem_limit_bytes=VMEM_LIMIT, **kw)


def _dot(a, b):
    return jnp.dot(a, b, preferred_element_type=F32)


def _dot_nt(a, b):
    return lax.dot_general(a, b, (((1,), (1,)), ((), ())), preferred_element_type=F32)


def _dot_tn(a, b):
    return lax.dot_general(a, b, (((0,), (0,)), ((), ())), preferred_element_type=F32)


def _dot_hi(a, b):
    return jnp.dot(a, b, preferred_element_type=F32, precision=lax.Precision.HIGHEST)


def _alibi_slopes(n):
    def pow2(m):
        start = 2.0 ** (-8.0 / m)
        return [start ** (i + 1) for i in range(m)]
    if math.log2(n).is_integer():
        s = pow2(n)
    else:
        c = 2 ** int(math.floor(math.log2(n)))
        s = pow2(c) + pow2(2 * c)[0::2][: n - c]
    return np.asarray(s, dtype=np.float32)


def _rms_fwd(v, g):
    r = lax.rsqrt(jnp.mean(v * v, axis=-1, keepdims=True) + EPS)
    vn = v * r
    return vn * g, vn, r


def _rms_bwd(dy, vn, r, g):
    w = dy * g
    dv = r * (w - vn * jnp.mean(w * vn, axis=-1, keepdims=True))
    return dv, jnp.sum(dy * vn, axis=0, keepdims=True)


_K0 = math.sqrt(2.0 / math.pi)
_K1 = 0.044715


def _gelu(v):
    return 0.5 * v * (1.0 + jnp.tanh(_K0 * (v + _K1 * (v * v * v))))


def _gelu_grad(v):
    t = jnp.tanh(_K0 * (v + _K1 * (v * v * v)))
    return 0.5 * (1.0 + t) + 0.5 * v * (1.0 - t * t) * (_K0 * (1.0 + 3.0 * _K1 * v * v))


def _row_spec(rows, cols):
    return pl.BlockSpec((rows, cols), lambda i: (i, 0))


def _const_spec(shape):
    nd = len(shape)
    return pl.BlockSpec(shape, lambda i: (0,) * nd, pipeline_mode=pl.Buffered(1))


DILS = tuple(d for _, d in PATTERNS)


def _fill_cols(scr, value):
    for cb in range(value.shape[1] // LANES):
        scr[cb] = value[:, cb * LANES:(cb + 1) * LANES]


def _split_residues(scr, out_ref, dil):
    nb, rows, _ = scr.shape
    for r in range(dil):
        for cb in range(nb):
            piece = scr.at[cb][pl.ds(r, rows // dil, stride=dil), :]
            out_ref[r, :, cb * LANES:(cb + 1) * LANES] = piece.astype(out_ref.dtype)


def _merge_residues(in_ref, scr, dil):
    nb, rows, _ = scr.shape
    for r in range(dil):
        for cb in range(nb):
            scr.at[cb][pl.ds(r, rows // dil, stride=dil), :] = in_ref[r, :, cb * LANES:(cb + 1) * LANES].astype(F32)
    return jnp.concatenate([scr[cb] for cb in range(nb)], axis=-1)


def _col_scratch(rows, width):
    return pltpu.VMEM((width // LANES, rows, LANES), F32)


def _res_spec(dil, rows, width):
    return pl.BlockSpec((dil, rows // dil, width), lambda i: (0, i, 0))


def _res_shape(s, dil, width, dtype):
    return jax.ShapeDtypeStruct((dil, s // dil, width), dtype)


def _norm1(x, g1):
    s = x.shape[0]

    def body(x_ref, g_ref, hn_ref):
        hn, _, _ = _rms_fwd(x_ref[...], g_ref[...])
        hn_ref[...] = hn.astype(BF16)

    return pl.pallas_call(
        body, name="norm1", grid=(s // TM,), in_specs=[_row_spec(TM, D), _const_spec((1, D))],
        out_specs=_row_spec(TM, D), out_shape=jax.ShapeDtypeStruct((s, D), BF16),
        compiler_params=_cparams("arbitrary"),
    )(x, g1)


def _inproj_fwd(hn1, win_t):
    s = hn1.shape[0]
    nd = len(DILS)

    def body(hn_ref, w_ref, *rest):
        qkv_refs = rest[:3 * nd]
        u_ref, z_ref, scr = rest[3 * nd:]
        hn = hn_ref[...]
        for t in range(3):
            seg = _dot_nt(hn, w_ref[t * A:(t + 1) * A, :])
            seg = seg * SCALE if t == 0 else seg
            _fill_cols(scr, seg)
            for di, dil in enumerate(DILS):
                if dil == 1:
                    qkv_refs[t * nd + di][0] = seg.astype(BF16)
                else:
                    _split_residues(scr, qkv_refs[t * nd + di], dil)
        u_ref[...] = _dot_nt(hn, w_ref[3 * A:3 * A + GW, :])
        z_ref[...] = _dot_nt(hn, w_ref[3 * A + GW:INW, :])

    res = pl.pallas_call(
        body, name="inproj_fwd", grid=(s // TM,),
        in_specs=[_row_spec(TM, D), _const_spec((INW, D))],
        out_specs=[_res_spec(d, TM, A) for _ in range(3) for d in DILS] + [_row_spec(TM, GW), _row_spec(TM, GW)],
        out_shape=[_res_shape(s, d, A, BF16) for _ in range(3) for d in DILS]
                  + [jax.ShapeDtypeStruct((s, GW), F32)] * 2,
        scratch_shapes=[_col_scratch(TM, A)],
        compiler_params=_cparams("arbitrary"),
    )(hn1, win_t)
    q, k, v = (res[t * nd:(t + 1) * nd] for t in range(3))
    return q, k, v, res[-2], res[-1]


def _att_geometry(length, dil):
    merge = max(1, min(dil, ATT_ROWS // length))
    rows = min(length * merge, ATT_ROWS)
    nsub = rows // CHUNK
    return merge, rows, length * merge // rows, nsub, min(length // CHUNK, nsub)


def _merged(t, merge):
    return t.reshape(t.shape[0] // merge, t.shape[1] * merge, t.shape[2])


def _stack_heads(t):
    lane = lax.broadcasted_iota(jnp.int32, t.shape, 1)
    zero = jnp.zeros_like(t)
    return jnp.concatenate([jnp.where(lane < DH, t, zero), jnp.where(lane >= DH, t, zero)], axis=0)


def _head_cols(t, hp):
    lane = lax.broadcasted_iota(jnp.int32, t.shape, 1)
    cols = [jnp.sum(jnp.where(lane == 2 * hp + h, t, 0.0), axis=-1, keepdims=True) for h in range(2)]
    return jnp.concatenate(cols, axis=0)


def _unstack_heads(t2):
    n = t2.shape[0] // 2
    lane = lax.broadcasted_iota(jnp.int32, (n, LANES), 1)
    return jnp.where(lane < DH, t2[:n], t2[n:])


def _query_window_bias(s0, s1, dil, first):
    row = lax.broadcasted_iota(jnp.int32, (2 * CHUNK, 2 * CHUNK), 0)
    col = lax.broadcasted_iota(jnp.int32, (2 * CHUNK, 2 * CHUNK), 1)
    steps = (row & (CHUNK - 1)) + CHUNK - col
    valid = (steps >= 0) & (steps <= CHUNK)
    if first:
        valid = valid & (col >= CHUNK)
    slope = jnp.where(row < CHUNK, s0, s1)
    return jnp.where(valid, -slope * (steps * dil).astype(F32), NEG)


def _key_block_bias(s0, s1, dil, last):
    key = lax.broadcasted_iota(jnp.int32, (CHUNK, 4 * CHUNK), 0)
    col = lax.broadcasted_iota(jnp.int32, (CHUNK, 4 * CHUNK), 1)
    wq = col & (2 * CHUNK - 1)
    steps = wq - key
    valid = (steps >= 0) & (steps <= CHUNK)
    if last:
        valid = valid & (wq < CHUNK)
    slope = jnp.where(col < 2 * CHUNK, s0, s1)
    return jnp.where(valid, -slope * (steps * dil).astype(F32), NEG)


def _head_rows(t, hp):
    row = lax.broadcasted_iota(jnp.int32, (8, LANES), 0)
    lane = lax.broadcasted_iota(jnp.int32, (8, LANES), 1)
    pick = jnp.where((row < 2) & (lane == 2 * hp + row), 1.0, 0.0).astype(BF16)
    hi = t.astype(BF16)
    rest = t - hi.astype(F32)
    mid = rest.astype(BF16)
    low = (rest - mid.astype(F32)).astype(BF16)
    return _dot_nt(pick, hi) + _dot_nt(pick, mid) + _dot_nt(pick, low)


def _att_specs(dil, rows, nsub, nblk):
    main = pl.BlockSpec((None, rows, LANES), lambda r, c, hp: (r, c, hp))
    prev = pl.BlockSpec((None, CHUNK, LANES), lambda r, c, hp: (r, jnp.maximum(c * nsub - 1, 0), hp))
    nxt = pl.BlockSpec((None, CHUNK, LANES), lambda r, c, hp: (r, jnp.minimum((c + 1) * nsub, nblk - 1), hp))
    main_heads = pl.BlockSpec((None, rows, LANES), lambda r, c, hp: (r, c, 0))
    nxt_heads = pl.BlockSpec((None, CHUNK, LANES), lambda r, c, hp: (r, jnp.minimum((c + 1) * nsub, nblk - 1), 0))
    return main, prev, nxt, main_heads, nxt_heads


def _row_start(i):
    return i * CHUNK if isinstance(i, int) else pl.multiple_of(i * CHUNK, CHUNK)


def _first_blocks(block, nsub, seg, nch, ch, first_bias, bias_buf):
    for i in range(nsub):
        if i % seg:
            block(i, bias_buf[...])
        elif nch == 1:
            block(i, first_bias())
        else:
            block(i, jnp.where(ch == 0, first_bias(), bias_buf[...]))


def _last_blocks(block, nsub, seg, nch, ch, last_bias, bias_buf):
    for i in range(nsub):
        if (i + 1) % seg:
            block(i, bias_buf[...])
        elif nch == 1:
            block(i, last_bias())
        else:
            block(i, jnp.where(ch == nch - 1, last_bias(), bias_buf[...]))


def _attn_fwd(q, k, v, slopes, dil):
    length = q.shape[1]
    merge, rows, nch, nsub, seg = _att_geometry(length, dil)
    main, prev, _, main_heads, _ = _att_specs(dil, rows, nsub, length * merge // CHUNK)
    q, k, v = (_merged(t, merge) for t in (q, k, v))

    def body(sl_ref, q_ref, k_ref, v_ref, kh_ref, vh_ref, o_ref, lse_ref, kbuf, vbuf, bias_buf):
        ch = pl.program_id(1)
        hp = pl.program_id(2)
        lane = lax.broadcasted_iota(jnp.int32, (CHUNK, LANES), 1)
        kbuf[0:CHUNK, :] = kh_ref[...]
        kbuf[CHUNK:, :] = k_ref[...]
        vbuf[0:CHUNK, :] = vh_ref[...]
        vbuf[CHUNK:, :] = v_ref[...]
        s0, s1 = sl_ref[2 * hp], sl_ref[2 * hp + 1]

        def block(i, bias):
            row = _row_start(i)
            rs = pl.ds(row, CHUNK)
            q2 = _stack_heads(q_ref[rs, :])
            kw = kbuf[pl.ds(row, 2 * CHUNK), :]
            vw = vbuf[pl.ds(row, 2 * CHUNK), :]
            sc = _dot_nt(q2, kw) + bias
            m = jnp.max(sc, axis=-1, keepdims=True)
            p = jnp.exp(sc - m)
            l = jnp.sum(p, axis=-1, keepdims=True)
            o2 = _dot(p.astype(BF16), vw) * (1.0 / l)
            o_ref[rs, :] = _unstack_heads(o2).astype(BF16)
            lse = m + jnp.log(l)
            seen = jnp.where(hp == 0, 0.0, lse_ref[rs, :])
            lse_ref[rs, :] = jnp.where(lane == 2 * hp, lse[:CHUNK], jnp.where(lane == 2 * hp + 1, lse[CHUNK:], seen))

        bias_buf[...] = _query_window_bias(s0, s1, dil, False)
        _first_blocks(block, nsub, seg, nch, ch, lambda: _query_window_bias(s0, s1, dil, True), bias_buf)

    sd = jax.ShapeDtypeStruct
    o, lse = pl.pallas_call(
        body, name=f"attn_fwd_d{dil}", grid=(dil // merge, nch, NH // 2),
        in_specs=[pl.BlockSpec(memory_space=pltpu.SMEM), main, main, main, prev, prev],
        out_specs=[main, main_heads],
        out_shape=[sd((dil // merge, length * merge, A), BF16), sd((dil // merge, length * merge, LANES), F32)],
        scratch_shapes=[pltpu.VMEM((rows + CHUNK, LANES), BF16), pltpu.VMEM((rows + CHUNK, LANES), BF16),
                        pltpu.VMEM((2 * CHUNK, 2 * CHUNK), F32)],
        compiler_params=_cparams("arbitrary", "arbitrary", "arbitrary"),
    )(slopes, q, k, v, k, v)
    return o.reshape(dil, length, A), lse.reshape(dil, length, LANES)


def _attn_bwd_dq(q, k, v, do, lse, delta, slopes, dil):
    length = q.shape[1]
    merge, rows, nch, nsub, seg = _att_geometry(length, dil)
    main, prev, _, main_heads, _ = _att_specs(dil, rows, nsub, length * merge // CHUNK)
    q, k, v, do, lse, delta = (_merged(t, merge) for t in (q, k, v, do, lse, delta))

    def body(sl_ref, q_ref, k_ref, v_ref, do_ref, lse_ref, dl_ref, kh_ref, vh_ref, dq_ref, kbuf, vbuf, bias_buf):
        ch = pl.program_id(1)
        hp = pl.program_id(2)
        kbuf[0:CHUNK, :] = kh_ref[...]
        kbuf[CHUNK:, :] = k_ref[...]
        vbuf[0:CHUNK, :] = vh_ref[...]
        vbuf[CHUNK:, :] = v_ref[...]
        s0, s1 = sl_ref[2 * hp], sl_ref[2 * hp + 1]

        def block(i, bias):
            row = _row_start(i)
            rs = pl.ds(row, CHUNK)
            q2 = _stack_heads(q_ref[rs, :])
            do2 = _stack_heads(do_ref[rs, :])
            lse2 = _head_cols(lse_ref[rs, :], hp)
            dl2 = _head_cols(dl_ref[rs, :], hp)
            kw = kbuf[pl.ds(row, 2 * CHUNK), :]
            vw = vbuf[pl.ds(row, 2 * CHUNK), :]
            p = jnp.exp(_dot_nt(q2, kw) + bias - lse2)
            ds = p * (_dot_nt(do2, vw) - dl2)
            dq_ref[rs, :] = _unstack_heads(_dot(ds.astype(BF16), kw)).astype(BF16)

        bias_buf[...] = _query_window_bias(s0, s1, dil, False)
        _first_blocks(block, nsub, seg, nch, ch, lambda: _query_window_bias(s0, s1, dil, True), bias_buf)

    dq = pl.pallas_call(
        body, name=f"attn_dq_d{dil}", grid=(dil // merge, nch, NH // 2),
        in_specs=[pl.BlockSpec(memory_space=pltpu.SMEM), main, main, main, main, main_heads, main_heads, prev, prev],
        out_specs=main, out_shape=jax.ShapeDtypeStruct((dil // merge, length * merge, A), BF16),
        scratch_shapes=[pltpu.VMEM((rows + CHUNK, LANES), BF16), pltpu.VMEM((rows + CHUNK, LANES), BF16),
                        pltpu.VMEM((2 * CHUNK, 2 * CHUNK), F32)],
        compiler_params=_cparams("arbitrary", "arbitrary", "arbitrary"),
    )(slopes, q, k, v, do, lse, delta, k, v)
    return dq.reshape(dil, length, A)


def _attn_bwd_dkv(q, k, v, do, lse, delta, slopes, dil):
    length = q.shape[1]
    merge, rows, nch, nsub, seg = _att_geometry(length, dil)
    main, _, nxt, main_heads, nxt_heads = _att_specs(dil, rows, nsub, length * merge // CHUNK)
    q, k, v, do, lse, delta = (_merged(t, merge) for t in (q, k, v, do, lse, delta))

    def body(sl_ref, k_ref, v_ref, q_ref, do_ref, lse_ref, dl_ref, qh_ref, doh_ref, lseh_ref, dlh_ref,
             dk_ref, dv_ref, qbuf, dobuf, lse_rows, dl_rows, bias_buf):
        ch = pl.program_id(1)
        hp = pl.program_id(2)
        for buf, main_ref, halo_ref in ((qbuf, q_ref, qh_ref), (dobuf, do_ref, doh_ref)):
            buf[0:rows, :] = main_ref[...]
            buf[rows:, :] = halo_ref[...]
        for buf, main_ref, halo_ref in ((lse_rows, lse_ref, lseh_ref), (dl_rows, dl_ref, dlh_ref)):
            buf[:, 0:rows] = _head_rows(main_ref[...], hp)
            buf[:, rows:] = _head_rows(halo_ref[...], hp)
        s0, s1 = sl_ref[2 * hp], sl_ref[2 * hp + 1]

        def block(i, bias):
            row = _row_start(i)
            rs = pl.ds(row, CHUNK)
            win = pl.ds(row, 2 * CHUNK)
            kc = k_ref[rs, :]
            vc = v_ref[rs, :]
            q2 = _stack_heads(qbuf[win, :])
            do2 = _stack_heads(dobuf[win, :])
            cols = slice(i * CHUNK, (i + 2) * CHUNK)
            lse2 = jnp.concatenate([lse_rows[0:1, cols], lse_rows[1:2, cols]], axis=1)
            dl2 = jnp.concatenate([dl_rows[0:1, cols], dl_rows[1:2, cols]], axis=1)
            pt = jnp.exp(_dot_nt(kc, q2) + bias - lse2)
            dst = pt * (_dot_nt(vc, do2) - dl2)
            dv_ref[rs, :] = _dot(pt.astype(BF16), do2).astype(BF16)
            dk_ref[rs, :] = _dot(dst.astype(BF16), q2).astype(BF16)

        bias_buf[...] = _key_block_bias(s0, s1, dil, False)
        _last_blocks(block, nsub, seg, nch, ch, lambda: _key_block_bias(s0, s1, dil, True), bias_buf)

    sd = jax.ShapeDtypeStruct((dil // merge, length * merge, A), BF16)
    dk, dv = pl.pallas_call(
        body, name=f"attn_dkv_d{dil}", grid=(dil // merge, nch, NH // 2),
        in_specs=[pl.BlockSpec(memory_space=pltpu.SMEM), main, main, main, main, main_heads, main_heads,
                  nxt, nxt, nxt_heads, nxt_heads],
        out_specs=[main, main], out_shape=[sd, sd],
        scratch_shapes=[pltpu.VMEM((rows + CHUNK, LANES), BF16), pltpu.VMEM((rows + CHUNK, LANES), BF16),
                        pltpu.VMEM((8, rows + CHUNK), F32), pltpu.VMEM((8, rows + CHUNK), F32),
                        pltpu.VMEM((CHUNK, 4 * CHUNK), F32)],
        compiler_params=_cparams("arbitrary", "arbitrary", "arbitrary"),
    )(slopes, k, v, q, do, lse, delta, q, do, lse, delta)
    return dk.reshape(dil, length, A), dv.reshape(dil, length, A)


def _group_masks(width):
    lane = lax.broadcasted_iota(jnp.int32, (1, width), 1)
    return [(lane >= g * DH) & (lane < (g + 1) * DH) for g in range(width // DH)]


def _group_mean_matrix():
    i = lax.broadcasted_iota(jnp.int32, (GW, GW), 0) // DH
    j = lax.broadcasted_iota(jnp.int32, (GW, GW), 1) // DH
    return jnp.where(i == j, 1.0 / DH, 0.0).astype(F32)


def _tri_mask(lower):
    t = lax.broadcasted_iota(jnp.int32, (CHUNK, CHUNK), 0)
    u = lax.broadcasted_iota(jnp.int32, (CHUNK, CHUNK), 1)
    return (u <= t) if lower else (u >= t)


def _sgu_forward(u, z, lng, lnb, w_ref, bias_t, pmat, rows):
    ug = _gelu(u)
    zg = _gelu(z)
    mu = _dot_hi(zg, pmat)
    zc = zg - mu
    var = _dot_hi(zc * zc, pmat)
    rstd = lax.rsqrt(var + EPS)
    zhat = zc * rstd
    zn = (zhat * lng + lnb).astype(BF16)
    gm = _group_masks(GW)
    tri = _tri_mask(True)
    ws = [jnp.where(tri, w_ref[g], 0.0).astype(BF16) for g in range(NG)]
    pieces = []
    for c in range(rows // CHUNK):
        znc = zn[c * CHUNK:(c + 1) * CHUNK, :]
        mix = None
        for g in range(NG):
            part = jnp.where(gm[g], _dot(ws[g], znc), 0.0)
            mix = part if mix is None else mix + part
        pieces.append(mix + bias_t)
    mixed = jnp.concatenate(pieces, axis=0) if len(pieces) > 1 else pieces[0]
    return ug * mixed, ug, zhat, rstd, zn, mixed


def _head_spread():
    h = lax.broadcasted_iota(jnp.int32, (LANES, A), 0)
    lane = lax.broadcasted_iota(jnp.int32, (LANES, A), 1)
    return jnp.where(lane // DH == h, 1.0, 0.0).astype(BF16)


def _bf16_pieces(t, n):
    pieces = []
    for _ in range(n):
        piece = t.astype(BF16)
        pieces.append(piece)
        t = t - piece.astype(F32)
    return pieces


def _mix_fwd(os_, ls_, u, z, x, lng, lnb, sgu_w, bias_t, ga, gg, wout):
    s = x.shape[0]
    nd = len(DILS)
    nscr = sum(1 for d in DILS if d > 1)

    def body(*refs):
        o_refs, l_refs = refs[:nd], refs[nd:2 * nd]
        u_ref, z_ref, x_ref, lng_ref, lnb_ref, w_ref, bt_ref, ga_ref, gg_ref, wo_ref = refs[2 * nd:2 * nd + 10]
        attn_ref = refs[2 * nd + 10]
        lse_refs = refs[2 * nd + 11:3 * nd + 11]
        mixed_ref, h1_ref = refs[3 * nd + 11:3 * nd + 13]
        scr = refs[3 * nd + 13:]
        scr_o, scr_l, scr_lse = scr[:nscr], scr[nscr:2 * nscr], scr[2 * nscr]
        ov, lv, j = [], [], 0
        for di, dil in enumerate(DILS):
            if dil == 1:
                ov.append(o_refs[di][0].astype(F32))
                lv.append(l_refs[di][0])
            else:
                ov.append(_merge_residues(o_refs[di], scr_o[j], dil))
                lv.append(_merge_residues(l_refs[di], scr_l[j], dil))
                j += 1
        mx = functools.reduce(jnp.maximum, lv)
        es = [jnp.exp(l - mx) for l in lv]
        den = functools.reduce(lambda a, b: a + b, es)
        spread = _head_spread()
        attn = None
        for e, o in zip(es, ov):
            wide = functools.reduce(lambda a, b: a + b, [_dot(piece, spread) for piece in _bf16_pieces(e / den, 2)])
            attn = wide * o if attn is None else attn + wide * o
        attn_ref[...] = attn
        lse = mx + jnp.log(den)
        _fill_cols(scr_lse, lse)
        for di, dil in enumerate(DILS):
            if dil == 1:
                lse_refs[di][0] = lse
            else:
                _split_residues(scr_lse, lse_refs[di], dil)
        an, _, _ = _rms_fwd(attn, ga_ref[...])
        gmv, _, _, _, _, _ = _sgu_forward(u_ref[...], z_ref[...], lng_ref[...], lnb_ref[...], w_ref,
                                          bt_ref[...], _group_mean_matrix(), TMX)
        gn, _, _ = _rms_fwd(gmv, gg_ref[...])
        mixed = jnp.concatenate([an, gn], axis=-1).astype(BF16)
        mixed_ref[...] = mixed
        h1_ref[...] = x_ref[...] + _dot(mixed, wo_ref[...])

    sd = jax.ShapeDtypeStruct
    res = pl.pallas_call(
        body, name="mix_fwd", grid=(s // TMX,),
        in_specs=[_res_spec(d, TMX, A) for d in DILS] + [_res_spec(d, TMX, LANES) for d in DILS]
                 + [_row_spec(TMX, GW), _row_spec(TMX, GW),
                    _row_spec(TMX, D), _const_spec((1, GW)), _const_spec((1, GW)), _const_spec((NG, CHUNK, CHUNK)),
                    _const_spec((CHUNK, GW)), _const_spec((1, A)), _const_spec((1, GW)), _const_spec((D, D))],
        out_specs=[_row_spec(TMX, A)] + [_res_spec(d, TMX, LANES) for d in DILS]
                  + [_row_spec(TMX, D), _row_spec(TMX, D)],
        out_shape=[sd((s, A), F32)] + [_res_shape(s, d, LANES, F32) for d in DILS]
                  + [sd((s, D), BF16), sd((s, D), F32)],
        scratch_shapes=[_col_scratch(TMX, A)] * nscr + [_col_scratch(TMX, LANES)] * (nscr + 1),
        compiler_params=_cparams("arbitrary"),
    )(*os_, *ls_, u, z, x, lng, lnb, sgu_w, bias_t, ga, gg, wout)
    return res[0], res[1:1 + nd], res[1 + nd], res[2 + nd]


def _mlp_fwd(h1, g2, wff1, wff2, gf, target):
    s = h1.shape[0]

    def body(h1_ref, g2_ref, w1_ref, w2_ref, gf_ref, t_ref, hn_ref, rf_ref, dh2_ref, loss_ref, dgf_ref):
        i = pl.program_id(0)
        h1v = h1_ref[...]
        hn, _, _ = _rms_fwd(h1v, g2_ref[...])
        hn = hn.astype(BF16)
        hn_ref[...] = hn
        acc = h1v
        for j in range(DFF // FF_CH):
            cols = slice(j * FF_CH, (j + 1) * FF_CH)
            rf = jnp.maximum(_dot(hn, w1_ref[j]), 0.0)
            act = (rf * rf).astype(BF16)
            rf_ref[:, cols] = rf.astype(BF16)
            acc = acc + _dot(act, w2_ref[cols, :])
        y, h2n, r3 = _rms_fwd(acc, gf_ref[...])
        err = y - t_ref[...]
        part = 0.5 * jnp.sum(jnp.mean(err * err, axis=-1, keepdims=True), axis=0, keepdims=True)
        dy = err * (1.0 / D)
        dh2, dgf = _rms_bwd(dy, h2n, r3, gf_ref[...])
        dh2_ref[...] = dh2

        @pl.when(i == 0)
        def _():
            loss_ref[...] = jnp.zeros_like(loss_ref)
            dgf_ref[...] = jnp.zeros_like(dgf_ref)

        loss_ref[...] += jnp.broadcast_to(part, loss_ref.shape)
        dgf_ref[...] += dgf

    sd = jax.ShapeDtypeStruct
    return pl.pallas_call(
        body, name="mlp_fwd", grid=(s // TM,),
        in_specs=[_row_spec(TM, D), _const_spec((1, D)), _const_spec((DFF // FF_CH, D, FF_CH)), _const_spec((DFF, D)),
                  _const_spec((1, D)), _row_spec(TM, D)],
        out_specs=[_row_spec(TM, D), _row_spec(TM, DFF), _row_spec(TM, D),
                   _const_spec((1, LANES)), _const_spec((1, D))],
        out_shape=[sd((s, D), BF16), sd((s, DFF), BF16), sd((s, D), F32),
                   sd((1, LANES), F32), sd((1, D), F32)],
        compiler_params=_cparams("arbitrary"),
    )(h1, g2, wff1, wff2, gf, target)


def _mlp_bwd(dh2, rf, h1, g2, wff1, wff2):
    s = h1.shape[0]

    def body(dh2_ref, rf_ref, h1_ref, g2_ref, w1_ref, w2_ref, df_ref, dh1_ref, dg2_ref):
        i = pl.program_id(0)
        dh2v = dh2_ref[...]
        dh2b = dh2v.astype(BF16)
        dhn = jnp.zeros((TM, D), F32)
        for j in range(DFF // FF_CH):
            cols = slice(j * FF_CH, (j + 1) * FF_CH)
            da = _dot_nt(dh2b, w2_ref[cols, :])
            df = (da * (2.0 * rf_ref[:, cols].astype(F32))).astype(BF16)
            df_ref[:, cols] = df
            dhn = dhn + _dot_nt(df, w1_ref[j])
        _, h1n, r2 = _rms_fwd(h1_ref[...], g2_ref[...])
        dres, dg2 = _rms_bwd(dhn, h1n, r2, g2_ref[...])
        dh1_ref[...] = dh2v + dres

        @pl.when(i == 0)
        def _():
            dg2_ref[...] = jnp.zeros_like(dg2_ref)

        dg2_ref[...] += dg2

    sd = jax.ShapeDtypeStruct
    return pl.pallas_call(
        body, name="mlp_bwd", grid=(s // TM,),
        in_specs=[_row_spec(TM, D), _row_spec(TM, DFF), _row_spec(TM, D), _const_spec((1, D)),
                  _const_spec((DFF // FF_CH, D, FF_CH)), _const_spec((DFF, D))],
        out_specs=[_row_spec(TM, DFF), _row_spec(TM, D), _const_spec((1, D))],
        out_shape=[sd((s, DFF), BF16), sd((s, D), F32), sd((1, D), F32)],
        compiler_params=_cparams("arbitrary"),
    )(dh2, rf, h1, g2, wff1, wff2)


def _mix_bwd(dh1, attn, u, z, lng, lnb, sgu_w, sgu_wt, bias_t, ga, gg, wout):
    s = dh1.shape[0]
    nsteps = s // TMX
    nd = len(DILS)

    def body(*refs):
        dh1_ref, attn_ref, u_ref, z_ref, lng_ref, lnb_ref, w_ref, wt_ref, bt_ref, ga_ref, gg_ref, wo_ref = refs[:12]
        do_refs, dl_refs = refs[12:12 + nd], refs[12 + nd:12 + 2 * nd]
        (du_ref, dz_ref, dga_ref, dgg_ref, dlng_ref, dlnb_ref, dws_ref, db_ref,
         dbt_acc, scr_do, scr_dl) = refs[12 + 2 * nd:]
        i = pl.program_id(0)

        @pl.when(i == 0)
        def _():
            for r in (dga_ref, dgg_ref, dlng_ref, dlnb_ref, dws_ref, db_ref, dbt_acc):
                r[...] = jnp.zeros_like(r)

        dmixed = _dot_nt(dh1_ref[...].astype(BF16), wo_ref[...])
        attn = attn_ref[...]
        _, an, ra = _rms_fwd(attn, ga_ref[...])
        dattn, dga = _rms_bwd(dmixed[:, :A], an, ra, ga_ref[...])
        dga_ref[...] += dga
        _fill_cols(scr_do, dattn)
        spread = _head_spread()
        delta = functools.reduce(lambda a, b: a + b, [_dot_nt(piece, spread) for piece in _bf16_pieces(dattn * attn, 3)])
        _fill_cols(scr_dl, delta)
        for di, dil in enumerate(DILS):
            if dil == 1:
                do_refs[di][0] = dattn.astype(BF16)
                dl_refs[di][0] = delta
            else:
                _split_residues(scr_do, do_refs[di], dil)
                _split_residues(scr_dl, dl_refs[di], dil)
        pmat = _group_mean_matrix()
        lng = lng_ref[...]
        uv, zv = u_ref[...], z_ref[...]
        gmv, ug, zhat, rstd, zn, mixed = _sgu_forward(uv, zv, lng, lnb_ref[...], w_ref, bt_ref[...], pmat, TMX)
        _, gmn, rg = _rms_fwd(gmv, gg_ref[...])
        dgm, dgg = _rms_bwd(dmixed[:, A:], gmn, rg, gg_ref[...])
        dgg_ref[...] += dgg
        du_ref[...] = (dgm * mixed * _gelu_grad(uv)).astype(BF16)
        dmx = dgm * ug
        dmxb = dmx.astype(BF16)
        gm = _group_masks(GW)
        tri_t = _tri_mask(False)
        wst = [jnp.where(tri_t, wt_ref[g], 0.0).astype(BF16) for g in range(NG)]
        zero = jnp.zeros((CHUNK, GW), BF16)
        dzn_pieces = []
        for c in range(TMX // CHUNK):
            rs = slice(c * CHUNK, (c + 1) * CHUNK)
            dmc = dmxb[rs, :]
            znc = zn[rs, :]
            dbt_acc[...] += dmx[rs, :]
            dzn = None
            for g in range(NG):
                dws_ref[g] += _dot_nt(jnp.where(gm[g], dmc, zero), znc)
                part = jnp.where(gm[g], _dot(wst[g], dmc), 0.0)
                dzn = part if dzn is None else dzn + part
            dzn_pieces.append(dzn)
        dzn = jnp.concatenate(dzn_pieces, axis=0)
        dlng_ref[...] += jnp.sum(dzn * zhat, axis=0, keepdims=True)
        dlnb_ref[...] += jnp.sum(dzn, axis=0, keepdims=True)
        dzh = dzn * lng
        dzg = rstd * (dzh - _dot_hi(dzh, pmat) - zhat * _dot_hi(dzh * zhat, pmat))
        dz_ref[...] = (dzg * _gelu_grad(zv)).astype(BF16)

        @pl.when(i == nsteps - 1)
        def _():
            tri = _tri_mask(True)
            for g in range(NG):
                dws_ref[g] = jnp.where(tri, dws_ref[g], 0.0)
            acc = dbt_acc[...]
            lane = lax.broadcasted_iota(jnp.int32, (CHUNK, LANES), 1)
            out = jnp.zeros((CHUNK, LANES), F32)
            for g in range(NG):
                sg = jnp.sum(jnp.where(gm[g], acc, 0.0), axis=-1, keepdims=True)
                out = jnp.where(lane == g, sg, out)
            db_ref[...] = out

    sd = jax.ShapeDtypeStruct
    res = pl.pallas_call(
        body, name="mix_bwd", grid=(nsteps,),
        in_specs=[_row_spec(TMX, D), _row_spec(TMX, A), _row_spec(TMX, GW), _row_spec(TMX, GW),
                  _const_spec((1, GW)), _const_spec((1, GW)), _const_spec((NG, CHUNK, CHUNK)),
                  _const_spec((NG, CHUNK, CHUNK)), _const_spec((CHUNK, GW)), _const_spec((1, A)),
                  _const_spec((1, GW)), _const_spec((D, D))],
        out_specs=[_res_spec(d, TMX, A) for d in DILS] + [_res_spec(d, TMX, LANES) for d in DILS]
                  + [_row_spec(TMX, GW), _row_spec(TMX, GW),
                   _const_spec((1, A)), _const_spec((1, GW)), _const_spec((1, GW)), _const_spec((1, GW)),
                   _const_spec((NG, CHUNK, CHUNK)), _const_spec((CHUNK, LANES))],
        out_shape=[_res_shape(s, d, A, BF16) for d in DILS] + [_res_shape(s, d, LANES, F32) for d in DILS]
                  + [sd((s, GW), BF16), sd((s, GW), BF16),
                   sd((1, A), F32), sd((1, GW), F32), sd((1, GW), F32), sd((1, GW), F32),
                   sd((NG, CHUNK, CHUNK), F32), sd((CHUNK, LANES), F32)],
        scratch_shapes=[pltpu.VMEM((CHUNK, GW), F32), _col_scratch(TMX, A), _col_scratch(TMX, LANES)],
        compiler_params=_cparams("arbitrary"),
    )(dh1, attn, u, z, lng, lnb, sgu_w, sgu_wt, bias_t, ga, gg, wout)
    return (res[:nd], res[nd:2 * nd]) + tuple(res[2 * nd:])


def _dproj_merge(dqs, dks, dvs, du, dz, pin):
    s = du.shape[0]
    nd = len(DILS)
    nscr = sum(1 for d in DILS if d > 1)

    def body(*refs):
        pin_ref = refs[0]
        parts = [refs[1 + t * nd:1 + (t + 1) * nd] for t in range(3)]
        du_ref, dz_ref, dp_ref = refs[1 + 3 * nd:4 + 3 * nd]
        scr = refs[4 + 3 * nd:]
        sums = []
        for t in range(3):
            total, j = None, 0
            for di, dil in enumerate(DILS):
                if dil == 1:
                    term = parts[t][di][0].astype(F32)
                else:
                    term = _merge_residues(parts[t][di], scr[t * nscr + j], dil)
                    j += 1
                total = term if total is None else total + term
            sums.append(total)
        dp_ref[...] = jnp.concatenate([sums[0] * SCALE, sums[1], sums[2], du_ref[...].astype(F32) + pin_ref[0, 0],
                                       dz_ref[...].astype(F32)], axis=-1).astype(BF16)

    return pl.pallas_call(
        body, name="dproj_merge", grid=(s // TMX,),
        in_specs=[pl.BlockSpec(memory_space=pltpu.SMEM)] + [_res_spec(d, TMX, A) for d in DILS] * 3
                 + [_row_spec(TMX, GW)] * 2,
        out_specs=_row_spec(TMX, INW), out_shape=jax.ShapeDtypeStruct((s, INW), BF16),
        scratch_shapes=[_col_scratch(TMX, A)] * (3 * nscr),
        compiler_params=_cparams("arbitrary"),
    )(pin, *dqs, *dks, *dvs, du, dz)


def _inproj_bwd(dproj, dh1, x, g1, win_t):
    s = x.shape[0]

    def body(dp_ref, dh1_ref, x_ref, g_ref, w_ref, dx_ref, dg_ref):
        i = pl.program_id(0)
        dhn = _dot(dp_ref[...], w_ref[...])
        _, xn, r1 = _rms_fwd(x_ref[...], g_ref[...])
        dres, dg = _rms_bwd(dhn, xn, r1, g_ref[...])
        dx_ref[...] = dh1_ref[...] + dres

        @pl.when(i == 0)
        def _():
            dg_ref[...] = jnp.zeros_like(dg_ref)

        dg_ref[...] += dg

    sd = jax.ShapeDtypeStruct
    return pl.pallas_call(
        body, name="inproj_bwd", grid=(s // TM,),
        in_specs=[_row_spec(TM, INW), _row_spec(TM, D), _row_spec(TM, D), _const_spec((1, D)), _const_spec((INW, D))],
        out_specs=[_row_spec(TM, D), _const_spec((1, D))],
        out_shape=[sd((s, D), F32), sd((1, D), F32)],
        compiler_params=_cparams("arbitrary"),
    )(dproj, dh1, x, g1, win_t)


def _wgrad(a, b, name, bm, bn, bk=2 * TM, square_a=False, also_bf16=False):
    s, m = a.shape
    n = b.shape[1]
    bm, bn = min(bm, m), min(bn, n)
    nk = s // bk

    def body(a_ref, b_ref, o_ref, *low):
        @pl.when(pl.program_id(2) == 0)
        def _():
            o_ref[...] = jnp.zeros_like(o_ref)

        av = a_ref[...]
        if square_a:
            av = av.astype(F32)
            av = av * av
        o_ref[...] += _dot_tn(av.astype(BF16), b_ref[...].astype(BF16))
        if also_bf16:
            @pl.when(pl.program_id(2) == nk - 1)
            def _():
                low[0][...] = o_ref[...].astype(BF16)

    out_spec = pl.BlockSpec((bm, bn), lambda i, j, k: (i, j))
    res = pl.pallas_call(
        body, name=name, grid=(m // bm, n // bn, nk),
        in_specs=[pl.BlockSpec((bk, bm), lambda i, j, k: (k, i)), pl.BlockSpec((bk, bn), lambda i, j, k: (k, j))],
        out_specs=[out_spec, out_spec] if also_bf16 else out_spec,
        out_shape=([jax.ShapeDtypeStruct((m, n), F32), jax.ShapeDtypeStruct((m, n), BF16)] if also_bf16
                   else jax.ShapeDtypeStruct((m, n), F32)),
        compiler_params=_cparams("arbitrary", "arbitrary", "arbitrary"),
    )(a, b)
    return res


def _adamw_math(w, g, m, v):
    m = B1 * m + (1.0 - B1) * g
    v = B2 * v + (1.0 - B2) * (g * g)
    m_hat = m / (1.0 - B1 ** STEP)
    v_hat = v / (1.0 - B2 ** STEP)
    delta = -LR * (m_hat / (jnp.sqrt(v_hat) + AEPS) + WD * w)
    return delta, m, v


def _adamw(w, g, m, v, name):
    rows, cols = w.shape
    br = min(rows, 256)
    while rows % br:
        br -= 8

    def body(w_ref, g_ref, m_ref, v_ref, d_ref, mo_ref, vo_ref):
        d, mn, vn = _adamw_math(w_ref[...], g_ref[...], m_ref[...], v_ref[...])
        d_ref[...] = d
        mo_ref[...] = mn
        vo_ref[...] = vn

    spec = _row_spec(br, cols)
    sd = jax.ShapeDtypeStruct((rows, cols), F32)
    return pl.pallas_call(
        body, name=name, grid=(rows // br,), in_specs=[spec] * 4, out_specs=[spec] * 3,
        out_shape=[sd, sd, sd], compiler_params=_cparams("arbitrary"),
    )(w, g, m, v)


def _local_step(x, hn1, target, small, win_t, rest_weights, early_grads=None, after_attention_bwd=None,
                late_grads=None):
    slopes = jnp.asarray(_alibi_slopes(NH))
    q, k, v, u, z = _inproj_fwd(hn1, win_t)
    outs, lses = [], []
    for i, dil in enumerate(DILS):
        o, l = _attn_fwd(q[i], k[i], v[i], slopes, dil)
        outs.append(o)
        lses.append(l)
    wout, wff1, wff2 = rest_weights(functools.reduce(lambda a, b: a + b, [l[0, 0:8, :] for l in lses]))
    attn, lse, mixed, h1 = _mix_fwd(outs, lses, u, z, x, small["ln_g"], small["ln_b"], small["sgu_w"],
                                    small["bias_t"], small["attn_out_g"], small["gmlp_out_g"], wout)
    hn2, rf, dh2, loss, dgf = _mlp_fwd(h1, small["norm2_g"], wff1, wff2, small["final_norm_g"], target)
    df, dh1, dg2 = _mlp_bwd(dh2, rf, h1, small["norm2_g"], wff1, wff2)
    gwff1 = _wgrad(hn2, df, "wgrad_ff1", D, 1024)
    gwff2 = _wgrad(rf, dh2, "wgrad_ff2", 1024, D, square_a=True)
    gwout = _wgrad(mixed, dh1, "wgrad_out", D, D)
    ga, g1 = small["attn_out_g"], small["norm1_g"]
    pin = early_grads(gwff1, gwff2, gwout) if early_grads else None
    if pin is not None:
        ga = ga + pin
    (do, delta, du, dz, dga, dgg, dlng, dlnb, dws, db) = _mix_bwd(
        dh1, attn, u, z, small["ln_g"], small["ln_b"], small["sgu_w"], small["sgu_wt"], small["bias_t"],
        ga, small["gmlp_out_g"], wout)
    dqs, dks, dvs = [], [], []
    for i, dil in enumerate(DILS):
        dqs.append(_attn_bwd_dq(q[i], k[i], v[i], do[i], lse[i], delta[i], slopes, dil))
        dk, dv = _attn_bwd_dkv(q[i], k[i], v[i], do[i], lse[i], delta[i], slopes, dil)
        dks.append(dk)
        dvs.append(dv)
    marker = functools.reduce(lambda a, b: a + b, [t[0, 0:8, 0:LANES] for t in dqs + dks + dvs])
    partial = dict(ln_g=dlng, ln_b=dlnb, sgu_w=dws, sgu_b=db[:, :NG].T, attn_out_g=dga, gmlp_out_g=dgg,
                   norm2_g=dg2, final_norm_g=dgf)
    pin = after_attention_bwd(marker, partial, loss[0, 0]) if after_attention_bwd else None
    dproj = _dproj_merge(dqs, dks, dvs, du, dz, jnp.zeros((1, 1), F32) if pin is None else pin)
    gwin_t, gwin_low = _wgrad(dproj, hn1, "wgrad_in", INW // 2, D, also_bf16=True)
    pin = late_grads(gwin_t, gwin_low) if late_grads else None
    if pin is not None:
        g1 = g1 + pin
    dx, dg1 = _inproj_bwd(dproj, dh1, x, g1, win_t)
    small_grads = dict(partial, norm1_g=dg1)
    return loss[0, 0], dx, small_grads, (gwin_t, gwout, gwff1, gwff2)


ANY = pl.BlockSpec(memory_space=pl.ANY)
NDEV = 8


def _position():
    return lax.axis_index("x"), lax.axis_index("y"), lax.axis_index("c")


def _other_chips(x, y):
    return [(1 - x, y), (x, 1 - y), (1 - x, 1 - y)]


def _remote(src, dst, send_sem, recv_sem, device):
    return pltpu.make_async_remote_copy(src_ref=src, dst_ref=dst, send_sem=send_sem, recv_sem=recv_sem,
                                        device_id=device, device_id_type=MESH)


HBM = pl.BlockSpec(memory_space=pltpu.HBM)
SEM = pl.BlockSpec(memory_space=pltpu.SEMAPHORE)
DATAFLOW = pltpu.SideEffectType.DATAFLOW_SIDE_EFFECTING


def _in_hbm(a):
    return pltpu.with_memory_space_constraint(a, pltpu.HBM)


def _gather_start(shards, name):
    n = len(shards)
    lands = [jnp.broadcast_to(sh[None], (NCHIP,) + sh.shape) for sh in shards]

    def body(*refs):
        w_refs, land_refs = refs[:n], refs[n:2 * n]
        send_sems, recv_sems = refs[2 * n:2 * n + 2]
        token = refs[-1]
        x, y, c = _position()
        for w in range(n):
            for k, (px, py) in enumerate(_other_chips(x, y)):
                m = 3 * w + k
                _remote(w_refs[w], land_refs[w].at[2 * x + y], send_sems.at[m], recv_sems.at[m], (px, py, c)).start()
        token[...] = jnp.zeros_like(token)

    res = _split_call(body, name, list(shards) + lands, (3 * n, 3 * n), (TOKEN,))
    return res[0], res[1], res[2:2 + n], res[2 + n:2 + 2 * n], res[-1]


def _gather_wait(send_sems, recv_sems, shards, lands, after, name):
    n = len(shards)

    def body(*refs):
        w_refs, land_refs = refs[:n], refs[n:2 * n]
        send_sems, recv_sems = refs[2 * n:2 * n + 2]
        x, y, c = _position()
        for w in range(n):
            for k, (px, py) in enumerate(_other_chips(x, y)):
                m = 3 * w + k
                cp = _remote(w_refs[w], land_refs[w].at[2 * px + py], send_sems.at[m], recv_sems.at[m], (px, py, c))
                cp.wait_send()
                cp.wait_recv()

    operands = list(shards) + list(lands)
    res = pl.pallas_call(
        body, name=name, out_shape=tuple(pltpu.HBM(a.shape, a.dtype) for a in operands),
        in_specs=(HBM,) * (2 * n) + (SEM, SEM, ANY), out_specs=(HBM,) * (2 * n),
        input_output_aliases={i: i for i in range(2 * n)},
        compiler_params=pltpu.CompilerParams(has_side_effects=DATAFLOW),
    )(*operands, send_sems, recv_sems, after)
    return res[n:]


def _xor_peers(x, y, c):
    peers = []
    for k in range(1, NDEV):
        kx, ky, kc = (k >> 2) & 1, (k >> 1) & 1, k & 1
        peers.append((1 - x if kx else x, 1 - y if ky else y, 1 - c if kc else c))
    return peers


def _piece(part_ref, px, py, pc):
    slab = 2 * px + py
    if len(part_ref.shape) == 3:
        half = part_ref.shape[1] // 2
        return part_ref.at[slab, pl.ds(pc * half, half), :]
    half = part_ref.shape[0] // 2
    return part_ref.at[pl.ds(pc * half, half), pl.ds(pl.multiple_of(slab * D, D), D)]


def _split_call(body, name, operands, n_sems, extra_out=()):
    n = len(operands)
    sems = tuple(pltpu.SemaphoreType.DMA((m,)) for m in n_sems)
    thru = tuple(pltpu.HBM(a.shape, a.dtype) for a in operands)
    return pl.pallas_call(
        body, name=name, out_shape=sems + thru + tuple(extra_out),
        in_specs=(HBM,) * n,
        out_specs=(SEM,) * len(sems) + (HBM,) * n + (pl.BlockSpec(memory_space=pltpu.VMEM),) * len(extra_out),
        input_output_aliases={i: len(sems) + i for i in range(n)},
        compiler_params=pltpu.CompilerParams(has_side_effects=DATAFLOW),
    )(*[_in_hbm(a) for a in operands])


TOKEN = jax.ShapeDtypeStruct((8, LANES), F32)


def _pack_copies(pack_ref, land_ref, send_sems, recv_sems, base, position, start):
    x, y, c = position
    for k, (px, py, pc) in enumerate(_xor_peers(x, y, c)):
        if start:
            _remote(pack_ref, land_ref.at[4 * x + 2 * y + c], send_sems.at[base + k], recv_sems.at[base + k],
                    (px, py, pc)).start()
        else:
            cp = _remote(pack_ref, land_ref.at[4 * px + 2 * py + pc], send_sems.at[base + k], recv_sems.at[base + k],
                         (px, py, pc))
            cp.wait_send()
            cp.wait_recv()


def _pack_landing(pack):
    return jnp.broadcast_to(pack[None], (NDEV,) + pack.shape)


def _reduce_start(parts, name, pack=None):
    nw = len(parts)
    lands = [lax.empty((NDEV - 1, p.shape[-2] // 2, D), p.dtype) for p in parts]
    operands = list(parts) + lands + ([pack, _pack_landing(pack)] if pack is not None else [])
    nops = len(operands)

    def body(*refs):
        part_refs, land_refs = refs[:nw], refs[nw:2 * nw]
        send_sems, recv_sems = refs[nops:nops + 2]
        token = refs[-1]
        x, y, c = _position()
        for w in range(nw):
            for k, peer in enumerate(_xor_peers(x, y, c)):
                n = w * (NDEV - 1) + k
                _remote(_piece(part_refs[w], *peer), land_refs[w].at[k], send_sems.at[n], recv_sems.at[n],
                        peer).start()
        if pack is not None:
            _pack_copies(refs[2 * nw], refs[2 * nw + 1], send_sems, recv_sems, nw * (NDEV - 1), (x, y, c), True)
        token[...] = jnp.zeros_like(token)

    n = (nw + (pack is not None)) * (NDEV - 1)
    res = _split_call(body, name, operands, (n, n), (TOKEN,))
    return res[0], res[1], res[2:2 + nops], res[-1]


def _reduce_wait(send_sems, recv_sems, operands, nw, after, name):
    nops = len(operands)
    has_pack = nops > 2 * nw

    def body(*refs):
        part_refs, land_refs = refs[:nw], refs[nw:2 * nw]
        send_sems, recv_sems = refs[nops:nops + 2]
        x, y, c = _position()
        for w in range(nw):
            for k, peer in enumerate(_xor_peers(x, y, c)):
                n = w * (NDEV - 1) + k
                cp = _remote(_piece(part_refs[w], *peer), land_refs[w].at[k], send_sems.at[n], recv_sems.at[n], peer)
                cp.wait_send()
                cp.wait_recv()
        if has_pack:
            _pack_copies(refs[2 * nw], refs[2 * nw + 1], send_sems, recv_sems, nw * (NDEV - 1), (x, y, c), False)

    res = pl.pallas_call(
        body, name=name, out_shape=tuple(pltpu.HBM(a.shape, a.dtype) for a in operands),
        in_specs=(HBM,) * nops + (SEM, SEM, ANY), out_specs=(HBM,) * nops,
        input_output_aliases={i: i for i in range(nops)},
        compiler_params=pltpu.CompilerParams(has_side_effects=DATAFLOW),
    )(*operands, send_sems, recv_sems, after)
    return res[:nw], res[nw:2 * nw], (res[2 * nw + 1] if has_pack else None)


def _sum_pieces(part, land, sel, name):
    half = part.shape[-2] // 2
    br = 128 if half % 128 == 0 else half // 2
    nb = half // br

    def body(sel_ref, own_ref, *refs):
        acc = own_ref[...]
        for r in refs[:NDEV - 1]:
            acc = acc + r[...].astype(F32)
        refs[NDEV - 1][...] = acc

    if part.ndim == 3:
        own_spec = pl.BlockSpec((None, br, D), lambda i, sel_ref: (sel_ref[0], sel_ref[1] * nb + i, 0))
    else:
        own_spec = pl.BlockSpec((br, D), lambda i, sel_ref: (sel_ref[1] * nb + i, sel_ref[0]))
    slot_specs = [pl.BlockSpec((None, br, D), functools.partial(lambda i, sel_ref, k: (k, i, 0), k=k))
                  for k in range(NDEV - 1)]
    return pl.pallas_call(
        body, name=name,
        grid_spec=pltpu.PrefetchScalarGridSpec(
            num_scalar_prefetch=1, grid=(nb,), in_specs=[own_spec] + slot_specs,
            out_specs=pl.BlockSpec((br, D), lambda i, sel_ref: (i, 0))),
        out_shape=jax.ShapeDtypeStruct((half, D), F32),
        compiler_params=_cparams("arbitrary"),
    )(sel, part, *([land] * (NDEV - 1)))


def _share_start(halves, name, pack=None):
    nw = len(halves)
    lands = [lax.empty(h.shape, F32) for h in halves]
    operands = list(halves) + lands + ([pack, _pack_landing(pack)] if pack is not None else [])
    nops = len(operands)

    def body(*refs):
        h_refs, land_refs = refs[:nw], refs[nw:2 * nw]
        send_sems, recv_sems = refs[nops:nops + 2]
        token = refs[-1]
        x, y, c = _position()
        for w in range(nw):
            _remote(h_refs[w], land_refs[w], send_sems.at[w], recv_sems.at[w], (x, y, 1 - c)).start()
        if pack is not None:
            _pack_copies(refs[2 * nw], refs[2 * nw + 1], send_sems, recv_sems, nw, (x, y, c), True)
        token[...] = jnp.zeros_like(token)

    n = nw + (NDEV - 1 if pack is not None else 0)
    res = _split_call(body, name, operands, (n, n), (TOKEN,))
    return res[0], res[1], res[2:2 + nops], res[-1]


def _share_wait(send_sems, recv_sems, operands, nw, after, name):
    nops = len(operands)
    has_pack = nops > 2 * nw

    def body(*refs):
        h_refs, land_refs = refs[:nw], refs[nw:2 * nw]
        send_sems, recv_sems = refs[nops:nops + 2]
        x, y, c = _position()
        for w in range(nw):
            cp = _remote(h_refs[w], land_refs[w], send_sems.at[w], recv_sems.at[w], (x, y, 1 - c))
            cp.wait_send()
            cp.wait_recv()
        if has_pack:
            _pack_copies(refs[2 * nw], refs[2 * nw + 1], send_sems, recv_sems, nw, (x, y, c), False)

    res = pl.pallas_call(
        body, name=name, out_shape=tuple(pltpu.HBM(a.shape, a.dtype) for a in operands),
        in_specs=(HBM,) * nops + (SEM, SEM, ANY), out_specs=(HBM,) * nops,
        input_output_aliases={i: i for i in range(nops)},
        compiler_params=pltpu.CompilerParams(has_side_effects=DATAFLOW),
    )(*operands, send_sems, recv_sems, after)
    return res[:nw], res[nw:2 * nw], (res[2 * nw + 1] if has_pack else None)


def _join_halves(own, other, c):
    first = jnp.where(c == 0, own, other)
    second = jnp.where(c == 0, other, own)
    return jnp.concatenate([first, second], axis=0)


SMALL_SIZES = (("norm1_g", D), ("sgu_ln_g", GW), ("sgu_ln_b", GW), ("sgu_w", NG * CHUNK * CHUNK),
               ("sgu_b", NG * CHUNK), ("attn_out_g", A), ("gmlp_out_g", GW), ("norm2_g", D),
               ("final_norm_g", D))
PARAM_ROWS = sum(n for _, n in SMALL_SIZES) // LANES
SMALL_ROWS = PARAM_ROWS + 8


def _pack_small(tree, first_extra=None):
    extra = jnp.zeros((8 * LANES,), F32)
    if first_extra is not None:
        extra = extra.at[0].set(first_extra)
    flat = jnp.concatenate([tree[n].reshape(-1) for n, _ in SMALL_SIZES] + [extra])
    return flat.reshape(SMALL_ROWS, LANES)


def _unpack_small(pack, shapes):
    flat = pack.reshape(-1)
    out, off = {}, 0
    for n, size in SMALL_SIZES:
        out[n] = flat[off:off + size].reshape(shapes[n])
        off += size
    return out


def _small_finish(pack_land, norm_land, wpack, mpack, vpack):
    def body(p_ref, n_ref, w_ref, m_ref, v_ref, go_ref, d_ref, mo_ref, vo_ref):
        total = p_ref[0]
        late = n_ref[0]
        for k in range(1, NDEV):
            total = total + p_ref[k]
            late = late + n_ref[k]
        go_ref[...] = total
        go_ref[0:8, :] = total[0:8, :] + late
        d, mn, vn = _adamw_math(w_ref[...], go_ref[...], m_ref[...], v_ref[...])
        d_ref[...] = d
        mo_ref[...] = mn
        vo_ref[...] = vn

    sd = jax.ShapeDtypeStruct((SMALL_ROWS, LANES), F32)
    vm = pl.BlockSpec(memory_space=pltpu.VMEM)
    return pl.pallas_call(
        body, name="small_finish", in_specs=[vm] * 5, out_specs=[vm] * 4, out_shape=[sd] * 4,
        compiler_params=_cparams(),
    )(pack_land, norm_land, wpack, mpack, vpack)


def kernel(x, norm1_g, w_in, sgu_ln_g, sgu_ln_b, sgu_w, sgu_b, attn_out_g, gmlp_out_g, w_out, norm2_g, w_ff1, w_ff2, final_norm_g, loss_target, m_norm1_g, m_w_in, m_sgu_ln_g, m_sgu_ln_b, m_sgu_w, m_sgu_b, m_attn_out_g, m_gmlp_out_g, m_w_out, m_norm2_g, m_w_ff1, m_w_ff2, m_final_norm_g, v_norm1_g, v_w_in, v_sgu_ln_g, v_sgu_ln_b, v_sgu_w, v_sgu_b, v_attn_out_g, v_gmlp_out_g, v_w_out, v_norm2_g, v_w_ff1, v_w_ff2, v_final_norm_g):
    names = [n for n, _ in SMALL_SIZES]
    w_small = dict(norm1_g=norm1_g, sgu_ln_g=sgu_ln_g, sgu_ln_b=sgu_ln_b, sgu_w=sgu_w, sgu_b=sgu_b,
                   attn_out_g=attn_out_g, gmlp_out_g=gmlp_out_g, norm2_g=norm2_g, final_norm_g=final_norm_g)
    m_small = dict(norm1_g=m_norm1_g, sgu_ln_g=m_sgu_ln_g, sgu_ln_b=m_sgu_ln_b, sgu_w=m_sgu_w, sgu_b=m_sgu_b,
                   attn_out_g=m_attn_out_g, gmlp_out_g=m_gmlp_out_g, norm2_g=m_norm2_g,
                   final_norm_g=m_final_norm_g)
    v_small = dict(norm1_g=v_norm1_g, sgu_ln_g=v_sgu_ln_g, sgu_ln_b=v_sgu_ln_b, sgu_w=v_sgu_w, sgu_b=v_sgu_b,
                   attn_out_g=v_attn_out_g, gmlp_out_g=v_gmlp_out_g, norm2_g=v_norm2_g,
                   final_norm_g=v_final_norm_g)
    shapes = {n: w_small[n].shape for n in names}

    start_in = _gather_start([w_in[0].T.astype(BF16)], "gather_in_start")
    issued = start_in[4][0:1, 0:1]
    start_rest = _gather_start([(w_out[0] + issued).astype(BF16), w_ff1[0].astype(BF16), w_ff2[0].astype(BF16)],
                               "gather_rest_start")
    hn1 = _norm1(x[0], norm1_g + start_rest[4][0:1, 0:1])
    win_t = _gather_wait(*start_in[:4], after=hn1, name="gather_in_wait")[0].reshape(INW, D)

    def rest_weights(after):
        wout, wff1, wff2 = _gather_wait(*start_rest[:4], after=after, name="gather_rest_wait")
        return wout.reshape(D, D), wff1, wff2.reshape(DFF, D)

    small = dict(
        norm1_g=norm1_g, ln_g=sgu_ln_g.reshape(1, GW), ln_b=sgu_ln_b.reshape(1, GW), sgu_w=sgu_w[0],
        sgu_wt=jnp.swapaxes(sgu_w[0], 1, 2), bias_t=jnp.repeat(sgu_b[0].T, DH, axis=1),
        attn_out_g=attn_out_g, gmlp_out_g=gmlp_out_g, norm2_g=norm2_g, final_norm_g=final_norm_g.reshape(1, D))
    xi, yi, ci = _position()
    sel = jnp.stack([2 * xi + yi, ci]).astype(jnp.int32)
    state = {}

    def as_slabs(g):
        return g.reshape(NCHIP, g.shape[0] // NCHIP, D)

    def early_grads(gwff1, gwff2, gwout):
        state["early"] = _reduce_start([gwff1, as_slabs(gwff2), as_slabs(gwout)], "reduce_early_start")
        return state["early"][3][0:1, 0:1]

    def after_attention_bwd(marker, partial, loss_part):
        send_sems, recv_sems, operands, _ = state["early"]
        parts, lands, _ = _reduce_wait(send_sems, recv_sems, operands, 3, marker, "reduce_early_wait")
        halves = [_sum_pieces(p, l, sel, "sum_" + n) for p, l, n in zip(parts, lands, ("w_ff1", "w_ff2", "w_out"))]
        pack = _pack_small(dict(partial, norm1_g=jnp.zeros((1, D), F32), sgu_ln_g=partial["ln_g"],
                                sgu_ln_b=partial["ln_b"]), loss_part)
        state["early_share"] = _share_start(halves, "share_early_start", pack)
        return state["early_share"][3][0:1, 0:1]

    def late_grads(gwin_t, gwin_low):
        state["late"] = _reduce_start([as_slabs(gwin_low)], "reduce_late_start")
        state["late_own"] = as_slabs(gwin_t)
        return state["late"][3][0:1, 0:1]

    _, dx, sg, _ = _local_step(
        x[0], hn1, loss_target[0], small, win_t, rest_weights, early_grads, after_attention_bwd, late_grads)
    send_sems, recv_sems, operands, _ = state["early_share"]
    own, other, pack_land = _share_wait(send_sems, recv_sems, operands, 3, dx, "share_early_wait")
    send_sems, recv_sems, operands, _ = state["late"]
    _, late_lands, _ = _reduce_wait(send_sems, recv_sems, operands, 1, dx, "reduce_late_wait")
    late_share = _share_start([_sum_pieces(state["late_own"], late_lands[0], sel, "sum_w_in")], "share_late_start",
                              sg["norm1_g"].reshape(8, LANES))
    issued = late_share[3][0:1, 0:1]
    g_big = {n: _join_halves(o, t, ci) + issued for n, o, t in zip(("w_ff1", "w_ff2", "w_out"), own, other)}
    w_big = dict(w_in=(w_in, m_w_in, v_w_in), w_out=(w_out, m_w_out, v_w_out),
                 w_ff1=(w_ff1, m_w_ff1, v_w_ff1), w_ff2=(w_ff2, m_w_ff2, v_w_ff2))
    grads, deltas, new_m, new_v = {}, {}, {}, {}

    def update(n):
        w, m, v = w_big[n]
        d, mn, vn = _adamw(w[0], g_big[n], m[0], v[0], "adamw_" + n)
        grads[n], deltas[n], new_m[n], new_v[n] = g_big[n][None], d[None], mn[None], vn[None]

    for n in ("w_ff1", "w_ff2", "w_out"):
        update(n)
    updated = deltas["w_out"][0, 0:8, 0:LANES] + deltas["w_ff1"][0, 0:8, 0:LANES] + deltas["w_ff2"][0, 0:8, 0:LANES]
    own, other, norm_land = _share_wait(late_share[0], late_share[1], late_share[2], 1, updated, "share_late_wait")
    g_big["w_in"] = _join_halves(own[0], other[0], ci).T
    update("w_in")

    packs = _small_finish(pack_land, norm_land, _pack_small(w_small), _pack_small(m_small), _pack_small(v_small))
    loss = packs[0][PARAM_ROWS, 0]
    for tree, pack in zip((grads, deltas, new_m, new_v), packs):
        tree.update(_unpack_small(pack, shapes))

    order = ["norm1_g", "w_in", "sgu_ln_g", "sgu_ln_b", "sgu_w", "sgu_b", "attn_out_g", "gmlp_out_g", "w_out",
             "norm2_g", "w_ff1", "w_ff2", "final_norm_g"]
    return (loss, dx[None], *[grads[n] for n in order], *[deltas[n] for n in order],
            *[new_m[n] for n in order], *[new_v[n] for n in order])
```

```python
import functools
import math

import numpy as np
import jax
import jax.numpy as jnp
from jax import lax
from jax.experimental import pallas as pl
from jax.experimental.pallas import tpu as pltpu

F32 = jnp.float32
BF16 = jnp.bfloat16

D = 1024
NH = 12
DH = 64
A = NH * DH
NG = 4
GW = NG * DH
INW = 3 * A + 2 * GW
DFF = 4 * D
CHUNK = 128
PATTERNS = ((128, 1), (512, 4), (2048, 16))
EPS = 1e-6
SCALE = DH ** -0.5
NEG = -1e30

LR, B1, B2, AEPS, WD, STEP = 0.001, 0.9, 0.999, 1e-08, 0.01, 10

TM = 512
TMX = 512
ATT_ROWS = 4096
FF_CH = 1024
LANES = 128
NCHIP = 4
VMEM_LIMIT = 56 * 1024 * 1024
MESH = pl.DeviceIdType.MESH


def _cparams(*sem, **kw):
    return pltpu.CompilerParams(dimension_semantics=sem if sem else None,
                                vmem_limit_bytes=VMEM_LIMIT, **kw)


def _dot(a, b):
    return jnp.dot(a, b, preferred_element_type=F32)


def _dot_nt(a, b):
    return lax.dot_general(a, b, (((1,), (1,)), ((), ())), preferred_element_type=F32)


def _dot_tn(a, b):
    return lax.dot_general(a, b, (((0,), (0,)), ((), ())), preferred_element_type=F32)


def _dot_hi(a, b):
    return jnp.dot(a, b, preferred_element_type=F32, precision=lax.Precision.HIGHEST)


def _alibi_slopes(n):
    def pow2(m):
        start = 2.0 ** (-8.0 / m)
        return [start ** (i + 1) for i in range(m)]
    if math.log2(n).is_integer():
        s = pow2(n)
    else:
        c = 2 ** int(math.floor(math.log2(n)))
        s = pow2(c) + pow2(2 * c)[0::2][: n - c]
    return np.asarray(s, dtype=np.float32)


def _rms_fwd(v, g):
    r = lax.rsqrt(jnp.mean(v * v, axis=-1, keepdims=True) + EPS)
    vn = v * r
    return vn * g, vn, r


def _rms_bwd(dy, vn, r, g):
    w = dy * g
    dv = r * (w - vn * jnp.mean(w * vn, axis=-1, keepdims=True))
    return dv, jnp.sum(dy * vn, axis=0, keepdims=True)


_K0 = math.sqrt(2.0 / math.pi)
_K1 = 0.044715


def _gelu(v):
    return 0.5 * v * (1.0 + jnp.tanh(_K0 * (v + _K1 * (v * v * v))))


def _gelu_grad(v):
    t = jnp.tanh(_K0 * (v + _K1 * (v * v * v)))
    return 0.5 * (1.0 + t) + 0.5 * v * (1.0 - t * t) * (_K0 * (1.0 + 3.0 * _K1 * v * v))


def _row_spec(rows, cols):
    return pl.BlockSpec((rows, cols), lambda i: (i, 0))


def _const_spec(shape):
    nd = len(shape)
    return pl.BlockSpec(shape, lambda i: (0,) * nd, pipeline_mode=pl.Buffered(1))


DILS = tuple(d for _, d in PATTERNS)


def _fill_cols(scr, value):
    for cb in range(value.shape[1] // LANES):
        scr[cb] = value[:, cb * LANES:(cb + 1) * LANES]


def _split_residues(scr, out_ref, dil):
    nb, rows, _ = scr.shape
    for r in range(dil):
        for cb in range(nb):
            piece = scr.at[cb][pl.ds(r, rows // dil, stride=dil), :]
            out_ref[r, :, cb * LANES:(cb + 1) * LANES] = piece.astype(out_ref.dtype)


def _merge_residues(in_ref, scr, dil):
    nb, rows, _ = scr.shape
    for r in range(dil):
        for cb in range(nb):
            scr.at[cb][pl.ds(r, rows // dil, stride=dil), :] = in_ref[r, :, cb * LANES:(cb + 1) * LANES].astype(F32)
    return jnp.concatenate([scr[cb] for cb in range(nb)], axis=-1)


def _col_scratch(rows, width):
    return pltpu.VMEM((width // LANES, rows, LANES), F32)


def _res_spec(dil, rows, width):
    return pl.BlockSpec((dil, rows // dil, width), lambda i: (0, i, 0))


def _res_shape(s, dil, width, dtype):
    return jax.ShapeDtypeStruct((dil, s // dil, width), dtype)


def _norm1(x, g1):
    s = x.shape[0]

    def body(x_ref, g_ref, hn_ref):
        hn, _, _ = _rms_fwd(x_ref[...], g_ref[...])
        hn_ref[...] = hn.astype(BF16)

    return pl.pallas_call(
        body, name="norm1", grid=(s // TM,), in_specs=[_row_spec(TM, D), _const_spec((1, D))],
        out_specs=_row_spec(TM, D), out_shape=jax.ShapeDtypeStruct((s, D), BF16),
        compiler_params=_cparams("arbitrary"),
    )(x, g1)


def _inproj_fwd(hn1, win_t):
    s = hn1.shape[0]
    nd = len(DILS)

    def body(hn_ref, w_ref, *rest):
        qkv_refs = rest[:3 * nd]
        u_ref, z_ref, scr = rest[3 * nd:]
        hn = hn_ref[...]
        for t in range(3):
            seg = _dot_nt(hn, w_ref[t * A:(t + 1) * A, :])
            seg = seg * SCALE if t == 0 else seg
            _fill_cols(scr, seg)
            for di, dil in enumerate(DILS):
                if dil == 1:
                    qkv_refs[t * nd + di][0] = seg.astype(BF16)
                else:
                    _split_residues(scr, qkv_refs[t * nd + di], dil)
        u_ref[...] = _dot_nt(hn, w_ref[3 * A:3 * A + GW, :])
        z_ref[...] = _dot_nt(hn, w_ref[3 * A + GW:INW, :])

    res = pl.pallas_call(
        body, name="inproj_fwd", grid=(s // TM,),
        in_specs=[_row_spec(TM, D), _const_spec((INW, D))],
        out_specs=[_res_spec(d, TM, A) for _ in range(3) for d in DILS] + [_row_spec(TM, GW), _row_spec(TM, GW)],
        out_shape=[_res_shape(s, d, A, BF16) for _ in range(3) for d in DILS]
                  + [jax.ShapeDtypeStruct((s, GW), F32)] * 2,
        scratch_shapes=[_col_scratch(TM, A)],
        compiler_params=_cparams("arbitrary"),
    )(hn1, win_t)
    q, k, v = (res[t * nd:(t + 1) * nd] for t in range(3))
    return q, k, v, res[-2], res[-1]


def _att_geometry(length, dil):
    merge = max(1, min(dil, ATT_ROWS // length))
    rows = min(length * merge, ATT_ROWS)
    nsub = rows // CHUNK
    return merge, rows, length * merge // rows, nsub, min(length // CHUNK, nsub)


def _merged(t, merge):
    return t.reshape(t.shape[0] // merge, t.shape[1] * merge, t.shape[2])


def _stack_heads(t):
    lane = lax.broadcasted_iota(jnp.int32, t.shape, 1)
    zero = jnp.zeros_like(t)
    return jnp.concatenate([jnp.where(lane < DH, t, zero), jnp.where(lane >= DH, t, zero)], axis=0)


def _head_cols(t, hp):
    lane = lax.broadcasted_iota(jnp.int32, t.shape, 1)
    cols = [jnp.sum(jnp.where(lane == 2 * hp + h, t, 0.0), axis=-1, keepdims=True) for h in range(2)]
    return jnp.concatenate(cols, axis=0)


def _unstack_heads(t2):
    n = t2.shape[0] // 2
    lane = lax.broadcasted_iota(jnp.int32, (n, LANES), 1)
    return jnp.where(lane < DH, t2[:n], t2[n:])


def _query_window_bias(s0, s1, dil, first):
    row = lax.broadcasted_iota(jnp.int32, (2 * CHUNK, 2 * CHUNK), 0)
    col = lax.broadcasted_iota(jnp.int32, (2 * CHUNK, 2 * CHUNK), 1)
    steps = (row & (CHUNK - 1)) + CHUNK - col
    valid = (steps >= 0) & (steps <= CHUNK)
    if first:
        valid = valid & (col >= CHUNK)
    slope = jnp.where(row < CHUNK, s0, s1)
    return jnp.where(valid, -slope * (steps * dil).astype(F32), NEG)


def _key_block_bias(s0, s1, dil, last):
    key = lax.broadcasted_iota(jnp.int32, (CHUNK, 4 * CHUNK), 0)
    col = lax.broadcasted_iota(jnp.int32, (CHUNK, 4 * CHUNK), 1)
    wq = col & (2 * CHUNK - 1)
    steps = wq - key
    valid = (steps >= 0) & (steps <= CHUNK)
    if last:
        valid = valid & (wq < CHUNK)
    slope = jnp.where(col < 2 * CHUNK, s0, s1)
    return jnp.where(valid, -slope * (steps * dil).astype(F32), NEG)


def _head_rows(t, hp):
    row = lax.broadcasted_iota(jnp.int32, (8, LANES), 0)
    lane = lax.broadcasted_iota(jnp.int32, (8, LANES), 1)
    pick = jnp.where((row < 2) & (lane == 2 * hp + row), 1.0, 0.0).astype(BF16)
    hi = t.astype(BF16)
    rest = t - hi.astype(F32)
    mid = rest.astype(BF16)
    low = (rest - mid.astype(F32)).astype(BF16)
    return _dot_nt(pick, hi) + _dot_nt(pick, mid) + _dot_nt(pick, low)


def _att_specs(dil, rows, nsub, nblk):
    main = pl.BlockSpec((None, rows, LANES), lambda r, c, hp: (r, c, hp))
    prev = pl.BlockSpec((None, CHUNK, LANES), lambda r, c, hp: (r, jnp.maximum(c * nsub - 1, 0), hp))
    nxt = pl.BlockSpec((None, CHUNK, LANES), lambda r, c, hp: (r, jnp.minimum((c + 1) * nsub, nblk - 1), hp))
    main_heads = pl.BlockSpec((None, rows, LANES), lambda r, c, hp: (r, c, 0))
    nxt_heads = pl.BlockSpec((None, CHUNK, LANES), lambda r, c, hp: (r, jnp.minimum((c + 1) * nsub, nblk - 1), 0))
    return main, prev, nxt, main_heads, nxt_heads


def _row_start(i):
    return i * CHUNK if isinstance(i, int) else pl.multiple_of(i * CHUNK, CHUNK)


def _first_blocks(block, nsub, seg, nch, ch, first_bias, bias_buf):
    for i in range(nsub):
        if i % seg:
            block(i, bias_buf[...])
        elif nch == 1:
            block(i, first_bias())
        else:
            block(i, jnp.where(ch == 0, first_bias(), bias_buf[...]))


def _last_blocks(block, nsub, seg, nch, ch, last_bias, bias_buf):
    for i in range(nsub):
        if (i + 1) % seg:
            block(i, bias_buf[...])
        elif nch == 1:
            block(i, last_bias())
        else:
            block(i, jnp.where(ch == nch - 1, last_bias(), bias_buf[...]))


def _attn_fwd(q, k, v, slopes, dil):
    length = q.shape[1]
    merge, rows, nch, nsub, seg = _att_geometry(length, dil)
    main, prev, _, main_heads, _ = _att_specs(dil, rows, nsub, length * merge // CHUNK)
    q, k, v = (_merged(t, merge) for t in (q, k, v))

    def body(sl_ref, q_ref, k_ref, v_ref, kh_ref, vh_ref, o_ref, lse_ref, kbuf, vbuf, bias_buf):
        ch = pl.program_id(1)
        hp = pl.program_id(2)
        lane = lax.broadcasted_iota(jnp.int32, (CHUNK, LANES), 1)
        kbuf[0:CHUNK, :] = kh_ref[...]
        kbuf[CHUNK:, :] = k_ref[...]
        vbuf[0:CHUNK, :] = vh_ref[...]
        vbuf[CHUNK:, :] = v_ref[...]
        s0, s1 = sl_ref[2 * hp], sl_ref[2 * hp + 1]

        def block(i, bias):
            row = _row_start(i)
            rs = pl.ds(row, CHUNK)
            q2 = _stack_heads(q_ref[rs, :])
            kw = kbuf[pl.ds(row, 2 * CHUNK), :]
            vw = vbuf[pl.ds(row, 2 * CHUNK), :]
            sc = _dot_nt(q2, kw) + bias
            m = jnp.max(sc, axis=-1, keepdims=True)
            p = jnp.exp(sc - m)
            l = jnp.sum(p, axis=-1, keepdims=True)
            o2 = _dot(p.astype(BF16), vw) * (1.0 / l)
            o_ref[rs, :] = _unstack_heads(o2).astype(BF16)
            lse = m + jnp.log(l)
            seen = jnp.where(hp == 0, 0.0, lse_ref[rs, :])
            lse_ref[rs, :] = jnp.where(lane == 2 * hp, lse[:CHUNK], jnp.where(lane == 2 * hp + 1, lse[CHUNK:], seen))

        bias_buf[...] = _query_window_bias(s0, s1, dil, False)
        _first_blocks(block, nsub, seg, nch, ch, lambda: _query_window_bias(s0, s1, dil, True), bias_buf)

    sd = jax.ShapeDtypeStruct
    o, lse = pl.pallas_call(
        body, name=f"attn_fwd_d{dil}", grid=(dil // merge, nch, NH // 2),
        in_specs=[pl.BlockSpec(memory_space=pltpu.SMEM), main, main, main, prev, prev],
        out_specs=[main, main_heads],
        out_shape=[sd((dil // merge, length * merge, A), BF16), sd((dil // merge, length * merge, LANES), F32)],
        scratch_shapes=[pltpu.VMEM((rows + CHUNK, LANES), BF16), pltpu.VMEM((rows + CHUNK, LANES), BF16),
                        pltpu.VMEM((2 * CHUNK, 2 * CHUNK), F32)],
        compiler_params=_cparams("arbitrary", "arbitrary", "arbitrary"),
    )(slopes, q, k, v, k, v)
    return o.reshape(dil, length, A), lse.reshape(dil, length, LANES)


def _attn_bwd_dq(q, k, v, do, lse, delta, slopes, dil):
    length = q.shape[1]
    merge, rows, nch, nsub, seg = _att_geometry(length, dil)
    main, prev, _, main_heads, _ = _att_specs(dil, rows, nsub, length * merge // CHUNK)
    q, k, v, do, lse, delta = (_merged(t, merge) for t in (q, k, v, do, lse, delta))

    def body(sl_ref, q_ref, k_ref, v_ref, do_ref, lse_ref, dl_ref, kh_ref, vh_ref, dq_ref, kbuf, vbuf, bias_buf):
        ch = pl.program_id(1)
        hp = pl.program_id(2)
        kbuf[0:CHUNK, :] = kh_ref[...]
        kbuf[CHUNK:, :] = k_ref[...]
        vbuf[0:CHUNK, :] = vh_ref[...]
        vbuf[CHUNK:, :] = v_ref[...]
        s0, s1 = sl_ref[2 * hp], sl_ref[2 * hp + 1]

        def block(i, bias):
            row = _row_start(i)
            rs = pl.ds(row, CHUNK)
            q2 = _stack_heads(q_ref[rs, :])
            do2 = _stack_heads(do_ref[rs, :])
            lse2 = _head_cols(lse_ref[rs, :], hp)
            dl2 = _head_cols(dl_ref[rs, :], hp)
            kw = kbuf[pl.ds(row, 2 * CHUNK), :]
            vw = vbuf[pl.ds(row, 2 * CHUNK), :]
            p = jnp.exp(_dot_nt(q2, kw) + bias - lse2)
            ds = p * (_dot_nt(do2, vw) - dl2)
            dq_ref[rs, :] = _unstack_heads(_dot(ds.astype(BF16), kw)).astype(BF16)

        bias_buf[...] = _query_window_bias(s0, s1, dil, False)
        _first_blocks(block, nsub, seg, nch, ch, lambda: _query_window_bias(s0, s1, dil, True), bias_buf)

    dq = pl.pallas_call(
        body, name=f"attn_dq_d{dil}", grid=(dil // merge, nch, NH // 2),
        in_specs=[pl.BlockSpec(memory_space=pltpu.SMEM), main, main, main, main, main_heads, main_heads, prev, prev],
        out_specs=main, out_shape=jax.ShapeDtypeStruct((dil // merge, length * merge, A), BF16),
        scratch_shapes=[pltpu.VMEM((rows + CHUNK, LANES), BF16), pltpu.VMEM((rows + CHUNK, LANES), BF16),
                        pltpu.VMEM((2 * CHUNK, 2 * CHUNK), F32)],
        compiler_params=_cparams("arbitrary", "arbitrary", "arbitrary"),
    )(slopes, q, k, v, do, lse, delta, k, v)
    return dq.reshape(dil, length, A)


def _attn_bwd_dkv(q, k, v, do, lse, delta, slopes, dil):
    length = q.shape[1]
    merge, rows, nch, nsub, seg = _att_geometry(length, dil)
    main, _, nxt, main_heads, nxt_heads = _att_specs(dil, rows, nsub, length * merge // CHUNK)
    q, k, v, do, lse, delta = (_merged(t, merge) for t in (q, k, v, do, lse, delta))

    def body(sl_ref, k_ref, v_ref, q_ref, do_ref, lse_ref, dl_ref, qh_ref, doh_ref, lseh_ref, dlh_ref,
             dk_ref, dv_ref, qbuf, dobuf, lse_rows, dl_rows, bias_buf):
        ch = pl.program_id(1)
        hp = pl.program_id(2)
        for buf, main_ref, halo_ref in ((qbuf, q_ref, qh_ref), (dobuf, do_ref, doh_ref)):
            buf[0:rows, :] = main_ref[...]
            buf[rows:, :] = halo_ref[...]
        for buf, main_ref, halo_ref in ((lse_rows, lse_ref, lseh_ref), (dl_rows, dl_ref, dlh_ref)):
            buf[:, 0:rows] = _head_rows(main_ref[...], hp)
            buf[:, rows:] = _head_rows(halo_ref[...], hp)
        s0, s1 = sl_ref[2 * hp], sl_ref[2 * hp + 1]

        def block(i, bias):
            row = _row_start(i)
            rs = pl.ds(row, CHUNK)
            win = pl.ds(row, 2 * CHUNK)
            kc = k_ref[rs, :]
            vc = v_ref[rs, :]
            q2 = _stack_heads(qbuf[win, :])
            do2 = _stack_heads(dobuf[win, :])
            cols = slice(i * CHUNK, (i + 2) * CHUNK)
            lse2 = jnp.concatenate([lse_rows[0:1, cols], lse_rows[1:2, cols]], axis=1)
            dl2 = jnp.concatenate([dl_rows[0:1, cols], dl_rows[1:2, cols]], axis=1)
            pt = jnp.exp(_dot_nt(kc, q2) + bias - lse2)
            dst = pt * (_dot_nt(vc, do2) - dl2)
            dv_ref[rs, :] = _dot(pt.astype(BF16), do2).astype(BF16)
            dk_ref[rs, :] = _dot(dst.astype(BF16), q2).astype(BF16)

        bias_buf[...] = _key_block_bias(s0, s1, dil, False)
        _last_blocks(block, nsub, seg, nch, ch, lambda: _key_block_bias(s0, s1, dil, True), bias_buf)

    sd = jax.ShapeDtypeStruct((dil // merge, length * merge, A), BF16)
    dk, dv = pl.pallas_call(
        body, name=f"attn_dkv_d{dil}", grid=(dil // merge, nch, NH // 2),
        in_specs=[pl.BlockSpec(memory_space=pltpu.SMEM), main, main, main, main, main_heads, main_heads,
                  nxt, nxt, nxt_heads, nxt_heads],
        out_specs=[main, main], out_shape=[sd, sd],
        scratch_shapes=[pltpu.VMEM((rows + CHUNK, LANES), BF16), pltpu.VMEM((rows + CHUNK, LANES), BF16),
                        pltpu.VMEM((8, rows + CHUNK), F32), pltpu.VMEM((8, rows + CHUNK), F32),
                        pltpu.VMEM((CHUNK, 4 * CHUNK), F32)],
        compiler_params=_cparams("arbitrary", "arbitrary", "arbitrary"),
    )(slopes, k, v, q, do, lse, delta, q, do, lse, delta)
    return dk.reshape(dil, length, A), dv.reshape(dil, length, A)


def _group_masks(width):
    lane = lax.broadcasted_iota(jnp.int32, (1, width), 1)
    return [(lane >= g * DH) & (lane < (g + 1) * DH) for g in range(width // DH)]


def _group_mean_matrix():
    i = lax.broadcasted_iota(jnp.int32, (GW, GW), 0) // DH
    j = lax.broadcasted_iota(jnp.int32, (GW, GW), 1) // DH
    return jnp.where(i == j, 1.0 / DH, 0.0).astype(F32)


def _tri_mask(lower):
    t = lax.broadcasted_iota(jnp.int32, (CHUNK, CHUNK), 0)
    u = lax.broadcasted_iota(jnp.int32, (CHUNK, CHUNK), 1)
    return (u <= t) if lower else (u >= t)


def _sgu_forward(u, z, lng, lnb, w_ref, bias_t, pmat, rows):
    ug = _gelu(u)
    zg = _gelu(z)
    mu = _dot_hi(zg, pmat)
    zc = zg - mu
    var = _dot_hi(zc * zc, pmat)
    rstd = lax.rsqrt(var + EPS)
    zhat = zc * rstd
    zn = (zhat * lng + lnb).astype(BF16)
    gm = _group_masks(GW)
    tri = _tri_mask(True)
    ws = [jnp.where(tri, w_ref[g], 0.0).astype(BF16) for g in range(NG)]
    pieces = []
    for c in range(rows // CHUNK):
        znc = zn[c * CHUNK:(c + 1) * CHUNK, :]
        mix = None
        for g in range(NG):
            part = jnp.where(gm[g], _dot(ws[g], znc), 0.0)
            mix = part if mix is None else mix + part
        pieces.append(mix + bias_t)
    mixed = jnp.concatenate(pieces, axis=0) if len(pieces) > 1 else pieces[0]
    return ug * mixed, ug, zhat, rstd, zn, mixed


def _head_spread():
    h = lax.broadcasted_iota(jnp.int32, (LANES, A), 0)
    lane = lax.broadcasted_iota(jnp.int32, (LANES, A), 1)
    return jnp.where(lane // DH == h, 1.0, 0.0).astype(BF16)


def _bf16_pieces(t, n):
    pieces = []
    for _ in range(n):
        piece = t.astype(BF16)
        pieces.append(piece)
        t = t - piece.astype(F32)
    return pieces


def _mix_fwd(os_, ls_, u, z, x, lng, lnb, sgu_w, bias_t, ga, gg, wout):
    s = x.shape[0]
    nd = len(DILS)
    nscr = sum(1 for d in DILS if d > 1)

    def body(*refs):
        o_refs, l_refs = refs[:nd], refs[nd:2 * nd]
        u_ref, z_ref, x_ref, lng_ref, lnb_ref, w_ref, bt_ref, ga_ref, gg_ref, wo_ref = refs[2 * nd:2 * nd + 10]
        attn_ref = refs[2 * nd + 10]
        lse_refs = refs[2 * nd + 11:3 * nd + 11]
        mixed_ref, h1_ref = refs[3 * nd + 11:3 * nd + 13]
        scr = refs[3 * nd + 13:]
        scr_o, scr_l, scr_lse = scr[:nscr], scr[nscr:2 * nscr], scr[2 * nscr]
        ov, lv, j = [], [], 0
        for di, dil in enumerate(DILS):
            if dil == 1:
                ov.append(o_refs[di][0].astype(F32))
                lv.append(l_refs[di][0])
            else:
                ov.append(_merge_residues(o_refs[di], scr_o[j], dil))
                lv.append(_merge_residues(l_refs[di], scr_l[j], dil))
                j += 1
        mx = functools.reduce(jnp.maximum, lv)
        es = [jnp.exp(l - mx) for l in lv]
        den = functools.reduce(lambda a, b: a + b, es)
        spread = _head_spread()
        attn = None
        for e, o in zip(es, ov):
            wide = functools.reduce(lambda a, b: a + b, [_dot(piece, spread) for piece in _bf16_pieces(e / den, 2)])
            attn = wide * o if attn is None else attn + wide * o
        attn_ref[...] = attn
        lse = mx + jnp.log(den)
        _fill_cols(scr_lse, lse)
        for di, dil in enumerate(DILS):
            if dil == 1:
                lse_refs[di][0] = lse
            else:
                _split_residues(scr_lse, lse_refs[di], dil)
        an, _, _ = _rms_fwd(attn, ga_ref[...])
        gmv, _, _, _, _, _ = _sgu_forward(u_ref[...], z_ref[...], lng_ref[...], lnb_ref[...], w_ref,
                                          bt_ref[...], _group_mean_matrix(), TMX)
        gn, _, _ = _rms_fwd(gmv, gg_ref[...])
        mixed = jnp.concatenate([an, gn], axis=-1).astype(BF16)
        mixed_ref[...] = mixed
        h1_ref[...] = x_ref[...] + _dot(mixed, wo_ref[...])

    sd = jax.ShapeDtypeStruct
    res = pl.pallas_call(
        body, name="mix_fwd", grid=(s // TMX,),
        in_specs=[_res_spec(d, TMX, A) for d in DILS] + [_res_spec(d, TMX, LANES) for d in DILS]
                 + [_row_spec(TMX, GW), _row_spec(TMX, GW),
                    _row_spec(TMX, D), _const_spec((1, GW)), _const_spec((1, GW)), _const_spec((NG, CHUNK, CHUNK)),
                    _const_spec((CHUNK, GW)), _const_spec((1, A)), _const_spec((1, GW)), _const_spec((D, D))],
        out_specs=[_row_spec(TMX, A)] + [_res_spec(d, TMX, LANES) for d in DILS]
                  + [_row_spec(TMX, D), _row_spec(TMX, D)],
        out_shape=[sd((s, A), F32)] + [_res_shape(s, d, LANES, F32) for d in DILS]
                  + [sd((s, D), BF16), sd((s, D), F32)],
        scratch_shapes=[_col_scratch(TMX, A)] * nscr + [_col_scratch(TMX, LANES)] * (nscr + 1),
        compiler_params=_cparams("arbitrary"),
    )(*os_, *ls_, u, z, x, lng, lnb, sgu_w, bias_t, ga, gg, wout)
    return res[0], res[1:1 + nd], res[1 + nd], res[2 + nd]


def _mlp_fwd(h1, g2, wff1, wff2, gf, target):
    s = h1.shape[0]

    def body(h1_ref, g2_ref, w1_ref, w2_ref, gf_ref, t_ref, hn_ref, rf_ref, dh2_ref, loss_ref, dgf_ref):
        i = pl.program_id(0)
        h1v = h1_ref[...]
        hn, _, _ = _rms_fwd(h1v, g2_ref[...])
        hn = hn.astype(BF16)
        hn_ref[...] = hn
        acc = h1v
        for j in range(DFF // FF_CH):
            cols = slice(j * FF_CH, (j + 1) * FF_CH)
            rf = jnp.maximum(_dot(hn, w1_ref[j]), 0.0)
            act = (rf * rf).astype(BF16)
            rf_ref[:, cols] = rf.astype(BF16)
            acc = acc + _dot(act, w2_ref[cols, :])
        y, h2n, r3 = _rms_fwd(acc, gf_ref[...])
        err = y - t_ref[...]
        part = 0.5 * jnp.sum(jnp.mean(err * err, axis=-1, keepdims=True), axis=0, keepdims=True)
        dy = err * (1.0 / D)
        dh2, dgf = _rms_bwd(dy, h2n, r3, gf_ref[...])
        dh2_ref[...] = dh2

        @pl.when(i == 0)
        def _():
            loss_ref[...] = jnp.zeros_like(loss_ref)
            dgf_ref[...] = jnp.zeros_like(dgf_ref)

        loss_ref[...] += jnp.broadcast_to(part, loss_ref.shape)
        dgf_ref[...] += dgf

    sd = jax.ShapeDtypeStruct
    return pl.pallas_call(
        body, name="mlp_fwd", grid=(s // TM,),
        in_specs=[_row_spec(TM, D), _const_spec((1, D)), _const_spec((DFF // FF_CH, D, FF_CH)), _const_spec((DFF, D)),
                  _const_spec((1, D)), _row_spec(TM, D)],
        out_specs=[_row_spec(TM, D), _row_spec(TM, DFF), _row_spec(TM, D),
                   _const_spec((1, LANES)), _const_spec((1, D))],
        out_shape=[sd((s, D), BF16), sd((s, DFF), BF16), sd((s, D), F32),
                   sd((1, LANES), F32), sd((1, D), F32)],
        compiler_params=_cparams("arbitrary"),
    )(h1, g2, wff1, wff2, gf, target)


def _mlp_bwd(dh2, rf, h1, g2, wff1, wff2):
    s = h1.shape[0]

    def body(dh2_ref, rf_ref, h1_ref, g2_ref, w1_ref, w2_ref, df_ref, dh1_ref, dg2_ref):
        i = pl.program_id(0)
        dh2v = dh2_ref[...]
        dh2b = dh2v.astype(BF16)
        dhn = jnp.zeros((TM, D), F32)
        for j in range(DFF // FF_CH):
            cols = slice(j * FF_CH, (j + 1) * FF_CH)
            da = _dot_nt(dh2b, w2_ref[cols, :])
            df = (da * (2.0 * rf_ref[:, cols].astype(F32))).astype(BF16)
            df_ref[:, cols] = df
            dhn = dhn + _dot_nt(df, w1_ref[j])
        _, h1n, r2 = _rms_fwd(h1_ref[...], g2_ref[...])
        dres, dg2 = _rms_bwd(dhn, h1n, r2, g2_ref[...])
        dh1_ref[...] = dh2v + dres

        @pl.when(i == 0)
        def _():
            dg2_ref[...] = jnp.zeros_like(dg2_ref)

        dg2_ref[...] += dg2

    sd = jax.ShapeDtypeStruct
    return pl.pallas_call(
        body, name="mlp_bwd", grid=(s // TM,),
        in_specs=[_row_spec(TM, D), _row_spec(TM, DFF), _row_spec(TM, D), _const_spec((1, D)),
                  _const_spec((DFF // FF_CH, D, FF_CH)), _const_spec((DFF, D))],
        out_specs=[_row_spec(TM, DFF), _row_spec(TM, D), _const_spec((1, D))],
        out_shape=[sd((s, DFF), BF16), sd((s, D), F32), sd((1, D), F32)],
        compiler_params=_cparams("arbitrary"),
    )(dh2, rf, h1, g2, wff1, wff2)


def _mix_bwd(dh1, attn, u, z, lng, lnb, sgu_w, sgu_wt, bias_t, ga, gg, wout):
    s = dh1.shape[0]
    nsteps = s // TMX
    nd = len(DILS)

    def body(*refs):
        dh1_ref, attn_ref, u_ref, z_ref, lng_ref, lnb_ref, w_ref, wt_ref, bt_ref, ga_ref, gg_ref, wo_ref = refs[:12]
        do_refs, dl_refs = refs[12:12 + nd], refs[12 + nd:12 + 2 * nd]
        (du_ref, dz_ref, dga_ref, dgg_ref, dlng_ref, dlnb_ref, dws_ref, db_ref,
         dbt_acc, scr_do, scr_dl) = refs[12 + 2 * nd:]
        i = pl.program_id(0)

        @pl.when(i == 0)
        def _():
            for r in (dga_ref, dgg_ref, dlng_ref, dlnb_ref, dws_ref, db_ref, dbt_acc):
                r[...] = jnp.zeros_like(r)

        dmixed = _dot_nt(dh1_ref[...].astype(BF16), wo_ref[...])
        attn = attn_ref[...]
        _, an, ra = _rms_fwd(attn, ga_ref[...])
        dattn, dga = _rms_bwd(dmixed[:, :A], an, ra, ga_ref[...])
        dga_ref[...] += dga
        _fill_cols(scr_do, dattn)
        spread = _head_spread()
        delta = functools.reduce(lambda a, b: a + b, [_dot_nt(piece, spread) for piece in _bf16_pieces(dattn * attn, 3)])
        _fill_cols(scr_dl, delta)
        for di, dil in enumerate(DILS):
            if dil == 1:
                do_refs[di][0] = dattn.astype(BF16)
                dl_refs[di][0] = delta
            else:
                _split_residues(scr_do, do_refs[di], dil)
                _split_residues(scr_dl, dl_refs[di], dil)
        pmat = _group_mean_matrix()
        lng = lng_ref[...]
        uv, zv = u_ref[...], z_ref[...]
        gmv, ug, zhat, rstd, zn, mixed = _sgu_forward(uv, zv, lng, lnb_ref[...], w_ref, bt_ref[...], pmat, TMX)
        _, gmn, rg = _rms_fwd(gmv, gg_ref[...])
        dgm, dgg = _rms_bwd(dmixed[:, A:], gmn, rg, gg_ref[...])
        dgg_ref[...] += dgg
        du_ref[...] = (dgm * mixed * _gelu_grad(uv)).astype(BF16)
        dmx = dgm * ug
        dmxb = dmx.astype(BF16)
        gm = _group_masks(GW)
        tri_t = _tri_mask(False)
        wst = [jnp.where(tri_t, wt_ref[g], 0.0).astype(BF16) for g in range(NG)]
        zero = jnp.zeros((CHUNK, GW), BF16)
        dzn_pieces = []
        for c in range(TMX // CHUNK):
            rs = slice(c * CHUNK, (c + 1) * CHUNK)
            dmc = dmxb[rs, :]
            znc = zn[rs, :]
            dbt_acc[...] += dmx[rs, :]
            dzn = None
            for g in range(NG):
                dws_ref[g] += _dot_nt(jnp.where(gm[g], dmc, zero), znc)
                part = jnp.where(gm[g], _dot(wst[g], dmc), 0.0)
                dzn = part if dzn is None else dzn + part
            dzn_pieces.append(dzn)
        dzn = jnp.concatenate(dzn_pieces, axis=0)
        dlng_ref[...] += jnp.sum(dzn * zhat, axis=0, keepdims=True)
        dlnb_ref[...] += jnp.sum(dzn, axis=0, keepdims=True)
        dzh = dzn * lng
        dzg = rstd * (dzh - _dot_hi(dzh, pmat) - zhat * _dot_hi(dzh * zhat, pmat))
        dz_ref[...] = (dzg * _gelu_grad(zv)).astype(BF16)

        @pl.when(i == nsteps - 1)
        def _():
            tri = _tri_mask(True)
            for g in range(NG):
                dws_ref[g] = jnp.where(tri, dws_ref[g], 0.0)
            acc = dbt_acc[...]
            lane = lax.broadcasted_iota(jnp.int32, (CHUNK, LANES), 1)
            out = jnp.zeros((CHUNK, LANES), F32)
            for g in range(NG):
                sg = jnp.sum(jnp.where(gm[g], acc, 0.0), axis=-1, keepdims=True)
                out = jnp.where(lane == g, sg, out)
            db_ref[...] = out

    sd = jax.ShapeDtypeStruct
    res = pl.pallas_call(
        body, name="mix_bwd", grid=(nsteps,),
        in_specs=[_row_spec(TMX, D), _row_spec(TMX, A), _row_spec(TMX, GW), _row_spec(TMX, GW),
                  _const_spec((1, GW)), _const_spec((1, GW)), _const_spec((NG, CHUNK, CHUNK)),
                  _const_spec((NG, CHUNK, CHUNK)), _const_spec((CHUNK, GW)), _const_spec((1, A)),
                  _const_spec((1, GW)), _const_spec((D, D))],
        out_specs=[_res_spec(d, TMX, A) for d in DILS] + [_res_spec(d, TMX, LANES) for d in DILS]
                  + [_row_spec(TMX, GW), _row_spec(TMX, GW),
                   _const_spec((1, A)), _const_spec((1, GW)), _const_spec((1, GW)), _const_spec((1, GW)),
                   _const_spec((NG, CHUNK, CHUNK)), _const_spec((CHUNK, LANES))],
        out_shape=[_res_shape(s, d, A, BF16) for d in DILS] + [_res_shape(s, d, LANES, F32) for d in DILS]
                  + [sd((s, GW), BF16), sd((s, GW), BF16),
                   sd((1, A), F32), sd((1, GW), F32), sd((1, GW), F32), sd((1, GW), F32),
                   sd((NG, CHUNK, CHUNK), F32), sd((CHUNK, LANES), F32)],
        scratch_shapes=[pltpu.VMEM((CHUNK, GW), F32), _col_scratch(TMX, A), _col_scratch(TMX, LANES)],
        compiler_params=_cparams("arbitrary"),
    )(dh1, attn, u, z, lng, lnb, sgu_w, sgu_wt, bias_t, ga, gg, wout)
    return (res[:nd], res[nd:2 * nd]) + tuple(res[2 * nd:])


def _dproj_merge(dqs, dks, dvs, du, dz, pin):
    s = du.shape[0]
    nd = len(DILS)
    nscr = sum(1 for d in DILS if d > 1)

    def body(*refs):
        pin_ref = refs[0]
        parts = [refs[1 + t * nd:1 + (t + 1) * nd] for t in range(3)]
        du_ref, dz_ref, dp_ref = refs[1 + 3 * nd:4 + 3 * nd]
        scr = refs[4 + 3 * nd:]
        sums = []
        for t in range(3):
            total, j = None, 0
            for di, dil in enumerate(DILS):
                if dil == 1:
                    term = parts[t][di][0].astype(F32)
                else:
                    term = _merge_residues(parts[t][di], scr[t * nscr + j], dil)
                    j += 1
                total = term if total is None else total + term
            sums.append(total)
        dp_ref[...] = jnp.concatenate([sums[0] * SCALE, sums[1], sums[2], du_ref[...].astype(F32) + pin_ref[0, 0],
                                       dz_ref[...].astype(F32)], axis=-1).astype(BF16)

    return pl.pallas_call(
        body, name="dproj_merge", grid=(s // TMX,),
        in_specs=[pl.BlockSpec(memory_space=pltpu.SMEM)] + [_res_spec(d, TMX, A) for d in DILS] * 3
                 + [_row_spec(TMX, GW)] * 2,
        out_specs=_row_spec(TMX, INW), out_shape=jax.ShapeDtypeStruct((s, INW), BF16),
        scratch_shapes=[_col_scratch(TMX, A)] * (3 * nscr),
        compiler_params=_cparams("arbitrary"),
    )(pin, *dqs, *dks, *dvs, du, dz)


def _inproj_bwd(dproj, dh1, x, g1, win_t):
    s = x.shape[0]

    def body(dp_ref, dh1_ref, x_ref, g_ref, w_ref, dx_ref, dg_ref):
        i = pl.program_id(0)
        dhn = _dot(dp_ref[...], w_ref[...])
        _, xn, r1 = _rms_fwd(x_ref[...], g_ref[...])
        dres, dg = _rms_bwd(dhn, xn, r1, g_ref[...])
        dx_ref[...] = dh1_ref[...] + dres

        @pl.when(i == 0)
        def _():
            dg_ref[...] = jnp.zeros_like(dg_ref)

        dg_ref[...] += dg

    sd = jax.ShapeDtypeStruct
    return pl.pallas_call(
        body, name="inproj_bwd", grid=(s // TM,),
        in_specs=[_row_spec(TM, INW), _row_spec(TM, D), _row_spec(TM, D), _const_spec((1, D)), _const_spec((INW, D))],
        out_specs=[_row_spec(TM, D), _const_spec((1, D))],
        out_shape=[sd((s, D), F32), sd((1, D), F32)],
        compiler_params=_cparams("arbitrary"),
    )(dproj, dh1, x, g1, win_t)


def _wgrad(a, b, name, bm, bn, bk=4 * TM, square_a=False, also_bf16=False):
    s, m = a.shape
    n = b.shape[1]
    bm, bn = min(bm, m), min(bn, n)
    nk = s // bk

    def body(a_ref, b_ref, o_ref, *low):
        @pl.when(pl.program_id(2) == 0)
        def _():
            o_ref[...] = jnp.zeros_like(o_ref)

        av = a_ref[...]
        if square_a:
            av = av.astype(F32)
            av = av * av
        o_ref[...] += _dot_tn(av.astype(BF16), b_ref[...].astype(BF16))
        if also_bf16:
            @pl.when(pl.program_id(2) == nk - 1)
            def _():
                low[0][...] = o_ref[...].astype(BF16)

    out_spec = pl.BlockSpec((bm, bn), lambda i, j, k: (i, j))
    res = pl.pallas_call(
        body, name=name, grid=(m // bm, n // bn, nk),
        in_specs=[pl.BlockSpec((bk, bm), lambda i, j, k: (k, i)), pl.BlockSpec((bk, bn), lambda i, j, k: (k, j))],
        out_specs=[out_spec, out_spec] if also_bf16 else out_spec,
        out_shape=([jax.ShapeDtypeStruct((m, n), F32), jax.ShapeDtypeStruct((m, n), BF16)] if also_bf16
                   else jax.ShapeDtypeStruct((m, n), F32)),
        compiler_params=_cparams("arbitrary", "arbitrary", "arbitrary"),
    )(a, b)
    return res


def _adamw_math(w, g, m, v):
    m = B1 * m + (1.0 - B1) * g
    v = B2 * v + (1.0 - B2) * (g * g)
    m_hat = m / (1.0 - B1 ** STEP)
    v_hat = v / (1.0 - B2 ** STEP)
    delta = -LR * (m_hat / (jnp.sqrt(v_hat) + AEPS) + WD * w)
    return delta, m, v


def _adamw(w, g, m, v, name):
    rows, cols = w.shape
    br = min(rows, 256)
    while rows % br:
        br -= 8

    def body(w_ref, g_ref, m_ref, v_ref, d_ref, mo_ref, vo_ref):
        d, mn, vn = _adamw_math(w_ref[...], g_ref[...], m_ref[...], v_ref[...])
        d_ref[...] = d
        mo_ref[...] = mn
        vo_ref[...] = vn

    spec = _row_spec(br, cols)
    sd = jax.ShapeDtypeStruct((rows, cols), F32)
    return pl.pallas_call(
        body, name=name, grid=(rows // br,), in_specs=[spec] * 4, out_specs=[spec] * 3,
        out_shape=[sd, sd, sd], compiler_params=_cparams("arbitrary"),
    )(w, g, m, v)


def _local_step(x, hn1, target, small, win_t, rest_weights, early_grads=None, after_attention_bwd=None,
                late_grads=None):
    slopes = jnp.asarray(_alibi_slopes(NH))
    q, k, v, u, z = _inproj_fwd(hn1, win_t)
    outs, lses = [], []
    for i, dil in enumerate(DILS):
        o, l = _attn_fwd(q[i], k[i], v[i], slopes, dil)
        outs.append(o)
        lses.append(l)
    wout, wff1, wff2 = rest_weights(functools.reduce(lambda a, b: a + b, [l[0, 0:8, :] for l in lses]))
    attn, lse, mixed, h1 = _mix_fwd(outs, lses, u, z, x, small["ln_g"], small["ln_b"], small["sgu_w"],
                                    small["bias_t"], small["attn_out_g"], small["gmlp_out_g"], wout)
    hn2, rf, dh2, loss, dgf = _mlp_fwd(h1, small["norm2_g"], wff1, wff2, small["final_norm_g"], target)
    df, dh1, dg2 = _mlp_bwd(dh2, rf, h1, small["norm2_g"], wff1, wff2)
    gwff1 = _wgrad(hn2, df, "wgrad_ff1", D, 1024)
    gwff2 = _wgrad(rf, dh2, "wgrad_ff2", 1024, D, square_a=True)
    gwout = _wgrad(mixed, dh1, "wgrad_out", D, D)
    ga, g1 = small["attn_out_g"], small["norm1_g"]
    pin = early_grads(gwff1, gwff2, gwout) if early_grads else None
    if pin is not None:
        ga = ga + pin
    (do, delta, du, dz, dga, dgg, dlng, dlnb, dws, db) = _mix_bwd(
        dh1, attn, u, z, small["ln_g"], small["ln_b"], small["sgu_w"], small["sgu_wt"], small["bias_t"],
        ga, small["gmlp_out_g"], wout)
    dqs, dks, dvs = [], [], []
    for i, dil in enumerate(DILS):
        dqs.append(_attn_bwd_dq(q[i], k[i], v[i], do[i], lse[i], delta[i], slopes, dil))
        dk, dv = _attn_bwd_dkv(q[i], k[i], v[i], do[i], lse[i], delta[i], slopes, dil)
        dks.append(dk)
        dvs.append(dv)
    marker = functools.reduce(lambda a, b: a + b, [t[0, 0:8, 0:LANES] for t in dqs + dks + dvs])
    partial = dict(ln_g=dlng, ln_b=dlnb, sgu_w=dws, sgu_b=db[:, :NG].T, attn_out_g=dga, gmlp_out_g=dgg,
                   norm2_g=dg2, final_norm_g=dgf)
    pin = after_attention_bwd(marker, partial, loss[0, 0]) if after_attention_bwd else None
    dproj = _dproj_merge(dqs, dks, dvs, du, dz, jnp.zeros((1, 1), F32) if pin is None else pin)
    gwin_t, gwin_low = _wgrad(dproj, hn1, "wgrad_in", INW // 2, D, also_bf16=True)
    pin = late_grads(gwin_t, gwin_low) if late_grads else None
    if pin is not None:
        g1 = g1 + pin
    dx, dg1 = _inproj_bwd(dproj, dh1, x, g1, win_t)
    small_grads = dict(partial, norm1_g=dg1)
    return loss[0, 0], dx, small_grads, (gwin_t, gwout, gwff1, gwff2)


ANY = pl.BlockSpec(memory_space=pl.ANY)
NDEV = 8


def _position():
    return lax.axis_index("x"), lax.axis_index("y"), lax.axis_index("c")


def _other_chips(x, y):
    return [(1 - x, y), (x, 1 - y), (1 - x, 1 - y)]


def _remote(src, dst, send_sem, recv_sem, device):
    return pltpu.make_async_remote_copy(src_ref=src, dst_ref=dst, send_sem=send_sem, recv_sem=recv_sem,
                                        device_id=device, device_id_type=MESH)


HBM = pl.BlockSpec(memory_space=pltpu.HBM)
SEM = pl.BlockSpec(memory_space=pltpu.SEMAPHORE)
DATAFLOW = pltpu.SideEffectType.DATAFLOW_SIDE_EFFECTING


def _in_hbm(a):
    return pltpu.with_memory_space_constraint(a, pltpu.HBM)


def _gather_start(shards, name):
    n = len(shards)
    lands = [jnp.broadcast_to(sh[None], (NCHIP,) + sh.shape) for sh in shards]

    def body(*refs):
        w_refs, land_refs = refs[:n], refs[n:2 * n]
        send_sems, recv_sems = refs[2 * n:2 * n + 2]
        token = refs[-1]
        x, y, c = _position()
        for w in range(n):
            for k, (px, py) in enumerate(_other_chips(x, y)):
                m = 3 * w + k
                _remote(w_refs[w], land_refs[w].at[2 * x + y], send_sems.at[m], recv_sems.at[m], (px, py, c)).start()
        token[...] = jnp.zeros_like(token)

    res = _split_call(body, name, list(shards) + lands, (3 * n, 3 * n), (TOKEN,))
    return res[0], res[1], res[2:2 + n], res[2 + n:2 + 2 * n], res[-1]


def _gather_wait(send_sems, recv_sems, shards, lands, after, name):
    n = len(shards)

    def body(*refs):
        w_refs, land_refs = refs[:n], refs[n:2 * n]
        send_sems, recv_sems = refs[2 * n:2 * n + 2]
        x, y, c = _position()
        for w in range(n):
            for k, (px, py) in enumerate(_other_chips(x, y)):
                m = 3 * w + k
                cp = _remote(w_refs[w], land_refs[w].at[2 * px + py], send_sems.at[m], recv_sems.at[m], (px, py, c))
                cp.wait_send()
                cp.wait_recv()

    operands = list(shards) + list(lands)
    res = pl.pallas_call(
        body, name=name, out_shape=tuple(pltpu.HBM(a.shape, a.dtype) for a in operands),
        in_specs=(HBM,) * (2 * n) + (SEM, SEM, ANY), out_specs=(HBM,) * (2 * n),
        input_output_aliases={i: i for i in range(2 * n)},
        compiler_params=pltpu.CompilerParams(has_side_effects=DATAFLOW),
    )(*operands, send_sems, recv_sems, after)
    return res[n:]


def _xor_peers(x, y, c):
    peers = []
    for k in range(1, NDEV):
        kx, ky, kc = (k >> 2) & 1, (k >> 1) & 1, k & 1
        peers.append((1 - x if kx else x, 1 - y if ky else y, 1 - c if kc else c))
    return peers


def _piece(part_ref, px, py, pc):
    slab = 2 * px + py
    if len(part_ref.shape) == 3:
        half = part_ref.shape[1] // 2
        return part_ref.at[slab, pl.ds(pc * half, half), :]
    half = part_ref.shape[0] // 2
    return part_ref.at[pl.ds(pc * half, half), pl.ds(pl.multiple_of(slab * D, D), D)]


def _split_call(body, name, operands, n_sems, extra_out=()):
    n = len(operands)
    sems = tuple(pltpu.SemaphoreType.DMA((m,)) for m in n_sems)
    thru = tuple(pltpu.HBM(a.shape, a.dtype) for a in operands)
    return pl.pallas_call(
        body, name=name, out_shape=sems + thru + tuple(extra_out),
        in_specs=(HBM,) * n,
        out_specs=(SEM,) * len(sems) + (HBM,) * n + (pl.BlockSpec(memory_space=pltpu.VMEM),) * len(extra_out),
        input_output_aliases={i: len(sems) + i for i in range(n)},
        compiler_params=pltpu.CompilerParams(has_side_effects=DATAFLOW),
    )(*[_in_hbm(a) for a in operands])


TOKEN = jax.ShapeDtypeStruct((8, LANES), F32)


def _pack_copies(pack_ref, land_ref, send_sems, recv_sems, base, position, start):
    x, y, c = position
    for k, (px, py, pc) in enumerate(_xor_peers(x, y, c)):
        if start:
            _remote(pack_ref, land_ref.at[4 * x + 2 * y + c], send_sems.at[base + k], recv_sems.at[base + k],
                    (px, py, pc)).start()
        else:
            cp = _remote(pack_ref, land_ref.at[4 * px + 2 * py + pc], send_sems.at[base + k], recv_sems.at[base + k],
                         (px, py, pc))
            cp.wait_send()
            cp.wait_recv()


def _pack_landing(pack):
    return jnp.broadcast_to(pack[None], (NDEV,) + pack.shape)


def _reduce_start(parts, name, pack=None):
    nw = len(parts)
    lands = [lax.empty((NDEV - 1, p.shape[-2] // 2, D), p.dtype) for p in parts]
    operands = list(parts) + lands + ([pack, _pack_landing(pack)] if pack is not None else [])
    nops = len(operands)

    def body(*refs):
        part_refs, land_refs = refs[:nw], refs[nw:2 * nw]
        send_sems, recv_sems = refs[nops:nops + 2]
        token = refs[-1]
        x, y, c = _position()
        for w in range(nw):
            for k, peer in enumerate(_xor_peers(x, y, c)):
                n = w * (NDEV - 1) + k
                _remote(_piece(part_refs[w], *peer), land_refs[w].at[k], send_sems.at[n], recv_sems.at[n],
                        peer).start()
        if pack is not None:
            _pack_copies(refs[2 * nw], refs[2 * nw + 1], send_sems, recv_sems, nw * (NDEV - 1), (x, y, c), True)
        token[...] = jnp.zeros_like(token)

    n = (nw + (pack is not None)) * (NDEV - 1)
    res = _split_call(body, name, operands, (n, n), (TOKEN,))
    return res[0], res[1], res[2:2 + nops], res[-1]


def _reduce_wait(send_sems, recv_sems, operands, nw, after, name):
    nops = len(operands)
    has_pack = nops > 2 * nw

    def body(*refs):
        part_refs, land_refs = refs[:nw], refs[nw:2 * nw]
        send_sems, recv_sems = refs[nops:nops + 2]
        x, y, c = _position()
        for w in range(nw):
            for k, peer in enumerate(_xor_peers(x, y, c)):
                n = w * (NDEV - 1) + k
                cp = _remote(_piece(part_refs[w], *peer), land_refs[w].at[k], send_sems.at[n], recv_sems.at[n], peer)
                cp.wait_send()
                cp.wait_recv()
        if has_pack:
            _pack_copies(refs[2 * nw], refs[2 * nw + 1], send_sems, recv_sems, nw * (NDEV - 1), (x, y, c), False)

    res = pl.pallas_call(
        body, name=name, out_shape=tuple(pltpu.HBM(a.shape, a.dtype) for a in operands),
        in_specs=(HBM,) * nops + (SEM, SEM, ANY), out_specs=(HBM,) * nops,
        input_output_aliases={i: i for i in range(nops)},
        compiler_params=pltpu.CompilerParams(has_side_effects=DATAFLOW),
    )(*operands, send_sems, recv_sems, after)
    return res[:nw], res[nw:2 * nw], (res[2 * nw + 1] if has_pack else None)


def _sum_pieces(part, land, sel, name):
    half = part.shape[-2] // 2
    br = 128 if half % 128 == 0 else half // 2
    nb = half // br

    def body(sel_ref, own_ref, *refs):
        acc = own_ref[...]
        for r in refs[:NDEV - 1]:
            acc = acc + r[...].astype(F32)
        refs[NDEV - 1][...] = acc

    if part.ndim == 3:
        own_spec = pl.BlockSpec((None, br, D), lambda i, sel_ref: (sel_ref[0], sel_ref[1] * nb + i, 0))
    else:
        own_spec = pl.BlockSpec((br, D), lambda i, sel_ref: (sel_ref[1] * nb + i, sel_ref[0]))
    slot_specs = [pl.BlockSpec((None, br, D), functools.partial(lambda i, sel_ref, k: (k, i, 0), k=k))
                  for k in range(NDEV - 1)]
    return pl.pallas_call(
        body, name=name,
        grid_spec=pltpu.PrefetchScalarGridSpec(
            num_scalar_prefetch=1, grid=(nb,), in_specs=[own_spec] + slot_specs,
            out_specs=pl.BlockSpec((br, D), lambda i, sel_ref: (i, 0))),
        out_shape=jax.ShapeDtypeStruct((half, D), F32),
        compiler_params=_cparams("arbitrary"),
    )(sel, part, *([land] * (NDEV - 1)))


def _share_start(halves, name, pack=None):
    nw = len(halves)
    lands = [lax.empty(h.shape, F32) for h in halves]
    operands = list(halves) + lands + ([pack, _pack_landing(pack)] if pack is not None else [])
    nops = len(operands)

    def body(*refs):
        h_refs, land_refs = refs[:nw], refs[nw:2 * nw]
        send_sems, recv_sems = refs[nops:nops + 2]
        token = refs[-1]
        x, y, c = _position()
        for w in range(nw):
            _remote(h_refs[w], land_refs[w], send_sems.at[w], recv_sems.at[w], (x, y, 1 - c)).start()
        if pack is not None:
            _pack_copies(refs[2 * nw], refs[2 * nw + 1], send_sems, recv_sems, nw, (x, y, c), True)
        token[...] = jnp.zeros_like(token)

    n = nw + (NDEV - 1 if pack is not None else 0)
    res = _split_call(body, name, operands, (n, n), (TOKEN,))
    return res[0], res[1], res[2:2 + nops], res[-1]


def _share_wait(send_sems, recv_sems, operands, nw, after, name):
    nops = len(operands)
    has_pack = nops > 2 * nw

    def body(*refs):
        h_refs, land_refs = refs[:nw], refs[nw:2 * nw]
        send_sems, recv_sems = refs[nops:nops + 2]
        x, y, c = _position()
        for w in range(nw):
            cp = _remote(h_refs[w], land_refs[w], send_sems.at[w], recv_sems.at[w], (x, y, 1 - c))
            cp.wait_send()
            cp.wait_recv()
        if has_pack:
            _pack_copies(refs[2 * nw], refs[2 * nw + 1], send_sems, recv_sems, nw, (x, y, c), False)

    res = pl.pallas_call(
        body, name=name, out_shape=tuple(pltpu.HBM(a.shape, a.dtype) for a in operands),
        in_specs=(HBM,) * nops + (SEM, SEM, ANY), out_specs=(HBM,) * nops,
        input_output_aliases={i: i for i in range(nops)},
        compiler_params=pltpu.CompilerParams(has_side_effects=DATAFLOW),
    )(*operands, send_sems, recv_sems, after)
    return res[:nw], res[nw:2 * nw], (res[2 * nw + 1] if has_pack else None)


def _join_halves(own, other, c):
    first = jnp.where(c == 0, own, other)
    second = jnp.where(c == 0, other, own)
    return jnp.concatenate([first, second], axis=0)


SMALL_SIZES = (("norm1_g", D), ("sgu_ln_g", GW), ("sgu_ln_b", GW), ("sgu_w", NG * CHUNK * CHUNK),
               ("sgu_b", NG * CHUNK), ("attn_out_g", A), ("gmlp_out_g", GW), ("norm2_g", D),
               ("final_norm_g", D))
PARAM_ROWS = sum(n for _, n in SMALL_SIZES) // LANES
SMALL_ROWS = PARAM_ROWS + 8


def _pack_small(tree, first_extra=None):
    extra = jnp.zeros((8 * LANES,), F32)
    if first_extra is not None:
        extra = extra.at[0].set(first_extra)
    flat = jnp.concatenate([tree[n].reshape(-1) for n, _ in SMALL_SIZES] + [extra])
    return flat.reshape(SMALL_ROWS, LANES)


def _unpack_small(pack, shapes):
    flat = pack.reshape(-1)
    out, off = {}, 0
    for n, size in SMALL_SIZES:
        out[n] = flat[off:off + size].reshape(shapes[n])
        off += size
    return out


def _small_finish(pack_land, norm_land, wpack, mpack, vpack):
    def body(p_ref, n_ref, w_ref, m_ref, v_ref, go_ref, d_ref, mo_ref, vo_ref):
        total = p_ref[0]
        late = n_ref[0]
        for k in range(1, NDEV):
            total = total + p_ref[k]
            late = late + n_ref[k]
        go_ref[...] = total
        go_ref[0:8, :] = total[0:8, :] + late
        d, mn, vn = _adamw_math(w_ref[...], go_ref[...], m_ref[...], v_ref[...])
        d_ref[...] = d
        mo_ref[...] = mn
        vo_ref[...] = vn

    sd = jax.ShapeDtypeStruct((SMALL_ROWS, LANES), F32)
    vm = pl.BlockSpec(memory_space=pltpu.VMEM)
    return pl.pallas_call(
        body, name="small_finish", in_specs=[vm] * 5, out_specs=[vm] * 4, out_shape=[sd] * 4,
        compiler_params=_cparams(),
    )(pack_land, norm_land, wpack, mpack, vpack)


def kernel(x, norm1_g, w_in, sgu_ln_g, sgu_ln_b, sgu_w, sgu_b, attn_out_g, gmlp_out_g, w_out, norm2_g, w_ff1, w_ff2, final_norm_g, loss_target, m_norm1_g, m_w_in, m_sgu_ln_g, m_sgu_ln_b, m_sgu_w, m_sgu_b, m_attn_out_g, m_gmlp_out_g, m_w_out, m_norm2_g, m_w_ff1, m_w_ff2, m_final_norm_g, v_norm1_g, v_w_in, v_sgu_ln_g, v_sgu_ln_b, v_sgu_w, v_sgu_b, v_attn_out_g, v_gmlp_out_g, v_w_out, v_norm2_g, v_w_ff1, v_w_ff2, v_final_norm_g):
    names = [n for n, _ in SMALL_SIZES]
    w_small = dict(norm1_g=norm1_g, sgu_ln_g=sgu_ln_g, sgu_ln_b=sgu_ln_b, sgu_w=sgu_w, sgu_b=sgu_b,
                   attn_out_g=attn_out_g, gmlp_out_g=gmlp_out_g, norm2_g=norm2_g, final_norm_g=final_norm_g)
    m_small = dict(norm1_g=m_norm1_g, sgu_ln_g=m_sgu_ln_g, sgu_ln_b=m_sgu_ln_b, sgu_w=m_sgu_w, sgu_b=m_sgu_b,
                   attn_out_g=m_attn_out_g, gmlp_out_g=m_gmlp_out_g, norm2_g=m_norm2_g,
                   final_norm_g=m_final_norm_g)
    v_small = dict(norm1_g=v_norm1_g, sgu_ln_g=v_sgu_ln_g, sgu_ln_b=v_sgu_ln_b, sgu_w=v_sgu_w, sgu_b=v_sgu_b,
                   attn_out_g=v_attn_out_g, gmlp_out_g=v_gmlp_out_g, norm2_g=v_norm2_g,
                   final_norm_g=v_final_norm_g)
    shapes = {n: w_small[n].shape for n in names}

    start_in = _gather_start([w_in[0].T.astype(BF16)], "gather_in_start")
    issued = start_in[4][0:1, 0:1]
    start_rest = _gather_start([(w_out[0] + issued).astype(BF16), w_ff1[0].astype(BF16), w_ff2[0].astype(BF16)],
                               "gather_rest_start")
    hn1 = _norm1(x[0], norm1_g + start_rest[4][0:1, 0:1])
    win_t = _gather_wait(*start_in[:4], after=hn1, name="gather_in_wait")[0].reshape(INW, D)

    def rest_weights(after):
        wout, wff1, wff2 = _gather_wait(*start_rest[:4], after=after, name="gather_rest_wait")
        return wout.reshape(D, D), wff1, wff2.reshape(DFF, D)

    small = dict(
        norm1_g=norm1_g, ln_g=sgu_ln_g.reshape(1, GW), ln_b=sgu_ln_b.reshape(1, GW), sgu_w=sgu_w[0],
        sgu_wt=jnp.swapaxes(sgu_w[0], 1, 2), bias_t=jnp.repeat(sgu_b[0].T, DH, axis=1),
        attn_out_g=attn_out_g, gmlp_out_g=gmlp_out_g, norm2_g=norm2_g, final_norm_g=final_norm_g.reshape(1, D))
    xi, yi, ci = _position()
    sel = jnp.stack([2 * xi + yi, ci]).astype(jnp.int32)
    state = {}

    def as_slabs(g):
        return g.reshape(NCHIP, g.shape[0] // NCHIP, D)

    def early_grads(gwff1, gwff2, gwout):
        state["early"] = _reduce_start([gwff1, as_slabs(gwff2), as_slabs(gwout)], "reduce_early_start")
        return state["early"][3][0:1, 0:1]

    def after_attention_bwd(marker, partial, loss_part):
        send_sems, recv_sems, operands, _ = state["early"]
        parts, lands, _ = _reduce_wait(send_sems, recv_sems, operands, 3, marker, "reduce_early_wait")
        halves = [_sum_pieces(p, l, sel, "sum_" + n) for p, l, n in zip(parts, lands, ("w_ff1", "w_ff2", "w_out"))]
        pack = _pack_small(dict(partial, norm1_g=jnp.zeros((1, D), F32), sgu_ln_g=partial["ln_g"],
                                sgu_ln_b=partial["ln_b"]), loss_part)
        state["early_share"] = _share_start(halves, "share_early_start", pack)
        return state["early_share"][3][0:1, 0:1]

    def late_grads(gwin_t, gwin_low):
        state["late"] = _reduce_start([as_slabs(gwin_low)], "reduce_late_start")
        state["late_own"] = as_slabs(gwin_t)
        return state["late"][3][0:1, 0:1]

    _, dx, sg, _ = _local_step(
        x[0], hn1, loss_target[0], small, win_t, rest_weights, early_grads, after_attention_bwd, late_grads)
    send_sems, recv_sems, operands, _ = state["early_share"]
    own, other, pack_land = _share_wait(send_sems, recv_sems, operands, 3, dx, "share_early_wait")
    send_sems, recv_sems, operands, _ = state["late"]
    _, late_lands, _ = _reduce_wait(send_sems, recv_sems, operands, 1, dx, "reduce_late_wait")
    late_share = _share_start([_sum_pieces(state["late_own"], late_lands[0], sel, "sum_w_in")], "share_late_start",
                              sg["norm1_g"].reshape(8, LANES))
    issued = late_share[3][0:1, 0:1]
    g_big = {n: _join_halves(o, t, ci) + issued for n, o, t in zip(("w_ff1", "w_ff2", "w_out"), own, other)}
    w_big = dict(w_in=(w_in, m_w_in, v_w_in), w_out=(w_out, m_w_out, v_w_out),
                 w_ff1=(w_ff1, m_w_ff1, v_w_ff1), w_ff2=(w_ff2, m_w_ff2, v_w_ff2))
    grads, deltas, new_m, new_v = {}, {}, {}, {}

    def update(n):
        w, m, v = w_big[n]
        d, mn, vn = _adamw(w[0], g_big[n], m[0], v[0], "adamw_" + n)
        grads[n], deltas[n], new_m[n], new_v[n] = g_big[n][None], d[None], mn[None], vn[None]

    for n in ("w_ff1", "w_ff2", "w_out"):
        update(n)
    updated = deltas["w_out"][0, 0:8, 0:LANES] + deltas["w_ff1"][0, 0:8, 0:LANES] + deltas["w_ff2"][0, 0:8, 0:LANES]
    own, other, norm_land = _share_wait(late_share[0], late_share[1], late_share[2], 1, updated, "share_late_wait")
    g_big["w_in"] = _join_halves(own[0], other[0], ci).T
    update("w_in")

    packs = _small_finish(pack_land, norm_land, _pack_small(w_small), _pack_small(m_small), _pack_small(v_small))
    loss = packs[0][PARAM_ROWS, 0]
    for tree, pack in zip((grads, deltas, new_m, new_v), packs):
        tree.update(_unpack_small(pack, shapes))

    order = ["norm1_g", "w_in", "sgu_ln_g", "sgu_ln_b", "sgu_w", "sgu_b", "attn_out_g", "gmlp_out_g", "w_out",
             "norm2_g", "w_ff1", "w_ff2", "final_norm_g"]
    return (loss, dx[None], *[grads[n] for n in order], *[deltas[n] for n in order],
            *[new_m[n] for n in order], *[new_v[n] for n in order])
```

```python
import functools
import math

import numpy as np
import jax
import jax.numpy as jnp
from jax import lax
from jax.experimental import pallas as pl
from jax.experimental.pallas import tpu as pltpu

F32 = jnp.float32
BF16 = jnp.bfloat16

D = 1024
NH = 12
DH = 64
A = NH * DH
NG = 4
GW = NG * DH
INW = 3 * A + 2 * GW
DFF = 4 * D
CHUNK = 128
PATTERNS = ((128, 1), (512, 4), (2048, 16))
EPS = 1e-6
SCALE = DH ** -0.5
NEG = -1e30

LR, B1, B2, AEPS, WD, STEP = 0.001, 0.9, 0.999, 1e-08, 0.01, 10

TM = 512
TM_BIG = 1024
TMX = 512
ATT_ROWS = 4096
FF_CH = 1024
LANES = 128
NCHIP = 4
VMEM_LIMIT = 56 * 1024 * 1024
MESH = pl.DeviceIdType.MESH


def _cparams(*sem, **kw):
    return pltpu.CompilerParams(dimension_semantics=sem if sem else None,
                                vmem_limit_bytes=VMEM_LIMIT, **kw)


def _dot(a, b):
    return jnp.dot(a, b, preferred_element_type=F32)


def _dot_nt(a, b):
    return lax.dot_general(a, b, (((1,), (1,)), ((), ())), preferred_element_type=F32)


def _dot_tn(a, b):
    return lax.dot_general(a, b, (((0,), (0,)), ((), ())), preferred_element_type=F32)


def _dot_hi(a, b):
    return jnp.dot(a, b, preferred_element_type=F32, precision=lax.Precision.HIGHEST)


def _alibi_slopes(n):
    def pow2(m):
        start = 2.0 ** (-8.0 / m)
        return [start ** (i + 1) for i in range(m)]
    if math.log2(n).is_integer():
        s = pow2(n)
    else:
        c = 2 ** int(math.floor(math.log2(n)))
        s = pow2(c) + pow2(2 * c)[0::2][: n - c]
    return np.asarray(s, dtype=np.float32)


def _rms_fwd(v, g):
    r = lax.rsqrt(jnp.mean(v * v, axis=-1, keepdims=True) + EPS)
    vn = v * r
    return vn * g, vn, r


def _rms_bwd(dy, vn, r, g):
    w = dy * g
    dv = r * (w - vn * jnp.mean(w * vn, axis=-1, keepdims=True))
    return dv, jnp.sum(dy * vn, axis=0, keepdims=True)


_K0 = math.sqrt(2.0 / math.pi)
_K1 = 0.044715


def _gelu(v):
    return 0.5 * v * (1.0 + jnp.tanh(_K0 * (v + _K1 * (v * v * v))))


def _gelu_grad(v):
    t = jnp.tanh(_K0 * (v + _K1 * (v * v * v)))
    return 0.5 * (1.0 + t) + 0.5 * v * (1.0 - t * t) * (_K0 * (1.0 + 3.0 * _K1 * v * v))


def _row_spec(rows, cols):
    return pl.BlockSpec((rows, cols), lambda i: (i, 0))


def _const_spec(shape):
    nd = len(shape)
    return pl.BlockSpec(shape, lambda i: (0,) * nd, pipeline_mode=pl.Buffered(1))


DILS = tuple(d for _, d in PATTERNS)


def _fill_cols(scr, value):
    for cb in range(value.shape[1] // LANES):
        scr[cb] = value[:, cb * LANES:(cb + 1) * LANES]


def _split_residues(scr, out_ref, dil):
    nb, rows, _ = scr.shape
    for r in range(dil):
        for cb in range(nb):
            piece = scr.at[cb][pl.ds(r, rows // dil, stride=dil), :]
            out_ref[r, :, cb * LANES:(cb + 1) * LANES] = piece.astype(out_ref.dtype)


def _merge_residues(in_ref, scr, dil):
    nb, rows, _ = scr.shape
    for r in range(dil):
        for cb in range(nb):
            scr.at[cb][pl.ds(r, rows // dil, stride=dil), :] = in_ref[r, :, cb * LANES:(cb + 1) * LANES].astype(F32)
    return jnp.concatenate([scr[cb] for cb in range(nb)], axis=-1)


def _col_scratch(rows, width):
    return pltpu.VMEM((width // LANES, rows, LANES), F32)


def _res_spec(dil, rows, width):
    return pl.BlockSpec((dil, rows // dil, width), lambda i: (0, i, 0))


def _res_shape(s, dil, width, dtype):
    return jax.ShapeDtypeStruct((dil, s // dil, width), dtype)


def _norm1(x, g1):
    s = x.shape[0]

    def body(x_ref, g_ref, hn_ref):
        hn, _, _ = _rms_fwd(x_ref[...], g_ref[...])
        hn_ref[...] = hn.astype(BF16)

    return pl.pallas_call(
        body, name="norm1", grid=(s // TM,), in_specs=[_row_spec(TM, D), _const_spec((1, D))],
        out_specs=_row_spec(TM, D), out_shape=jax.ShapeDtypeStruct((s, D), BF16),
        compiler_params=_cparams("arbitrary"),
    )(x, g1)


def _inproj_fwd(hn1, win_t):
    s = hn1.shape[0]
    nd = len(DILS)

    def body(hn_ref, w_ref, *rest):
        qkv_refs = rest[:3 * nd]
        u_ref, z_ref, scr = rest[3 * nd:]
        hn = hn_ref[...]
        for t in range(3):
            seg = _dot_nt(hn, w_ref[t * A:(t + 1) * A, :])
            seg = seg * SCALE if t == 0 else seg
            _fill_cols(scr, seg)
            for di, dil in enumerate(DILS):
                if dil == 1:
                    qkv_refs[t * nd + di][0] = seg.astype(BF16)
                else:
                    _split_residues(scr, qkv_refs[t * nd + di], dil)
        u_ref[...] = _dot_nt(hn, w_ref[3 * A:3 * A + GW, :])
        z_ref[...] = _dot_nt(hn, w_ref[3 * A + GW:INW, :])

    res = pl.pallas_call(
        body, name="inproj_fwd", grid=(s // TM_BIG,),
        in_specs=[_row_spec(TM_BIG, D), _const_spec((INW, D))],
        out_specs=[_res_spec(d, TM_BIG, A) for _ in range(3) for d in DILS]
                  + [_row_spec(TM_BIG, GW), _row_spec(TM_BIG, GW)],
        out_shape=[_res_shape(s, d, A, BF16) for _ in range(3) for d in DILS]
                  + [jax.ShapeDtypeStruct((s, GW), F32)] * 2,
        scratch_shapes=[_col_scratch(TM_BIG, A)],
        compiler_params=_cparams("arbitrary"),
    )(hn1, win_t)
    q, k, v = (res[t * nd:(t + 1) * nd] for t in range(3))
    return q, k, v, res[-2], res[-1]


def _att_geometry(length, dil):
    merge = max(1, min(dil, ATT_ROWS // length))
    rows = min(length * merge, ATT_ROWS)
    nsub = rows // CHUNK
    return merge, rows, length * merge // rows, nsub, min(length // CHUNK, nsub)


def _merged(t, merge):
    return t.reshape(t.shape[0] // merge, t.shape[1] * merge, t.shape[2])


def _stack_heads(t):
    lane = lax.broadcasted_iota(jnp.int32, t.shape, 1)
    zero = jnp.zeros_like(t)
    return jnp.concatenate([jnp.where(lane < DH, t, zero), jnp.where(lane >= DH, t, zero)], axis=0)


def _head_cols(t, hp):
    lane = lax.broadcasted_iota(jnp.int32, t.shape, 1)
    cols = [jnp.sum(jnp.where(lane == 2 * hp + h, t, 0.0), axis=-1, keepdims=True) for h in range(2)]
    return jnp.concatenate(cols, axis=0)


def _unstack_heads(t2):
    n = t2.shape[0] // 2
    lane = lax.broadcasted_iota(jnp.int32, (n, LANES), 1)
    return jnp.where(lane < DH, t2[:n], t2[n:])


def _query_window_bias(s0, s1, dil, first):
    row = lax.broadcasted_iota(jnp.int32, (2 * CHUNK, 2 * CHUNK), 0)
    col = lax.broadcasted_iota(jnp.int32, (2 * CHUNK, 2 * CHUNK), 1)
    steps = (row & (CHUNK - 1)) + CHUNK - col
    valid = (steps >= 0) & (steps <= CHUNK)
    if first:
        valid = valid & (col >= CHUNK)
    slope = jnp.where(row < CHUNK, s0, s1)
    return jnp.where(valid, -slope * (steps * dil).astype(F32), NEG)


def _key_block_bias(s0, s1, dil, last):
    key = lax.broadcasted_iota(jnp.int32, (CHUNK, 4 * CHUNK), 0)
    col = lax.broadcasted_iota(jnp.int32, (CHUNK, 4 * CHUNK), 1)
    wq = col & (2 * CHUNK - 1)
    steps = wq - key
    valid = (steps >= 0) & (steps <= CHUNK)
    if last:
        valid = valid & (wq < CHUNK)
    slope = jnp.where(col < 2 * CHUNK, s0, s1)
    return jnp.where(valid, -slope * (steps * dil).astype(F32), NEG)


def _head_rows(t, hp):
    row = lax.broadcasted_iota(jnp.int32, (8, LANES), 0)
    lane = lax.broadcasted_iota(jnp.int32, (8, LANES), 1)
    pick = jnp.where((row < 2) & (lane == 2 * hp + row), 1.0, 0.0).astype(BF16)
    hi = t.astype(BF16)
    rest = t - hi.astype(F32)
    mid = rest.astype(BF16)
    low = (rest - mid.astype(F32)).astype(BF16)
    return _dot_nt(pick, hi) + _dot_nt(pick, mid) + _dot_nt(pick, low)


def _att_specs(dil, rows, nsub, nblk):
    main = pl.BlockSpec((None, rows, LANES), lambda r, c, hp: (r, c, hp))
    prev = pl.BlockSpec((None, CHUNK, LANES), lambda r, c, hp: (r, jnp.maximum(c * nsub - 1, 0), hp))
    nxt = pl.BlockSpec((None, CHUNK, LANES), lambda r, c, hp: (r, jnp.minimum((c + 1) * nsub, nblk - 1), hp))
    main_heads = pl.BlockSpec((None, rows, LANES), lambda r, c, hp: (r, c, 0))
    nxt_heads = pl.BlockSpec((None, CHUNK, LANES), lambda r, c, hp: (r, jnp.minimum((c + 1) * nsub, nblk - 1), 0))
    return main, prev, nxt, main_heads, nxt_heads


def _row_start(i):
    return i * CHUNK if isinstance(i, int) else pl.multiple_of(i * CHUNK, CHUNK)


def _first_blocks(block, nsub, seg, nch, ch, first_bias, bias_buf):
    for i in range(nsub):
        if i % seg:
            block(i, bias_buf[...])
        elif nch == 1:
            block(i, first_bias())
        else:
            block(i, jnp.where(ch == 0, first_bias(), bias_buf[...]))


def _last_blocks(block, nsub, seg, nch, ch, last_bias, bias_buf):
    for i in range(nsub):
        if (i + 1) % seg:
            block(i, bias_buf[...])
        elif nch == 1:
            block(i, last_bias())
        else:
            block(i, jnp.where(ch == nch - 1, last_bias(), bias_buf[...]))


def _attn_fwd(q, k, v, slopes, dil):
    length = q.shape[1]
    merge, rows, nch, nsub, seg = _att_geometry(length, dil)
    main, prev, _, main_heads, _ = _att_specs(dil, rows, nsub, length * merge // CHUNK)
    q, k, v = (_merged(t, merge) for t in (q, k, v))

    def body(sl_ref, q_ref, k_ref, v_ref, kh_ref, vh_ref, o_ref, lse_ref, kbuf, vbuf, bias_buf):
        ch = pl.program_id(1)
        hp = pl.program_id(2)
        lane = lax.broadcasted_iota(jnp.int32, (CHUNK, LANES), 1)
        kbuf[0:CHUNK, :] = kh_ref[...]
        kbuf[CHUNK:, :] = k_ref[...]
        vbuf[0:CHUNK, :] = vh_ref[...]
        vbuf[CHUNK:, :] = v_ref[...]
        s0, s1 = sl_ref[2 * hp], sl_ref[2 * hp + 1]

        def block(i, bias):
            row = _row_start(i)
            rs = pl.ds(row, CHUNK)
            q2 = _stack_heads(q_ref[rs, :])
            kw = kbuf[pl.ds(row, 2 * CHUNK), :]
            vw = vbuf[pl.ds(row, 2 * CHUNK), :]
            sc = _dot_nt(q2, kw) + bias
            m = jnp.max(sc, axis=-1, keepdims=True)
            p = jnp.exp(sc - m)
            l = jnp.sum(p, axis=-1, keepdims=True)
            o2 = _dot(p.astype(BF16), vw) * (1.0 / l)
            o_ref[rs, :] = _unstack_heads(o2).astype(BF16)
            lse = m + jnp.log(l)
            seen = jnp.where(hp == 0, 0.0, lse_ref[rs, :])
            lse_ref[rs, :] = jnp.where(lane == 2 * hp, lse[:CHUNK], jnp.where(lane == 2 * hp + 1, lse[CHUNK:], seen))

        bias_buf[...] = _query_window_bias(s0, s1, dil, False)
        _first_blocks(block, nsub, seg, nch, ch, lambda: _query_window_bias(s0, s1, dil, True), bias_buf)

    sd = jax.ShapeDtypeStruct
    o, lse = pl.pallas_call(
        body, name=f"attn_fwd_d{dil}", grid=(dil // merge, nch, NH // 2),
        in_specs=[pl.BlockSpec(memory_space=pltpu.SMEM), main, main, main, prev, prev],
        out_specs=[main, main_heads],
        out_shape=[sd((dil // merge, length * merge, A), BF16), sd((dil // merge, length * merge, LANES), F32)],
        scratch_shapes=[pltpu.VMEM((rows + CHUNK, LANES), BF16), pltpu.VMEM((rows + CHUNK, LANES), BF16),
                        pltpu.VMEM((2 * CHUNK, 2 * CHUNK), F32)],
        compiler_params=_cparams("arbitrary", "arbitrary", "arbitrary"),
    )(slopes, q, k, v, k, v)
    return o.reshape(dil, length, A), lse.reshape(dil, length, LANES)


def _attn_bwd_dq(q, k, v, do, lse, delta, slopes, dil):
    length = q.shape[1]
    merge, rows, nch, nsub, seg = _att_geometry(length, dil)
    main, prev, _, main_heads, _ = _att_specs(dil, rows, nsub, length * merge // CHUNK)
    q, k, v, do, lse, delta = (_merged(t, merge) for t in (q, k, v, do, lse, delta))

    def body(sl_ref, q_ref, k_ref, v_ref, do_ref, lse_ref, dl_ref, kh_ref, vh_ref, dq_ref, kbuf, vbuf, bias_buf):
        ch = pl.program_id(1)
        hp = pl.program_id(2)
        kbuf[0:CHUNK, :] = kh_ref[...]
        kbuf[CHUNK:, :] = k_ref[...]
        vbuf[0:CHUNK, :] = vh_ref[...]
        vbuf[CHUNK:, :] = v_ref[...]
        s0, s1 = sl_ref[2 * hp], sl_ref[2 * hp + 1]

        def block(i, bias):
            row = _row_start(i)
            rs = pl.ds(row, CHUNK)
            q2 = _stack_heads(q_ref[rs, :])
            do2 = _stack_heads(do_ref[rs, :])
            lse2 = _head_cols(lse_ref[rs, :], hp)
            dl2 = _head_cols(dl_ref[rs, :], hp)
            kw = kbuf[pl.ds(row, 2 * CHUNK), :]
            vw = vbuf[pl.ds(row, 2 * CHUNK), :]
            p = jnp.exp(_dot_nt(q2, kw) + bias - lse2)
            ds = p * (_dot_nt(do2, vw) - dl2)
            dq_ref[rs, :] = _unstack_heads(_dot(ds.astype(BF16), kw)).astype(BF16)

        bias_buf[...] = _query_window_bias(s0, s1, dil, False)
        _first_blocks(block, nsub, seg, nch, ch, lambda: _query_window_bias(s0, s1, dil, True), bias_buf)

    dq = pl.pallas_call(
        body, name=f"attn_dq_d{dil}", grid=(dil // merge, nch, NH // 2),
        in_specs=[pl.BlockSpec(memory_space=pltpu.SMEM), main, main, main, main, main_heads, main_heads, prev, prev],
        out_specs=main, out_shape=jax.ShapeDtypeStruct((dil // merge, length * merge, A), BF16),
        scratch_shapes=[pltpu.VMEM((rows + CHUNK, LANES), BF16), pltpu.VMEM((rows + CHUNK, LANES), BF16),
                        pltpu.VMEM((2 * CHUNK, 2 * CHUNK), F32)],
        compiler_params=_cparams("arbitrary", "arbitrary", "arbitrary"),
    )(slopes, q, k, v, do, lse, delta, k, v)
    return dq.reshape(dil, length, A)


def _attn_bwd_dkv(q, k, v, do, lse, delta, slopes, dil):
    length = q.shape[1]
    merge, rows, nch, nsub, seg = _att_geometry(length, dil)
    main, _, nxt, main_heads, nxt_heads = _att_specs(dil, rows, nsub, length * merge // CHUNK)
    q, k, v, do, lse, delta = (_merged(t, merge) for t in (q, k, v, do, lse, delta))

    def body(sl_ref, k_ref, v_ref, q_ref, do_ref, lse_ref, dl_ref, qh_ref, doh_ref, lseh_ref, dlh_ref,
             dk_ref, dv_ref, qbuf, dobuf, lse_rows, dl_rows, bias_buf):
        ch = pl.program_id(1)
        hp = pl.program_id(2)
        for buf, main_ref, halo_ref in ((qbuf, q_ref, qh_ref), (dobuf, do_ref, doh_ref)):
            buf[0:rows, :] = main_ref[...]
            buf[rows:, :] = halo_ref[...]
        for buf, main_ref, halo_ref in ((lse_rows, lse_ref, lseh_ref), (dl_rows, dl_ref, dlh_ref)):
            buf[:, 0:rows] = _head_rows(main_ref[...], hp)
            buf[:, rows:] = _head_rows(halo_ref[...], hp)
        s0, s1 = sl_ref[2 * hp], sl_ref[2 * hp + 1]

        def block(i, bias):
            row = _row_start(i)
            rs = pl.ds(row, CHUNK)
            win = pl.ds(row, 2 * CHUNK)
            kc = k_ref[rs, :]
            vc = v_ref[rs, :]
            q2 = _stack_heads(qbuf[win, :])
            do2 = _stack_heads(dobuf[win, :])
            cols = slice(i * CHUNK, (i + 2) * CHUNK)
            lse2 = jnp.concatenate([lse_rows[0:1, cols], lse_rows[1:2, cols]], axis=1)
            dl2 = jnp.concatenate([dl_rows[0:1, cols], dl_rows[1:2, cols]], axis=1)
            pt = jnp.exp(_dot_nt(kc, q2) + bias - lse2)
            dst = pt * (_dot_nt(vc, do2) - dl2)
            dv_ref[rs, :] = _dot(pt.astype(BF16), do2).astype(BF16)
            dk_ref[rs, :] = _dot(dst.astype(BF16), q2).astype(BF16)

        bias_buf[...] = _key_block_bias(s0, s1, dil, False)
        _last_blocks(block, nsub, seg, nch, ch, lambda: _key_block_bias(s0, s1, dil, True), bias_buf)

    sd = jax.ShapeDtypeStruct((dil // merge, length * merge, A), BF16)
    dk, dv = pl.pallas_call(
        body, name=f"attn_dkv_d{dil}", grid=(dil // merge, nch, NH // 2),
        in_specs=[pl.BlockSpec(memory_space=pltpu.SMEM), main, main, main, main, main_heads, main_heads,
                  nxt, nxt, nxt_heads, nxt_heads],
        out_specs=[main, main], out_shape=[sd, sd],
        scratch_shapes=[pltpu.VMEM((rows + CHUNK, LANES), BF16), pltpu.VMEM((rows + CHUNK, LANES), BF16),
                        pltpu.VMEM((8, rows + CHUNK), F32), pltpu.VMEM((8, rows + CHUNK), F32),
                        pltpu.VMEM((CHUNK, 4 * CHUNK), F32)],
        compiler_params=_cparams("arbitrary", "arbitrary", "arbitrary"),
    )(slopes, k, v, q, do, lse, delta, q, do, lse, delta)
    return dk.reshape(dil, length, A), dv.reshape(dil, length, A)


def _group_masks(width):
    lane = lax.broadcasted_iota(jnp.int32, (1, width), 1)
    return [(lane >= g * DH) & (lane < (g + 1) * DH) for g in range(width // DH)]


def _group_mean_matrix():
    i = lax.broadcasted_iota(jnp.int32, (GW, GW), 0) // DH
    j = lax.broadcasted_iota(jnp.int32, (GW, GW), 1) // DH
    return jnp.where(i == j, 1.0 / DH, 0.0).astype(F32)


def _tri_mask(lower):
    t = lax.broadcasted_iota(jnp.int32, (CHUNK, CHUNK), 0)
    u = lax.broadcasted_iota(jnp.int32, (CHUNK, CHUNK), 1)
    return (u <= t) if lower else (u >= t)


def _sgu_forward(u, z, lng, lnb, w_ref, bias_t, pmat, rows):
    ug = _gelu(u)
    zg = _gelu(z)
    mu = _dot_hi(zg, pmat)
    zc = zg - mu
    var = _dot_hi(zc * zc, pmat)
    rstd = lax.rsqrt(var + EPS)
    zhat = zc * rstd
    zn = (zhat * lng + lnb).astype(BF16)
    gm = _group_masks(GW)
    tri = _tri_mask(True)
    ws = [jnp.where(tri, w_ref[g], 0.0).astype(BF16) for g in range(NG)]
    pieces = []
    for c in range(rows // CHUNK):
        znc = zn[c * CHUNK:(c + 1) * CHUNK, :]
        mix = None
        for g in range(NG):
            part = jnp.where(gm[g], _dot(ws[g], znc), 0.0)
            mix = part if mix is None else mix + part
        pieces.append(mix + bias_t)
    mixed = jnp.concatenate(pieces, axis=0) if len(pieces) > 1 else pieces[0]
    return ug * mixed, ug, zhat, rstd, zn, mixed


def _head_spread():
    h = lax.broadcasted_iota(jnp.int32, (LANES, A), 0)
    lane = lax.broadcasted_iota(jnp.int32, (LANES, A), 1)
    return jnp.where(lane // DH == h, 1.0, 0.0).astype(BF16)


def _bf16_pieces(t, n):
    pieces = []
    for _ in range(n):
        piece = t.astype(BF16)
        pieces.append(piece)
        t = t - piece.astype(F32)
    return pieces


def _mix_fwd(os_, ls_, u, z, x, lng, lnb, sgu_w, bias_t, ga, gg, wout):
    s = x.shape[0]
    nd = len(DILS)
    nscr = sum(1 for d in DILS if d > 1)

    def body(*refs):
        o_refs, l_refs = refs[:nd], refs[nd:2 * nd]
        u_ref, z_ref, x_ref, lng_ref, lnb_ref, w_ref, bt_ref, ga_ref, gg_ref, wo_ref = refs[2 * nd:2 * nd + 10]
        attn_ref = refs[2 * nd + 10]
        lse_refs = refs[2 * nd + 11:3 * nd + 11]
        mixed_ref, h1_ref = refs[3 * nd + 11:3 * nd + 13]
        scr = refs[3 * nd + 13:]
        scr_o, scr_l, scr_lse = scr[:nscr], scr[nscr:2 * nscr], scr[2 * nscr]
        ov, lv, j = [], [], 0
        for di, dil in enumerate(DILS):
            if dil == 1:
                ov.append(o_refs[di][0].astype(F32))
                lv.append(l_refs[di][0])
            else:
                ov.append(_merge_residues(o_refs[di], scr_o[j], dil))
                lv.append(_merge_residues(l_refs[di], scr_l[j], dil))
                j += 1
        mx = functools.reduce(jnp.maximum, lv)
        es = [jnp.exp(l - mx) for l in lv]
        den = functools.reduce(lambda a, b: a + b, es)
        spread = _head_spread()
        attn = None
        for e, o in zip(es, ov):
            wide = functools.reduce(lambda a, b: a + b, [_dot(piece, spread) for piece in _bf16_pieces(e / den, 2)])
            attn = wide * o if attn is None else attn + wide * o
        attn_ref[...] = attn
        lse = mx + jnp.log(den)
        _fill_cols(scr_lse, lse)
        for di, dil in enumerate(DILS):
            if dil == 1:
                lse_refs[di][0] = lse
            else:
                _split_residues(scr_lse, lse_refs[di], dil)
        an, _, _ = _rms_fwd(attn, ga_ref[...])
        gmv, _, _, _, _, _ = _sgu_forward(u_ref[...], z_ref[...], lng_ref[...], lnb_ref[...], w_ref,
                                          bt_ref[...], _group_mean_matrix(), TMX)
        gn, _, _ = _rms_fwd(gmv, gg_ref[...])
        mixed = jnp.concatenate([an, gn], axis=-1).astype(BF16)
        mixed_ref[...] = mixed
        h1_ref[...] = x_ref[...] + _dot(mixed, wo_ref[...])

    sd = jax.ShapeDtypeStruct
    res = pl.pallas_call(
        body, name="mix_fwd", grid=(s // TMX,),
        in_specs=[_res_spec(d, TMX, A) for d in DILS] + [_res_spec(d, TMX, LANES) for d in DILS]
                 + [_row_spec(TMX, GW), _row_spec(TMX, GW),
                    _row_spec(TMX, D), _const_spec((1, GW)), _const_spec((1, GW)), _const_spec((NG, CHUNK, CHUNK)),
                    _const_spec((CHUNK, GW)), _const_spec((1, A)), _const_spec((1, GW)), _const_spec((D, D))],
        out_specs=[_row_spec(TMX, A)] + [_res_spec(d, TMX, LANES) for d in DILS]
                  + [_row_spec(TMX, D), _row_spec(TMX, D)],
        out_shape=[sd((s, A), F32)] + [_res_shape(s, d, LANES, F32) for d in DILS]
                  + [sd((s, D), BF16), sd((s, D), F32)],
        scratch_shapes=[_col_scratch(TMX, A)] * nscr + [_col_scratch(TMX, LANES)] * (nscr + 1),
        compiler_params=_cparams("arbitrary"),
    )(*os_, *ls_, u, z, x, lng, lnb, sgu_w, bias_t, ga, gg, wout)
    return res[0], res[1:1 + nd], res[1 + nd], res[2 + nd]


def _mlp_fwd(h1, g2, wff1, wff2, gf, target):
    s = h1.shape[0]

    def body(h1_ref, g2_ref, w1_ref, w2_ref, gf_ref, t_ref, hn_ref, rf_ref, dh2_ref, loss_ref, dgf_ref):
        i = pl.program_id(0)
        h1v = h1_ref[...]
        hn, _, _ = _rms_fwd(h1v, g2_ref[...])
        hn = hn.astype(BF16)
        hn_ref[...] = hn
        acc = h1v
        for j in range(DFF // FF_CH):
            cols = slice(j * FF_CH, (j + 1) * FF_CH)
            rf = jnp.maximum(_dot(hn, w1_ref[j]), 0.0)
            act = (rf * rf).astype(BF16)
            rf_ref[:, cols] = rf.astype(BF16)
            acc = acc + _dot(act, w2_ref[cols, :])
        y, h2n, r3 = _rms_fwd(acc, gf_ref[...])
        err = y - t_ref[...]
        part = 0.5 * jnp.sum(jnp.mean(err * err, axis=-1, keepdims=True), axis=0, keepdims=True)
        dy = err * (1.0 / D)
        dh2, dgf = _rms_bwd(dy, h2n, r3, gf_ref[...])
        dh2_ref[...] = dh2

        @pl.when(i == 0)
        def _():
            loss_ref[...] = jnp.zeros_like(loss_ref)
            dgf_ref[...] = jnp.zeros_like(dgf_ref)

        loss_ref[...] += jnp.broadcast_to(part, loss_ref.shape)
        dgf_ref[...] += dgf

    sd = jax.ShapeDtypeStruct
    return pl.pallas_call(
        body, name="mlp_fwd", grid=(s // TM,),
        in_specs=[_row_spec(TM, D), _const_spec((1, D)), _const_spec((DFF // FF_CH, D, FF_CH)), _const_spec((DFF, D)),
                  _const_spec((1, D)), _row_spec(TM, D)],
        out_specs=[_row_spec(TM, D), _row_spec(TM, DFF), _row_spec(TM, D),
                   _const_spec((1, LANES)), _const_spec((1, D))],
        out_shape=[sd((s, D), BF16), sd((s, DFF), BF16), sd((s, D), F32),
                   sd((1, LANES), F32), sd((1, D), F32)],
        compiler_params=_cparams("arbitrary"),
    )(h1, g2, wff1, wff2, gf, target)


def _mlp_bwd(dh2, rf, h1, g2, wff1, wff2):
    s = h1.shape[0]

    def body(dh2_ref, rf_ref, h1_ref, g2_ref, w1_ref, w2_ref, df_ref, dh1_ref, dg2_ref):
        i = pl.program_id(0)
        dh2v = dh2_ref[...]
        dh2b = dh2v.astype(BF16)
        dhn = jnp.zeros((TM, D), F32)
        for j in range(DFF // FF_CH):
            cols = slice(j * FF_CH, (j + 1) * FF_CH)
            da = _dot_nt(dh2b, w2_ref[cols, :])
            df = (da * (2.0 * rf_ref[:, cols].astype(F32))).astype(BF16)
            df_ref[:, cols] = df
            dhn = dhn + _dot_nt(df, w1_ref[j])
        _, h1n, r2 = _rms_fwd(h1_ref[...], g2_ref[...])
        dres, dg2 = _rms_bwd(dhn, h1n, r2, g2_ref[...])
        dh1_ref[...] = dh2v + dres

        @pl.when(i == 0)
        def _():
            dg2_ref[...] = jnp.zeros_like(dg2_ref)

        dg2_ref[...] += dg2

    sd = jax.ShapeDtypeStruct
    return pl.pallas_call(
        body, name="mlp_bwd", grid=(s // TM,),
        in_specs=[_row_spec(TM, D), _row_spec(TM, DFF), _row_spec(TM, D), _const_spec((1, D)),
                  _const_spec((DFF // FF_CH, D, FF_CH)), _const_spec((DFF, D))],
        out_specs=[_row_spec(TM, DFF), _row_spec(TM, D), _const_spec((1, D))],
        out_shape=[sd((s, DFF), BF16), sd((s, D), F32), sd((1, D), F32)],
        compiler_params=_cparams("arbitrary"),
    )(dh2, rf, h1, g2, wff1, wff2)


def _mix_bwd(dh1, attn, u, z, lng, lnb, sgu_w, sgu_wt, bias_t, ga, gg, wout):
    s = dh1.shape[0]
    nsteps = s // TMX
    nd = len(DILS)

    def body(*refs):
        dh1_ref, attn_ref, u_ref, z_ref, lng_ref, lnb_ref, w_ref, wt_ref, bt_ref, ga_ref, gg_ref, wo_ref = refs[:12]
        do_refs, dl_refs = refs[12:12 + nd], refs[12 + nd:12 + 2 * nd]
        (du_ref, dz_ref, dga_ref, dgg_ref, dlng_ref, dlnb_ref, dws_ref, db_ref,
         dbt_acc, scr_do, scr_dl) = refs[12 + 2 * nd:]
        i = pl.program_id(0)

        @pl.when(i == 0)
        def _():
            for r in (dga_ref, dgg_ref, dlng_ref, dlnb_ref, dws_ref, db_ref, dbt_acc):
                r[...] = jnp.zeros_like(r)

        dmixed = _dot_nt(dh1_ref[...].astype(BF16), wo_ref[...])
        attn = attn_ref[...]
        _, an, ra = _rms_fwd(attn, ga_ref[...])
        dattn, dga = _rms_bwd(dmixed[:, :A], an, ra, ga_ref[...])
        dga_ref[...] += dga
        _fill_cols(scr_do, dattn)
        spread = _head_spread()
        delta = functools.reduce(lambda a, b: a + b, [_dot_nt(piece, spread) for piece in _bf16_pieces(dattn * attn, 3)])
        _fill_cols(scr_dl, delta)
        for di, dil in enumerate(DILS):
            if dil == 1:
                do_refs[di][0] = dattn.astype(BF16)
                dl_refs[di][0] = delta
            else:
                _split_residues(scr_do, do_refs[di], dil)
                _split_residues(scr_dl, dl_refs[di], dil)
        pmat = _group_mean_matrix()
        lng = lng_ref[...]
        uv, zv = u_ref[...], z_ref[...]
        gmv, ug, zhat, rstd, zn, mixed = _sgu_forward(uv, zv, lng, lnb_ref[...], w_ref, bt_ref[...], pmat, TMX)
        _, gmn, rg = _rms_fwd(gmv, gg_ref[...])
        dgm, dgg = _rms_bwd(dmixed[:, A:], gmn, rg, gg_ref[...])
        dgg_ref[...] += dgg
        du_ref[...] = (dgm * mixed * _gelu_grad(uv)).astype(BF16)
        dmx = dgm * ug
        dmxb = dmx.astype(BF16)
        gm = _group_masks(GW)
        tri_t = _tri_mask(False)
        wst = [jnp.where(tri_t, wt_ref[g], 0.0).astype(BF16) for g in range(NG)]
        zero = jnp.zeros((CHUNK, GW), BF16)
        dzn_pieces = []
        for c in range(TMX // CHUNK):
            rs = slice(c * CHUNK, (c + 1) * CHUNK)
            dmc = dmxb[rs, :]
            znc = zn[rs, :]
            dbt_acc[...] += dmx[rs, :]
            dzn = None
            for g in range(NG):
                dws_ref[g] += _dot_nt(jnp.where(gm[g], dmc, zero), znc)
                part = jnp.where(gm[g], _dot(wst[g], dmc), 0.0)
                dzn = part if dzn is None else dzn + part
            dzn_pieces.append(dzn)
        dzn = jnp.concatenate(dzn_pieces, axis=0)
        dlng_ref[...] += jnp.sum(dzn * zhat, axis=0, keepdims=True)
        dlnb_ref[...] += jnp.sum(dzn, axis=0, keepdims=True)
        dzh = dzn * lng
        dzg = rstd * (dzh - _dot_hi(dzh, pmat) - zhat * _dot_hi(dzh * zhat, pmat))
        dz_ref[...] = (dzg * _gelu_grad(zv)).astype(BF16)

        @pl.when(i == nsteps - 1)
        def _():
            tri = _tri_mask(True)
            for g in range(NG):
                dws_ref[g] = jnp.where(tri, dws_ref[g], 0.0)
            acc = dbt_acc[...]
            lane = lax.broadcasted_iota(jnp.int32, (CHUNK, LANES), 1)
            out = jnp.zeros((CHUNK, LANES), F32)
            for g in range(NG):
                sg = jnp.sum(jnp.where(gm[g], acc, 0.0), axis=-1, keepdims=True)
                out = jnp.where(lane == g, sg, out)
            db_ref[...] = out

    sd = jax.ShapeDtypeStruct
    res = pl.pallas_call(
        body, name="mix_bwd", grid=(nsteps,),
        in_specs=[_row_spec(TMX, D), _row_spec(TMX, A), _row_spec(TMX, GW), _row_spec(TMX, GW),
                  _const_spec((1, GW)), _const_spec((1, GW)), _const_spec((NG, CHUNK, CHUNK)),
                  _const_spec((NG, CHUNK, CHUNK)), _const_spec((CHUNK, GW)), _const_spec((1, A)),
                  _const_spec((1, GW)), _const_spec((D, D))],
        out_specs=[_res_spec(d, TMX, A) for d in DILS] + [_res_spec(d, TMX, LANES) for d in DILS]
                  + [_row_spec(TMX, GW), _row_spec(TMX, GW),
                   _const_spec((1, A)), _const_spec((1, GW)), _const_spec((1, GW)), _const_spec((1, GW)),
                   _const_spec((NG, CHUNK, CHUNK)), _const_spec((CHUNK, LANES))],
        out_shape=[_res_shape(s, d, A, BF16) for d in DILS] + [_res_shape(s, d, LANES, F32) for d in DILS]
                  + [sd((s, GW), BF16), sd((s, GW), BF16),
                   sd((1, A), F32), sd((1, GW), F32), sd((1, GW), F32), sd((1, GW), F32),
                   sd((NG, CHUNK, CHUNK), F32), sd((CHUNK, LANES), F32)],
        scratch_shapes=[pltpu.VMEM((CHUNK, GW), F32), _col_scratch(TMX, A), _col_scratch(TMX, LANES)],
        compiler_params=_cparams("arbitrary"),
    )(dh1, attn, u, z, lng, lnb, sgu_w, sgu_wt, bias_t, ga, gg, wout)
    return (res[:nd], res[nd:2 * nd]) + tuple(res[2 * nd:])


def _dproj_merge(dqs, dks, dvs, du, dz, pin):
    s = du.shape[0]
    nd = len(DILS)
    nscr = sum(1 for d in DILS if d > 1)

    def body(*refs):
        pin_ref = refs[0]
        parts = [refs[1 + t * nd:1 + (t + 1) * nd] for t in range(3)]
        du_ref, dz_ref, dp_ref = refs[1 + 3 * nd:4 + 3 * nd]
        scr = refs[4 + 3 * nd:]
        sums = []
        for t in range(3):
            total, j = None, 0
            for di, dil in enumerate(DILS):
                if dil == 1:
                    term = parts[t][di][0].astype(F32)
                else:
                    term = _merge_residues(parts[t][di], scr[t * nscr + j], dil)
                    j += 1
                total = term if total is None else total + term
            sums.append(total)
        dp_ref[...] = jnp.concatenate([sums[0] * SCALE, sums[1], sums[2], du_ref[...].astype(F32) + pin_ref[0, 0],
                                       dz_ref[...].astype(F32)], axis=-1).astype(BF16)

    return pl.pallas_call(
        body, name="dproj_merge", grid=(s // TMX,),
        in_specs=[pl.BlockSpec(memory_space=pltpu.SMEM)] + [_res_spec(d, TMX, A) for d in DILS] * 3
                 + [_row_spec(TMX, GW)] * 2,
        out_specs=_row_spec(TMX, INW), out_shape=jax.ShapeDtypeStruct((s, INW), BF16),
        scratch_shapes=[_col_scratch(TMX, A)] * (3 * nscr),
        compiler_params=_cparams("arbitrary"),
    )(pin, *dqs, *dks, *dvs, du, dz)


def _inproj_bwd(dproj, dh1, x, g1, win_t):
    s = x.shape[0]

    def body(dp_ref, dh1_ref, x_ref, g_ref, w_ref, dx_ref, dg_ref):
        i = pl.program_id(0)
        dhn = _dot(dp_ref[...], w_ref[...])
        _, xn, r1 = _rms_fwd(x_ref[...], g_ref[...])
        dres, dg = _rms_bwd(dhn, xn, r1, g_ref[...])
        dx_ref[...] = dh1_ref[...] + dres

        @pl.when(i == 0)
        def _():
            dg_ref[...] = jnp.zeros_like(dg_ref)

        dg_ref[...] += dg

    sd = jax.ShapeDtypeStruct
    return pl.pallas_call(
        body, name="inproj_bwd", grid=(s // TM_BIG,),
        in_specs=[_row_spec(TM_BIG, INW), _row_spec(TM_BIG, D), _row_spec(TM_BIG, D), _const_spec((1, D)), _const_spec((INW, D))],
        out_specs=[_row_spec(TM_BIG, D), _const_spec((1, D))],
        out_shape=[sd((s, D), F32), sd((1, D), F32)],
        compiler_params=_cparams("arbitrary"),
    )(dproj, dh1, x, g1, win_t)


def _wgrad(a, b, name, bm, bn, bk=4 * TM, square_a=False, also_bf16=False):
    s, m = a.shape
    n = b.shape[1]
    bm, bn = min(bm, m), min(bn, n)
    nk = s // bk

    def body(a_ref, b_ref, o_ref, *low):
        @pl.when(pl.program_id(2) == 0)
        def _():
            o_ref[...] = jnp.zeros_like(o_ref)

        av = a_ref[...]
        if square_a:
            av = av.astype(F32)
            av = av * av
        o_ref[...] += _dot_tn(av.astype(BF16), b_ref[...].astype(BF16))
        if also_bf16:
            @pl.when(pl.program_id(2) == nk - 1)
            def _():
                low[0][...] = o_ref[...].astype(BF16)

    out_spec = pl.BlockSpec((bm, bn), lambda i, j, k: (i, j))
    res = pl.pallas_call(
        body, name=name, grid=(m // bm, n // bn, nk),
        in_specs=[pl.BlockSpec((bk, bm), lambda i, j, k: (k, i)), pl.BlockSpec((bk, bn), lambda i, j, k: (k, j))],
        out_specs=[out_spec, out_spec] if also_bf16 else out_spec,
        out_shape=([jax.ShapeDtypeStruct((m, n), F32), jax.ShapeDtypeStruct((m, n), BF16)] if also_bf16
                   else jax.ShapeDtypeStruct((m, n), F32)),
        compiler_params=_cparams("arbitrary", "arbitrary", "arbitrary"),
    )(a, b)
    return res


def _adamw_math(w, g, m, v):
    m = B1 * m + (1.0 - B1) * g
    v = B2 * v + (1.0 - B2) * (g * g)
    m_hat = m / (1.0 - B1 ** STEP)
    v_hat = v / (1.0 - B2 ** STEP)
    delta = -LR * (m_hat / (jnp.sqrt(v_hat) + AEPS) + WD * w)
    return delta, m, v


def _adamw(w, g, m, v, name):
    rows, cols = w.shape
    br = min(rows, 256)
    while rows % br:
        br -= 8

    def body(w_ref, g_ref, m_ref, v_ref, d_ref, mo_ref, vo_ref):
        d, mn, vn = _adamw_math(w_ref[...], g_ref[...], m_ref[...], v_ref[...])
        d_ref[...] = d
        mo_ref[...] = mn
        vo_ref[...] = vn

    spec = _row_spec(br, cols)
    sd = jax.ShapeDtypeStruct((rows, cols), F32)
    return pl.pallas_call(
        body, name=name, grid=(rows // br,), in_specs=[spec] * 4, out_specs=[spec] * 3,
        out_shape=[sd, sd, sd], compiler_params=_cparams("arbitrary"),
    )(w, g, m, v)


def _local_step(x, hn1, target, small, win_t, rest_weights, early_grads=None, after_attention_bwd=None,
                late_grads=None):
    slopes = jnp.asarray(_alibi_slopes(NH))
    q, k, v, u, z = _inproj_fwd(hn1, win_t)
    outs, lses = [], []
    for i, dil in enumerate(DILS):
        o, l = _attn_fwd(q[i], k[i], v[i], slopes, dil)
        outs.append(o)
        lses.append(l)
    wout, wff1, wff2 = rest_weights(functools.reduce(lambda a, b: a + b, [l[0, 0:8, :] for l in lses]))
    attn, lse, mixed, h1 = _mix_fwd(outs, lses, u, z, x, small["ln_g"], small["ln_b"], small["sgu_w"],
                                    small["bias_t"], small["attn_out_g"], small["gmlp_out_g"], wout)
    hn2, rf, dh2, loss, dgf = _mlp_fwd(h1, small["norm2_g"], wff1, wff2, small["final_norm_g"], target)
    df, dh1, dg2 = _mlp_bwd(dh2, rf, h1, small["norm2_g"], wff1, wff2)
    gwff1 = _wgrad(hn2, df, "wgrad_ff1", D, 1024)
    gwff2 = _wgrad(rf, dh2, "wgrad_ff2", 1024, D, square_a=True)
    gwout = _wgrad(mixed, dh1, "wgrad_out", D, D)
    ga, g1 = small["attn_out_g"], small["norm1_g"]
    pin = early_grads(gwff1, gwff2, gwout) if early_grads else None
    if pin is not None:
        ga = ga + pin
    (do, delta, du, dz, dga, dgg, dlng, dlnb, dws, db) = _mix_bwd(
        dh1, attn, u, z, small["ln_g"], small["ln_b"], small["sgu_w"], small["sgu_wt"], small["bias_t"],
        ga, small["gmlp_out_g"], wout)
    dqs, dks, dvs = [], [], []
    for i, dil in enumerate(DILS):
        dqs.append(_attn_bwd_dq(q[i], k[i], v[i], do[i], lse[i], delta[i], slopes, dil))
        dk, dv = _attn_bwd_dkv(q[i], k[i], v[i], do[i], lse[i], delta[i], slopes, dil)
        dks.append(dk)
        dvs.append(dv)
    marker = functools.reduce(lambda a, b: a + b, [t[0, 0:8, 0:LANES] for t in dqs + dks + dvs])
    partial = dict(ln_g=dlng, ln_b=dlnb, sgu_w=dws, sgu_b=db[:, :NG].T, attn_out_g=dga, gmlp_out_g=dgg,
                   norm2_g=dg2, final_norm_g=dgf)
    pin = after_attention_bwd(marker, partial, loss[0, 0]) if after_attention_bwd else None
    dproj = _dproj_merge(dqs, dks, dvs, du, dz, jnp.zeros((1, 1), F32) if pin is None else pin)
    gwin_t, gwin_low = _wgrad(dproj, hn1, "wgrad_in", INW // 2, D, also_bf16=True)
    pin = late_grads(gwin_t, gwin_low) if late_grads else None
    if pin is not None:
        g1 = g1 + pin
    dx, dg1 = _inproj_bwd(dproj, dh1, x, g1, win_t)
    small_grads = dict(partial, norm1_g=dg1)
    return loss[0, 0], dx, small_grads, (gwin_t, gwout, gwff1, gwff2)


ANY = pl.BlockSpec(memory_space=pl.ANY)
NDEV = 8


def _position():
    return lax.axis_index("x"), lax.axis_index("y"), lax.axis_index("c")


def _other_chips(x, y):
    return [(1 - x, y), (x, 1 - y), (1 - x, 1 - y)]


def _remote(src, dst, send_sem, recv_sem, device):
    return pltpu.make_async_remote_copy(src_ref=src, dst_ref=dst, send_sem=send_sem, recv_sem=recv_sem,
                                        device_id=device, device_id_type=MESH)


HBM = pl.BlockSpec(memory_space=pltpu.HBM)
SEM = pl.BlockSpec(memory_space=pltpu.SEMAPHORE)
DATAFLOW = pltpu.SideEffectType.DATAFLOW_SIDE_EFFECTING


def _in_hbm(a):
    return pltpu.with_memory_space_constraint(a, pltpu.HBM)


def _gather_start(shards, name):
    n = len(shards)
    lands = [jnp.broadcast_to(sh[None], (NCHIP,) + sh.shape) for sh in shards]

    def body(*refs):
        w_refs, land_refs = refs[:n], refs[n:2 * n]
        send_sems, recv_sems = refs[2 * n:2 * n + 2]
        token = refs[-1]
        x, y, c = _position()
        for w in range(n):
            for k, (px, py) in enumerate(_other_chips(x, y)):
                m = 3 * w + k
                _remote(w_refs[w], land_refs[w].at[2 * x + y], send_sems.at[m], recv_sems.at[m], (px, py, c)).start()
        token[...] = jnp.zeros_like(token)

    res = _split_call(body, name, list(shards) + lands, (3 * n, 3 * n), (TOKEN,))
    return res[0], res[1], res[2:2 + n], res[2 + n:2 + 2 * n], res[-1]


def _gather_wait(send_sems, recv_sems, shards, lands, after, name):
    n = len(shards)

    def body(*refs):
        w_refs, land_refs = refs[:n], refs[n:2 * n]
        send_sems, recv_sems = refs[2 * n:2 * n + 2]
        x, y, c = _position()
        for w in range(n):
            for k, (px, py) in enumerate(_other_chips(x, y)):
                m = 3 * w + k
                cp = _remote(w_refs[w], land_refs[w].at[2 * px + py], send_sems.at[m], recv_sems.at[m], (px, py, c))
                cp.wait_send()
                cp.wait_recv()

    operands = list(shards) + list(lands)
    res = pl.pallas_call(
        body, name=name, out_shape=tuple(pltpu.HBM(a.shape, a.dtype) for a in operands),
        in_specs=(HBM,) * (2 * n) + (SEM, SEM, ANY), out_specs=(HBM,) * (2 * n),
        input_output_aliases={i: i for i in range(2 * n)},
        compiler_params=pltpu.CompilerParams(has_side_effects=DATAFLOW),
    )(*operands, send_sems, recv_sems, after)
    return res[n:]


def _xor_peers(x, y, c):
    peers = []
    for k in range(1, NDEV):
        kx, ky, kc = (k >> 2) & 1, (k >> 1) & 1, k & 1
        peers.append((1 - x if kx else x, 1 - y if ky else y, 1 - c if kc else c))
    return peers


def _piece(part_ref, px, py, pc):
    slab = 2 * px + py
    if len(part_ref.shape) == 3:
        half = part_ref.shape[1] // 2
        return part_ref.at[slab, pl.ds(pc * half, half), :]
    half = part_ref.shape[0] // 2
    return part_ref.at[pl.ds(pc * half, half), pl.ds(pl.multiple_of(slab * D, D), D)]


def _split_call(body, name, operands, n_sems, extra_out=()):
    n = len(operands)
    sems = tuple(pltpu.SemaphoreType.DMA((m,)) for m in n_sems)
    thru = tuple(pltpu.HBM(a.shape, a.dtype) for a in operands)
    return pl.pallas_call(
        body, name=name, out_shape=sems + thru + tuple(extra_out),
        in_specs=(HBM,) * n,
        out_specs=(SEM,) * len(sems) + (HBM,) * n + (pl.BlockSpec(memory_space=pltpu.VMEM),) * len(extra_out),
        input_output_aliases={i: len(sems) + i for i in range(n)},
        compiler_params=pltpu.CompilerParams(has_side_effects=DATAFLOW),
    )(*[_in_hbm(a) for a in operands])


TOKEN = jax.ShapeDtypeStruct((8, LANES), F32)


def _pack_copies(pack_ref, land_ref, send_sems, recv_sems, base, position, start):
    x, y, c = position
    for k, (px, py, pc) in enumerate(_xor_peers(x, y, c)):
        if start:
            _remote(pack_ref, land_ref.at[4 * x + 2 * y + c], send_sems.at[base + k], recv_sems.at[base + k],
                    (px, py, pc)).start()
        else:
            cp = _remote(pack_ref, land_ref.at[4 * px + 2 * py + pc], send_sems.at[base + k], recv_sems.at[base + k],
                         (px, py, pc))
            cp.wait_send()
            cp.wait_recv()


def _pack_landing(pack):
    return jnp.broadcast_to(pack[None], (NDEV,) + pack.shape)


def _reduce_start(parts, name, pack=None):
    nw = len(parts)
    lands = [lax.empty((NDEV - 1, p.shape[-2] // 2, D), p.dtype) for p in parts]
    operands = list(parts) + lands + ([pack, _pack_landing(pack)] if pack is not None else [])
    nops = len(operands)

    def body(*refs):
        part_refs, land_refs = refs[:nw], refs[nw:2 * nw]
        send_sems, recv_sems = refs[nops:nops + 2]
        token = refs[-1]
        x, y, c = _position()
        for w in range(nw):
            for k, peer in enumerate(_xor_peers(x, y, c)):
                n = w * (NDEV - 1) + k
                _remote(_piece(part_refs[w], *peer), land_refs[w].at[k], send_sems.at[n], recv_sems.at[n],
                        peer).start()
        if pack is not None:
            _pack_copies(refs[2 * nw], refs[2 * nw + 1], send_sems, recv_sems, nw * (NDEV - 1), (x, y, c), True)
        token[...] = jnp.zeros_like(token)

    n = (nw + (pack is not None)) * (NDEV - 1)
    res = _split_call(body, name, operands, (n, n), (TOKEN,))
    return res[0], res[1], res[2:2 + nops], res[-1]


def _reduce_wait(send_sems, recv_sems, operands, nw, after, name):
    nops = len(operands)
    has_pack = nops > 2 * nw

    def body(*refs):
        part_refs, land_refs = refs[:nw], refs[nw:2 * nw]
        send_sems, recv_sems = refs[nops:nops + 2]
        x, y, c = _position()
        for w in range(nw):
            for k, peer in enumerate(_xor_peers(x, y, c)):
                n = w * (NDEV - 1) + k
                cp = _remote(_piece(part_refs[w], *peer), land_refs[w].at[k], send_sems.at[n], recv_sems.at[n], peer)
                cp.wait_send()
                cp.wait_recv()
        if has_pack:
            _pack_copies(refs[2 * nw], refs[2 * nw + 1], send_sems, recv_sems, nw * (NDEV - 1), (x, y, c), False)

    res = pl.pallas_call(
        body, name=name, out_shape=tuple(pltpu.HBM(a.shape, a.dtype) for a in operands),
        in_specs=(HBM,) * nops + (SEM, SEM, ANY), out_specs=(HBM,) * nops,
        input_output_aliases={i: i for i in range(nops)},
        compiler_params=pltpu.CompilerParams(has_side_effects=DATAFLOW),
    )(*operands, send_sems, recv_sems, after)
    return res[:nw], res[nw:2 * nw], (res[2 * nw + 1] if has_pack else None)


def _sum_pieces(part, land, sel, name):
    half = part.shape[-2] // 2
    br = 128 if half % 128 == 0 else half // 2
    nb = half // br

    def body(sel_ref, own_ref, *refs):
        acc = own_ref[...]
        for r in refs[:NDEV - 1]:
            acc = acc + r[...].astype(F32)
        refs[NDEV - 1][...] = acc

    if part.ndim == 3:
        own_spec = pl.BlockSpec((None, br, D), lambda i, sel_ref: (sel_ref[0], sel_ref[1] * nb + i, 0))
    else:
        own_spec = pl.BlockSpec((br, D), lambda i, sel_ref: (sel_ref[1] * nb + i, sel_ref[0]))
    slot_specs = [pl.BlockSpec((None, br, D), functools.partial(lambda i, sel_ref, k: (k, i, 0), k=k))
                  for k in range(NDEV - 1)]
    return pl.pallas_call(
        body, name=name,
        grid_spec=pltpu.PrefetchScalarGridSpec(
            num_scalar_prefetch=1, grid=(nb,), in_specs=[own_spec] + slot_specs,
            out_specs=pl.BlockSpec((br, D), lambda i, sel_ref: (i, 0))),
        out_shape=jax.ShapeDtypeStruct((half, D), F32),
        compiler_params=_cparams("arbitrary"),
    )(sel, part, *([land] * (NDEV - 1)))


def _share_start(halves, name, pack=None):
    nw = len(halves)
    lands = [lax.empty(h.shape, F32) for h in halves]
    operands = list(halves) + lands + ([pack, _pack_landing(pack)] if pack is not None else [])
    nops = len(operands)

    def body(*refs):
        h_refs, land_refs = refs[:nw], refs[nw:2 * nw]
        send_sems, recv_sems = refs[nops:nops + 2]
        token = refs[-1]
        x, y, c = _position()
        for w in range(nw):
            _remote(h_refs[w], land_refs[w], send_sems.at[w], recv_sems.at[w], (x, y, 1 - c)).start()
        if pack is not None:
            _pack_copies(refs[2 * nw], refs[2 * nw + 1], send_sems, recv_sems, nw, (x, y, c), True)
        token[...] = jnp.zeros_like(token)

    n = nw + (NDEV - 1 if pack is not None else 0)
    res = _split_call(body, name, operands, (n, n), (TOKEN,))
    return res[0], res[1], res[2:2 + nops], res[-1]


def _share_wait(send_sems, recv_sems, operands, nw, after, name):
    nops = len(operands)
    has_pack = nops > 2 * nw

    def body(*refs):
        h_refs, land_refs = refs[:nw], refs[nw:2 * nw]
        send_sems, recv_sems = refs[nops:nops + 2]
        x, y, c = _position()
        for w in range(nw):
            cp = _remote(h_refs[w], land_refs[w], send_sems.at[w], recv_sems.at[w], (x, y, 1 - c))
            cp.wait_send()
            cp.wait_recv()
        if has_pack:
            _pack_copies(refs[2 * nw], refs[2 * nw + 1], send_sems, recv_sems, nw, (x, y, c), False)

    res = pl.pallas_call(
        body, name=name, out_shape=tuple(pltpu.HBM(a.shape, a.dtype) for a in operands),
        in_specs=(HBM,) * nops + (SEM, SEM, ANY), out_specs=(HBM,) * nops,
        input_output_aliases={i: i for i in range(nops)},
        compiler_params=pltpu.CompilerParams(has_side_effects=DATAFLOW),
    )(*operands, send_sems, recv_sems, after)
    return res[:nw], res[nw:2 * nw], (res[2 * nw + 1] if has_pack else None)


def _join_halves(own, other, c):
    first = jnp.where(c == 0, own, other)
    second = jnp.where(c == 0, other, own)
    return jnp.concatenate([first, second], axis=0)


SMALL_SIZES = (("norm1_g", D), ("sgu_ln_g", GW), ("sgu_ln_b", GW), ("sgu_w", NG * CHUNK * CHUNK),
               ("sgu_b", NG * CHUNK), ("attn_out_g", A), ("gmlp_out_g", GW), ("norm2_g", D),
               ("final_norm_g", D))
PARAM_ROWS = sum(n for _, n in SMALL_SIZES) // LANES
SMALL_ROWS = PARAM_ROWS + 8


def _pack_small(tree, first_extra=None):
    extra = jnp.zeros((8 * LANES,), F32)
    if first_extra is not None:
        extra = extra.at[0].set(first_extra)
    flat = jnp.concatenate([tree[n].reshape(-1) for n, _ in SMALL_SIZES] + [extra])
    return flat.reshape(SMALL_ROWS, LANES)


def _unpack_small(pack, shapes):
    flat = pack.reshape(-1)
    out, off = {}, 0
    for n, size in SMALL_SIZES:
        out[n] = flat[off:off + size].reshape(shapes[n])
        off += size
    return out


def _small_finish(pack_land, norm_land, wpack, mpack, vpack):
    def body(p_ref, n_ref, w_ref, m_ref, v_ref, go_ref, d_ref, mo_ref, vo_ref):
        total = p_ref[0]
        late = n_ref[0]
        for k in range(1, NDEV):
            total = total + p_ref[k]
            late = late + n_ref[k]
        go_ref[...] = total
        go_ref[0:8, :] = total[0:8, :] + late
        d, mn, vn = _adamw_math(w_ref[...], go_ref[...], m_ref[...], v_ref[...])
        d_ref[...] = d
        mo_ref[...] = mn
        vo_ref[...] = vn

    sd = jax.ShapeDtypeStruct((SMALL_ROWS, LANES), F32)
    vm = pl.BlockSpec(memory_space=pltpu.VMEM)
    return pl.pallas_call(
        body, name="small_finish", in_specs=[vm] * 5, out_specs=[vm] * 4, out_shape=[sd] * 4,
        compiler_params=_cparams(),
    )(pack_land, norm_land, wpack, mpack, vpack)


def kernel(x, norm1_g, w_in, sgu_ln_g, sgu_ln_b, sgu_w, sgu_b, attn_out_g, gmlp_out_g, w_out, norm2_g, w_ff1, w_ff2, final_norm_g, loss_target, m_norm1_g, m_w_in, m_sgu_ln_g, m_sgu_ln_b, m_sgu_w, m_sgu_b, m_attn_out_g, m_gmlp_out_g, m_w_out, m_norm2_g, m_w_ff1, m_w_ff2, m_final_norm_g, v_norm1_g, v_w_in, v_sgu_ln_g, v_sgu_ln_b, v_sgu_w, v_sgu_b, v_attn_out_g, v_gmlp_out_g, v_w_out, v_norm2_g, v_w_ff1, v_w_ff2, v_final_norm_g):
    names = [n for n, _ in SMALL_SIZES]
    w_small = dict(norm1_g=norm1_g, sgu_ln_g=sgu_ln_g, sgu_ln_b=sgu_ln_b, sgu_w=sgu_w, sgu_b=sgu_b,
                   attn_out_g=attn_out_g, gmlp_out_g=gmlp_out_g, norm2_g=norm2_g, final_norm_g=final_norm_g)
    m_small = dict(norm1_g=m_norm1_g, sgu_ln_g=m_sgu_ln_g, sgu_ln_b=m_sgu_ln_b, sgu_w=m_sgu_w, sgu_b=m_sgu_b,
                   attn_out_g=m_attn_out_g, gmlp_out_g=m_gmlp_out_g, norm2_g=m_norm2_g,
                   final_norm_g=m_final_norm_g)
    v_small = dict(norm1_g=v_norm1_g, sgu_ln_g=v_sgu_ln_g, sgu_ln_b=v_sgu_ln_b, sgu_w=v_sgu_w, sgu_b=v_sgu_b,
                   attn_out_g=v_attn_out_g, gmlp_out_g=v_gmlp_out_g, norm2_g=v_norm2_g,
                   final_norm_g=v_final_norm_g)
    shapes = {n: w_small[n].shape for n in names}

    start_in = _gather_start([w_in[0].T.astype(BF16)], "gather_in_start")
    issued = start_in[4][0:1, 0:1]
    start_rest = _gather_start([(w_out[0] + issued).astype(BF16), w_ff1[0].astype(BF16), w_ff2[0].astype(BF16)],
                               "gather_rest_start")
    hn1 = _norm1(x[0], norm1_g + start_rest[4][0:1, 0:1])
    win_t = _gather_wait(*start_in[:4], after=hn1, name="gather_in_wait")[0].reshape(INW, D)

    def rest_weights(after):
        wout, wff1, wff2 = _gather_wait(*start_rest[:4], after=after, name="gather_rest_wait")
        return wout.reshape(D, D), wff1, wff2.reshape(DFF, D)

    small = dict(
        norm1_g=norm1_g, ln_g=sgu_ln_g.reshape(1, GW), ln_b=sgu_ln_b.reshape(1, GW), sgu_w=sgu_w[0],
        sgu_wt=jnp.swapaxes(sgu_w[0], 1, 2), bias_t=jnp.repeat(sgu_b[0].T, DH, axis=1),
        attn_out_g=attn_out_g, gmlp_out_g=gmlp_out_g, norm2_g=norm2_g, final_norm_g=final_norm_g.reshape(1, D))
    xi, yi, ci = _position()
    sel = jnp.stack([2 * xi + yi, ci]).astype(jnp.int32)
    state = {}

    def as_slabs(g):
        return g.reshape(NCHIP, g.shape[0] // NCHIP, D)

    def early_grads(gwff1, gwff2, gwout):
        state["early"] = _reduce_start([gwff1, as_slabs(gwff2), as_slabs(gwout)], "reduce_early_start")
        return state["early"][3][0:1, 0:1]

    def after_attention_bwd(marker, partial, loss_part):
        send_sems, recv_sems, operands, _ = state["early"]
        parts, lands, _ = _reduce_wait(send_sems, recv_sems, operands, 3, marker, "reduce_early_wait")
        halves = [_sum_pieces(p, l, sel, "sum_" + n) for p, l, n in zip(parts, lands, ("w_ff1", "w_ff2", "w_out"))]
        pack = _pack_small(dict(partial, norm1_g=jnp.zeros((1, D), F32), sgu_ln_g=partial["ln_g"],
                                sgu_ln_b=partial["ln_b"]), loss_part)
        state["early_share"] = _share_start(halves, "share_early_start", pack)
        return state["early_share"][3][0:1, 0:1]

    def late_grads(gwin_t, gwin_low):
        state["late"] = _reduce_start([as_slabs(gwin_low)], "reduce_late_start")
        state["late_own"] = as_slabs(gwin_t)
        return state["late"][3][0:1, 0:1]

    _, dx, sg, _ = _local_step(
        x[0], hn1, loss_target[0], small, win_t, rest_weights, early_grads, after_attention_bwd, late_grads)
    send_sems, recv_sems, operands, _ = state["early_share"]
    own, other, pack_land = _share_wait(send_sems, recv_sems, operands, 3, dx, "share_early_wait")
    send_sems, recv_sems, operands, _ = state["late"]
    _, late_lands, _ = _reduce_wait(send_sems, recv_sems, operands, 1, dx, "reduce_late_wait")
    late_share = _share_start([_sum_pieces(state["late_own"], late_lands[0], sel, "sum_w_in")], "share_late_start",
                              sg["norm1_g"].reshape(8, LANES))
    issued = late_share[3][0:1, 0:1]
    g_big = {n: _join_halves(o, t, ci) + issued for n, o, t in zip(("w_ff1", "w_ff2", "w_out"), own, other)}
    w_big = dict(w_in=(w_in, m_w_in, v_w_in), w_out=(w_out, m_w_out, v_w_out),
                 w_ff1=(w_ff1, m_w_ff1, v_w_ff1), w_ff2=(w_ff2, m_w_ff2, v_w_ff2))
    grads, deltas, new_m, new_v = {}, {}, {}, {}

    def update(n):
        w, m, v = w_big[n]
        d, mn, vn = _adamw(w[0], g_big[n], m[0], v[0], "adamw_" + n)
        grads[n], deltas[n], new_m[n], new_v[n] = g_big[n][None], d[None], mn[None], vn[None]

    for n in ("w_ff1", "w_ff2", "w_out"):
        update(n)
    updated = deltas["w_out"][0, 0:8, 0:LANES] + deltas["w_ff1"][0, 0:8, 0:LANES] + deltas["w_ff2"][0, 0:8, 0:LANES]
    own, other, norm_land = _share_wait(late_share[0], late_share[1], late_share[2], 1, updated, "share_late_wait")
    g_big["w_in"] = _join_halves(own[0], other[0], ci).T
    update("w_in")

    packs = _small_finish(pack_land, norm_land, _pack_small(w_small), _pack_small(m_small), _pack_small(v_small))
    loss = packs[0][PARAM_ROWS, 0]
    for tree, pack in zip((grads, deltas, new_m, new_v), packs):
        tree.update(_unpack_small(pack, shapes))

    order = ["norm1_g", "w_in", "sgu_ln_g", "sgu_ln_b", "sgu_w", "sgu_b", "attn_out_g", "gmlp_out_g", "w_out",
             "norm2_g", "w_ff1", "w_ff2", "final_norm_g"]
    return (loss, dx[None], *[grads[n] for n in order], *[deltas[n] for n in order],
            *[new_m[n] for n in order], *[new_v[n] for n in order])
```

```python
import functools
import math

import numpy as np
import jax
import jax.numpy as jnp
from jax import lax
from jax.experimental import pallas as pl
from jax.experimental.pallas import tpu as pltpu

F32 = jnp.float32
BF16 = jnp.bfloat16

D = 1024
NH = 12
DH = 64
A = NH * DH
NG = 4
GW = NG * DH
INW = 3 * A + 2 * GW
DFF = 4 * D
CHUNK = 128
PATTERNS = ((128, 1), (512, 4), (2048, 16))
EPS = 1e-6
SCALE = DH ** -0.5
LOG2E = 1.0 / math.log(2.0)
LN2 = math.log(2.0)
NEG = -1e30

LR, B1, B2, AEPS, WD, STEP = 0.001, 0.9, 0.999, 1e-08, 0.01, 10

TM = 512
TM_BIG = 1024
TMX = 512
ATT_ROWS = 4096
FF_CH = 1024
LANES = 128
NCHIP = 4
VMEM_LIMIT = 56 * 1024 * 1024
MESH = pl.DeviceIdType.MESH


def _cparams(*sem, **kw):
    return pltpu.CompilerParams(dimension_semantics=sem if sem else None,
                                vmem_limit_bytes=VMEM_LIMIT, **kw)


def _dot(a, b):
    return jnp.dot(a, b, preferred_element_type=F32)


def _dot_nt(a, b):
    return lax.dot_general(a, b, (((1,), (1,)), ((), ())), preferred_element_type=F32)


def _dot_tn(a, b):
    return lax.dot_general(a, b, (((0,), (0,)), ((), ())), preferred_element_type=F32)


def _dot_hi(a, b):
    return jnp.dot(a, b, preferred_element_type=F32, precision=lax.Precision.HIGHEST)


def _alibi_slopes(n):
    def pow2(m):
        start = 2.0 ** (-8.0 / m)
        return [start ** (i + 1) for i in range(m)]
    if math.log2(n).is_integer():
        s = pow2(n)
    else:
        c = 2 ** int(math.floor(math.log2(n)))
        s = pow2(c) + pow2(2 * c)[0::2][: n - c]
    return np.asarray(s, dtype=np.float32)


def _rms_fwd(v, g):
    r = lax.rsqrt(jnp.mean(v * v, axis=-1, keepdims=True) + EPS)
    vn = v * r
    return vn * g, vn, r


def _rms_bwd(dy, vn, r, g):
    w = dy * g
    dv = r * (w - vn * jnp.mean(w * vn, axis=-1, keepdims=True))
    return dv, jnp.sum(dy * vn, axis=0, keepdims=True)


_K0 = math.sqrt(2.0 / math.pi)
_K1 = 0.044715


def _gelu(v):
    return 0.5 * v * (1.0 + jnp.tanh(_K0 * (v + _K1 * (v * v * v))))


def _gelu_grad(v):
    t = jnp.tanh(_K0 * (v + _K1 * (v * v * v)))
    return 0.5 * (1.0 + t) + 0.5 * v * (1.0 - t * t) * (_K0 * (1.0 + 3.0 * _K1 * v * v))


def _row_spec(rows, cols):
    return pl.BlockSpec((rows, cols), lambda i: (i, 0))


def _const_spec(shape):
    nd = len(shape)
    return pl.BlockSpec(shape, lambda i: (0,) * nd, pipeline_mode=pl.Buffered(1))


DILS = tuple(d for _, d in PATTERNS)


def _fill_cols(scr, value):
    for cb in range(value.shape[1] // LANES):
        scr[cb] = value[:, cb * LANES:(cb + 1) * LANES]


def _split_residues(scr, out_ref, dil):
    nb, rows, _ = scr.shape
    for r in range(dil):
        for cb in range(nb):
            piece = scr.at[cb][pl.ds(r, rows // dil, stride=dil), :]
            out_ref[r, :, cb * LANES:(cb + 1) * LANES] = piece.astype(out_ref.dtype)


def _merge_residues(in_ref, scr, dil):
    nb, rows, _ = scr.shape
    for r in range(dil):
        for cb in range(nb):
            scr.at[cb][pl.ds(r, rows // dil, stride=dil), :] = in_ref[r, :, cb * LANES:(cb + 1) * LANES].astype(F32)
    return jnp.concatenate([scr[cb] for cb in range(nb)], axis=-1)


def _col_scratch(rows, width):
    return pltpu.VMEM((width // LANES, rows, LANES), F32)


def _res_spec(dil, rows, width):
    return pl.BlockSpec((dil, rows // dil, width), lambda i: (0, i, 0))


def _res_shape(s, dil, width, dtype):
    return jax.ShapeDtypeStruct((dil, s // dil, width), dtype)


def _norm1(x, g1):
    s = x.shape[0]

    def body(x_ref, g_ref, hn_ref):
        hn, _, _ = _rms_fwd(x_ref[...], g_ref[...])
        hn_ref[...] = hn.astype(BF16)

    return pl.pallas_call(
        body, name="norm1", grid=(s // TM,), in_specs=[_row_spec(TM, D), _const_spec((1, D))],
        out_specs=_row_spec(TM, D), out_shape=jax.ShapeDtypeStruct((s, D), BF16),
        compiler_params=_cparams("arbitrary"),
    )(x, g1)


def _inproj_fwd(hn1, win_t):
    s = hn1.shape[0]
    nd = len(DILS)

    def body(hn_ref, w_ref, *rest):
        qkv_refs = rest[:3 * nd]
        u_ref, z_ref, scr = rest[3 * nd:]
        hn = hn_ref[...]
        for t in range(3):
            seg = _dot_nt(hn, w_ref[t * A:(t + 1) * A, :])
            seg = seg * (SCALE * LOG2E) if t == 0 else seg
            _fill_cols(scr, seg)
            for di, dil in enumerate(DILS):
                if dil == 1:
                    qkv_refs[t * nd + di][0] = seg.astype(BF16)
                else:
                    _split_residues(scr, qkv_refs[t * nd + di], dil)
        u_ref[...] = _dot_nt(hn, w_ref[3 * A:3 * A + GW, :])
        z_ref[...] = _dot_nt(hn, w_ref[3 * A + GW:INW, :])

    res = pl.pallas_call(
        body, name="inproj_fwd", grid=(s // TM_BIG,),
        in_specs=[_row_spec(TM_BIG, D), _const_spec((INW, D))],
        out_specs=[_res_spec(d, TM_BIG, A) for _ in range(3) for d in DILS]
                  + [_row_spec(TM_BIG, GW), _row_spec(TM_BIG, GW)],
        out_shape=[_res_shape(s, d, A, BF16) for _ in range(3) for d in DILS]
                  + [jax.ShapeDtypeStruct((s, GW), F32)] * 2,
        scratch_shapes=[_col_scratch(TM_BIG, A)],
        compiler_params=_cparams("arbitrary"),
    )(hn1, win_t)
    q, k, v = (res[t * nd:(t + 1) * nd] for t in range(3))
    return q, k, v, res[-2], res[-1]


def _att_geometry(length, dil):
    merge = max(1, min(dil, ATT_ROWS // length))
    rows = min(length * merge, ATT_ROWS)
    nsub = rows // CHUNK
    return merge, rows, length * merge // rows, nsub, min(length // CHUNK, nsub)


def _merged(t, merge):
    return t.reshape(t.shape[0] // merge, t.shape[1] * merge, t.shape[2])


def _stack_heads(t):
    lane = lax.broadcasted_iota(jnp.int32, t.shape, 1)
    zero = jnp.zeros_like(t)
    return jnp.concatenate([jnp.where(lane < DH, t, zero), jnp.where(lane >= DH, t, zero)], axis=0)


def _head_cols(t, hp):
    lane = lax.broadcasted_iota(jnp.int32, t.shape, 1)
    cols = [jnp.sum(jnp.where(lane == 2 * hp + h, t, 0.0), axis=-1, keepdims=True) for h in range(2)]
    return jnp.concatenate(cols, axis=0)


def _unstack_heads(t2):
    n = t2.shape[0] // 2
    lane = lax.broadcasted_iota(jnp.int32, (n, LANES), 1)
    return jnp.where(lane < DH, t2[:n], t2[n:])


def _query_window_bias(s0, s1, dil, first):
    row = lax.broadcasted_iota(jnp.int32, (2 * CHUNK, 2 * CHUNK), 0)
    col = lax.broadcasted_iota(jnp.int32, (2 * CHUNK, 2 * CHUNK), 1)
    steps = (row & (CHUNK - 1)) + CHUNK - col
    valid = (steps >= 0) & (steps <= CHUNK)
    if first:
        valid = valid & (col >= CHUNK)
    slope = jnp.where(row < CHUNK, s0, s1)
    return jnp.where(valid, -slope * (steps * dil).astype(F32), NEG)


def _key_block_bias(s0, s1, dil, last):
    key = lax.broadcasted_iota(jnp.int32, (CHUNK, 4 * CHUNK), 0)
    col = lax.broadcasted_iota(jnp.int32, (CHUNK, 4 * CHUNK), 1)
    wq = col & (2 * CHUNK - 1)
    steps = wq - key
    valid = (steps >= 0) & (steps <= CHUNK)
    if last:
        valid = valid & (wq < CHUNK)
    slope = jnp.where(col < 2 * CHUNK, s0, s1)
    return jnp.where(valid, -slope * (steps * dil).astype(F32), NEG)


def _head_rows(t, hp):
    row = lax.broadcasted_iota(jnp.int32, (8, LANES), 0)
    lane = lax.broadcasted_iota(jnp.int32, (8, LANES), 1)
    pick = jnp.where((row < 2) & (lane == 2 * hp + row), 1.0, 0.0).astype(BF16)
    hi = t.astype(BF16)
    rest = t - hi.astype(F32)
    mid = rest.astype(BF16)
    low = (rest - mid.astype(F32)).astype(BF16)
    return _dot_nt(pick, hi) + _dot_nt(pick, mid) + _dot_nt(pick, low)


def _att_specs(dil, rows, nsub, nblk):
    main = pl.BlockSpec((None, rows, LANES), lambda r, c, hp: (r, c, hp))
    prev = pl.BlockSpec((None, CHUNK, LANES), lambda r, c, hp: (r, jnp.maximum(c * nsub - 1, 0), hp))
    nxt = pl.BlockSpec((None, CHUNK, LANES), lambda r, c, hp: (r, jnp.minimum((c + 1) * nsub, nblk - 1), hp))
    main_heads = pl.BlockSpec((None, rows, LANES), lambda r, c, hp: (r, c, 0))
    nxt_heads = pl.BlockSpec((None, CHUNK, LANES), lambda r, c, hp: (r, jnp.minimum((c + 1) * nsub, nblk - 1), 0))
    return main, prev, nxt, main_heads, nxt_heads


def _row_start(i):
    return i * CHUNK if isinstance(i, int) else pl.multiple_of(i * CHUNK, CHUNK)


def _first_blocks(block, nsub, seg, nch, ch, first_bias, bias_buf):
    for i in range(nsub):
        if i % seg:
            block(i, bias_buf[...])
        elif nch == 1:
            block(i, first_bias())
        else:
            block(i, jnp.where(ch == 0, first_bias(), bias_buf[...]))


def _last_blocks(block, nsub, seg, nch, ch, last_bias, bias_buf):
    for i in range(nsub):
        if (i + 1) % seg:
            block(i, bias_buf[...])
        elif nch == 1:
            block(i, last_bias())
        else:
            block(i, jnp.where(ch == nch - 1, last_bias(), bias_buf[...]))


def _attn_fwd(q, k, v, slopes, dil):
    length = q.shape[1]
    merge, rows, nch, nsub, seg = _att_geometry(length, dil)
    main, prev, _, main_heads, _ = _att_specs(dil, rows, nsub, length * merge // CHUNK)
    q, k, v = (_merged(t, merge) for t in (q, k, v))

    def body(sl_ref, q_ref, k_ref, v_ref, kh_ref, vh_ref, o_ref, lse_ref, kbuf, vbuf, bias_buf):
        ch = pl.program_id(1)
        hp = pl.program_id(2)
        lane = lax.broadcasted_iota(jnp.int32, (CHUNK, LANES), 1)
        kbuf[0:CHUNK, :] = kh_ref[...]
        kbuf[CHUNK:, :] = k_ref[...]
        vbuf[0:CHUNK, :] = vh_ref[...]
        vbuf[CHUNK:, :] = v_ref[...]
        s0, s1 = sl_ref[2 * hp], sl_ref[2 * hp + 1]

        def block(i, bias):
            row = _row_start(i)
            rs = pl.ds(row, CHUNK)
            q2 = _stack_heads(q_ref[rs, :])
            kw = kbuf[pl.ds(row, 2 * CHUNK), :]
            vw = vbuf[pl.ds(row, 2 * CHUNK), :]
            sc = _dot_nt(q2, kw) + bias
            m = jnp.max(sc, axis=-1, keepdims=True)
            p = jnp.exp2(sc - m)
            l = jnp.sum(p, axis=-1, keepdims=True)
            o2 = _dot(p.astype(BF16), vw) * (1.0 / l)
            o_ref[rs, :] = _unstack_heads(o2).astype(BF16)
            lse = m + jnp.log2(l)
            seen = jnp.where(hp == 0, 0.0, lse_ref[rs, :])
            lse_ref[rs, :] = jnp.where(lane == 2 * hp, lse[:CHUNK], jnp.where(lane == 2 * hp + 1, lse[CHUNK:], seen))

        bias_buf[...] = _query_window_bias(s0, s1, dil, False)
        _first_blocks(block, nsub, seg, nch, ch, lambda: _query_window_bias(s0, s1, dil, True), bias_buf)

    sd = jax.ShapeDtypeStruct
    o, lse = pl.pallas_call(
        body, name=f"attn_fwd_d{dil}", grid=(dil // merge, nch, NH // 2),
        in_specs=[pl.BlockSpec(memory_space=pltpu.SMEM), main, main, main, prev, prev],
        out_specs=[main, main_heads],
        out_shape=[sd((dil // merge, length * merge, A), BF16), sd((dil // merge, length * merge, LANES), F32)],
        scratch_shapes=[pltpu.VMEM((rows + CHUNK, LANES), BF16), pltpu.VMEM((rows + CHUNK, LANES), BF16),
                        pltpu.VMEM((2 * CHUNK, 2 * CHUNK), F32)],
        compiler_params=_cparams("arbitrary", "arbitrary", "arbitrary"),
    )(slopes, q, k, v, k, v)
    return o.reshape(dil, length, A), lse.reshape(dil, length, LANES)


def _attn_bwd_dq(q, k, v, do, lse, delta, slopes, dil):
    length = q.shape[1]
    merge, rows, nch, nsub, seg = _att_geometry(length, dil)
    main, prev, _, main_heads, _ = _att_specs(dil, rows, nsub, length * merge // CHUNK)
    q, k, v, do, lse, delta = (_merged(t, merge) for t in (q, k, v, do, lse, delta))

    def body(sl_ref, q_ref, k_ref, v_ref, do_ref, lse_ref, dl_ref, kh_ref, vh_ref, dq_ref, kbuf, vbuf, bias_buf):
        ch = pl.program_id(1)
        hp = pl.program_id(2)
        kbuf[0:CHUNK, :] = kh_ref[...]
        kbuf[CHUNK:, :] = k_ref[...]
        vbuf[0:CHUNK, :] = vh_ref[...]
        vbuf[CHUNK:, :] = v_ref[...]
        s0, s1 = sl_ref[2 * hp], sl_ref[2 * hp + 1]

        def block(i, bias):
            row = _row_start(i)
            rs = pl.ds(row, CHUNK)
            q2 = _stack_heads(q_ref[rs, :])
            do2 = _stack_heads(do_ref[rs, :])
            lse2 = _head_cols(lse_ref[rs, :], hp)
            dl2 = _head_cols(dl_ref[rs, :], hp)
            kw = kbuf[pl.ds(row, 2 * CHUNK), :]
            vw = vbuf[pl.ds(row, 2 * CHUNK), :]
            p = jnp.exp2(_dot_nt(q2, kw) + bias - lse2)
            ds = p * (_dot_nt(do2, vw) - dl2)
            dq_ref[rs, :] = _unstack_heads(_dot(ds.astype(BF16), kw)).astype(BF16)

        bias_buf[...] = _query_window_bias(s0, s1, dil, False)
        _first_blocks(block, nsub, seg, nch, ch, lambda: _query_window_bias(s0, s1, dil, True), bias_buf)

    dq = pl.pallas_call(
        body, name=f"attn_dq_d{dil}", grid=(dil // merge, nch, NH // 2),
        in_specs=[pl.BlockSpec(memory_space=pltpu.SMEM), main, main, main, main, main_heads, main_heads, prev, prev],
        out_specs=main, out_shape=jax.ShapeDtypeStruct((dil // merge, length * merge, A), BF16),
        scratch_shapes=[pltpu.VMEM((rows + CHUNK, LANES), BF16), pltpu.VMEM((rows + CHUNK, LANES), BF16),
                        pltpu.VMEM((2 * CHUNK, 2 * CHUNK), F32)],
        compiler_params=_cparams("arbitrary", "arbitrary", "arbitrary"),
    )(slopes, q, k, v, do, lse, delta, k, v)
    return dq.reshape(dil, length, A)


def _attn_bwd_dkv(q, k, v, do, lse, delta, slopes, dil):
    length = q.shape[1]
    merge, rows, nch, nsub, seg = _att_geometry(length, dil)
    main, _, nxt, main_heads, nxt_heads = _att_specs(dil, rows, nsub, length * merge // CHUNK)
    q, k, v, do, lse, delta = (_merged(t, merge) for t in (q, k, v, do, lse, delta))

    def body(sl_ref, k_ref, v_ref, q_ref, do_ref, lse_ref, dl_ref, qh_ref, doh_ref, lseh_ref, dlh_ref,
             dk_ref, dv_ref, qbuf, dobuf, lse_rows, dl_rows, bias_buf):
        ch = pl.program_id(1)
        hp = pl.program_id(2)
        for buf, main_ref, halo_ref in ((qbuf, q_ref, qh_ref), (dobuf, do_ref, doh_ref)):
            buf[0:rows, :] = main_ref[...]
            buf[rows:, :] = halo_ref[...]
        for buf, main_ref, halo_ref in ((lse_rows, lse_ref, lseh_ref), (dl_rows, dl_ref, dlh_ref)):
            buf[:, 0:rows] = _head_rows(main_ref[...], hp)
            buf[:, rows:] = _head_rows(halo_ref[...], hp)
        s0, s1 = sl_ref[2 * hp], sl_ref[2 * hp + 1]

        def block(i, bias):
            row = _row_start(i)
            rs = pl.ds(row, CHUNK)
            win = pl.ds(row, 2 * CHUNK)
            kc = k_ref[rs, :]
            vc = v_ref[rs, :]
            q2 = _stack_heads(qbuf[win, :])
            do2 = _stack_heads(dobuf[win, :])
            cols = slice(i * CHUNK, (i + 2) * CHUNK)
            lse2 = jnp.concatenate([lse_rows[0:1, cols], lse_rows[1:2, cols]], axis=1)
            dl2 = jnp.concatenate([dl_rows[0:1, cols], dl_rows[1:2, cols]], axis=1)
            pt = jnp.exp2(_dot_nt(kc, q2) + bias - lse2)
            dst = pt * (_dot_nt(vc, do2) - dl2)
            dv_ref[rs, :] = _dot(pt.astype(BF16), do2).astype(BF16)
            dk_ref[rs, :] = (_dot(dst.astype(BF16), q2) * LN2).astype(BF16)

        bias_buf[...] = _key_block_bias(s0, s1, dil, False)
        _last_blocks(block, nsub, seg, nch, ch, lambda: _key_block_bias(s0, s1, dil, True), bias_buf)

    sd = jax.ShapeDtypeStruct((dil // merge, length * merge, A), BF16)
    dk, dv = pl.pallas_call(
        body, name=f"attn_dkv_d{dil}", grid=(dil // merge, nch, NH // 2),
        in_specs=[pl.BlockSpec(memory_space=pltpu.SMEM), main, main, main, main, main_heads, main_heads,
                  nxt, nxt, nxt_heads, nxt_heads],
        out_specs=[main, main], out_shape=[sd, sd],
        scratch_shapes=[pltpu.VMEM((rows + CHUNK, LANES), BF16), pltpu.VMEM((rows + CHUNK, LANES), BF16),
                        pltpu.VMEM((8, rows + CHUNK), F32), pltpu.VMEM((8, rows + CHUNK), F32),
                        pltpu.VMEM((CHUNK, 4 * CHUNK), F32)],
        compiler_params=_cparams("arbitrary", "arbitrary", "arbitrary"),
    )(slopes, k, v, q, do, lse, delta, q, do, lse, delta)
    return dk.reshape(dil, length, A), dv.reshape(dil, length, A)


def _group_masks(width):
    lane = lax.broadcasted_iota(jnp.int32, (1, width), 1)
    return [(lane >= g * DH) & (lane < (g + 1) * DH) for g in range(width // DH)]


def _group_mean_matrix():
    i = lax.broadcasted_iota(jnp.int32, (GW, GW), 0) // DH
    j = lax.broadcasted_iota(jnp.int32, (GW, GW), 1) // DH
    return jnp.where(i == j, 1.0 / DH, 0.0).astype(F32)


def _tri_mask(lower):
    t = lax.broadcasted_iota(jnp.int32, (CHUNK, CHUNK), 0)
    u = lax.broadcasted_iota(jnp.int32, (CHUNK, CHUNK), 1)
    return (u <= t) if lower else (u >= t)


def _sgu_forward(u, z, lng, lnb, w_ref, bias_t, pmat, rows):
    ug = _gelu(u)
    zg = _gelu(z)
    mu = _dot_hi(zg, pmat)
    zc = zg - mu
    var = _dot_hi(zc * zc, pmat)
    rstd = lax.rsqrt(var + EPS)
    zhat = zc * rstd
    zn = (zhat * lng + lnb).astype(BF16)
    gm = _group_masks(GW)
    tri = _tri_mask(True)
    ws = [jnp.where(tri, w_ref[g], 0.0).astype(BF16) for g in range(NG)]
    pieces = []
    for c in range(rows // CHUNK):
        znc = zn[c * CHUNK:(c + 1) * CHUNK, :]
        mix = None
        for g in range(NG):
            part = jnp.where(gm[g], _dot(ws[g], znc), 0.0)
            mix = part if mix is None else mix + part
        pieces.append(mix + bias_t)
    mixed = jnp.concatenate(pieces, axis=0) if len(pieces) > 1 else pieces[0]
    return ug * mixed, ug, zhat, rstd, zn, mixed


def _head_spread():
    h = lax.broadcasted_iota(jnp.int32, (LANES, A), 0)
    lane = lax.broadcasted_iota(jnp.int32, (LANES, A), 1)
    return jnp.where(lane // DH == h, 1.0, 0.0).astype(BF16)


def _bf16_pieces(t, n):
    pieces = []
    for _ in range(n):
        piece = t.astype(BF16)
        pieces.append(piece)
        t = t - piece.astype(F32)
    return pieces


def _mix_fwd(os_, ls_, u, z, x, lng, lnb, sgu_w, bias_t, ga, gg, wout):
    s = x.shape[0]
    nd = len(DILS)
    nscr = sum(1 for d in DILS if d > 1)

    def body(*refs):
        o_refs, l_refs = refs[:nd], refs[nd:2 * nd]
        u_ref, z_ref, x_ref, lng_ref, lnb_ref, w_ref, bt_ref, ga_ref, gg_ref, wo_ref = refs[2 * nd:2 * nd + 10]
        attn_ref = refs[2 * nd + 10]
        lse_refs = refs[2 * nd + 11:3 * nd + 11]
        mixed_ref, h1_ref = refs[3 * nd + 11:3 * nd + 13]
        scr = refs[3 * nd + 13:]
        scr_o, scr_l, scr_lse = scr[:nscr], scr[nscr:2 * nscr], scr[2 * nscr]
        ov, lv, j = [], [], 0
        for di, dil in enumerate(DILS):
            if dil == 1:
                ov.append(o_refs[di][0].astype(F32))
                lv.append(l_refs[di][0])
            else:
                ov.append(_merge_residues(o_refs[di], scr_o[j], dil))
                lv.append(_merge_residues(l_refs[di], scr_l[j], dil))
                j += 1
        mx = functools.reduce(jnp.maximum, lv)
        es = [jnp.exp2(l - mx) for l in lv]
        den = functools.reduce(lambda a, b: a + b, es)
        spread = _head_spread()
        attn = None
        for e, o in zip(es, ov):
            wide = functools.reduce(lambda a, b: a + b, [_dot(piece, spread) for piece in _bf16_pieces(e / den, 2)])
            attn = wide * o if attn is None else attn + wide * o
        attn_ref[...] = attn
        lse = mx + jnp.log2(den)
        _fill_cols(scr_lse, lse)
        for di, dil in enumerate(DILS):
            if dil == 1:
                lse_refs[di][0] = lse
            else:
                _split_residues(scr_lse, lse_refs[di], dil)
        an, _, _ = _rms_fwd(attn, ga_ref[...])
        gmv, _, _, _, _, _ = _sgu_forward(u_ref[...], z_ref[...], lng_ref[...], lnb_ref[...], w_ref,
                                          bt_ref[...], _group_mean_matrix(), TMX)
        gn, _, _ = _rms_fwd(gmv, gg_ref[...])
        mixed = jnp.concatenate([an, gn], axis=-1).astype(BF16)
        mixed_ref[...] = mixed
        h1_ref[...] = x_ref[...] + _dot(mixed, wo_ref[...])

    sd = jax.ShapeDtypeStruct
    res = pl.pallas_call(
        body, name="mix_fwd", grid=(s // TMX,),
        in_specs=[_res_spec(d, TMX, A) for d in DILS] + [_res_spec(d, TMX, LANES) for d in DILS]
                 + [_row_spec(TMX, GW), _row_spec(TMX, GW),
                    _row_spec(TMX, D), _const_spec((1, GW)), _const_spec((1, GW)), _const_spec((NG, CHUNK, CHUNK)),
                    _const_spec((CHUNK, GW)), _const_spec((1, A)), _const_spec((1, GW)), _const_spec((D, D))],
        out_specs=[_row_spec(TMX, A)] + [_res_spec(d, TMX, LANES) for d in DILS]
                  + [_row_spec(TMX, D), _row_spec(TMX, D)],
        out_shape=[sd((s, A), F32)] + [_res_shape(s, d, LANES, F32) for d in DILS]
                  + [sd((s, D), BF16), sd((s, D), F32)],
        scratch_shapes=[_col_scratch(TMX, A)] * nscr + [_col_scratch(TMX, LANES)] * (nscr + 1),
        compiler_params=_cparams("arbitrary"),
    )(*os_, *ls_, u, z, x, lng, lnb, sgu_w, bias_t, ga, gg, wout)
    return res[0], res[1:1 + nd], res[1 + nd], res[2 + nd]


def _mlp_fwd(h1, g2, wff1, wff2, gf, target):
    s = h1.shape[0]

    def body(h1_ref, g2_ref, w1_ref, w2_ref, gf_ref, t_ref, hn_ref, rf_ref, dh2_ref, loss_ref, dgf_ref):
        i = pl.program_id(0)
        h1v = h1_ref[...]
        hn, _, _ = _rms_fwd(h1v, g2_ref[...])
        hn = hn.astype(BF16)
        hn_ref[...] = hn
        acc = h1v
        for j in range(DFF // FF_CH):
            cols = slice(j * FF_CH, (j + 1) * FF_CH)
            rf = jnp.maximum(_dot(hn, w1_ref[j]), 0.0)
            act = (rf * rf).astype(BF16)
            rf_ref[:, cols] = rf.astype(BF16)
            acc = acc + _dot(act, w2_ref[cols, :])
        y, h2n, r3 = _rms_fwd(acc, gf_ref[...])
        err = y - t_ref[...]
        part = 0.5 * jnp.sum(jnp.mean(err * err, axis=-1, keepdims=True), axis=0, keepdims=True)
        dy = err * (1.0 / D)
        dh2, dgf = _rms_bwd(dy, h2n, r3, gf_ref[...])
        dh2_ref[...] = dh2

        @pl.when(i == 0)
        def _():
            loss_ref[...] = jnp.zeros_like(loss_ref)
            dgf_ref[...] = jnp.zeros_like(dgf_ref)

        loss_ref[...] += jnp.broadcast_to(part, loss_ref.shape)
        dgf_ref[...] += dgf

    sd = jax.ShapeDtypeStruct
    return pl.pallas_call(
        body, name="mlp_fwd", grid=(s // TM,),
        in_specs=[_row_spec(TM, D), _const_spec((1, D)), _const_spec((DFF // FF_CH, D, FF_CH)), _const_spec((DFF, D)),
                  _const_spec((1, D)), _row_spec(TM, D)],
        out_specs=[_row_spec(TM, D), _row_spec(TM, DFF), _row_spec(TM, D),
                   _const_spec((1, LANES)), _const_spec((1, D))],
        out_shape=[sd((s, D), BF16), sd((s, DFF), BF16), sd((s, D), F32),
                   sd((1, LANES), F32), sd((1, D), F32)],
        compiler_params=_cparams("arbitrary"),
    )(h1, g2, wff1, wff2, gf, target)


def _mlp_bwd(dh2, rf, h1, g2, wff1, wff2):
    s = h1.shape[0]

    def body(dh2_ref, rf_ref, h1_ref, g2_ref, w1_ref, w2_ref, df_ref, dh1_ref, dg2_ref):
        i = pl.program_id(0)
        dh2v = dh2_ref[...]
        dh2b = dh2v.astype(BF16)
        dhn = jnp.zeros((TM, D), F32)
        for j in range(DFF // FF_CH):
            cols = slice(j * FF_CH, (j + 1) * FF_CH)
            da = _dot_nt(dh2b, w2_ref[cols, :])
            df = (da * (2.0 * rf_ref[:, cols].astype(F32))).astype(BF16)
            df_ref[:, cols] = df
            dhn = dhn + _dot_nt(df, w1_ref[j])
        _, h1n, r2 = _rms_fwd(h1_ref[...], g2_ref[...])
        dres, dg2 = _rms_bwd(dhn, h1n, r2, g2_ref[...])
        dh1_ref[...] = dh2v + dres

        @pl.when(i == 0)
        def _():
            dg2_ref[...] = jnp.zeros_like(dg2_ref)

        dg2_ref[...] += dg2

    sd = jax.ShapeDtypeStruct
    return pl.pallas_call(
        body, name="mlp_bwd", grid=(s // TM,),
        in_specs=[_row_spec(TM, D), _row_spec(TM, DFF), _row_spec(TM, D), _const_spec((1, D)),
                  _const_spec((DFF // FF_CH, D, FF_CH)), _const_spec((DFF, D))],
        out_specs=[_row_spec(TM, DFF), _row_spec(TM, D), _const_spec((1, D))],
        out_shape=[sd((s, DFF), BF16), sd((s, D), F32), sd((1, D), F32)],
        compiler_params=_cparams("arbitrary"),
    )(dh2, rf, h1, g2, wff1, wff2)


def _mix_bwd(dh1, attn, u, z, lng, lnb, sgu_w, sgu_wt, bias_t, ga, gg, wout):
    s = dh1.shape[0]
    nsteps = s // TMX
    nd = len(DILS)

    def body(*refs):
        dh1_ref, attn_ref, u_ref, z_ref, lng_ref, lnb_ref, w_ref, wt_ref, bt_ref, ga_ref, gg_ref, wo_ref = refs[:12]
        do_refs, dl_refs = refs[12:12 + nd], refs[12 + nd:12 + 2 * nd]
        (du_ref, dz_ref, dga_ref, dgg_ref, dlng_ref, dlnb_ref, dws_ref, db_ref,
         dbt_acc, scr_do, scr_dl) = refs[12 + 2 * nd:]
        i = pl.program_id(0)

        @pl.when(i == 0)
        def _():
            for r in (dga_ref, dgg_ref, dlng_ref, dlnb_ref, dws_ref, db_ref, dbt_acc):
                r[...] = jnp.zeros_like(r)

        dmixed = _dot_nt(dh1_ref[...].astype(BF16), wo_ref[...])
        attn = attn_ref[...]
        _, an, ra = _rms_fwd(attn, ga_ref[...])
        dattn, dga = _rms_bwd(dmixed[:, :A], an, ra, ga_ref[...])
        dga_ref[...] += dga
        _fill_cols(scr_do, dattn)
        spread = _head_spread()
        delta = functools.reduce(lambda a, b: a + b, [_dot_nt(piece, spread) for piece in _bf16_pieces(dattn * attn, 3)])
        _fill_cols(scr_dl, delta)
        for di, dil in enumerate(DILS):
            if dil == 1:
                do_refs[di][0] = dattn.astype(BF16)
                dl_refs[di][0] = delta
            else:
                _split_residues(scr_do, do_refs[di], dil)
                _split_residues(scr_dl, dl_refs[di], dil)
        pmat = _group_mean_matrix()
        lng = lng_ref[...]
        uv, zv = u_ref[...], z_ref[...]
        gmv, ug, zhat, rstd, zn, mixed = _sgu_forward(uv, zv, lng, lnb_ref[...], w_ref, bt_ref[...], pmat, TMX)
        _, gmn, rg = _rms_fwd(gmv, gg_ref[...])
        dgm, dgg = _rms_bwd(dmixed[:, A:], gmn, rg, gg_ref[...])
        dgg_ref[...] += dgg
        du_ref[...] = (dgm * mixed * _gelu_grad(uv)).astype(BF16)
        dmx = dgm * ug
        dmxb = dmx.astype(BF16)
        gm = _group_masks(GW)
        tri_t = _tri_mask(False)
        wst = [jnp.where(tri_t, wt_ref[g], 0.0).astype(BF16) for g in range(NG)]
        zero = jnp.zeros((CHUNK, GW), BF16)
        dzn_pieces = []
        for c in range(TMX // CHUNK):
            rs = slice(c * CHUNK, (c + 1) * CHUNK)
            dmc = dmxb[rs, :]
            znc = zn[rs, :]
            dbt_acc[...] += dmx[rs, :]
            dzn = None
            for g in range(NG):
                dws_ref[g] += _dot_nt(jnp.where(gm[g], dmc, zero), znc)
                part = jnp.where(gm[g], _dot(wst[g], dmc), 0.0)
                dzn = part if dzn is None else dzn + part
            dzn_pieces.append(dzn)
        dzn = jnp.concatenate(dzn_pieces, axis=0)
        dlng_ref[...] += jnp.sum(dzn * zhat, axis=0, keepdims=True)
        dlnb_ref[...] += jnp.sum(dzn, axis=0, keepdims=True)
        dzh = dzn * lng
        dzg = rstd * (dzh - _dot_hi(dzh, pmat) - zhat * _dot_hi(dzh * zhat, pmat))
        dz_ref[...] = (dzg * _gelu_grad(zv)).astype(BF16)

        @pl.when(i == nsteps - 1)
        def _():
            tri = _tri_mask(True)
            for g in range(NG):
                dws_ref[g] = jnp.where(tri, dws_ref[g], 0.0)
            acc = dbt_acc[...]
            lane = lax.broadcasted_iota(jnp.int32, (CHUNK, LANES), 1)
            out = jnp.zeros((CHUNK, LANES), F32)
            for g in range(NG):
                sg = jnp.sum(jnp.where(gm[g], acc, 0.0), axis=-1, keepdims=True)
                out = jnp.where(lane == g, sg, out)
            db_ref[...] = out

    sd = jax.ShapeDtypeStruct
    res = pl.pallas_call(
        body, name="mix_bwd", grid=(nsteps,),
        in_specs=[_row_spec(TMX, D), _row_spec(TMX, A), _row_spec(TMX, GW), _row_spec(TMX, GW),
                  _const_spec((1, GW)), _const_spec((1, GW)), _const_spec((NG, CHUNK, CHUNK)),
                  _const_spec((NG, CHUNK, CHUNK)), _const_spec((CHUNK, GW)), _const_spec((1, A)),
                  _const_spec((1, GW)), _const_spec((D, D))],
        out_specs=[_res_spec(d, TMX, A) for d in DILS] + [_res_spec(d, TMX, LANES) for d in DILS]
                  + [_row_spec(TMX, GW), _row_spec(TMX, GW),
                   _const_spec((1, A)), _const_spec((1, GW)), _const_spec((1, GW)), _const_spec((1, GW)),
                   _const_spec((NG, CHUNK, CHUNK)), _const_spec((CHUNK, LANES))],
        out_shape=[_res_shape(s, d, A, BF16) for d in DILS] + [_res_shape(s, d, LANES, F32) for d in DILS]
                  + [sd((s, GW), BF16), sd((s, GW), BF16),
                   sd((1, A), F32), sd((1, GW), F32), sd((1, GW), F32), sd((1, GW), F32),
                   sd((NG, CHUNK, CHUNK), F32), sd((CHUNK, LANES), F32)],
        scratch_shapes=[pltpu.VMEM((CHUNK, GW), F32), _col_scratch(TMX, A), _col_scratch(TMX, LANES)],
        compiler_params=_cparams("arbitrary"),
    )(dh1, attn, u, z, lng, lnb, sgu_w, sgu_wt, bias_t, ga, gg, wout)
    return (res[:nd], res[nd:2 * nd]) + tuple(res[2 * nd:])


def _dproj_merge(dqs, dks, dvs, du, dz, pin):
    s = du.shape[0]
    nd = len(DILS)
    nscr = sum(1 for d in DILS if d > 1)

    def body(*refs):
        pin_ref = refs[0]
        parts = [refs[1 + t * nd:1 + (t + 1) * nd] for t in range(3)]
        du_ref, dz_ref, dp_ref = refs[1 + 3 * nd:4 + 3 * nd]
        scr = refs[4 + 3 * nd:]
        sums = []
        for t in range(3):
            total, j = None, 0
            for di, dil in enumerate(DILS):
                if dil == 1:
                    term = parts[t][di][0].astype(F32)
                else:
                    term = _merge_residues(parts[t][di], scr[t * nscr + j], dil)
                    j += 1
                total = term if total is None else total + term
            sums.append(total)
        dp_ref[...] = jnp.concatenate([sums[0] * SCALE, sums[1], sums[2], du_ref[...].astype(F32) + pin_ref[0, 0],
                                       dz_ref[...].astype(F32)], axis=-1).astype(BF16)

    return pl.pallas_call(
        body, name="dproj_merge", grid=(s // TMX,),
        in_specs=[pl.BlockSpec(memory_space=pltpu.SMEM)] + [_res_spec(d, TMX, A) for d in DILS] * 3
                 + [_row_spec(TMX, GW)] * 2,
        out_specs=_row_spec(TMX, INW), out_shape=jax.ShapeDtypeStruct((s, INW), BF16),
        scratch_shapes=[_col_scratch(TMX, A)] * (3 * nscr),
        compiler_params=_cparams("arbitrary"),
    )(pin, *dqs, *dks, *dvs, du, dz)


def _inproj_bwd(dproj, dh1, x, g1, win_t):
    s = x.shape[0]

    def body(dp_ref, dh1_ref, x_ref, g_ref, w_ref, dx_ref, dg_ref):
        i = pl.program_id(0)
        dhn = _dot(dp_ref[...], w_ref[...])
        _, xn, r1 = _rms_fwd(x_ref[...], g_ref[...])
        dres, dg = _rms_bwd(dhn, xn, r1, g_ref[...])
        dx_ref[...] = dh1_ref[...] + dres

        @pl.when(i == 0)
        def _():
            dg_ref[...] = jnp.zeros_like(dg_ref)

        dg_ref[...] += dg

    sd = jax.ShapeDtypeStruct
    return pl.pallas_call(
        body, name="inproj_bwd", grid=(s // TM_BIG,),
        in_specs=[_row_spec(TM_BIG, INW), _row_spec(TM_BIG, D), _row_spec(TM_BIG, D), _const_spec((1, D)), _const_spec((INW, D))],
        out_specs=[_row_spec(TM_BIG, D), _const_spec((1, D))],
        out_shape=[sd((s, D), F32), sd((1, D), F32)],
        compiler_params=_cparams("arbitrary"),
    )(dproj, dh1, x, g1, win_t)


def _wgrad(a, b, name, bm, bn, bk=4 * TM, square_a=False, also_bf16=False):
    s, m = a.shape
    n = b.shape[1]
    bm, bn = min(bm, m), min(bn, n)
    nk = s // bk

    def body(a_ref, b_ref, o_ref, *low):
        @pl.when(pl.program_id(2) == 0)
        def _():
            o_ref[...] = jnp.zeros_like(o_ref)

        av = a_ref[...]
        if square_a:
            av = av.astype(F32)
            av = av * av
        o_ref[...] += _dot_tn(av.astype(BF16), b_ref[...].astype(BF16))
        if also_bf16:
            @pl.when(pl.program_id(2) == nk - 1)
            def _():
                low[0][...] = o_ref[...].astype(BF16)

    out_spec = pl.BlockSpec((bm, bn), lambda i, j, k: (i, j))
    res = pl.pallas_call(
        body, name=name, grid=(m // bm, n // bn, nk),
        in_specs=[pl.BlockSpec((bk, bm), lambda i, j, k: (k, i)), pl.BlockSpec((bk, bn), lambda i, j, k: (k, j))],
        out_specs=[out_spec, out_spec] if also_bf16 else out_spec,
        out_shape=([jax.ShapeDtypeStruct((m, n), F32), jax.ShapeDtypeStruct((m, n), BF16)] if also_bf16
                   else jax.ShapeDtypeStruct((m, n), F32)),
        compiler_params=_cparams("arbitrary", "arbitrary", "arbitrary"),
    )(a, b)
    return res


def _adamw_math(w, g, m, v):
    m = B1 * m + (1.0 - B1) * g
    v = B2 * v + (1.0 - B2) * (g * g)
    m_hat = m / (1.0 - B1 ** STEP)
    v_hat = v / (1.0 - B2 ** STEP)
    delta = -LR * (m_hat / (jnp.sqrt(v_hat) + AEPS) + WD * w)
    return delta, m, v


def _adamw(w, g, m, v, name):
    rows, cols = w.shape
    br = min(rows, 256)
    while rows % br:
        br -= 8

    def body(w_ref, g_ref, m_ref, v_ref, d_ref, mo_ref, vo_ref):
        d, mn, vn = _adamw_math(w_ref[...], g_ref[...], m_ref[...], v_ref[...])
        d_ref[...] = d
        mo_ref[...] = mn
        vo_ref[...] = vn

    spec = _row_spec(br, cols)
    sd = jax.ShapeDtypeStruct((rows, cols), F32)
    return pl.pallas_call(
        body, name=name, grid=(rows // br,), in_specs=[spec] * 4, out_specs=[spec] * 3,
        out_shape=[sd, sd, sd], compiler_params=_cparams("arbitrary"),
    )(w, g, m, v)


def _local_step(x, hn1, target, small, win_t, rest_weights, early_grads=None, after_attention_bwd=None,
                late_grads=None):
    slopes = jnp.asarray(_alibi_slopes(NH) * np.float32(LOG2E))
    q, k, v, u, z = _inproj_fwd(hn1, win_t)
    outs, lses = [], []
    for i, dil in enumerate(DILS):
        o, l = _attn_fwd(q[i], k[i], v[i], slopes, dil)
        outs.append(o)
        lses.append(l)
    wout, wff1, wff2 = rest_weights(functools.reduce(lambda a, b: a + b, [l[0, 0:8, :] for l in lses]))
    attn, lse, mixed, h1 = _mix_fwd(outs, lses, u, z, x, small["ln_g"], small["ln_b"], small["sgu_w"],
                                    small["bias_t"], small["attn_out_g"], small["gmlp_out_g"], wout)
    hn2, rf, dh2, loss, dgf = _mlp_fwd(h1, small["norm2_g"], wff1, wff2, small["final_norm_g"], target)
    df, dh1, dg2 = _mlp_bwd(dh2, rf, h1, small["norm2_g"], wff1, wff2)
    gwff1 = _wgrad(hn2, df, "wgrad_ff1", D, 1024)
    gwff2 = _wgrad(rf, dh2, "wgrad_ff2", 1024, D, square_a=True)
    gwout = _wgrad(mixed, dh1, "wgrad_out", D, D)
    ga, g1 = small["attn_out_g"], small["norm1_g"]
    pin = early_grads(gwff1, gwff2, gwout) if early_grads else None
    if pin is not None:
        ga = ga + pin
    (do, delta, du, dz, dga, dgg, dlng, dlnb, dws, db) = _mix_bwd(
        dh1, attn, u, z, small["ln_g"], small["ln_b"], small["sgu_w"], small["sgu_wt"], small["bias_t"],
        ga, small["gmlp_out_g"], wout)
    dqs, dks, dvs = [], [], []
    for i, dil in enumerate(DILS):
        dqs.append(_attn_bwd_dq(q[i], k[i], v[i], do[i], lse[i], delta[i], slopes, dil))
        dk, dv = _attn_bwd_dkv(q[i], k[i], v[i], do[i], lse[i], delta[i], slopes, dil)
        dks.append(dk)
        dvs.append(dv)
    marker = functools.reduce(lambda a, b: a + b, [t[0, 0:8, 0:LANES] for t in dqs + dks + dvs])
    partial = dict(ln_g=dlng, ln_b=dlnb, sgu_w=dws, sgu_b=db[:, :NG].T, attn_out_g=dga, gmlp_out_g=dgg,
                   norm2_g=dg2, final_norm_g=dgf)
    pin = after_attention_bwd(marker, partial, loss[0, 0]) if after_attention_bwd else None
    dproj = _dproj_merge(dqs, dks, dvs, du, dz, jnp.zeros((1, 1), F32) if pin is None else pin)
    gwin_t, gwin_low = _wgrad(dproj, hn1, "wgrad_in", INW // 2, D, also_bf16=True)
    pin = late_grads(gwin_t, gwin_low) if late_grads else None
    if pin is not None:
        g1 = g1 + pin
    dx, dg1 = _inproj_bwd(dproj, dh1, x, g1, win_t)
    small_grads = dict(partial, norm1_g=dg1)
    return loss[0, 0], dx, small_grads, (gwin_t, gwout, gwff1, gwff2)


ANY = pl.BlockSpec(memory_space=pl.ANY)
NDEV = 8


def _position():
    return lax.axis_index("x"), lax.axis_index("y"), lax.axis_index("c")


def _other_chips(x, y):
    return [(1 - x, y), (x, 1 - y), (1 - x, 1 - y)]


def _remote(src, dst, send_sem, recv_sem, device):
    return pltpu.make_async_remote_copy(src_ref=src, dst_ref=dst, send_sem=send_sem, recv_sem=recv_sem,
                                        device_id=device, device_id_type=MESH)


HBM = pl.BlockSpec(memory_space=pltpu.HBM)
SEM = pl.BlockSpec(memory_space=pltpu.SEMAPHORE)
DATAFLOW = pltpu.SideEffectType.DATAFLOW_SIDE_EFFECTING


def _in_hbm(a):
    return pltpu.with_memory_space_constraint(a, pltpu.HBM)


def _gather_start(shards, name):
    n = len(shards)
    lands = [jnp.broadcast_to(sh[None], (NCHIP,) + sh.shape) for sh in shards]

    def body(*refs):
        w_refs, land_refs = refs[:n], refs[n:2 * n]
        send_sems, recv_sems = refs[2 * n:2 * n + 2]
        token = refs[-1]
        x, y, c = _position()
        for w in range(n):
            for k, (px, py) in enumerate(_other_chips(x, y)):
                m = 3 * w + k
                _remote(w_refs[w], land_refs[w].at[2 * x + y], send_sems.at[m], recv_sems.at[m], (px, py, c)).start()
        token[...] = jnp.zeros_like(token)

    res = _split_call(body, name, list(shards) + lands, (3 * n, 3 * n), (TOKEN,))
    return res[0], res[1], res[2:2 + n], res[2 + n:2 + 2 * n], res[-1]


def _gather_wait(send_sems, recv_sems, shards, lands, after, name):
    n = len(shards)

    def body(*refs):
        w_refs, land_refs = refs[:n], refs[n:2 * n]
        send_sems, recv_sems = refs[2 * n:2 * n + 2]
        x, y, c = _position()
        for w in range(n):
            for k, (px, py) in enumerate(_other_chips(x, y)):
                m = 3 * w + k
                cp = _remote(w_refs[w], land_refs[w].at[2 * px + py], send_sems.at[m], recv_sems.at[m], (px, py, c))
                cp.wait_send()
                cp.wait_recv()

    operands = list(shards) + list(lands)
    res = pl.pallas_call(
        body, name=name, out_shape=tuple(pltpu.HBM(a.shape, a.dtype) for a in operands),
        in_specs=(HBM,) * (2 * n) + (SEM, SEM, ANY), out_specs=(HBM,) * (2 * n),
        input_output_aliases={i: i for i in range(2 * n)},
        compiler_params=pltpu.CompilerParams(has_side_effects=DATAFLOW),
    )(*operands, send_sems, recv_sems, after)
    return res[n:]


def _xor_peers(x, y, c):
    peers = []
    for k in range(1, NDEV):
        kx, ky, kc = (k >> 2) & 1, (k >> 1) & 1, k & 1
        peers.append((1 - x if kx else x, 1 - y if ky else y, 1 - c if kc else c))
    return peers


def _piece(part_ref, px, py, pc):
    slab = 2 * px + py
    if len(part_ref.shape) == 3:
        half = part_ref.shape[1] // 2
        return part_ref.at[slab, pl.ds(pc * half, half), :]
    half = part_ref.shape[0] // 2
    return part_ref.at[pl.ds(pc * half, half), pl.ds(pl.multiple_of(slab * D, D), D)]


def _split_call(body, name, operands, n_sems, extra_out=()):
    n = len(operands)
    sems = tuple(pltpu.SemaphoreType.DMA((m,)) for m in n_sems)
    thru = tuple(pltpu.HBM(a.shape, a.dtype) for a in operands)
    return pl.pallas_call(
        body, name=name, out_shape=sems + thru + tuple(extra_out),
        in_specs=(HBM,) * n,
        out_specs=(SEM,) * len(sems) + (HBM,) * n + (pl.BlockSpec(memory_space=pltpu.VMEM),) * len(extra_out),
        input_output_aliases={i: len(sems) + i for i in range(n)},
        compiler_params=pltpu.CompilerParams(has_side_effects=DATAFLOW),
    )(*[_in_hbm(a) for a in operands])


TOKEN = jax.ShapeDtypeStruct((8, LANES), F32)


def _pack_copies(pack_ref, land_ref, send_sems, recv_sems, base, position, start):
    x, y, c = position
    for k, (px, py, pc) in enumerate(_xor_peers(x, y, c)):
        if start:
            _remote(pack_ref, land_ref.at[4 * x + 2 * y + c], send_sems.at[base + k], recv_sems.at[base + k],
                    (px, py, pc)).start()
        else:
            cp = _remote(pack_ref, land_ref.at[4 * px + 2 * py + pc], send_sems.at[base + k], recv_sems.at[base + k],
                         (px, py, pc))
            cp.wait_send()
            cp.wait_recv()


def _pack_landing(pack):
    return jnp.broadcast_to(pack[None], (NDEV,) + pack.shape)


def _reduce_start(parts, name, pack=None):
    nw = len(parts)
    lands = [lax.empty((NDEV - 1, p.shape[-2] // 2, D), p.dtype) for p in parts]
    operands = list(parts) + lands + ([pack, _pack_landing(pack)] if pack is not None else [])
    nops = len(operands)

    def body(*refs):
        part_refs, land_refs = refs[:nw], refs[nw:2 * nw]
        send_sems, recv_sems = refs[nops:nops + 2]
        token = refs[-1]
        x, y, c = _position()
        for w in range(nw):
            for k, peer in enumerate(_xor_peers(x, y, c)):
                n = w * (NDEV - 1) + k
                _remote(_piece(part_refs[w], *peer), land_refs[w].at[k], send_sems.at[n], recv_sems.at[n],
                        peer).start()
        if pack is not None:
            _pack_copies(refs[2 * nw], refs[2 * nw + 1], send_sems, recv_sems, nw * (NDEV - 1), (x, y, c), True)
        token[...] = jnp.zeros_like(token)

    n = (nw + (pack is not None)) * (NDEV - 1)
    res = _split_call(body, name, operands, (n, n), (TOKEN,))
    return res[0], res[1], res[2:2 + nops], res[-1]


def _reduce_wait(send_sems, recv_sems, operands, nw, after, name):
    nops = len(operands)
    has_pack = nops > 2 * nw

    def body(*refs):
        part_refs, land_refs = refs[:nw], refs[nw:2 * nw]
        send_sems, recv_sems = refs[nops:nops + 2]
        x, y, c = _position()
        for w in range(nw):
            for k, peer in enumerate(_xor_peers(x, y, c)):
                n = w * (NDEV - 1) + k
                cp = _remote(_piece(part_refs[w], *peer), land_refs[w].at[k], send_sems.at[n], recv_sems.at[n], peer)
                cp.wait_send()
                cp.wait_recv()
        if has_pack:
            _pack_copies(refs[2 * nw], refs[2 * nw + 1], send_sems, recv_sems, nw * (NDEV - 1), (x, y, c), False)

    res = pl.pallas_call(
        body, name=name, out_shape=tuple(pltpu.HBM(a.shape, a.dtype) for a in operands),
        in_specs=(HBM,) * nops + (SEM, SEM, ANY), out_specs=(HBM,) * nops,
        input_output_aliases={i: i for i in range(nops)},
        compiler_params=pltpu.CompilerParams(has_side_effects=DATAFLOW),
    )(*operands, send_sems, recv_sems, after)
    return res[:nw], res[nw:2 * nw], (res[2 * nw + 1] if has_pack else None)


def _sum_pieces(part, land, sel, name):
    half = part.shape[-2] // 2
    br = 128 if half % 128 == 0 else half // 2
    nb = half // br

    def body(sel_ref, own_ref, *refs):
        acc = own_ref[...]
        for r in refs[:NDEV - 1]:
            acc = acc + r[...].astype(F32)
        refs[NDEV - 1][...] = acc

    if part.ndim == 3:
        own_spec = pl.BlockSpec((None, br, D), lambda i, sel_ref: (sel_ref[0], sel_ref[1] * nb + i, 0))
    else:
        own_spec = pl.BlockSpec((br, D), lambda i, sel_ref: (sel_ref[1] * nb + i, sel_ref[0]))
    slot_specs = [pl.BlockSpec((None, br, D), functools.partial(lambda i, sel_ref, k: (k, i, 0), k=k))
                  for k in range(NDEV - 1)]
    return pl.pallas_call(
        body, name=name,
        grid_spec=pltpu.PrefetchScalarGridSpec(
            num_scalar_prefetch=1, grid=(nb,), in_specs=[own_spec] + slot_specs,
            out_specs=pl.BlockSpec((br, D), lambda i, sel_ref: (i, 0))),
        out_shape=jax.ShapeDtypeStruct((half, D), F32),
        compiler_params=_cparams("arbitrary"),
    )(sel, part, *([land] * (NDEV - 1)))


def _share_start(halves, name, pack=None):
    nw = len(halves)
    lands = [lax.empty(h.shape, F32) for h in halves]
    operands = list(halves) + lands + ([pack, _pack_landing(pack)] if pack is not None else [])
    nops = len(operands)

    def body(*refs):
        h_refs, land_refs = refs[:nw], refs[nw:2 * nw]
        send_sems, recv_sems = refs[nops:nops + 2]
        token = refs[-1]
        x, y, c = _position()
        for w in range(nw):
            _remote(h_refs[w], land_refs[w], send_sems.at[w], recv_sems.at[w], (x, y, 1 - c)).start()
        if pack is not None:
            _pack_copies(refs[2 * nw], refs[2 * nw + 1], send_sems, recv_sems, nw, (x, y, c), True)
        token[...] = jnp.zeros_like(token)

    n = nw + (NDEV - 1 if pack is not None else 0)
    res = _split_call(body, name, operands, (n, n), (TOKEN,))
    return res[0], res[1], res[2:2 + nops], res[-1]


def _share_wait(send_sems, recv_sems, operands, nw, after, name):
    nops = len(operands)
    has_pack = nops > 2 * nw

    def body(*refs):
        h_refs, land_refs = refs[:nw], refs[nw:2 * nw]
        send_sems, recv_sems = refs[nops:nops + 2]
        x, y, c = _position()
        for w in range(nw):
            cp = _remote(h_refs[w], land_refs[w], send_sems.at[w], recv_sems.at[w], (x, y, 1 - c))
            cp.wait_send()
            cp.wait_recv()
        if has_pack:
            _pack_copies(refs[2 * nw], refs[2 * nw + 1], send_sems, recv_sems, nw, (x, y, c), False)

    res = pl.pallas_call(
        body, name=name, out_shape=tuple(pltpu.HBM(a.shape, a.dtype) for a in operands),
        in_specs=(HBM,) * nops + (SEM, SEM, ANY), out_specs=(HBM,) * nops,
        input_output_aliases={i: i for i in range(nops)},
        compiler_params=pltpu.CompilerParams(has_side_effects=DATAFLOW),
    )(*operands, send_sems, recv_sems, after)
    return res[:nw], res[nw:2 * nw], (res[2 * nw + 1] if has_pack else None)


def _join_halves(own, other, c):
    first = jnp.where(c == 0, own, other)
    second = jnp.where(c == 0, other, own)
    return jnp.concatenate([first, second], axis=0)


SMALL_SIZES = (("norm1_g", D), ("sgu_ln_g", GW), ("sgu_ln_b", GW), ("sgu_w", NG * CHUNK * CHUNK),
               ("sgu_b", NG * CHUNK), ("attn_out_g", A), ("gmlp_out_g", GW), ("norm2_g", D),
               ("final_norm_g", D))
PARAM_ROWS = sum(n for _, n in SMALL_SIZES) // LANES
SMALL_ROWS = PARAM_ROWS + 8


def _pack_small(tree, first_extra=None):
    extra = jnp.zeros((8 * LANES,), F32)
    if first_extra is not None:
        extra = extra.at[0].set(first_extra)
    flat = jnp.concatenate([tree[n].reshape(-1) for n, _ in SMALL_SIZES] + [extra])
    return flat.reshape(SMALL_ROWS, LANES)


def _unpack_small(pack, shapes):
    flat = pack.reshape(-1)
    out, off = {}, 0
    for n, size in SMALL_SIZES:
        out[n] = flat[off:off + size].reshape(shapes[n])
        off += size
    return out


def _small_finish(pack_land, norm_land, wpack, mpack, vpack):
    def body(p_ref, n_ref, w_ref, m_ref, v_ref, go_ref, d_ref, mo_ref, vo_ref):
        total = p_ref[0]
        late = n_ref[0]
        for k in range(1, NDEV):
            total = total + p_ref[k]
            late = late + n_ref[k]
        go_ref[...] = total
        go_ref[0:8, :] = total[0:8, :] + late
        d, mn, vn = _adamw_math(w_ref[...], go_ref[...], m_ref[...], v_ref[...])
        d_ref[...] = d
        mo_ref[...] = mn
        vo_ref[...] = vn

    sd = jax.ShapeDtypeStruct((SMALL_ROWS, LANES), F32)
    vm = pl.BlockSpec(memory_space=pltpu.VMEM)
    return pl.pallas_call(
        body, name="small_finish", in_specs=[vm] * 5, out_specs=[vm] * 4, out_shape=[sd] * 4,
        compiler_params=_cparams(),
    )(pack_land, norm_land, wpack, mpack, vpack)


def kernel(x, norm1_g, w_in, sgu_ln_g, sgu_ln_b, sgu_w, sgu_b, attn_out_g, gmlp_out_g, w_out, norm2_g, w_ff1, w_ff2, final_norm_g, loss_target, m_norm1_g, m_w_in, m_sgu_ln_g, m_sgu_ln_b, m_sgu_w, m_sgu_b, m_attn_out_g, m_gmlp_out_g, m_w_out, m_norm2_g, m_w_ff1, m_w_ff2, m_final_norm_g, v_norm1_g, v_w_in, v_sgu_ln_g, v_sgu_ln_b, v_sgu_w, v_sgu_b, v_attn_out_g, v_gmlp_out_g, v_w_out, v_norm2_g, v_w_ff1, v_w_ff2, v_final_norm_g):
    names = [n for n, _ in SMALL_SIZES]
    w_small = dict(norm1_g=norm1_g, sgu_ln_g=sgu_ln_g, sgu_ln_b=sgu_ln_b, sgu_w=sgu_w, sgu_b=sgu_b,
                   attn_out_g=attn_out_g, gmlp_out_g=gmlp_out_g, norm2_g=norm2_g, final_norm_g=final_norm_g)
    m_small = dict(norm1_g=m_norm1_g, sgu_ln_g=m_sgu_ln_g, sgu_ln_b=m_sgu_ln_b, sgu_w=m_sgu_w, sgu_b=m_sgu_b,
                   attn_out_g=m_attn_out_g, gmlp_out_g=m_gmlp_out_g, norm2_g=m_norm2_g,
                   final_norm_g=m_final_norm_g)
    v_small = dict(norm1_g=v_norm1_g, sgu_ln_g=v_sgu_ln_g, sgu_ln_b=v_sgu_ln_b, sgu_w=v_sgu_w, sgu_b=v_sgu_b,
                   attn_out_g=v_attn_out_g, gmlp_out_g=v_gmlp_out_g, norm2_g=v_norm2_g,
                   final_norm_g=v_final_norm_g)
    shapes = {n: w_small[n].shape for n in names}

    start_in = _gather_start([w_in[0].T.astype(BF16)], "gather_in_start")
    issued = start_in[4][0:1, 0:1]
    start_rest = _gather_start([(w_out[0] + issued).astype(BF16), w_ff1[0].astype(BF16), w_ff2[0].astype(BF16)],
                               "gather_rest_start")
    hn1 = _norm1(x[0], norm1_g + start_rest[4][0:1, 0:1])
    win_t = _gather_wait(*start_in[:4], after=hn1, name="gather_in_wait")[0].reshape(INW, D)

    def rest_weights(after):
        wout, wff1, wff2 = _gather_wait(*start_rest[:4], after=after, name="gather_rest_wait")
        return wout.reshape(D, D), wff1, wff2.reshape(DFF, D)

    small = dict(
        norm1_g=norm1_g, ln_g=sgu_ln_g.reshape(1, GW), ln_b=sgu_ln_b.reshape(1, GW), sgu_w=sgu_w[0],
        sgu_wt=jnp.swapaxes(sgu_w[0], 1, 2), bias_t=jnp.repeat(sgu_b[0].T, DH, axis=1),
        attn_out_g=attn_out_g, gmlp_out_g=gmlp_out_g, norm2_g=norm2_g, final_norm_g=final_norm_g.reshape(1, D))
    xi, yi, ci = _position()
    sel = jnp.stack([2 * xi + yi, ci]).astype(jnp.int32)
    state = {}

    def as_slabs(g):
        return g.reshape(NCHIP, g.shape[0] // NCHIP, D)

    def early_grads(gwff1, gwff2, gwout):
        state["early"] = _reduce_start([gwff1, as_slabs(gwff2), as_slabs(gwout)], "reduce_early_start")
        return state["early"][3][0:1, 0:1]

    def after_attention_bwd(marker, partial, loss_part):
        send_sems, recv_sems, operands, _ = state["early"]
        parts, lands, _ = _reduce_wait(send_sems, recv_sems, operands, 3, marker, "reduce_early_wait")
        halves = [_sum_pieces(p, l, sel, "sum_" + n) for p, l, n in zip(parts, lands, ("w_ff1", "w_ff2", "w_out"))]
        pack = _pack_small(dict(partial, norm1_g=jnp.zeros((1, D), F32), sgu_ln_g=partial["ln_g"],
                                sgu_ln_b=partial["ln_b"]), loss_part)
        state["early_share"] = _share_start(halves, "share_early_start", pack)
        return state["early_share"][3][0:1, 0:1]

    def late_grads(gwin_t, gwin_low):
        state["late"] = _reduce_start([as_slabs(gwin_low)], "reduce_late_start")
        state["late_own"] = as_slabs(gwin_t)
        return state["late"][3][0:1, 0:1]

    _, dx, sg, _ = _local_step(
        x[0], hn1, loss_target[0], small, win_t, rest_weights, early_grads, after_attention_bwd, late_grads)
    send_sems, recv_sems, operands, _ = state["early_share"]
    own, other, pack_land = _share_wait(send_sems, recv_sems, operands, 3, dx, "share_early_wait")
    send_sems, recv_sems, operands, _ = state["late"]
    _, late_lands, _ = _reduce_wait(send_sems, recv_sems, operands, 1, dx, "reduce_late_wait")
    late_share = _share_start([_sum_pieces(state["late_own"], late_lands[0], sel, "sum_w_in")], "share_late_start",
                              sg["norm1_g"].reshape(8, LANES))
    issued = late_share[3][0:1, 0:1]
    g_big = {n: _join_halves(o, t, ci) + issued for n, o, t in zip(("w_ff1", "w_ff2", "w_out"), own, other)}
    w_big = dict(w_in=(w_in, m_w_in, v_w_in), w_out=(w_out, m_w_out, v_w_out),
                 w_ff1=(w_ff1, m_w_ff1, v_w_ff1), w_ff2=(w_ff2, m_w_ff2, v_w_ff2))
    grads, deltas, new_m, new_v = {}, {}, {}, {}

    def update(n):
        w, m, v = w_big[n]
        d, mn, vn = _adamw(w[0], g_big[n], m[0], v[0], "adamw_" + n)
        grads[n], deltas[n], new_m[n], new_v[n] = g_big[n][None], d[None], mn[None], vn[None]

    for n in ("w_ff1", "w_ff2", "w_out"):
        update(n)
    updated = deltas["w_out"][0, 0:8, 0:LANES] + deltas["w_ff1"][0, 0:8, 0:LANES] + deltas["w_ff2"][0, 0:8, 0:LANES]
    own, other, norm_land = _share_wait(late_share[0], late_share[1], late_share[2], 1, updated, "share_late_wait")
    g_big["w_in"] = _join_halves(own[0], other[0], ci).T
    update("w_in")

    packs = _small_finish(pack_land, norm_land, _pack_small(w_small), _pack_small(m_small), _pack_small(v_small))
    loss = packs[0][PARAM_ROWS, 0]
    for tree, pack in zip((grads, deltas, new_m, new_v), packs):
        tree.update(_unpack_small(pack, shapes))

    order = ["norm1_g", "w_in", "sgu_ln_g", "sgu_ln_b", "sgu_w", "sgu_b", "attn_out_g", "gmlp_out_g", "w_out",
             "norm2_g", "w_ff1", "w_ff2", "final_norm_g"]
    return (loss, dx[None], *[grads[n] for n in order], *[deltas[n] for n in order],
            *[new_m[n] for n in order], *[new_v[n] for n in order])
```

```python
import functools
import math

import numpy as np
import jax
import jax.numpy as jnp
from jax import lax
from jax.experimental import pallas as pl
from jax.experimental.pallas import tpu as pltpu

F32 = jnp.float32
BF16 = jnp.bfloat16

D = 1024
NH = 12
DH = 64
A = NH * DH
NG = 4
GW = NG * DH
INW = 3 * A + 2 * GW
DFF = 4 * D
CHUNK = 128
PATTERNS = ((128, 1), (512, 4), (2048, 16))
EPS = 1e-6
SCALE = DH ** -0.5
LOG2E = 1.0 / math.log(2.0)
LN2 = math.log(2.0)
NEG = -1e30

LR, B1, B2, AEPS, WD, STEP = 0.001, 0.9, 0.999, 1e-08, 0.01, 10

TM = 512
TM_BIG = 1024
TMX = 512
ATT_ROWS = 4096
FF_CH = 1024
LANES = 128
NCHIP = 4
VMEM_LIMIT = 56 * 1024 * 1024
MESH = pl.DeviceIdType.MESH


def _cparams(*sem, **kw):
    return pltpu.CompilerParams(dimension_semantics=sem if sem else None,
                                vmem_limit_bytes=VMEM_LIMIT, **kw)


def _dot(a, b):
    return jnp.dot(a, b, preferred_element_type=F32)


def _dot_nt(a, b):
    return lax.dot_general(a, b, (((1,), (1,)), ((), ())), preferred_element_type=F32)


def _dot_tn(a, b):
    return lax.dot_general(a, b, (((0,), (0,)), ((), ())), preferred_element_type=F32)


def _dot_hi(a, b):
    bb = b.astype(BF16)
    return functools.reduce(lambda x, y: x + y, [_dot(piece, bb) for piece in _bf16_pieces(a, 3)])


def _alibi_slopes(n):
    def pow2(m):
        start = 2.0 ** (-8.0 / m)
        return [start ** (i + 1) for i in range(m)]
    if math.log2(n).is_integer():
        s = pow2(n)
    else:
        c = 2 ** int(math.floor(math.log2(n)))
        s = pow2(c) + pow2(2 * c)[0::2][: n - c]
    return np.asarray(s, dtype=np.float32)


def _rms_fwd(v, g):
    r = lax.rsqrt(jnp.mean(v * v, axis=-1, keepdims=True) + EPS)
    vn = v * r
    return vn * g, vn, r


def _rms_bwd(dy, vn, r, g):
    w = dy * g
    dv = r * (w - vn * jnp.mean(w * vn, axis=-1, keepdims=True))
    return dv, jnp.sum(dy * vn, axis=0, keepdims=True)


_K0 = math.sqrt(2.0 / math.pi)
_K1 = 0.044715


def _gelu(v):
    return 0.5 * v * (1.0 + jnp.tanh(_K0 * (v + _K1 * (v * v * v))))


def _gelu_grad(v):
    t = jnp.tanh(_K0 * (v + _K1 * (v * v * v)))
    return 0.5 * (1.0 + t) + 0.5 * v * (1.0 - t * t) * (_K0 * (1.0 + 3.0 * _K1 * v * v))


def _row_spec(rows, cols):
    return pl.BlockSpec((rows, cols), lambda i: (i, 0))


def _const_spec(shape):
    nd = len(shape)
    return pl.BlockSpec(shape, lambda i: (0,) * nd, pipeline_mode=pl.Buffered(1))


DILS = tuple(d for _, d in PATTERNS)


def _fill_cols(scr, value):
    for cb in range(value.shape[1] // LANES):
        scr[cb] = value[:, cb * LANES:(cb + 1) * LANES]


def _split_residues(scr, out_ref, dil):
    nb, rows, _ = scr.shape
    for r in range(dil):
        for cb in range(nb):
            piece = scr.at[cb][pl.ds(r, rows // dil, stride=dil), :]
            out_ref[r, :, cb * LANES:(cb + 1) * LANES] = piece.astype(out_ref.dtype)


def _merge_residues(in_ref, scr, dil):
    nb, rows, _ = scr.shape
    for r in range(dil):
        for cb in range(nb):
            scr.at[cb][pl.ds(r, rows // dil, stride=dil), :] = in_ref[r, :, cb * LANES:(cb + 1) * LANES].astype(F32)
    return jnp.concatenate([scr[cb] for cb in range(nb)], axis=-1)


def _col_scratch(rows, width):
    return pltpu.VMEM((width // LANES, rows, LANES), F32)


def _res_spec(dil, rows, width):
    return pl.BlockSpec((dil, rows // dil, width), lambda i: (0, i, 0))


def _res_shape(s, dil, width, dtype):
    return jax.ShapeDtypeStruct((dil, s // dil, width), dtype)


def _norm1(x, g1):
    s = x.shape[0]

    def body(x_ref, g_ref, hn_ref):
        hn, _, _ = _rms_fwd(x_ref[...], g_ref[...])
        hn_ref[...] = hn.astype(BF16)

    return pl.pallas_call(
        body, name="norm1", grid=(s // TM,), in_specs=[_row_spec(TM, D), _const_spec((1, D))],
        out_specs=_row_spec(TM, D), out_shape=jax.ShapeDtypeStruct((s, D), BF16),
        compiler_params=_cparams("arbitrary"),
    )(x, g1)


def _inproj_fwd(hn1, win_t):
    s = hn1.shape[0]
    nd = len(DILS)

    def body(hn_ref, w_ref, *rest):
        qkv_refs = rest[:3 * nd]
        u_ref, z_ref, scr = rest[3 * nd:]
        hn = hn_ref[...]
        for t in range(3):
            seg = _dot_nt(hn, w_ref[t * A:(t + 1) * A, :])
            seg = seg * (SCALE * LOG2E) if t == 0 else seg
            _fill_cols(scr, seg)
            for di, dil in enumerate(DILS):
                if dil == 1:
                    qkv_refs[t * nd + di][0] = seg.astype(BF16)
                else:
                    _split_residues(scr, qkv_refs[t * nd + di], dil)
        u_ref[...] = _dot_nt(hn, w_ref[3 * A:3 * A + GW, :])
        z_ref[...] = _dot_nt(hn, w_ref[3 * A + GW:INW, :])

    res = pl.pallas_call(
        body, name="inproj_fwd", grid=(s // TM_BIG,),
        in_specs=[_row_spec(TM_BIG, D), _const_spec((INW, D))],
        out_specs=[_res_spec(d, TM_BIG, A) for _ in range(3) for d in DILS]
                  + [_row_spec(TM_BIG, GW), _row_spec(TM_BIG, GW)],
        out_shape=[_res_shape(s, d, A, BF16) for _ in range(3) for d in DILS]
                  + [jax.ShapeDtypeStruct((s, GW), F32)] * 2,
        scratch_shapes=[_col_scratch(TM_BIG, A)],
        compiler_params=_cparams("arbitrary"),
    )(hn1, win_t)
    q, k, v = (res[t * nd:(t + 1) * nd] for t in range(3))
    return q, k, v, res[-2], res[-1]


def _att_geometry(length, dil):
    merge = max(1, min(dil, ATT_ROWS // length))
    rows = min(length * merge, ATT_ROWS)
    nsub = rows // CHUNK
    return merge, rows, length * merge // rows, nsub, min(length // CHUNK, nsub)


def _merged(t, merge):
    return t.reshape(t.shape[0] // merge, t.shape[1] * merge, t.shape[2])


def _stack_heads(t):
    lane = lax.broadcasted_iota(jnp.int32, t.shape, 1)
    zero = jnp.zeros_like(t)
    return jnp.concatenate([jnp.where(lane < DH, t, zero), jnp.where(lane >= DH, t, zero)], axis=0)


def _head_cols(t, hp):
    lane = lax.broadcasted_iota(jnp.int32, t.shape, 1)
    cols = [jnp.sum(jnp.where(lane == 2 * hp + h, t, 0.0), axis=-1, keepdims=True) for h in range(2)]
    return jnp.concatenate(cols, axis=0)


def _unstack_heads(t2):
    n = t2.shape[0] // 2
    lane = lax.broadcasted_iota(jnp.int32, (n, LANES), 1)
    return jnp.where(lane < DH, t2[:n], t2[n:])


def _query_window_bias(s0, s1, dil, first):
    row = lax.broadcasted_iota(jnp.int32, (2 * CHUNK, 2 * CHUNK), 0)
    col = lax.broadcasted_iota(jnp.int32, (2 * CHUNK, 2 * CHUNK), 1)
    steps = (row & (CHUNK - 1)) + CHUNK - col
    valid = (steps >= 0) & (steps <= CHUNK)
    if first:
        valid = valid & (col >= CHUNK)
    slope = jnp.where(row < CHUNK, s0, s1)
    return jnp.where(valid, -slope * (steps * dil).astype(F32), NEG)


def _key_block_bias(s0, s1, dil, last):
    key = lax.broadcasted_iota(jnp.int32, (CHUNK, 4 * CHUNK), 0)
    col = lax.broadcasted_iota(jnp.int32, (CHUNK, 4 * CHUNK), 1)
    wq = col & (2 * CHUNK - 1)
    steps = wq - key
    valid = (steps >= 0) & (steps <= CHUNK)
    if last:
        valid = valid & (wq < CHUNK)
    slope = jnp.where(col < 2 * CHUNK, s0, s1)
    return jnp.where(valid, -slope * (steps * dil).astype(F32), NEG)


def _head_rows(t, hp):
    row = lax.broadcasted_iota(jnp.int32, (8, LANES), 0)
    lane = lax.broadcasted_iota(jnp.int32, (8, LANES), 1)
    pick = jnp.where((row < 2) & (lane == 2 * hp + row), 1.0, 0.0).astype(BF16)
    hi = t.astype(BF16)
    rest = t - hi.astype(F32)
    mid = rest.astype(BF16)
    low = (rest - mid.astype(F32)).astype(BF16)
    return _dot_nt(pick, hi) + _dot_nt(pick, mid) + _dot_nt(pick, low)


def _att_specs(dil, rows, nsub, nblk):
    main = pl.BlockSpec((None, rows, LANES), lambda r, c, hp: (r, c, hp))
    prev = pl.BlockSpec((None, CHUNK, LANES), lambda r, c, hp: (r, jnp.maximum(c * nsub - 1, 0), hp))
    nxt = pl.BlockSpec((None, CHUNK, LANES), lambda r, c, hp: (r, jnp.minimum((c + 1) * nsub, nblk - 1), hp))
    main_heads = pl.BlockSpec((None, rows, LANES), lambda r, c, hp: (r, c, 0))
    nxt_heads = pl.BlockSpec((None, CHUNK, LANES), lambda r, c, hp: (r, jnp.minimum((c + 1) * nsub, nblk - 1), 0))
    return main, prev, nxt, main_heads, nxt_heads


def _row_start(i):
    return i * CHUNK if isinstance(i, int) else pl.multiple_of(i * CHUNK, CHUNK)


def _first_blocks(block, nsub, seg, nch, ch, first_bias, bias_buf):
    for i in range(nsub):
        if i % seg:
            block(i, bias_buf[...])
        elif nch == 1:
            block(i, first_bias())
        else:
            block(i, jnp.where(ch == 0, first_bias(), bias_buf[...]))


def _last_blocks(block, nsub, seg, nch, ch, last_bias, bias_buf):
    for i in range(nsub):
        if (i + 1) % seg:
            block(i, bias_buf[...])
        elif nch == 1:
            block(i, last_bias())
        else:
            block(i, jnp.where(ch == nch - 1, last_bias(), bias_buf[...]))


def _attn_fwd(q, k, v, slopes, dil):
    length = q.shape[1]
    merge, rows, nch, nsub, seg = _att_geometry(length, dil)
    main, prev, _, main_heads, _ = _att_specs(dil, rows, nsub, length * merge // CHUNK)
    q, k, v = (_merged(t, merge) for t in (q, k, v))

    def body(sl_ref, q_ref, k_ref, v_ref, kh_ref, vh_ref, o_ref, lse_ref, kbuf, vbuf, bias_buf):
        ch = pl.program_id(1)
        hp = pl.program_id(2)
        lane = lax.broadcasted_iota(jnp.int32, (CHUNK, LANES), 1)
        kbuf[0:CHUNK, :] = kh_ref[...]
        kbuf[CHUNK:, :] = k_ref[...]
        vbuf[0:CHUNK, :] = vh_ref[...]
        vbuf[CHUNK:, :] = v_ref[...]
        s0, s1 = sl_ref[2 * hp], sl_ref[2 * hp + 1]

        def block(i, bias):
            row = _row_start(i)
            rs = pl.ds(row, CHUNK)
            q2 = _stack_heads(q_ref[rs, :])
            kw = kbuf[pl.ds(row, 2 * CHUNK), :]
            vw = vbuf[pl.ds(row, 2 * CHUNK), :]
            sc = _dot_nt(q2, kw) + bias
            m = jnp.max(sc, axis=-1, keepdims=True)
            p = jnp.exp2(sc - m)
            l = jnp.sum(p, axis=-1, keepdims=True)
            o2 = _dot(p.astype(BF16), vw) * (1.0 / l)
            o_ref[rs, :] = _unstack_heads(o2).astype(BF16)
            lse = m + jnp.log2(l)
            seen = jnp.where(hp == 0, 0.0, lse_ref[rs, :])
            lse_ref[rs, :] = jnp.where(lane == 2 * hp, lse[:CHUNK], jnp.where(lane == 2 * hp + 1, lse[CHUNK:], seen))

        bias_buf[...] = _query_window_bias(s0, s1, dil, False)
        _first_blocks(block, nsub, seg, nch, ch, lambda: _query_window_bias(s0, s1, dil, True), bias_buf)

    sd = jax.ShapeDtypeStruct
    o, lse = pl.pallas_call(
        body, name=f"attn_fwd_d{dil}", grid=(dil // merge, nch, NH // 2),
        in_specs=[pl.BlockSpec(memory_space=pltpu.SMEM), main, main, main, prev, prev],
        out_specs=[main, main_heads],
        out_shape=[sd((dil // merge, length * merge, A), BF16), sd((dil // merge, length * merge, LANES), F32)],
        scratch_shapes=[pltpu.VMEM((rows + CHUNK, LANES), BF16), pltpu.VMEM((rows + CHUNK, LANES), BF16),
                        pltpu.VMEM((2 * CHUNK, 2 * CHUNK), F32)],
        compiler_params=_cparams("arbitrary", "arbitrary", "arbitrary"),
    )(slopes, q, k, v, k, v)
    return o.reshape(dil, length, A), lse.reshape(dil, length, LANES)


def _attn_bwd_dq(q, k, v, do, lse, delta, slopes, dil):
    length = q.shape[1]
    merge, rows, nch, nsub, seg = _att_geometry(length, dil)
    main, prev, _, main_heads, _ = _att_specs(dil, rows, nsub, length * merge // CHUNK)
    q, k, v, do, lse, delta = (_merged(t, merge) for t in (q, k, v, do, lse, delta))

    def body(sl_ref, q_ref, k_ref, v_ref, do_ref, lse_ref, dl_ref, kh_ref, vh_ref, dq_ref, kbuf, vbuf, bias_buf):
        ch = pl.program_id(1)
        hp = pl.program_id(2)
        kbuf[0:CHUNK, :] = kh_ref[...]
        kbuf[CHUNK:, :] = k_ref[...]
        vbuf[0:CHUNK, :] = vh_ref[...]
        vbuf[CHUNK:, :] = v_ref[...]
        s0, s1 = sl_ref[2 * hp], sl_ref[2 * hp + 1]

        def block(i, bias):
            row = _row_start(i)
            rs = pl.ds(row, CHUNK)
            q2 = _stack_heads(q_ref[rs, :])
            do2 = _stack_heads(do_ref[rs, :])
            lse2 = _head_cols(lse_ref[rs, :], hp)
            dl2 = _head_cols(dl_ref[rs, :], hp)
            kw = kbuf[pl.ds(row, 2 * CHUNK), :]
            vw = vbuf[pl.ds(row, 2 * CHUNK), :]
            p = jnp.exp2(_dot_nt(q2, kw) + bias - lse2)
            ds = p * (_dot_nt(do2, vw) - dl2)
            dq_ref[rs, :] = _unstack_heads(_dot(ds.astype(BF16), kw)).astype(BF16)

        bias_buf[...] = _query_window_bias(s0, s1, dil, False)
        _first_blocks(block, nsub, seg, nch, ch, lambda: _query_window_bias(s0, s1, dil, True), bias_buf)

    dq = pl.pallas_call(
        body, name=f"attn_dq_d{dil}", grid=(dil // merge, nch, NH // 2),
        in_specs=[pl.BlockSpec(memory_space=pltpu.SMEM), main, main, main, main, main_heads, main_heads, prev, prev],
        out_specs=main, out_shape=jax.ShapeDtypeStruct((dil // merge, length * merge, A), BF16),
        scratch_shapes=[pltpu.VMEM((rows + CHUNK, LANES), BF16), pltpu.VMEM((rows + CHUNK, LANES), BF16),
                        pltpu.VMEM((2 * CHUNK, 2 * CHUNK), F32)],
        compiler_params=_cparams("arbitrary", "arbitrary", "arbitrary"),
    )(slopes, q, k, v, do, lse, delta, k, v)
    return dq.reshape(dil, length, A)


def _attn_bwd_dkv(q, k, v, do, lse, delta, slopes, dil):
    length = q.shape[1]
    merge, rows, nch, nsub, seg = _att_geometry(length, dil)
    main, _, nxt, main_heads, nxt_heads = _att_specs(dil, rows, nsub, length * merge // CHUNK)
    q, k, v, do, lse, delta = (_merged(t, merge) for t in (q, k, v, do, lse, delta))

    def body(sl_ref, k_ref, v_ref, q_ref, do_ref, lse_ref, dl_ref, qh_ref, doh_ref, lseh_ref, dlh_ref,
             dk_ref, dv_ref, qbuf, dobuf, lse_rows, dl_rows, bias_buf):
        ch = pl.program_id(1)
        hp = pl.program_id(2)
        for buf, main_ref, halo_ref in ((qbuf, q_ref, qh_ref), (dobuf, do_ref, doh_ref)):
            buf[0:rows, :] = main_ref[...]
            buf[rows:, :] = halo_ref[...]
        for buf, main_ref, halo_ref in ((lse_rows, lse_ref, lseh_ref), (dl_rows, dl_ref, dlh_ref)):
            buf[:, 0:rows] = _head_rows(main_ref[...], hp)
            buf[:, rows:] = _head_rows(halo_ref[...], hp)
        s0, s1 = sl_ref[2 * hp], sl_ref[2 * hp + 1]

        def block(i, bias):
            row = _row_start(i)
            rs = pl.ds(row, CHUNK)
            win = pl.ds(row, 2 * CHUNK)
            kc = k_ref[rs, :]
            vc = v_ref[rs, :]
            q2 = _stack_heads(qbuf[win, :])
            do2 = _stack_heads(dobuf[win, :])
            cols = slice(i * CHUNK, (i + 2) * CHUNK)
            lse2 = jnp.concatenate([lse_rows[0:1, cols], lse_rows[1:2, cols]], axis=1)
            dl2 = jnp.concatenate([dl_rows[0:1, cols], dl_rows[1:2, cols]], axis=1)
            pt = jnp.exp2(_dot_nt(kc, q2) + bias - lse2)
            dst = pt * (_dot_nt(vc, do2) - dl2)
            dv_ref[rs, :] = _dot(pt.astype(BF16), do2).astype(BF16)
            dk_ref[rs, :] = (_dot(dst.astype(BF16), q2) * LN2).astype(BF16)

        bias_buf[...] = _key_block_bias(s0, s1, dil, False)
        _last_blocks(block, nsub, seg, nch, ch, lambda: _key_block_bias(s0, s1, dil, True), bias_buf)

    sd = jax.ShapeDtypeStruct((dil // merge, length * merge, A), BF16)
    dk, dv = pl.pallas_call(
        body, name=f"attn_dkv_d{dil}", grid=(dil // merge, nch, NH // 2),
        in_specs=[pl.BlockSpec(memory_space=pltpu.SMEM), main, main, main, main, main_heads, main_heads,
                  nxt, nxt, nxt_heads, nxt_heads],
        out_specs=[main, main], out_shape=[sd, sd],
        scratch_shapes=[pltpu.VMEM((rows + CHUNK, LANES), BF16), pltpu.VMEM((rows + CHUNK, LANES), BF16),
                        pltpu.VMEM((8, rows + CHUNK), F32), pltpu.VMEM((8, rows + CHUNK), F32),
                        pltpu.VMEM((CHUNK, 4 * CHUNK), F32)],
        compiler_params=_cparams("arbitrary", "arbitrary", "arbitrary"),
    )(slopes, k, v, q, do, lse, delta, q, do, lse, delta)
    return dk.reshape(dil, length, A), dv.reshape(dil, length, A)


def _group_masks(width):
    lane = lax.broadcasted_iota(jnp.int32, (1, width), 1)
    return [(lane >= g * DH) & (lane < (g + 1) * DH) for g in range(width // DH)]


def _group_mean_matrix():
    i = lax.broadcasted_iota(jnp.int32, (GW, GW), 0) // DH
    j = lax.broadcasted_iota(jnp.int32, (GW, GW), 1) // DH
    return jnp.where(i == j, 1.0 / DH, 0.0).astype(F32)


def _tri_mask(lower):
    t = lax.broadcasted_iota(jnp.int32, (CHUNK, CHUNK), 0)
    u = lax.broadcasted_iota(jnp.int32, (CHUNK, CHUNK), 1)
    return (u <= t) if lower else (u >= t)


def _sgu_forward(u, z, lng, lnb, w_ref, bias_t, pmat, rows):
    ug = _gelu(u)
    zg = _gelu(z)
    mu = _dot_hi(zg, pmat)
    zc = zg - mu
    var = _dot_hi(zc * zc, pmat)
    rstd = lax.rsqrt(var + EPS)
    zhat = zc * rstd
    zn = (zhat * lng + lnb).astype(BF16)
    gm = _group_masks(GW)
    tri = _tri_mask(True)
    ws = [jnp.where(tri, w_ref[g], 0.0).astype(BF16) for g in range(NG)]
    pieces = []
    for c in range(rows // CHUNK):
        znc = zn[c * CHUNK:(c + 1) * CHUNK, :]
        mix = None
        for g in range(NG):
            part = jnp.where(gm[g], _dot(ws[g], znc), 0.0)
            mix = part if mix is None else mix + part
        pieces.append(mix + bias_t)
    mixed = jnp.concatenate(pieces, axis=0) if len(pieces) > 1 else pieces[0]
    return ug * mixed, ug, zhat, rstd, zn, mixed


def _head_spread():
    h = lax.broadcasted_iota(jnp.int32, (LANES, A), 0)
    lane = lax.broadcasted_iota(jnp.int32, (LANES, A), 1)
    return jnp.where(lane // DH == h, 1.0, 0.0).astype(BF16)


def _bf16_pieces(t, n):
    pieces = []
    for _ in range(n):
        piece = t.astype(BF16)
        pieces.append(piece)
        t = t - piece.astype(F32)
    return pieces


def _mix_fwd(os_, ls_, u, z, x, lng, lnb, sgu_w, bias_t, ga, gg, wout):
    s = x.shape[0]
    nd = len(DILS)
    nscr = sum(1 for d in DILS if d > 1)

    def body(*refs):
        o_refs, l_refs = refs[:nd], refs[nd:2 * nd]
        u_ref, z_ref, x_ref, lng_ref, lnb_ref, w_ref, bt_ref, ga_ref, gg_ref, wo_ref = refs[2 * nd:2 * nd + 10]
        attn_ref = refs[2 * nd + 10]
        lse_refs = refs[2 * nd + 11:3 * nd + 11]
        mixed_ref, h1_ref = refs[3 * nd + 11:3 * nd + 13]
        scr = refs[3 * nd + 13:]
        scr_o, scr_l, scr_lse = scr[:nscr], scr[nscr:2 * nscr], scr[2 * nscr]
        ov, lv, j = [], [], 0
        for di, dil in enumerate(DILS):
            if dil == 1:
                ov.append(o_refs[di][0].astype(F32))
                lv.append(l_refs[di][0])
            else:
                ov.append(_merge_residues(o_refs[di], scr_o[j], dil))
                lv.append(_merge_residues(l_refs[di], scr_l[j], dil))
                j += 1
        mx = functools.reduce(jnp.maximum, lv)
        es = [jnp.exp2(l - mx) for l in lv]
        den = functools.reduce(lambda a, b: a + b, es)
        spread = _head_spread()
        attn = None
        for e, o in zip(es, ov):
            wide = functools.reduce(lambda a, b: a + b, [_dot(piece, spread) for piece in _bf16_pieces(e / den, 2)])
            attn = wide * o if attn is None else attn + wide * o
        attn_ref[...] = attn
        lse = mx + jnp.log2(den)
        _fill_cols(scr_lse, lse)
        for di, dil in enumerate(DILS):
            if dil == 1:
                lse_refs[di][0] = lse
            else:
                _split_residues(scr_lse, lse_refs[di], dil)
        an, _, _ = _rms_fwd(attn, ga_ref[...])
        gmv, _, _, _, _, _ = _sgu_forward(u_ref[...], z_ref[...], lng_ref[...], lnb_ref[...], w_ref,
                                          bt_ref[...], _group_mean_matrix(), TMX)
        gn, _, _ = _rms_fwd(gmv, gg_ref[...])
        mixed = jnp.concatenate([an, gn], axis=-1).astype(BF16)
        mixed_ref[...] = mixed
        h1_ref[...] = x_ref[...] + _dot(mixed, wo_ref[...])

    sd = jax.ShapeDtypeStruct
    res = pl.pallas_call(
        body, name="mix_fwd", grid=(s // TMX,),
        in_specs=[_res_spec(d, TMX, A) for d in DILS] + [_res_spec(d, TMX, LANES) for d in DILS]
                 + [_row_spec(TMX, GW), _row_spec(TMX, GW),
                    _row_spec(TMX, D), _const_spec((1, GW)), _const_spec((1, GW)), _const_spec((NG, CHUNK, CHUNK)),
                    _const_spec((CHUNK, GW)), _const_spec((1, A)), _const_spec((1, GW)), _const_spec((D, D))],
        out_specs=[_row_spec(TMX, A)] + [_res_spec(d, TMX, LANES) for d in DILS]
                  + [_row_spec(TMX, D), _row_spec(TMX, D)],
        out_shape=[sd((s, A), F32)] + [_res_shape(s, d, LANES, F32) for d in DILS]
                  + [sd((s, D), BF16), sd((s, D), F32)],
        scratch_shapes=[_col_scratch(TMX, A)] * nscr + [_col_scratch(TMX, LANES)] * (nscr + 1),
        compiler_params=_cparams("arbitrary"),
    )(*os_, *ls_, u, z, x, lng, lnb, sgu_w, bias_t, ga, gg, wout)
    return res[0], res[1:1 + nd], res[1 + nd], res[2 + nd]


def _mlp_fwd(h1, g2, wff1, wff2, gf, target):
    s = h1.shape[0]

    def body(h1_ref, g2_ref, w1_ref, w2_ref, gf_ref, t_ref, hn_ref, rf_ref, dh2_ref, loss_ref, dgf_ref):
        i = pl.program_id(0)
        h1v = h1_ref[...]
        hn, _, _ = _rms_fwd(h1v, g2_ref[...])
        hn = hn.astype(BF16)
        hn_ref[...] = hn
        acc = h1v
        for j in range(DFF // FF_CH):
            cols = slice(j * FF_CH, (j + 1) * FF_CH)
            rf = jnp.maximum(_dot(hn, w1_ref[j]), 0.0)
            act = (rf * rf).astype(BF16)
            rf_ref[:, cols] = rf.astype(BF16)
            acc = acc + _dot(act, w2_ref[cols, :])
        y, h2n, r3 = _rms_fwd(acc, gf_ref[...])
        err = y - t_ref[...]
        part = 0.5 * jnp.sum(jnp.mean(err * err, axis=-1, keepdims=True), axis=0, keepdims=True)
        dy = err * (1.0 / D)
        dh2, dgf = _rms_bwd(dy, h2n, r3, gf_ref[...])
        dh2_ref[...] = dh2

        @pl.when(i == 0)
        def _():
            loss_ref[...] = jnp.zeros_like(loss_ref)
            dgf_ref[...] = jnp.zeros_like(dgf_ref)

        loss_ref[...] += jnp.broadcast_to(part, loss_ref.shape)
        dgf_ref[...] += dgf

    sd = jax.ShapeDtypeStruct
    return pl.pallas_call(
        body, name="mlp_fwd", grid=(s // TM,),
        in_specs=[_row_spec(TM, D), _const_spec((1, D)), _const_spec((DFF // FF_CH, D, FF_CH)), _const_spec((DFF, D)),
                  _const_spec((1, D)), _row_spec(TM, D)],
        out_specs=[_row_spec(TM, D), _row_spec(TM, DFF), _row_spec(TM, D),
                   _const_spec((1, LANES)), _const_spec((1, D))],
        out_shape=[sd((s, D), BF16), sd((s, DFF), BF16), sd((s, D), F32),
                   sd((1, LANES), F32), sd((1, D), F32)],
        compiler_params=_cparams("arbitrary"),
    )(h1, g2, wff1, wff2, gf, target)


def _mlp_bwd(dh2, rf, h1, g2, wff1, wff2):
    s = h1.shape[0]

    def body(dh2_ref, rf_ref, h1_ref, g2_ref, w1_ref, w2_ref, df_ref, dh1_ref, dg2_ref):
        i = pl.program_id(0)
        dh2v = dh2_ref[...]
        dh2b = dh2v.astype(BF16)
        dhn = jnp.zeros((TM, D), F32)
        for j in range(DFF // FF_CH):
            cols = slice(j * FF_CH, (j + 1) * FF_CH)
            da = _dot_nt(dh2b, w2_ref[cols, :])
            df = (da * (2.0 * rf_ref[:, cols].astype(F32))).astype(BF16)
            df_ref[:, cols] = df
            dhn = dhn + _dot_nt(df, w1_ref[j])
        _, h1n, r2 = _rms_fwd(h1_ref[...], g2_ref[...])
        dres, dg2 = _rms_bwd(dhn, h1n, r2, g2_ref[...])
        dh1_ref[...] = dh2v + dres

        @pl.when(i == 0)
        def _():
            dg2_ref[...] = jnp.zeros_like(dg2_ref)

        dg2_ref[...] += dg2

    sd = jax.ShapeDtypeStruct
    return pl.pallas_call(
        body, name="mlp_bwd", grid=(s // TM,),
        in_specs=[_row_spec(TM, D), _row_spec(TM, DFF), _row_spec(TM, D), _const_spec((1, D)),
                  _const_spec((DFF // FF_CH, D, FF_CH)), _const_spec((DFF, D))],
        out_specs=[_row_spec(TM, DFF), _row_spec(TM, D), _const_spec((1, D))],
        out_shape=[sd((s, DFF), BF16), sd((s, D), F32), sd((1, D), F32)],
        compiler_params=_cparams("arbitrary"),
    )(dh2, rf, h1, g2, wff1, wff2)


def _mix_bwd(dh1, attn, u, z, lng, lnb, sgu_w, sgu_wt, bias_t, ga, gg, wout):
    s = dh1.shape[0]
    nsteps = s // TMX
    nd = len(DILS)

    def body(*refs):
        dh1_ref, attn_ref, u_ref, z_ref, lng_ref, lnb_ref, w_ref, wt_ref, bt_ref, ga_ref, gg_ref, wo_ref = refs[:12]
        do_refs, dl_refs = refs[12:12 + nd], refs[12 + nd:12 + 2 * nd]
        (du_ref, dz_ref, dga_ref, dgg_ref, dlng_ref, dlnb_ref, dws_ref, db_ref,
         dbt_acc, scr_do, scr_dl) = refs[12 + 2 * nd:]
        i = pl.program_id(0)

        @pl.when(i == 0)
        def _():
            for r in (dga_ref, dgg_ref, dlng_ref, dlnb_ref, dws_ref, db_ref, dbt_acc):
                r[...] = jnp.zeros_like(r)

        dmixed = _dot_nt(dh1_ref[...].astype(BF16), wo_ref[...])
        attn = attn_ref[...]
        _, an, ra = _rms_fwd(attn, ga_ref[...])
        dattn, dga = _rms_bwd(dmixed[:, :A], an, ra, ga_ref[...])
        dga_ref[...] += dga
        _fill_cols(scr_do, dattn)
        spread = _head_spread()
        delta = functools.reduce(lambda a, b: a + b, [_dot_nt(piece, spread) for piece in _bf16_pieces(dattn * attn, 3)])
        _fill_cols(scr_dl, delta)
        for di, dil in enumerate(DILS):
            if dil == 1:
                do_refs[di][0] = dattn.astype(BF16)
                dl_refs[di][0] = delta
            else:
                _split_residues(scr_do, do_refs[di], dil)
                _split_residues(scr_dl, dl_refs[di], dil)
        pmat = _group_mean_matrix()
        lng = lng_ref[...]
        uv, zv = u_ref[...], z_ref[...]
        gmv, ug, zhat, rstd, zn, mixed = _sgu_forward(uv, zv, lng, lnb_ref[...], w_ref, bt_ref[...], pmat, TMX)
        _, gmn, rg = _rms_fwd(gmv, gg_ref[...])
        dgm, dgg = _rms_bwd(dmixed[:, A:], gmn, rg, gg_ref[...])
        dgg_ref[...] += dgg
        du_ref[...] = (dgm * mixed * _gelu_grad(uv)).astype(BF16)
        dmx = dgm * ug
        dmxb = dmx.astype(BF16)
        gm = _group_masks(GW)
        tri_t = _tri_mask(False)
        wst = [jnp.where(tri_t, wt_ref[g], 0.0).astype(BF16) for g in range(NG)]
        zero = jnp.zeros((CHUNK, GW), BF16)
        dzn_pieces = []
        for c in range(TMX // CHUNK):
            rs = slice(c * CHUNK, (c + 1) * CHUNK)
            dmc = dmxb[rs, :]
            znc = zn[rs, :]
            dbt_acc[...] += dmx[rs, :]
            dzn = None
            for g in range(NG):
                dws_ref[g] += _dot_nt(jnp.where(gm[g], dmc, zero), znc)
                part = jnp.where(gm[g], _dot(wst[g], dmc), 0.0)
                dzn = part if dzn is None else dzn + part
            dzn_pieces.append(dzn)
        dzn = jnp.concatenate(dzn_pieces, axis=0)
        dlng_ref[...] += jnp.sum(dzn * zhat, axis=0, keepdims=True)
        dlnb_ref[...] += jnp.sum(dzn, axis=0, keepdims=True)
        dzh = dzn * lng
        dzg = rstd * (dzh - _dot_hi(dzh, pmat) - zhat * _dot_hi(dzh * zhat, pmat))
        dz_ref[...] = (dzg * _gelu_grad(zv)).astype(BF16)

        @pl.when(i == nsteps - 1)
        def _():
            tri = _tri_mask(True)
            for g in range(NG):
                dws_ref[g] = jnp.where(tri, dws_ref[g], 0.0)
            acc = dbt_acc[...]
            lane = lax.broadcasted_iota(jnp.int32, (CHUNK, LANES), 1)
            out = jnp.zeros((CHUNK, LANES), F32)
            for g in range(NG):
                sg = jnp.sum(jnp.where(gm[g], acc, 0.0), axis=-1, keepdims=True)
                out = jnp.where(lane == g, sg, out)
            db_ref[...] = out

    sd = jax.ShapeDtypeStruct
    res = pl.pallas_call(
        body, name="mix_bwd", grid=(nsteps,),
        in_specs=[_row_spec(TMX, D), _row_spec(TMX, A), _row_spec(TMX, GW), _row_spec(TMX, GW),
                  _const_spec((1, GW)), _const_spec((1, GW)), _const_spec((NG, CHUNK, CHUNK)),
                  _const_spec((NG, CHUNK, CHUNK)), _const_spec((CHUNK, GW)), _const_spec((1, A)),
                  _const_spec((1, GW)), _const_spec((D, D))],
        out_specs=[_res_spec(d, TMX, A) for d in DILS] + [_res_spec(d, TMX, LANES) for d in DILS]
                  + [_row_spec(TMX, GW), _row_spec(TMX, GW),
                   _const_spec((1, A)), _const_spec((1, GW)), _const_spec((1, GW)), _const_spec((1, GW)),
                   _const_spec((NG, CHUNK, CHUNK)), _const_spec((CHUNK, LANES))],
        out_shape=[_res_shape(s, d, A, BF16) for d in DILS] + [_res_shape(s, d, LANES, F32) for d in DILS]
                  + [sd((s, GW), BF16), sd((s, GW), BF16),
                   sd((1, A), F32), sd((1, GW), F32), sd((1, GW), F32), sd((1, GW), F32),
                   sd((NG, CHUNK, CHUNK), F32), sd((CHUNK, LANES), F32)],
        scratch_shapes=[pltpu.VMEM((CHUNK, GW), F32), _col_scratch(TMX, A), _col_scratch(TMX, LANES)],
        compiler_params=_cparams("arbitrary"),
    )(dh1, attn, u, z, lng, lnb, sgu_w, sgu_wt, bias_t, ga, gg, wout)
    return (res[:nd], res[nd:2 * nd]) + tuple(res[2 * nd:])


def _dproj_merge(dqs, dks, dvs, du, dz, pin):
    s = du.shape[0]
    nd = len(DILS)
    nscr = sum(1 for d in DILS if d > 1)

    def body(*refs):
        pin_ref = refs[0]
        parts = [refs[1 + t * nd:1 + (t + 1) * nd] for t in range(3)]
        du_ref, dz_ref, dp_ref = refs[1 + 3 * nd:4 + 3 * nd]
        scr = refs[4 + 3 * nd:]
        sums = []
        for t in range(3):
            total, j = None, 0
            for di, dil in enumerate(DILS):
                if dil == 1:
                    term = parts[t][di][0].astype(F32)
                else:
                    term = _merge_residues(parts[t][di], scr[t * nscr + j], dil)
                    j += 1
                total = term if total is None else total + term
            sums.append(total)
        dp_ref[...] = jnp.concatenate([sums[0] * SCALE, sums[1], sums[2], du_ref[...].astype(F32) + pin_ref[0, 0],
                                       dz_ref[...].astype(F32)], axis=-1).astype(BF16)

    return pl.pallas_call(
        body, name="dproj_merge", grid=(s // TMX,),
        in_specs=[pl.BlockSpec(memory_space=pltpu.SMEM)] + [_res_spec(d, TMX, A) for d in DILS] * 3
                 + [_row_spec(TMX, GW)] * 2,
        out_specs=_row_spec(TMX, INW), out_shape=jax.ShapeDtypeStruct((s, INW), BF16),
        scratch_shapes=[_col_scratch(TMX, A)] * (3 * nscr),
        compiler_params=_cparams("arbitrary"),
    )(pin, *dqs, *dks, *dvs, du, dz)


def _inproj_bwd(dproj, dh1, x, g1, win_t):
    s = x.shape[0]

    def body(dp_ref, dh1_ref, x_ref, g_ref, w_ref, dx_ref, dg_ref):
        i = pl.program_id(0)
        dhn = _dot(dp_ref[...], w_ref[...])
        _, xn, r1 = _rms_fwd(x_ref[...], g_ref[...])
        dres, dg = _rms_bwd(dhn, xn, r1, g_ref[...])
        dx_ref[...] = dh1_ref[...] + dres

        @pl.when(i == 0)
        def _():
            dg_ref[...] = jnp.zeros_like(dg_ref)

        dg_ref[...] += dg

    sd = jax.ShapeDtypeStruct
    return pl.pallas_call(
        body, name="inproj_bwd", grid=(s // TM_BIG,),
        in_specs=[_row_spec(TM_BIG, INW), _row_spec(TM_BIG, D), _row_spec(TM_BIG, D), _const_spec((1, D)), _const_spec((INW, D))],
        out_specs=[_row_spec(TM_BIG, D), _const_spec((1, D))],
        out_shape=[sd((s, D), F32), sd((1, D), F32)],
        compiler_params=_cparams("arbitrary"),
    )(dproj, dh1, x, g1, win_t)


def _wgrad(a, b, name, bm, bn, bk=4 * TM, square_a=False, also_bf16=False):
    s, m = a.shape
    n = b.shape[1]
    bm, bn = min(bm, m), min(bn, n)
    nk = s // bk

    def body(a_ref, b_ref, o_ref, *low):
        @pl.when(pl.program_id(2) == 0)
        def _():
            o_ref[...] = jnp.zeros_like(o_ref)

        av = a_ref[...]
        if square_a:
            av = av.astype(F32)
            av = av * av
        o_ref[...] += _dot_tn(av.astype(BF16), b_ref[...].astype(BF16))
        if also_bf16:
            @pl.when(pl.program_id(2) == nk - 1)
            def _():
                low[0][...] = o_ref[...].astype(BF16)

    out_spec = pl.BlockSpec((bm, bn), lambda i, j, k: (i, j))
    res = pl.pallas_call(
        body, name=name, grid=(m // bm, n // bn, nk),
        in_specs=[pl.BlockSpec((bk, bm), lambda i, j, k: (k, i)), pl.BlockSpec((bk, bn), lambda i, j, k: (k, j))],
        out_specs=[out_spec, out_spec] if also_bf16 else out_spec,
        out_shape=([jax.ShapeDtypeStruct((m, n), F32), jax.ShapeDtypeStruct((m, n), BF16)] if also_bf16
                   else jax.ShapeDtypeStruct((m, n), F32)),
        compiler_params=_cparams("arbitrary", "arbitrary", "arbitrary"),
    )(a, b)
    return res


def _adamw_math(w, g, m, v):
    m = B1 * m + (1.0 - B1) * g
    v = B2 * v + (1.0 - B2) * (g * g)
    m_hat = m / (1.0 - B1 ** STEP)
    v_hat = v / (1.0 - B2 ** STEP)
    delta = -LR * (m_hat / (jnp.sqrt(v_hat) + AEPS) + WD * w)
    return delta, m, v


def _adamw(w, g, m, v, name):
    rows, cols = w.shape
    br = min(rows, 256)
    while rows % br:
        br -= 8

    def body(w_ref, g_ref, m_ref, v_ref, d_ref, mo_ref, vo_ref):
        d, mn, vn = _adamw_math(w_ref[...], g_ref[...], m_ref[...], v_ref[...])
        d_ref[...] = d
        mo_ref[...] = mn
        vo_ref[...] = vn

    spec = _row_spec(br, cols)
    sd = jax.ShapeDtypeStruct((rows, cols), F32)
    return pl.pallas_call(
        body, name=name, grid=(rows // br,), in_specs=[spec] * 4, out_specs=[spec] * 3,
        out_shape=[sd, sd, sd], compiler_params=_cparams("arbitrary"),
    )(w, g, m, v)


def _local_step(x, hn1, target, small, win_t, rest_weights, early_grads=None, after_attention_bwd=None,
                late_grads=None):
    slopes = jnp.asarray(_alibi_slopes(NH) * np.float32(LOG2E))
    q, k, v, u, z = _inproj_fwd(hn1, win_t)
    outs, lses = [], []
    for i, dil in enumerate(DILS):
        o, l = _attn_fwd(q[i], k[i], v[i], slopes, dil)
        outs.append(o)
        lses.append(l)
    wout, wff1, wff2 = rest_weights(functools.reduce(lambda a, b: a + b, [l[0, 0:8, :] for l in lses]))
    attn, lse, mixed, h1 = _mix_fwd(outs, lses, u, z, x, small["ln_g"], small["ln_b"], small["sgu_w"],
                                    small["bias_t"], small["attn_out_g"], small["gmlp_out_g"], wout)
    hn2, rf, dh2, loss, dgf = _mlp_fwd(h1, small["norm2_g"], wff1, wff2, small["final_norm_g"], target)
    df, dh1, dg2 = _mlp_bwd(dh2, rf, h1, small["norm2_g"], wff1, wff2)
    gwff1 = _wgrad(hn2, df, "wgrad_ff1", D, 1024)
    gwff2 = _wgrad(rf, dh2, "wgrad_ff2", 1024, D, square_a=True)
    gwout = _wgrad(mixed, dh1, "wgrad_out", D, D)
    ga, g1 = small["attn_out_g"], small["norm1_g"]
    pin = early_grads(gwff1, gwff2, gwout) if early_grads else None
    if pin is not None:
        ga = ga + pin
    (do, delta, du, dz, dga, dgg, dlng, dlnb, dws, db) = _mix_bwd(
        dh1, attn, u, z, small["ln_g"], small["ln_b"], small["sgu_w"], small["sgu_wt"], small["bias_t"],
        ga, small["gmlp_out_g"], wout)
    dqs, dks, dvs = [], [], []
    for i, dil in enumerate(DILS):
        dqs.append(_attn_bwd_dq(q[i], k[i], v[i], do[i], lse[i], delta[i], slopes, dil))
        dk, dv = _attn_bwd_dkv(q[i], k[i], v[i], do[i], lse[i], delta[i], slopes, dil)
        dks.append(dk)
        dvs.append(dv)
    marker = functools.reduce(lambda a, b: a + b, [t[0, 0:8, 0:LANES] for t in dqs + dks + dvs])
    partial = dict(ln_g=dlng, ln_b=dlnb, sgu_w=dws, sgu_b=db[:, :NG].T, attn_out_g=dga, gmlp_out_g=dgg,
                   norm2_g=dg2, final_norm_g=dgf)
    pin = after_attention_bwd(marker, partial, loss[0, 0]) if after_attention_bwd else None
    dproj = _dproj_merge(dqs, dks, dvs, du, dz, jnp.zeros((1, 1), F32) if pin is None else pin)
    gwin_t, gwin_low = _wgrad(dproj, hn1, "wgrad_in", INW // 2, D, also_bf16=True)
    pin = late_grads(gwin_t, gwin_low) if late_grads else None
    if pin is not None:
        g1 = g1 + pin
    dx, dg1 = _inproj_bwd(dproj, dh1, x, g1, win_t)
    small_grads = dict(partial, norm1_g=dg1)
    return loss[0, 0], dx, small_grads, (gwin_t, gwout, gwff1, gwff2)


ANY = pl.BlockSpec(memory_space=pl.ANY)
NDEV = 8


def _position():
    return lax.axis_index("x"), lax.axis_index("y"), lax.axis_index("c")


def _other_chips(x, y):
    return [(1 - x, y), (x, 1 - y), (1 - x, 1 - y)]


def _remote(src, dst, send_sem, recv_sem, device):
    return pltpu.make_async_remote_copy(src_ref=src, dst_ref=dst, send_sem=send_sem, recv_sem=recv_sem,
                                        device_id=device, device_id_type=MESH)


HBM = pl.BlockSpec(memory_space=pltpu.HBM)
SEM = pl.BlockSpec(memory_space=pltpu.SEMAPHORE)
DATAFLOW = pltpu.SideEffectType.DATAFLOW_SIDE_EFFECTING


def _in_hbm(a):
    return pltpu.with_memory_space_constraint(a, pltpu.HBM)


def _gather_start(shards, name):
    n = len(shards)
    lands = [jnp.broadcast_to(sh[None], (NCHIP,) + sh.shape) for sh in shards]

    def body(*refs):
        w_refs, land_refs = refs[:n], refs[n:2 * n]
        send_sems, recv_sems = refs[2 * n:2 * n + 2]
        token = refs[-1]
        x, y, c = _position()
        for w in range(n):
            for k, (px, py) in enumerate(_other_chips(x, y)):
                m = 3 * w + k
                _remote(w_refs[w], land_refs[w].at[2 * x + y], send_sems.at[m], recv_sems.at[m], (px, py, c)).start()
        token[...] = jnp.zeros_like(token)

    res = _split_call(body, name, list(shards) + lands, (3 * n, 3 * n), (TOKEN,))
    return res[0], res[1], res[2:2 + n], res[2 + n:2 + 2 * n], res[-1]


def _gather_wait(send_sems, recv_sems, shards, lands, after, name):
    n = len(shards)

    def body(*refs):
        w_refs, land_refs = refs[:n], refs[n:2 * n]
        send_sems, recv_sems = refs[2 * n:2 * n + 2]
        x, y, c = _position()
        for w in range(n):
            for k, (px, py) in enumerate(_other_chips(x, y)):
                m = 3 * w + k
                cp = _remote(w_refs[w], land_refs[w].at[2 * px + py], send_sems.at[m], recv_sems.at[m], (px, py, c))
                cp.wait_send()
                cp.wait_recv()

    operands = list(shards) + list(lands)
    res = pl.pallas_call(
        body, name=name, out_shape=tuple(pltpu.HBM(a.shape, a.dtype) for a in operands),
        in_specs=(HBM,) * (2 * n) + (SEM, SEM, ANY), out_specs=(HBM,) * (2 * n),
        input_output_aliases={i: i for i in range(2 * n)},
        compiler_params=pltpu.CompilerParams(has_side_effects=DATAFLOW),
    )(*operands, send_sems, recv_sems, after)
    return res[n:]


def _xor_peers(x, y, c):
    peers = []
    for k in range(1, NDEV):
        kx, ky, kc = (k >> 2) & 1, (k >> 1) & 1, k & 1
        peers.append((1 - x if kx else x, 1 - y if ky else y, 1 - c if kc else c))
    return peers


def _piece(part_ref, px, py, pc):
    slab = 2 * px + py
    if len(part_ref.shape) == 3:
        half = part_ref.shape[1] // 2
        return part_ref.at[slab, pl.ds(pc * half, half), :]
    half = part_ref.shape[0] // 2
    return part_ref.at[pl.ds(pc * half, half), pl.ds(pl.multiple_of(slab * D, D), D)]


def _split_call(body, name, operands, n_sems, extra_out=()):
    n = len(operands)
    sems = tuple(pltpu.SemaphoreType.DMA((m,)) for m in n_sems)
    thru = tuple(pltpu.HBM(a.shape, a.dtype) for a in operands)
    return pl.pallas_call(
        body, name=name, out_shape=sems + thru + tuple(extra_out),
        in_specs=(HBM,) * n,
        out_specs=(SEM,) * len(sems) + (HBM,) * n + (pl.BlockSpec(memory_space=pltpu.VMEM),) * len(extra_out),
        input_output_aliases={i: len(sems) + i for i in range(n)},
        compiler_params=pltpu.CompilerParams(has_side_effects=DATAFLOW),
    )(*[_in_hbm(a) for a in operands])


TOKEN = jax.ShapeDtypeStruct((8, LANES), F32)


def _pack_copies(pack_ref, land_ref, send_sems, recv_sems, base, position, start):
    x, y, c = position
    for k, (px, py, pc) in enumerate(_xor_peers(x, y, c)):
        if start:
            _remote(pack_ref, land_ref.at[4 * x + 2 * y + c], send_sems.at[base + k], recv_sems.at[base + k],
                    (px, py, pc)).start()
        else:
            cp = _remote(pack_ref, land_ref.at[4 * px + 2 * py + pc], send_sems.at[base + k], recv_sems.at[base + k],
                         (px, py, pc))
            cp.wait_send()
            cp.wait_recv()


def _pack_landing(pack):
    return jnp.broadcast_to(pack[None], (NDEV,) + pack.shape)


def _reduce_start(parts, name, pack=None):
    nw = len(parts)
    lands = [lax.empty((NDEV - 1, p.shape[-2] // 2, D), p.dtype) for p in parts]
    operands = list(parts) + lands + ([pack, _pack_landing(pack)] if pack is not None else [])
    nops = len(operands)

    def body(*refs):
        part_refs, land_refs = refs[:nw], refs[nw:2 * nw]
        send_sems, recv_sems = refs[nops:nops + 2]
        token = refs[-1]
        x, y, c = _position()
        for w in range(nw):
            for k, peer in enumerate(_xor_peers(x, y, c)):
                n = w * (NDEV - 1) + k
                _remote(_piece(part_refs[w], *peer), land_refs[w].at[k], send_sems.at[n], recv_sems.at[n],
                        peer).start()
        if pack is not None:
            _pack_copies(refs[2 * nw], refs[2 * nw + 1], send_sems, recv_sems, nw * (NDEV - 1), (x, y, c), True)
        token[...] = jnp.zeros_like(token)

    n = (nw + (pack is not None)) * (NDEV - 1)
    res = _split_call(body, name, operands, (n, n), (TOKEN,))
    return res[0], res[1], res[2:2 + nops], res[-1]


def _reduce_wait(send_sems, recv_sems, operands, nw, after, name):
    nops = len(operands)
    has_pack = nops > 2 * nw

    def body(*refs):
        part_refs, land_refs = refs[:nw], refs[nw:2 * nw]
        send_sems, recv_sems = refs[nops:nops + 2]
        x, y, c = _position()
        for w in range(nw):
            for k, peer in enumerate(_xor_peers(x, y, c)):
                n = w * (NDEV - 1) + k
                cp = _remote(_piece(part_refs[w], *peer), land_refs[w].at[k], send_sems.at[n], recv_sems.at[n], peer)
                cp.wait_send()
                cp.wait_recv()
        if has_pack:
            _pack_copies(refs[2 * nw], refs[2 * nw + 1], send_sems, recv_sems, nw * (NDEV - 1), (x, y, c), False)

    res = pl.pallas_call(
        body, name=name, out_shape=tuple(pltpu.HBM(a.shape, a.dtype) for a in operands),
        in_specs=(HBM,) * nops + (SEM, SEM, ANY), out_specs=(HBM,) * nops,
        input_output_aliases={i: i for i in range(nops)},
        compiler_params=pltpu.CompilerParams(has_side_effects=DATAFLOW),
    )(*operands, send_sems, recv_sems, after)
    return res[:nw], res[nw:2 * nw], (res[2 * nw + 1] if has_pack else None)


def _sum_pieces(part, land, sel, name):
    half = part.shape[-2] // 2
    br = 128 if half % 128 == 0 else half // 2
    nb = half // br

    def body(sel_ref, own_ref, *refs):
        acc = own_ref[...]
        for r in refs[:NDEV - 1]:
            acc = acc + r[...].astype(F32)
        refs[NDEV - 1][...] = acc

    if part.ndim == 3:
        own_spec = pl.BlockSpec((None, br, D), lambda i, sel_ref: (sel_ref[0], sel_ref[1] * nb + i, 0))
    else:
        own_spec = pl.BlockSpec((br, D), lambda i, sel_ref: (sel_ref[1] * nb + i, sel_ref[0]))
    slot_specs = [pl.BlockSpec((None, br, D), functools.partial(lambda i, sel_ref, k: (k, i, 0), k=k))
                  for k in range(NDEV - 1)]
    return pl.pallas_call(
        body, name=name,
        grid_spec=pltpu.PrefetchScalarGridSpec(
            num_scalar_prefetch=1, grid=(nb,), in_specs=[own_spec] + slot_specs,
            out_specs=pl.BlockSpec((br, D), lambda i, sel_ref: (i, 0))),
        out_shape=jax.ShapeDtypeStruct((half, D), F32),
        compiler_params=_cparams("arbitrary"),
    )(sel, part, *([land] * (NDEV - 1)))


def _share_start(halves, name, pack=None):
    nw = len(halves)
    lands = [lax.empty(h.shape, F32) for h in halves]
    operands = list(halves) + lands + ([pack, _pack_landing(pack)] if pack is not None else [])
    nops = len(operands)

    def body(*refs):
        h_refs, land_refs = refs[:nw], refs[nw:2 * nw]
        send_sems, recv_sems = refs[nops:nops + 2]
        token = refs[-1]
        x, y, c = _position()
        for w in range(nw):
            _remote(h_refs[w], land_refs[w], send_sems.at[w], recv_sems.at[w], (x, y, 1 - c)).start()
        if pack is not None:
            _pack_copies(refs[2 * nw], refs[2 * nw + 1], send_sems, recv_sems, nw, (x, y, c), True)
        token[...] = jnp.zeros_like(token)

    n = nw + (NDEV - 1 if pack is not None else 0)
    res = _split_call(body, name, operands, (n, n), (TOKEN,))
    return res[0], res[1], res[2:2 + nops], res[-1]


def _share_wait(send_sems, recv_sems, operands, nw, after, name):
    nops = len(operands)
    has_pack = nops > 2 * nw

    def body(*refs):
        h_refs, land_refs = refs[:nw], refs[nw:2 * nw]
        send_sems, recv_sems = refs[nops:nops + 2]
        x, y, c = _position()
        for w in range(nw):
            cp = _remote(h_refs[w], land_refs[w], send_sems.at[w], recv_sems.at[w], (x, y, 1 - c))
            cp.wait_send()
            cp.wait_recv()
        if has_pack:
            _pack_copies(refs[2 * nw], refs[2 * nw + 1], send_sems, recv_sems, nw, (x, y, c), False)

    res = pl.pallas_call(
        body, name=name, out_shape=tuple(pltpu.HBM(a.shape, a.dtype) for a in operands),
        in_specs=(HBM,) * nops + (SEM, SEM, ANY), out_specs=(HBM,) * nops,
        input_output_aliases={i: i for i in range(nops)},
        compiler_params=pltpu.CompilerParams(has_side_effects=DATAFLOW),
    )(*operands, send_sems, recv_sems, after)
    return res[:nw], res[nw:2 * nw], (res[2 * nw + 1] if has_pack else None)


def _join_halves(own, other, c):
    first = jnp.where(c == 0, own, other)
    second = jnp.where(c == 0, other, own)
    return jnp.concatenate([first, second], axis=0)


SMALL_SIZES = (("norm1_g", D), ("sgu_ln_g", GW), ("sgu_ln_b", GW), ("sgu_w", NG * CHUNK * CHUNK),
               ("sgu_b", NG * CHUNK), ("attn_out_g", A), ("gmlp_out_g", GW), ("norm2_g", D),
               ("final_norm_g", D))
PARAM_ROWS = sum(n for _, n in SMALL_SIZES) // LANES
SMALL_ROWS = PARAM_ROWS + 8


def _pack_small(tree, first_extra=None):
    extra = jnp.zeros((8 * LANES,), F32)
    if first_extra is not None:
        extra = extra.at[0].set(first_extra)
    flat = jnp.concatenate([tree[n].reshape(-1) for n, _ in SMALL_SIZES] + [extra])
    return flat.reshape(SMALL_ROWS, LANES)


def _unpack_small(pack, shapes):
    flat = pack.reshape(-1)
    out, off = {}, 0
    for n, size in SMALL_SIZES:
        out[n] = flat[off:off + size].reshape(shapes[n])
        off += size
    return out


def _small_finish(pack_land, norm_land, wpack, mpack, vpack):
    def body(p_ref, n_ref, w_ref, m_ref, v_ref, go_ref, d_ref, mo_ref, vo_ref):
        total = p_ref[0]
        late = n_ref[0]
        for k in range(1, NDEV):
            total = total + p_ref[k]
            late = late + n_ref[k]
        go_ref[...] = total
        go_ref[0:8, :] = total[0:8, :] + late
        d, mn, vn = _adamw_math(w_ref[...], go_ref[...], m_ref[...], v_ref[...])
        d_ref[...] = d
        mo_ref[...] = mn
        vo_ref[...] = vn

    sd = jax.ShapeDtypeStruct((SMALL_ROWS, LANES), F32)
    vm = pl.BlockSpec(memory_space=pltpu.VMEM)
    return pl.pallas_call(
        body, name="small_finish", in_specs=[vm] * 5, out_specs=[vm] * 4, out_shape=[sd] * 4,
        compiler_params=_cparams(),
    )(pack_land, norm_land, wpack, mpack, vpack)


def kernel(x, norm1_g, w_in, sgu_ln_g, sgu_ln_b, sgu_w, sgu_b, attn_out_g, gmlp_out_g, w_out, norm2_g, w_ff1, w_ff2, final_norm_g, loss_target, m_norm1_g, m_w_in, m_sgu_ln_g, m_sgu_ln_b, m_sgu_w, m_sgu_b, m_attn_out_g, m_gmlp_out_g, m_w_out, m_norm2_g, m_w_ff1, m_w_ff2, m_final_norm_g, v_norm1_g, v_w_in, v_sgu_ln_g, v_sgu_ln_b, v_sgu_w, v_sgu_b, v_attn_out_g, v_gmlp_out_g, v_w_out, v_norm2_g, v_w_ff1, v_w_ff2, v_final_norm_g):
    names = [n for n, _ in SMALL_SIZES]
    w_small = dict(norm1_g=norm1_g, sgu_ln_g=sgu_ln_g, sgu_ln_b=sgu_ln_b, sgu_w=sgu_w, sgu_b=sgu_b,
                   attn_out_g=attn_out_g, gmlp_out_g=gmlp_out_g, norm2_g=norm2_g, final_norm_g=final_norm_g)
    m_small = dict(norm1_g=m_norm1_g, sgu_ln_g=m_sgu_ln_g, sgu_ln_b=m_sgu_ln_b, sgu_w=m_sgu_w, sgu_b=m_sgu_b,
                   attn_out_g=m_attn_out_g, gmlp_out_g=m_gmlp_out_g, norm2_g=m_norm2_g,
                   final_norm_g=m_final_norm_g)
    v_small = dict(norm1_g=v_norm1_g, sgu_ln_g=v_sgu_ln_g, sgu_ln_b=v_sgu_ln_b, sgu_w=v_sgu_w, sgu_b=v_sgu_b,
                   attn_out_g=v_attn_out_g, gmlp_out_g=v_gmlp_out_g, norm2_g=v_norm2_g,
                   final_norm_g=v_final_norm_g)
    shapes = {n: w_small[n].shape for n in names}

    start_in = _gather_start([w_in[0].T.astype(BF16)], "gather_in_start")
    issued = start_in[4][0:1, 0:1]
    start_rest = _gather_start([(w_out[0] + issued).astype(BF16), w_ff1[0].astype(BF16), w_ff2[0].astype(BF16)],
                               "gather_rest_start")
    hn1 = _norm1(x[0], norm1_g + start_rest[4][0:1, 0:1])
    win_t = _gather_wait(*start_in[:4], after=hn1, name="gather_in_wait")[0].reshape(INW, D)

    def rest_weights(after):
        wout, wff1, wff2 = _gather_wait(*start_rest[:4], after=after, name="gather_rest_wait")
        return wout.reshape(D, D), wff1, wff2.reshape(DFF, D)

    small = dict(
        norm1_g=norm1_g, ln_g=sgu_ln_g.reshape(1, GW), ln_b=sgu_ln_b.reshape(1, GW), sgu_w=sgu_w[0],
        sgu_wt=jnp.swapaxes(sgu_w[0], 1, 2), bias_t=jnp.repeat(sgu_b[0].T, DH, axis=1),
        attn_out_g=attn_out_g, gmlp_out_g=gmlp_out_g, norm2_g=norm2_g, final_norm_g=final_norm_g.reshape(1, D))
    xi, yi, ci = _position()
    sel = jnp.stack([2 * xi + yi, ci]).astype(jnp.int32)
    state = {}

    def as_slabs(g):
        return g.reshape(NCHIP, g.shape[0] // NCHIP, D)

    def early_grads(gwff1, gwff2, gwout):
        state["early"] = _reduce_start([gwff1, as_slabs(gwff2), as_slabs(gwout)], "reduce_early_start")
        return state["early"][3][0:1, 0:1]

    def after_attention_bwd(marker, partial, loss_part):
        send_sems, recv_sems, operands, _ = state["early"]
        parts, lands, _ = _reduce_wait(send_sems, recv_sems, operands, 3, marker, "reduce_early_wait")
        halves = [_sum_pieces(p, l, sel, "sum_" + n) for p, l, n in zip(parts, lands, ("w_ff1", "w_ff2", "w_out"))]
        pack = _pack_small(dict(partial, norm1_g=jnp.zeros((1, D), F32), sgu_ln_g=partial["ln_g"],
                                sgu_ln_b=partial["ln_b"]), loss_part)
        state["early_share"] = _share_start(halves, "share_early_start", pack)
        return state["early_share"][3][0:1, 0:1]

    def late_grads(gwin_t, gwin_low):
        state["late"] = _reduce_start([as_slabs(gwin_low)], "reduce_late_start")
        state["late_own"] = as_slabs(gwin_t)
        return state["late"][3][0:1, 0:1]

    _, dx, sg, _ = _local_step(
        x[0], hn1, loss_target[0], small, win_t, rest_weights, early_grads, after_attention_bwd, late_grads)
    send_sems, recv_sems, operands, _ = state["early_share"]
    own, other, pack_land = _share_wait(send_sems, recv_sems, operands, 3, dx, "share_early_wait")
    send_sems, recv_sems, operands, _ = state["late"]
    _, late_lands, _ = _reduce_wait(send_sems, recv_sems, operands, 1, dx, "reduce_late_wait")
    late_share = _share_start([_sum_pieces(state["late_own"], late_lands[0], sel, "sum_w_in")], "share_late_start",
                              sg["norm1_g"].reshape(8, LANES))
    issued = late_share[3][0:1, 0:1]
    g_big = {n: _join_halves(o, t, ci) + issued for n, o, t in zip(("w_ff1", "w_ff2", "w_out"), own, other)}
    w_big = dict(w_in=(w_in, m_w_in, v_w_in), w_out=(w_out, m_w_out, v_w_out),
                 w_ff1=(w_ff1, m_w_ff1, v_w_ff1), w_ff2=(w_ff2, m_w_ff2, v_w_ff2))
    grads, deltas, new_m, new_v = {}, {}, {}, {}

    def update(n):
        w, m, v = w_big[n]
        d, mn, vn = _adamw(w[0], g_big[n], m[0], v[0], "adamw_" + n)
        grads[n], deltas[n], new_m[n], new_v[n] = g_big[n][None], d[None], mn[None], vn[None]

    for n in ("w_ff1", "w_ff2", "w_out"):
        update(n)
    updated = deltas["w_out"][0, 0:8, 0:LANES] + deltas["w_ff1"][0, 0:8, 0:LANES] + deltas["w_ff2"][0, 0:8, 0:LANES]
    own, other, norm_land = _share_wait(late_share[0], late_share[1], late_share[2], 1, updated, "share_late_wait")
    g_big["w_in"] = _join_halves(own[0], other[0], ci).T
    update("w_in")

    packs = _small_finish(pack_land, norm_land, _pack_small(w_small), _pack_small(m_small), _pack_small(v_small))
    loss = packs[0][PARAM_ROWS, 0]
    for tree, pack in zip((grads, deltas, new_m, new_v), packs):
        tree.update(_unpack_small(pack, shapes))

    order = ["norm1_g", "w_in", "sgu_ln_g", "sgu_ln_b", "sgu_w", "sgu_b", "attn_out_g", "gmlp_out_g", "w_out",
             "norm2_g", "w_ff1", "w_ff2", "final_norm_g"]
    return (loss, dx[None], *[grads[n] for n in order], *[deltas[n] for n in order],
            *[new_m[n] for n in order], *[new_v[n] for n in order])
```

```python
import functools
import math

import numpy as np
import jax
import jax.numpy as jnp
from jax import lax
from jax.experimental import pallas as pl
from jax.experimental.pallas import tpu as pltpu

F32 = jnp.float32
BF16 = jnp.bfloat16

D = 1024
NH = 12
DH = 64
A = NH * DH
NG = 4
GW = NG * DH
INW = 3 * A + 2 * GW
DFF = 4 * D
CHUNK = 128
PATTERNS = ((128, 1), (512, 4), (2048, 16))
EPS = 1e-6
SCALE = DH ** -0.5
LOG2E = 1.0 / math.log(2.0)
LN2 = math.log(2.0)
NEG = -1e30

LR, B1, B2, AEPS, WD, STEP = 0.001, 0.9, 0.999, 1e-08, 0.01, 10

TM = 512
TM_BIG = 1024
TMX = 512
ATT_ROWS = 4096
FF_CH = 1024
LANES = 128
NCHIP = 4
VMEM_LIMIT = 56 * 1024 * 1024
MESH = pl.DeviceIdType.MESH


def _cparams(*sem, **kw):
    return pltpu.CompilerParams(dimension_semantics=sem if sem else None,
                                vmem_limit_bytes=VMEM_LIMIT, **kw)


def _dot(a, b):
    return jnp.dot(a, b, preferred_element_type=F32)


def _dot_nt(a, b):
    return lax.dot_general(a, b, (((1,), (1,)), ((), ())), preferred_element_type=F32)


def _dot_tn(a, b):
    return lax.dot_general(a, b, (((0,), (0,)), ((), ())), preferred_element_type=F32)


def _dot_hi(a, b):
    bb = b.astype(BF16)
    return functools.reduce(lambda x, y: x + y, [_dot(piece, bb) for piece in _bf16_pieces(a, 3)])


def _alibi_slopes(n):
    def pow2(m):
        start = 2.0 ** (-8.0 / m)
        return [start ** (i + 1) for i in range(m)]
    if math.log2(n).is_integer():
        s = pow2(n)
    else:
        c = 2 ** int(math.floor(math.log2(n)))
        s = pow2(c) + pow2(2 * c)[0::2][: n - c]
    return np.asarray(s, dtype=np.float32)


def _rms_fwd(v, g):
    r = lax.rsqrt(jnp.mean(v * v, axis=-1, keepdims=True) + EPS)
    vn = v * r
    return vn * g, vn, r


def _rms_bwd(dy, vn, r, g):
    w = dy * g
    dv = r * (w - vn * jnp.mean(w * vn, axis=-1, keepdims=True))
    return dv, jnp.sum(dy * vn, axis=0, keepdims=True)


_K0 = math.sqrt(2.0 / math.pi)
_K1 = 0.044715


def _gelu(v):
    return 0.5 * v * (1.0 + jnp.tanh(_K0 * (v + _K1 * (v * v * v))))


def _gelu_grad(v):
    t = jnp.tanh(_K0 * (v + _K1 * (v * v * v)))
    return 0.5 * (1.0 + t) + 0.5 * v * (1.0 - t * t) * (_K0 * (1.0 + 3.0 * _K1 * v * v))


def _row_spec(rows, cols):
    return pl.BlockSpec((rows, cols), lambda i: (i, 0))


def _const_spec(shape):
    nd = len(shape)
    return pl.BlockSpec(shape, lambda i: (0,) * nd, pipeline_mode=pl.Buffered(1))


DILS = tuple(d for _, d in PATTERNS)


def _fill_cols(scr, value):
    for cb in range(value.shape[1] // LANES):
        scr[cb] = value[:, cb * LANES:(cb + 1) * LANES]


def _split_residues(scr, out_ref, dil):
    nb, rows, _ = scr.shape
    for r in range(dil):
        for cb in range(nb):
            piece = scr.at[cb][pl.ds(r, rows // dil, stride=dil), :]
            out_ref[r, :, cb * LANES:(cb + 1) * LANES] = piece.astype(out_ref.dtype)


def _merge_residues(in_ref, scr, dil):
    nb, rows, _ = scr.shape
    for r in range(dil):
        for cb in range(nb):
            scr.at[cb][pl.ds(r, rows // dil, stride=dil), :] = in_ref[r, :, cb * LANES:(cb + 1) * LANES].astype(F32)
    return jnp.concatenate([scr[cb] for cb in range(nb)], axis=-1)


def _col_scratch(rows, width):
    return pltpu.VMEM((width // LANES, rows, LANES), F32)


def _res_spec(dil, rows, width):
    return pl.BlockSpec((dil, rows // dil, width), lambda i: (0, i, 0))


def _res_shape(s, dil, width, dtype):
    return jax.ShapeDtypeStruct((dil, s // dil, width), dtype)


def _norm1(x, g1):
    s = x.shape[0]

    def body(x_ref, g_ref, hn_ref):
        hn, _, _ = _rms_fwd(x_ref[...], g_ref[...])
        hn_ref[...] = hn.astype(BF16)

    return pl.pallas_call(
        body, name="norm1", grid=(s // TM,), in_specs=[_row_spec(TM, D), _const_spec((1, D))],
        out_specs=_row_spec(TM, D), out_shape=jax.ShapeDtypeStruct((s, D), BF16),
        compiler_params=_cparams("arbitrary"),
    )(x, g1)


def _inproj_fwd(hn1, win_t):
    s = hn1.shape[0]
    nd = len(DILS)

    def body(hn_ref, w_ref, *rest):
        qkv_refs = rest[:3 * nd]
        u_ref, z_ref, scr = rest[3 * nd:]
        hn = hn_ref[...]
        for t in range(3):
            seg = _dot_nt(hn, w_ref[t * A:(t + 1) * A, :])
            seg = seg * (SCALE * LOG2E) if t == 0 else seg
            _fill_cols(scr, seg)
            for di, dil in enumerate(DILS):
                if dil == 1:
                    qkv_refs[t * nd + di][0] = seg.astype(BF16)
                else:
                    _split_residues(scr, qkv_refs[t * nd + di], dil)
        u_ref[...] = _dot_nt(hn, w_ref[3 * A:3 * A + GW, :])
        z_ref[...] = _dot_nt(hn, w_ref[3 * A + GW:INW, :])

    res = pl.pallas_call(
        body, name="inproj_fwd", grid=(s // TM_BIG,),
        in_specs=[_row_spec(TM_BIG, D), _const_spec((INW, D))],
        out_specs=[_res_spec(d, TM_BIG, A) for _ in range(3) for d in DILS]
                  + [_row_spec(TM_BIG, GW), _row_spec(TM_BIG, GW)],
        out_shape=[_res_shape(s, d, A, BF16) for _ in range(3) for d in DILS]
                  + [jax.ShapeDtypeStruct((s, GW), F32)] * 2,
        scratch_shapes=[_col_scratch(TM_BIG, A)],
        compiler_params=_cparams("arbitrary"),
    )(hn1, win_t)
    q, k, v = (res[t * nd:(t + 1) * nd] for t in range(3))
    return q, k, v, res[-2], res[-1]


def _att_geometry(length, dil):
    merge = max(1, min(dil, ATT_ROWS // length))
    rows = min(length * merge, ATT_ROWS)
    nsub = rows // CHUNK
    return merge, rows, length * merge // rows, nsub, min(length // CHUNK, nsub)


def _merged(t, merge):
    return t.reshape(t.shape[0] // merge, t.shape[1] * merge, t.shape[2])


def _stack_heads(t):
    lane = lax.broadcasted_iota(jnp.int32, t.shape, 1)
    zero = jnp.zeros_like(t)
    return jnp.concatenate([jnp.where(lane < DH, t, zero), jnp.where(lane >= DH, t, zero)], axis=0)


def _head_cols(t, hp):
    lane = lax.broadcasted_iota(jnp.int32, t.shape, 1)
    cols = [jnp.sum(jnp.where(lane == 2 * hp + h, t, 0.0), axis=-1, keepdims=True) for h in range(2)]
    return jnp.concatenate(cols, axis=0)


def _unstack_heads(t2):
    n = t2.shape[0] // 2
    lane = lax.broadcasted_iota(jnp.int32, (n, LANES), 1)
    return jnp.where(lane < DH, t2[:n], t2[n:])


def _query_window_bias(s0, s1, dil, first):
    row = lax.broadcasted_iota(jnp.int32, (2 * CHUNK, 2 * CHUNK), 0)
    col = lax.broadcasted_iota(jnp.int32, (2 * CHUNK, 2 * CHUNK), 1)
    steps = (row & (CHUNK - 1)) + CHUNK - col
    valid = (steps >= 0) & (steps <= CHUNK)
    if first:
        valid = valid & (col >= CHUNK)
    slope = jnp.where(row < CHUNK, s0, s1)
    return jnp.where(valid, -slope * (steps * dil).astype(F32), NEG)


def _key_block_bias(s0, s1, dil, last):
    key = lax.broadcasted_iota(jnp.int32, (CHUNK, 4 * CHUNK), 0)
    col = lax.broadcasted_iota(jnp.int32, (CHUNK, 4 * CHUNK), 1)
    wq = col & (2 * CHUNK - 1)
    steps = wq - key
    valid = (steps >= 0) & (steps <= CHUNK)
    if last:
        valid = valid & (wq < CHUNK)
    slope = jnp.where(col < 2 * CHUNK, s0, s1)
    return jnp.where(valid, -slope * (steps * dil).astype(F32), NEG)


def _head_rows(t, hp):
    row = lax.broadcasted_iota(jnp.int32, (8, LANES), 0)
    lane = lax.broadcasted_iota(jnp.int32, (8, LANES), 1)
    pick = jnp.where((row < 2) & (lane == 2 * hp + row), 1.0, 0.0).astype(BF16)
    hi = t.astype(BF16)
    rest = t - hi.astype(F32)
    mid = rest.astype(BF16)
    low = (rest - mid.astype(F32)).astype(BF16)
    return _dot_nt(pick, hi) + _dot_nt(pick, mid) + _dot_nt(pick, low)


def _att_specs(dil, rows, nsub, nblk):
    main = pl.BlockSpec((None, rows, LANES), lambda r, c, hp: (r, c, hp))
    prev = pl.BlockSpec((None, CHUNK, LANES), lambda r, c, hp: (r, jnp.maximum(c * nsub - 1, 0), hp))
    nxt = pl.BlockSpec((None, CHUNK, LANES), lambda r, c, hp: (r, jnp.minimum((c + 1) * nsub, nblk - 1), hp))
    main_heads = pl.BlockSpec((None, rows, LANES), lambda r, c, hp: (r, c, 0))
    nxt_heads = pl.BlockSpec((None, CHUNK, LANES), lambda r, c, hp: (r, jnp.minimum((c + 1) * nsub, nblk - 1), 0))
    return main, prev, nxt, main_heads, nxt_heads


def _row_start(i):
    return i * CHUNK if isinstance(i, int) else pl.multiple_of(i * CHUNK, CHUNK)


def _first_blocks(block, nsub, seg, nch, ch, first_bias, bias_buf):
    for i in range(nsub):
        if i % seg:
            block(i, bias_buf[...])
        elif nch == 1:
            block(i, first_bias())
        else:
            block(i, jnp.where(ch == 0, first_bias(), bias_buf[...]))


def _last_blocks(block, nsub, seg, nch, ch, last_bias, bias_buf):
    for i in range(nsub):
        if (i + 1) % seg:
            block(i, bias_buf[...])
        elif nch == 1:
            block(i, last_bias())
        else:
            block(i, jnp.where(ch == nch - 1, last_bias(), bias_buf[...]))


def _attn_fwd(q, k, v, slopes, dil):
    length = q.shape[1]
    merge, rows, nch, nsub, seg = _att_geometry(length, dil)
    main, prev, _, main_heads, _ = _att_specs(dil, rows, nsub, length * merge // CHUNK)
    q, k, v = (_merged(t, merge) for t in (q, k, v))

    def body(sl_ref, q_ref, k_ref, v_ref, kh_ref, vh_ref, o_ref, lse_ref, kbuf, vbuf, bias_buf):
        ch = pl.program_id(1)
        hp = pl.program_id(2)
        lane = lax.broadcasted_iota(jnp.int32, (CHUNK, LANES), 1)
        kbuf[0:CHUNK, :] = kh_ref[...]
        kbuf[CHUNK:, :] = k_ref[...]
        vbuf[0:CHUNK, :] = vh_ref[...]
        vbuf[CHUNK:, :] = v_ref[...]
        s0, s1 = sl_ref[2 * hp], sl_ref[2 * hp + 1]

        def block(i, bias):
            row = _row_start(i)
            rs = pl.ds(row, CHUNK)
            q2 = _stack_heads(q_ref[rs, :])
            kw = kbuf[pl.ds(row, 2 * CHUNK), :]
            vw = vbuf[pl.ds(row, 2 * CHUNK), :]
            sc = _dot_nt(q2, kw) + bias
            m = jnp.max(sc, axis=-1, keepdims=True)
            p = jnp.exp2(sc - m)
            l = jnp.sum(p, axis=-1, keepdims=True)
            o2 = _dot(p.astype(BF16), vw) * (1.0 / l)
            o_ref[rs, :] = _unstack_heads(o2).astype(BF16)
            lse = m + jnp.log2(l)
            seen = jnp.where(hp == 0, 0.0, lse_ref[rs, :])
            lse_ref[rs, :] = jnp.where(lane == 2 * hp, lse[:CHUNK], jnp.where(lane == 2 * hp + 1, lse[CHUNK:], seen))

        bias_buf[...] = _query_window_bias(s0, s1, dil, False)
        _first_blocks(block, nsub, seg, nch, ch, lambda: _query_window_bias(s0, s1, dil, True), bias_buf)

    sd = jax.ShapeDtypeStruct
    o, lse = pl.pallas_call(
        body, name=f"attn_fwd_d{dil}", grid=(dil // merge, nch, NH // 2),
        in_specs=[pl.BlockSpec(memory_space=pltpu.SMEM), main, main, main, prev, prev],
        out_specs=[main, main_heads],
        out_shape=[sd((dil // merge, length * merge, A), BF16), sd((dil // merge, length * merge, LANES), F32)],
        scratch_shapes=[pltpu.VMEM((rows + CHUNK, LANES), BF16), pltpu.VMEM((rows + CHUNK, LANES), BF16),
                        pltpu.VMEM((2 * CHUNK, 2 * CHUNK), F32)],
        compiler_params=_cparams("arbitrary", "arbitrary", "arbitrary"),
    )(slopes, q, k, v, k, v)
    return o.reshape(dil, length, A), lse.reshape(dil, length, LANES)


def _attn_bwd_dq(q, k, v, do, lse, delta, slopes, dil):
    length = q.shape[1]
    merge, rows, nch, nsub, seg = _att_geometry(length, dil)
    main, prev, _, main_heads, _ = _att_specs(dil, rows, nsub, length * merge // CHUNK)
    q, k, v, do, lse, delta = (_merged(t, merge) for t in (q, k, v, do, lse, delta))

    def body(sl_ref, q_ref, k_ref, v_ref, do_ref, lse_ref, dl_ref, kh_ref, vh_ref, dq_ref, kbuf, vbuf, bias_buf):
        ch = pl.program_id(1)
        hp = pl.program_id(2)
        kbuf[0:CHUNK, :] = kh_ref[...]
        kbuf[CHUNK:, :] = k_ref[...]
        vbuf[0:CHUNK, :] = vh_ref[...]
        vbuf[CHUNK:, :] = v_ref[...]
        s0, s1 = sl_ref[2 * hp], sl_ref[2 * hp + 1]

        def block(i, bias):
            row = _row_start(i)
            rs = pl.ds(row, CHUNK)
            q2 = _stack_heads(q_ref[rs, :])
            do2 = _stack_heads(do_ref[rs, :])
            lse2 = _head_cols(lse_ref[rs, :], hp)
            dl2 = _head_cols(dl_ref[rs, :], hp)
            kw = kbuf[pl.ds(row, 2 * CHUNK), :]
            vw = vbuf[pl.ds(row, 2 * CHUNK), :]
            p = jnp.exp2(_dot_nt(q2, kw) + bias - lse2)
            ds = p * (_dot_nt(do2, vw) - dl2)
            dq_ref[rs, :] = _unstack_heads(_dot(ds.astype(BF16), kw)).astype(BF16)

        bias_buf[...] = _query_window_bias(s0, s1, dil, False)
        _first_blocks(block, nsub, seg, nch, ch, lambda: _query_window_bias(s0, s1, dil, True), bias_buf)

    dq = pl.pallas_call(
        body, name=f"attn_dq_d{dil}", grid=(dil // merge, nch, NH // 2),
        in_specs=[pl.BlockSpec(memory_space=pltpu.SMEM), main, main, main, main, main_heads, main_heads, prev, prev],
        out_specs=main, out_shape=jax.ShapeDtypeStruct((dil // merge, length * merge, A), BF16),
        scratch_shapes=[pltpu.VMEM((rows + CHUNK, LANES), BF16), pltpu.VMEM((rows + CHUNK, LANES), BF16),
                        pltpu.VMEM((2 * CHUNK, 2 * CHUNK), F32)],
        compiler_params=_cparams("arbitrary", "arbitrary", "arbitrary"),
    )(slopes, q, k, v, do, lse, delta, k, v)
    return dq.reshape(dil, length, A)


def _attn_bwd_dkv(q, k, v, do, lse, delta, slopes, dil):
    length = q.shape[1]
    merge, rows, nch, nsub, seg = _att_geometry(length, dil)
    main, _, nxt, main_heads, nxt_heads = _att_specs(dil, rows, nsub, length * merge // CHUNK)
    q, k, v, do, lse, delta = (_merged(t, merge) for t in (q, k, v, do, lse, delta))

    def body(sl_ref, k_ref, v_ref, q_ref, do_ref, lse_ref, dl_ref, qh_ref, doh_ref, lseh_ref, dlh_ref,
             dk_ref, dv_ref, qbuf, dobuf, lse_rows, dl_rows, bias_buf):
        ch = pl.program_id(1)
        hp = pl.program_id(2)
        for buf, main_ref, halo_ref in ((qbuf, q_ref, qh_ref), (dobuf, do_ref, doh_ref)):
            buf[0:rows, :] = main_ref[...]
            buf[rows:, :] = halo_ref[...]
        for buf, main_ref, halo_ref in ((lse_rows, lse_ref, lseh_ref), (dl_rows, dl_ref, dlh_ref)):
            buf[:, 0:rows] = _head_rows(main_ref[...], hp)
            buf[:, rows:] = _head_rows(halo_ref[...], hp)
        s0, s1 = sl_ref[2 * hp], sl_ref[2 * hp + 1]

        def block(i, bias):
            row = _row_start(i)
            rs = pl.ds(row, CHUNK)
            win = pl.ds(row, 2 * CHUNK)
            kc = k_ref[rs, :]
            vc = v_ref[rs, :]
            q2 = _stack_heads(qbuf[win, :])
            do2 = _stack_heads(dobuf[win, :])
            cols = slice(i * CHUNK, (i + 2) * CHUNK)
            lse2 = jnp.concatenate([lse_rows[0:1, cols], lse_rows[1:2, cols]], axis=1)
            dl2 = jnp.concatenate([dl_rows[0:1, cols], dl_rows[1:2, cols]], axis=1)
            pt = jnp.exp2(_dot_nt(kc, q2) + bias - lse2)
            dst = pt * (_dot_nt(vc, do2) - dl2)
            dv_ref[rs, :] = _dot(pt.astype(BF16), do2).astype(BF16)
            dk_ref[rs, :] = (_dot(dst.astype(BF16), q2) * LN2).astype(BF16)

        bias_buf[...] = _key_block_bias(s0, s1, dil, False)
        _last_blocks(block, nsub, seg, nch, ch, lambda: _key_block_bias(s0, s1, dil, True), bias_buf)

    sd = jax.ShapeDtypeStruct((dil // merge, length * merge, A), BF16)
    dk, dv = pl.pallas_call(
        body, name=f"attn_dkv_d{dil}", grid=(dil // merge, nch, NH // 2),
        in_specs=[pl.BlockSpec(memory_space=pltpu.SMEM), main, main, main, main, main_heads, main_heads,
                  nxt, nxt, nxt_heads, nxt_heads],
        out_specs=[main, main], out_shape=[sd, sd],
        scratch_shapes=[pltpu.VMEM((rows + CHUNK, LANES), BF16), pltpu.VMEM((rows + CHUNK, LANES), BF16),
                        pltpu.VMEM((8, rows + CHUNK), F32), pltpu.VMEM((8, rows + CHUNK), F32),
                        pltpu.VMEM((CHUNK, 4 * CHUNK), F32)],
        compiler_params=_cparams("arbitrary", "arbitrary", "arbitrary"),
    )(slopes, k, v, q, do, lse, delta, q, do, lse, delta)
    return dk.reshape(dil, length, A), dv.reshape(dil, length, A)


def _group_masks(width):
    lane = lax.broadcasted_iota(jnp.int32, (1, width), 1)
    return [(lane >= g * DH) & (lane < (g + 1) * DH) for g in range(width // DH)]


def _group_mean_matrix():
    i = lax.broadcasted_iota(jnp.int32, (GW, GW), 0) // DH
    j = lax.broadcasted_iota(jnp.int32, (GW, GW), 1) // DH
    return jnp.where(i == j, 1.0 / DH, 0.0).astype(F32)


def _tri_mask(lower):
    t = lax.broadcasted_iota(jnp.int32, (CHUNK, CHUNK), 0)
    u = lax.broadcasted_iota(jnp.int32, (CHUNK, CHUNK), 1)
    return (u <= t) if lower else (u >= t)


def _sgu_forward(u, z, lng, lnb, w_ref, bias_t, pmat, rows):
    ug = _gelu(u)
    zg = _gelu(z)
    mu = _dot_hi(zg, pmat)
    zc = zg - mu
    var = _dot_hi(zc * zc, pmat)
    rstd = lax.rsqrt(var + EPS)
    zhat = zc * rstd
    zn = (zhat * lng + lnb).astype(BF16)
    gm = _group_masks(GW)
    tri = _tri_mask(True)
    ws = [jnp.where(tri, w_ref[g], 0.0).astype(BF16) for g in range(NG)]
    pieces = []
    for c in range(rows // CHUNK):
        znc = zn[c * CHUNK:(c + 1) * CHUNK, :]
        mix = None
        for g in range(NG):
            part = jnp.where(gm[g], _dot(ws[g], znc), 0.0)
            mix = part if mix is None else mix + part
        pieces.append(mix + bias_t)
    mixed = jnp.concatenate(pieces, axis=0) if len(pieces) > 1 else pieces[0]
    return ug * mixed, ug, zhat, rstd, zn, mixed


def _head_spread():
    h = lax.broadcasted_iota(jnp.int32, (LANES, A), 0)
    lane = lax.broadcasted_iota(jnp.int32, (LANES, A), 1)
    return jnp.where(lane // DH == h, 1.0, 0.0).astype(BF16)


def _bf16_pieces(t, n):
    pieces = []
    for _ in range(n):
        piece = t.astype(BF16)
        pieces.append(piece)
        t = t - piece.astype(F32)
    return pieces


def _mix_fwd(os_, ls_, u, z, x, lng, lnb, sgu_w, bias_t, ga, gg, wout):
    s = x.shape[0]
    nd = len(DILS)
    nscr = sum(1 for d in DILS if d > 1)

    def body(*refs):
        o_refs, l_refs = refs[:nd], refs[nd:2 * nd]
        u_ref, z_ref, x_ref, lng_ref, lnb_ref, w_ref, bt_ref, ga_ref, gg_ref, wo_ref = refs[2 * nd:2 * nd + 10]
        attn_ref = refs[2 * nd + 10]
        lse_refs = refs[2 * nd + 11:3 * nd + 11]
        mixed_ref, h1_ref = refs[3 * nd + 11:3 * nd + 13]
        scr = refs[3 * nd + 13:]
        scr_o, scr_l, scr_lse = scr[:nscr], scr[nscr:2 * nscr], scr[2 * nscr]
        ov, lv, j = [], [], 0
        for di, dil in enumerate(DILS):
            if dil == 1:
                ov.append(o_refs[di][0].astype(F32))
                lv.append(l_refs[di][0])
            else:
                ov.append(_merge_residues(o_refs[di], scr_o[j], dil))
                lv.append(_merge_residues(l_refs[di], scr_l[j], dil))
                j += 1
        mx = functools.reduce(jnp.maximum, lv)
        es = [jnp.exp2(l - mx) for l in lv]
        den = functools.reduce(lambda a, b: a + b, es)
        spread = _head_spread()
        attn = None
        for e, o in zip(es, ov):
            wide = functools.reduce(lambda a, b: a + b, [_dot(piece, spread) for piece in _bf16_pieces(e / den, 2)])
            attn = wide * o if attn is None else attn + wide * o
        attn_ref[...] = attn
        lse = mx + jnp.log2(den)
        _fill_cols(scr_lse, lse)
        for di, dil in enumerate(DILS):
            if dil == 1:
                lse_refs[di][0] = lse
            else:
                _split_residues(scr_lse, lse_refs[di], dil)
        an, _, _ = _rms_fwd(attn, ga_ref[...])
        gmv, _, _, _, _, _ = _sgu_forward(u_ref[...], z_ref[...], lng_ref[...], lnb_ref[...], w_ref,
                                          bt_ref[...], _group_mean_matrix(), TMX)
        gn, _, _ = _rms_fwd(gmv, gg_ref[...])
        mixed = jnp.concatenate([an, gn], axis=-1).astype(BF16)
        mixed_ref[...] = mixed
        h1_ref[...] = x_ref[...] + _dot(mixed, wo_ref[...])

    sd = jax.ShapeDtypeStruct
    res = pl.pallas_call(
        body, name="mix_fwd", grid=(s // TMX,),
        in_specs=[_res_spec(d, TMX, A) for d in DILS] + [_res_spec(d, TMX, LANES) for d in DILS]
                 + [_row_spec(TMX, GW), _row_spec(TMX, GW),
                    _row_spec(TMX, D), _const_spec((1, GW)), _const_spec((1, GW)), _const_spec((NG, CHUNK, CHUNK)),
                    _const_spec((CHUNK, GW)), _const_spec((1, A)), _const_spec((1, GW)), _const_spec((D, D))],
        out_specs=[_row_spec(TMX, A)] + [_res_spec(d, TMX, LANES) for d in DILS]
                  + [_row_spec(TMX, D), _row_spec(TMX, D)],
        out_shape=[sd((s, A), F32)] + [_res_shape(s, d, LANES, F32) for d in DILS]
                  + [sd((s, D), BF16), sd((s, D), F32)],
        scratch_shapes=[_col_scratch(TMX, A)] * nscr + [_col_scratch(TMX, LANES)] * (nscr + 1),
        compiler_params=_cparams("arbitrary"),
    )(*os_, *ls_, u, z, x, lng, lnb, sgu_w, bias_t, ga, gg, wout)
    return res[0], res[1:1 + nd], res[1 + nd], res[2 + nd]


def _mlp_fwd(h1, g2, wff1, wff2, gf, target):
    s = h1.shape[0]

    def body(h1_ref, g2_ref, w1_ref, w2_ref, gf_ref, t_ref, hn_ref, rf_ref, dh2_ref, loss_ref, dgf_ref):
        i = pl.program_id(0)
        h1v = h1_ref[...]
        hn, _, _ = _rms_fwd(h1v, g2_ref[...])
        hn = hn.astype(BF16)
        hn_ref[...] = hn
        acc = h1v
        for j in range(DFF // FF_CH):
            cols = slice(j * FF_CH, (j + 1) * FF_CH)
            rf = jnp.maximum(_dot(hn, w1_ref[j]), 0.0)
            act = (rf * rf).astype(BF16)
            rf_ref[:, cols] = rf.astype(BF16)
            acc = acc + _dot(act, w2_ref[cols, :])
        y, h2n, r3 = _rms_fwd(acc, gf_ref[...])
        err = y - t_ref[...]
        part = 0.5 * jnp.sum(jnp.mean(err * err, axis=-1, keepdims=True), axis=0, keepdims=True)
        dy = err * (1.0 / D)
        dh2, dgf = _rms_bwd(dy, h2n, r3, gf_ref[...])
        dh2_ref[...] = dh2

        @pl.when(i == 0)
        def _():
            loss_ref[...] = jnp.zeros_like(loss_ref)
            dgf_ref[...] = jnp.zeros_like(dgf_ref)

        loss_ref[...] += jnp.broadcast_to(part, loss_ref.shape)
        dgf_ref[...] += dgf

    sd = jax.ShapeDtypeStruct
    return pl.pallas_call(
        body, name="mlp_fwd", grid=(s // TM,),
        in_specs=[_row_spec(TM, D), _const_spec((1, D)), _const_spec((DFF // FF_CH, D, FF_CH)), _const_spec((DFF, D)),
                  _const_spec((1, D)), _row_spec(TM, D)],
        out_specs=[_row_spec(TM, D), _row_spec(TM, DFF), _row_spec(TM, D),
                   _const_spec((1, LANES)), _const_spec((1, D))],
        out_shape=[sd((s, D), BF16), sd((s, DFF), BF16), sd((s, D), F32),
                   sd((1, LANES), F32), sd((1, D), F32)],
        compiler_params=_cparams("arbitrary"),
    )(h1, g2, wff1, wff2, gf, target)


def _mlp_bwd(dh2, rf, h1, g2, wff1, wff2):
    s = h1.shape[0]

    def body(dh2_ref, rf_ref, h1_ref, g2_ref, w1_ref, w2_ref, df_ref, dh1_ref, dg2_ref):
        i = pl.program_id(0)
        dh2v = dh2_ref[...]
        dh2b = dh2v.astype(BF16)
        dhn = jnp.zeros((TM, D), F32)
        for j in range(DFF // FF_CH):
            cols = slice(j * FF_CH, (j + 1) * FF_CH)
            da = _dot_nt(dh2b, w2_ref[cols, :])
            df = (da * (2.0 * rf_ref[:, cols].astype(F32))).astype(BF16)
            df_ref[:, cols] = df
            dhn = dhn + _dot_nt(df, w1_ref[j])
        _, h1n, r2 = _rms_fwd(h1_ref[...], g2_ref[...])
        dres, dg2 = _rms_bwd(dhn, h1n, r2, g2_ref[...])
        dh1_ref[...] = dh2v + dres

        @pl.when(i == 0)
        def _():
            dg2_ref[...] = jnp.zeros_like(dg2_ref)

        dg2_ref[...] += dg2

    sd = jax.ShapeDtypeStruct
    return pl.pallas_call(
        body, name="mlp_bwd", grid=(s // TM,),
        in_specs=[_row_spec(TM, D), _row_spec(TM, DFF), _row_spec(TM, D), _const_spec((1, D)),
                  _const_spec((DFF // FF_CH, D, FF_CH)), _const_spec((DFF, D))],
        out_specs=[_row_spec(TM, DFF), _row_spec(TM, D), _const_spec((1, D))],
        out_shape=[sd((s, DFF), BF16), sd((s, D), F32), sd((1, D), F32)],
        compiler_params=_cparams("arbitrary"),
    )(dh2, rf, h1, g2, wff1, wff2)


def _mix_bwd(dh1, attn, u, z, lng, lnb, sgu_w, sgu_wt, bias_t, ga, gg, wout):
    s = dh1.shape[0]
    nsteps = s // TMX
    nd = len(DILS)

    def body(*refs):
        dh1_ref, attn_ref, u_ref, z_ref, lng_ref, lnb_ref, w_ref, wt_ref, bt_ref, ga_ref, gg_ref, wo_ref = refs[:12]
        do_refs, dl_refs = refs[12:12 + nd], refs[12 + nd:12 + 2 * nd]
        (du_ref, dz_ref, dga_ref, dgg_ref, dlng_ref, dlnb_ref, dws_ref, db_ref,
         dbt_acc, scr_do, scr_dl) = refs[12 + 2 * nd:]
        i = pl.program_id(0)

        @pl.when(i == 0)
        def _():
            for r in (dga_ref, dgg_ref, dlng_ref, dlnb_ref, dws_ref, db_ref, dbt_acc):
                r[...] = jnp.zeros_like(r)

        dmixed = _dot_nt(dh1_ref[...].astype(BF16), wo_ref[...])
        attn = attn_ref[...]
        _, an, ra = _rms_fwd(attn, ga_ref[...])
        dattn, dga = _rms_bwd(dmixed[:, :A], an, ra, ga_ref[...])
        dga_ref[...] += dga
        _fill_cols(scr_do, dattn)
        spread = _head_spread()
        delta = functools.reduce(lambda a, b: a + b, [_dot_nt(piece, spread) for piece in _bf16_pieces(dattn * attn, 3)])
        _fill_cols(scr_dl, delta)
        for di, dil in enumerate(DILS):
            if dil == 1:
                do_refs[di][0] = dattn.astype(BF16)
                dl_refs[di][0] = delta
            else:
                _split_residues(scr_do, do_refs[di], dil)
                _split_residues(scr_dl, dl_refs[di], dil)
        pmat = _group_mean_matrix()
        lng = lng_ref[...]
        uv, zv = u_ref[...], z_ref[...]
        gmv, ug, zhat, rstd, zn, mixed = _sgu_forward(uv, zv, lng, lnb_ref[...], w_ref, bt_ref[...], pmat, TMX)
        _, gmn, rg = _rms_fwd(gmv, gg_ref[...])
        dgm, dgg = _rms_bwd(dmixed[:, A:], gmn, rg, gg_ref[...])
        dgg_ref[...] += dgg
        du_ref[...] = (dgm * mixed * _gelu_grad(uv)).astype(BF16)
        dmx = dgm * ug
        dmxb = dmx.astype(BF16)
        gm = _group_masks(GW)
        tri_t = _tri_mask(False)
        wst = [jnp.where(tri_t, wt_ref[g], 0.0).astype(BF16) for g in range(NG)]
        zero = jnp.zeros((CHUNK, GW), BF16)
        dzn_pieces = []
        for c in range(TMX // CHUNK):
            rs = slice(c * CHUNK, (c + 1) * CHUNK)
            dmc = dmxb[rs, :]
            znc = zn[rs, :]
            dbt_acc[...] += dmx[rs, :]
            dzn = None
            for g in range(NG):
                dws_ref[g] += _dot_nt(jnp.where(gm[g], dmc, zero), znc)
                part = jnp.where(gm[g], _dot(wst[g], dmc), 0.0)
                dzn = part if dzn is None else dzn + part
            dzn_pieces.append(dzn)
        dzn = jnp.concatenate(dzn_pieces, axis=0)
        dlng_ref[...] += jnp.sum(dzn * zhat, axis=0, keepdims=True)
        dlnb_ref[...] += jnp.sum(dzn, axis=0, keepdims=True)
        dzh = dzn * lng
        dzg = rstd * (dzh - _dot_hi(dzh, pmat) - zhat * _dot_hi(dzh * zhat, pmat))
        dz_ref[...] = (dzg * _gelu_grad(zv)).astype(BF16)

        @pl.when(i == nsteps - 1)
        def _():
            tri = _tri_mask(True)
            for g in range(NG):
                dws_ref[g] = jnp.where(tri, dws_ref[g], 0.0)
            acc = dbt_acc[...]
            lane = lax.broadcasted_iota(jnp.int32, (CHUNK, LANES), 1)
            out = jnp.zeros((CHUNK, LANES), F32)
            for g in range(NG):
                sg = jnp.sum(jnp.where(gm[g], acc, 0.0), axis=-1, keepdims=True)
                out = jnp.where(lane == g, sg, out)
            db_ref[...] = out

    sd = jax.ShapeDtypeStruct
    res = pl.pallas_call(
        body, name="mix_bwd", grid=(nsteps,),
        in_specs=[_row_spec(TMX, D), _row_spec(TMX, A), _row_spec(TMX, GW), _row_spec(TMX, GW),
                  _const_spec((1, GW)), _const_spec((1, GW)), _const_spec((NG, CHUNK, CHUNK)),
                  _const_spec((NG, CHUNK, CHUNK)), _const_spec((CHUNK, GW)), _const_spec((1, A)),
                  _const_spec((1, GW)), _const_spec((D, D))],
        out_specs=[_res_spec(d, TMX, A) for d in DILS] + [_res_spec(d, TMX, LANES) for d in DILS]
                  + [_row_spec(TMX, GW), _row_spec(TMX, GW),
                   _const_spec((1, A)), _const_spec((1, GW)), _const_spec((1, GW)), _const_spec((1, GW)),
                   _const_spec((NG, CHUNK, CHUNK)), _const_spec((CHUNK, LANES))],
        out_shape=[_res_shape(s, d, A, BF16) for d in DILS] + [_res_shape(s, d, LANES, F32) for d in DILS]
                  + [sd((s, GW), BF16), sd((s, GW), BF16),
                   sd((1, A), F32), sd((1, GW), F32), sd((1, GW), F32), sd((1, GW), F32),
                   sd((NG, CHUNK, CHUNK), F32), sd((CHUNK, LANES), F32)],
        scratch_shapes=[pltpu.VMEM((CHUNK, GW), F32), _col_scratch(TMX, A), _col_scratch(TMX, LANES)],
        compiler_params=_cparams("arbitrary"),
    )(dh1, attn, u, z, lng, lnb, sgu_w, sgu_wt, bias_t, ga, gg, wout)
    return (res[:nd], res[nd:2 * nd]) + tuple(res[2 * nd:])


def _dproj_merge(dqs, dks, dvs, du, dz, pin):
    s = du.shape[0]
    nd = len(DILS)
    nscr = sum(1 for d in DILS if d > 1)

    def body(*refs):
        pin_ref = refs[0]
        parts = [refs[1 + t * nd:1 + (t + 1) * nd] for t in range(3)]
        du_ref, dz_ref, dp_ref = refs[1 + 3 * nd:4 + 3 * nd]
        scr = refs[4 + 3 * nd:]
        sums = []
        for t in range(3):
            total, j = None, 0
            for di, dil in enumerate(DILS):
                if dil == 1:
                    term = parts[t][di][0].astype(F32)
                else:
                    term = _merge_residues(parts[t][di], scr[t * nscr + j], dil)
                    j += 1
                total = term if total is None else total + term
            sums.append(total)
        dp_ref[...] = jnp.concatenate([sums[0] * SCALE, sums[1], sums[2], du_ref[...].astype(F32) + pin_ref[0, 0],
                                       dz_ref[...].astype(F32)], axis=-1).astype(BF16)

    return pl.pallas_call(
        body, name="dproj_merge", grid=(s // TMX,),
        in_specs=[pl.BlockSpec(memory_space=pltpu.SMEM)] + [_res_spec(d, TMX, A) for d in DILS] * 3
                 + [_row_spec(TMX, GW)] * 2,
        out_specs=_row_spec(TMX, INW), out_shape=jax.ShapeDtypeStruct((s, INW), BF16),
        scratch_shapes=[_col_scratch(TMX, A)] * (3 * nscr),
        compiler_params=_cparams("arbitrary"),
    )(pin, *dqs, *dks, *dvs, du, dz)


def _inproj_bwd(dproj, dh1, x, g1, win_t):
    s = x.shape[0]

    def body(dp_ref, dh1_ref, x_ref, g_ref, w_ref, dx_ref, dg_ref):
        i = pl.program_id(0)
        dhn = _dot(dp_ref[...], w_ref[...])
        _, xn, r1 = _rms_fwd(x_ref[...], g_ref[...])
        dres, dg = _rms_bwd(dhn, xn, r1, g_ref[...])
        dx_ref[...] = dh1_ref[...] + dres

        @pl.when(i == 0)
        def _():
            dg_ref[...] = jnp.zeros_like(dg_ref)

        dg_ref[...] += dg

    sd = jax.ShapeDtypeStruct
    return pl.pallas_call(
        body, name="inproj_bwd", grid=(s // TM_BIG,),
        in_specs=[_row_spec(TM_BIG, INW), _row_spec(TM_BIG, D), _row_spec(TM_BIG, D), _const_spec((1, D)), _const_spec((INW, D))],
        out_specs=[_row_spec(TM_BIG, D), _const_spec((1, D))],
        out_shape=[sd((s, D), F32), sd((1, D), F32)],
        compiler_params=_cparams("arbitrary"),
    )(dproj, dh1, x, g1, win_t)


def _wgrad(a, b, name, bm, bn, bk=4 * TM, square_a=False, also_bf16=False):
    s, m = a.shape
    n = b.shape[1]
    bm, bn = min(bm, m), min(bn, n)
    nk = s // bk

    def body(a_ref, b_ref, o_ref, *low):
        @pl.when(pl.program_id(2) == 0)
        def _():
            o_ref[...] = jnp.zeros_like(o_ref)

        av = a_ref[...]
        if square_a:
            av = av.astype(F32)
            av = av * av
        o_ref[...] += _dot_tn(av.astype(BF16), b_ref[...].astype(BF16))
        if also_bf16:
            @pl.when(pl.program_id(2) == nk - 1)
            def _():
                low[0][...] = o_ref[...].astype(BF16)

    out_spec = pl.BlockSpec((bm, bn), lambda i, j, k: (i, j))
    res = pl.pallas_call(
        body, name=name, grid=(m // bm, n // bn, nk),
        in_specs=[pl.BlockSpec((bk, bm), lambda i, j, k: (k, i)), pl.BlockSpec((bk, bn), lambda i, j, k: (k, j))],
        out_specs=[out_spec, out_spec] if also_bf16 else out_spec,
        out_shape=([jax.ShapeDtypeStruct((m, n), F32), jax.ShapeDtypeStruct((m, n), BF16)] if also_bf16
                   else jax.ShapeDtypeStruct((m, n), F32)),
        compiler_params=_cparams("arbitrary", "arbitrary", "arbitrary"),
    )(a, b)
    return res


def _adamw_math(w, g, m, v):
    m = B1 * m + (1.0 - B1) * g
    v = B2 * v + (1.0 - B2) * (g * g)
    m_hat = m / (1.0 - B1 ** STEP)
    v_hat = v / (1.0 - B2 ** STEP)
    delta = -LR * (m_hat / (jnp.sqrt(v_hat) + AEPS) + WD * w)
    return delta, m, v


def _adamw(w, g, m, v, name):
    rows, cols = w.shape
    br = min(rows, 256)
    while rows % br:
        br -= 8

    def body(w_ref, g_ref, m_ref, v_ref, d_ref, mo_ref, vo_ref):
        d, mn, vn = _adamw_math(w_ref[...], g_ref[...], m_ref[...], v_ref[...])
        d_ref[...] = d
        mo_ref[...] = mn
        vo_ref[...] = vn

    spec = _row_spec(br, cols)
    sd = jax.ShapeDtypeStruct((rows, cols), F32)
    return pl.pallas_call(
        body, name=name, grid=(rows // br,), in_specs=[spec] * 4, out_specs=[spec] * 3,
        out_shape=[sd, sd, sd], compiler_params=_cparams("arbitrary"),
    )(w, g, m, v)


def _local_step(x, hn1, target, small, win_t, rest_weights, early_grads=None, after_attention_bwd=None,
                late_grads=None):
    slopes = jnp.asarray(_alibi_slopes(NH) * np.float32(LOG2E))
    q, k, v, u, z = _inproj_fwd(hn1, win_t)
    outs, lses = [], []
    for i, dil in enumerate(DILS):
        o, l = _attn_fwd(q[i], k[i], v[i], slopes, dil)
        outs.append(o)
        lses.append(l)
    wout, wff1, wff2 = rest_weights(functools.reduce(lambda a, b: a + b, [l[0, 0:8, :] for l in lses]))
    attn, lse, mixed, h1 = _mix_fwd(outs, lses, u, z, x, small["ln_g"], small["ln_b"], small["sgu_w"],
                                    small["bias_t"], small["attn_out_g"], small["gmlp_out_g"], wout)
    hn2, rf, dh2, loss, dgf = _mlp_fwd(h1, small["norm2_g"], wff1, wff2, small["final_norm_g"], target)
    df, dh1, dg2 = _mlp_bwd(dh2, rf, h1, small["norm2_g"], wff1, wff2)
    gwff1 = _wgrad(hn2, df, "wgrad_ff1", D, 1024)
    gwff2 = _wgrad(rf, dh2, "wgrad_ff2", 1024, D, square_a=True)
    gwout = _wgrad(mixed, dh1, "wgrad_out", D, D)
    ga, g1 = small["attn_out_g"], small["norm1_g"]
    pin = early_grads(gwff1, gwff2, gwout) if early_grads else None
    if pin is not None:
        ga = ga + pin
    (do, delta, du, dz, dga, dgg, dlng, dlnb, dws, db) = _mix_bwd(
        dh1, attn, u, z, small["ln_g"], small["ln_b"], small["sgu_w"], small["sgu_wt"], small["bias_t"],
        ga, small["gmlp_out_g"], wout)
    dqs, dks, dvs = [], [], []
    for i, dil in enumerate(DILS):
        dqs.append(_attn_bwd_dq(q[i], k[i], v[i], do[i], lse[i], delta[i], slopes, dil))
        dk, dv = _attn_bwd_dkv(q[i], k[i], v[i], do[i], lse[i], delta[i], slopes, dil)
        dks.append(dk)
        dvs.append(dv)
    marker = functools.reduce(lambda a, b: a + b, [t[0, 0:8, 0:LANES] for t in dqs + dks + dvs])
    partial = dict(ln_g=dlng, ln_b=dlnb, sgu_w=dws, sgu_b=db[:, :NG].T, attn_out_g=dga, gmlp_out_g=dgg,
                   norm2_g=dg2, final_norm_g=dgf)
    pin = after_attention_bwd(marker, partial, loss[0, 0]) if after_attention_bwd else None
    dproj = _dproj_merge(dqs, dks, dvs, du, dz, jnp.zeros((1, 1), F32) if pin is None else pin)
    gwin_t, gwin_low = _wgrad(dproj, hn1, "wgrad_in", INW // 2, D, also_bf16=True)
    pin = late_grads(gwin_t, gwin_low) if late_grads else None
    if pin is not None:
        g1 = g1 + pin
    dx, dg1 = _inproj_bwd(dproj, dh1, x, g1, win_t)
    small_grads = dict(partial, norm1_g=dg1)
    return loss[0, 0], dx, small_grads, (gwin_t, gwout, gwff1, gwff2)


ANY = pl.BlockSpec(memory_space=pl.ANY)
NDEV = 8


def _position():
    return lax.axis_index("x"), lax.axis_index("y"), lax.axis_index("c")


def _other_chips(x, y):
    return [(1 - x, y), (x, 1 - y), (1 - x, 1 - y)]


def _remote(src, dst, send_sem, recv_sem, device):
    return pltpu.make_async_remote_copy(src_ref=src, dst_ref=dst, send_sem=send_sem, recv_sem=recv_sem,
                                        device_id=device, device_id_type=MESH)


HBM = pl.BlockSpec(memory_space=pltpu.HBM)
SEM = pl.BlockSpec(memory_space=pltpu.SEMAPHORE)
DATAFLOW = pltpu.SideEffectType.DATAFLOW_SIDE_EFFECTING


def _in_hbm(a):
    return pltpu.with_memory_space_constraint(a, pltpu.HBM)


def _gather_start(shards, name):
    n = len(shards)
    lands = [jnp.broadcast_to(sh[None], (NCHIP,) + sh.shape) for sh in shards]

    def body(*refs):
        w_refs, land_refs = refs[:n], refs[n:2 * n]
        send_sems, recv_sems = refs[2 * n:2 * n + 2]
        token = refs[-1]
        x, y, c = _position()
        for w in range(n):
            for k, (px, py) in enumerate(_other_chips(x, y)):
                m = 3 * w + k
                _remote(w_refs[w], land_refs[w].at[2 * x + y], send_sems.at[m], recv_sems.at[m], (px, py, c)).start()
        token[...] = jnp.zeros_like(token)

    res = _split_call(body, name, list(shards) + lands, (3 * n, 3 * n), (TOKEN,))
    return res[0], res[1], res[2:2 + n], res[2 + n:2 + 2 * n], res[-1]


def _gather_wait(send_sems, recv_sems, shards, lands, after, name):
    n = len(shards)

    def body(*refs):
        w_refs, land_refs = refs[:n], refs[n:2 * n]
        send_sems, recv_sems = refs[2 * n:2 * n + 2]
        x, y, c = _position()
        for w in range(n):
            for k, (px, py) in enumerate(_other_chips(x, y)):
                m = 3 * w + k
                cp = _remote(w_refs[w], land_refs[w].at[2 * px + py], send_sems.at[m], recv_sems.at[m], (px, py, c))
                cp.wait_send()
                cp.wait_recv()

    operands = list(shards) + list(lands)
    res = pl.pallas_call(
        body, name=name, out_shape=tuple(pltpu.HBM(a.shape, a.dtype) for a in operands),
        in_specs=(HBM,) * (2 * n) + (SEM, SEM, ANY), out_specs=(HBM,) * (2 * n),
        input_output_aliases={i: i for i in range(2 * n)},
        compiler_params=pltpu.CompilerParams(has_side_effects=DATAFLOW),
    )(*operands, send_sems, recv_sems, after)
    return res[n:]


def _xor_peers(x, y, c):
    peers = []
    for k in range(1, NDEV):
        kx, ky, kc = (k >> 2) & 1, (k >> 1) & 1, k & 1
        peers.append((1 - x if kx else x, 1 - y if ky else y, 1 - c if kc else c))
    return peers


def _piece(part_ref, px, py, pc):
    slab = 2 * px + py
    if len(part_ref.shape) == 3:
        half = part_ref.shape[1] // 2
        return part_ref.at[slab, pl.ds(pc * half, half), :]
    half = part_ref.shape[0] // 2
    return part_ref.at[pl.ds(pc * half, half), pl.ds(pl.multiple_of(slab * D, D), D)]


def _split_call(body, name, operands, n_sems, extra_out=()):
    n = len(operands)
    sems = tuple(pltpu.SemaphoreType.DMA((m,)) for m in n_sems)
    thru = tuple(pltpu.HBM(a.shape, a.dtype) for a in operands)
    return pl.pallas_call(
        body, name=name, out_shape=sems + thru + tuple(extra_out),
        in_specs=(HBM,) * n,
        out_specs=(SEM,) * len(sems) + (HBM,) * n + (pl.BlockSpec(memory_space=pltpu.VMEM),) * len(extra_out),
        input_output_aliases={i: len(sems) + i for i in range(n)},
        compiler_params=pltpu.CompilerParams(has_side_effects=DATAFLOW),
    )(*[_in_hbm(a) for a in operands])


TOKEN = jax.ShapeDtypeStruct((8, LANES), F32)


def _pack_copies(pack_ref, land_ref, send_sems, recv_sems, base, position, start):
    x, y, c = position
    for k, (px, py, pc) in enumerate(_xor_peers(x, y, c)):
        if start:
            _remote(pack_ref, land_ref.at[4 * x + 2 * y + c], send_sems.at[base + k], recv_sems.at[base + k],
                    (px, py, pc)).start()
        else:
            cp = _remote(pack_ref, land_ref.at[4 * px + 2 * py + pc], send_sems.at[base + k], recv_sems.at[base + k],
                         (px, py, pc))
            cp.wait_send()
            cp.wait_recv()


def _pack_landing(pack):
    return jnp.broadcast_to(pack[None], (NDEV,) + pack.shape)


def _reduce_start(parts, name, pack=None):
    nw = len(parts)
    lands = [lax.empty((NDEV - 1, p.shape[-2] // 2, D), p.dtype) for p in parts]
    operands = list(parts) + lands + ([pack, _pack_landing(pack)] if pack is not None else [])
    nops = len(operands)

    def body(*refs):
        part_refs, land_refs = refs[:nw], refs[nw:2 * nw]
        send_sems, recv_sems = refs[nops:nops + 2]
        token = refs[-1]
        x, y, c = _position()
        for w in range(nw):
            for k, peer in enumerate(_xor_peers(x, y, c)):
                n = w * (NDEV - 1) + k
                _remote(_piece(part_refs[w], *peer), land_refs[w].at[k], send_sems.at[n], recv_sems.at[n],
                        peer).start()
        if pack is not None:
            _pack_copies(refs[2 * nw], refs[2 * nw + 1], send_sems, recv_sems, nw * (NDEV - 1), (x, y, c), True)
        token[...] = jnp.zeros_like(token)

    n = (nw + (pack is not None)) * (NDEV - 1)
    res = _split_call(body, name, operands, (n, n), (TOKEN,))
    return res[0], res[1], res[2:2 + nops], res[-1]


def _reduce_wait(send_sems, recv_sems, operands, nw, after, name):
    nops = len(operands)
    has_pack = nops > 2 * nw

    def body(*refs):
        part_refs, land_refs = refs[:nw], refs[nw:2 * nw]
        send_sems, recv_sems = refs[nops:nops + 2]
        x, y, c = _position()
        for w in range(nw):
            for k, peer in enumerate(_xor_peers(x, y, c)):
                n = w * (NDEV - 1) + k
                cp = _remote(_piece(part_refs[w], *peer), land_refs[w].at[k], send_sems.at[n], recv_sems.at[n], peer)
                cp.wait_send()
                cp.wait_recv()
        if has_pack:
            _pack_copies(refs[2 * nw], refs[2 * nw + 1], send_sems, recv_sems, nw * (NDEV - 1), (x, y, c), False)

    res = pl.pallas_call(
        body, name=name, out_shape=tuple(pltpu.HBM(a.shape, a.dtype) for a in operands),
        in_specs=(HBM,) * nops + (SEM, SEM, ANY), out_specs=(HBM,) * nops,
        input_output_aliases={i: i for i in range(nops)},
        compiler_params=pltpu.CompilerParams(has_side_effects=DATAFLOW),
    )(*operands, send_sems, recv_sems, after)
    return res[:nw], res[nw:2 * nw], (res[2 * nw + 1] if has_pack else None)


def _sum_pieces(part, land, sel, name):
    half = part.shape[-2] // 2
    br = 128 if half % 128 == 0 else half // 2
    nb = half // br

    def body(sel_ref, own_ref, *refs):
        acc = own_ref[...]
        for r in refs[:NDEV - 1]:
            acc = acc + r[...].astype(F32)
        refs[NDEV - 1][...] = acc

    if part.ndim == 3:
        own_spec = pl.BlockSpec((None, br, D), lambda i, sel_ref: (sel_ref[0], sel_ref[1] * nb + i, 0))
    else:
        own_spec = pl.BlockSpec((br, D), lambda i, sel_ref: (sel_ref[1] * nb + i, sel_ref[0]))
    slot_specs = [pl.BlockSpec((None, br, D), functools.partial(lambda i, sel_ref, k: (k, i, 0), k=k))
                  for k in range(NDEV - 1)]
    return pl.pallas_call(
        body, name=name,
        grid_spec=pltpu.PrefetchScalarGridSpec(
            num_scalar_prefetch=1, grid=(nb,), in_specs=[own_spec] + slot_specs,
            out_specs=pl.BlockSpec((br, D), lambda i, sel_ref: (i, 0))),
        out_shape=jax.ShapeDtypeStruct((half, D), F32),
        compiler_params=_cparams("arbitrary"),
    )(sel, part, *([land] * (NDEV - 1)))


def _share_start(halves, name, pack=None):
    nw = len(halves)
    lands = [lax.empty(h.shape, F32) for h in halves]
    operands = list(halves) + lands + ([pack, _pack_landing(pack)] if pack is not None else [])
    nops = len(operands)

    def body(*refs):
        h_refs, land_refs = refs[:nw], refs[nw:2 * nw]
        send_sems, recv_sems = refs[nops:nops + 2]
        token = refs[-1]
        x, y, c = _position()
        for w in range(nw):
            _remote(h_refs[w], land_refs[w], send_sems.at[w], recv_sems.at[w], (x, y, 1 - c)).start()
        if pack is not None:
            _pack_copies(refs[2 * nw], refs[2 * nw + 1], send_sems, recv_sems, nw, (x, y, c), True)
        token[...] = jnp.zeros_like(token)

    n = nw + (NDEV - 1 if pack is not None else 0)
    res = _split_call(body, name, operands, (n, n), (TOKEN,))
    return res[0], res[1], res[2:2 + nops], res[-1]


def _share_wait(send_sems, recv_sems, operands, nw, after, name):
    nops = len(operands)
    has_pack = nops > 2 * nw

    def body(*refs):
        h_refs, land_refs = refs[:nw], refs[nw:2 * nw]
        send_sems, recv_sems = refs[nops:nops + 2]
        x, y, c = _position()
        for w in range(nw):
            cp = _remote(h_refs[w], land_refs[w], send_sems.at[w], recv_sems.at[w], (x, y, 1 - c))
            cp.wait_send()
            cp.wait_recv()
        if has_pack:
            _pack_copies(refs[2 * nw], refs[2 * nw + 1], send_sems, recv_sems, nw, (x, y, c), False)

    res = pl.pallas_call(
        body, name=name, out_shape=tuple(pltpu.HBM(a.shape, a.dtype) for a in operands),
        in_specs=(HBM,) * nops + (SEM, SEM, ANY), out_specs=(HBM,) * nops,
        input_output_aliases={i: i for i in range(nops)},
        compiler_params=pltpu.CompilerParams(has_side_effects=DATAFLOW),
    )(*operands, send_sems, recv_sems, after)
    return res[:nw], res[nw:2 * nw], (res[2 * nw + 1] if has_pack else None)


def _join_halves(own, other, c):
    first = jnp.where(c == 0, own, other)
    second = jnp.where(c == 0, other, own)
    return jnp.concatenate([first, second], axis=0)


SMALL_SIZES = (("norm1_g", D), ("sgu_w", NG * CHUNK * CHUNK), ("norm2_g", D), ("final_norm_g", D),
               ("sgu_b", NG * CHUNK), ("attn_out_g", A), ("sgu_ln_g", GW), ("sgu_ln_b", GW), ("gmlp_out_g", GW))
PARAM_ROWS = sum(n for _, n in SMALL_SIZES) // LANES
SMALL_ROWS = PARAM_ROWS + 8


def _pack_small(tree, first_extra=None):
    extra = jnp.zeros((8 * LANES,), F32)
    if first_extra is not None:
        extra = extra.at[0].set(first_extra)
    flat = jnp.concatenate([tree[n].reshape(-1) for n, _ in SMALL_SIZES] + [extra])
    return flat.reshape(SMALL_ROWS, LANES)


def _written_shape(shape):
    core = tuple(shape)
    while len(core) > 2 and core[0] == 1:
        core = core[1:]
    if len(core) >= 2 and (core[-1] == LANES or (core[-1] % LANES == 0 and math.prod(core[:-1]) == 1)):
        return core
    return (math.prod(shape) // LANES, LANES)


def _store_small(out_ref, pack_ref, row, shape):
    if len(shape) == 3:
        for g in range(shape[0]):
            out_ref[g] = pack_ref[row + g * shape[1]:row + (g + 1) * shape[1], :]
    elif shape[-1] == LANES:
        out_ref[...] = pack_ref[row:row + shape[0], :]
    else:
        for j in range(shape[1] // LANES):
            out_ref[:, j * LANES:(j + 1) * LANES] = pack_ref[row + j:row + j + 1, :]


def _small_finish(pack_land, norm_land, wpack, mpack, vpack, shapes):
    written = [_written_shape(shapes[n]) for n, _ in SMALL_SIZES]
    nsmall = len(SMALL_SIZES)

    def body(*refs):
        p_ref, n_ref, w_ref, m_ref, v_ref, loss_ref = refs[:6]
        outs = refs[6:6 + 4 * nsmall]
        packs = refs[6 + 4 * nsmall:]
        go_ref = packs[0]
        total = p_ref[0]
        late = n_ref[0]
        for k in range(1, NDEV):
            total = total + p_ref[k]
            late = late + n_ref[k]
        go_ref[...] = total
        go_ref[0:8, :] = total[0:8, :] + late
        d, mn, vn = _adamw_math(w_ref[...], go_ref[...], m_ref[...], v_ref[...])
        packs[1][...] = d
        packs[2][...] = mn
        packs[3][...] = vn
        loss_ref[...] = go_ref[PARAM_ROWS:SMALL_ROWS, :]
        for kind in range(4):
            row = 0
            for j, (_, size) in enumerate(SMALL_SIZES):
                _store_small(outs[kind * nsmall + j], packs[kind], row, written[j])
                row += size // LANES

    vm = pl.BlockSpec(memory_space=pltpu.VMEM)
    out_shape = [jax.ShapeDtypeStruct((8, LANES), F32)] + [jax.ShapeDtypeStruct(w, F32) for w in written] * 4
    outs = pl.pallas_call(
        body, name="small_finish", in_specs=[vm] * 5, out_specs=[vm] * len(out_shape), out_shape=out_shape,
        scratch_shapes=[pltpu.VMEM((SMALL_ROWS, LANES), F32)] * 4,
        compiler_params=_cparams(),
    )(pack_land, norm_land, wpack, mpack, vpack)
    trees = [{n: outs[1 + kind * nsmall + j].reshape(shapes[n]) for j, (n, _) in enumerate(SMALL_SIZES)}
             for kind in range(4)]
    return outs[0], trees


def kernel(x, norm1_g, w_in, sgu_ln_g, sgu_ln_b, sgu_w, sgu_b, attn_out_g, gmlp_out_g, w_out, norm2_g, w_ff1, w_ff2, final_norm_g, loss_target, m_norm1_g, m_w_in, m_sgu_ln_g, m_sgu_ln_b, m_sgu_w, m_sgu_b, m_attn_out_g, m_gmlp_out_g, m_w_out, m_norm2_g, m_w_ff1, m_w_ff2, m_final_norm_g, v_norm1_g, v_w_in, v_sgu_ln_g, v_sgu_ln_b, v_sgu_w, v_sgu_b, v_attn_out_g, v_gmlp_out_g, v_w_out, v_norm2_g, v_w_ff1, v_w_ff2, v_final_norm_g):
    names = [n for n, _ in SMALL_SIZES]
    w_small = dict(norm1_g=norm1_g, sgu_ln_g=sgu_ln_g, sgu_ln_b=sgu_ln_b, sgu_w=sgu_w, sgu_b=sgu_b,
                   attn_out_g=attn_out_g, gmlp_out_g=gmlp_out_g, norm2_g=norm2_g, final_norm_g=final_norm_g)
    m_small = dict(norm1_g=m_norm1_g, sgu_ln_g=m_sgu_ln_g, sgu_ln_b=m_sgu_ln_b, sgu_w=m_sgu_w, sgu_b=m_sgu_b,
                   attn_out_g=m_attn_out_g, gmlp_out_g=m_gmlp_out_g, norm2_g=m_norm2_g,
                   final_norm_g=m_final_norm_g)
    v_small = dict(norm1_g=v_norm1_g, sgu_ln_g=v_sgu_ln_g, sgu_ln_b=v_sgu_ln_b, sgu_w=v_sgu_w, sgu_b=v_sgu_b,
                   attn_out_g=v_attn_out_g, gmlp_out_g=v_gmlp_out_g, norm2_g=v_norm2_g,
                   final_norm_g=v_final_norm_g)
    shapes = {n: w_small[n].shape for n in names}

    start_in = _gather_start([w_in[0].T.astype(BF16)], "gather_in_start")
    issued = start_in[4][0:1, 0:1]
    start_rest = _gather_start([(w_out[0] + issued).astype(BF16), w_ff1[0].astype(BF16), w_ff2[0].astype(BF16)],
                               "gather_rest_start")
    hn1 = _norm1(x[0], norm1_g + start_rest[4][0:1, 0:1])
    win_t = _gather_wait(*start_in[:4], after=hn1, name="gather_in_wait")[0].reshape(INW, D)

    def rest_weights(after):
        wout, wff1, wff2 = _gather_wait(*start_rest[:4], after=after, name="gather_rest_wait")
        return wout.reshape(D, D), wff1, wff2.reshape(DFF, D)

    small = dict(
        norm1_g=norm1_g, ln_g=sgu_ln_g.reshape(1, GW), ln_b=sgu_ln_b.reshape(1, GW), sgu_w=sgu_w[0],
        sgu_wt=jnp.swapaxes(sgu_w[0], 1, 2), bias_t=jnp.repeat(sgu_b[0].T, DH, axis=1),
        attn_out_g=attn_out_g, gmlp_out_g=gmlp_out_g, norm2_g=norm2_g, final_norm_g=final_norm_g.reshape(1, D))
    xi, yi, ci = _position()
    sel = jnp.stack([2 * xi + yi, ci]).astype(jnp.int32)
    state = {}

    def as_slabs(g):
        return g.reshape(NCHIP, g.shape[0] // NCHIP, D)

    def early_grads(gwff1, gwff2, gwout):
        state["early"] = _reduce_start([gwff1, as_slabs(gwff2), as_slabs(gwout)], "reduce_early_start")
        return state["early"][3][0:1, 0:1]

    def after_attention_bwd(marker, partial, loss_part):
        send_sems, recv_sems, operands, _ = state["early"]
        parts, lands, _ = _reduce_wait(send_sems, recv_sems, operands, 3, marker, "reduce_early_wait")
        halves = [_sum_pieces(p, l, sel, "sum_" + n) for p, l, n in zip(parts, lands, ("w_ff1", "w_ff2", "w_out"))]
        pack = _pack_small(dict(partial, norm1_g=jnp.zeros((1, D), F32), sgu_ln_g=partial["ln_g"],
                                sgu_ln_b=partial["ln_b"]), loss_part)
        state["early_share"] = _share_start(halves, "share_early_start", pack)
        return state["early_share"][3][0:1, 0:1]

    def late_grads(gwin_t, gwin_low):
        state["late"] = _reduce_start([as_slabs(gwin_low)], "reduce_late_start")
        state["late_own"] = as_slabs(gwin_t)
        return state["late"][3][0:1, 0:1]

    _, dx, sg, _ = _local_step(
        x[0], hn1, loss_target[0], small, win_t, rest_weights, early_grads, after_attention_bwd, late_grads)
    send_sems, recv_sems, operands, _ = state["early_share"]
    own, other, pack_land = _share_wait(send_sems, recv_sems, operands, 3, dx, "share_early_wait")
    send_sems, recv_sems, operands, _ = state["late"]
    _, late_lands, _ = _reduce_wait(send_sems, recv_sems, operands, 1, dx, "reduce_late_wait")
    late_share = _share_start([_sum_pieces(state["late_own"], late_lands[0], sel, "sum_w_in")], "share_late_start",
                              sg["norm1_g"].reshape(8, LANES))
    issued = late_share[3][0:1, 0:1]
    g_big = {n: _join_halves(o, t, ci) + issued for n, o, t in zip(("w_ff1", "w_ff2", "w_out"), own, other)}
    w_big = dict(w_in=(w_in, m_w_in, v_w_in), w_out=(w_out, m_w_out, v_w_out),
                 w_ff1=(w_ff1, m_w_ff1, v_w_ff1), w_ff2=(w_ff2, m_w_ff2, v_w_ff2))
    grads, deltas, new_m, new_v = {}, {}, {}, {}

    def update(n):
        w, m, v = w_big[n]
        d, mn, vn = _adamw(w[0], g_big[n], m[0], v[0], "adamw_" + n)
        grads[n], deltas[n], new_m[n], new_v[n] = g_big[n][None], d[None], mn[None], vn[None]

    for n in ("w_ff1", "w_ff2", "w_out"):
        update(n)
    updated = deltas["w_out"][0, 0:8, 0:LANES] + deltas["w_ff1"][0, 0:8, 0:LANES] + deltas["w_ff2"][0, 0:8, 0:LANES]
    own, other, norm_land = _share_wait(late_share[0], late_share[1], late_share[2], 1, updated, "share_late_wait")
    g_big["w_in"] = _join_halves(own[0], other[0], ci).T
    update("w_in")

    loss_tile, small_trees = _small_finish(pack_land, norm_land, _pack_small(w_small), _pack_small(m_small),
                                           _pack_small(v_small), shapes)
    loss = loss_tile[0, 0]
    for tree, small_tree in zip((grads, deltas, new_m, new_v), small_trees):
        tree.update(small_tree)

    order = ["norm1_g", "w_in", "sgu_ln_g", "sgu_ln_b", "sgu_w", "sgu_b", "attn_out_g", "gmlp_out_g", "w_out",
             "norm2_g", "w_ff1", "w_ff2", "final_norm_g"]
    return (loss, dx[None], *[grads[n] for n in order], *[deltas[n] for n in order],
            *[new_m[n] for n in order], *[new_v[n] for n in order])
```

```python
import functools
import math

import numpy as np
import jax
import jax.numpy as jnp
from jax import lax
from jax.experimental import pallas as pl
from jax.experimental.pallas import tpu as pltpu

F32 = jnp.float32
BF16 = jnp.bfloat16

D = 1024
NH = 12
DH = 64
A = NH * DH
NG = 4
GW = NG * DH
INW = 3 * A + 2 * GW
DFF = 4 * D
CHUNK = 128
PATTERNS = ((128, 1), (512, 4), (2048, 16))
EPS = 1e-6
SCALE = DH ** -0.5
LOG2E = 1.0 / math.log(2.0)
LN2 = math.log(2.0)
NEG = -1e30

LR, B1, B2, AEPS, WD, STEP = 0.001, 0.9, 0.999, 1e-08, 0.01, 10

TM = 512
TM_BIG = 1024
TMX = 512
ATT_ROWS = 4096
FF_CH = 1024
LANES = 128
NCHIP = 4
VMEM_LIMIT = 56 * 1024 * 1024
MESH = pl.DeviceIdType.MESH


def _cparams(*sem, **kw):
    return pltpu.CompilerParams(dimension_semantics=sem if sem else None,
                                vmem_limit_bytes=VMEM_LIMIT, **kw)


def _dot(a, b):
    return jnp.dot(a, b, preferred_element_type=F32)


def _dot_nt(a, b):
    return lax.dot_general(a, b, (((1,), (1,)), ((), ())), preferred_element_type=F32)


def _dot_tn(a, b):
    return lax.dot_general(a, b, (((0,), (0,)), ((), ())), preferred_element_type=F32)


def _dot_hi(a, b):
    bb = b.astype(BF16)
    return functools.reduce(lambda x, y: x + y, [_dot(piece, bb) for piece in _bf16_pieces(a, 3)])


def _alibi_slopes(n):
    def pow2(m):
        start = 2.0 ** (-8.0 / m)
        return [start ** (i + 1) for i in range(m)]
    if math.log2(n).is_integer():
        s = pow2(n)
    else:
        c = 2 ** int(math.floor(math.log2(n)))
        s = pow2(c) + pow2(2 * c)[0::2][: n - c]
    return np.asarray(s, dtype=np.float32)


def _rms_fwd(v, g):
    r = lax.rsqrt(jnp.mean(v * v, axis=-1, keepdims=True) + EPS)
    vn = v * r
    return vn * g, vn, r


def _rms_bwd(dy, vn, r, g):
    w = dy * g
    dv = r * (w - vn * jnp.mean(w * vn, axis=-1, keepdims=True))
    return dv, jnp.sum(dy * vn, axis=0, keepdims=True)


_K0 = math.sqrt(2.0 / math.pi)
_K1 = 0.044715


def _gelu(v):
    return 0.5 * v * (1.0 + jnp.tanh(_K0 * (v + _K1 * (v * v * v))))


def _gelu_grad(v):
    t = jnp.tanh(_K0 * (v + _K1 * (v * v * v)))
    return 0.5 * (1.0 + t) + 0.5 * v * (1.0 - t * t) * (_K0 * (1.0 + 3.0 * _K1 * v * v))


def _row_spec(rows, cols):
    return pl.BlockSpec((rows, cols), lambda i: (i, 0))


def _const_spec(shape):
    nd = len(shape)
    return pl.BlockSpec(shape, lambda i: (0,) * nd, pipeline_mode=pl.Buffered(1))


DILS = tuple(d for _, d in PATTERNS)


def _fill_cols(scr, value):
    for cb in range(value.shape[1] // LANES):
        scr[cb] = value[:, cb * LANES:(cb + 1) * LANES]


def _split_residues(scr, out_ref, dil):
    nb, rows, _ = scr.shape
    for r in range(dil):
        for cb in range(nb):
            piece = scr.at[cb][pl.ds(r, rows // dil, stride=dil), :]
            out_ref[r, :, cb * LANES:(cb + 1) * LANES] = piece.astype(out_ref.dtype)


def _merge_residues(in_ref, scr, dil):
    nb, rows, _ = scr.shape
    for r in range(dil):
        for cb in range(nb):
            scr.at[cb][pl.ds(r, rows // dil, stride=dil), :] = in_ref[r, :, cb * LANES:(cb + 1) * LANES].astype(F32)
    return jnp.concatenate([scr[cb] for cb in range(nb)], axis=-1)


def _col_scratch(rows, width):
    return pltpu.VMEM((width // LANES, rows, LANES), F32)


def _res_spec(dil, rows, width):
    return pl.BlockSpec((dil, rows // dil, width), lambda i: (0, i, 0))


def _res_shape(s, dil, width, dtype):
    return jax.ShapeDtypeStruct((dil, s // dil, width), dtype)


def _norm1(x, g1):
    s = x.shape[0]

    def body(x_ref, g_ref, hn_ref):
        hn, _, _ = _rms_fwd(x_ref[...], g_ref[...])
        hn_ref[...] = hn.astype(BF16)

    return pl.pallas_call(
        body, name="norm1", grid=(s // TM,), in_specs=[_row_spec(TM, D), _const_spec((1, D))],
        out_specs=_row_spec(TM, D), out_shape=jax.ShapeDtypeStruct((s, D), BF16),
        compiler_params=_cparams("arbitrary"),
    )(x, g1)


def _inproj_fwd(hn1, win_t):
    s = hn1.shape[0]
    nd = len(DILS)

    def body(hn_ref, w_ref, *rest):
        qkv_refs = rest[:3 * nd]
        u_ref, z_ref, scr = rest[3 * nd:]
        hn = hn_ref[...]
        for t in range(3):
            seg = _dot_nt(hn, w_ref[t * A:(t + 1) * A, :])
            seg = seg * (SCALE * LOG2E) if t == 0 else seg
            _fill_cols(scr, seg)
            for di, dil in enumerate(DILS):
                if dil == 1:
                    qkv_refs[t * nd + di][0] = seg.astype(BF16)
                else:
                    _split_residues(scr, qkv_refs[t * nd + di], dil)
        u_ref[...] = _dot_nt(hn, w_ref[3 * A:3 * A + GW, :])
        z_ref[...] = _dot_nt(hn, w_ref[3 * A + GW:INW, :])

    res = pl.pallas_call(
        body, name="inproj_fwd", grid=(s // TM_BIG,),
        in_specs=[_row_spec(TM_BIG, D), _const_spec((INW, D))],
        out_specs=[_res_spec(d, TM_BIG, A) for _ in range(3) for d in DILS]
                  + [_row_spec(TM_BIG, GW), _row_spec(TM_BIG, GW)],
        out_shape=[_res_shape(s, d, A, BF16) for _ in range(3) for d in DILS]
                  + [jax.ShapeDtypeStruct((s, GW), F32)] * 2,
        scratch_shapes=[_col_scratch(TM_BIG, A)],
        compiler_params=_cparams("arbitrary"),
    )(hn1, win_t)
    q, k, v = (res[t * nd:(t + 1) * nd] for t in range(3))
    return q, k, v, res[-2], res[-1]


def _att_geometry(length, dil):
    merge = max(1, min(dil, ATT_ROWS // length))
    rows = min(length * merge, ATT_ROWS)
    nsub = rows // CHUNK
    return merge, rows, length * merge // rows, nsub, min(length // CHUNK, nsub)


def _merged(t, merge):
    return t.reshape(t.shape[0] // merge, t.shape[1] * merge, t.shape[2])


def _stack_heads(t):
    lane = lax.broadcasted_iota(jnp.int32, t.shape, 1)
    zero = jnp.zeros_like(t)
    return jnp.concatenate([jnp.where(lane < DH, t, zero), jnp.where(lane >= DH, t, zero)], axis=0)


def _head_cols(t, hp):
    lane = lax.broadcasted_iota(jnp.int32, t.shape, 1)
    cols = [jnp.sum(jnp.where(lane == 2 * hp + h, t, 0.0), axis=-1, keepdims=True) for h in range(2)]
    return jnp.concatenate(cols, axis=0)


def _unstack_heads(t2):
    n = t2.shape[0] // 2
    lane = lax.broadcasted_iota(jnp.int32, (n, LANES), 1)
    return jnp.where(lane < DH, t2[:n], t2[n:])


def _query_window_bias(s0, s1, dil, first):
    row = lax.broadcasted_iota(jnp.int32, (2 * CHUNK, 2 * CHUNK), 0)
    col = lax.broadcasted_iota(jnp.int32, (2 * CHUNK, 2 * CHUNK), 1)
    steps = (row & (CHUNK - 1)) + CHUNK - col
    valid = (steps >= 0) & (steps <= CHUNK)
    if first:
        valid = valid & (col >= CHUNK)
    slope = jnp.where(row < CHUNK, s0, s1)
    return jnp.where(valid, -slope * (steps * dil).astype(F32), NEG)


def _key_block_bias(s0, s1, dil, last):
    key = lax.broadcasted_iota(jnp.int32, (CHUNK, 4 * CHUNK), 0)
    col = lax.broadcasted_iota(jnp.int32, (CHUNK, 4 * CHUNK), 1)
    wq = col & (2 * CHUNK - 1)
    steps = wq - key
    valid = (steps >= 0) & (steps <= CHUNK)
    if last:
        valid = valid & (wq < CHUNK)
    slope = jnp.where(col < 2 * CHUNK, s0, s1)
    return jnp.where(valid, -slope * (steps * dil).astype(F32), NEG)


def _head_rows(t, hp):
    row = lax.broadcasted_iota(jnp.int32, (8, LANES), 0)
    lane = lax.broadcasted_iota(jnp.int32, (8, LANES), 1)
    pick = jnp.where((row < 2) & (lane == 2 * hp + row), 1.0, 0.0).astype(BF16)
    hi = t.astype(BF16)
    rest = t - hi.astype(F32)
    mid = rest.astype(BF16)
    low = (rest - mid.astype(F32)).astype(BF16)
    return _dot_nt(pick, hi) + _dot_nt(pick, mid) + _dot_nt(pick, low)


def _att_specs(dil, rows, nsub, nblk):
    main = pl.BlockSpec((None, rows, LANES), lambda r, c, hp: (r, c, hp))
    prev = pl.BlockSpec((None, CHUNK, LANES), lambda r, c, hp: (r, jnp.maximum(c * nsub - 1, 0), hp))
    nxt = pl.BlockSpec((None, CHUNK, LANES), lambda r, c, hp: (r, jnp.minimum((c + 1) * nsub, nblk - 1), hp))
    main_heads = pl.BlockSpec((None, rows, LANES), lambda r, c, hp: (r, c, 0))
    nxt_heads = pl.BlockSpec((None, CHUNK, LANES), lambda r, c, hp: (r, jnp.minimum((c + 1) * nsub, nblk - 1), 0))
    return main, prev, nxt, main_heads, nxt_heads


def _row_start(i):
    return i * CHUNK if isinstance(i, int) else pl.multiple_of(i * CHUNK, CHUNK)


def _first_blocks(block, nsub, seg, nch, ch, first_bias, bias_buf):
    for i in range(nsub):
        if i % seg:
            block(i, bias_buf[...])
        elif nch == 1:
            block(i, first_bias())
        else:
            block(i, jnp.where(ch == 0, first_bias(), bias_buf[...]))


def _last_blocks(block, nsub, seg, nch, ch, last_bias, bias_buf):
    for i in range(nsub):
        if (i + 1) % seg:
            block(i, bias_buf[...])
        elif nch == 1:
            block(i, last_bias())
        else:
            block(i, jnp.where(ch == nch - 1, last_bias(), bias_buf[...]))


def _attn_fwd(q, k, v, slopes, dil):
    length = q.shape[1]
    merge, rows, nch, nsub, seg = _att_geometry(length, dil)
    main, prev, _, main_heads, _ = _att_specs(dil, rows, nsub, length * merge // CHUNK)
    q, k, v = (_merged(t, merge) for t in (q, k, v))

    def body(sl_ref, q_ref, k_ref, v_ref, kh_ref, vh_ref, o_ref, lse_ref, kbuf, vbuf, bias_buf):
        ch = pl.program_id(1)
        hp = pl.program_id(2)
        lane = lax.broadcasted_iota(jnp.int32, (CHUNK, LANES), 1)
        kbuf[0:CHUNK, :] = kh_ref[...]
        kbuf[CHUNK:, :] = k_ref[...]
        vbuf[0:CHUNK, :] = vh_ref[...]
        vbuf[CHUNK:, :] = v_ref[...]
        s0, s1 = sl_ref[2 * hp], sl_ref[2 * hp + 1]

        def block(i, bias):
            row = _row_start(i)
            rs = pl.ds(row, CHUNK)
            q2 = _stack_heads(q_ref[rs, :])
            kw = kbuf[pl.ds(row, 2 * CHUNK), :]
            vw = vbuf[pl.ds(row, 2 * CHUNK), :]
            sc = _dot_nt(q2, kw) + bias
            m = jnp.max(sc, axis=-1, keepdims=True)
            p = jnp.exp2(sc - m)
            l = jnp.sum(p, axis=-1, keepdims=True)
            o2 = _dot(p.astype(BF16), vw) * (1.0 / l)
            o_ref[rs, :] = _unstack_heads(o2).astype(BF16)
            lse = m + jnp.log2(l)
            seen = jnp.where(hp == 0, 0.0, lse_ref[rs, :])
            lse_ref[rs, :] = jnp.where(lane == 2 * hp, lse[:CHUNK], jnp.where(lane == 2 * hp + 1, lse[CHUNK:], seen))

        bias_buf[...] = _query_window_bias(s0, s1, dil, False)
        _first_blocks(block, nsub, seg, nch, ch, lambda: _query_window_bias(s0, s1, dil, True), bias_buf)

    sd = jax.ShapeDtypeStruct
    o, lse = pl.pallas_call(
        body, name=f"attn_fwd_d{dil}", grid=(dil // merge, nch, NH // 2),
        in_specs=[pl.BlockSpec(memory_space=pltpu.SMEM), main, main, main, prev, prev],
        out_specs=[main, main_heads],
        out_shape=[sd((dil // merge, length * merge, A), BF16), sd((dil // merge, length * merge, LANES), F32)],
        scratch_shapes=[pltpu.VMEM((rows + CHUNK, LANES), BF16), pltpu.VMEM((rows + CHUNK, LANES), BF16),
                        pltpu.VMEM((2 * CHUNK, 2 * CHUNK), F32)],
        compiler_params=_cparams("arbitrary", "arbitrary", "arbitrary"),
    )(slopes, q, k, v, k, v)
    return o.reshape(dil, length, A), lse.reshape(dil, length, LANES)


def _attn_bwd_dq(q, k, v, do, lse, delta, slopes, dil):
    length = q.shape[1]
    merge, rows, nch, nsub, seg = _att_geometry(length, dil)
    main, prev, _, main_heads, _ = _att_specs(dil, rows, nsub, length * merge // CHUNK)
    q, k, v, do, lse, delta = (_merged(t, merge) for t in (q, k, v, do, lse, delta))

    def body(sl_ref, q_ref, k_ref, v_ref, do_ref, lse_ref, dl_ref, kh_ref, vh_ref, dq_ref, kbuf, vbuf, bias_buf):
        ch = pl.program_id(1)
        hp = pl.program_id(2)
        kbuf[0:CHUNK, :] = kh_ref[...]
        kbuf[CHUNK:, :] = k_ref[...]
        vbuf[0:CHUNK, :] = vh_ref[...]
        vbuf[CHUNK:, :] = v_ref[...]
        s0, s1 = sl_ref[2 * hp], sl_ref[2 * hp + 1]

        def block(i, bias):
            row = _row_start(i)
            rs = pl.ds(row, CHUNK)
            q2 = _stack_heads(q_ref[rs, :])
            do2 = _stack_heads(do_ref[rs, :])
            lse2 = _head_cols(lse_ref[rs, :], hp)
            dl2 = _head_cols(dl_ref[rs, :], hp)
            kw = kbuf[pl.ds(row, 2 * CHUNK), :]
            vw = vbuf[pl.ds(row, 2 * CHUNK), :]
            p = jnp.exp2(_dot_nt(q2, kw) + bias - lse2)
            ds = p * (_dot_nt(do2, vw) - dl2)
            dq_ref[rs, :] = _unstack_heads(_dot(ds.astype(BF16), kw)).astype(BF16)

        bias_buf[...] = _query_window_bias(s0, s1, dil, False)
        _first_blocks(block, nsub, seg, nch, ch, lambda: _query_window_bias(s0, s1, dil, True), bias_buf)

    dq = pl.pallas_call(
        body, name=f"attn_dq_d{dil}", grid=(dil // merge, nch, NH // 2),
        in_specs=[pl.BlockSpec(memory_space=pltpu.SMEM), main, main, main, main, main_heads, main_heads, prev, prev],
        out_specs=main, out_shape=jax.ShapeDtypeStruct((dil // merge, length * merge, A), BF16),
        scratch_shapes=[pltpu.VMEM((rows + CHUNK, LANES), BF16), pltpu.VMEM((rows + CHUNK, LANES), BF16),
                        pltpu.VMEM((2 * CHUNK, 2 * CHUNK), F32)],
        compiler_params=_cparams("arbitrary", "arbitrary", "arbitrary"),
    )(slopes, q, k, v, do, lse, delta, k, v)
    return dq.reshape(dil, length, A)


def _attn_bwd_dkv(q, k, v, do, lse, delta, slopes, dil):
    length = q.shape[1]
    merge, rows, nch, nsub, seg = _att_geometry(length, dil)
    main, _, nxt, main_heads, nxt_heads = _att_specs(dil, rows, nsub, length * merge // CHUNK)
    q, k, v, do, lse, delta = (_merged(t, merge) for t in (q, k, v, do, lse, delta))

    def body(sl_ref, k_ref, v_ref, q_ref, do_ref, lse_ref, dl_ref, qh_ref, doh_ref, lseh_ref, dlh_ref,
             dk_ref, dv_ref, qbuf, dobuf, lse_rows, dl_rows, bias_buf):
        ch = pl.program_id(1)
        hp = pl.program_id(2)
        for buf, main_ref, halo_ref in ((qbuf, q_ref, qh_ref), (dobuf, do_ref, doh_ref)):
            buf[0:rows, :] = main_ref[...]
            buf[rows:, :] = halo_ref[...]
        for buf, main_ref, halo_ref in ((lse_rows, lse_ref, lseh_ref), (dl_rows, dl_ref, dlh_ref)):
            buf[:, 0:rows] = _head_rows(main_ref[...], hp)
            buf[:, rows:] = _head_rows(halo_ref[...], hp)
        s0, s1 = sl_ref[2 * hp], sl_ref[2 * hp + 1]

        def block(i, bias):
            row = _row_start(i)
            rs = pl.ds(row, CHUNK)
            win = pl.ds(row, 2 * CHUNK)
            kc = k_ref[rs, :]
            vc = v_ref[rs, :]
            q2 = _stack_heads(qbuf[win, :])
            do2 = _stack_heads(dobuf[win, :])
            cols = slice(i * CHUNK, (i + 2) * CHUNK)
            lse2 = jnp.concatenate([lse_rows[0:1, cols], lse_rows[1:2, cols]], axis=1)
            dl2 = jnp.concatenate([dl_rows[0:1, cols], dl_rows[1:2, cols]], axis=1)
            pt = jnp.exp2(_dot_nt(kc, q2) + bias - lse2)
            dst = pt * (_dot_nt(vc, do2) - dl2)
            dv_ref[rs, :] = _dot(pt.astype(BF16), do2).astype(BF16)
            dk_ref[rs, :] = (_dot(dst.astype(BF16), q2) * LN2).astype(BF16)

        bias_buf[...] = _key_block_bias(s0, s1, dil, False)
        _last_blocks(block, nsub, seg, nch, ch, lambda: _key_block_bias(s0, s1, dil, True), bias_buf)

    sd = jax.ShapeDtypeStruct((dil // merge, length * merge, A), BF16)
    dk, dv = pl.pallas_call(
        body, name=f"attn_dkv_d{dil}", grid=(dil // merge, nch, NH // 2),
        in_specs=[pl.BlockSpec(memory_space=pltpu.SMEM), main, main, main, main, main_heads, main_heads,
                  nxt, nxt, nxt_heads, nxt_heads],
        out_specs=[main, main], out_shape=[sd, sd],
        scratch_shapes=[pltpu.VMEM((rows + CHUNK, LANES), BF16), pltpu.VMEM((rows + CHUNK, LANES), BF16),
                        pltpu.VMEM((8, rows + CHUNK), F32), pltpu.VMEM((8, rows + CHUNK), F32),
                        pltpu.VMEM((CHUNK, 4 * CHUNK), F32)],
        compiler_params=_cparams("arbitrary", "arbitrary", "arbitrary"),
    )(slopes, k, v, q, do, lse, delta, q, do, lse, delta)
    return dk.reshape(dil, length, A), dv.reshape(dil, length, A)


def _group_masks(width):
    lane = lax.broadcasted_iota(jnp.int32, (1, width), 1)
    return [(lane >= g * DH) & (lane < (g + 1) * DH) for g in range(width // DH)]


def _group_mean_matrix():
    i = lax.broadcasted_iota(jnp.int32, (GW, GW), 0) // DH
    j = lax.broadcasted_iota(jnp.int32, (GW, GW), 1) // DH
    return jnp.where(i == j, 1.0 / DH, 0.0).astype(F32)


def _tri_mask(lower):
    t = lax.broadcasted_iota(jnp.int32, (CHUNK, CHUNK), 0)
    u = lax.broadcasted_iota(jnp.int32, (CHUNK, CHUNK), 1)
    return (u <= t) if lower else (u >= t)


def _sgu_forward(u, z, lng, lnb, w_ref, bias_t, pmat, rows):
    ug = _gelu(u)
    zg = _gelu(z)
    mu = _dot_hi(zg, pmat)
    zc = zg - mu
    var = _dot_hi(zc * zc, pmat)
    rstd = lax.rsqrt(var + EPS)
    zhat = zc * rstd
    zn = (zhat * lng + lnb).astype(BF16)
    gm = _group_masks(GW)
    tri = _tri_mask(True)
    ws = [jnp.where(tri, w_ref[g], 0.0).astype(BF16) for g in range(NG)]
    pieces = []
    for c in range(rows // CHUNK):
        znc = zn[c * CHUNK:(c + 1) * CHUNK, :]
        mix = None
        for g in range(NG):
            part = jnp.where(gm[g], _dot(ws[g], znc), 0.0)
            mix = part if mix is None else mix + part
        pieces.append(mix + bias_t)
    mixed = jnp.concatenate(pieces, axis=0) if len(pieces) > 1 else pieces[0]
    return ug * mixed, ug, zhat, rstd, zn, mixed


def _head_spread():
    h = lax.broadcasted_iota(jnp.int32, (LANES, A), 0)
    lane = lax.broadcasted_iota(jnp.int32, (LANES, A), 1)
    return jnp.where(lane // DH == h, 1.0, 0.0).astype(BF16)


def _bf16_pieces(t, n):
    pieces = []
    for _ in range(n):
        piece = t.astype(BF16)
        pieces.append(piece)
        t = t - piece.astype(F32)
    return pieces


def _mix_fwd(os_, ls_, u, z, x, lng, lnb, sgu_w, bias_t, ga, gg, wout):
    s = x.shape[0]
    nd = len(DILS)
    nscr = sum(1 for d in DILS if d > 1)

    def body(*refs):
        o_refs, l_refs = refs[:nd], refs[nd:2 * nd]
        u_ref, z_ref, x_ref, lng_ref, lnb_ref, w_ref, bt_ref, ga_ref, gg_ref, wo_ref = refs[2 * nd:2 * nd + 10]
        attn_ref = refs[2 * nd + 10]
        lse_refs = refs[2 * nd + 11:3 * nd + 11]
        mixed_ref, h1_ref = refs[3 * nd + 11:3 * nd + 13]
        scr = refs[3 * nd + 13:]
        scr_o, scr_l, scr_lse = scr[:nscr], scr[nscr:2 * nscr], scr[2 * nscr]
        ov, lv, j = [], [], 0
        for di, dil in enumerate(DILS):
            if dil == 1:
                ov.append(o_refs[di][0].astype(F32))
                lv.append(l_refs[di][0])
            else:
                ov.append(_merge_residues(o_refs[di], scr_o[j], dil))
                lv.append(_merge_residues(l_refs[di], scr_l[j], dil))
                j += 1
        mx = functools.reduce(jnp.maximum, lv)
        es = [jnp.exp2(l - mx) for l in lv]
        den = functools.reduce(lambda a, b: a + b, es)
        spread = _head_spread()
        attn = None
        for e, o in zip(es, ov):
            wide = functools.reduce(lambda a, b: a + b, [_dot(piece, spread) for piece in _bf16_pieces(e / den, 2)])
            attn = wide * o if attn is None else attn + wide * o
        attn_ref[...] = attn
        lse = mx + jnp.log2(den)
        _fill_cols(scr_lse, lse)
        for di, dil in enumerate(DILS):
            if dil == 1:
                lse_refs[di][0] = lse
            else:
                _split_residues(scr_lse, lse_refs[di], dil)
        an, _, _ = _rms_fwd(attn, ga_ref[...])
        gmv, _, _, _, _, _ = _sgu_forward(u_ref[...], z_ref[...], lng_ref[...], lnb_ref[...], w_ref,
                                          bt_ref[...], _group_mean_matrix(), TMX)
        gn, _, _ = _rms_fwd(gmv, gg_ref[...])
        mixed = jnp.concatenate([an, gn], axis=-1).astype(BF16)
        mixed_ref[...] = mixed
        h1_ref[...] = x_ref[...] + _dot(mixed, wo_ref[...])

    sd = jax.ShapeDtypeStruct
    res = pl.pallas_call(
        body, name="mix_fwd", grid=(s // TMX,),
        in_specs=[_res_spec(d, TMX, A) for d in DILS] + [_res_spec(d, TMX, LANES) for d in DILS]
                 + [_row_spec(TMX, GW), _row_spec(TMX, GW),
                    _row_spec(TMX, D), _const_spec((1, GW)), _const_spec((1, GW)), _const_spec((NG, CHUNK, CHUNK)),
                    _const_spec((CHUNK, GW)), _const_spec((1, A)), _const_spec((1, GW)), _const_spec((D, D))],
        out_specs=[_row_spec(TMX, A)] + [_res_spec(d, TMX, LANES) for d in DILS]
                  + [_row_spec(TMX, D), _row_spec(TMX, D)],
        out_shape=[sd((s, A), F32)] + [_res_shape(s, d, LANES, F32) for d in DILS]
                  + [sd((s, D), BF16), sd((s, D), F32)],
        scratch_shapes=[_col_scratch(TMX, A)] * nscr + [_col_scratch(TMX, LANES)] * (nscr + 1),
        compiler_params=_cparams("arbitrary"),
    )(*os_, *ls_, u, z, x, lng, lnb, sgu_w, bias_t, ga, gg, wout)
    return res[0], res[1:1 + nd], res[1 + nd], res[2 + nd]


def _mlp_fwd(h1, g2, wff1, wff2, gf, target):
    s = h1.shape[0]

    def body(h1_ref, g2_ref, w1_ref, w2_ref, gf_ref, t_ref, hn_ref, rf_ref, dh2_ref, loss_ref, dgf_ref):
        i = pl.program_id(0)
        h1v = h1_ref[...]
        hn, _, _ = _rms_fwd(h1v, g2_ref[...])
        hn = hn.astype(BF16)
        hn_ref[...] = hn
        acc = h1v
        for j in range(DFF // FF_CH):
            cols = slice(j * FF_CH, (j + 1) * FF_CH)
            rf = jnp.maximum(_dot(hn, w1_ref[j]), 0.0)
            act = (rf * rf).astype(BF16)
            rf_ref[:, cols] = rf.astype(BF16)
            acc = acc + _dot(act, w2_ref[cols, :])
        y, h2n, r3 = _rms_fwd(acc, gf_ref[...])
        err = y - t_ref[...]
        part = 0.5 * jnp.sum(jnp.mean(err * err, axis=-1, keepdims=True), axis=0, keepdims=True)
        dy = err * (1.0 / D)
        dh2, dgf = _rms_bwd(dy, h2n, r3, gf_ref[...])
        dh2_ref[...] = dh2

        @pl.when(i == 0)
        def _():
            loss_ref[...] = jnp.zeros_like(loss_ref)
            dgf_ref[...] = jnp.zeros_like(dgf_ref)

        loss_ref[...] += jnp.broadcast_to(part, loss_ref.shape)
        dgf_ref[...] += dgf

    sd = jax.ShapeDtypeStruct
    return pl.pallas_call(
        body, name="mlp_fwd", grid=(s // TM,),
        in_specs=[_row_spec(TM, D), _const_spec((1, D)), _const_spec((DFF // FF_CH, D, FF_CH)), _const_spec((DFF, D)),
                  _const_spec((1, D)), _row_spec(TM, D)],
        out_specs=[_row_spec(TM, D), _row_spec(TM, DFF), _row_spec(TM, D),
                   _const_spec((1, LANES)), _const_spec((1, D))],
        out_shape=[sd((s, D), BF16), sd((s, DFF), BF16), sd((s, D), F32),
                   sd((1, LANES), F32), sd((1, D), F32)],
        compiler_params=_cparams("arbitrary"),
    )(h1, g2, wff1, wff2, gf, target)


def _mlp_bwd(dh2, rf, h1, g2, wff1, wff2):
    s = h1.shape[0]

    def body(dh2_ref, rf_ref, h1_ref, g2_ref, w1_ref, w2_ref, df_ref, dh1_ref, dg2_ref):
        i = pl.program_id(0)
        dh2v = dh2_ref[...]
        dh2b = dh2v.astype(BF16)
        dhn = jnp.zeros((TM, D), F32)
        for j in range(DFF // FF_CH):
            cols = slice(j * FF_CH, (j + 1) * FF_CH)
            da = _dot_nt(dh2b, w2_ref[cols, :])
            df = (da * (2.0 * rf_ref[:, cols].astype(F32))).astype(BF16)
            df_ref[:, cols] = df
            dhn = dhn + _dot_nt(df, w1_ref[j])
        _, h1n, r2 = _rms_fwd(h1_ref[...], g2_ref[...])
        dres, dg2 = _rms_bwd(dhn, h1n, r2, g2_ref[...])
        dh1_ref[...] = dh2v + dres

        @pl.when(i == 0)
        def _():
            dg2_ref[...] = jnp.zeros_like(dg2_ref)

        dg2_ref[...] += dg2

    sd = jax.ShapeDtypeStruct
    return pl.pallas_call(
        body, name="mlp_bwd", grid=(s // TM,),
        in_specs=[_row_spec(TM, D), _row_spec(TM, DFF), _row_spec(TM, D), _const_spec((1, D)),
                  _const_spec((DFF // FF_CH, D, FF_CH)), _const_spec((DFF, D))],
        out_specs=[_row_spec(TM, DFF), _row_spec(TM, D), _const_spec((1, D))],
        out_shape=[sd((s, DFF), BF16), sd((s, D), F32), sd((1, D), F32)],
        compiler_params=_cparams("arbitrary"),
    )(dh2, rf, h1, g2, wff1, wff2)


def _mix_bwd(dh1, attn, u, z, lng, lnb, sgu_w, sgu_wt, bias_t, ga, gg, wout):
    s = dh1.shape[0]
    nsteps = s // TMX
    nd = len(DILS)

    def body(*refs):
        dh1_ref, attn_ref, u_ref, z_ref, lng_ref, lnb_ref, w_ref, wt_ref, bt_ref, ga_ref, gg_ref, wo_ref = refs[:12]
        do_refs, dl_refs = refs[12:12 + nd], refs[12 + nd:12 + 2 * nd]
        (du_ref, dz_ref, dga_ref, dgg_ref, dlng_ref, dlnb_ref, dws_ref, db_ref,
         dbt_acc, scr_do, scr_dl) = refs[12 + 2 * nd:]
        i = pl.program_id(0)

        @pl.when(i == 0)
        def _():
            for r in (dga_ref, dgg_ref, dlng_ref, dlnb_ref, dws_ref, db_ref, dbt_acc):
                r[...] = jnp.zeros_like(r)

        dmixed = _dot_nt(dh1_ref[...].astype(BF16), wo_ref[...])
        attn = attn_ref[...]
        _, an, ra = _rms_fwd(attn, ga_ref[...])
        dattn, dga = _rms_bwd(dmixed[:, :A], an, ra, ga_ref[...])
        dga_ref[...] += dga
        _fill_cols(scr_do, dattn)
        spread = _head_spread()
        delta = functools.reduce(lambda a, b: a + b, [_dot_nt(piece, spread) for piece in _bf16_pieces(dattn * attn, 3)])
        _fill_cols(scr_dl, delta)
        for di, dil in enumerate(DILS):
            if dil == 1:
                do_refs[di][0] = dattn.astype(BF16)
                dl_refs[di][0] = delta
            else:
                _split_residues(scr_do, do_refs[di], dil)
                _split_residues(scr_dl, dl_refs[di], dil)
        pmat = _group_mean_matrix()
        lng = lng_ref[...]
        uv, zv = u_ref[...], z_ref[...]
        gmv, ug, zhat, rstd, zn, mixed = _sgu_forward(uv, zv, lng, lnb_ref[...], w_ref, bt_ref[...], pmat, TMX)
        _, gmn, rg = _rms_fwd(gmv, gg_ref[...])
        dgm, dgg = _rms_bwd(dmixed[:, A:], gmn, rg, gg_ref[...])
        dgg_ref[...] += dgg
        du_ref[...] = (dgm * mixed * _gelu_grad(uv)).astype(BF16)
        dmx = dgm * ug
        dmxb = dmx.astype(BF16)
        gm = _group_masks(GW)
        tri_t = _tri_mask(False)
        wst = [jnp.where(tri_t, wt_ref[g], 0.0).astype(BF16) for g in range(NG)]
        zero = jnp.zeros((CHUNK, GW), BF16)
        dzn_pieces = []
        for c in range(TMX // CHUNK):
            rs = slice(c * CHUNK, (c + 1) * CHUNK)
            dmc = dmxb[rs, :]
            znc = zn[rs, :]
            dbt_acc[...] += dmx[rs, :]
            dzn = None
            for g in range(NG):
                dws_ref[g] += _dot_nt(jnp.where(gm[g], dmc, zero), znc)
                part = jnp.where(gm[g], _dot(wst[g], dmc), 0.0)
                dzn = part if dzn is None else dzn + part
            dzn_pieces.append(dzn)
        dzn = jnp.concatenate(dzn_pieces, axis=0)
        dlng_ref[...] += jnp.sum(dzn * zhat, axis=0, keepdims=True)
        dlnb_ref[...] += jnp.sum(dzn, axis=0, keepdims=True)
        dzh = dzn * lng
        dzg = rstd * (dzh - _dot_hi(dzh, pmat) - zhat * _dot_hi(dzh * zhat, pmat))
        dz_ref[...] = (dzg * _gelu_grad(zv)).astype(BF16)

        @pl.when(i == nsteps - 1)
        def _():
            tri = _tri_mask(True)
            for g in range(NG):
                dws_ref[g] = jnp.where(tri, dws_ref[g], 0.0)
            acc = dbt_acc[...]
            lane = lax.broadcasted_iota(jnp.int32, (CHUNK, LANES), 1)
            out = jnp.zeros((CHUNK, LANES), F32)
            for g in range(NG):
                sg = jnp.sum(jnp.where(gm[g], acc, 0.0), axis=-1, keepdims=True)
                out = jnp.where(lane == g, sg, out)
            db_ref[...] = out

    sd = jax.ShapeDtypeStruct
    res = pl.pallas_call(
        body, name="mix_bwd", grid=(nsteps,),
        in_specs=[_row_spec(TMX, D), _row_spec(TMX, A), _row_spec(TMX, GW), _row_spec(TMX, GW),
                  _const_spec((1, GW)), _const_spec((1, GW)), _const_spec((NG, CHUNK, CHUNK)),
                  _const_spec((NG, CHUNK, CHUNK)), _const_spec((CHUNK, GW)), _const_spec((1, A)),
                  _const_spec((1, GW)), _const_spec((D, D))],
        out_specs=[_res_spec(d, TMX, A) for d in DILS] + [_res_spec(d, TMX, LANES) for d in DILS]
                  + [_row_spec(TMX, GW), _row_spec(TMX, GW),
                   _const_spec((1, A)), _const_spec((1, GW)), _const_spec((1, GW)), _const_spec((1, GW)),
                   _const_spec((NG, CHUNK, CHUNK)), _const_spec((CHUNK, LANES))],
        out_shape=[_res_shape(s, d, A, BF16) for d in DILS] + [_res_shape(s, d, LANES, F32) for d in DILS]
                  + [sd((s, GW), BF16), sd((s, GW), BF16),
                   sd((1, A), F32), sd((1, GW), F32), sd((1, GW), F32), sd((1, GW), F32),
                   sd((NG, CHUNK, CHUNK), F32), sd((CHUNK, LANES), F32)],
        scratch_shapes=[pltpu.VMEM((CHUNK, GW), F32), _col_scratch(TMX, A), _col_scratch(TMX, LANES)],
        compiler_params=_cparams("arbitrary"),
    )(dh1, attn, u, z, lng, lnb, sgu_w, sgu_wt, bias_t, ga, gg, wout)
    return (res[:nd], res[nd:2 * nd]) + tuple(res[2 * nd:])


def _dproj_merge(dqs, dks, dvs, du, dz, pin):
    s = du.shape[0]
    nd = len(DILS)
    nscr = sum(1 for d in DILS if d > 1)

    def body(*refs):
        pin_ref = refs[0]
        parts = [refs[1 + t * nd:1 + (t + 1) * nd] for t in range(3)]
        du_ref, dz_ref, dp_ref = refs[1 + 3 * nd:4 + 3 * nd]
        scr = refs[4 + 3 * nd:]
        sums = []
        for t in range(3):
            total, j = None, 0
            for di, dil in enumerate(DILS):
                if dil == 1:
                    term = parts[t][di][0].astype(F32)
                else:
                    term = _merge_residues(parts[t][di], scr[t * nscr + j], dil)
                    j += 1
                total = term if total is None else total + term
            sums.append(total)
        dp_ref[...] = jnp.concatenate([sums[0] * SCALE, sums[1], sums[2], du_ref[...].astype(F32) + pin_ref[0, 0],
                                       dz_ref[...].astype(F32)], axis=-1).astype(BF16)

    return pl.pallas_call(
        body, name="dproj_merge", grid=(s // TMX,),
        in_specs=[pl.BlockSpec(memory_space=pltpu.SMEM)] + [_res_spec(d, TMX, A) for d in DILS] * 3
                 + [_row_spec(TMX, GW)] * 2,
        out_specs=_row_spec(TMX, INW), out_shape=jax.ShapeDtypeStruct((s, INW), BF16),
        scratch_shapes=[_col_scratch(TMX, A)] * (3 * nscr),
        compiler_params=_cparams("arbitrary"),
    )(pin, *dqs, *dks, *dvs, du, dz)


def _inproj_bwd(dproj, dh1, x, g1, win_t):
    s = x.shape[0]

    def body(dp_ref, dh1_ref, x_ref, g_ref, w_ref, dx_ref, dg_ref):
        i = pl.program_id(0)
        dhn = _dot(dp_ref[...], w_ref[...])
        _, xn, r1 = _rms_fwd(x_ref[...], g_ref[...])
        dres, dg = _rms_bwd(dhn, xn, r1, g_ref[...])
        dx_ref[...] = dh1_ref[...] + dres

        @pl.when(i == 0)
        def _():
            dg_ref[...] = jnp.zeros_like(dg_ref)

        dg_ref[...] += dg

    sd = jax.ShapeDtypeStruct
    return pl.pallas_call(
        body, name="inproj_bwd", grid=(s // TM_BIG,),
        in_specs=[_row_spec(TM_BIG, INW), _row_spec(TM_BIG, D), _row_spec(TM_BIG, D), _const_spec((1, D)), _const_spec((INW, D))],
        out_specs=[_row_spec(TM_BIG, D), _const_spec((1, D))],
        out_shape=[sd((s, D), F32), sd((1, D), F32)],
        compiler_params=_cparams("arbitrary"),
    )(dproj, dh1, x, g1, win_t)


def _wgrad(a, b, name, bm, bn, bk=4 * TM, square_a=False, also_bf16=False):
    s, m = a.shape
    n = b.shape[1]
    bm, bn = min(bm, m), min(bn, n)
    nk = s // bk

    def body(a_ref, b_ref, o_ref, *low):
        @pl.when(pl.program_id(2) == 0)
        def _():
            o_ref[...] = jnp.zeros_like(o_ref)

        av = a_ref[...]
        if square_a:
            av = av.astype(F32)
            av = av * av
        o_ref[...] += _dot_tn(av.astype(BF16), b_ref[...].astype(BF16))
        if also_bf16:
            @pl.when(pl.program_id(2) == nk - 1)
            def _():
                low[0][...] = o_ref[...].astype(BF16)

    out_spec = pl.BlockSpec((bm, bn), lambda i, j, k: (i, j))
    res = pl.pallas_call(
        body, name=name, grid=(m // bm, n // bn, nk),
        in_specs=[pl.BlockSpec((bk, bm), lambda i, j, k: (k, i)), pl.BlockSpec((bk, bn), lambda i, j, k: (k, j))],
        out_specs=[out_spec, out_spec] if also_bf16 else out_spec,
        out_shape=([jax.ShapeDtypeStruct((m, n), F32), jax.ShapeDtypeStruct((m, n), BF16)] if also_bf16
                   else jax.ShapeDtypeStruct((m, n), F32)),
        compiler_params=_cparams("arbitrary", "arbitrary", "arbitrary"),
    )(a, b)
    return res


def _adamw_math(w, g, m, v):
    m = B1 * m + (1.0 - B1) * g
    v = B2 * v + (1.0 - B2) * (g * g)
    m_hat = m / (1.0 - B1 ** STEP)
    v_hat = v / (1.0 - B2 ** STEP)
    delta = -LR * (m_hat / (jnp.sqrt(v_hat) + AEPS) + WD * w)
    return delta, m, v


def _adamw(w, g, m, v, name):
    rows, cols = w.shape
    br = min(rows, 256)
    while rows % br:
        br -= 8

    def body(w_ref, g_ref, m_ref, v_ref, d_ref, mo_ref, vo_ref):
        d, mn, vn = _adamw_math(w_ref[...], g_ref[...], m_ref[...], v_ref[...])
        d_ref[...] = d
        mo_ref[...] = mn
        vo_ref[...] = vn

    spec = _row_spec(br, cols)
    sd = jax.ShapeDtypeStruct((rows, cols), F32)
    return pl.pallas_call(
        body, name=name, grid=(rows // br,), in_specs=[spec] * 4, out_specs=[spec] * 3,
        out_shape=[sd, sd, sd], compiler_params=_cparams("arbitrary"),
    )(w, g, m, v)


def _local_step(x, hn1, target, small, win_t, rest_weights, early_grads=None, after_attention_bwd=None,
                late_grads=None):
    slopes = jnp.asarray(_alibi_slopes(NH) * np.float32(LOG2E))
    q, k, v, u, z = _inproj_fwd(hn1, win_t)
    outs, lses = [], []
    for i, dil in enumerate(DILS):
        o, l = _attn_fwd(q[i], k[i], v[i], slopes, dil)
        outs.append(o)
        lses.append(l)
    wout, wff1, wff2 = rest_weights(functools.reduce(lambda a, b: a + b, [l[0, 0:8, :] for l in lses]))
    attn, lse, mixed, h1 = _mix_fwd(outs, lses, u, z, x, small["ln_g"], small["ln_b"], small["sgu_w"],
                                    small["bias_t"], small["attn_out_g"], small["gmlp_out_g"], wout)
    hn2, rf, dh2, loss, dgf = _mlp_fwd(h1, small["norm2_g"], wff1, wff2, small["final_norm_g"], target)
    df, dh1, dg2 = _mlp_bwd(dh2, rf, h1, small["norm2_g"], wff1, wff2)
    gwff1 = _wgrad(hn2, df, "wgrad_ff1", D, 1024)
    gwff2 = _wgrad(rf, dh2, "wgrad_ff2", 1024, D, square_a=True)
    gwout = _wgrad(mixed, dh1, "wgrad_out", D, D)
    ga, g1 = small["attn_out_g"], small["norm1_g"]
    pin = early_grads(gwff1, gwff2, gwout) if early_grads else None
    if pin is not None:
        ga = ga + pin
    (do, delta, du, dz, dga, dgg, dlng, dlnb, dws, db) = _mix_bwd(
        dh1, attn, u, z, small["ln_g"], small["ln_b"], small["sgu_w"], small["sgu_wt"], small["bias_t"],
        ga, small["gmlp_out_g"], wout)
    dqs, dks, dvs = [], [], []
    for i, dil in enumerate(DILS):
        dqs.append(_attn_bwd_dq(q[i], k[i], v[i], do[i], lse[i], delta[i], slopes, dil))
        dk, dv = _attn_bwd_dkv(q[i], k[i], v[i], do[i], lse[i], delta[i], slopes, dil)
        dks.append(dk)
        dvs.append(dv)
    marker = functools.reduce(lambda a, b: a + b, [t[0, 0:8, 0:LANES] for t in dqs + dks + dvs])
    partial = dict(ln_g=dlng, ln_b=dlnb, sgu_w=dws, sgu_b=db[:, :NG].T, attn_out_g=dga, gmlp_out_g=dgg,
                   norm2_g=dg2, final_norm_g=dgf)
    pin = after_attention_bwd(marker, partial, loss[0, 0]) if after_attention_bwd else None
    dproj = _dproj_merge(dqs, dks, dvs, du, dz, jnp.zeros((1, 1), F32) if pin is None else pin)
    gwin_t, gwin_low = _wgrad(dproj, hn1, "wgrad_in", INW // 2, D, also_bf16=True)
    pin = late_grads(gwin_t, gwin_low) if late_grads else None
    if pin is not None:
        g1 = g1 + pin
    dx, dg1 = _inproj_bwd(dproj, dh1, x, g1, win_t)
    small_grads = dict(partial, norm1_g=dg1)
    return loss[0, 0], dx, small_grads, (gwin_t, gwout, gwff1, gwff2)


ANY = pl.BlockSpec(memory_space=pl.ANY)
NDEV = 8


def _position():
    return lax.axis_index("x"), lax.axis_index("y"), lax.axis_index("c")


def _other_chips(x, y):
    return [(1 - x, y), (x, 1 - y), (1 - x, 1 - y)]


def _remote(src, dst, send_sem, recv_sem, device):
    return pltpu.make_async_remote_copy(src_ref=src, dst_ref=dst, send_sem=send_sem, recv_sem=recv_sem,
                                        device_id=device, device_id_type=MESH)


HBM = pl.BlockSpec(memory_space=pltpu.HBM)
SEM = pl.BlockSpec(memory_space=pltpu.SEMAPHORE)
DATAFLOW = pltpu.SideEffectType.DATAFLOW_SIDE_EFFECTING


def _in_hbm(a):
    return pltpu.with_memory_space_constraint(a, pltpu.HBM)


def _gather_start(shards, name):
    n = len(shards)
    lands = [jnp.broadcast_to(sh[None], (NCHIP,) + sh.shape) for sh in shards]

    def body(*refs):
        w_refs, land_refs = refs[:n], refs[n:2 * n]
        send_sems, recv_sems = refs[2 * n:2 * n + 2]
        token = refs[-1]
        x, y, c = _position()
        for w in range(n):
            for k, (px, py) in enumerate(_other_chips(x, y)):
                m = 3 * w + k
                _remote(w_refs[w], land_refs[w].at[2 * x + y], send_sems.at[m], recv_sems.at[m], (px, py, c)).start()
        token[...] = jnp.zeros_like(token)

    res = _split_call(body, name, list(shards) + lands, (3 * n, 3 * n), (TOKEN,))
    return res[0], res[1], res[2:2 + n], res[2 + n:2 + 2 * n], res[-1]


def _gather_wait(send_sems, recv_sems, shards, lands, after, name):
    n = len(shards)

    def body(*refs):
        w_refs, land_refs = refs[:n], refs[n:2 * n]
        send_sems, recv_sems = refs[2 * n:2 * n + 2]
        x, y, c = _position()
        for w in range(n):
            for k, (px, py) in enumerate(_other_chips(x, y)):
                m = 3 * w + k
                cp = _remote(w_refs[w], land_refs[w].at[2 * px + py], send_sems.at[m], recv_sems.at[m], (px, py, c))
                cp.wait_send()
                cp.wait_recv()

    operands = list(shards) + list(lands)
    res = pl.pallas_call(
        body, name=name, out_shape=tuple(pltpu.HBM(a.shape, a.dtype) for a in operands),
        in_specs=(HBM,) * (2 * n) + (SEM, SEM, ANY), out_specs=(HBM,) * (2 * n),
        input_output_aliases={i: i for i in range(2 * n)},
        compiler_params=pltpu.CompilerParams(has_side_effects=DATAFLOW),
    )(*operands, send_sems, recv_sems, after)
    return res[n:]


def _xor_peers(x, y, c):
    peers = []
    for k in range(1, NDEV):
        kx, ky, kc = (k >> 2) & 1, (k >> 1) & 1, k & 1
        peers.append((1 - x if kx else x, 1 - y if ky else y, 1 - c if kc else c))
    return peers


def _piece(part_ref, px, py, pc):
    slab = 2 * px + py
    if len(part_ref.shape) == 3:
        half = part_ref.shape[1] // 2
        return part_ref.at[slab, pl.ds(pc * half, half), :]
    half = part_ref.shape[0] // 2
    return part_ref.at[pl.ds(pc * half, half), pl.ds(pl.multiple_of(slab * D, D), D)]


def _split_call(body, name, operands, n_sems, extra_out=()):
    n = len(operands)
    sems = tuple(pltpu.SemaphoreType.DMA((m,)) for m in n_sems)
    thru = tuple(pltpu.HBM(a.shape, a.dtype) for a in operands)
    return pl.pallas_call(
        body, name=name, out_shape=sems + thru + tuple(extra_out),
        in_specs=(HBM,) * n,
        out_specs=(SEM,) * len(sems) + (HBM,) * n + (pl.BlockSpec(memory_space=pltpu.VMEM),) * len(extra_out),
        input_output_aliases={i: len(sems) + i for i in range(n)},
        compiler_params=pltpu.CompilerParams(has_side_effects=DATAFLOW),
    )(*[_in_hbm(a) for a in operands])


TOKEN = jax.ShapeDtypeStruct((8, LANES), F32)


def _pack_copies(pack_ref, land_ref, send_sems, recv_sems, base, position, start):
    x, y, c = position
    for k, (px, py, pc) in enumerate(_xor_peers(x, y, c)):
        if start:
            _remote(pack_ref, land_ref.at[4 * x + 2 * y + c], send_sems.at[base + k], recv_sems.at[base + k],
                    (px, py, pc)).start()
        else:
            cp = _remote(pack_ref, land_ref.at[4 * px + 2 * py + pc], send_sems.at[base + k], recv_sems.at[base + k],
                         (px, py, pc))
            cp.wait_send()
            cp.wait_recv()


def _pack_landing(pack):
    return jnp.broadcast_to(pack[None], (NDEV,) + pack.shape)


def _reduce_start(parts, name, pack=None):
    nw = len(parts)
    lands = [lax.empty((NDEV - 1, p.shape[-2] // 2, D), p.dtype) for p in parts]
    operands = list(parts) + lands + ([pack, _pack_landing(pack)] if pack is not None else [])
    nops = len(operands)

    def body(*refs):
        part_refs, land_refs = refs[:nw], refs[nw:2 * nw]
        send_sems, recv_sems = refs[nops:nops + 2]
        token = refs[-1]
        x, y, c = _position()
        for w in range(nw):
            for k, peer in enumerate(_xor_peers(x, y, c)):
                n = w * (NDEV - 1) + k
                _remote(_piece(part_refs[w], *peer), land_refs[w].at[k], send_sems.at[n], recv_sems.at[n],
                        peer).start()
        if pack is not None:
            _pack_copies(refs[2 * nw], refs[2 * nw + 1], send_sems, recv_sems, nw * (NDEV - 1), (x, y, c), True)
        token[...] = jnp.zeros_like(token)

    n = (nw + (pack is not None)) * (NDEV - 1)
    res = _split_call(body, name, operands, (n, n), (TOKEN,))
    return res[0], res[1], res[2:2 + nops], res[-1]


def _reduce_wait(send_sems, recv_sems, operands, nw, after, name):
    nops = len(operands)
    has_pack = nops > 2 * nw

    def body(*refs):
        part_refs, land_refs = refs[:nw], refs[nw:2 * nw]
        send_sems, recv_sems = refs[nops:nops + 2]
        x, y, c = _position()
        for w in range(nw):
            for k, peer in enumerate(_xor_peers(x, y, c)):
                n = w * (NDEV - 1) + k
                cp = _remote(_piece(part_refs[w], *peer), land_refs[w].at[k], send_sems.at[n], recv_sems.at[n], peer)
                cp.wait_send()
                cp.wait_recv()
        if has_pack:
            _pack_copies(refs[2 * nw], refs[2 * nw + 1], send_sems, recv_sems, nw * (NDEV - 1), (x, y, c), False)

    res = pl.pallas_call(
        body, name=name, out_shape=tuple(pltpu.HBM(a.shape, a.dtype) for a in operands),
        in_specs=(HBM,) * nops + (SEM, SEM, ANY), out_specs=(HBM,) * nops,
        input_output_aliases={i: i for i in range(nops)},
        compiler_params=pltpu.CompilerParams(has_side_effects=DATAFLOW),
    )(*operands, send_sems, recv_sems, after)
    return res[:nw], res[nw:2 * nw], (res[2 * nw + 1] if has_pack else None)


def _sum_pieces(part, land, sel, name):
    half = part.shape[-2] // 2
    br = 128 if half % 128 == 0 else half // 2
    nb = half // br

    def body(sel_ref, own_ref, *refs):
        acc = own_ref[...]
        for r in refs[:NDEV - 1]:
            acc = acc + r[...].astype(F32)
        refs[NDEV - 1][...] = acc

    if part.ndim == 3:
        own_spec = pl.BlockSpec((None, br, D), lambda i, sel_ref: (sel_ref[0], sel_ref[1] * nb + i, 0))
    else:
        own_spec = pl.BlockSpec((br, D), lambda i, sel_ref: (sel_ref[1] * nb + i, sel_ref[0]))
    slot_specs = [pl.BlockSpec((None, br, D), functools.partial(lambda i, sel_ref, k: (k, i, 0), k=k))
                  for k in range(NDEV - 1)]
    return pl.pallas_call(
        body, name=name,
        grid_spec=pltpu.PrefetchScalarGridSpec(
            num_scalar_prefetch=1, grid=(nb,), in_specs=[own_spec] + slot_specs,
            out_specs=pl.BlockSpec((br, D), lambda i, sel_ref: (i, 0))),
        out_shape=jax.ShapeDtypeStruct((half, D), F32),
        compiler_params=_cparams("arbitrary"),
    )(sel, part, *([land] * (NDEV - 1)))


def _share_start(halves, name, pack=None):
    nw = len(halves)
    lands = [lax.empty(h.shape, F32) for h in halves]
    operands = list(halves) + lands + ([pack, _pack_landing(pack)] if pack is not None else [])
    nops = len(operands)

    def body(*refs):
        h_refs, land_refs = refs[:nw], refs[nw:2 * nw]
        send_sems, recv_sems = refs[nops:nops + 2]
        token = refs[-1]
        x, y, c = _position()
        for w in range(nw):
            _remote(h_refs[w], land_refs[w], send_sems.at[w], recv_sems.at[w], (x, y, 1 - c)).start()
        if pack is not None:
            _pack_copies(refs[2 * nw], refs[2 * nw + 1], send_sems, recv_sems, nw, (x, y, c), True)
        token[...] = jnp.zeros_like(token)

    n = nw + (NDEV - 1 if pack is not None else 0)
    res = _split_call(body, name, operands, (n, n), (TOKEN,))
    return res[0], res[1], res[2:2 + nops], res[-1]


def _share_wait(send_sems, recv_sems, operands, nw, after, name):
    nops = len(operands)
    has_pack = nops > 2 * nw

    def body(*refs):
        h_refs, land_refs = refs[:nw], refs[nw:2 * nw]
        send_sems, recv_sems = refs[nops:nops + 2]
        x, y, c = _position()
        for w in range(nw):
            cp = _remote(h_refs[w], land_refs[w], send_sems.at[w], recv_sems.at[w], (x, y, 1 - c))
            cp.wait_send()
            cp.wait_recv()
        if has_pack:
            _pack_copies(refs[2 * nw], refs[2 * nw + 1], send_sems, recv_sems, nw, (x, y, c), False)

    res = pl.pallas_call(
        body, name=name, out_shape=tuple(pltpu.HBM(a.shape, a.dtype) for a in operands),
        in_specs=(HBM,) * nops + (SEM, SEM, ANY), out_specs=(HBM,) * nops,
        input_output_aliases={i: i for i in range(nops)},
        compiler_params=pltpu.CompilerParams(has_side_effects=DATAFLOW),
    )(*operands, send_sems, recv_sems, after)
    return res[:nw], res[nw:2 * nw], (res[2 * nw + 1] if has_pack else None)


def _join_halves(own, other, c):
    first = jnp.where(c == 0, own, other)
    second = jnp.where(c == 0, other, own)
    return jnp.concatenate([first, second], axis=0)


SMALL_SIZES = (("norm1_g", D), ("sgu_w", NG * CHUNK * CHUNK), ("norm2_g", D), ("final_norm_g", D),
               ("sgu_b", NG * CHUNK), ("attn_out_g", A), ("sgu_ln_g", GW), ("sgu_ln_b", GW), ("gmlp_out_g", GW))
PARAM_ROWS = sum(n for _, n in SMALL_SIZES) // LANES
SMALL_ROWS = PARAM_ROWS + 8


def _pack_small(tree, first_extra=None):
    extra = jnp.zeros((8 * LANES,), F32)
    if first_extra is not None:
        extra = extra.at[0].set(first_extra)
    flat = jnp.concatenate([tree[n].reshape(-1) for n, _ in SMALL_SIZES] + [extra])
    return flat.reshape(SMALL_ROWS, LANES)


def _written_shape(shape):
    core = tuple(shape)
    while len(core) > 2 and core[0] == 1:
        core = core[1:]
    if len(core) >= 2 and (core[-1] == LANES or (core[-1] % LANES == 0 and math.prod(core[:-1]) == 1)):
        return core
    if len(core) == 2 and core[-1] * 2 == LANES and core[0] % 2 == 0:
        return core
    return (math.prod(shape) // LANES, LANES)


def _store_small(out_ref, pack_ref, row, shape):
    if len(shape) == 3:
        for g in range(shape[0]):
            out_ref[g] = pack_ref[row + g * shape[1]:row + (g + 1) * shape[1], :]
    elif shape[-1] == LANES:
        out_ref[...] = pack_ref[row:row + shape[0], :]
    elif shape[-1] * 2 == LANES:
        base = row - row % 8
        assert row + shape[0] // 2 <= base + 8
        tile = pack_ref[base:base + 8, :]
        shifted = pltpu.roll(tile, LANES // 2, 1)
        for g in range(shape[0]):
            src = tile if g % 2 == 0 else shifted
            r = row - base + g // 2
            out_ref[g:g + 1, :] = src[r:r + 1, 0:LANES // 2]
    else:
        for j in range(shape[1] // LANES):
            out_ref[:, j * LANES:(j + 1) * LANES] = pack_ref[row + j:row + j + 1, :]


def _small_finish(pack_land, norm_land, wpack, mpack, vpack, shapes):
    written = [_written_shape(shapes[n]) for n, _ in SMALL_SIZES]
    nsmall = len(SMALL_SIZES)

    def body(*refs):
        p_ref, n_ref, w_ref, m_ref, v_ref, loss_ref = refs[:6]
        outs = refs[6:6 + 4 * nsmall]
        packs = refs[6 + 4 * nsmall:]
        go_ref = packs[0]
        total = p_ref[0]
        late = n_ref[0]
        for k in range(1, NDEV):
            total = total + p_ref[k]
            late = late + n_ref[k]
        go_ref[...] = total
        go_ref[0:8, :] = total[0:8, :] + late
        d, mn, vn = _adamw_math(w_ref[...], go_ref[...], m_ref[...], v_ref[...])
        packs[1][...] = d
        packs[2][...] = mn
        packs[3][...] = vn
        loss_ref[...] = go_ref[PARAM_ROWS:SMALL_ROWS, :]
        for kind in range(4):
            row = 0
            for j, (_, size) in enumerate(SMALL_SIZES):
                _store_small(outs[kind * nsmall + j], packs[kind], row, written[j])
                row += size // LANES

    vm = pl.BlockSpec(memory_space=pltpu.VMEM)
    out_shape = [jax.ShapeDtypeStruct((8, LANES), F32)] + [jax.ShapeDtypeStruct(w, F32) for w in written] * 4
    outs = pl.pallas_call(
        body, name="small_finish", in_specs=[vm] * 5, out_specs=[vm] * len(out_shape), out_shape=out_shape,
        scratch_shapes=[pltpu.VMEM((SMALL_ROWS, LANES), F32)] * 4,
        compiler_params=_cparams(),
    )(pack_land, norm_land, wpack, mpack, vpack)
    trees = [{n: outs[1 + kind * nsmall + j].reshape(shapes[n]) for j, (n, _) in enumerate(SMALL_SIZES)}
             for kind in range(4)]
    return outs[0], trees


def kernel(x, norm1_g, w_in, sgu_ln_g, sgu_ln_b, sgu_w, sgu_b, attn_out_g, gmlp_out_g, w_out, norm2_g, w_ff1, w_ff2, final_norm_g, loss_target, m_norm1_g, m_w_in, m_sgu_ln_g, m_sgu_ln_b, m_sgu_w, m_sgu_b, m_attn_out_g, m_gmlp_out_g, m_w_out, m_norm2_g, m_w_ff1, m_w_ff2, m_final_norm_g, v_norm1_g, v_w_in, v_sgu_ln_g, v_sgu_ln_b, v_sgu_w, v_sgu_b, v_attn_out_g, v_gmlp_out_g, v_w_out, v_norm2_g, v_w_ff1, v_w_ff2, v_final_norm_g):
    names = [n for n, _ in SMALL_SIZES]
    w_small = dict(norm1_g=norm1_g, sgu_ln_g=sgu_ln_g, sgu_ln_b=sgu_ln_b, sgu_w=sgu_w, sgu_b=sgu_b,
                   attn_out_g=attn_out_g, gmlp_out_g=gmlp_out_g, norm2_g=norm2_g, final_norm_g=final_norm_g)
    m_small = dict(norm1_g=m_norm1_g, sgu_ln_g=m_sgu_ln_g, sgu_ln_b=m_sgu_ln_b, sgu_w=m_sgu_w, sgu_b=m_sgu_b,
                   attn_out_g=m_attn_out_g, gmlp_out_g=m_gmlp_out_g, norm2_g=m_norm2_g,
                   final_norm_g=m_final_norm_g)
    v_small = dict(norm1_g=v_norm1_g, sgu_ln_g=v_sgu_ln_g, sgu_ln_b=v_sgu_ln_b, sgu_w=v_sgu_w, sgu_b=v_sgu_b,
                   attn_out_g=v_attn_out_g, gmlp_out_g=v_gmlp_out_g, norm2_g=v_norm2_g,
                   final_norm_g=v_final_norm_g)
    shapes = {n: w_small[n].shape for n in names}

    start_in = _gather_start([w_in[0].T.astype(BF16)], "gather_in_start")
    issued = start_in[4][0:1, 0:1]
    start_rest = _gather_start([(w_out[0] + issued).astype(BF16), w_ff1[0].astype(BF16), w_ff2[0].astype(BF16)],
                               "gather_rest_start")
    hn1 = _norm1(x[0], norm1_g + start_rest[4][0:1, 0:1])
    win_t = _gather_wait(*start_in[:4], after=hn1, name="gather_in_wait")[0].reshape(INW, D)

    def rest_weights(after):
        wout, wff1, wff2 = _gather_wait(*start_rest[:4], after=after, name="gather_rest_wait")
        return wout.reshape(D, D), wff1, wff2.reshape(DFF, D)

    small = dict(
        norm1_g=norm1_g, ln_g=sgu_ln_g.reshape(1, GW), ln_b=sgu_ln_b.reshape(1, GW), sgu_w=sgu_w[0],
        sgu_wt=jnp.swapaxes(sgu_w[0], 1, 2), bias_t=jnp.repeat(sgu_b[0].T, DH, axis=1),
        attn_out_g=attn_out_g, gmlp_out_g=gmlp_out_g, norm2_g=norm2_g, final_norm_g=final_norm_g.reshape(1, D))
    xi, yi, ci = _position()
    sel = jnp.stack([2 * xi + yi, ci]).astype(jnp.int32)
    state = {}

    def as_slabs(g):
        return g.reshape(NCHIP, g.shape[0] // NCHIP, D)

    def early_grads(gwff1, gwff2, gwout):
        state["early"] = _reduce_start([gwff1, as_slabs(gwff2), as_slabs(gwout)], "reduce_early_start")
        return state["early"][3][0:1, 0:1]

    def after_attention_bwd(marker, partial, loss_part):
        send_sems, recv_sems, operands, _ = state["early"]
        parts, lands, _ = _reduce_wait(send_sems, recv_sems, operands, 3, marker, "reduce_early_wait")
        halves = [_sum_pieces(p, l, sel, "sum_" + n) for p, l, n in zip(parts, lands, ("w_ff1", "w_ff2", "w_out"))]
        pack = _pack_small(dict(partial, norm1_g=jnp.zeros((1, D), F32), sgu_ln_g=partial["ln_g"],
                                sgu_ln_b=partial["ln_b"]), loss_part)
        state["early_share"] = _share_start(halves, "share_early_start", pack)
        return state["early_share"][3][0:1, 0:1]

    def late_grads(gwin_t, gwin_low):
        state["late"] = _reduce_start([as_slabs(gwin_low)], "reduce_late_start")
        state["late_own"] = as_slabs(gwin_t)
        return state["late"][3][0:1, 0:1]

    _, dx, sg, _ = _local_step(
        x[0], hn1, loss_target[0], small, win_t, rest_weights, early_grads, after_attention_bwd, late_grads)
    send_sems, recv_sems, operands, _ = state["early_share"]
    own, other, pack_land = _share_wait(send_sems, recv_sems, operands, 3, dx, "share_early_wait")
    send_sems, recv_sems, operands, _ = state["late"]
    _, late_lands, _ = _reduce_wait(send_sems, recv_sems, operands, 1, dx, "reduce_late_wait")
    late_share = _share_start([_sum_pieces(state["late_own"], late_lands[0], sel, "sum_w_in")], "share_late_start",
                              sg["norm1_g"].reshape(8, LANES))
    issued = late_share[3][0:1, 0:1]
    g_big = {n: _join_halves(o, t, ci) + issued for n, o, t in zip(("w_ff1", "w_ff2", "w_out"), own, other)}
    w_big = dict(w_in=(w_in, m_w_in, v_w_in), w_out=(w_out, m_w_out, v_w_out),
                 w_ff1=(w_ff1, m_w_ff1, v_w_ff1), w_ff2=(w_ff2, m_w_ff2, v_w_ff2))
    grads, deltas, new_m, new_v = {}, {}, {}, {}

    def update(n):
        w, m, v = w_big[n]
        d, mn, vn = _adamw(w[0], g_big[n], m[0], v[0], "adamw_" + n)
        grads[n], deltas[n], new_m[n], new_v[n] = g_big[n][None], d[None], mn[None], vn[None]

    for n in ("w_ff1", "w_ff2", "w_out"):
        update(n)
    updated = deltas["w_out"][0, 0:8, 0:LANES] + deltas["w_ff1"][0, 0:8, 0:LANES] + deltas["w_ff2"][0, 0:8, 0:LANES]
    own, other, norm_land = _share_wait(late_share[0], late_share[1], late_share[2], 1, updated, "share_late_wait")
    g_big["w_in"] = _join_halves(own[0], other[0], ci).T
    update("w_in")

    loss_tile, small_trees = _small_finish(pack_land, norm_land, _pack_small(w_small), _pack_small(m_small),
                                           _pack_small(v_small), shapes)
    loss = loss_tile[0, 0]
    for tree, small_tree in zip((grads, deltas, new_m, new_v), small_trees):
        tree.update(small_tree)

    order = ["norm1_g", "w_in", "sgu_ln_g", "sgu_ln_b", "sgu_w", "sgu_b", "attn_out_g", "gmlp_out_g", "w_out",
             "norm2_g", "w_ff1", "w_ff2", "final_norm_g"]
    return (loss, dx[None], *[grads[n] for n in order], *[deltas[n] for n in order],
            *[new_m[n] for n in order], *[new_v[n] for n in order])
```

```python
import functools
import math

import numpy as np
import jax
import jax.numpy as jnp
from jax import lax
from jax.experimental import pallas as pl
from jax.experimental.pallas import tpu as pltpu

F32 = jnp.float32
BF16 = jnp.bfloat16

D = 1024
NH = 12
DH = 64
A = NH * DH
NG = 4
GW = NG * DH
INW = 3 * A + 2 * GW
DFF = 4 * D
CHUNK = 128
PATTERNS = ((128, 1), (512, 4), (2048, 16))
EPS = 1e-6
SCALE = DH ** -0.5
LOG2E = 1.0 / math.log(2.0)
LN2 = math.log(2.0)
NEG = -1e30

LR, B1, B2, AEPS, WD, STEP = 0.001, 0.9, 0.999, 1e-08, 0.01, 10

TM = 512
TM_BIG = 1024
TMX = 512
ATT_ROWS = 4096
FF_CH = 1024
LANES = 128
NCHIP = 4
VMEM_LIMIT = 56 * 1024 * 1024
MESH = pl.DeviceIdType.MESH


def _cparams(*sem, **kw):
    return pltpu.CompilerParams(dimension_semantics=sem if sem else None,
                                vmem_limit_bytes=VMEM_LIMIT, **kw)


def _dot(a, b):
    return jnp.dot(a, b, preferred_element_type=F32)


def _dot_nt(a, b):
    return lax.dot_general(a, b, (((1,), (1,)), ((), ())), preferred_element_type=F32)


def _dot_tn(a, b):
    return lax.dot_general(a, b, (((0,), (0,)), ((), ())), preferred_element_type=F32)


def _dot_hi(a, b):
    bb = b.astype(BF16)
    return functools.reduce(lambda x, y: x + y, [_dot(piece, bb) for piece in _bf16_pieces(a, 3)])


def _alibi_slopes(n):
    def pow2(m):
        start = 2.0 ** (-8.0 / m)
        return [start ** (i + 1) for i in range(m)]
    if math.log2(n).is_integer():
        s = pow2(n)
    else:
        c = 2 ** int(math.floor(math.log2(n)))
        s = pow2(c) + pow2(2 * c)[0::2][: n - c]
    return np.asarray(s, dtype=np.float32)


def _rms_fwd(v, g):
    r = lax.rsqrt(jnp.mean(v * v, axis=-1, keepdims=True) + EPS)
    vn = v * r
    return vn * g, vn, r


def _rms_bwd(dy, vn, r, g):
    w = dy * g
    dv = r * (w - vn * jnp.mean(w * vn, axis=-1, keepdims=True))
    return dv, jnp.sum(dy * vn, axis=0, keepdims=True)


_K0 = math.sqrt(2.0 / math.pi)
_K1 = 0.044715


def _gelu(v):
    return 0.5 * v * (1.0 + jnp.tanh(_K0 * (v + _K1 * (v * v * v))))


def _gelu_grad(v):
    t = jnp.tanh(_K0 * (v + _K1 * (v * v * v)))
    return 0.5 * (1.0 + t) + 0.5 * v * (1.0 - t * t) * (_K0 * (1.0 + 3.0 * _K1 * v * v))


def _row_spec(rows, cols):
    return pl.BlockSpec((rows, cols), lambda i: (i, 0))


def _const_spec(shape):
    nd = len(shape)
    return pl.BlockSpec(shape, lambda i: (0,) * nd, pipeline_mode=pl.Buffered(1))


DILS = tuple(d for _, d in PATTERNS)


def _fill_cols(scr, value):
    for cb in range(value.shape[1] // LANES):
        scr[cb] = value[:, cb * LANES:(cb + 1) * LANES]


def _split_residues(scr, out_ref, dil):
    nb, rows, _ = scr.shape
    for r in range(dil):
        for cb in range(nb):
            piece = scr.at[cb][pl.ds(r, rows // dil, stride=dil), :]
            out_ref[r, :, cb * LANES:(cb + 1) * LANES] = piece.astype(out_ref.dtype)


def _merge_residues(in_ref, scr, dil):
    nb, rows, _ = scr.shape
    for r in range(dil):
        for cb in range(nb):
            scr.at[cb][pl.ds(r, rows // dil, stride=dil), :] = in_ref[r, :, cb * LANES:(cb + 1) * LANES].astype(F32)
    return jnp.concatenate([scr[cb] for cb in range(nb)], axis=-1)


def _col_scratch(rows, width):
    return pltpu.VMEM((width // LANES, rows, LANES), F32)


def _res_spec(dil, rows, width):
    return pl.BlockSpec((dil, rows // dil, width), lambda i: (0, i, 0))


def _res_shape(s, dil, width, dtype):
    return jax.ShapeDtypeStruct((dil, s // dil, width), dtype)


def _norm1(x, g1):
    s = x.shape[0]

    def body(x_ref, g_ref, hn_ref):
        hn, _, _ = _rms_fwd(x_ref[...], g_ref[...])
        hn_ref[...] = hn.astype(BF16)

    return pl.pallas_call(
        body, name="norm1", grid=(s // TM,), in_specs=[_row_spec(TM, D), _const_spec((1, D))],
        out_specs=_row_spec(TM, D), out_shape=jax.ShapeDtypeStruct((s, D), BF16),
        compiler_params=_cparams("arbitrary"),
    )(x, g1)


def _inproj_fwd(hn1, win_t):
    s = hn1.shape[0]
    nd = len(DILS)

    def body(hn_ref, w_ref, *rest):
        qkv_refs = rest[:3 * nd]
        u_ref, z_ref, scr = rest[3 * nd:]
        hn = hn_ref[...]
        for t in range(3):
            seg = _dot_nt(hn, w_ref[t * A:(t + 1) * A, :])
            seg = seg * (SCALE * LOG2E) if t == 0 else seg
            _fill_cols(scr, seg)
            for di, dil in enumerate(DILS):
                if dil == 1:
                    qkv_refs[t * nd + di][0] = seg.astype(BF16)
                else:
                    _split_residues(scr, qkv_refs[t * nd + di], dil)
        u_ref[...] = _dot_nt(hn, w_ref[3 * A:3 * A + GW, :])
        z_ref[...] = _dot_nt(hn, w_ref[3 * A + GW:INW, :])

    res = pl.pallas_call(
        body, name="inproj_fwd", grid=(s // TM_BIG,),
        in_specs=[_row_spec(TM_BIG, D), _const_spec((INW, D))],
        out_specs=[_res_spec(d, TM_BIG, A) for _ in range(3) for d in DILS]
                  + [_row_spec(TM_BIG, GW), _row_spec(TM_BIG, GW)],
        out_shape=[_res_shape(s, d, A, BF16) for _ in range(3) for d in DILS]
                  + [jax.ShapeDtypeStruct((s, GW), F32)] * 2,
        scratch_shapes=[_col_scratch(TM_BIG, A)],
        compiler_params=_cparams("arbitrary"),
    )(hn1, win_t)
    q, k, v = (res[t * nd:(t + 1) * nd] for t in range(3))
    return q, k, v, res[-2], res[-1]


def _att_geometry(length, dil):
    merge = max(1, min(dil, ATT_ROWS // length))
    rows = min(length * merge, ATT_ROWS)
    nsub = rows // CHUNK
    return merge, rows, length * merge // rows, nsub, min(length // CHUNK, nsub)


def _merged(t, merge):
    return t.reshape(t.shape[0] // merge, t.shape[1] * merge, t.shape[2])


def _stack_heads(t):
    lane = lax.broadcasted_iota(jnp.int32, t.shape, 1)
    zero = jnp.zeros_like(t)
    return jnp.concatenate([jnp.where(lane < DH, t, zero), jnp.where(lane >= DH, t, zero)], axis=0)


def _head_cols(t, hp):
    lane = lax.broadcasted_iota(jnp.int32, t.shape, 1)
    cols = [jnp.sum(jnp.where(lane == 2 * hp + h, t, 0.0), axis=-1, keepdims=True) for h in range(2)]
    return jnp.concatenate(cols, axis=0)


def _unstack_heads(t2):
    n = t2.shape[0] // 2
    lane = lax.broadcasted_iota(jnp.int32, (n, LANES), 1)
    return jnp.where(lane < DH, t2[:n], t2[n:])


def _query_window_bias(s0, s1, dil, first):
    row = lax.broadcasted_iota(jnp.int32, (2 * CHUNK, 2 * CHUNK), 0)
    col = lax.broadcasted_iota(jnp.int32, (2 * CHUNK, 2 * CHUNK), 1)
    steps = (row & (CHUNK - 1)) + CHUNK - col
    valid = (steps >= 0) & (steps <= CHUNK)
    if first:
        valid = valid & (col >= CHUNK)
    slope = jnp.where(row < CHUNK, s0, s1)
    return jnp.where(valid, -slope * (steps * dil).astype(F32), NEG)


def _key_block_bias(s0, s1, dil, last):
    key = lax.broadcasted_iota(jnp.int32, (CHUNK, 4 * CHUNK), 0)
    col = lax.broadcasted_iota(jnp.int32, (CHUNK, 4 * CHUNK), 1)
    wq = col & (2 * CHUNK - 1)
    steps = wq - key
    valid = (steps >= 0) & (steps <= CHUNK)
    if last:
        valid = valid & (wq < CHUNK)
    slope = jnp.where(col < 2 * CHUNK, s0, s1)
    return jnp.where(valid, -slope * (steps * dil).astype(F32), NEG)


def _head_rows(t, hp):
    row = lax.broadcasted_iota(jnp.int32, (8, LANES), 0)
    lane = lax.broadcasted_iota(jnp.int32, (8, LANES), 1)
    pick = jnp.where((row < 2) & (lane == 2 * hp + row), 1.0, 0.0).astype(BF16)
    hi = t.astype(BF16)
    rest = t - hi.astype(F32)
    mid = rest.astype(BF16)
    low = (rest - mid.astype(F32)).astype(BF16)
    return _dot_nt(pick, hi) + _dot_nt(pick, mid) + _dot_nt(pick, low)


def _att_specs(dil, rows, nsub, nblk):
    main = pl.BlockSpec((None, rows, LANES), lambda r, c, hp: (r, c, hp))
    prev = pl.BlockSpec((None, CHUNK, LANES), lambda r, c, hp: (r, jnp.maximum(c * nsub - 1, 0), hp))
    nxt = pl.BlockSpec((None, CHUNK, LANES), lambda r, c, hp: (r, jnp.minimum((c + 1) * nsub, nblk - 1), hp))
    main_heads = pl.BlockSpec((None, rows, LANES), lambda r, c, hp: (r, c, 0))
    nxt_heads = pl.BlockSpec((None, CHUNK, LANES), lambda r, c, hp: (r, jnp.minimum((c + 1) * nsub, nblk - 1), 0))
    return main, prev, nxt, main_heads, nxt_heads


def _row_start(i):
    return i * CHUNK if isinstance(i, int) else pl.multiple_of(i * CHUNK, CHUNK)


def _first_blocks(block, nsub, seg, nch, ch, first_bias, bias_buf):
    for i in range(nsub):
        if i % seg:
            block(i, bias_buf[...])
        elif nch == 1:
            block(i, first_bias())
        else:
            block(i, jnp.where(ch == 0, first_bias(), bias_buf[...]))


def _last_blocks(block, nsub, seg, nch, ch, last_bias, bias_buf):
    for i in range(nsub):
        if (i + 1) % seg:
            block(i, bias_buf[...])
        elif nch == 1:
            block(i, last_bias())
        else:
            block(i, jnp.where(ch == nch - 1, last_bias(), bias_buf[...]))


def _attn_fwd(q, k, v, slopes, dil):
    length = q.shape[1]
    merge, rows, nch, nsub, seg = _att_geometry(length, dil)
    main, prev, _, main_heads, _ = _att_specs(dil, rows, nsub, length * merge // CHUNK)
    q, k, v = (_merged(t, merge) for t in (q, k, v))

    def body(sl_ref, q_ref, k_ref, v_ref, kh_ref, vh_ref, o_ref, lse_ref, kbuf, vbuf, bias_buf):
        ch = pl.program_id(1)
        hp = pl.program_id(2)
        lane = lax.broadcasted_iota(jnp.int32, (CHUNK, LANES), 1)
        kbuf[0:CHUNK, :] = kh_ref[...]
        kbuf[CHUNK:, :] = k_ref[...]
        vbuf[0:CHUNK, :] = vh_ref[...]
        vbuf[CHUNK:, :] = v_ref[...]
        s0, s1 = sl_ref[2 * hp], sl_ref[2 * hp + 1]

        def block(i, bias):
            row = _row_start(i)
            rs = pl.ds(row, CHUNK)
            q2 = _stack_heads(q_ref[rs, :])
            kw = kbuf[pl.ds(row, 2 * CHUNK), :]
            vw = vbuf[pl.ds(row, 2 * CHUNK), :]
            sc = _dot_nt(q2, kw) + bias
            m = jnp.max(sc, axis=-1, keepdims=True)
            p = jnp.exp2(sc - m)
            l = jnp.sum(p, axis=-1, keepdims=True)
            o2 = _dot(p.astype(BF16), vw) * (1.0 / l)
            o_ref[rs, :] = _unstack_heads(o2).astype(BF16)
            lse = m + jnp.log2(l)
            seen = jnp.where(hp == 0, 0.0, lse_ref[rs, :])
            lse_ref[rs, :] = jnp.where(lane == 2 * hp, lse[:CHUNK], jnp.where(lane == 2 * hp + 1, lse[CHUNK:], seen))

        bias_buf[...] = _query_window_bias(s0, s1, dil, False)
        _first_blocks(block, nsub, seg, nch, ch, lambda: _query_window_bias(s0, s1, dil, True), bias_buf)

    sd = jax.ShapeDtypeStruct
    o, lse = pl.pallas_call(
        body, name=f"attn_fwd_d{dil}", grid=(dil // merge, nch, NH // 2),
        in_specs=[pl.BlockSpec(memory_space=pltpu.SMEM), main, main, main, prev, prev],
        out_specs=[main, main_heads],
        out_shape=[sd((dil // merge, length * merge, A), BF16), sd((dil // merge, length * merge, LANES), F32)],
        scratch_shapes=[pltpu.VMEM((rows + CHUNK, LANES), BF16), pltpu.VMEM((rows + CHUNK, LANES), BF16),
                        pltpu.VMEM((2 * CHUNK, 2 * CHUNK), F32)],
        compiler_params=_cparams("arbitrary", "arbitrary", "arbitrary"),
    )(slopes, q, k, v, k, v)
    return o.reshape(dil, length, A), lse.reshape(dil, length, LANES)


def _attn_bwd_dq(q, k, v, do, lse, delta, slopes, dil):
    length = q.shape[1]
    merge, rows, nch, nsub, seg = _att_geometry(length, dil)
    main, prev, _, main_heads, _ = _att_specs(dil, rows, nsub, length * merge // CHUNK)
    q, k, v, do, lse, delta = (_merged(t, merge) for t in (q, k, v, do, lse, delta))

    def body(sl_ref, q_ref, k_ref, v_ref, do_ref, lse_ref, dl_ref, kh_ref, vh_ref, dq_ref, kbuf, vbuf, bias_buf):
        ch = pl.program_id(1)
        hp = pl.program_id(2)
        kbuf[0:CHUNK, :] = kh_ref[...]
        kbuf[CHUNK:, :] = k_ref[...]
        vbuf[0:CHUNK, :] = vh_ref[...]
        vbuf[CHUNK:, :] = v_ref[...]
        s0, s1 = sl_ref[2 * hp], sl_ref[2 * hp + 1]

        def block(i, bias):
            row = _row_start(i)
            rs = pl.ds(row, CHUNK)
            q2 = _stack_heads(q_ref[rs, :])
            do2 = _stack_heads(do_ref[rs, :])
            lse2 = _head_cols(lse_ref[rs, :], hp)
            dl2 = _head_cols(dl_ref[rs, :], hp)
            kw = kbuf[pl.ds(row, 2 * CHUNK), :]
            vw = vbuf[pl.ds(row, 2 * CHUNK), :]
            p = jnp.exp2(_dot_nt(q2, kw) + bias - lse2)
            ds = p * (_dot_nt(do2, vw) - dl2)
            dq_ref[rs, :] = _unstack_heads(_dot(ds.astype(BF16), kw)).astype(BF16)

        bias_buf[...] = _query_window_bias(s0, s1, dil, False)
        _first_blocks(block, nsub, seg, nch, ch, lambda: _query_window_bias(s0, s1, dil, True), bias_buf)

    dq = pl.pallas_call(
        body, name=f"attn_dq_d{dil}", grid=(dil // merge, nch, NH // 2),
        in_specs=[pl.BlockSpec(memory_space=pltpu.SMEM), main, main, main, main, main_heads, main_heads, prev, prev],
        out_specs=main, out_shape=jax.ShapeDtypeStruct((dil // merge, length * merge, A), BF16),
        scratch_shapes=[pltpu.VMEM((rows + CHUNK, LANES), BF16), pltpu.VMEM((rows + CHUNK, LANES), BF16),
                        pltpu.VMEM((2 * CHUNK, 2 * CHUNK), F32)],
        compiler_params=_cparams("arbitrary", "arbitrary", "arbitrary"),
    )(slopes, q, k, v, do, lse, delta, k, v)
    return dq.reshape(dil, length, A)


def _attn_bwd_dkv(q, k, v, do, lse, delta, slopes, dil):
    length = q.shape[1]
    merge, rows, nch, nsub, seg = _att_geometry(length, dil)
    main, _, nxt, main_heads, nxt_heads = _att_specs(dil, rows, nsub, length * merge // CHUNK)
    q, k, v, do, lse, delta = (_merged(t, merge) for t in (q, k, v, do, lse, delta))

    def body(sl_ref, k_ref, v_ref, q_ref, do_ref, lse_ref, dl_ref, qh_ref, doh_ref, lseh_ref, dlh_ref,
             dk_ref, dv_ref, qbuf, dobuf, lse_rows, dl_rows, bias_buf):
        ch = pl.program_id(1)
        hp = pl.program_id(2)
        for buf, main_ref, halo_ref in ((qbuf, q_ref, qh_ref), (dobuf, do_ref, doh_ref)):
            buf[0:rows, :] = main_ref[...]
            buf[rows:, :] = halo_ref[...]
        for buf, main_ref, halo_ref in ((lse_rows, lse_ref, lseh_ref), (dl_rows, dl_ref, dlh_ref)):
            buf[:, 0:rows] = _head_rows(main_ref[...], hp)
            buf[:, rows:] = _head_rows(halo_ref[...], hp)
        s0, s1 = sl_ref[2 * hp], sl_ref[2 * hp + 1]

        def block(i, bias):
            row = _row_start(i)
            rs = pl.ds(row, CHUNK)
            win = pl.ds(row, 2 * CHUNK)
            kc = k_ref[rs, :]
            vc = v_ref[rs, :]
            q2 = _stack_heads(qbuf[win, :])
            do2 = _stack_heads(dobuf[win, :])
            cols = slice(i * CHUNK, (i + 2) * CHUNK)
            lse2 = jnp.concatenate([lse_rows[0:1, cols], lse_rows[1:2, cols]], axis=1)
            dl2 = jnp.concatenate([dl_rows[0:1, cols], dl_rows[1:2, cols]], axis=1)
            pt = jnp.exp2(_dot_nt(kc, q2) + bias - lse2)
            dst = pt * (_dot_nt(vc, do2) - dl2)
            dv_ref[rs, :] = _dot(pt.astype(BF16), do2).astype(BF16)
            dk_ref[rs, :] = (_dot(dst.astype(BF16), q2) * LN2).astype(BF16)

        bias_buf[...] = _key_block_bias(s0, s1, dil, False)
        _last_blocks(block, nsub, seg, nch, ch, lambda: _key_block_bias(s0, s1, dil, True), bias_buf)

    sd = jax.ShapeDtypeStruct((dil // merge, length * merge, A), BF16)
    dk, dv = pl.pallas_call(
        body, name=f"attn_dkv_d{dil}", grid=(dil // merge, nch, NH // 2),
        in_specs=[pl.BlockSpec(memory_space=pltpu.SMEM), main, main, main, main, main_heads, main_heads,
                  nxt, nxt, nxt_heads, nxt_heads],
        out_specs=[main, main], out_shape=[sd, sd],
        scratch_shapes=[pltpu.VMEM((rows + CHUNK, LANES), BF16), pltpu.VMEM((rows + CHUNK, LANES), BF16),
                        pltpu.VMEM((8, rows + CHUNK), F32), pltpu.VMEM((8, rows + CHUNK), F32),
                        pltpu.VMEM((CHUNK, 4 * CHUNK), F32)],
        compiler_params=_cparams("arbitrary", "arbitrary", "arbitrary"),
    )(slopes, k, v, q, do, lse, delta, q, do, lse, delta)
    return dk.reshape(dil, length, A), dv.reshape(dil, length, A)


def _group_masks(width):
    lane = lax.broadcasted_iota(jnp.int32, (1, width), 1)
    return [(lane >= g * DH) & (lane < (g + 1) * DH) for g in range(width // DH)]


def _group_mean_matrix():
    i = lax.broadcasted_iota(jnp.int32, (GW, GW), 0) // DH
    j = lax.broadcasted_iota(jnp.int32, (GW, GW), 1) // DH
    return jnp.where(i == j, 1.0 / DH, 0.0).astype(F32)


def _tri_mask(lower):
    t = lax.broadcasted_iota(jnp.int32, (CHUNK, CHUNK), 0)
    u = lax.broadcasted_iota(jnp.int32, (CHUNK, CHUNK), 1)
    return (u <= t) if lower else (u >= t)


def _sgu_forward(u, z, lng, lnb, w_ref, bias_t, pmat, rows):
    ug = _gelu(u)
    zg = _gelu(z)
    mu = _dot_hi(zg, pmat)
    zc = zg - mu
    var = _dot_hi(zc * zc, pmat)
    rstd = lax.rsqrt(var + EPS)
    zhat = zc * rstd
    zn = (zhat * lng + lnb).astype(BF16)
    gm = _group_masks(GW)
    tri = _tri_mask(True)
    ws = [jnp.where(tri, w_ref[g], 0.0).astype(BF16) for g in range(NG)]
    pieces = []
    for c in range(rows // CHUNK):
        znc = zn[c * CHUNK:(c + 1) * CHUNK, :]
        mix = None
        for g in range(NG):
            part = jnp.where(gm[g], _dot(ws[g], znc), 0.0)
            mix = part if mix is None else mix + part
        pieces.append(mix + bias_t)
    mixed = jnp.concatenate(pieces, axis=0) if len(pieces) > 1 else pieces[0]
    return ug * mixed, ug, zhat, rstd, zn, mixed


def _head_spread():
    h = lax.broadcasted_iota(jnp.int32, (LANES, A), 0)
    lane = lax.broadcasted_iota(jnp.int32, (LANES, A), 1)
    return jnp.where(lane // DH == h, 1.0, 0.0).astype(BF16)


def _bf16_pieces(t, n):
    pieces = []
    for _ in range(n):
        piece = t.astype(BF16)
        pieces.append(piece)
        t = t - piece.astype(F32)
    return pieces


def _mix_fwd(os_, ls_, u, z, x, lng, lnb, sgu_w, bias_t, ga, gg, wout):
    s = x.shape[0]
    nd = len(DILS)
    nscr = sum(1 for d in DILS if d > 1)

    def body(*refs):
        o_refs, l_refs = refs[:nd], refs[nd:2 * nd]
        u_ref, z_ref, x_ref, lng_ref, lnb_ref, w_ref, bt_ref, ga_ref, gg_ref, wo_ref = refs[2 * nd:2 * nd + 10]
        attn_ref = refs[2 * nd + 10]
        lse_refs = refs[2 * nd + 11:3 * nd + 11]
        mixed_ref, h1_ref = refs[3 * nd + 11:3 * nd + 13]
        scr = refs[3 * nd + 13:]
        scr_o, scr_l, scr_lse = scr[:nscr], scr[nscr:2 * nscr], scr[2 * nscr]
        ov, lv, j = [], [], 0
        for di, dil in enumerate(DILS):
            if dil == 1:
                ov.append(o_refs[di][0].astype(F32))
                lv.append(l_refs[di][0])
            else:
                ov.append(_merge_residues(o_refs[di], scr_o[j], dil))
                lv.append(_merge_residues(l_refs[di], scr_l[j], dil))
                j += 1
        mx = functools.reduce(jnp.maximum, lv)
        es = [jnp.exp2(l - mx) for l in lv]
        den = functools.reduce(lambda a, b: a + b, es)
        spread = _head_spread()
        attn = None
        for e, o in zip(es, ov):
            wide = functools.reduce(lambda a, b: a + b, [_dot(piece, spread) for piece in _bf16_pieces(e / den, 2)])
            attn = wide * o if attn is None else attn + wide * o
        attn_ref[...] = attn
        lse = mx + jnp.log2(den)
        _fill_cols(scr_lse, lse)
        for di, dil in enumerate(DILS):
            if dil == 1:
                lse_refs[di][0] = lse
            else:
                _split_residues(scr_lse, lse_refs[di], dil)
        an, _, _ = _rms_fwd(attn, ga_ref[...])
        gmv, _, _, _, _, _ = _sgu_forward(u_ref[...], z_ref[...], lng_ref[...], lnb_ref[...], w_ref,
                                          bt_ref[...], _group_mean_matrix(), TMX)
        gn, _, _ = _rms_fwd(gmv, gg_ref[...])
        mixed = jnp.concatenate([an, gn], axis=-1).astype(BF16)
        mixed_ref[...] = mixed
        h1_ref[...] = x_ref[...] + _dot(mixed, wo_ref[...])

    sd = jax.ShapeDtypeStruct
    res = pl.pallas_call(
        body, name="mix_fwd", grid=(s // TMX,),
        in_specs=[_res_spec(d, TMX, A) for d in DILS] + [_res_spec(d, TMX, LANES) for d in DILS]
                 + [_row_spec(TMX, GW), _row_spec(TMX, GW),
                    _row_spec(TMX, D), _const_spec((1, GW)), _const_spec((1, GW)), _const_spec((NG, CHUNK, CHUNK)),
                    _const_spec((CHUNK, GW)), _const_spec((1, A)), _const_spec((1, GW)), _const_spec((D, D))],
        out_specs=[_row_spec(TMX, A)] + [_res_spec(d, TMX, LANES) for d in DILS]
                  + [_row_spec(TMX, D), _row_spec(TMX, D)],
        out_shape=[sd((s, A), F32)] + [_res_shape(s, d, LANES, F32) for d in DILS]
                  + [sd((s, D), BF16), sd((s, D), F32)],
        scratch_shapes=[_col_scratch(TMX, A)] * nscr + [_col_scratch(TMX, LANES)] * (nscr + 1),
        compiler_params=_cparams("arbitrary"),
    )(*os_, *ls_, u, z, x, lng, lnb, sgu_w, bias_t, ga, gg, wout)
    return res[0], res[1:1 + nd], res[1 + nd], res[2 + nd]


def _mlp_fwd(h1, g2, wff1, wff2, gf, target):
    s = h1.shape[0]

    def body(h1_ref, g2_ref, w1_ref, w2_ref, gf_ref, t_ref, hn_ref, rf_ref, dh2_ref, loss_ref, dgf_ref):
        i = pl.program_id(0)
        h1v = h1_ref[...]
        hn, _, _ = _rms_fwd(h1v, g2_ref[...])
        hn = hn.astype(BF16)
        hn_ref[...] = hn
        acc = h1v
        for j in range(DFF // FF_CH):
            cols = slice(j * FF_CH, (j + 1) * FF_CH)
            rf = jnp.maximum(_dot(hn, w1_ref[j]), 0.0)
            act = (rf * rf).astype(BF16)
            rf_ref[:, cols] = rf.astype(BF16)
            acc = acc + _dot(act, w2_ref[cols, :])
        y, h2n, r3 = _rms_fwd(acc, gf_ref[...])
        err = y - t_ref[...]
        part = 0.5 * jnp.sum(jnp.mean(err * err, axis=-1, keepdims=True), axis=0, keepdims=True)
        dy = err * (1.0 / D)
        dh2, dgf = _rms_bwd(dy, h2n, r3, gf_ref[...])
        dh2_ref[...] = dh2

        @pl.when(i == 0)
        def _():
            loss_ref[...] = jnp.zeros_like(loss_ref)
            dgf_ref[...] = jnp.zeros_like(dgf_ref)

        loss_ref[...] += jnp.broadcast_to(part, loss_ref.shape)
        dgf_ref[...] += dgf

    sd = jax.ShapeDtypeStruct
    return pl.pallas_call(
        body, name="mlp_fwd", grid=(s // TM,),
        in_specs=[_row_spec(TM, D), _const_spec((1, D)), _const_spec((DFF // FF_CH, D, FF_CH)), _const_spec((DFF, D)),
                  _const_spec((1, D)), _row_spec(TM, D)],
        out_specs=[_row_spec(TM, D), _row_spec(TM, DFF), _row_spec(TM, D),
                   _const_spec((1, LANES)), _const_spec((1, D))],
        out_shape=[sd((s, D), BF16), sd((s, DFF), BF16), sd((s, D), F32),
                   sd((1, LANES), F32), sd((1, D), F32)],
        compiler_params=_cparams("arbitrary"),
    )(h1, g2, wff1, wff2, gf, target)


def _mlp_bwd(dh2, rf, h1, g2, wff1, wff2):
    s = h1.shape[0]

    def body(dh2_ref, rf_ref, h1_ref, g2_ref, w1_ref, w2_ref, df_ref, dh1_ref, dg2_ref):
        i = pl.program_id(0)
        dh2v = dh2_ref[...]
        dh2b = dh2v.astype(BF16)
        dhn = jnp.zeros((TM, D), F32)
        for j in range(DFF // FF_CH):
            cols = slice(j * FF_CH, (j + 1) * FF_CH)
            da = _dot_nt(dh2b, w2_ref[cols, :])
            df = (da * (2.0 * rf_ref[:, cols].astype(F32))).astype(BF16)
            df_ref[:, cols] = df
            dhn = dhn + _dot_nt(df, w1_ref[j])
        _, h1n, r2 = _rms_fwd(h1_ref[...], g2_ref[...])
        dres, dg2 = _rms_bwd(dhn, h1n, r2, g2_ref[...])
        dh1_ref[...] = dh2v + dres

        @pl.when(i == 0)
        def _():
            dg2_ref[...] = jnp.zeros_like(dg2_ref)

        dg2_ref[...] += dg2

    sd = jax.ShapeDtypeStruct
    return pl.pallas_call(
        body, name="mlp_bwd", grid=(s // TM,),
        in_specs=[_row_spec(TM, D), _row_spec(TM, DFF), _row_spec(TM, D), _const_spec((1, D)),
                  _const_spec((DFF // FF_CH, D, FF_CH)), _const_spec((DFF, D))],
        out_specs=[_row_spec(TM, DFF), _row_spec(TM, D), _const_spec((1, D))],
        out_shape=[sd((s, DFF), BF16), sd((s, D), F32), sd((1, D), F32)],
        compiler_params=_cparams("arbitrary"),
    )(dh2, rf, h1, g2, wff1, wff2)


def _mix_bwd(dh1, attn, u, z, lng, lnb, sgu_w, sgu_wt, bias_t, ga, gg, wout):
    s = dh1.shape[0]
    nsteps = s // TMX
    nd = len(DILS)

    def body(*refs):
        dh1_ref, attn_ref, u_ref, z_ref, lng_ref, lnb_ref, w_ref, wt_ref, bt_ref, ga_ref, gg_ref, wo_ref = refs[:12]
        do_refs, dl_refs = refs[12:12 + nd], refs[12 + nd:12 + 2 * nd]
        (du_ref, dz_ref, dga_ref, dgg_ref, dlng_ref, dlnb_ref, dws_ref, db_ref,
         dbt_acc, scr_do, scr_dl) = refs[12 + 2 * nd:]
        i = pl.program_id(0)

        @pl.when(i == 0)
        def _():
            for r in (dga_ref, dgg_ref, dlng_ref, dlnb_ref, dws_ref, db_ref, dbt_acc):
                r[...] = jnp.zeros_like(r)

        dmixed = _dot_nt(dh1_ref[...].astype(BF16), wo_ref[...])
        attn = attn_ref[...]
        _, an, ra = _rms_fwd(attn, ga_ref[...])
        dattn, dga = _rms_bwd(dmixed[:, :A], an, ra, ga_ref[...])
        dga_ref[...] += dga
        _fill_cols(scr_do, dattn)
        spread = _head_spread()
        delta = functools.reduce(lambda a, b: a + b, [_dot_nt(piece, spread) for piece in _bf16_pieces(dattn * attn, 3)])
        _fill_cols(scr_dl, delta)
        for di, dil in enumerate(DILS):
            if dil == 1:
                do_refs[di][0] = dattn.astype(BF16)
                dl_refs[di][0] = delta
            else:
                _split_residues(scr_do, do_refs[di], dil)
                _split_residues(scr_dl, dl_refs[di], dil)
        pmat = _group_mean_matrix()
        lng = lng_ref[...]
        uv, zv = u_ref[...], z_ref[...]
        gmv, ug, zhat, rstd, zn, mixed = _sgu_forward(uv, zv, lng, lnb_ref[...], w_ref, bt_ref[...], pmat, TMX)
        _, gmn, rg = _rms_fwd(gmv, gg_ref[...])
        dgm, dgg = _rms_bwd(dmixed[:, A:], gmn, rg, gg_ref[...])
        dgg_ref[...] += dgg
        du_ref[...] = (dgm * mixed * _gelu_grad(uv)).astype(BF16)
        dmx = dgm * ug
        dmxb = dmx.astype(BF16)
        gm = _group_masks(GW)
        tri_t = _tri_mask(False)
        wst = [jnp.where(tri_t, wt_ref[g], 0.0).astype(BF16) for g in range(NG)]
        zero = jnp.zeros((CHUNK, GW), BF16)
        dzn_pieces = []
        for c in range(TMX // CHUNK):
            rs = slice(c * CHUNK, (c + 1) * CHUNK)
            dmc = dmxb[rs, :]
            znc = zn[rs, :]
            dbt_acc[...] += dmx[rs, :]
            dzn = None
            for g in range(NG):
                dws_ref[g] += _dot_nt(jnp.where(gm[g], dmc, zero), znc)
                part = jnp.where(gm[g], _dot(wst[g], dmc), 0.0)
                dzn = part if dzn is None else dzn + part
            dzn_pieces.append(dzn)
        dzn = jnp.concatenate(dzn_pieces, axis=0)
        dlng_ref[...] += jnp.sum(dzn * zhat, axis=0, keepdims=True)
        dlnb_ref[...] += jnp.sum(dzn, axis=0, keepdims=True)
        dzh = dzn * lng
        dzg = rstd * (dzh - _dot_hi(dzh, pmat) - zhat * _dot_hi(dzh * zhat, pmat))
        dz_ref[...] = (dzg * _gelu_grad(zv)).astype(BF16)

        @pl.when(i == nsteps - 1)
        def _():
            tri = _tri_mask(True)
            for g in range(NG):
                dws_ref[g] = jnp.where(tri, dws_ref[g], 0.0)
            acc = dbt_acc[...]
            lane = lax.broadcasted_iota(jnp.int32, (CHUNK, LANES), 1)
            out = jnp.zeros((CHUNK, LANES), F32)
            for g in range(NG):
                sg = jnp.sum(jnp.where(gm[g], acc, 0.0), axis=-1, keepdims=True)
                out = jnp.where(lane == g, sg, out)
            db_ref[...] = out

    sd = jax.ShapeDtypeStruct
    res = pl.pallas_call(
        body, name="mix_bwd", grid=(nsteps,),
        in_specs=[_row_spec(TMX, D), _row_spec(TMX, A), _row_spec(TMX, GW), _row_spec(TMX, GW),
                  _const_spec((1, GW)), _const_spec((1, GW)), _const_spec((NG, CHUNK, CHUNK)),
                  _const_spec((NG, CHUNK, CHUNK)), _const_spec((CHUNK, GW)), _const_spec((1, A)),
                  _const_spec((1, GW)), _const_spec((D, D))],
        out_specs=[_res_spec(d, TMX, A) for d in DILS] + [_res_spec(d, TMX, LANES) for d in DILS]
                  + [_row_spec(TMX, GW), _row_spec(TMX, GW),
                   _const_spec((1, A)), _const_spec((1, GW)), _const_spec((1, GW)), _const_spec((1, GW)),
                   _const_spec((NG, CHUNK, CHUNK)), _const_spec((CHUNK, LANES))],
        out_shape=[_res_shape(s, d, A, BF16) for d in DILS] + [_res_shape(s, d, LANES, F32) for d in DILS]
                  + [sd((s, GW), BF16), sd((s, GW), BF16),
                   sd((1, A), F32), sd((1, GW), F32), sd((1, GW), F32), sd((1, GW), F32),
                   sd((NG, CHUNK, CHUNK), F32), sd((CHUNK, LANES), F32)],
        scratch_shapes=[pltpu.VMEM((CHUNK, GW), F32), _col_scratch(TMX, A), _col_scratch(TMX, LANES)],
        compiler_params=_cparams("arbitrary"),
    )(dh1, attn, u, z, lng, lnb, sgu_w, sgu_wt, bias_t, ga, gg, wout)
    return (res[:nd], res[nd:2 * nd]) + tuple(res[2 * nd:])


def _dproj_merge(dqs, dks, dvs, du, dz, pin):
    s = du.shape[0]
    nd = len(DILS)
    nscr = sum(1 for d in DILS if d > 1)

    def body(*refs):
        pin_ref = refs[0]
        parts = [refs[1 + t * nd:1 + (t + 1) * nd] for t in range(3)]
        du_ref, dz_ref, dp_ref = refs[1 + 3 * nd:4 + 3 * nd]
        scr = refs[4 + 3 * nd:]
        sums = []
        for t in range(3):
            total, j = None, 0
            for di, dil in enumerate(DILS):
                if dil == 1:
                    term = parts[t][di][0].astype(F32)
                else:
                    term = _merge_residues(parts[t][di], scr[t * nscr + j], dil)
                    j += 1
                total = term if total is None else total + term
            sums.append(total)
        dp_ref[...] = jnp.concatenate([sums[0] * SCALE, sums[1], sums[2], du_ref[...].astype(F32) + pin_ref[0, 0],
                                       dz_ref[...].astype(F32)], axis=-1).astype(BF16)

    return pl.pallas_call(
        body, name="dproj_merge", grid=(s // TMX,),
        in_specs=[pl.BlockSpec(memory_space=pltpu.SMEM)] + [_res_spec(d, TMX, A) for d in DILS] * 3
                 + [_row_spec(TMX, GW)] * 2,
        out_specs=_row_spec(TMX, INW), out_shape=jax.ShapeDtypeStruct((s, INW), BF16),
        scratch_shapes=[_col_scratch(TMX, A)] * (3 * nscr),
        compiler_params=_cparams("arbitrary"),
    )(pin, *dqs, *dks, *dvs, du, dz)


def _inproj_bwd(dproj, dh1, x, g1, win_t):
    s = x.shape[0]

    def body(dp_ref, dh1_ref, x_ref, g_ref, w_ref, dx_ref, dg_ref):
        i = pl.program_id(0)
        dhn = _dot(dp_ref[...], w_ref[...])
        _, xn, r1 = _rms_fwd(x_ref[...], g_ref[...])
        dres, dg = _rms_bwd(dhn, xn, r1, g_ref[...])
        dx_ref[...] = dh1_ref[...] + dres

        @pl.when(i == 0)
        def _():
            dg_ref[...] = jnp.zeros_like(dg_ref)

        dg_ref[...] += dg

    sd = jax.ShapeDtypeStruct
    return pl.pallas_call(
        body, name="inproj_bwd", grid=(s // TM_BIG,),
        in_specs=[_row_spec(TM_BIG, INW), _row_spec(TM_BIG, D), _row_spec(TM_BIG, D), _const_spec((1, D)), _const_spec((INW, D))],
        out_specs=[_row_spec(TM_BIG, D), _const_spec((1, D))],
        out_shape=[sd((s, D), F32), sd((1, D), F32)],
        compiler_params=_cparams("arbitrary"),
    )(dproj, dh1, x, g1, win_t)


def _wgrad(a, b, name, bm, bn, bk=4 * TM, square_a=False, also_bf16=False):
    s, m = a.shape
    n = b.shape[1]
    bm, bn = min(bm, m), min(bn, n)
    nk = s // bk

    def body(a_ref, b_ref, o_ref, *low):
        @pl.when(pl.program_id(2) == 0)
        def _():
            o_ref[...] = jnp.zeros_like(o_ref)

        av = a_ref[...]
        if square_a:
            av = av.astype(F32)
            av = av * av
        o_ref[...] += _dot_tn(av.astype(BF16), b_ref[...].astype(BF16))
        if also_bf16:
            @pl.when(pl.program_id(2) == nk - 1)
            def _():
                low[0][...] = o_ref[...].astype(BF16)

    out_spec = pl.BlockSpec((bm, bn), lambda i, j, k: (i, j))
    res = pl.pallas_call(
        body, name=name, grid=(m // bm, n // bn, nk),
        in_specs=[pl.BlockSpec((bk, bm), lambda i, j, k: (k, i)), pl.BlockSpec((bk, bn), lambda i, j, k: (k, j))],
        out_specs=[out_spec, out_spec] if also_bf16 else out_spec,
        out_shape=([jax.ShapeDtypeStruct((m, n), F32), jax.ShapeDtypeStruct((m, n), BF16)] if also_bf16
                   else jax.ShapeDtypeStruct((m, n), F32)),
        compiler_params=_cparams("arbitrary", "arbitrary", "arbitrary"),
    )(a, b)
    return res


def _adamw_math(w, g, m, v):
    m = B1 * m + (1.0 - B1) * g
    v = B2 * v + (1.0 - B2) * (g * g)
    m_hat = m / (1.0 - B1 ** STEP)
    v_hat = v / (1.0 - B2 ** STEP)
    delta = -LR * (m_hat / (jnp.sqrt(v_hat) + AEPS) + WD * w)
    return delta, m, v


def _adamw(w, g, m, v, name):
    rows, cols = w.shape
    br = min(rows, 256)
    while rows % br:
        br -= 8

    def body(w_ref, g_ref, m_ref, v_ref, d_ref, mo_ref, vo_ref):
        d, mn, vn = _adamw_math(w_ref[...], g_ref[...], m_ref[...], v_ref[...])
        d_ref[...] = d
        mo_ref[...] = mn
        vo_ref[...] = vn

    spec = _row_spec(br, cols)
    sd = jax.ShapeDtypeStruct((rows, cols), F32)
    return pl.pallas_call(
        body, name=name, grid=(rows // br,), in_specs=[spec] * 4, out_specs=[spec] * 3,
        out_shape=[sd, sd, sd], compiler_params=_cparams("arbitrary"),
    )(w, g, m, v)


def _local_step(x, hn1, target, small, win_t, rest_weights, early_grads=None, after_attention_bwd=None,
                late_grads=None):
    slopes = jnp.asarray(_alibi_slopes(NH) * np.float32(LOG2E))
    q, k, v, u, z = _inproj_fwd(hn1, win_t)
    outs, lses = [], []
    for i, dil in enumerate(DILS):
        o, l = _attn_fwd(q[i], k[i], v[i], slopes, dil)
        outs.append(o)
        lses.append(l)
    wout, wff1, wff2 = rest_weights(functools.reduce(lambda a, b: a + b, [l[0, 0:8, :] for l in lses]))
    attn, lse, mixed, h1 = _mix_fwd(outs, lses, u, z, x, small["ln_g"], small["ln_b"], small["sgu_w"],
                                    small["bias_t"], small["attn_out_g"], small["gmlp_out_g"], wout)
    hn2, rf, dh2, loss, dgf = _mlp_fwd(h1, small["norm2_g"], wff1, wff2, small["final_norm_g"], target)
    df, dh1, dg2 = _mlp_bwd(dh2, rf, h1, small["norm2_g"], wff1, wff2)
    gwff1 = _wgrad(hn2, df, "wgrad_ff1", D, 1024)
    gwff2 = _wgrad(rf, dh2, "wgrad_ff2", 1024, D, square_a=True)
    gwout = _wgrad(mixed, dh1, "wgrad_out", D, D)
    ga, g1 = small["attn_out_g"], small["norm1_g"]
    pin = early_grads(gwff1, gwff2, gwout) if early_grads else None
    if pin is not None:
        ga = ga + pin
    (do, delta, du, dz, dga, dgg, dlng, dlnb, dws, db) = _mix_bwd(
        dh1, attn, u, z, small["ln_g"], small["ln_b"], small["sgu_w"], small["sgu_wt"], small["bias_t"],
        ga, small["gmlp_out_g"], wout)
    dqs, dks, dvs = [], [], []
    for i, dil in enumerate(DILS):
        dqs.append(_attn_bwd_dq(q[i], k[i], v[i], do[i], lse[i], delta[i], slopes, dil))
        dk, dv = _attn_bwd_dkv(q[i], k[i], v[i], do[i], lse[i], delta[i], slopes, dil)
        dks.append(dk)
        dvs.append(dv)
    marker = functools.reduce(lambda a, b: a + b, [t[0, 0:8, 0:LANES] for t in dqs + dks + dvs])
    partial = dict(ln_g=dlng, ln_b=dlnb, sgu_w=dws, sgu_b=db[:, :NG].T, attn_out_g=dga, gmlp_out_g=dgg,
                   norm2_g=dg2, final_norm_g=dgf)
    pin = after_attention_bwd(marker, partial, loss[0, 0]) if after_attention_bwd else None
    dproj = _dproj_merge(dqs, dks, dvs, du, dz, jnp.zeros((1, 1), F32) if pin is None else pin)
    gwin_t, gwin_low = _wgrad(dproj, hn1, "wgrad_in", INW // 2, D, also_bf16=True)
    pin = late_grads(gwin_t, gwin_low) if late_grads else None
    if pin is not None:
        g1 = g1 + pin
    dx, dg1 = _inproj_bwd(dproj, dh1, x, g1, win_t)
    small_grads = dict(partial, norm1_g=dg1)
    return loss[0, 0], dx, small_grads, (gwin_t, gwout, gwff1, gwff2)


ANY = pl.BlockSpec(memory_space=pl.ANY)
NDEV = 8


def _position():
    return lax.axis_index("x"), lax.axis_index("y"), lax.axis_index("c")


def _other_chips(x, y):
    return [(1 - x, y), (x, 1 - y), (1 - x, 1 - y)]


def _remote(src, dst, send_sem, recv_sem, device):
    return pltpu.make_async_remote_copy(src_ref=src, dst_ref=dst, send_sem=send_sem, recv_sem=recv_sem,
                                        device_id=device, device_id_type=MESH)


HBM = pl.BlockSpec(memory_space=pltpu.HBM)
SEM = pl.BlockSpec(memory_space=pltpu.SEMAPHORE)
DATAFLOW = pltpu.SideEffectType.DATAFLOW_SIDE_EFFECTING


def _in_hbm(a):
    return pltpu.with_memory_space_constraint(a, pltpu.HBM)


def _gather_start(shards, name):
    n = len(shards)
    lands = [jnp.broadcast_to(sh[None], (NCHIP,) + sh.shape) for sh in shards]

    def body(*refs):
        w_refs, land_refs = refs[:n], refs[n:2 * n]
        send_sems, recv_sems = refs[2 * n:2 * n + 2]
        token = refs[-1]
        x, y, c = _position()
        for w in range(n):
            for k, (px, py) in enumerate(_other_chips(x, y)):
                m = 3 * w + k
                _remote(w_refs[w], land_refs[w].at[2 * x + y], send_sems.at[m], recv_sems.at[m], (px, py, c)).start()
        token[...] = jnp.zeros_like(token)

    res = _split_call(body, name, list(shards) + lands, (3 * n, 3 * n), (TOKEN,))
    return res[0], res[1], res[2:2 + n], res[2 + n:2 + 2 * n], res[-1]


def _gather_wait(send_sems, recv_sems, shards, lands, after, name):
    n = len(shards)

    def body(*refs):
        w_refs, land_refs = refs[:n], refs[n:2 * n]
        send_sems, recv_sems = refs[2 * n:2 * n + 2]
        x, y, c = _position()
        for w in range(n):
            for k, (px, py) in enumerate(_other_chips(x, y)):
                m = 3 * w + k
                cp = _remote(w_refs[w], land_refs[w].at[2 * px + py], send_sems.at[m], recv_sems.at[m], (px, py, c))
                cp.wait_send()
                cp.wait_recv()

    operands = list(shards) + list(lands)
    res = pl.pallas_call(
        body, name=name, out_shape=tuple(pltpu.HBM(a.shape, a.dtype) for a in operands),
        in_specs=(HBM,) * (2 * n) + (SEM, SEM, ANY), out_specs=(HBM,) * (2 * n),
        input_output_aliases={i: i for i in range(2 * n)},
        compiler_params=pltpu.CompilerParams(has_side_effects=DATAFLOW),
    )(*operands, send_sems, recv_sems, after)
    return res[n:]


def _xor_peers(x, y, c):
    peers = []
    for k in range(1, NDEV):
        kx, ky, kc = (k >> 2) & 1, (k >> 1) & 1, k & 1
        peers.append((1 - x if kx else x, 1 - y if ky else y, 1 - c if kc else c))
    return peers


def _piece(part_ref, px, py, pc):
    slab = 2 * px + py
    if len(part_ref.shape) == 3:
        half = part_ref.shape[1] // 2
        return part_ref.at[slab, pl.ds(pc * half, half), :]
    half = part_ref.shape[0] // 2
    return part_ref.at[pl.ds(pc * half, half), pl.ds(pl.multiple_of(slab * D, D), D)]


def _split_call(body, name, operands, n_sems, extra_out=()):
    n = len(operands)
    sems = tuple(pltpu.SemaphoreType.DMA((m,)) for m in n_sems)
    thru = tuple(pltpu.HBM(a.shape, a.dtype) for a in operands)
    return pl.pallas_call(
        body, name=name, out_shape=sems + thru + tuple(extra_out),
        in_specs=(HBM,) * n,
        out_specs=(SEM,) * len(sems) + (HBM,) * n + (pl.BlockSpec(memory_space=pltpu.VMEM),) * len(extra_out),
        input_output_aliases={i: len(sems) + i for i in range(n)},
        compiler_params=pltpu.CompilerParams(has_side_effects=DATAFLOW),
    )(*[_in_hbm(a) for a in operands])


TOKEN = jax.ShapeDtypeStruct((8, LANES), F32)


def _pack_copies(pack_ref, land_ref, send_sems, recv_sems, base, position, start):
    x, y, c = position
    for k, (px, py, pc) in enumerate(_xor_peers(x, y, c)):
        if start:
            _remote(pack_ref, land_ref.at[4 * x + 2 * y + c], send_sems.at[base + k], recv_sems.at[base + k],
                    (px, py, pc)).start()
        else:
            cp = _remote(pack_ref, land_ref.at[4 * px + 2 * py + pc], send_sems.at[base + k], recv_sems.at[base + k],
                         (px, py, pc))
            cp.wait_send()
            cp.wait_recv()


def _pack_landing(pack):
    return jnp.broadcast_to(pack[None], (NDEV,) + pack.shape)


def _reduce_start(parts, name, pack=None):
    nw = len(parts)
    lands = [lax.empty((NDEV - 1, p.shape[-2] // 2, D), p.dtype) for p in parts]
    operands = list(parts) + lands + ([pack, _pack_landing(pack)] if pack is not None else [])
    nops = len(operands)

    def body(*refs):
        part_refs, land_refs = refs[:nw], refs[nw:2 * nw]
        send_sems, recv_sems = refs[nops:nops + 2]
        token = refs[-1]
        x, y, c = _position()
        for w in range(nw):
            for k, peer in enumerate(_xor_peers(x, y, c)):
                n = w * (NDEV - 1) + k
                _remote(_piece(part_refs[w], *peer), land_refs[w].at[k], send_sems.at[n], recv_sems.at[n],
                        peer).start()
        if pack is not None:
            _pack_copies(refs[2 * nw], refs[2 * nw + 1], send_sems, recv_sems, nw * (NDEV - 1), (x, y, c), True)
        token[...] = jnp.zeros_like(token)

    n = (nw + (pack is not None)) * (NDEV - 1)
    res = _split_call(body, name, operands, (n, n), (TOKEN,))
    return res[0], res[1], res[2:2 + nops], res[-1]


def _reduce_wait(send_sems, recv_sems, operands, nw, after, name):
    nops = len(operands)
    has_pack = nops > 2 * nw

    def body(*refs):
        part_refs, land_refs = refs[:nw], refs[nw:2 * nw]
        send_sems, recv_sems = refs[nops:nops + 2]
        x, y, c = _position()
        for w in range(nw):
            for k, peer in enumerate(_xor_peers(x, y, c)):
                n = w * (NDEV - 1) + k
                cp = _remote(_piece(part_refs[w], *peer), land_refs[w].at[k], send_sems.at[n], recv_sems.at[n], peer)
                cp.wait_send()
                cp.wait_recv()
        if has_pack:
            _pack_copies(refs[2 * nw], refs[2 * nw + 1], send_sems, recv_sems, nw * (NDEV - 1), (x, y, c), False)

    res = pl.pallas_call(
        body, name=name, out_shape=tuple(pltpu.HBM(a.shape, a.dtype) for a in operands),
        in_specs=(HBM,) * nops + (SEM, SEM, ANY), out_specs=(HBM,) * nops,
        input_output_aliases={i: i for i in range(nops)},
        compiler_params=pltpu.CompilerParams(has_side_effects=DATAFLOW),
    )(*operands, send_sems, recv_sems, after)
    return res[:nw], res[nw:2 * nw], (res[2 * nw + 1] if has_pack else None)


def _sum_pieces(part, land, sel, name):
    half = part.shape[-2] // 2
    br = 128 if half % 128 == 0 else half // 2
    nb = half // br

    def body(sel_ref, own_ref, *refs):
        acc = own_ref[...]
        for r in refs[:NDEV - 1]:
            acc = acc + r[...].astype(F32)
        refs[NDEV - 1][...] = acc

    if part.ndim == 3:
        own_spec = pl.BlockSpec((None, br, D), lambda i, sel_ref: (sel_ref[0], sel_ref[1] * nb + i, 0))
    else:
        own_spec = pl.BlockSpec((br, D), lambda i, sel_ref: (sel_ref[1] * nb + i, sel_ref[0]))
    slot_specs = [pl.BlockSpec((None, br, D), functools.partial(lambda i, sel_ref, k: (k, i, 0), k=k))
                  for k in range(NDEV - 1)]
    return pl.pallas_call(
        body, name=name,
        grid_spec=pltpu.PrefetchScalarGridSpec(
            num_scalar_prefetch=1, grid=(nb,), in_specs=[own_spec] + slot_specs,
            out_specs=pl.BlockSpec((br, D), lambda i, sel_ref: (i, 0))),
        out_shape=jax.ShapeDtypeStruct((half, D), F32),
        compiler_params=_cparams("arbitrary"),
    )(sel, part, *([land] * (NDEV - 1)))


def _share_start(halves, name, pack=None):
    nw = len(halves)
    lands = [lax.empty(h.shape, F32) for h in halves]
    operands = list(halves) + lands + ([pack, _pack_landing(pack)] if pack is not None else [])
    nops = len(operands)

    def body(*refs):
        h_refs, land_refs = refs[:nw], refs[nw:2 * nw]
        send_sems, recv_sems = refs[nops:nops + 2]
        token = refs[-1]
        x, y, c = _position()
        for w in range(nw):
            _remote(h_refs[w], land_refs[w], send_sems.at[w], recv_sems.at[w], (x, y, 1 - c)).start()
        if pack is not None:
            _pack_copies(refs[2 * nw], refs[2 * nw + 1], send_sems, recv_sems, nw, (x, y, c), True)
        token[...] = jnp.zeros_like(token)

    n = nw + (NDEV - 1 if pack is not None else 0)
    res = _split_call(body, name, operands, (n, n), (TOKEN,))
    return res[0], res[1], res[2:2 + nops], res[-1]


def _share_wait(send_sems, recv_sems, operands, nw, after, name):
    nops = len(operands)
    has_pack = nops > 2 * nw

    def body(*refs):
        h_refs, land_refs = refs[:nw], refs[nw:2 * nw]
        send_sems, recv_sems = refs[nops:nops + 2]
        x, y, c = _position()
        for w in range(nw):
            cp = _remote(h_refs[w], land_refs[w], send_sems.at[w], recv_sems.at[w], (x, y, 1 - c))
            cp.wait_send()
            cp.wait_recv()
        if has_pack:
            _pack_copies(refs[2 * nw], refs[2 * nw + 1], send_sems, recv_sems, nw, (x, y, c), False)

    res = pl.pallas_call(
        body, name=name, out_shape=tuple(pltpu.HBM(a.shape, a.dtype) for a in operands),
        in_specs=(HBM,) * nops + (SEM, SEM, ANY), out_specs=(HBM,) * nops,
        input_output_aliases={i: i for i in range(nops)},
        compiler_params=pltpu.CompilerParams(has_side_effects=DATAFLOW),
    )(*operands, send_sems, recv_sems, after)
    return res[:nw], res[nw:2 * nw], (res[2 * nw + 1] if has_pack else None)


def _adamw_joined(w, own, other, sel, pin, m, v, name):
    rows, cols = w.shape
    half = rows // 2
    br = min(half, 256)
    while half % br:
        br -= 8
    nh = half // br

    def body(sel_ref, pin_ref, w_ref, own_ref, other_ref, m_ref, v_ref, g_ref, d_ref, mo_ref, vo_ref):
        mine = pl.program_id(0) // nh == sel_ref[1]
        g = jnp.where(mine, own_ref[...], other_ref[...]) + pin_ref[0, 0]
        g_ref[...] = g
        d, mn, vn = _adamw_math(w_ref[...], g, m_ref[...], v_ref[...])
        d_ref[...] = d
        mo_ref[...] = mn
        vo_ref[...] = vn

    own_spec = pl.BlockSpec((br, cols), lambda i, sel_ref: (jnp.clip(i - sel_ref[1] * nh, 0, nh - 1), 0))
    other_spec = pl.BlockSpec((br, cols), lambda i, sel_ref: (jnp.clip(i - (1 - sel_ref[1]) * nh, 0, nh - 1), 0))
    spec = pl.BlockSpec((br, cols), lambda i, sel_ref: (i, 0))
    sd = jax.ShapeDtypeStruct((rows, cols), F32)
    return pl.pallas_call(
        body, name=name,
        grid_spec=pltpu.PrefetchScalarGridSpec(
            num_scalar_prefetch=1, grid=(rows // br,),
            in_specs=[pl.BlockSpec(memory_space=pltpu.SMEM), spec, own_spec, other_spec, spec, spec],
            out_specs=[spec] * 4),
        out_shape=[sd] * 4, compiler_params=_cparams("arbitrary"),
    )(sel, pin, w, own, other, m, v)


def _join_halves(own, other, c):
    first = jnp.where(c == 0, own, other)
    second = jnp.where(c == 0, other, own)
    return jnp.concatenate([first, second], axis=0)


SMALL_SIZES = (("norm1_g", D), ("sgu_w", NG * CHUNK * CHUNK), ("norm2_g", D), ("final_norm_g", D),
               ("sgu_b", NG * CHUNK), ("attn_out_g", A), ("sgu_ln_g", GW), ("sgu_ln_b", GW), ("gmlp_out_g", GW))
PARAM_ROWS = sum(n for _, n in SMALL_SIZES) // LANES
SMALL_ROWS = PARAM_ROWS + 8


def _pack_small(tree, first_extra=None):
    extra = jnp.zeros((8 * LANES,), F32)
    if first_extra is not None:
        extra = extra.at[0].set(first_extra)
    flat = jnp.concatenate([tree[n].reshape(-1) for n, _ in SMALL_SIZES] + [extra])
    return flat.reshape(SMALL_ROWS, LANES)


def _written_shape(shape):
    core = tuple(shape)
    while len(core) > 2 and core[0] == 1:
        core = core[1:]
    if len(core) >= 2 and (core[-1] == LANES or (core[-1] % LANES == 0 and math.prod(core[:-1]) == 1)):
        return core
    if len(core) == 2 and core[-1] * 2 == LANES and core[0] % 2 == 0:
        return core
    return (math.prod(shape) // LANES, LANES)


def _store_small(out_ref, pack_ref, row, shape):
    if len(shape) == 3:
        for g in range(shape[0]):
            out_ref[g] = pack_ref[row + g * shape[1]:row + (g + 1) * shape[1], :]
    elif shape[-1] == LANES:
        out_ref[...] = pack_ref[row:row + shape[0], :]
    elif shape[-1] * 2 == LANES:
        base = row - row % 8
        assert row + shape[0] // 2 <= base + 8
        tile = pack_ref[base:base + 8, :]
        shifted = pltpu.roll(tile, LANES // 2, 1)
        for g in range(shape[0]):
            src = tile if g % 2 == 0 else shifted
            r = row - base + g // 2
            out_ref[g:g + 1, :] = src[r:r + 1, 0:LANES // 2]
    else:
        for j in range(shape[1] // LANES):
            out_ref[:, j * LANES:(j + 1) * LANES] = pack_ref[row + j:row + j + 1, :]


def _small_finish(pack_land, norm_land, wpack, mpack, vpack, shapes):
    written = [_written_shape(shapes[n]) for n, _ in SMALL_SIZES]
    nsmall = len(SMALL_SIZES)

    def body(*refs):
        p_ref, n_ref, w_ref, m_ref, v_ref, loss_ref = refs[:6]
        outs = refs[6:6 + 4 * nsmall]
        packs = refs[6 + 4 * nsmall:]
        go_ref = packs[0]
        total = p_ref[0]
        late = n_ref[0]
        for k in range(1, NDEV):
            total = total + p_ref[k]
            late = late + n_ref[k]
        go_ref[...] = total
        go_ref[0:8, :] = total[0:8, :] + late
        d, mn, vn = _adamw_math(w_ref[...], go_ref[...], m_ref[...], v_ref[...])
        packs[1][...] = d
        packs[2][...] = mn
        packs[3][...] = vn
        loss_ref[...] = go_ref[PARAM_ROWS:SMALL_ROWS, :]
        for kind in range(4):
            row = 0
            for j, (_, size) in enumerate(SMALL_SIZES):
                _store_small(outs[kind * nsmall + j], packs[kind], row, written[j])
                row += size // LANES

    vm = pl.BlockSpec(memory_space=pltpu.VMEM)
    out_shape = [jax.ShapeDtypeStruct((8, LANES), F32)] + [jax.ShapeDtypeStruct(w, F32) for w in written] * 4
    outs = pl.pallas_call(
        body, name="small_finish", in_specs=[vm] * 5, out_specs=[vm] * len(out_shape), out_shape=out_shape,
        scratch_shapes=[pltpu.VMEM((SMALL_ROWS, LANES), F32)] * 4,
        compiler_params=_cparams(),
    )(pack_land, norm_land, wpack, mpack, vpack)
    trees = [{n: outs[1 + kind * nsmall + j].reshape(shapes[n]) for j, (n, _) in enumerate(SMALL_SIZES)}
             for kind in range(4)]
    return outs[0], trees


def kernel(x, norm1_g, w_in, sgu_ln_g, sgu_ln_b, sgu_w, sgu_b, attn_out_g, gmlp_out_g, w_out, norm2_g, w_ff1, w_ff2, final_norm_g, loss_target, m_norm1_g, m_w_in, m_sgu_ln_g, m_sgu_ln_b, m_sgu_w, m_sgu_b, m_attn_out_g, m_gmlp_out_g, m_w_out, m_norm2_g, m_w_ff1, m_w_ff2, m_final_norm_g, v_norm1_g, v_w_in, v_sgu_ln_g, v_sgu_ln_b, v_sgu_w, v_sgu_b, v_attn_out_g, v_gmlp_out_g, v_w_out, v_norm2_g, v_w_ff1, v_w_ff2, v_final_norm_g):
    names = [n for n, _ in SMALL_SIZES]
    w_small = dict(norm1_g=norm1_g, sgu_ln_g=sgu_ln_g, sgu_ln_b=sgu_ln_b, sgu_w=sgu_w, sgu_b=sgu_b,
                   attn_out_g=attn_out_g, gmlp_out_g=gmlp_out_g, norm2_g=norm2_g, final_norm_g=final_norm_g)
    m_small = dict(norm1_g=m_norm1_g, sgu_ln_g=m_sgu_ln_g, sgu_ln_b=m_sgu_ln_b, sgu_w=m_sgu_w, sgu_b=m_sgu_b,
                   attn_out_g=m_attn_out_g, gmlp_out_g=m_gmlp_out_g, norm2_g=m_norm2_g,
                   final_norm_g=m_final_norm_g)
    v_small = dict(norm1_g=v_norm1_g, sgu_ln_g=v_sgu_ln_g, sgu_ln_b=v_sgu_ln_b, sgu_w=v_sgu_w, sgu_b=v_sgu_b,
                   attn_out_g=v_attn_out_g, gmlp_out_g=v_gmlp_out_g, norm2_g=v_norm2_g,
                   final_norm_g=v_final_norm_g)
    shapes = {n: w_small[n].shape for n in names}

    start_in = _gather_start([w_in[0].T.astype(BF16)], "gather_in_start")
    issued = start_in[4][0:1, 0:1]
    start_rest = _gather_start([(w_out[0] + issued).astype(BF16), w_ff1[0].astype(BF16), w_ff2[0].astype(BF16)],
                               "gather_rest_start")
    hn1 = _norm1(x[0], norm1_g + start_rest[4][0:1, 0:1])
    win_t = _gather_wait(*start_in[:4], after=hn1, name="gather_in_wait")[0].reshape(INW, D)

    def rest_weights(after):
        wout, wff1, wff2 = _gather_wait(*start_rest[:4], after=after, name="gather_rest_wait")
        return wout.reshape(D, D), wff1, wff2.reshape(DFF, D)

    small = dict(
        norm1_g=norm1_g, ln_g=sgu_ln_g.reshape(1, GW), ln_b=sgu_ln_b.reshape(1, GW), sgu_w=sgu_w[0],
        sgu_wt=jnp.swapaxes(sgu_w[0], 1, 2), bias_t=jnp.repeat(sgu_b[0].T, DH, axis=1),
        attn_out_g=attn_out_g, gmlp_out_g=gmlp_out_g, norm2_g=norm2_g, final_norm_g=final_norm_g.reshape(1, D))
    xi, yi, ci = _position()
    sel = jnp.stack([2 * xi + yi, ci]).astype(jnp.int32)
    state = {}

    def as_slabs(g):
        return g.reshape(NCHIP, g.shape[0] // NCHIP, D)

    def early_grads(gwff1, gwff2, gwout):
        state["early"] = _reduce_start([gwff1, as_slabs(gwff2), as_slabs(gwout)], "reduce_early_start")
        return state["early"][3][0:1, 0:1]

    def after_attention_bwd(marker, partial, loss_part):
        send_sems, recv_sems, operands, _ = state["early"]
        parts, lands, _ = _reduce_wait(send_sems, recv_sems, operands, 3, marker, "reduce_early_wait")
        halves = [_sum_pieces(p, l, sel, "sum_" + n) for p, l, n in zip(parts, lands, ("w_ff1", "w_ff2", "w_out"))]
        pack = _pack_small(dict(partial, norm1_g=jnp.zeros((1, D), F32), sgu_ln_g=partial["ln_g"],
                                sgu_ln_b=partial["ln_b"]), loss_part)
        state["early_share"] = _share_start(halves, "share_early_start", pack)
        return state["early_share"][3][0:1, 0:1]

    def late_grads(gwin_t, gwin_low):
        state["late"] = _reduce_start([as_slabs(gwin_low)], "reduce_late_start")
        state["late_own"] = as_slabs(gwin_t)
        return state["late"][3][0:1, 0:1]

    _, dx, sg, _ = _local_step(
        x[0], hn1, loss_target[0], small, win_t, rest_weights, early_grads, after_attention_bwd, late_grads)
    send_sems, recv_sems, operands, _ = state["early_share"]
    own, other, pack_land = _share_wait(send_sems, recv_sems, operands, 3, dx, "share_early_wait")
    send_sems, recv_sems, operands, _ = state["late"]
    _, late_lands, _ = _reduce_wait(send_sems, recv_sems, operands, 1, dx, "reduce_late_wait")
    late_share = _share_start([_sum_pieces(state["late_own"], late_lands[0], sel, "sum_w_in")], "share_late_start",
                              sg["norm1_g"].reshape(8, LANES))
    issued = late_share[3][0:1, 0:1]
    early_halves = dict(zip(("w_ff1", "w_ff2", "w_out"), zip(own, other)))
    g_big = {}
    w_big = dict(w_in=(w_in, m_w_in, v_w_in), w_out=(w_out, m_w_out, v_w_out),
                 w_ff1=(w_ff1, m_w_ff1, v_w_ff1), w_ff2=(w_ff2, m_w_ff2, v_w_ff2))
    grads, deltas, new_m, new_v = {}, {}, {}, {}

    def update(n):
        w, m, v = w_big[n]
        d, mn, vn = _adamw(w[0], g_big[n], m[0], v[0], "adamw_" + n)
        grads[n], deltas[n], new_m[n], new_v[n] = g_big[n][None], d[None], mn[None], vn[None]

    for n in ("w_ff1", "w_ff2", "w_out"):
        w, m, v = w_big[n]
        g, d, mn, vn = _adamw_joined(w[0], *early_halves[n], sel, issued, m[0], v[0], "adamw_" + n)
        grads[n], deltas[n], new_m[n], new_v[n] = g[None], d[None], mn[None], vn[None]
    updated = deltas["w_out"][0, 0:8, 0:LANES] + deltas["w_ff1"][0, 0:8, 0:LANES] + deltas["w_ff2"][0, 0:8, 0:LANES]
    own, other, norm_land = _share_wait(late_share[0], late_share[1], late_share[2], 1, updated, "share_late_wait")
    g_big["w_in"] = _join_halves(own[0], other[0], ci).T
    update("w_in")

    loss_tile, small_trees = _small_finish(pack_land, norm_land, _pack_small(w_small), _pack_small(m_small),
                                           _pack_small(v_small), shapes)
    loss = loss_tile[0, 0]
    for tree, small_tree in zip((grads, deltas, new_m, new_v), small_trees):
        tree.update(small_tree)

    order = ["norm1_g", "w_in", "sgu_ln_g", "sgu_ln_b", "sgu_w", "sgu_b", "attn_out_g", "gmlp_out_g", "w_out",
             "norm2_g", "w_ff1", "w_ff2", "final_norm_g"]
    return (loss, dx[None], *[grads[n] for n in order], *[deltas[n] for n in order],
            *[new_m[n] for n in order], *[new_v[n] for n in order])
```
